```python
import math
import jax, jax.numpy as jnp
from jax import lax
import numpy as np

D_MODEL = 1024
BATCH = 4
SEQ = 8192
DEPTH = 1

MOBA_HEADS = 8
MOBA_HEAD_DIM = 64
MOBA_WIDTH = MOBA_HEADS * MOBA_HEAD_DIM
MOBA_BLOCK = 256
MOBA_TOPK = 3
MOBA_QBLOCK = 64
REL_BUCKETS = 32
REL_MAX_DIST = 128
GLA_HEADS = 4
GLA_KEY_DIM = D_MODEL // 2
GLA_VALUE_DIM = D_MODEL
GLA_GATE_RANK = 16
GLA_GATE_NORMALIZER = 16.0
GLA_CHUNK = 64
N_EXPERTS = 32
TOP_K = 4
D_FF = D_MODEL
SWIGLU_ALPHA = 1.702
SWIGLU_LIMIT = 7.0
MOE_BLOCK = 256
N_BRANCHES = 2
EPS = 1e-6
IN_SIZES = (MOBA_WIDTH, MOBA_WIDTH, MOBA_WIDTH,
            GLA_KEY_DIM, GLA_KEY_DIM, GLA_VALUE_DIM, GLA_GATE_RANK, GLA_VALUE_DIM,
            N_BRANCHES * D_MODEL)
D_IN = sum(IN_SIZES)

kernel_name = 'hybrid_moba_gla_moe_block'


def rmsnorm(x, w):
    xf = x.astype(jnp.float32)
    xf = xf * lax.rsqrt(jnp.mean(xf * xf, axis=-1, keepdims=True) + EPS)
    return xf.astype(x.dtype) * w


def modulate(h, shift, scale):
    return h * (1 + scale[:, None, :]) + shift[:, None, :]


def t5_bucket(q_pos, k_pos):
    n = jnp.maximum(q_pos - k_pos, 0)
    max_exact = REL_BUCKETS // 2
    nf = jnp.maximum(n, max_exact).astype(jnp.float32)
    large = max_exact + (jnp.log(nf / max_exact) / math.log(REL_MAX_DIST / max_exact)
                         * (REL_BUCKETS - max_exact)).astype(jnp.int32)
    large = jnp.minimum(large, REL_BUCKETS - 1)
    return jnp.where(n < max_exact, n, large)


def moba_attention(q, k, v, rel_bias):
    B, H, S, hd = q.shape
    n_blk = -(-S // MOBA_BLOCK)
    s_pad = n_blk * MOBA_BLOCK
    pad = ((0, 0), (0, 0), (0, s_pad - S), (0, 0))
    q = jnp.pad(q * hd ** -0.5, pad)
    k = jnp.pad(k, pad)
    v = jnp.pad(v, pad)
    k_blocks = k.reshape(B, H, n_blk, MOBA_BLOCK, hd)
    v_blocks = v.reshape(B, H, n_blk, MOBA_BLOCK, hd)
    k_mean = jnp.mean(k_blocks, axis=3)
    q_blk = jnp.arange(s_pad) // MOBA_BLOCK
    gate = jnp.einsum('bhsd,bhnd->bhsn', q, k_mean).astype(jnp.float32)
    past = jnp.arange(n_blk)[None, :] < q_blk[:, None]
    gate = jnp.where(past, gate, -jnp.inf)
    n_sel = min(MOBA_TOPK, n_blk)
    _, sel = lax.top_k(gate, n_sel)
    b_idx = jnp.arange(B)[:, None, None, None]
    h_idx = jnp.arange(H)[None, :, None, None]
    bias_f = rel_bias.astype(jnp.float32)
    bias_hb = bias_f.T
    blk_off = jnp.arange(MOBA_BLOCK)

    def query_block(t0):
        q_c = lax.dynamic_slice_in_dim(q, t0, MOBA_QBLOCK, axis=2)
        sel_c = lax.dynamic_slice_in_dim(sel, t0, MOBA_QBLOCK, axis=2)
        q_pos = t0 + jnp.arange(MOBA_QBLOCK)
        own0 = (t0 // MOBA_BLOCK) * MOBA_BLOCK
        k_sel = k_blocks[b_idx, h_idx, sel_c]
        v_sel = v_blocks[b_idx, h_idx, sel_c]
        k_pos_sel = sel_c[..., None] * MOBA_BLOCK + blk_off
        bias_sel = bias_hb[h_idx[..., None], t5_bucket(q_pos[:, None, None], k_pos_sel)]
        valid_sel = jnp.arange(n_sel)[None, :] < (q_pos // MOBA_BLOCK)[:, None]
        logit_sel = jnp.einsum('bhqd,bhqnkd->bhqnk', q_c, k_sel).astype(jnp.float32) + bias_sel
        logit_sel = jnp.where(valid_sel[:, :, None], logit_sel, -jnp.inf)
        logit_sel = logit_sel.reshape(B, H, MOBA_QBLOCK, n_sel * MOBA_BLOCK)
        k_own = lax.dynamic_slice_in_dim(k, own0, MOBA_BLOCK, axis=2)
        v_own = lax.dynamic_slice_in_dim(v, own0, MOBA_BLOCK, axis=2)
        k_pos_own = own0 + blk_off
        bias_own = jnp.moveaxis(bias_f[t5_bucket(q_pos[:, None], k_pos_own[None, :])], -1, 0)
        logit_own = jnp.einsum('bhqd,bhkd->bhqk', q_c, k_own).astype(jnp.float32) + bias_own
        logit_own = jnp.where(k_pos_own[None, :] <= q_pos[:, None], logit_own, -jnp.inf)
        p = jax.nn.softmax(jnp.concatenate([logit_sel, logit_own], axis=-1), axis=-1).astype(v.dtype)
        p_sel = p[..., :n_sel * MOBA_BLOCK].reshape(B, H, MOBA_QBLOCK, n_sel, MOBA_BLOCK)
        p_own = p[..., n_sel * MOBA_BLOCK:]
        return (jnp.einsum('bhqnk,bhqnkd->bhqd', p_sel, v_sel)
                + jnp.einsum('bhqk,bhkd->bhqd', p_own, v_own))

    out = lax.map(query_block, jnp.arange(s_pad // MOBA_QBLOCK) * MOBA_QBLOCK)
    return jnp.moveaxis(out, 0, 2).reshape(B, H, s_pad, hd)[:, :, :S]


def gla_chunked(q, k, v, log_a):
    B, S, H, dk = q.shape
    dv = v.shape[-1]
    nc = S // GLA_CHUNK

    def chunks(t):
        return jnp.moveaxis(t.reshape(B, nc, GLA_CHUNK, H, t.shape[-1]), 3, 1)

    dt = v.dtype
    q = chunks(q) * dk ** -0.5
    k = chunks(k)
    v = chunks(v)
    b = jnp.cumsum(chunks(log_a), axis=3)
    b_last = b[:, :, :, -1:, :]
    q_g = q * jnp.exp(b).astype(dt)
    k_g = k * jnp.exp(-b).astype(dt)
    k_end = k * jnp.exp(b_last - b).astype(dt)
    causal = jnp.tril(jnp.ones((GLA_CHUNK, GLA_CHUNK), dtype=bool))
    att = jnp.where(causal, jnp.einsum('bhncd,bhnsd->bhncs', q_g, k_g), 0)
    o_intra = jnp.einsum('bhncs,bhnse->bhnce', att, v)
    decay = jnp.exp(b_last[:, :, :, 0, :]).astype(dt)

    def step(state, xs):
        q_n, k_n, v_n, d_n = xs
        o_n = jnp.einsum('bhcd,bhde->bhce', q_n, state)
        state = d_n[..., None] * state + jnp.einsum('bhcd,bhce->bhde', k_n, v_n)
        return state, o_n

    xs = (jnp.moveaxis(q_g, 2, 0), jnp.moveaxis(k_end, 2, 0), jnp.moveaxis(v, 2, 0), jnp.moveaxis(decay, 2, 0))
    _, o_inter = lax.scan(step, jnp.zeros((B, H, dk, dv), dt), xs)
    o = o_intra + jnp.moveaxis(o_inter, 0, 2)
    return jnp.moveaxis(o, 1, 3).reshape(B, S, H, dv)


def moe_ffn(h, w_router, b_router, w_gate, b_gate, w_up, b_up, w_down, b_down):
    B, S, D = h.shape
    T = B * S
    A = T * TOP_K
    hf = h.reshape(T, D)
    logits = (hf @ w_router + b_router).astype(jnp.float32)
    top_vals, top_idx = lax.top_k(logits, TOP_K)
    weights = jax.nn.softmax(top_vals, axis=-1).astype(h.dtype)
    expert_ids = top_idx.reshape(A)
    token_of = jnp.arange(A) // TOP_K
    order = jnp.argsort(expert_ids)
    sorted_e = expert_ids[order]
    counts = jnp.bincount(expert_ids, length=N_EXPERTS)
    padded = (counts + MOE_BLOCK - 1) // MOE_BLOCK * MOE_BLOCK
    pcum = jnp.cumsum(padded)
    start = jnp.cumsum(counts) - counts
    pstart = pcum - padded
    dest_sorted = pstart[sorted_e] + (jnp.arange(A) - start[sorted_e])
    dest = jnp.zeros((A,), jnp.int32).at[order].set(dest_sorted.astype(jnp.int32))
    n_blocks = (A + MOE_BLOCK - 1) // MOE_BLOCK + N_EXPERTS
    n_pad = n_blocks * MOE_BLOCK
    slot_token = jnp.zeros((n_pad,), jnp.int32).at[dest].set(token_of)
    x_pad = hf[slot_token].reshape(n_blocks, MOE_BLOCK, D)
    block_expert = jnp.minimum(jnp.searchsorted(pcum, jnp.arange(n_blocks) * MOE_BLOCK, side='right'),
                               N_EXPERTS - 1)

    def expert_block(args):
        xb, e = args
        g = jnp.minimum(xb @ w_gate[e] + b_gate[e], SWIGLU_LIMIT)
        u = jnp.clip(xb @ w_up[e] + b_up[e], -SWIGLU_LIMIT, SWIGLU_LIMIT)
        act = g * jax.nn.sigmoid(SWIGLU_ALPHA * g) * (u + 1)
        return act @ w_down[e] + b_down[e]

    y_pad = lax.map(expert_block, (x_pad, block_expert)).reshape(n_pad, D)
    y = y_pad[dest].reshape(T, TOP_K, D)
    return jnp.einsum('tkd,tk->td', y, weights).reshape(B, S, D)


def setup_inputs(seed: int = 0) -> dict:
    key = jax.random.key(seed)
    it = iter(jax.random.split(key, 24))
    L, D, E, F = DEPTH, D_MODEL, N_EXPERTS, D_FF

    def nrm(shape, fan_in, s=1.0):
        return jax.random.normal(next(it), shape, jnp.float32) * (s * fan_in ** -0.5)

    def gain(shape):
        return 1.0 + 0.05 * jax.random.normal(next(it), shape, jnp.float32)

    def small(shape, s=0.01):
        return s * jax.random.normal(next(it), shape, jnp.float32)

    return {
        'x': jax.random.normal(next(it), (BATCH, SEQ, D), jnp.float32),
        'c': jax.random.normal(next(it), (BATCH, D), jnp.float32),
        'rel_bias': small((REL_BUCKETS, MOBA_HEADS), 0.5),
        'w_ada': nrm((L, D, 6 * D), D, 0.5),
        'b_ada': small((L, 6 * D)),
        'norm_mix': gain((L, D)),
        'w_in': nrm((L, D, D_IN), D),
        'w_gk_up': nrm((L, GLA_GATE_RANK, GLA_KEY_DIM), GLA_GATE_RANK),
        'b_gk': small((L, GLA_KEY_DIM)),
        'gla_norm': gain((L, GLA_VALUE_DIM // GLA_HEADS)),
        'w_proj_moba': nrm((L, MOBA_WIDTH, D), MOBA_WIDTH),
        'w_proj_gla': nrm((L, GLA_VALUE_DIM, D), GLA_VALUE_DIM),
        'w_out': nrm((L, D, D), D),
        'norm_ffn': gain((L, D)),
        'w_router': nrm((L, D, E), D),
        'b_router': small((L, E)),
        'w_gate': nrm((L, E, D, F), D),
        'b_gate': small((L, E, F)),
        'w_up': nrm((L, E, D, F), D),
        'b_up': small((L, E, F)),
        'w_down': nrm((L, E, F, D), F),
        'b_down': small((L, E, D)),
        'norm_final': gain((D,)),
    }


def reference(x, c, rel_bias, w_ada, b_ada, norm_mix, w_in, w_gk_up, b_gk, gla_norm,
              w_proj_moba, w_proj_gla, w_out, norm_ffn, w_router, b_router,
              w_gate, b_gate, w_up, b_up, w_down, b_down, norm_final):
    B, S, D = x.shape
    split_at = np.cumsum(IN_SIZES)[:-1].tolist()

    def to_heads(t, n):
        return jnp.moveaxis(t.reshape(B, S, n, -1), 2, 1)

    def gla_heads(t):
        return t.reshape(B, S, GLA_HEADS, -1)

    for l in range(DEPTH):
        mod = jax.nn.silu(c) @ w_ada[l] + b_ada[l]
        sh1, sc1, g1, sh2, sc2, g2 = jnp.split(mod, 6, axis=-1)
        h = modulate(rmsnorm(x, norm_mix[l]), sh1, sc1)
        proj = h @ w_in[l]
        qa, ka, va, qb, kb, vb, gk_low, r, gate_logits = jnp.split(proj, split_at, axis=-1)
        y_a = moba_attention(to_heads(qa, MOBA_HEADS), to_heads(ka, MOBA_HEADS),
                             to_heads(va, MOBA_HEADS), rel_bias)
        y_a = jnp.moveaxis(y_a, 1, 2).reshape(B, S, MOBA_WIDTH)
        log_a = jax.nn.log_sigmoid((gk_low @ w_gk_up[l] + b_gk[l]).astype(jnp.float32)) / GLA_GATE_NORMALIZER
        o_b = gla_chunked(gla_heads(qb), gla_heads(kb), gla_heads(vb), gla_heads(log_a))
        y_b = (rmsnorm(o_b, gla_norm[l]) * jax.nn.silu(gla_heads(r))).reshape(B, S, GLA_VALUE_DIM)
        g_a, g_b = jnp.split(jax.nn.sigmoid(gate_logits), N_BRANCHES, axis=-1)
        mixed = g_a * (y_a @ w_proj_moba[l]) + g_b * (y_b @ w_proj_gla[l])
        x = x + g1[:, None, :] * (mixed @ w_out[l])
        h = modulate(rmsnorm(x, norm_ffn[l]), sh2, sc2)
        x = x + g2[:, None, :] * moe_ffn(h, w_router[l], b_router[l], w_gate[l], b_gate[l],
                                         w_up[l], b_up[l], w_down[l], b_down[l])
    return rmsnorm(x, norm_final)
```

```python
import functools
import math

import numpy as np
import jax
import jax.numpy as jnp
from jax import lax
from jax.experimental import pallas as pl
from jax.experimental.pallas import tpu as pltpu

F32 = jnp.float32
BF16 = jnp.bfloat16
I32 = jnp.int32
HIGHEST = lax.Precision.HIGHEST

D_MODEL = 1024
MOBA_HEADS = 8
MOBA_HEAD_DIM = 64
MOBA_WIDTH = MOBA_HEADS * MOBA_HEAD_DIM
MOBA_BLOCK = 256
MOBA_TOPK = 3
MOBA_MAX_BLOCKS = 32
REL_BUCKETS = 32
REL_MAX_DIST = 128
GLA_HEADS = 4
GLA_KEY_DIM = D_MODEL // 2
GLA_VALUE_DIM = D_MODEL
GLA_DK = GLA_KEY_DIM // GLA_HEADS
GLA_DV = GLA_VALUE_DIM // GLA_HEADS
GLA_GATE_RANK = 16
GLA_GATE_NORMALIZER = 16.0
GLA_CHUNK = 64
N_EXPERTS = 32
TOP_K = 4
D_FF = D_MODEL
SWIGLU_ALPHA = 1.702
SWIGLU_LIMIT = 7.0
MOE_BLOCK = 256
EPS = 1e-6
LANES = 128
NEG_BIG = -1e30
VMEM_LIMIT = 56 * 1024 * 1024


def _cparams(sem, vmem=None):
    return pltpu.CompilerParams(dimension_semantics=sem,
                                vmem_limit_bytes=vmem or VMEM_LIMIT)


def _nt_dot(a, b, **kw):
    return lax.dot_general(a, b, (((1,), (1,)), ((), ())),
                           preferred_element_type=F32, **kw)


def _rms(x):
    return x * lax.rsqrt(jnp.mean(x * x, axis=-1, keepdims=True) + EPS)


def _sigmoid(x):
    return 1.0 / (1.0 + jnp.exp(-x))


def _ada_kernel(c_ref, w_ref, b_ref, o_ref):
    c = c_ref[...]
    s = c * _sigmoid(c)
    o_ref[...] = jnp.dot(s, w_ref[...], precision=HIGHEST,
                         preferred_element_type=F32) + b_ref[...]


def _ada(c, w, b):
    bsz, d = c.shape
    n = w.shape[1]
    rows = -(-bsz // 8) * 8
    cp = jnp.zeros((rows, d), F32).at[:bsz].set(c)
    tn = 768
    out = pl.pallas_call(
        _ada_kernel,
        grid=(n // tn,),
        in_specs=[pl.BlockSpec((rows, d), lambda j: (0, 0)),
                  pl.BlockSpec((d, tn), lambda j: (0, j)),
                  pl.BlockSpec((1, tn), lambda j: (0, j))],
        out_specs=pl.BlockSpec((rows, tn), lambda j: (0, j)),
        out_shape=jax.ShapeDtypeStruct((rows, n), F32),
        compiler_params=_cparams(("arbitrary",)),
        name="ada",
    )(cp, w, b.reshape(1, n))
    return out[:bsz]


_W_QKVA = 3 * MOBA_WIDTH
_W_QKB = 2 * GLA_KEY_DIM
_OFF_QKB = _W_QKVA
_OFF_VB = _OFF_QKB + _W_QKB
_OFF_GK = _OFF_VB + GLA_VALUE_DIM
_OFF_R = _OFF_GK + LANES
_OFF_G = _OFF_R + GLA_VALUE_DIM
_W_CAT = _OFF_G + 2 * D_MODEL


def _inproj_kernel(x_ref, nw_ref, sc_ref, sh_ref, w_ref,
                   qkva_ref, qkb_ref, vb_ref, gk_ref, r_ref, g_ref):
    h = _rms(x_ref[...]) * nw_ref[...]
    h = h * (1.0 + sc_ref[0]) + sh_ref[0]
    hb = h.astype(BF16)

    def mm(a, b):
        return jnp.dot(hb, w_ref[:, a:b], preferred_element_type=F32)

    qscale = MOBA_HEAD_DIM ** -0.5
    qkva_ref[:, 0:MOBA_WIDTH] = (mm(0, MOBA_WIDTH) * qscale).astype(BF16)
    qkva_ref[:, MOBA_WIDTH:_W_QKVA] = mm(MOBA_WIDTH, _W_QKVA).astype(BF16)
    qkb_ref[...] = mm(_OFF_QKB, _OFF_VB).astype(BF16)
    vb_ref[...] = mm(_OFF_VB, _OFF_GK).astype(BF16)
    gk_ref[...] = mm(_OFF_GK, _OFF_R)
    r = mm(_OFF_R, _OFF_G)
    r_ref[...] = (r * _sigmoid(r)).astype(BF16)
    g_ref[...] = _sigmoid(mm(_OFF_G, _W_CAT)).astype(BF16)


def _inproj(x2d, nw, sc, sh, w_cat, seq, tm):
    t, d = x2d.shape
    tpb = seq // tm
    row = lambda w: pl.BlockSpec((tm, w), lambda i: (i, 0))
    per_b = pl.BlockSpec((1, 1, d), lambda i: (i // tpb, 0, 0))
    outs = [(_W_QKVA, BF16), (_W_QKB, BF16), (GLA_VALUE_DIM, BF16), (LANES, F32),
            (GLA_VALUE_DIM, BF16), (2 * D_MODEL, BF16)]
    return pl.pallas_call(
        _inproj_kernel,
        grid=(t // tm,),
        in_specs=[row(d), pl.BlockSpec((1, d), lambda i: (0, 0)), per_b, per_b,
                  pl.BlockSpec((d, _W_CAT), lambda i: (0, 0))],
        out_specs=[row(w) for w, _ in outs],
        out_shape=[jax.ShapeDtypeStruct((t, w), dt) for w, dt in outs],
        compiler_params=_cparams(("arbitrary",)),
        name="inproj",
    )(x2d, nw.reshape(1, d), sc, sh, w_cat)


def _t5_bucket_np(n):
    n = np.maximum(n, 0)
    max_exact = REL_BUCKETS // 2
    nf = np.maximum(n, max_exact).astype(np.float32)
    large = max_exact + (np.log(nf / max_exact) / math.log(REL_MAX_DIST / max_exact)
                         * (REL_BUCKETS - max_exact)).astype(np.int32)
    large = np.minimum(large, REL_BUCKETS - 1)
    return np.where(n < max_exact, n, large).astype(np.int32)


def _bucket_table():
    qi = np.arange(MOBA_BLOCK)[:, None]
    kj = np.arange(2 * MOBA_BLOCK)[None, :] - MOBA_BLOCK
    return _t5_bucket_np(qi - kj)


def _bias_kernel(rb_ref, bucket_ref, o_ref):
    h = pl.program_id(0)
    bk = bucket_ref[...]
    far = rb_ref[(REL_BUCKETS - 1) * MOBA_HEADS + h]
    acc = jnp.zeros(bk.shape, F32)
    for b in range(REL_BUCKETS):
        acc = jnp.where(bk == b, rb_ref[b * MOBA_HEADS + h] - far, acc)
    o_ref[0] = acc


def _bias_tiles(rel_bias):
    bucket = jnp.asarray(_bucket_table())
    return pl.pallas_call(
        _bias_kernel,
        grid=(MOBA_HEADS,),
        in_specs=[pl.BlockSpec(memory_space=pltpu.SMEM),
                  pl.BlockSpec(bucket.shape, lambda h: (0, 0))],
        out_specs=pl.BlockSpec((1,) + bucket.shape, lambda h: (h, 0, 0)),
        out_shape=jax.ShapeDtypeStruct((MOBA_HEADS,) + bucket.shape, F32),
        compiler_params=_cparams(("arbitrary",)),
        name="bias",
    )(rel_bias.reshape(-1), bucket)


def _moba_kernel(q_ref, k_ref, v_ref, bias_ref, o_ref, kmean_ref, m_ref, l_ref, acc_ref):
    blk = MOBA_BLOCK
    nb = MOBA_MAX_BLOCKS
    qi = pl.program_id(1)

    @pl.when(qi == 0)
    def _():
        kmean_ref[...] = jnp.zeros(kmean_ref.shape, F32)

    q = q_ref[0]
    own0 = pl.multiple_of(qi * blk, blk)
    k_own = k_ref[0, pl.ds(own0, blk), :]
    v_own = v_ref[0, pl.ds(own0, blk), :]

    col = lax.broadcasted_iota(I32, (blk, nb), 1)
    colf = col.astype(F32)
    gate = _nt_dot(q.astype(F32), kmean_ref[:, 0:MOBA_HEAD_DIM], precision=HIGHEST)
    neg = -jnp.inf
    g = jnp.where(col < qi, gate, neg)
    sel = jnp.zeros((blk, nb), F32)
    for _ in range(MOBA_TOPK):
        mx = jnp.max(g, axis=1, keepdims=True)
        first = jnp.min(jnp.where(g == mx, colf, float(nb)), axis=1, keepdims=True)
        pick = colf == jnp.where(mx > neg, first, -1.0)
        sel = jnp.where(pick, 1.0, sel)
        g = jnp.where(pick, neg, g)
    maskrow = jnp.where(sel > 0.0, 0.0, jnp.where(col >= qi, 0.0, NEG_BIG))
    q_aug = jnp.concatenate(
        [q, maskrow.astype(BF16), jnp.zeros((blk, LANES - MOBA_HEAD_DIM - nb), BF16)], axis=1)

    def scores(kt):
        return _nt_dot(q_aug, kt)

    r_i = lax.broadcasted_iota(I32, (blk, blk), 0)
    c_i = lax.broadcasted_iota(I32, (blk, blk), 1)
    s = scores(k_own) + bias_ref[0, :, blk:2 * blk]
    s = jnp.where(c_i <= r_i, s, neg)
    m0 = jnp.max(s, axis=1, keepdims=True)
    p = jnp.exp(s - m0)
    m_ref[...] = m0
    l_ref[...] = jnp.sum(p, axis=1, keepdims=True)
    acc_ref[...] = jnp.dot(p.astype(BF16), v_own, preferred_element_type=F32)

    def update(s, vt):
        m_old = m_ref[...]
        m_new = jnp.maximum(m_old, jnp.max(s, axis=1, keepdims=True))
        a = jnp.exp(m_old - m_new)
        p = jnp.exp(s - m_new)
        m_ref[...] = m_new
        l_ref[...] = a * l_ref[...] + jnp.sum(p, axis=1, keepdims=True)
        acc_ref[...] = a * acc_ref[...] + jnp.dot(p.astype(BF16), vt,
                                                  preferred_element_type=F32)

    @pl.when(qi > 0)
    def _():
        j0 = pl.multiple_of((qi - 1) * blk, blk)
        s = scores(k_ref[0, pl.ds(j0, blk), :]) + bias_ref[0, :, 0:blk]
        update(s, v_ref[0, pl.ds(j0, blk), :])

    def far(j, carry):
        j0 = pl.multiple_of(j * blk, blk)
        update(scores(k_ref[0, pl.ds(j0, blk), :]), v_ref[0, pl.ds(j0, blk), :])
        return carry

    lax.fori_loop(0, jnp.maximum(qi - 1, 0), far, 0)

    o_ref[0] = (acc_ref[...] / l_ref[...]).astype(o_ref.dtype)
    kmean_ref[pl.ds(qi, 1), :] = jnp.mean(k_own.astype(F32), axis=0, keepdims=True)


def _moba(q, k_aug, v, bias):
    bh, s, hd = q.shape
    blk = MOBA_BLOCK
    return pl.pallas_call(
        _moba_kernel,
        grid=(bh, s // blk),
        in_specs=[pl.BlockSpec((1, blk, hd), lambda b, i: (b, i, 0)),
                  pl.BlockSpec((1, s, LANES), lambda b, i: (b, 0, 0)),
                  pl.BlockSpec((1, s, hd), lambda b, i: (b, 0, 0)),
                  pl.BlockSpec((1, blk, 2 * blk), lambda b, i: (b % MOBA_HEADS, 0, 0))],
        out_specs=pl.BlockSpec((1, blk, hd), lambda b, i: (b, i, 0)),
        out_shape=jax.ShapeDtypeStruct((bh, s, hd), BF16),
        scratch_shapes=[pltpu.VMEM((MOBA_MAX_BLOCKS, LANES), F32),
                        pltpu.VMEM((blk, 1), F32), pltpu.VMEM((blk, 1), F32),
                        pltpu.VMEM((blk, hd), F32)],
        compiler_params=_cparams(("arbitrary", "arbitrary")),
        name="moba",
    )(q, k_aug, v, bias)


def _gla_kernel(q_ref, k_ref, v_ref, gk_ref, wup_ref, bgk_ref, r_ref, gn_ref, o_ref,
                state_ref, *, nchunk):
    ch = GLA_CHUNK
    tc = nchunk * ch

    @pl.when(pl.program_id(2) == 0)
    def _():
        state_ref[...] = jnp.zeros(state_ref.shape, F32)

    z = jnp.dot(gk_ref[...], wup_ref[...], precision=HIGHEST,
                preferred_element_type=F32) + bgk_ref[...]
    log_a = (jnp.minimum(z, 0.0) - jnp.log(1.0 + jnp.exp(-jnp.abs(z)))) / GLA_GATE_NORMALIZER
    rin = lax.broadcasted_iota(I32, (tc, GLA_DK), 0) & (ch - 1)
    b = log_a
    sh = 1
    while sh < ch:
        b = b + jnp.where(rin >= sh, pltpu.roll(b, sh, axis=0), 0.0)
        sh *= 2
    q = q_ref[...].astype(F32) * (GLA_DK ** -0.5)
    k = k_ref[...].astype(F32)
    q_g = (q * jnp.exp(b)).astype(BF16)
    k_g = (k * jnp.exp(-b)).astype(BF16)
    causal = (lax.broadcasted_iota(I32, (ch, ch), 1) <= lax.broadcasted_iota(I32, (ch, ch), 0))
    eye = (lax.broadcasted_iota(I32, (GLA_DK, GLA_DK), 0)
           == lax.broadcasted_iota(I32, (GLA_DK, GLA_DK), 1))
    gn = gn_ref[...]
    state = state_ref[...]
    for n in range(nchunk):
        sl = slice(n * ch, (n + 1) * ch)
        b_c = b[sl]
        b_last = b[n * ch + ch - 1:n * ch + ch, :]
        k_end = (k[sl] * jnp.exp(b_last - b_c)).astype(BF16)
        v_c = v_ref[sl, :]
        att = jnp.where(causal, _nt_dot(q_g[sl], k_g[sl]), 0.0)
        o = (jnp.dot(att.astype(BF16), v_c, preferred_element_type=F32)
             + jnp.dot(q_g[sl], state.astype(BF16), preferred_element_type=F32))
        decay = jnp.exp(b_last)
        decay_col = jnp.sum(jnp.where(eye, jnp.broadcast_to(decay, (GLA_DK, GLA_DK)), 0.0),
                            axis=1, keepdims=True)
        kv = lax.dot_general(k_end, v_c, (((0,), (0,)), ((), ())),
                             preferred_element_type=F32)
        state = decay_col * state + kv
        o_ref[sl, :] = (_rms(o) * gn * r_ref[sl, :].astype(F32)).astype(o_ref.dtype)
    state_ref[...] = state


def _gla(qkb, vb, gk, wup, bgk, r_act, gn, bsz, seq, nchunk):
    t = qkb.shape[0]
    tc = nchunk * GLA_CHUNK
    nc = seq // tc
    rowblk = lambda w, off: pl.BlockSpec((tc, w), lambda b, h, c: (b * nc + c, h + off))
    return pl.pallas_call(
        functools.partial(_gla_kernel, nchunk=nchunk),
        grid=(bsz, GLA_HEADS, nc),
        in_specs=[rowblk(GLA_DK, 0), rowblk(GLA_DK, GLA_HEADS), rowblk(GLA_DV, 0),
                  pl.BlockSpec((tc, LANES), lambda b, h, c: (b * nc + c, 0)),
                  pl.BlockSpec((LANES, GLA_DK), lambda b, h, c: (0, h)),
                  pl.BlockSpec((1, GLA_DK), lambda b, h, c: (0, h)),
                  rowblk(GLA_DV, 0),
                  pl.BlockSpec((1, GLA_DV), lambda b, h, c: (0, 0))],
        out_specs=rowblk(GLA_DV, 0),
        out_shape=jax.ShapeDtypeStruct((t, GLA_VALUE_DIM), BF16),
        scratch_shapes=[pltpu.VMEM((GLA_DK, GLA_DV), F32)],
        compiler_params=_cparams(("arbitrary", "arbitrary", "arbitrary")),
        name="gla",
    )(qkb, qkb, vb, gk, wup, bgk, r_act, gn)


def _merge_kernel(ya_ref, yb_ref, g_ref, x_ref, wpa_ref, wpb_ref, wout_ref, g1_ref,
                  nw_ref, sc_ref, sh_ref, wr_ref, br_ref,
                  x1_ref, h2_ref, idx_ref, wts_ref):
    pa = jnp.dot(ya_ref[...], wpa_ref[...], preferred_element_type=F32)
    pb = jnp.dot(yb_ref[...], wpb_ref[...], preferred_element_type=F32)
    mixed = (g_ref[:, 0:D_MODEL].astype(F32) * pa
             + g_ref[:, D_MODEL:2 * D_MODEL].astype(F32) * pb)
    y = jnp.dot(mixed.astype(BF16), wout_ref[...], preferred_element_type=F32)
    x1 = x_ref[...] + g1_ref[0] * y
    x1_ref[...] = x1
    h2 = _rms(x1) * nw_ref[...]
    h2 = h2 * (1.0 + sc_ref[0]) + sh_ref[0]
    h2_ref[...] = h2
    logits = _nt_dot(wr_ref[...], h2, precision=HIGHEST) + br_ref[...]
    rowf = lax.broadcasted_iota(I32, logits.shape, 0).astype(F32)
    vals, idxs = [], []
    cur = logits
    for _ in range(TOP_K):
        mx = jnp.max(cur, axis=0, keepdims=True)
        first = jnp.min(jnp.where(cur == mx, rowf, float(N_EXPERTS)), axis=0, keepdims=True)
        vals.append(mx)
        idxs.append(first)
        cur = jnp.where(rowf == first, -jnp.inf, cur)
    es = [jnp.exp(v - vals[0]) for v in vals]
    tot = es[0]
    for e in es[1:]:
        tot = tot + e
    idx_ref[...] = jnp.concatenate(idxs, axis=0).astype(I32)
    wts_ref[...] = jnp.concatenate([e / tot for e in es], axis=0)


def _merge(ya, yb, gates, x2d, wpa, wpb, wout, g1, nw, sc, sh, wr_t, br, seq, tm):
    t, d = x2d.shape
    tpb = seq // tm
    row = lambda w: pl.BlockSpec((tm, w), lambda i: (i, 0))
    full = lambda a: pl.BlockSpec(a.shape, lambda i: (0,) * a.ndim)
    per_b = pl.BlockSpec((1, 1, d), lambda i: (i // tpb, 0, 0))
    colblk = pl.BlockSpec((TOP_K, tm), lambda i: (0, i))
    return pl.pallas_call(
        _merge_kernel,
        grid=(t // tm,),
        in_specs=[row(MOBA_WIDTH), row(GLA_VALUE_DIM), row(2 * D_MODEL), row(d),
                  full(wpa), full(wpb), full(wout), per_b,
                  pl.BlockSpec((1, d), lambda i: (0, 0)), per_b, per_b,
                  full(wr_t), full(br)],
        out_specs=[row(d), row(d), colblk, colblk],
        out_shape=[jax.ShapeDtypeStruct((t, d), F32), jax.ShapeDtypeStruct((t, d), F32),
                   jax.ShapeDtypeStruct((TOP_K, t), I32), jax.ShapeDtypeStruct((TOP_K, t), F32)],
        compiler_params=_cparams(("arbitrary",)),
        name="merge",
    )(ya, yb, gates, x2d, wpa, wpb, wout, g1, nw.reshape(1, d), sc, sh, wr_t, br)


def _rank_kernel(idx_ref, rank_ref, cnt_ref, carry_ref):
    tm = idx_ref.shape[1]

    @pl.when(pl.program_id(0) == 0)
    def _():
        carry_ref[...] = jnp.zeros(carry_ref.shape, F32)

    rows = lax.broadcasted_iota(I32, (N_EXPERTS, tm), 0)
    before = (lax.broadcasted_iota(I32, (tm, tm), 0)
              < lax.broadcasted_iota(I32, (tm, tm), 1))
    upper = jnp.where(before, 1.0, 0.0).astype(BF16)
    carry = carry_ref[:, 0:1]
    ranks = []
    for k in range(TOP_K):
        onehot = idx_ref[k:k + 1, :] == rows
        onef = jnp.where(onehot, 1.0, 0.0)
        earlier = jnp.dot(onef.astype(BF16), upper, preferred_element_type=F32) + carry
        ranks.append(jnp.sum(jnp.where(onehot, earlier, 0.0), axis=0, keepdims=True))
        carry = carry + jnp.sum(onef, axis=1, keepdims=True)
    rank_ref[...] = jnp.concatenate(ranks, axis=0).astype(I32)
    total = jnp.broadcast_to(carry, carry_ref.shape)
    carry_ref[...] = total
    cnt_ref[...] = total


def _rank(idx, tm):
    t = idx.shape[1]
    return pl.pallas_call(
        _rank_kernel,
        grid=(t // tm,),
        in_specs=[pl.BlockSpec((TOP_K, tm), lambda i: (0, i))],
        out_specs=[pl.BlockSpec((TOP_K, tm), lambda i: (0, i)),
                   pl.BlockSpec((N_EXPERTS, LANES), lambda i: (0, 0))],
        out_shape=[jax.ShapeDtypeStruct((TOP_K, t), I32),
                   jax.ShapeDtypeStruct((N_EXPERTS, LANES), F32)],
        scratch_shapes=[pltpu.VMEM((N_EXPERTS, LANES), F32)],
        compiler_params=_cparams(("arbitrary",)),
        name="rank",
    )(idx)


def _dest_kernel(pstart_ref, idx_ref, rank_ref, dest_ref):
    idx = idx_ref[...]
    off = jnp.zeros(idx.shape, I32)
    for e in range(N_EXPERTS):
        off = jnp.where(idx == e, pstart_ref[e], off)
    dest_ref[...] = rank_ref[...] + off


def _dest(pstart, idx, rank, tm):
    t = idx.shape[1]
    blk = pl.BlockSpec((TOP_K, tm), lambda i: (0, i))
    return pl.pallas_call(
        _dest_kernel,
        grid=(t // tm,),
        in_specs=[pl.BlockSpec(memory_space=pltpu.SMEM), blk, blk],
        out_specs=blk,
        out_shape=jax.ShapeDtypeStruct((TOP_K, t), I32),
        compiler_params=_cparams(("arbitrary",)),
        name="dest",
    )(pstart, idx, rank)


def _scatter_kernel(dest_ref, h_hbm, xin_hbm, xout_hbm, sem, *, tm):
    del xin_hbm
    i = pl.program_id(0)
    n = pl.num_programs(0)
    nrow = tm * TOP_K

    def wait_tile():
        pltpu.make_async_copy(h_hbm.at[pl.ds(0, nrow)], xout_hbm.at[pl.ds(0, nrow)], sem).wait()

    def issue(tt, carry):
        src = h_hbm.at[pl.ds(i * tm + tt, 1)]
        for k in range(TOP_K):
            d = dest_ref[tt * TOP_K + k]
            pltpu.make_async_copy(src, xout_hbm.at[pl.ds(d, 1)], sem).start()
        return carry

    lax.fori_loop(0, tm, issue, 0)

    @pl.when(i > 0)
    def _():
        wait_tile()

    @pl.when(i == n - 1)
    def _():
        wait_tile()


def _scatter(dest_tok, h2, n_pad, tm):
    t, d = h2.shape
    zeros = jnp.zeros((n_pad, d), h2.dtype)
    return pl.pallas_call(
        functools.partial(_scatter_kernel, tm=tm),
        grid=(t // tm,),
        in_specs=[pl.BlockSpec((tm * TOP_K,), lambda i: (i,), memory_space=pltpu.SMEM),
                  pl.BlockSpec(memory_space=pl.ANY),
                  pl.BlockSpec(memory_space=pl.ANY)],
        out_specs=pl.BlockSpec(memory_space=pl.ANY),
        out_shape=jax.ShapeDtypeStruct((n_pad, d), h2.dtype),
        scratch_shapes=[pltpu.SemaphoreType.DMA],
        input_output_aliases={2: 0},
        compiler_params=_cparams(("arbitrary",)),
        name="scatter",
    )(dest_tok, h2, zeros)


def _expert_kernel(be_ref, x_ref, wg_ref, bg_ref, wu_ref, bu_ref, wd_ref, bd_ref, o_ref,
                   wgb_ref, wub_ref, wdb_ref):
    i = pl.program_id(0)
    prev = be_ref[jnp.maximum(i - 1, 0)]

    @pl.when(jnp.logical_or(i == 0, be_ref[i] != prev))
    def _():
        wgb_ref[...] = wg_ref[0].astype(BF16)
        wub_ref[...] = wu_ref[0].astype(BF16)
        wdb_ref[...] = wd_ref[0].astype(BF16)

    xb = x_ref[...].astype(BF16)
    g = jnp.minimum(jnp.dot(xb, wgb_ref[...], preferred_element_type=F32) + bg_ref[0],
                    SWIGLU_LIMIT)
    u = jnp.clip(jnp.dot(xb, wub_ref[...], preferred_element_type=F32) + bu_ref[0],
                 -SWIGLU_LIMIT, SWIGLU_LIMIT)
    act = g * _sigmoid(SWIGLU_ALPHA * g) * (u + 1.0)
    o_ref[...] = jnp.dot(act.astype(BF16), wdb_ref[...], preferred_element_type=F32) + bd_ref[0]


def _expert(block_expert, x_pad, wg, bg, wu, bu, wd, bd):
    n_pad, d = x_pad.shape
    f = wg.shape[2]
    nblk = n_pad // MOE_BLOCK
    wspec = lambda a, b: pl.BlockSpec((1, a, b), lambda i, be: (be[i], 0, 0))
    grid_spec = pltpu.PrefetchScalarGridSpec(
        num_scalar_prefetch=1,
        grid=(nblk,),
        in_specs=[pl.BlockSpec((MOE_BLOCK, d), lambda i, be: (i, 0)),
                  wspec(d, f), wspec(1, f), wspec(d, f), wspec(1, f), wspec(f, d), wspec(1, d)],
        out_specs=pl.BlockSpec((MOE_BLOCK, d), lambda i, be: (i, 0)),
        scratch_shapes=[pltpu.VMEM((d, f), BF16), pltpu.VMEM((d, f), BF16),
                        pltpu.VMEM((f, d), BF16)])
    return pl.pallas_call(
        _expert_kernel,
        grid_spec=grid_spec,
        out_shape=jax.ShapeDtypeStruct((n_pad, d), F32),
        compiler_params=_cparams(("arbitrary",)),
        name="expert",
    )(block_expert, x_pad, wg, bg.reshape(N_EXPERTS, 1, f), wu, bu.reshape(N_EXPERTS, 1, f),
      wd, bd.reshape(N_EXPERTS, 1, d))


def _combine_kernel(dest_ref, y_hbm, x1_ref, wts_ref, g2_ref, nf_ref, o_ref, ybuf, sem, *,
                    tm, final):
    def row_copy(tt, k):
        d = dest_ref[tt * TOP_K + k]
        return pltpu.make_async_copy(y_hbm.at[pl.ds(d, 1)], ybuf.at[k, pl.ds(tt, 1)], sem)

    def issue(tt, carry):
        for k in range(TOP_K):
            row_copy(tt, k).start()
        return carry

    lax.fori_loop(0, tm, issue, 0)
    for k in range(TOP_K):
        pltpu.make_async_copy(y_hbm.at[pl.ds(0, tm)], ybuf.at[k], sem).wait()

    w = wts_ref[...]
    moe = w[:, 0:1] * ybuf[0]
    for k in range(1, TOP_K):
        moe = moe + w[:, k:k + 1] * ybuf[k]
    x2 = x1_ref[...] + g2_ref[0] * moe
    o_ref[...] = _rms(x2) * nf_ref[...] if final else x2


def _combine(dest_tok, y_pad, x1, wts_tok, g2, nf, seq, tm, final):
    t, d = x1.shape
    tpb = seq // tm
    return pl.pallas_call(
        functools.partial(_combine_kernel, tm=tm, final=final),
        grid=(t // tm,),
        in_specs=[pl.BlockSpec((tm * TOP_K,), lambda i: (i,), memory_space=pltpu.SMEM),
                  pl.BlockSpec(memory_space=pl.ANY),
                  pl.BlockSpec((tm, d), lambda i: (i, 0)),
                  pl.BlockSpec((tm, TOP_K), lambda i: (i, 0)),
                  pl.BlockSpec((1, 1, d), lambda i: (i // tpb, 0, 0)),
                  pl.BlockSpec((1, d), lambda i: (0, 0))],
        out_specs=pl.BlockSpec((tm, d), lambda i: (i, 0)),
        out_shape=jax.ShapeDtypeStruct((t, d), F32),
        scratch_shapes=[pltpu.VMEM((TOP_K, tm, d), F32), pltpu.SemaphoreType.DMA],
        compiler_params=_cparams(("arbitrary",)),
        name="combine",
    )(dest_tok, y_pad, x1, wts_tok, g2, nf.reshape(1, d))


def _pick(n, cands):
    for c in cands:
        if n % c == 0:
            return c
    raise ValueError(f"no tile in {cands} divides {n}")


def kernel(x, c, rel_bias, w_ada, b_ada, norm_mix, w_in, w_gk_up, b_gk, gla_norm,
           w_proj_moba, w_proj_gla, w_out, norm_ffn, w_router, b_router,
           w_gate, b_gate, w_up, b_up, w_down, b_down, norm_final):
    bsz, seq, d = x.shape
    depth = w_ada.shape[0]
    assert d == D_MODEL and seq % MOBA_BLOCK == 0 and seq // MOBA_BLOCK <= MOBA_MAX_BLOCKS
    t = bsz * seq
    tm = _pick(seq, (512, 256))
    nchunk = _pick(seq // GLA_CHUNK, (8, 4))
    n_blk = seq // MOBA_BLOCK
    x2d = x.reshape(t, d)
    bias = _bias_tiles(rel_bias)
    onehot_blk = jnp.asarray(
        np.eye(MOBA_MAX_BLOCKS, LANES - MOBA_HEAD_DIM, dtype=np.float32)[
            np.arange(seq) // MOBA_BLOCK]).astype(BF16)
    per_b = lambda v: v.reshape(bsz, 1, d)

    for l in range(depth):
        mod = _ada(c, w_ada[l], b_ada[l])
        sh1, sc1, g1, sh2, sc2, g2 = [per_b(m) for m in jnp.split(mod, 6, axis=-1)]
        w = w_in[l]
        o_gk = _W_QKVA + _W_QKB + GLA_VALUE_DIM
        w_cat = jnp.concatenate(
            [w[:, :o_gk], jnp.pad(w[:, o_gk:o_gk + GLA_GATE_RANK],
                                  ((0, 0), (0, LANES - GLA_GATE_RANK))),
             w[:, o_gk + GLA_GATE_RANK:]], axis=1).astype(BF16)
        qkva, qkb, vb, gk, r_act, gates = _inproj(x2d, norm_mix[l], sc1, sh1, w_cat, seq, tm)
        heads = qkva.reshape(bsz, seq, 3, MOBA_HEADS, MOBA_HEAD_DIM).transpose(2, 0, 3, 1, 4)
        heads = heads.reshape(3, bsz * MOBA_HEADS, seq, MOBA_HEAD_DIM)
        k_aug = jnp.concatenate(
            [heads[1], jnp.broadcast_to(onehot_blk, (bsz * MOBA_HEADS,) + onehot_blk.shape)],
            axis=-1)
        ya = _moba(heads[0], k_aug, heads[2], bias)
        ya = ya.reshape(bsz, MOBA_HEADS, seq, MOBA_HEAD_DIM).transpose(0, 2, 1, 3)
        ya = ya.reshape(t, MOBA_WIDTH)
        wup = jnp.pad(w_gk_up[l], ((0, LANES - GLA_GATE_RANK), (0, 0)))
        yb = _gla(qkb, vb, gk, wup, b_gk[l].reshape(1, -1), r_act,
                  gla_norm[l].reshape(1, -1), bsz, seq, nchunk)
        x1, h2, idx, wts = _merge(
            ya, yb, gates, x2d, w_proj_moba[l].astype(BF16), w_proj_gla[l].astype(BF16),
            w_out[l].astype(BF16), g1, norm_ffn[l], sc2, sh2,
            w_router[l].T, b_router[l].reshape(N_EXPERTS, 1), seq, tm)
        rank, cnt = _rank(idx, tm)
        counts = cnt[:, 0].astype(I32)
        padded = (counts + MOE_BLOCK - 1) // MOE_BLOCK * MOE_BLOCK
        pcum = jnp.cumsum(padded)
        pstart = (pcum - padded).astype(I32)
        n_blocks = (t * TOP_K + MOE_BLOCK - 1) // MOE_BLOCK + N_EXPERTS
        block_expert = jnp.minimum(
            jnp.searchsorted(pcum, jnp.arange(n_blocks) * MOE_BLOCK, side='right'),
            N_EXPERTS - 1).astype(I32)
        dest = _dest(pstart, idx, rank, tm)
        dest_tok = dest.T.reshape(t * TOP_K)
        tg = _pick(t, (256,))
        x_pad = _scatter(dest_tok, h2, n_blocks * MOE_BLOCK, tg)
        y_pad = _expert(block_expert, x_pad, w_gate[l], b_gate[l], w_up[l], b_up[l],
                        w_down[l], b_down[l])
        x2d = _combine(dest_tok, y_pad, x1, wts.T, g2, norm_final, seq, tg, l == depth - 1)
    return x2d.reshape(bsz, seq, d)
```

```python
import functools
import math

import numpy as np
import jax
import jax.numpy as jnp
from jax import lax
from jax.experimental import pallas as pl
from jax.experimental.pallas import tpu as pltpu

F32 = jnp.float32
BF16 = jnp.bfloat16
I32 = jnp.int32
HIGHEST = lax.Precision.HIGHEST

D_MODEL = 1024
MOBA_HEADS = 8
MOBA_HEAD_DIM = 64
MOBA_WIDTH = MOBA_HEADS * MOBA_HEAD_DIM
MOBA_BLOCK = 256
MOBA_TOPK = 3
MOBA_MAX_BLOCKS = 32
REL_BUCKETS = 32
REL_MAX_DIST = 128
GLA_HEADS = 4
GLA_KEY_DIM = D_MODEL // 2
GLA_VALUE_DIM = D_MODEL
GLA_DK = GLA_KEY_DIM // GLA_HEADS
GLA_DV = GLA_VALUE_DIM // GLA_HEADS
GLA_GATE_RANK = 16
GLA_GATE_NORMALIZER = 16.0
GLA_CHUNK = 64
N_EXPERTS = 32
TOP_K = 4
D_FF = D_MODEL
SWIGLU_ALPHA = 1.702
SWIGLU_LIMIT = 7.0
MOE_BLOCK = 256
EPS = 1e-6
LANES = 128
NEG_BIG = -1e30
LOG2E = math.log2(math.e)
VMEM_LIMIT = 56 * 1024 * 1024


def _cparams(sem, vmem=None):
    return pltpu.CompilerParams(dimension_semantics=sem,
                                vmem_limit_bytes=vmem or VMEM_LIMIT)


def _nt_dot(a, b, **kw):
    return lax.dot_general(a, b, (((1,), (1,)), ((), ())),
                           preferred_element_type=F32, **kw)


def _rms(x):
    return x * lax.rsqrt(jnp.mean(x * x, axis=-1, keepdims=True) + EPS)


def _sigmoid(x):
    return 1.0 / (1.0 + jnp.exp(-x))


def _ada_kernel(c_ref, w_ref, b_ref, o_ref):
    c = c_ref[...]
    s = c * _sigmoid(c)
    o_ref[...] = jnp.dot(s, w_ref[...], precision=HIGHEST,
                         preferred_element_type=F32) + b_ref[...]


def _ada(c, w, b):
    bsz, d = c.shape
    n = w.shape[1]
    rows = -(-bsz // 8) * 8
    cp = jnp.zeros((rows, d), F32).at[:bsz].set(c)
    tn = 768
    out = pl.pallas_call(
        _ada_kernel,
        grid=(n // tn,),
        in_specs=[pl.BlockSpec((rows, d), lambda j: (0, 0)),
                  pl.BlockSpec((d, tn), lambda j: (0, j)),
                  pl.BlockSpec((1, tn), lambda j: (0, j))],
        out_specs=pl.BlockSpec((rows, tn), lambda j: (0, j)),
        out_shape=jax.ShapeDtypeStruct((rows, n), F32),
        compiler_params=_cparams(("arbitrary",)),
        name="ada",
    )(cp, w, b.reshape(1, n))
    return out[:bsz]


_W_QKVA = 3 * MOBA_WIDTH
_W_QKB = 2 * GLA_KEY_DIM
_OFF_QKB = _W_QKVA
_OFF_VB = _OFF_QKB + _W_QKB
_OFF_GK = _OFF_VB + GLA_VALUE_DIM
_OFF_R = _OFF_GK + LANES
_OFF_G = _OFF_R + GLA_VALUE_DIM
_W_CAT = _OFF_G + 2 * D_MODEL


def _inproj_kernel(x_ref, nw_ref, sc_ref, sh_ref, w_ref,
                   qkva_ref, qkb_ref, vb_ref, gk_ref, r_ref, g_ref):
    h = _rms(x_ref[...]) * nw_ref[...]
    h = h * (1.0 + sc_ref[0]) + sh_ref[0]
    hb = h.astype(BF16)

    def mm(a, b):
        return jnp.dot(hb, w_ref[:, a:b], preferred_element_type=F32)

    qscale = MOBA_HEAD_DIM ** -0.5 * LOG2E
    qkva_ref[:, 0:MOBA_WIDTH] = (mm(0, MOBA_WIDTH) * qscale).astype(BF16)
    qkva_ref[:, MOBA_WIDTH:_W_QKVA] = mm(MOBA_WIDTH, _W_QKVA).astype(BF16)
    qkb_ref[...] = mm(_OFF_QKB, _OFF_VB).astype(BF16)
    vb_ref[...] = mm(_OFF_VB, _OFF_GK).astype(BF16)
    gk_ref[...] = mm(_OFF_GK, _OFF_R)
    r = mm(_OFF_R, _OFF_G)
    r_ref[...] = (r * _sigmoid(r)).astype(BF16)
    g_ref[...] = _sigmoid(mm(_OFF_G, _W_CAT)).astype(BF16)


def _inproj(x2d, nw, sc, sh, w_cat, seq, tm):
    t, d = x2d.shape
    tpb = seq // tm
    row = lambda w: pl.BlockSpec((tm, w), lambda i: (i, 0))
    per_b = pl.BlockSpec((1, 1, d), lambda i: (i // tpb, 0, 0))
    outs = [(_W_QKVA, BF16), (_W_QKB, BF16), (GLA_VALUE_DIM, BF16), (LANES, F32),
            (GLA_VALUE_DIM, BF16), (2 * D_MODEL, BF16)]
    return pl.pallas_call(
        _inproj_kernel,
        grid=(t // tm,),
        in_specs=[row(d), pl.BlockSpec((1, d), lambda i: (0, 0)), per_b, per_b,
                  pl.BlockSpec((d, _W_CAT), lambda i: (0, 0))],
        out_specs=[row(w) for w, _ in outs],
        out_shape=[jax.ShapeDtypeStruct((t, w), dt) for w, dt in outs],
        compiler_params=_cparams(("arbitrary",)),
        name="inproj",
    )(x2d, nw.reshape(1, d), sc, sh, w_cat)


def _t5_bucket_np(n):
    n = np.maximum(n, 0)
    max_exact = REL_BUCKETS // 2
    nf = np.maximum(n, max_exact).astype(np.float32)
    large = max_exact + (np.log(nf / max_exact) / math.log(REL_MAX_DIST / max_exact)
                         * (REL_BUCKETS - max_exact)).astype(np.int32)
    large = np.minimum(large, REL_BUCKETS - 1)
    return np.where(n < max_exact, n, large).astype(np.int32)


def _bucket_table():
    kj = np.arange(MOBA_BLOCK)[:, None]
    qi = np.arange(2 * MOBA_BLOCK)[None, :] % MOBA_BLOCK
    prev = np.arange(2 * MOBA_BLOCK)[None, :] < MOBA_BLOCK
    return _t5_bucket_np(qi - kj + np.where(prev, MOBA_BLOCK, 0))


def _bias_kernel(rb_ref, bucket_ref, o_ref):
    h = pl.program_id(0)
    bk = bucket_ref[...]
    far = rb_ref[(REL_BUCKETS - 1) * MOBA_HEADS + h]
    acc = jnp.zeros(bk.shape, F32)
    for b in range(REL_BUCKETS):
        acc = jnp.where(bk == b, rb_ref[b * MOBA_HEADS + h] - far, acc)
    o_ref[0] = acc * LOG2E


def _bias_tiles(rel_bias):
    bucket = jnp.asarray(_bucket_table())
    return pl.pallas_call(
        _bias_kernel,
        grid=(MOBA_HEADS,),
        in_specs=[pl.BlockSpec(memory_space=pltpu.SMEM),
                  pl.BlockSpec(bucket.shape, lambda h: (0, 0))],
        out_specs=pl.BlockSpec((1,) + bucket.shape, lambda h: (h, 0, 0)),
        out_shape=jax.ShapeDtypeStruct((MOBA_HEADS,) + bucket.shape, F32),
        compiler_params=_cparams(("arbitrary",)),
        name="bias",
    )(rel_bias.reshape(-1), bucket)


MOBA_HEADS_PER_STEP = 4
MOBA_FAR_GROUP = 2


def _moba_kernel(qt_ref, k_ref, vt_ref, bias_ref, o_ref, kmean_ref, qa_ref):
    blk = MOBA_BLOCK
    nb = MOBA_MAX_BLOCKS
    hd = MOBA_HEAD_DIM
    hp = MOBA_HEADS_PER_STEP
    grp = MOBA_FAR_GROUP
    qi = pl.program_id(2)
    neg = -jnp.inf

    @pl.when(qi == 0)
    def _():
        kmean_ref[...] = jnp.zeros(kmean_ref.shape, F32)

    row = lax.broadcasted_iota(I32, (nb, blk), 0)
    rowf = row.astype(F32)
    key_i = lax.broadcasted_iota(I32, (blk, blk), 0)
    qry_i = lax.broadcasted_iota(I32, (blk, blk), 1)
    pad = jnp.zeros((LANES - hd, blk), BF16)
    pad_hi = jnp.zeros((LANES - hd - nb, blk), BF16)
    prev_j = jnp.maximum(qi - 1, 0)

    def skewed(stages):
        vals = [None] * hp
        for step in range(hp + len(stages) - 1):
            for si, stage in enumerate(stages):
                h = step - si
                if 0 <= h < hp:
                    vals[h] = stage(h, vals[h])
        return vals

    def select(h, _):
        qt = qt_ref[0, h]
        gate = jnp.dot(kmean_ref[h, :, 0:hd], qt.astype(F32), precision=HIGHEST,
                       preferred_element_type=F32)
        g = jnp.where(row < qi, gate, neg)
        sel = jnp.zeros((nb, blk), F32)
        for _ in range(MOBA_TOPK):
            mx = jnp.max(g, axis=0, keepdims=True)
            first = jnp.min(jnp.where(g == mx, rowf, float(nb)), axis=0, keepdims=True)
            pick = rowf == jnp.where(mx > neg, first, -1.0)
            sel = jnp.where(pick, 1.0, sel)
            g = jnp.where(pick, neg, g)
        mask_prev = jnp.where(sel > 0.0, jnp.where(row == qi - 1, 0.0, NEG_BIG), NEG_BIG)
        mask_far = jnp.where(sel > 0.0, jnp.where(row < qi - 1, 0.0, NEG_BIG), NEG_BIG)
        qa_ref[h] = jnp.concatenate([qt, mask_far.astype(BF16), pad_hi], axis=0)
        return (jnp.concatenate([qt, pad], axis=0),
                jnp.concatenate([qt, mask_prev.astype(BF16), pad_hi], axis=0))

    def near_scores(h, qa):
        qa_own, qa_prev = qa
        s_own = jnp.dot(k_ref[0, h, qi], qa_own, preferred_element_type=F32)
        s_prev = jnp.dot(k_ref[0, h, prev_j], qa_prev, preferred_element_type=F32)
        return s_own, s_prev

    def near_softmax(h, ss):
        s_own, s_prev = ss
        s_own = jnp.where(key_i <= qry_i, s_own + bias_ref[h, :, blk:2 * blk], neg)
        s = jnp.concatenate([s_own, s_prev + bias_ref[h, :, 0:blk]], axis=0)
        m0 = jnp.max(s, axis=0, keepdims=True)
        return m0, jnp.exp2(s - m0)

    def near_pv(h, mp):
        m0, p = mp
        pb = p.astype(BF16)
        acc = (jnp.dot(vt_ref[0, h, qi], pb[0:blk], preferred_element_type=F32)
               + jnp.dot(vt_ref[0, h, prev_j], pb[blk:2 * blk], preferred_element_type=F32))
        return m0, jnp.sum(p, axis=0, keepdims=True), acc

    states = tuple(skewed([select, near_scores, near_softmax, near_pv]))

    def far(gi, states):
        j0 = gi * grp

        def qk(h, _):
            kt = k_ref[0, h, pl.ds(j0, grp)].reshape(grp * blk, LANES)
            return jnp.dot(kt, qa_ref[h], preferred_element_type=F32)

        def softmax(h, s):
            m_old = states[h][0]
            m_new = jnp.maximum(m_old, jnp.max(s, axis=0, keepdims=True))
            return m_new, jnp.exp2(m_old - m_new), jnp.exp2(s - m_new)

        def pv(h, sm):
            m_new, a, p = sm
            pb = p.astype(BF16)
            tot = a * states[h][2]
            for i in range(grp):
                tot = tot + jnp.dot(vt_ref[0, h, j0 + i], pb[i * blk:(i + 1) * blk],
                                    preferred_element_type=F32)
            return m_new, a * states[h][1] + jnp.sum(p, axis=0, keepdims=True), tot

        return tuple(skewed([qk, softmax, pv]))

    states = lax.fori_loop(0, (prev_j + grp - 1) // grp, far, tuple(states))

    for h in range(hp):
        _, l_fin, acc = states[h]
        o_ref[0, h * hd:(h + 1) * hd, :] = (acc / l_fin).astype(o_ref.dtype)
        kmean_ref[h, pl.ds(qi, 1), :] = jnp.mean(k_ref[0, h, qi].astype(F32), axis=0,
                                                 keepdims=True)


def _moba(qt, k_aug, vt, bias):
    bsz, nh, hd, s = qt.shape
    blk = MOBA_BLOCK
    hp = MOBA_HEADS_PER_STEP
    nblk = s // blk
    assert nh % hp == 0 and (nblk % MOBA_FAR_GROUP == 0 or nblk == 1)
    return pl.pallas_call(
        _moba_kernel,
        grid=(bsz, nh // hp, nblk),
        in_specs=[pl.BlockSpec((1, hp, hd, blk), lambda b, g, i: (b, g, 0, i)),
                  pl.BlockSpec((1, hp, nblk, blk, LANES), lambda b, g, i: (b, g, 0, 0, 0)),
                  pl.BlockSpec((1, hp, nblk, hd, blk), lambda b, g, i: (b, g, 0, 0, 0)),
                  pl.BlockSpec((hp, blk, 2 * blk), lambda b, g, i: (g, 0, 0))],
        out_specs=pl.BlockSpec((1, hp * hd, blk), lambda b, g, i: (b, g, i)),
        out_shape=jax.ShapeDtypeStruct((bsz, nh * hd, s), BF16),
        scratch_shapes=[pltpu.VMEM((hp, MOBA_MAX_BLOCKS, LANES), F32),
                        pltpu.VMEM((hp, LANES, blk), BF16)],
        compiler_params=_cparams(("arbitrary", "arbitrary", "arbitrary")),
        name="moba",
    )(qt, k_aug, vt, bias)


def _gla_kernel(q_ref, k_ref, v_ref, gk_ref, wup_ref, bgk_ref, r_ref, gn_ref, o_ref,
                state_ref, *, nchunk):
    ch = GLA_CHUNK
    tc = nchunk * ch

    @pl.when(pl.program_id(2) == 0)
    def _():
        state_ref[...] = jnp.zeros(state_ref.shape, F32)

    z = jnp.dot(gk_ref[...], wup_ref[...], precision=HIGHEST,
                preferred_element_type=F32) + bgk_ref[...]
    log_a = (jnp.minimum(z, 0.0) - jnp.log(1.0 + jnp.exp(-jnp.abs(z)))) / GLA_GATE_NORMALIZER
    rin = lax.broadcasted_iota(I32, (tc, GLA_DK), 0) & (ch - 1)
    b = log_a
    sh = 1
    while sh < ch:
        b = b + jnp.where(rin >= sh, pltpu.roll(b, sh, axis=0), 0.0)
        sh *= 2
    q = q_ref[...].astype(F32) * (GLA_DK ** -0.5)
    k = k_ref[...].astype(F32)
    q_g = (q * jnp.exp(b)).astype(BF16)
    k_g = (k * jnp.exp(-b)).astype(BF16)
    causal = (lax.broadcasted_iota(I32, (ch, ch), 1) <= lax.broadcasted_iota(I32, (ch, ch), 0))
    eye = (lax.broadcasted_iota(I32, (GLA_DK, GLA_DK), 0)
           == lax.broadcasted_iota(I32, (GLA_DK, GLA_DK), 1))
    gn = gn_ref[...]
    state = state_ref[...]
    for n in range(nchunk):
        sl = slice(n * ch, (n + 1) * ch)
        b_c = b[sl]
        b_last = b[n * ch + ch - 1:n * ch + ch, :]
        k_end = (k[sl] * jnp.exp(b_last - b_c)).astype(BF16)
        v_c = v_ref[sl, :]
        att = jnp.where(causal, _nt_dot(q_g[sl], k_g[sl]), 0.0)
        o = (jnp.dot(att.astype(BF16), v_c, preferred_element_type=F32)
             + jnp.dot(q_g[sl], state.astype(BF16), preferred_element_type=F32))
        decay = jnp.exp(b_last)
        decay_col = jnp.sum(jnp.where(eye, jnp.broadcast_to(decay, (GLA_DK, GLA_DK)), 0.0),
                            axis=1, keepdims=True)
        kv = lax.dot_general(k_end, v_c, (((0,), (0,)), ((), ())),
                             preferred_element_type=F32)
        state = decay_col * state + kv
        o_ref[sl, :] = (_rms(o) * gn * r_ref[sl, :].astype(F32)).astype(o_ref.dtype)
    state_ref[...] = state


def _gla(qkb, vb, gk, wup, bgk, r_act, gn, bsz, seq, nchunk):
    t = qkb.shape[0]
    tc = nchunk * GLA_CHUNK
    nc = seq // tc
    rowblk = lambda w, off: pl.BlockSpec((tc, w), lambda b, h, c: (b * nc + c, h + off))
    return pl.pallas_call(
        functools.partial(_gla_kernel, nchunk=nchunk),
        grid=(bsz, GLA_HEADS, nc),
        in_specs=[rowblk(GLA_DK, 0), rowblk(GLA_DK, GLA_HEADS), rowblk(GLA_DV, 0),
                  pl.BlockSpec((tc, LANES), lambda b, h, c: (b * nc + c, 0)),
                  pl.BlockSpec((LANES, GLA_DK), lambda b, h, c: (0, h)),
                  pl.BlockSpec((1, GLA_DK), lambda b, h, c: (0, h)),
                  rowblk(GLA_DV, 0),
                  pl.BlockSpec((1, GLA_DV), lambda b, h, c: (0, 0))],
        out_specs=rowblk(GLA_DV, 0),
        out_shape=jax.ShapeDtypeStruct((t, GLA_VALUE_DIM), BF16),
        scratch_shapes=[pltpu.VMEM((GLA_DK, GLA_DV), F32)],
        compiler_params=_cparams(("arbitrary", "arbitrary", "arbitrary")),
        name="gla",
    )(qkb, qkb, vb, gk, wup, bgk, r_act, gn)


def _merge_kernel(ya_ref, yb_ref, g_ref, x_ref, wpa_ref, wpb_ref, wout_ref, g1_ref,
                  nw_ref, sc_ref, sh_ref, wr_ref, br_ref,
                  x1_ref, h2_ref, idx_ref, wts_ref):
    pa = lax.dot_general(ya_ref[0], wpa_ref[...], (((0,), (0,)), ((), ())),
                         preferred_element_type=F32)
    pb = jnp.dot(yb_ref[...], wpb_ref[...], preferred_element_type=F32)
    mixed = (g_ref[:, 0:D_MODEL].astype(F32) * pa
             + g_ref[:, D_MODEL:2 * D_MODEL].astype(F32) * pb)
    y = jnp.dot(mixed.astype(BF16), wout_ref[...], preferred_element_type=F32)
    x1 = x_ref[...] + g1_ref[0] * y
    x1_ref[...] = x1
    h2 = _rms(x1) * nw_ref[...]
    h2 = h2 * (1.0 + sc_ref[0]) + sh_ref[0]
    h2_ref[...] = h2
    logits = _nt_dot(wr_ref[...], h2, precision=HIGHEST) + br_ref[...]
    rowf = lax.broadcasted_iota(I32, logits.shape, 0).astype(F32)
    vals, idxs = [], []
    cur = logits
    for _ in range(TOP_K):
        mx = jnp.max(cur, axis=0, keepdims=True)
        first = jnp.min(jnp.where(cur == mx, rowf, float(N_EXPERTS)), axis=0, keepdims=True)
        vals.append(mx)
        idxs.append(first)
        cur = jnp.where(rowf == first, -jnp.inf, cur)
    es = [jnp.exp(v - vals[0]) for v in vals]
    tot = es[0]
    for e in es[1:]:
        tot = tot + e
    idx_ref[...] = jnp.concatenate(idxs, axis=0).astype(I32)
    wts_ref[...] = jnp.concatenate([e / tot for e in es], axis=0)


def _merge(ya, yb, gates, x2d, wpa, wpb, wout, g1, nw, sc, sh, wr_t, br, seq, tm):
    t, d = x2d.shape
    tpb = seq // tm
    row = lambda w: pl.BlockSpec((tm, w), lambda i: (i, 0))
    full = lambda a: pl.BlockSpec(a.shape, lambda i: (0,) * a.ndim)
    per_b = pl.BlockSpec((1, 1, d), lambda i: (i // tpb, 0, 0))
    colblk = pl.BlockSpec((TOP_K, tm), lambda i: (0, i))
    return pl.pallas_call(
        _merge_kernel,
        grid=(t // tm,),
        in_specs=[pl.BlockSpec((1, MOBA_WIDTH, tm), lambda i: (i // tpb, 0, i % tpb)),
                  row(GLA_VALUE_DIM), row(2 * D_MODEL), row(d),
                  full(wpa), full(wpb), full(wout), per_b,
                  pl.BlockSpec((1, d), lambda i: (0, 0)), per_b, per_b,
                  full(wr_t), full(br)],
        out_specs=[row(d), row(d), colblk, colblk],
        out_shape=[jax.ShapeDtypeStruct((t, d), F32), jax.ShapeDtypeStruct((t, d), F32),
                   jax.ShapeDtypeStruct((TOP_K, t), I32), jax.ShapeDtypeStruct((TOP_K, t), F32)],
        compiler_params=_cparams(("arbitrary",)),
        name="merge",
    )(ya, yb, gates, x2d, wpa, wpb, wout, g1, nw.reshape(1, d), sc, sh, wr_t, br)


def _rank_kernel(idx_ref, rank_ref, cnt_ref, carry_ref):
    tm = idx_ref.shape[1]

    @pl.when(pl.program_id(0) == 0)
    def _():
        carry_ref[...] = jnp.zeros(carry_ref.shape, F32)

    rows = lax.broadcasted_iota(I32, (N_EXPERTS, tm), 0)
    before = (lax.broadcasted_iota(I32, (tm, tm), 0)
              < lax.broadcasted_iota(I32, (tm, tm), 1))
    upper = jnp.where(before, 1.0, 0.0).astype(BF16)
    carry = carry_ref[:, 0:1]
    ranks = []
    for k in range(TOP_K):
        onehot = idx_ref[k:k + 1, :] == rows
        onef = jnp.where(onehot, 1.0, 0.0)
        earlier = jnp.dot(onef.astype(BF16), upper, preferred_element_type=F32) + carry
        ranks.append(jnp.sum(jnp.where(onehot, earlier, 0.0), axis=0, keepdims=True))
        carry = carry + jnp.sum(onef, axis=1, keepdims=True)
    rank_ref[...] = jnp.concatenate(ranks, axis=0).astype(I32)
    total = jnp.broadcast_to(carry, carry_ref.shape)
    carry_ref[...] = total
    cnt_ref[...] = total


def _rank(idx, tm):
    t = idx.shape[1]
    return pl.pallas_call(
        _rank_kernel,
        grid=(t // tm,),
        in_specs=[pl.BlockSpec((TOP_K, tm), lambda i: (0, i))],
        out_specs=[pl.BlockSpec((TOP_K, tm), lambda i: (0, i)),
                   pl.BlockSpec((N_EXPERTS, LANES), lambda i: (0, 0))],
        out_shape=[jax.ShapeDtypeStruct((TOP_K, t), I32),
                   jax.ShapeDtypeStruct((N_EXPERTS, LANES), F32)],
        scratch_shapes=[pltpu.VMEM((N_EXPERTS, LANES), F32)],
        compiler_params=_cparams(("arbitrary",)),
        name="rank",
    )(idx)


def _dest_kernel(pstart_ref, idx_ref, rank_ref, dest_ref):
    idx = idx_ref[...]
    off = jnp.zeros(idx.shape, I32)
    for e in range(N_EXPERTS):
        off = jnp.where(idx == e, pstart_ref[e], off)
    dest_ref[...] = rank_ref[...] + off


def _dest(pstart, idx, rank, tm):
    t = idx.shape[1]
    blk = pl.BlockSpec((TOP_K, tm), lambda i: (0, i))
    return pl.pallas_call(
        _dest_kernel,
        grid=(t // tm,),
        in_specs=[pl.BlockSpec(memory_space=pltpu.SMEM), blk, blk],
        out_specs=blk,
        out_shape=jax.ShapeDtypeStruct((TOP_K, t), I32),
        compiler_params=_cparams(("arbitrary",)),
        name="dest",
    )(pstart, idx, rank)


def _scatter_kernel(dest_ref, h_ref, xin_hbm, xout_hbm, sem, *, tm):
    del xin_hbm

    def issue(tt, carry):
        for k in range(TOP_K):
            d = dest_ref[tt * TOP_K + k]
            pltpu.make_async_copy(h_ref.at[pl.ds(tt, 1)], xout_hbm.at[pl.ds(d, 1)], sem).start()
        return carry

    lax.fori_loop(0, tm, issue, 0)
    for k in range(TOP_K):
        pltpu.make_async_copy(h_ref, xout_hbm.at[pl.ds(0, tm)], sem).wait()


def _scatter(dest_tok, h2, n_pad, tm):
    t, d = h2.shape
    zeros = jnp.zeros((n_pad, d), h2.dtype)
    return pl.pallas_call(
        functools.partial(_scatter_kernel, tm=tm),
        grid=(t // tm,),
        in_specs=[pl.BlockSpec((tm * TOP_K,), lambda i: (i,), memory_space=pltpu.SMEM),
                  pl.BlockSpec((tm, d), lambda i: (i, 0)),
                  pl.BlockSpec(memory_space=pl.ANY)],
        out_specs=pl.BlockSpec(memory_space=pl.ANY),
        out_shape=jax.ShapeDtypeStruct((n_pad, d), h2.dtype),
        scratch_shapes=[pltpu.SemaphoreType.DMA],
        input_output_aliases={2: 0},
        compiler_params=_cparams(("arbitrary",)),
        name="scatter",
    )(dest_tok, h2, zeros)


def _expert_kernel(be_ref, x_ref, wg_ref, bg_ref, wu_ref, bu_ref, wd_ref, bd_ref, o_ref,
                   wgb_ref, wub_ref, wdb_ref):
    i = pl.program_id(0)
    prev = be_ref[jnp.maximum(i - 1, 0)]

    @pl.when(jnp.logical_or(i == 0, be_ref[i] != prev))
    def _():
        wgb_ref[...] = wg_ref[0].astype(BF16)
        wub_ref[...] = wu_ref[0].astype(BF16)
        wdb_ref[...] = wd_ref[0].astype(BF16)

    xb = x_ref[...].astype(BF16)
    g = jnp.minimum(jnp.dot(xb, wgb_ref[...], preferred_element_type=F32) + bg_ref[0],
                    SWIGLU_LIMIT)
    u = jnp.clip(jnp.dot(xb, wub_ref[...], preferred_element_type=F32) + bu_ref[0],
                 -SWIGLU_LIMIT, SWIGLU_LIMIT)
    act = g * _sigmoid(SWIGLU_ALPHA * g) * (u + 1.0)
    o_ref[...] = jnp.dot(act.astype(BF16), wdb_ref[...], preferred_element_type=F32) + bd_ref[0]


def _expert(block_expert, x_pad, wg, bg, wu, bu, wd, bd):
    n_pad, d = x_pad.shape
    f = wg.shape[2]
    nblk = n_pad // MOE_BLOCK
    wspec = lambda a, b: pl.BlockSpec((1, a, b), lambda i, be: (be[i], 0, 0))
    grid_spec = pltpu.PrefetchScalarGridSpec(
        num_scalar_prefetch=1,
        grid=(nblk,),
        in_specs=[pl.BlockSpec((MOE_BLOCK, d), lambda i, be: (i, 0)),
                  wspec(d, f), wspec(1, f), wspec(d, f), wspec(1, f), wspec(f, d), wspec(1, d)],
        out_specs=pl.BlockSpec((MOE_BLOCK, d), lambda i, be: (i, 0)),
        scratch_shapes=[pltpu.VMEM((d, f), BF16), pltpu.VMEM((d, f), BF16),
                        pltpu.VMEM((f, d), BF16)])
    return pl.pallas_call(
        _expert_kernel,
        grid_spec=grid_spec,
        out_shape=jax.ShapeDtypeStruct((n_pad, d), F32),
        compiler_params=_cparams(("arbitrary",)),
        name="expert",
    )(block_expert, x_pad, wg, bg.reshape(N_EXPERTS, 1, f), wu, bu.reshape(N_EXPERTS, 1, f),
      wd, bd.reshape(N_EXPERTS, 1, d))


def _combine_kernel(dest_ref, y_hbm, x1_ref, wts_ref, g2_ref, nf_ref, o_ref, ybuf, sem, *,
                    tm, final):
    def row_copy(tt, k):
        d = dest_ref[tt * TOP_K + k]
        return pltpu.make_async_copy(y_hbm.at[pl.ds(d, 1)], ybuf.at[k, pl.ds(tt, 1)], sem)

    def issue(tt, carry):
        for k in range(TOP_K):
            row_copy(tt, k).start()
        return carry

    lax.fori_loop(0, tm, issue, 0)
    for k in range(TOP_K):
        pltpu.make_async_copy(y_hbm.at[pl.ds(0, tm)], ybuf.at[k], sem).wait()

    w = wts_ref[...]
    moe = w[:, 0:1] * ybuf[0]
    for k in range(1, TOP_K):
        moe = moe + w[:, k:k + 1] * ybuf[k]
    x2 = x1_ref[...] + g2_ref[0] * moe
    o_ref[...] = _rms(x2) * nf_ref[...] if final else x2


def _combine(dest_tok, y_pad, x1, wts_tok, g2, nf, seq, tm, final):
    t, d = x1.shape
    tpb = seq // tm
    return pl.pallas_call(
        functools.partial(_combine_kernel, tm=tm, final=final),
        grid=(t // tm,),
        in_specs=[pl.BlockSpec((tm * TOP_K,), lambda i: (i,), memory_space=pltpu.SMEM),
                  pl.BlockSpec(memory_space=pl.ANY),
                  pl.BlockSpec((tm, d), lambda i: (i, 0)),
                  pl.BlockSpec((tm, TOP_K), lambda i: (i, 0)),
                  pl.BlockSpec((1, 1, d), lambda i: (i // tpb, 0, 0)),
                  pl.BlockSpec((1, d), lambda i: (0, 0))],
        out_specs=pl.BlockSpec((tm, d), lambda i: (i, 0)),
        out_shape=jax.ShapeDtypeStruct((t, d), F32),
        scratch_shapes=[pltpu.VMEM((TOP_K, tm, d), F32), pltpu.SemaphoreType.DMA],
        compiler_params=_cparams(("arbitrary",)),
        name="combine",
    )(dest_tok, y_pad, x1, wts_tok, g2, nf.reshape(1, d))


def _pick(n, cands):
    for c in cands:
        if n % c == 0:
            return c
    raise ValueError(f"no tile in {cands} divides {n}")


def kernel(x, c, rel_bias, w_ada, b_ada, norm_mix, w_in, w_gk_up, b_gk, gla_norm,
           w_proj_moba, w_proj_gla, w_out, norm_ffn, w_router, b_router,
           w_gate, b_gate, w_up, b_up, w_down, b_down, norm_final):
    bsz, seq, d = x.shape
    depth = w_ada.shape[0]
    assert d == D_MODEL and seq % MOBA_BLOCK == 0 and seq // MOBA_BLOCK <= MOBA_MAX_BLOCKS
    t = bsz * seq
    tm = _pick(seq, (512, 256))
    nchunk = _pick(seq // GLA_CHUNK, (8, 4))
    n_blk = seq // MOBA_BLOCK
    x2d = x.reshape(t, d)
    bias = _bias_tiles(rel_bias)
    onehot_blk = jnp.asarray(
        np.eye(MOBA_MAX_BLOCKS, LANES - MOBA_HEAD_DIM, dtype=np.float32)[
            np.arange(n_blk)][:, None, :]).astype(BF16)
    per_b = lambda v: v.reshape(bsz, 1, d)

    for l in range(depth):
        mod = _ada(c, w_ada[l], b_ada[l])
        sh1, sc1, g1, sh2, sc2, g2 = [per_b(m) for m in jnp.split(mod, 6, axis=-1)]
        w = w_in[l]
        o_gk = _W_QKVA + _W_QKB + GLA_VALUE_DIM
        w_cat = jnp.concatenate(
            [w[:, :o_gk], jnp.pad(w[:, o_gk:o_gk + GLA_GATE_RANK],
                                  ((0, 0), (0, LANES - GLA_GATE_RANK))),
             w[:, o_gk + GLA_GATE_RANK:]], axis=1).astype(BF16)
        qkva, qkb, vb, gk, r_act, gates = _inproj(x2d, norm_mix[l], sc1, sh1, w_cat, seq, tm)
        heads = qkva.reshape(bsz, n_blk, MOBA_BLOCK, 3, MOBA_HEADS, MOBA_HEAD_DIM)
        qt = heads[:, :, :, 0].transpose(0, 3, 4, 1, 2).reshape(
            bsz, MOBA_HEADS, MOBA_HEAD_DIM, seq)
        k_blocks = heads[:, :, :, 1].transpose(0, 3, 1, 2, 4)
        k_aug = jnp.concatenate(
            [k_blocks, jnp.broadcast_to(onehot_blk, k_blocks.shape)], axis=-1)
        vt = heads[:, :, :, 2].transpose(0, 3, 1, 4, 2)
        ya = _moba(qt, k_aug, vt, bias)
        wup = jnp.pad(w_gk_up[l], ((0, LANES - GLA_GATE_RANK), (0, 0)))
        yb = _gla(qkb, vb, gk, wup, b_gk[l].reshape(1, -1), r_act,
                  gla_norm[l].reshape(1, -1), bsz, seq, nchunk)
        x1, h2, idx, wts = _merge(
            ya, yb, gates, x2d, w_proj_moba[l].astype(BF16), w_proj_gla[l].astype(BF16),
            w_out[l].astype(BF16), g1, norm_ffn[l], sc2, sh2,
            w_router[l].T, b_router[l].reshape(N_EXPERTS, 1), seq, tm)
        rank, cnt = _rank(idx, tm)
        counts = cnt[:, 0].astype(I32)
        padded = (counts + MOE_BLOCK - 1) // MOE_BLOCK * MOE_BLOCK
        pcum = jnp.cumsum(padded)
        pstart = (pcum - padded).astype(I32)
        n_blocks = (t * TOP_K + MOE_BLOCK - 1) // MOE_BLOCK + N_EXPERTS
        block_row0 = jnp.arange(n_blocks, dtype=I32) * MOE_BLOCK
        block_expert = jnp.minimum(
            jnp.sum((pcum[None, :] <= block_row0[:, None]).astype(I32), axis=1),
            N_EXPERTS - 1).astype(I32)
        dest = _dest(pstart, idx, rank, tm)
        dest_tok = dest.T.reshape(t * TOP_K)
        tg = _pick(t, (256,))
        x_pad = _scatter(dest_tok, h2, n_blocks * MOE_BLOCK, tg)
        y_pad = _expert(block_expert, x_pad, w_gate[l], b_gate[l], w_up[l], b_up[l],
                        w_down[l], b_down[l])
        x2d = _combine(dest_tok, y_pad, x1, wts.T, g2, norm_final, seq, tg, l == depth - 1)
    return x2d.reshape(bsz, seq, d)
```

```python
import functools
import math

import numpy as np
import jax
import jax.numpy as jnp
from jax import lax
from jax.experimental import pallas as pl
from jax.experimental.pallas import tpu as pltpu

F32 = jnp.float32
BF16 = jnp.bfloat16
I32 = jnp.int32
HIGHEST = lax.Precision.HIGHEST

D_MODEL = 1024
MOBA_HEADS = 8
MOBA_HEAD_DIM = 64
MOBA_WIDTH = MOBA_HEADS * MOBA_HEAD_DIM
MOBA_BLOCK = 256
MOBA_TOPK = 3
MOBA_MAX_BLOCKS = 32
REL_BUCKETS = 32
REL_MAX_DIST = 128
GLA_HEADS = 4
GLA_KEY_DIM = D_MODEL // 2
GLA_VALUE_DIM = D_MODEL
GLA_DK = GLA_KEY_DIM // GLA_HEADS
GLA_DV = GLA_VALUE_DIM // GLA_HEADS
GLA_GATE_RANK = 16
GLA_GATE_NORMALIZER = 16.0
GLA_CHUNK = 64
N_EXPERTS = 32
TOP_K = 4
D_FF = D_MODEL
SWIGLU_ALPHA = 1.702
SWIGLU_LIMIT = 7.0
MOE_ROWS = 512
EPS = 1e-6
LANES = 128
NEG_BIG = -1e30
LOG2E = math.log2(math.e)
VMEM_LIMIT = 56 * 1024 * 1024


def _cparams(sem, vmem=None):
    return pltpu.CompilerParams(dimension_semantics=sem,
                                vmem_limit_bytes=vmem or VMEM_LIMIT)


def _nt_dot(a, b, **kw):
    return lax.dot_general(a, b, (((1,), (1,)), ((), ())),
                           preferred_element_type=F32, **kw)


def _rms(x):
    return x * lax.rsqrt(jnp.mean(x * x, axis=-1, keepdims=True) + EPS)


def _sigmoid(x):
    return 1.0 / (1.0 + jnp.exp(-x))


def _ada_kernel(c_ref, w_ref, b_ref, o_ref):
    c = c_ref[...]
    s = c * _sigmoid(c)
    o_ref[...] = jnp.dot(s, w_ref[...], precision=HIGHEST,
                         preferred_element_type=F32) + b_ref[...]


def _ada(c, w, b):
    bsz, d = c.shape
    n = w.shape[1]
    rows = -(-bsz // 8) * 8
    cp = jnp.zeros((rows, d), F32).at[:bsz].set(c)
    tn = 768
    out = pl.pallas_call(
        _ada_kernel,
        grid=(n // tn,),
        in_specs=[pl.BlockSpec((rows, d), lambda j: (0, 0)),
                  pl.BlockSpec((d, tn), lambda j: (0, j)),
                  pl.BlockSpec((1, tn), lambda j: (0, j))],
        out_specs=pl.BlockSpec((rows, tn), lambda j: (0, j)),
        out_shape=jax.ShapeDtypeStruct((rows, n), F32),
        compiler_params=_cparams(("arbitrary",)),
        name="ada",
    )(cp, w, b.reshape(1, n))
    return out[:bsz]


_W_QKVA = 3 * MOBA_WIDTH
_W_QKB = 2 * GLA_KEY_DIM
_OFF_QKB = _W_QKVA
_OFF_VB = _OFF_QKB + _W_QKB
_OFF_GK = _OFF_VB + GLA_VALUE_DIM
_OFF_R = _OFF_GK + LANES
_OFF_G = _OFF_R + GLA_VALUE_DIM
_W_CAT = _OFF_G + 2 * D_MODEL


def _inproj_kernel(x_ref, nw_ref, sc_ref, sh_ref, w_ref,
                   qkva_ref, qkb_ref, vb_ref, gk_ref, r_ref, g_ref):
    h = _rms(x_ref[...]) * nw_ref[...]
    h = h * (1.0 + sc_ref[0]) + sh_ref[0]
    hb = h.astype(BF16)

    def mm(a, b):
        return jnp.dot(hb, w_ref[:, a:b], preferred_element_type=F32)

    qscale = MOBA_HEAD_DIM ** -0.5 * LOG2E
    qkva_ref[:, 0:MOBA_WIDTH] = (mm(0, MOBA_WIDTH) * qscale).astype(BF16)
    qkva_ref[:, MOBA_WIDTH:_W_QKVA] = mm(MOBA_WIDTH, _W_QKVA).astype(BF16)
    qkb_ref[...] = mm(_OFF_QKB, _OFF_VB).astype(BF16)
    vb_ref[...] = mm(_OFF_VB, _OFF_GK).astype(BF16)
    gk_ref[...] = mm(_OFF_GK, _OFF_R)
    r = mm(_OFF_R, _OFF_G)
    r_ref[...] = (r * _sigmoid(r)).astype(BF16)
    g_ref[...] = _sigmoid(mm(_OFF_G, _W_CAT)).astype(BF16)


def _inproj(x2d, nw, sc, sh, w_cat, seq, tm):
    t, d = x2d.shape
    tpb = seq // tm
    row = lambda w: pl.BlockSpec((tm, w), lambda i: (i, 0))
    per_b = pl.BlockSpec((1, 1, d), lambda i: (i // tpb, 0, 0))
    outs = [(_W_QKVA, BF16), (_W_QKB, BF16), (GLA_VALUE_DIM, BF16), (LANES, F32),
            (GLA_VALUE_DIM, BF16), (2 * D_MODEL, BF16)]
    return pl.pallas_call(
        _inproj_kernel,
        grid=(t // tm,),
        in_specs=[row(d), pl.BlockSpec((1, d), lambda i: (0, 0)), per_b, per_b,
                  pl.BlockSpec((d, _W_CAT), lambda i: (0, 0))],
        out_specs=[row(w) for w, _ in outs],
        out_shape=[jax.ShapeDtypeStruct((t, w), dt) for w, dt in outs],
        compiler_params=_cparams(("arbitrary",)),
        name="inproj",
    )(x2d, nw.reshape(1, d), sc, sh, w_cat)


def _t5_bucket_np(n):
    n = np.maximum(n, 0)
    max_exact = REL_BUCKETS // 2
    nf = np.maximum(n, max_exact).astype(np.float32)
    large = max_exact + (np.log(nf / max_exact) / math.log(REL_MAX_DIST / max_exact)
                         * (REL_BUCKETS - max_exact)).astype(np.int32)
    large = np.minimum(large, REL_BUCKETS - 1)
    return np.where(n < max_exact, n, large).astype(np.int32)


def _bucket_table():
    kj = np.arange(MOBA_BLOCK)[:, None]
    qi = np.arange(2 * MOBA_BLOCK)[None, :] % MOBA_BLOCK
    prev = np.arange(2 * MOBA_BLOCK)[None, :] < MOBA_BLOCK
    return _t5_bucket_np(qi - kj + np.where(prev, MOBA_BLOCK, 0))


def _bias_kernel(rb_ref, bucket_ref, o_ref):
    h = pl.program_id(0)
    bk = bucket_ref[...]
    far = rb_ref[(REL_BUCKETS - 1) * MOBA_HEADS + h]
    acc = jnp.zeros(bk.shape, F32)
    for b in range(REL_BUCKETS):
        acc = jnp.where(bk == b, rb_ref[b * MOBA_HEADS + h] - far, acc)
    o_ref[0] = acc * LOG2E


def _bias_tiles(rel_bias):
    bucket = jnp.asarray(_bucket_table())
    return pl.pallas_call(
        _bias_kernel,
        grid=(MOBA_HEADS,),
        in_specs=[pl.BlockSpec(memory_space=pltpu.SMEM),
                  pl.BlockSpec(bucket.shape, lambda h: (0, 0))],
        out_specs=pl.BlockSpec((1,) + bucket.shape, lambda h: (h, 0, 0)),
        out_shape=jax.ShapeDtypeStruct((MOBA_HEADS,) + bucket.shape, F32),
        compiler_params=_cparams(("arbitrary",)),
        name="bias",
    )(rel_bias.reshape(-1), bucket)


MOBA_HEADS_PER_STEP = 8
MOBA_FAR_GROUP = 2
MOBA_FAR_LAGS = (0, 3, 6)
MOBA_VT_ROWS = MOBA_HEAD_DIM + 16


def _moba_kernel(qt_ref, k_ref, vt_ref, bias_ref, o_ref, kmean_ref, qa_ref):
    blk = MOBA_BLOCK
    nb = MOBA_MAX_BLOCKS
    hd = MOBA_HEAD_DIM
    hp = MOBA_HEADS_PER_STEP
    grp = MOBA_FAR_GROUP
    qi = pl.program_id(2)
    neg = -jnp.inf

    @pl.when(qi == 0)
    def _():
        kmean_ref[...] = jnp.zeros(kmean_ref.shape, F32)

    row = lax.broadcasted_iota(I32, (nb, blk), 0)
    rowf = row.astype(F32)
    key_i = lax.broadcasted_iota(I32, (blk, blk), 0)
    qry_i = lax.broadcasted_iota(I32, (blk, blk), 1)
    pad = jnp.zeros((LANES - hd, blk), BF16)
    pad_hi = jnp.zeros((LANES - hd - nb, blk), BF16)
    prev_j = jnp.maximum(qi - 1, 0)

    def skewed(stages, lags):
        vals = [None] * hp
        for step in range(hp + lags[-1]):
            for stage, lag in zip(stages, lags):
                h = step - lag
                if 0 <= h < hp:
                    vals[h] = stage(h, vals[h])
        return vals

    def select(h, _):
        qt = qt_ref[0, h]
        gate = jnp.dot(kmean_ref[h, :, 0:hd], qt.astype(F32), precision=HIGHEST,
                       preferred_element_type=F32)
        g = jnp.where(row < qi, gate, neg)
        sel = jnp.zeros((nb, blk), F32)
        for _ in range(MOBA_TOPK):
            mx = jnp.max(g, axis=0, keepdims=True)
            first = jnp.min(jnp.where(g == mx, rowf, float(nb)), axis=0, keepdims=True)
            pick = rowf == jnp.where(mx > neg, first, -1.0)
            sel = jnp.where(pick, 1.0, sel)
            g = jnp.where(pick, neg, g)
        mask_prev = jnp.where(sel > 0.0, jnp.where(row == qi - 1, 0.0, NEG_BIG), NEG_BIG)
        mask_far = jnp.where(sel > 0.0, jnp.where(row < qi - 1, 0.0, NEG_BIG), NEG_BIG)
        qa_ref[h] = jnp.concatenate([qt, mask_far.astype(BF16), pad_hi], axis=0)
        return (jnp.concatenate([qt, pad], axis=0),
                jnp.concatenate([qt, mask_prev.astype(BF16), pad_hi], axis=0))

    def near_scores(h, qa):
        qa_own, qa_prev = qa
        s_own = jnp.dot(k_ref[0, h, qi], qa_own, preferred_element_type=F32)
        s_prev = jnp.dot(k_ref[0, h, prev_j], qa_prev, preferred_element_type=F32)
        return s_own, s_prev

    def near_softmax(h, ss):
        s_own, s_prev = ss
        s_own = jnp.where(key_i <= qry_i, s_own + bias_ref[h, :, blk:2 * blk], neg)
        s = jnp.concatenate([s_own, s_prev + bias_ref[h, :, 0:blk]], axis=0)
        m0 = jnp.max(s, axis=0, keepdims=True)
        return m0, jnp.exp2(s - m0)

    def near_pv(h, mp):
        m0, p = mp
        pb = p.astype(BF16)
        acc = (jnp.dot(vt_ref[0, h, qi], pb[0:blk], preferred_element_type=F32)
               + jnp.dot(vt_ref[0, h, prev_j], pb[blk:2 * blk], preferred_element_type=F32))
        return m0, acc

    states = tuple(skewed([select, near_scores, near_softmax, near_pv], (0, 1, 2, 3)))

    def far(gi, states):
        j0 = gi * grp

        def qk(h, _):
            kt = k_ref[0, h, pl.ds(j0, grp)].reshape(grp * blk, LANES)
            return jnp.dot(kt, qa_ref[h], preferred_element_type=F32)

        def softmax(h, s):
            m_old = states[h][0]
            m_new = jnp.maximum(m_old, jnp.max(s, axis=0, keepdims=True))
            return m_new, jnp.exp2(m_old - m_new), jnp.exp2(s - m_new)

        def pv(h, sm):
            m_new, a, p = sm
            pb = p.astype(BF16)
            tot = a * states[h][1]
            for i in range(grp):
                tot = tot + jnp.dot(vt_ref[0, h, j0 + i], pb[i * blk:(i + 1) * blk],
                                    preferred_element_type=F32)
            return m_new, tot

        return tuple(skewed([qk, softmax, pv], MOBA_FAR_LAGS))

    states = lax.fori_loop(0, (prev_j + grp - 1) // grp, far, tuple(states))

    for h in range(hp):
        acc = states[h][1]
        o_ref[0, h * hd:(h + 1) * hd, :] = (acc[0:hd] / acc[hd:hd + 1]).astype(o_ref.dtype)
        kmean_ref[h, pl.ds(qi, 1), :] = jnp.mean(k_ref[0, h, qi].astype(F32), axis=0,
                                                 keepdims=True)


def _moba(qt, k_aug, vt, bias):
    bsz, nh, hd, s = qt.shape
    blk = MOBA_BLOCK
    hp = MOBA_HEADS_PER_STEP
    nblk = s // blk
    assert nh % hp == 0 and (nblk % MOBA_FAR_GROUP == 0 or nblk == 1)
    return pl.pallas_call(
        _moba_kernel,
        grid=(bsz, nh // hp, nblk),
        in_specs=[pl.BlockSpec((1, hp, hd, blk), lambda b, g, i: (b, g, 0, i)),
                  pl.BlockSpec((1, hp, nblk, blk, LANES), lambda b, g, i: (b, g, 0, 0, 0),
                               pipeline_mode=pl.Buffered(1)),
                  pl.BlockSpec((1, hp, nblk, MOBA_VT_ROWS, blk), lambda b, g, i: (b, g, 0, 0, 0),
                               pipeline_mode=pl.Buffered(1)),
                  pl.BlockSpec((hp, blk, 2 * blk), lambda b, g, i: (g, 0, 0),
                               pipeline_mode=pl.Buffered(1))],
        out_specs=pl.BlockSpec((1, hp * hd, blk), lambda b, g, i: (b, g, i)),
        out_shape=jax.ShapeDtypeStruct((bsz, nh * hd, s), BF16),
        scratch_shapes=[pltpu.VMEM((hp, MOBA_MAX_BLOCKS, LANES), F32),
                        pltpu.VMEM((hp, LANES, blk), BF16)],
        compiler_params=_cparams(("arbitrary", "arbitrary", "arbitrary")),
        name="moba",
    )(qt, k_aug, vt, bias)


def _gla_kernel(q_ref, k_ref, v_ref, gk_ref, wup_ref, bgk_ref, r_ref, gn_ref, o_ref,
                state_ref, *, nchunk):
    ch = GLA_CHUNK
    tc = nchunk * ch

    @pl.when(pl.program_id(2) == 0)
    def _():
        state_ref[...] = jnp.zeros(state_ref.shape, F32)

    z = jnp.dot(gk_ref[...], wup_ref[...], precision=HIGHEST,
                preferred_element_type=F32) + bgk_ref[...]
    log_a = (jnp.minimum(z, 0.0) - jnp.log(1.0 + jnp.exp(-jnp.abs(z)))) / GLA_GATE_NORMALIZER
    rin = lax.broadcasted_iota(I32, (tc, GLA_DK), 0) & (ch - 1)
    b = log_a
    sh = 1
    while sh < ch:
        b = b + jnp.where(rin >= sh, pltpu.roll(b, sh, axis=0), 0.0)
        sh *= 2
    q = q_ref[...].astype(F32) * (GLA_DK ** -0.5)
    k = k_ref[...].astype(F32)
    q_g = (q * jnp.exp(b)).astype(BF16)
    k_g = (k * jnp.exp(-b)).astype(BF16)
    causal = (lax.broadcasted_iota(I32, (ch, ch), 1) <= lax.broadcasted_iota(I32, (ch, ch), 0))
    eye = (lax.broadcasted_iota(I32, (GLA_DK, GLA_DK), 0)
           == lax.broadcasted_iota(I32, (GLA_DK, GLA_DK), 1))
    b3 = b.reshape(nchunk, ch, GLA_DK)
    b_last = b3[:, ch - 1:ch, :]
    k_end = (k * jnp.exp(jnp.broadcast_to(b_last, b3.shape) - b3).reshape(tc, GLA_DK)
             ).astype(BF16)
    decay = jnp.exp(b_last)
    chunks = [slice(n * ch, (n + 1) * ch) for n in range(nchunk)]
    o_intra, kv, decay_col = [], [], []
    for n, sl in enumerate(chunks):
        v_c = v_ref[sl, :]
        att = jnp.where(causal, _nt_dot(q_g[sl], k_g[sl]), 0.0)
        o_intra.append(jnp.dot(att.astype(BF16), v_c, preferred_element_type=F32))
        kv.append(lax.dot_general(k_end[sl], v_c, (((0,), (0,)), ((), ())),
                                  preferred_element_type=F32))
        decay_col.append(jnp.sum(
            jnp.where(eye, jnp.broadcast_to(decay[n], (GLA_DK, GLA_DK)), 0.0),
            axis=1, keepdims=True))
    state = state_ref[...]
    o_inter = []
    for n, sl in enumerate(chunks):
        o_inter.append(jnp.dot(q_g[sl], state.astype(BF16), preferred_element_type=F32))
        state = decay_col[n] * state + kv[n]
    state_ref[...] = state
    o = jnp.concatenate([a + c for a, c in zip(o_intra, o_inter)], axis=0)
    o_ref[...] = (_rms(o) * gn_ref[...] * r_ref[...].astype(F32)).astype(o_ref.dtype)


def _gla(qkb, vb, gk, wup, bgk, r_act, gn, bsz, seq, nchunk):
    t = qkb.shape[0]
    tc = nchunk * GLA_CHUNK
    nc = seq // tc
    rowblk = lambda w, off: pl.BlockSpec((tc, w), lambda b, h, c: (b * nc + c, h + off))
    return pl.pallas_call(
        functools.partial(_gla_kernel, nchunk=nchunk),
        grid=(bsz, GLA_HEADS, nc),
        in_specs=[rowblk(GLA_DK, 0), rowblk(GLA_DK, GLA_HEADS), rowblk(GLA_DV, 0),
                  pl.BlockSpec((tc, LANES), lambda b, h, c: (b * nc + c, 0)),
                  pl.BlockSpec((LANES, GLA_DK), lambda b, h, c: (0, h)),
                  pl.BlockSpec((1, GLA_DK), lambda b, h, c: (0, h)),
                  rowblk(GLA_DV, 0),
                  pl.BlockSpec((1, GLA_DV), lambda b, h, c: (0, 0))],
        out_specs=rowblk(GLA_DV, 0),
        out_shape=jax.ShapeDtypeStruct((t, GLA_VALUE_DIM), BF16),
        scratch_shapes=[pltpu.VMEM((GLA_DK, GLA_DV), F32)],
        compiler_params=_cparams(("arbitrary", "arbitrary", "arbitrary")),
        name="gla",
    )(qkb, qkb, vb, gk, wup, bgk, r_act, gn)


def _merge_kernel(ya_ref, yb_ref, g_ref, x_ref, wpa_ref, wpb_ref, wout_ref, g1_ref,
                  nw_ref, sc_ref, sh_ref, wr_ref, br_ref,
                  x1_ref, h2_ref, idx_ref, wts_ref):
    pa = lax.dot_general(ya_ref[0], wpa_ref[...], (((0,), (0,)), ((), ())),
                         preferred_element_type=F32)
    pb = jnp.dot(yb_ref[...], wpb_ref[...], preferred_element_type=F32)
    mixed = (g_ref[:, 0:D_MODEL].astype(F32) * pa
             + g_ref[:, D_MODEL:2 * D_MODEL].astype(F32) * pb)
    y = jnp.dot(mixed.astype(BF16), wout_ref[...], preferred_element_type=F32)
    x1 = x_ref[...] + g1_ref[0] * y
    x1_ref[...] = x1
    h2 = _rms(x1) * nw_ref[...]
    h2 = h2 * (1.0 + sc_ref[0]) + sh_ref[0]
    h2_ref[...] = h2
    logits = _nt_dot(wr_ref[...], h2, precision=HIGHEST) + br_ref[...]
    rowf = lax.broadcasted_iota(I32, logits.shape, 0).astype(F32)
    vals, idxs = [], []
    cur = logits
    for _ in range(TOP_K):
        mx = jnp.max(cur, axis=0, keepdims=True)
        first = jnp.min(jnp.where(cur == mx, rowf, float(N_EXPERTS)), axis=0, keepdims=True)
        vals.append(mx)
        idxs.append(first)
        cur = jnp.where(rowf == first, -jnp.inf, cur)
    es = [jnp.exp(v - vals[0]) for v in vals]
    tot = es[0]
    for e in es[1:]:
        tot = tot + e
    idx_ref[...] = jnp.concatenate(idxs, axis=0).astype(I32)
    wts_ref[...] = jnp.concatenate([e / tot for e in es], axis=0)


def _merge(ya, yb, gates, x2d, wpa, wpb, wout, g1, nw, sc, sh, wr_t, br, seq, tm):
    t, d = x2d.shape
    tpb = seq // tm
    row = lambda w: pl.BlockSpec((tm, w), lambda i: (i, 0))
    full = lambda a: pl.BlockSpec(a.shape, lambda i: (0,) * a.ndim)
    per_b = pl.BlockSpec((1, 1, d), lambda i: (i // tpb, 0, 0))
    colblk = pl.BlockSpec((TOP_K, tm), lambda i: (0, i))
    return pl.pallas_call(
        _merge_kernel,
        grid=(t // tm,),
        in_specs=[pl.BlockSpec((1, MOBA_WIDTH, tm), lambda i: (i // tpb, 0, i % tpb)),
                  row(GLA_VALUE_DIM), row(2 * D_MODEL), row(d),
                  full(wpa), full(wpb), full(wout), per_b,
                  pl.BlockSpec((1, d), lambda i: (0, 0)), per_b, per_b,
                  full(wr_t), full(br)],
        out_specs=[row(d), row(d), colblk, colblk],
        out_shape=[jax.ShapeDtypeStruct((t, d), F32), jax.ShapeDtypeStruct((t, d), F32),
                   jax.ShapeDtypeStruct((TOP_K, t), I32), jax.ShapeDtypeStruct((TOP_K, t), F32)],
        compiler_params=_cparams(("arbitrary",)),
        name="merge",
    )(ya, yb, gates, x2d, wpa, wpb, wout, g1, nw.reshape(1, d), sc, sh, wr_t, br)


def _rank_kernel(idx_ref, rank_ref, cnt_ref, carry_ref):
    tm = idx_ref.shape[1]

    @pl.when(pl.program_id(0) == 0)
    def _():
        carry_ref[...] = jnp.zeros(carry_ref.shape, F32)

    rows = lax.broadcasted_iota(I32, (N_EXPERTS, tm), 0)
    before = (lax.broadcasted_iota(I32, (tm, tm), 0)
              < lax.broadcasted_iota(I32, (tm, tm), 1))
    upper = jnp.where(before, 1.0, 0.0).astype(BF16)
    carry = carry_ref[:, 0:1]
    ranks = []
    for k in range(TOP_K):
        onehot = idx_ref[k:k + 1, :] == rows
        onef = jnp.where(onehot, 1.0, 0.0)
        earlier = jnp.dot(onef.astype(BF16), upper, preferred_element_type=F32) + carry
        ranks.append(jnp.sum(jnp.where(onehot, earlier, 0.0), axis=0, keepdims=True))
        carry = carry + jnp.sum(onef, axis=1, keepdims=True)
    rank_ref[...] = jnp.concatenate(ranks, axis=0).astype(I32)
    total = jnp.broadcast_to(carry, carry_ref.shape)
    carry_ref[...] = total
    cnt_ref[...] = total


def _rank(idx, tm):
    t = idx.shape[1]
    return pl.pallas_call(
        _rank_kernel,
        grid=(t // tm,),
        in_specs=[pl.BlockSpec((TOP_K, tm), lambda i: (0, i))],
        out_specs=[pl.BlockSpec((TOP_K, tm), lambda i: (0, i)),
                   pl.BlockSpec((N_EXPERTS, LANES), lambda i: (0, 0))],
        out_shape=[jax.ShapeDtypeStruct((TOP_K, t), I32),
                   jax.ShapeDtypeStruct((N_EXPERTS, LANES), F32)],
        scratch_shapes=[pltpu.VMEM((N_EXPERTS, LANES), F32)],
        compiler_params=_cparams(("arbitrary",)),
        name="rank",
    )(idx)


def _dest_kernel(pstart_ref, idx_ref, rank_ref, dest_ref):
    idx = idx_ref[...]
    off = jnp.zeros(idx.shape, I32)
    for e in range(N_EXPERTS):
        off = jnp.where(idx == e, pstart_ref[e], off)
    dest_ref[...] = rank_ref[...] + off


def _dest(pstart, idx, rank, tm):
    t = idx.shape[1]
    blk = pl.BlockSpec((TOP_K, tm), lambda i: (0, i))
    return pl.pallas_call(
        _dest_kernel,
        grid=(t // tm,),
        in_specs=[pl.BlockSpec(memory_space=pltpu.SMEM), blk, blk],
        out_specs=blk,
        out_shape=jax.ShapeDtypeStruct((TOP_K, t), I32),
        compiler_params=_cparams(("arbitrary",)),
        name="dest",
    )(pstart, idx, rank)


def _scatter_kernel(dest_ref, h_ref, xin_hbm, xout_hbm, sem, *, tm):
    del xin_hbm

    def issue(tt, carry):
        for k in range(TOP_K):
            d = dest_ref[tt * TOP_K + k]
            pltpu.make_async_copy(h_ref.at[pl.ds(tt, 1)], xout_hbm.at[pl.ds(d, 1)], sem).start()
        return carry

    lax.fori_loop(0, tm, issue, 0)
    for k in range(TOP_K):
        pltpu.make_async_copy(h_ref, xout_hbm.at[pl.ds(0, tm)], sem).wait()


def _scatter(dest_tok, h2, n_pad, tm):
    t, d = h2.shape
    zeros = jnp.zeros((n_pad, d), h2.dtype)
    return pl.pallas_call(
        functools.partial(_scatter_kernel, tm=tm),
        grid=(t // tm,),
        in_specs=[pl.BlockSpec((tm * TOP_K,), lambda i: (i,), memory_space=pltpu.SMEM),
                  pl.BlockSpec((tm, d), lambda i: (i, 0)),
                  pl.BlockSpec(memory_space=pl.ANY)],
        out_specs=pl.BlockSpec(memory_space=pl.ANY),
        out_shape=jax.ShapeDtypeStruct((n_pad, d), h2.dtype),
        scratch_shapes=[pltpu.SemaphoreType.DMA],
        input_output_aliases={2: 0},
        compiler_params=_cparams(("arbitrary",)),
        name="scatter",
    )(dest_tok, h2, zeros)


def _expert_kernel(be_ref, nused_ref, x_ref, wg_ref, bg_ref, wu_ref, bu_ref, wd_ref, bd_ref,
                   o_ref, wgb_ref, wub_ref, wdb_ref):
    i = pl.program_id(0)
    prev = be_ref[jnp.maximum(i - 1, 0)]

    @pl.when(jnp.logical_or(i == 0, be_ref[i] != prev))
    def _():
        wgb_ref[...] = wg_ref[0].astype(BF16)
        wub_ref[...] = wu_ref[0].astype(BF16)
        wdb_ref[...] = wd_ref[0].astype(BF16)

    @pl.when(i < nused_ref[0])
    def _():
        xb = x_ref[...].astype(BF16)
        g = jnp.minimum(jnp.dot(xb, wgb_ref[...], preferred_element_type=F32) + bg_ref[0],
                        SWIGLU_LIMIT)
        u = jnp.clip(jnp.dot(xb, wub_ref[...], preferred_element_type=F32) + bu_ref[0],
                     -SWIGLU_LIMIT, SWIGLU_LIMIT)
        act = g * _sigmoid(SWIGLU_ALPHA * g) * (u + 1.0)
        o_ref[...] = (jnp.dot(act.astype(BF16), wdb_ref[...], preferred_element_type=F32)
                      + bd_ref[0])

    @pl.when(i >= nused_ref[0])
    def _():
        o_ref[...] = jnp.zeros(o_ref.shape, o_ref.dtype)


def _expert(block_expert, n_used, x_pad, wg, bg, wu, bu, wd, bd):
    n_pad, d = x_pad.shape
    f = wg.shape[2]
    nblk = n_pad // MOE_ROWS
    wspec = lambda a, b: pl.BlockSpec((1, a, b), lambda i, be, nu: (be[i], 0, 0))
    grid_spec = pltpu.PrefetchScalarGridSpec(
        num_scalar_prefetch=2,
        grid=(nblk,),
        in_specs=[pl.BlockSpec((MOE_ROWS, d), lambda i, be, nu: (i, 0)),
                  wspec(d, f), wspec(1, f), wspec(d, f), wspec(1, f), wspec(f, d), wspec(1, d)],
        out_specs=pl.BlockSpec((MOE_ROWS, d), lambda i, be, nu: (i, 0)),
        scratch_shapes=[pltpu.VMEM((d, f), BF16), pltpu.VMEM((d, f), BF16),
                        pltpu.VMEM((f, d), BF16)])
    return pl.pallas_call(
        _expert_kernel,
        grid_spec=grid_spec,
        out_shape=jax.ShapeDtypeStruct((n_pad, d), F32),
        compiler_params=_cparams(("arbitrary",)),
        name="expert",
    )(block_expert, n_used, x_pad, wg, bg.reshape(N_EXPERTS, 1, f), wu,
      bu.reshape(N_EXPERTS, 1, f), wd, bd.reshape(N_EXPERTS, 1, d))


def _combine_kernel(dest_ref, y_hbm, x1_ref, wts_ref, g2_ref, nf_ref, o_ref, ybuf, sem, *,
                    tm, final):
    def row_copy(tt, k):
        d = dest_ref[tt * TOP_K + k]
        return pltpu.make_async_copy(y_hbm.at[pl.ds(d, 1)], ybuf.at[k, pl.ds(tt, 1)], sem)

    def issue(tt, carry):
        for k in range(TOP_K):
            row_copy(tt, k).start()
        return carry

    lax.fori_loop(0, tm, issue, 0)
    for k in range(TOP_K):
        pltpu.make_async_copy(y_hbm.at[pl.ds(0, tm)], ybuf.at[k], sem).wait()

    w = wts_ref[...]
    moe = w[:, 0:1] * ybuf[0]
    for k in range(1, TOP_K):
        moe = moe + w[:, k:k + 1] * ybuf[k]
    x2 = x1_ref[...] + g2_ref[0] * moe
    o_ref[...] = _rms(x2) * nf_ref[...] if final else x2


def _combine(dest_tok, y_pad, x1, wts_tok, g2, nf, seq, tm, final):
    t, d = x1.shape
    tpb = seq // tm
    return pl.pallas_call(
        functools.partial(_combine_kernel, tm=tm, final=final),
        grid=(t // tm,),
        in_specs=[pl.BlockSpec((tm * TOP_K,), lambda i: (i,), memory_space=pltpu.SMEM),
                  pl.BlockSpec(memory_space=pl.ANY),
                  pl.BlockSpec((tm, d), lambda i: (i, 0)),
                  pl.BlockSpec((tm, TOP_K), lambda i: (i, 0)),
                  pl.BlockSpec((1, 1, d), lambda i: (i // tpb, 0, 0)),
                  pl.BlockSpec((1, d), lambda i: (0, 0))],
        out_specs=pl.BlockSpec((tm, d), lambda i: (i, 0)),
        out_shape=jax.ShapeDtypeStruct((t, d), F32),
        scratch_shapes=[pltpu.VMEM((TOP_K, tm, d), F32), pltpu.SemaphoreType.DMA],
        compiler_params=_cparams(("arbitrary",)),
        name="combine",
    )(dest_tok, y_pad, x1, wts_tok, g2, nf.reshape(1, d))


def _pick(n, cands):
    for c in cands:
        if n % c == 0:
            return c
    raise ValueError(f"no tile in {cands} divides {n}")


def kernel(x, c, rel_bias, w_ada, b_ada, norm_mix, w_in, w_gk_up, b_gk, gla_norm,
           w_proj_moba, w_proj_gla, w_out, norm_ffn, w_router, b_router,
           w_gate, b_gate, w_up, b_up, w_down, b_down, norm_final):
    bsz, seq, d = x.shape
    depth = w_ada.shape[0]
    assert d == D_MODEL and seq % MOBA_BLOCK == 0 and seq // MOBA_BLOCK <= MOBA_MAX_BLOCKS
    t = bsz * seq
    tm = _pick(seq, (512, 256))
    nchunk = _pick(seq // GLA_CHUNK, (8, 4))
    n_blk = seq // MOBA_BLOCK
    x2d = x.reshape(t, d)
    bias = _bias_tiles(rel_bias)
    onehot_blk = jnp.asarray(
        np.eye(MOBA_MAX_BLOCKS, LANES - MOBA_HEAD_DIM, dtype=np.float32)[
            np.arange(n_blk)][:, None, :]).astype(BF16)
    per_b = lambda v: v.reshape(bsz, 1, d)

    for l in range(depth):
        mod = _ada(c, w_ada[l], b_ada[l])
        sh1, sc1, g1, sh2, sc2, g2 = [per_b(m) for m in jnp.split(mod, 6, axis=-1)]
        w = w_in[l]
        o_gk = _W_QKVA + _W_QKB + GLA_VALUE_DIM
        w_cat = jnp.concatenate(
            [w[:, :o_gk], jnp.pad(w[:, o_gk:o_gk + GLA_GATE_RANK],
                                  ((0, 0), (0, LANES - GLA_GATE_RANK))),
             w[:, o_gk + GLA_GATE_RANK:]], axis=1).astype(BF16)
        qkva, qkb, vb, gk, r_act, gates = _inproj(x2d, norm_mix[l], sc1, sh1, w_cat, seq, tm)
        heads = qkva.reshape(bsz, n_blk, MOBA_BLOCK, 3, MOBA_HEADS, MOBA_HEAD_DIM)
        qt = heads[:, :, :, 0].transpose(0, 3, 4, 1, 2).reshape(
            bsz, MOBA_HEADS, MOBA_HEAD_DIM, seq)
        k_blocks = heads[:, :, :, 1].transpose(0, 3, 1, 2, 4)
        k_aug = jnp.concatenate(
            [k_blocks, jnp.broadcast_to(onehot_blk, k_blocks.shape)], axis=-1)
        vt = heads[:, :, :, 2].transpose(0, 3, 1, 4, 2)
        ones_rows = jnp.zeros((MOBA_VT_ROWS - MOBA_HEAD_DIM, MOBA_BLOCK), BF16).at[0].set(1.0)
        vt = jnp.concatenate(
            [vt, jnp.broadcast_to(ones_rows, vt.shape[:3] + ones_rows.shape)], axis=3)
        ya = _moba(qt, k_aug, vt, bias)
        wup = jnp.pad(w_gk_up[l], ((0, LANES - GLA_GATE_RANK), (0, 0)))
        yb = _gla(qkb, vb, gk, wup, b_gk[l].reshape(1, -1), r_act,
                  gla_norm[l].reshape(1, -1), bsz, seq, nchunk)
        x1, h2, idx, wts = _merge(
            ya, yb, gates, x2d, w_proj_moba[l].astype(BF16), w_proj_gla[l].astype(BF16),
            w_out[l].astype(BF16), g1, norm_ffn[l], sc2, sh2,
            w_router[l].T, b_router[l].reshape(N_EXPERTS, 1), seq, tm)
        rank, cnt = _rank(idx, tm)
        counts = cnt[:, 0].astype(I32)
        padded = (counts + MOE_ROWS - 1) // MOE_ROWS * MOE_ROWS
        pcum = jnp.cumsum(padded)
        pstart = (pcum - padded).astype(I32)
        n_blocks = (t * TOP_K + MOE_ROWS - 1) // MOE_ROWS + N_EXPERTS
        block_row0 = jnp.arange(n_blocks, dtype=I32) * MOE_ROWS
        block_expert = jnp.minimum(
            jnp.sum((pcum[None, :] <= block_row0[:, None]).astype(I32), axis=1),
            N_EXPERTS - 1).astype(I32)
        n_used = (pcum[-1:] // MOE_ROWS).astype(I32)
        dest = _dest(pstart, idx, rank, tm)
        dest_tok = dest.T.reshape(t * TOP_K)
        tg = _pick(t, (256,))
        x_pad = _scatter(dest_tok, h2, n_blocks * MOE_ROWS, tg)
        y_pad = _expert(block_expert, n_used, x_pad, w_gate[l], b_gate[l], w_up[l], b_up[l],
                        w_down[l], b_down[l])
        x2d = _combine(dest_tok, y_pad, x1, wts.T, g2, norm_final, seq, tg, l == depth - 1)
    return x2d.reshape(bsz, seq, d)
```

```python
import functools
import math

import numpy as np
import jax
import jax.numpy as jnp
from jax import lax
from jax.experimental import pallas as pl
from jax.experimental.pallas import tpu as pltpu

F32 = jnp.float32
BF16 = jnp.bfloat16
I32 = jnp.int32
HIGHEST = lax.Precision.HIGHEST

D_MODEL = 1024
MOBA_HEADS = 8
MOBA_HEAD_DIM = 64
MOBA_WIDTH = MOBA_HEADS * MOBA_HEAD_DIM
MOBA_BLOCK = 256
MOBA_TOPK = 3
MOBA_MAX_BLOCKS = 32
REL_BUCKETS = 32
REL_MAX_DIST = 128
GLA_HEADS = 4
GLA_KEY_DIM = D_MODEL // 2
GLA_VALUE_DIM = D_MODEL
GLA_DK = GLA_KEY_DIM // GLA_HEADS
GLA_DV = GLA_VALUE_DIM // GLA_HEADS
GLA_GATE_RANK = 16
GLA_GATE_NORMALIZER = 16.0
GLA_CHUNK = 64
N_EXPERTS = 32
TOP_K = 4
D_FF = D_MODEL
SWIGLU_ALPHA = 1.702
SWIGLU_LIMIT = 7.0
MOE_ROWS = 512
EPS = 1e-6
LANES = 128
NEG_BIG = -1e30
LOG2E = math.log2(math.e)
VMEM_LIMIT = 56 * 1024 * 1024


def _cparams(sem, vmem=None):
    return pltpu.CompilerParams(dimension_semantics=sem,
                                vmem_limit_bytes=vmem or VMEM_LIMIT)


def _nt_dot(a, b, **kw):
    return lax.dot_general(a, b, (((1,), (1,)), ((), ())),
                           preferred_element_type=F32, **kw)


def _rms(x):
    return x * lax.rsqrt(jnp.mean(x * x, axis=-1, keepdims=True) + EPS)


def _sigmoid(x):
    return 1.0 / (1.0 + jnp.exp(-x))


def _ada_kernel(c_ref, w_ref, b_ref, o_ref):
    c = c_ref[...]
    s = c * _sigmoid(c)
    o_ref[...] = jnp.dot(s, w_ref[...], precision=HIGHEST,
                         preferred_element_type=F32) + b_ref[...]


def _ada(c, w, b):
    bsz, d = c.shape
    n = w.shape[1]
    rows = -(-bsz // 8) * 8
    cp = jnp.zeros((rows, d), F32).at[:bsz].set(c)
    tn = 768
    out = pl.pallas_call(
        _ada_kernel,
        grid=(n // tn,),
        in_specs=[pl.BlockSpec((rows, d), lambda j: (0, 0)),
                  pl.BlockSpec((d, tn), lambda j: (0, j)),
                  pl.BlockSpec((1, tn), lambda j: (0, j))],
        out_specs=pl.BlockSpec((rows, tn), lambda j: (0, j)),
        out_shape=jax.ShapeDtypeStruct((rows, n), F32),
        compiler_params=_cparams(("arbitrary",)),
        name="ada",
    )(cp, w, b.reshape(1, n))
    return out[:bsz]


_OFF_QA = 0
_OFF_KA = _OFF_QA + MOBA_WIDTH
_OFF_VA = _OFF_KA + MOBA_HEADS * LANES
_OFF_QKB = _OFF_VA + MOBA_WIDTH
_OFF_VB = _OFF_QKB + 2 * GLA_KEY_DIM
_OFF_GK = _OFF_VB + GLA_VALUE_DIM
_OFF_R = _OFF_GK + LANES
_OFF_G = _OFF_R + GLA_VALUE_DIM
_W_CAT = _OFF_G + 2 * D_MODEL


def _regroup_w_in(w):
    d = w.shape[0]
    hd = MOBA_HEAD_DIM
    o_k, o_v = MOBA_WIDTH, 2 * MOBA_WIDTH
    o_qkb = 3 * MOBA_WIDTH
    o_gk = o_qkb + 2 * GLA_KEY_DIM + GLA_VALUE_DIM
    k_heads = w[:, o_k:o_v].reshape(d, MOBA_HEADS, hd)
    k_groups = jnp.pad(k_heads, ((0, 0), (0, 0), (0, LANES - hd))).reshape(d, -1)
    gk = jnp.pad(w[:, o_gk:o_gk + GLA_GATE_RANK], ((0, 0), (0, LANES - GLA_GATE_RANK)))
    return jnp.concatenate(
        [w[:, :o_k], k_groups, w[:, o_v:o_qkb], w[:, o_qkb:o_gk], gk,
         w[:, o_gk + GLA_GATE_RANK:]], axis=1).astype(BF16)


def _inproj_kernel(x_ref, nw_ref, sc_ref, sh_ref, w_ref,
                   qt_ref, ka_ref, vt_ref, qkb_ref, vb_ref, gk_ref, r_ref, g_ref, *, tpb):
    tm = x_ref.shape[0]
    hd = MOBA_HEAD_DIM
    nbt = tm // MOBA_BLOCK
    h = _rms(x_ref[...]) * nw_ref[...]
    h = h * (1.0 + sc_ref[0]) + sh_ref[0]
    hb = h.astype(BF16)

    def mm(a, b):
        return jnp.dot(hb, w_ref[:, a:b], preferred_element_type=F32)

    q_t = (mm(_OFF_QA, _OFF_KA) * (hd ** -0.5 * LOG2E)).T
    v_t = mm(_OFF_VA, _OFF_QKB).T
    blk0 = (pl.program_id(0) % tpb) * nbt
    lane = lax.broadcasted_iota(I32, (MOBA_BLOCK, LANES), 1)
    ones_rows = jnp.where(
        lax.broadcasted_iota(I32, (MOBA_VT_ROWS - hd, MOBA_BLOCK), 0) == 0, 1.0, 0.0)
    for hh in range(MOBA_HEADS):
        qt_ref[0, hh] = q_t[hh * hd:(hh + 1) * hd].astype(BF16)
        k_h = mm(_OFF_KA + hh * LANES, _OFF_KA + (hh + 1) * LANES)
        for j in range(nbt):
            rows = slice(j * MOBA_BLOCK, (j + 1) * MOBA_BLOCK)
            ka_ref[0, hh, j] = jnp.where(lane == hd + blk0 + j, 1.0, k_h[rows]).astype(BF16)
            vt_ref[0, hh, j, 0:hd, :] = v_t[hh * hd:(hh + 1) * hd, rows].astype(BF16)
            vt_ref[0, hh, j, hd:MOBA_VT_ROWS, :] = ones_rows.astype(BF16)
    qkb_ref[...] = mm(_OFF_QKB, _OFF_VB).astype(BF16)
    vb_ref[...] = mm(_OFF_VB, _OFF_GK).astype(BF16)
    gk_ref[...] = mm(_OFF_GK, _OFF_R)
    r = mm(_OFF_R, _OFF_G)
    r_ref[...] = (r * _sigmoid(r)).astype(BF16)
    g_ref[...] = _sigmoid(mm(_OFF_G, _W_CAT)).astype(BF16)


def _inproj(x2d, nw, sc, sh, w_cat, bsz, seq, tm):
    t, d = x2d.shape
    tpb = seq // tm
    nbt = tm // MOBA_BLOCK
    nh, hd = MOBA_HEADS, MOBA_HEAD_DIM
    row = lambda w: pl.BlockSpec((tm, w), lambda i: (i, 0))
    per_b = pl.BlockSpec((1, 1, d), lambda i: (i // tpb, 0, 0))
    rows_out = [(2 * GLA_KEY_DIM, BF16), (GLA_VALUE_DIM, BF16), (LANES, F32),
                (GLA_VALUE_DIM, BF16), (2 * D_MODEL, BF16)]
    return pl.pallas_call(
        functools.partial(_inproj_kernel, tpb=tpb),
        grid=(t // tm,),
        in_specs=[row(d), pl.BlockSpec((1, d), lambda i: (0, 0)), per_b, per_b,
                  pl.BlockSpec((d, _W_CAT), lambda i: (0, 0), pipeline_mode=pl.Buffered(1))],
        out_specs=[pl.BlockSpec((1, nh, hd, tm), lambda i: (i // tpb, 0, 0, i % tpb)),
                   pl.BlockSpec((1, nh, nbt, MOBA_BLOCK, LANES),
                                lambda i: (i // tpb, 0, i % tpb, 0, 0)),
                   pl.BlockSpec((1, nh, nbt, MOBA_VT_ROWS, MOBA_BLOCK),
                                lambda i: (i // tpb, 0, i % tpb, 0, 0))]
                  + [row(w) for w, _ in rows_out],
        out_shape=[jax.ShapeDtypeStruct((bsz, nh, hd, seq), BF16),
                   jax.ShapeDtypeStruct((bsz, nh, seq // MOBA_BLOCK, MOBA_BLOCK, LANES), BF16),
                   jax.ShapeDtypeStruct((bsz, nh, seq // MOBA_BLOCK, MOBA_VT_ROWS, MOBA_BLOCK),
                                        BF16)]
                  + [jax.ShapeDtypeStruct((t, w), dt) for w, dt in rows_out],
        compiler_params=_cparams(("arbitrary",)),
        name="inproj",
    )(x2d, nw.reshape(1, d), sc, sh, w_cat)


def _t5_bucket_np(n):
    n = np.maximum(n, 0)
    max_exact = REL_BUCKETS // 2
    nf = np.maximum(n, max_exact).astype(np.float32)
    large = max_exact + (np.log(nf / max_exact) / math.log(REL_MAX_DIST / max_exact)
                         * (REL_BUCKETS - max_exact)).astype(np.int32)
    large = np.minimum(large, REL_BUCKETS - 1)
    return np.where(n < max_exact, n, large).astype(np.int32)


def _bucket_table():
    kj = np.arange(MOBA_BLOCK)[:, None]
    qi = np.arange(2 * MOBA_BLOCK)[None, :] % MOBA_BLOCK
    prev = np.arange(2 * MOBA_BLOCK)[None, :] < MOBA_BLOCK
    return _t5_bucket_np(qi - kj + np.where(prev, MOBA_BLOCK, 0))


def _bias_kernel(rb_ref, bucket_ref, o_ref):
    h = pl.program_id(0)
    bk = bucket_ref[...]
    far = rb_ref[(REL_BUCKETS - 1) * MOBA_HEADS + h]
    acc = jnp.zeros(bk.shape, F32)
    for b in range(REL_BUCKETS):
        acc = jnp.where(bk == b, rb_ref[b * MOBA_HEADS + h] - far, acc)
    o_ref[0] = acc * LOG2E


def _bias_tiles(rel_bias):
    bucket = jnp.asarray(_bucket_table())
    return pl.pallas_call(
        _bias_kernel,
        grid=(MOBA_HEADS,),
        in_specs=[pl.BlockSpec(memory_space=pltpu.SMEM),
                  pl.BlockSpec(bucket.shape, lambda h: (0, 0))],
        out_specs=pl.BlockSpec((1,) + bucket.shape, lambda h: (h, 0, 0)),
        out_shape=jax.ShapeDtypeStruct((MOBA_HEADS,) + bucket.shape, F32),
        compiler_params=_cparams(("arbitrary",)),
        name="bias",
    )(rel_bias.reshape(-1), bucket)


MOBA_HEADS_PER_STEP = 8
MOBA_FAR_GROUP = 2
MOBA_FAR_LAGS = (0, 3, 6)
MOBA_VT_ROWS = MOBA_HEAD_DIM + 16


def _moba_kernel(qt_ref, k_ref, vt_ref, bias_ref, o_ref, kmean_ref, qa_ref):
    blk = MOBA_BLOCK
    nb = MOBA_MAX_BLOCKS
    hd = MOBA_HEAD_DIM
    hp = MOBA_HEADS_PER_STEP
    grp = MOBA_FAR_GROUP
    qi = pl.program_id(2)
    neg = -jnp.inf

    @pl.when(qi == 0)
    def _():
        kmean_ref[...] = jnp.zeros(kmean_ref.shape, F32)

    row = lax.broadcasted_iota(I32, (nb, blk), 0)
    rowf = row.astype(F32)
    key_i = lax.broadcasted_iota(I32, (blk, blk), 0)
    qry_i = lax.broadcasted_iota(I32, (blk, blk), 1)
    pad = jnp.zeros((LANES - hd, blk), BF16)
    pad_hi = jnp.zeros((LANES - hd - nb, blk), BF16)
    prev_j = jnp.maximum(qi - 1, 0)

    def skewed(stages, lags):
        vals = [None] * hp
        for step in range(hp + lags[-1]):
            for stage, lag in zip(stages, lags):
                h = step - lag
                if 0 <= h < hp:
                    vals[h] = stage(h, vals[h])
        return vals

    def select(h, _):
        qt = qt_ref[0, h]
        gate = jnp.dot(kmean_ref[h, :, 0:hd], qt.astype(F32), precision=HIGHEST,
                       preferred_element_type=F32)
        g = jnp.where(row < qi, gate, neg)
        sel = jnp.zeros((nb, blk), F32)
        for _ in range(MOBA_TOPK):
            mx = jnp.max(g, axis=0, keepdims=True)
            first = jnp.min(jnp.where(g == mx, rowf, float(nb)), axis=0, keepdims=True)
            pick = rowf == jnp.where(mx > neg, first, -1.0)
            sel = jnp.where(pick, 1.0, sel)
            g = jnp.where(pick, neg, g)
        mask_prev = jnp.where(sel > 0.0, jnp.where(row == qi - 1, 0.0, NEG_BIG), NEG_BIG)
        mask_far = jnp.where(sel > 0.0, jnp.where(row < qi - 1, 0.0, NEG_BIG), NEG_BIG)
        qa_ref[h] = jnp.concatenate([qt, mask_far.astype(BF16), pad_hi], axis=0)
        return (jnp.concatenate([qt, pad], axis=0),
                jnp.concatenate([qt, mask_prev.astype(BF16), pad_hi], axis=0))

    def near_scores(h, qa):
        qa_own, qa_prev = qa
        s_own = jnp.dot(k_ref[0, h, qi], qa_own, preferred_element_type=F32)
        s_prev = jnp.dot(k_ref[0, h, prev_j], qa_prev, preferred_element_type=F32)
        return s_own, s_prev

    def near_softmax(h, ss):
        s_own, s_prev = ss
        s_own = jnp.where(key_i <= qry_i, s_own + bias_ref[h, :, blk:2 * blk], neg)
        s = jnp.concatenate([s_own, s_prev + bias_ref[h, :, 0:blk]], axis=0)
        m0 = jnp.max(s, axis=0, keepdims=True)
        return m0, jnp.exp2(s - m0)

    def near_pv(h, mp):
        m0, p = mp
        pb = p.astype(BF16)
        acc = (jnp.dot(vt_ref[0, h, qi], pb[0:blk], preferred_element_type=F32)
               + jnp.dot(vt_ref[0, h, prev_j], pb[blk:2 * blk], preferred_element_type=F32))
        return m0, acc

    states = tuple(skewed([select, near_scores, near_softmax, near_pv], (0, 1, 2, 3)))

    def far(gi, states):
        j0 = gi * grp

        def qk(h, _):
            kt = k_ref[0, h, pl.ds(j0, grp)].reshape(grp * blk, LANES)
            return jnp.dot(kt, qa_ref[h], preferred_element_type=F32)

        def softmax(h, s):
            m_old = states[h][0]
            m_new = jnp.maximum(m_old, jnp.max(s, axis=0, keepdims=True))
            return m_new, jnp.exp2(m_old - m_new), jnp.exp2(s - m_new)

        def pv(h, sm):
            m_new, a, p = sm
            pb = p.astype(BF16)
            tot = a * states[h][1]
            for i in range(grp):
                tot = tot + jnp.dot(vt_ref[0, h, j0 + i], pb[i * blk:(i + 1) * blk],
                                    preferred_element_type=F32)
            return m_new, tot

        return tuple(skewed([qk, softmax, pv], MOBA_FAR_LAGS))

    states = lax.fori_loop(0, (prev_j + grp - 1) // grp, far, tuple(states))

    for h in range(hp):
        acc = states[h][1]
        o_ref[0, h * hd:(h + 1) * hd, :] = (acc[0:hd] / acc[hd:hd + 1]).astype(o_ref.dtype)
        kmean_ref[h, pl.ds(qi, 1), :] = jnp.mean(k_ref[0, h, qi].astype(F32), axis=0,
                                                 keepdims=True)


def _moba(qt, k_aug, vt, bias):
    bsz, nh, hd, s = qt.shape
    blk = MOBA_BLOCK
    hp = MOBA_HEADS_PER_STEP
    nblk = s // blk
    assert nh % hp == 0 and (nblk % MOBA_FAR_GROUP == 0 or nblk == 1)
    return pl.pallas_call(
        _moba_kernel,
        grid=(bsz, nh // hp, nblk),
        in_specs=[pl.BlockSpec((1, hp, hd, blk), lambda b, g, i: (b, g, 0, i)),
                  pl.BlockSpec((1, hp, nblk, blk, LANES), lambda b, g, i: (b, g, 0, 0, 0),
                               pipeline_mode=pl.Buffered(1)),
                  pl.BlockSpec((1, hp, nblk, MOBA_VT_ROWS, blk), lambda b, g, i: (b, g, 0, 0, 0),
                               pipeline_mode=pl.Buffered(1)),
                  pl.BlockSpec((hp, blk, 2 * blk), lambda b, g, i: (g, 0, 0),
                               pipeline_mode=pl.Buffered(1))],
        out_specs=pl.BlockSpec((1, hp * hd, blk), lambda b, g, i: (b, g, i)),
        out_shape=jax.ShapeDtypeStruct((bsz, nh * hd, s), BF16),
        scratch_shapes=[pltpu.VMEM((hp, MOBA_MAX_BLOCKS, LANES), F32),
                        pltpu.VMEM((hp, LANES, blk), BF16)],
        compiler_params=_cparams(("arbitrary", "arbitrary", "arbitrary")),
        name="moba",
    )(qt, k_aug, vt, bias)


def _gla_kernel(q_ref, k_ref, v_ref, gk_ref, wup_ref, bgk_ref, r_ref, gn_ref, o_ref,
                state_ref, *, nchunk):
    ch = GLA_CHUNK
    tc = nchunk * ch

    @pl.when(pl.program_id(2) == 0)
    def _():
        state_ref[...] = jnp.zeros(state_ref.shape, F32)

    z = jnp.dot(gk_ref[...], wup_ref[...], precision=HIGHEST,
                preferred_element_type=F32) + bgk_ref[...]
    log_a = (jnp.minimum(z, 0.0) - jnp.log(1.0 + jnp.exp(-jnp.abs(z)))) / GLA_GATE_NORMALIZER
    rin = lax.broadcasted_iota(I32, (tc, GLA_DK), 0) & (ch - 1)
    b = log_a
    sh = 1
    while sh < ch:
        b = b + jnp.where(rin >= sh, pltpu.roll(b, sh, axis=0), 0.0)
        sh *= 2
    q = q_ref[...].astype(F32) * (GLA_DK ** -0.5)
    k = k_ref[...].astype(F32)
    q_g = (q * jnp.exp(b)).astype(BF16)
    k_g = (k * jnp.exp(-b)).astype(BF16)
    causal = (lax.broadcasted_iota(I32, (ch, ch), 1) <= lax.broadcasted_iota(I32, (ch, ch), 0))
    eye = (lax.broadcasted_iota(I32, (GLA_DK, GLA_DK), 0)
           == lax.broadcasted_iota(I32, (GLA_DK, GLA_DK), 1))
    b3 = b.reshape(nchunk, ch, GLA_DK)
    b_last = b3[:, ch - 1:ch, :]
    k_end = (k * jnp.exp(jnp.broadcast_to(b_last, b3.shape) - b3).reshape(tc, GLA_DK)
             ).astype(BF16)
    decay = jnp.exp(b_last)
    chunks = [slice(n * ch, (n + 1) * ch) for n in range(nchunk)]
    o_intra, kv, decay_col = [], [], []
    for n, sl in enumerate(chunks):
        v_c = v_ref[sl, :]
        att = jnp.where(causal, _nt_dot(q_g[sl], k_g[sl]), 0.0)
        o_intra.append(jnp.dot(att.astype(BF16), v_c, preferred_element_type=F32))
        kv.append(lax.dot_general(k_end[sl], v_c, (((0,), (0,)), ((), ())),
                                  preferred_element_type=F32))
        decay_col.append(jnp.sum(
            jnp.where(eye, jnp.broadcast_to(decay[n], (GLA_DK, GLA_DK)), 0.0),
            axis=1, keepdims=True))
    state = state_ref[...]
    o_inter = []
    for n, sl in enumerate(chunks):
        o_inter.append(jnp.dot(q_g[sl], state.astype(BF16), preferred_element_type=F32))
        state = decay_col[n] * state + kv[n]
    state_ref[...] = state
    o = jnp.concatenate([a + c for a, c in zip(o_intra, o_inter)], axis=0)
    o_ref[...] = (_rms(o) * gn_ref[...] * r_ref[...].astype(F32)).astype(o_ref.dtype)


def _gla(qkb, vb, gk, wup, bgk, r_act, gn, bsz, seq, nchunk):
    t = qkb.shape[0]
    tc = nchunk * GLA_CHUNK
    nc = seq // tc
    rowblk = lambda w, off: pl.BlockSpec((tc, w), lambda b, h, c: (b * nc + c, h + off))
    return pl.pallas_call(
        functools.partial(_gla_kernel, nchunk=nchunk),
        grid=(bsz, GLA_HEADS, nc),
        in_specs=[rowblk(GLA_DK, 0), rowblk(GLA_DK, GLA_HEADS), rowblk(GLA_DV, 0),
                  pl.BlockSpec((tc, LANES), lambda b, h, c: (b * nc + c, 0)),
                  pl.BlockSpec((LANES, GLA_DK), lambda b, h, c: (0, h)),
                  pl.BlockSpec((1, GLA_DK), lambda b, h, c: (0, h)),
                  rowblk(GLA_DV, 0),
                  pl.BlockSpec((1, GLA_DV), lambda b, h, c: (0, 0))],
        out_specs=rowblk(GLA_DV, 0),
        out_shape=jax.ShapeDtypeStruct((t, GLA_VALUE_DIM), BF16),
        scratch_shapes=[pltpu.VMEM((GLA_DK, GLA_DV), F32)],
        compiler_params=_cparams(("arbitrary", "arbitrary", "arbitrary")),
        name="gla",
    )(qkb, qkb, vb, gk, wup, bgk, r_act, gn)


def _merge_kernel(ya_ref, yb_ref, g_ref, x_ref, wpa_ref, wpb_ref, wout_ref, g1_ref,
                  nw_ref, sc_ref, sh_ref, wr_ref, br_ref,
                  x1_ref, h2_ref, idx_ref, wts_ref):
    pa = lax.dot_general(ya_ref[0], wpa_ref[...], (((0,), (0,)), ((), ())),
                         preferred_element_type=F32)
    pb = jnp.dot(yb_ref[...], wpb_ref[...], preferred_element_type=F32)
    mixed = (g_ref[:, 0:D_MODEL].astype(F32) * pa
             + g_ref[:, D_MODEL:2 * D_MODEL].astype(F32) * pb)
    y = jnp.dot(mixed.astype(BF16), wout_ref[...], preferred_element_type=F32)
    x1 = x_ref[...] + g1_ref[0] * y
    x1_ref[...] = x1
    h2 = _rms(x1) * nw_ref[...]
    h2 = h2 * (1.0 + sc_ref[0]) + sh_ref[0]
    h2_ref[...] = h2
    logits = _nt_dot(wr_ref[...], h2, precision=HIGHEST) + br_ref[...]
    rowf = lax.broadcasted_iota(I32, logits.shape, 0).astype(F32)
    vals, idxs = [], []
    cur = logits
    for _ in range(TOP_K):
        mx = jnp.max(cur, axis=0, keepdims=True)
        first = jnp.min(jnp.where(cur == mx, rowf, float(N_EXPERTS)), axis=0, keepdims=True)
        vals.append(mx)
        idxs.append(first)
        cur = jnp.where(rowf == first, -jnp.inf, cur)
    es = [jnp.exp(v - vals[0]) for v in vals]
    tot = es[0]
    for e in es[1:]:
        tot = tot + e
    idx_ref[...] = jnp.concatenate(idxs, axis=0).astype(I32)
    wts_ref[...] = jnp.concatenate([e / tot for e in es], axis=0)


def _merge(ya, yb, gates, x2d, wpa, wpb, wout, g1, nw, sc, sh, wr_t, br, seq, tm):
    t, d = x2d.shape
    tpb = seq // tm
    row = lambda w: pl.BlockSpec((tm, w), lambda i: (i, 0))
    full = lambda a: pl.BlockSpec(a.shape, lambda i: (0,) * a.ndim)
    per_b = pl.BlockSpec((1, 1, d), lambda i: (i // tpb, 0, 0))
    colblk = pl.BlockSpec((TOP_K, tm), lambda i: (0, i))
    return pl.pallas_call(
        _merge_kernel,
        grid=(t // tm,),
        in_specs=[pl.BlockSpec((1, MOBA_WIDTH, tm), lambda i: (i // tpb, 0, i % tpb)),
                  row(GLA_VALUE_DIM), row(2 * D_MODEL), row(d),
                  full(wpa), full(wpb), full(wout), per_b,
                  pl.BlockSpec((1, d), lambda i: (0, 0)), per_b, per_b,
                  full(wr_t), full(br)],
        out_specs=[row(d), row(d), colblk, colblk],
        out_shape=[jax.ShapeDtypeStruct((t, d), F32), jax.ShapeDtypeStruct((t, d), F32),
                   jax.ShapeDtypeStruct((TOP_K, t), I32), jax.ShapeDtypeStruct((TOP_K, t), F32)],
        compiler_params=_cparams(("arbitrary",)),
        name="merge",
    )(ya, yb, gates, x2d, wpa, wpb, wout, g1, nw.reshape(1, d), sc, sh, wr_t, br)


def _rank_kernel(idx_ref, rank_ref, cnt_ref, carry_ref):
    tm = idx_ref.shape[1]

    @pl.when(pl.program_id(0) == 0)
    def _():
        carry_ref[...] = jnp.zeros(carry_ref.shape, F32)

    rows = lax.broadcasted_iota(I32, (N_EXPERTS, tm), 0)
    before = (lax.broadcasted_iota(I32, (tm, tm), 0)
              < lax.broadcasted_iota(I32, (tm, tm), 1))
    upper = jnp.where(before, 1.0, 0.0).astype(BF16)
    carry = carry_ref[:, 0:1]
    ranks = []
    for k in range(TOP_K):
        onehot = idx_ref[k:k + 1, :] == rows
        onef = jnp.where(onehot, 1.0, 0.0)
        earlier = jnp.dot(onef.astype(BF16), upper, preferred_element_type=F32) + carry
        ranks.append(jnp.sum(jnp.where(onehot, earlier, 0.0), axis=0, keepdims=True))
        carry = carry + jnp.sum(onef, axis=1, keepdims=True)
    rank_ref[...] = jnp.concatenate(ranks, axis=0).astype(I32)
    total = jnp.broadcast_to(carry, carry_ref.shape)
    carry_ref[...] = total
    cnt_ref[...] = total


def _rank(idx, tm):
    t = idx.shape[1]
    return pl.pallas_call(
        _rank_kernel,
        grid=(t // tm,),
        in_specs=[pl.BlockSpec((TOP_K, tm), lambda i: (0, i))],
        out_specs=[pl.BlockSpec((TOP_K, tm), lambda i: (0, i)),
                   pl.BlockSpec((N_EXPERTS, LANES), lambda i: (0, 0))],
        out_shape=[jax.ShapeDtypeStruct((TOP_K, t), I32),
                   jax.ShapeDtypeStruct((N_EXPERTS, LANES), F32)],
        scratch_shapes=[pltpu.VMEM((N_EXPERTS, LANES), F32)],
        compiler_params=_cparams(("arbitrary",)),
        name="rank",
    )(idx)


def _dest_kernel(pstart_ref, idx_ref, rank_ref, dest_ref):
    idx = idx_ref[...]
    off = jnp.zeros(idx.shape, I32)
    for e in range(N_EXPERTS):
        off = jnp.where(idx == e, pstart_ref[e], off)
    dest_ref[...] = rank_ref[...] + off


def _dest(pstart, idx, rank, tm):
    t = idx.shape[1]
    blk = pl.BlockSpec((TOP_K, tm), lambda i: (0, i))
    return pl.pallas_call(
        _dest_kernel,
        grid=(t // tm,),
        in_specs=[pl.BlockSpec(memory_space=pltpu.SMEM), blk, blk],
        out_specs=blk,
        out_shape=jax.ShapeDtypeStruct((TOP_K, t), I32),
        compiler_params=_cparams(("arbitrary",)),
        name="dest",
    )(pstart, idx, rank)


def _scatter_kernel(dest_ref, zrow_ref, h_ref, xout_hbm, zbuf, sem, zsem, *, tm):
    @pl.when(pl.program_id(0) == 0)
    def _():
        zbuf[...] = jnp.zeros(zbuf.shape, zbuf.dtype)

        def zero_rows(row0):
            row0 = pl.multiple_of(row0, MOE_ROWS)
            return pltpu.make_async_copy(zbuf, xout_hbm.at[pl.ds(row0, MOE_ROWS)], zsem)

        for e in range(N_EXPERTS):
            zero_rows(zrow_ref[e]).start()
        for e in range(N_EXPERTS):
            zero_rows(zrow_ref[e]).wait()

        def start_tail(j, carry):
            zero_rows(j * MOE_ROWS).start()
            return carry

        def wait_tail(j, carry):
            zero_rows(j * MOE_ROWS).wait()
            return carry

        n_all = xout_hbm.shape[0] // MOE_ROWS
        lax.fori_loop(zrow_ref[N_EXPERTS], n_all, start_tail, 0)
        lax.fori_loop(zrow_ref[N_EXPERTS], n_all, wait_tail, 0)

    def issue(tt, carry):
        for k in range(TOP_K):
            d = dest_ref[tt * TOP_K + k]
            pltpu.make_async_copy(h_ref.at[pl.ds(tt, 1)], xout_hbm.at[pl.ds(d, 1)], sem).start()
        return carry

    lax.fori_loop(0, tm, issue, 0)
    for k in range(TOP_K):
        pltpu.make_async_copy(h_ref, xout_hbm.at[pl.ds(0, tm)], sem).wait()


def _scatter(dest_tok, last_block_row, h2, n_pad, tm):
    t, d = h2.shape
    return pl.pallas_call(
        functools.partial(_scatter_kernel, tm=tm),
        grid=(t // tm,),
        in_specs=[pl.BlockSpec((tm * TOP_K,), lambda i: (i,), memory_space=pltpu.SMEM),
                  pl.BlockSpec(memory_space=pltpu.SMEM),
                  pl.BlockSpec((tm, d), lambda i: (i, 0))],
        out_specs=pl.BlockSpec(memory_space=pl.ANY),
        out_shape=jax.ShapeDtypeStruct((n_pad, d), h2.dtype),
        scratch_shapes=[pltpu.VMEM((MOE_ROWS, d), h2.dtype), pltpu.SemaphoreType.DMA,
                        pltpu.SemaphoreType.DMA],
        compiler_params=_cparams(("arbitrary",)),
        name="scatter",
    )(dest_tok, last_block_row, h2)


def _expert_kernel(be_ref, nused_ref, x_ref, wg_ref, bg_ref, wu_ref, bu_ref, wd_ref, bd_ref,
                   o_ref, wgb_ref, wub_ref, wdb_ref):
    i = pl.program_id(0)
    prev = be_ref[jnp.maximum(i - 1, 0)]

    @pl.when(jnp.logical_or(i == 0, be_ref[i] != prev))
    def _():
        wgb_ref[...] = wg_ref[0].astype(BF16)
        wub_ref[...] = wu_ref[0].astype(BF16)
        wdb_ref[...] = wd_ref[0].astype(BF16)

    @pl.when(i < nused_ref[0])
    def _():
        xb = x_ref[...].astype(BF16)
        g = jnp.minimum(jnp.dot(xb, wgb_ref[...], preferred_element_type=F32) + bg_ref[0],
                        SWIGLU_LIMIT)
        u = jnp.clip(jnp.dot(xb, wub_ref[...], preferred_element_type=F32) + bu_ref[0],
                     -SWIGLU_LIMIT, SWIGLU_LIMIT)
        act = g * _sigmoid(SWIGLU_ALPHA * g) * (u + 1.0)
        o_ref[...] = (jnp.dot(act.astype(BF16), wdb_ref[...], preferred_element_type=F32)
                      + bd_ref[0])

    @pl.when(i >= nused_ref[0])
    def _():
        o_ref[...] = jnp.zeros(o_ref.shape, o_ref.dtype)


def _expert(block_expert, n_used, x_pad, wg, bg, wu, bu, wd, bd):
    n_pad, d = x_pad.shape
    f = wg.shape[2]
    nblk = n_pad // MOE_ROWS
    wspec = lambda a, b: pl.BlockSpec((1, a, b), lambda i, be, nu: (be[i], 0, 0))
    grid_spec = pltpu.PrefetchScalarGridSpec(
        num_scalar_prefetch=2,
        grid=(nblk,),
        in_specs=[pl.BlockSpec((MOE_ROWS, d), lambda i, be, nu: (jnp.minimum(i, nu[0] - 1), 0)),
                  wspec(d, f), wspec(1, f), wspec(d, f), wspec(1, f), wspec(f, d), wspec(1, d)],
        out_specs=pl.BlockSpec((MOE_ROWS, d), lambda i, be, nu: (i, 0)),
        scratch_shapes=[pltpu.VMEM((d, f), BF16), pltpu.VMEM((d, f), BF16),
                        pltpu.VMEM((f, d), BF16)])
    return pl.pallas_call(
        _expert_kernel,
        grid_spec=grid_spec,
        out_shape=jax.ShapeDtypeStruct((n_pad, d), F32),
        compiler_params=_cparams(("arbitrary",)),
        name="expert",
    )(block_expert, n_used, x_pad, wg, bg.reshape(N_EXPERTS, 1, f), wu,
      bu.reshape(N_EXPERTS, 1, f), wd, bd.reshape(N_EXPERTS, 1, d))


def _combine_kernel(dest_ref, y_hbm, x1_ref, wts_ref, g2_ref, nf_ref, o_ref, ybuf, sem, *,
                    tm, final):
    def row_copy(tt, k):
        d = dest_ref[tt * TOP_K + k]
        return pltpu.make_async_copy(y_hbm.at[pl.ds(d, 1)], ybuf.at[k, pl.ds(tt, 1)], sem)

    def issue(tt, carry):
        for k in range(TOP_K):
            row_copy(tt, k).start()
        return carry

    lax.fori_loop(0, tm, issue, 0)
    for k in range(TOP_K):
        pltpu.make_async_copy(y_hbm.at[pl.ds(0, tm)], ybuf.at[k], sem).wait()

    w = wts_ref[...]
    moe = w[:, 0:1] * ybuf[0]
    for k in range(1, TOP_K):
        moe = moe + w[:, k:k + 1] * ybuf[k]
    x2 = x1_ref[...] + g2_ref[0] * moe
    o_ref[...] = _rms(x2) * nf_ref[...] if final else x2


def _combine(dest_tok, y_pad, x1, wts_tok, g2, nf, seq, tm, final):
    t, d = x1.shape
    tpb = seq // tm
    return pl.pallas_call(
        functools.partial(_combine_kernel, tm=tm, final=final),
        grid=(t // tm,),
        in_specs=[pl.BlockSpec((tm * TOP_K,), lambda i: (i,), memory_space=pltpu.SMEM),
                  pl.BlockSpec(memory_space=pl.ANY),
                  pl.BlockSpec((tm, d), lambda i: (i, 0)),
                  pl.BlockSpec((tm, TOP_K), lambda i: (i, 0)),
                  pl.BlockSpec((1, 1, d), lambda i: (i // tpb, 0, 0)),
                  pl.BlockSpec((1, d), lambda i: (0, 0))],
        out_specs=pl.BlockSpec((tm, d), lambda i: (i, 0)),
        out_shape=jax.ShapeDtypeStruct((t, d), F32),
        scratch_shapes=[pltpu.VMEM((TOP_K, tm, d), F32), pltpu.SemaphoreType.DMA],
        compiler_params=_cparams(("arbitrary",)),
        name="combine",
    )(dest_tok, y_pad, x1, wts_tok, g2, nf.reshape(1, d))


def _pick(n, cands):
    for c in cands:
        if n % c == 0:
            return c
    raise ValueError(f"no tile in {cands} divides {n}")


def kernel(x, c, rel_bias, w_ada, b_ada, norm_mix, w_in, w_gk_up, b_gk, gla_norm,
           w_proj_moba, w_proj_gla, w_out, norm_ffn, w_router, b_router,
           w_gate, b_gate, w_up, b_up, w_down, b_down, norm_final):
    bsz, seq, d = x.shape
    depth = w_ada.shape[0]
    assert d == D_MODEL and seq % MOBA_BLOCK == 0 and seq // MOBA_BLOCK <= MOBA_MAX_BLOCKS
    t = bsz * seq
    tm = _pick(seq, (512, 256))
    nchunk = _pick(seq // GLA_CHUNK, (8, 4))
    n_blk = seq // MOBA_BLOCK
    x2d = x.reshape(t, d)
    bias = _bias_tiles(rel_bias)
    per_b = lambda v: v.reshape(bsz, 1, d)

    for l in range(depth):
        mod = _ada(c, w_ada[l], b_ada[l])
        sh1, sc1, g1, sh2, sc2, g2 = [per_b(m) for m in jnp.split(mod, 6, axis=-1)]
        qt, k_aug, vt, qkb, vb, gk, r_act, gates = _inproj(
            x2d, norm_mix[l], sc1, sh1, _regroup_w_in(w_in[l]), bsz, seq, tm)
        ya = _moba(qt, k_aug, vt, bias)
        wup = jnp.pad(w_gk_up[l], ((0, LANES - GLA_GATE_RANK), (0, 0)))
        yb = _gla(qkb, vb, gk, wup, b_gk[l].reshape(1, -1), r_act,
                  gla_norm[l].reshape(1, -1), bsz, seq, nchunk)
        x1, h2, idx, wts = _merge(
            ya, yb, gates, x2d, w_proj_moba[l].astype(BF16), w_proj_gla[l].astype(BF16),
            w_out[l].astype(BF16), g1, norm_ffn[l], sc2, sh2,
            w_router[l].T, b_router[l].reshape(N_EXPERTS, 1), seq, tm)
        rank, cnt = _rank(idx, tm)
        counts = cnt[:, 0].astype(I32)
        padded = (counts + MOE_ROWS - 1) // MOE_ROWS * MOE_ROWS
        pcum = jnp.cumsum(padded)
        pstart = (pcum - padded).astype(I32)
        n_blocks = (t * TOP_K + MOE_ROWS - 1) // MOE_ROWS + N_EXPERTS
        block_row0 = jnp.arange(n_blocks, dtype=I32) * MOE_ROWS
        block_expert = jnp.minimum(
            jnp.sum((pcum[None, :] <= block_row0[:, None]).astype(I32), axis=1),
            N_EXPERTS - 1).astype(I32)
        n_used = (pcum[-1:] // MOE_ROWS).astype(I32)
        dest = _dest(pstart, idx, rank, tm)
        dest_tok = dest.T.reshape(t * TOP_K)
        tg = _pick(t, (256,))
        zero_info = jnp.concatenate([jnp.maximum(pcum - MOE_ROWS, 0).astype(I32), n_used])
        x_pad = _scatter(dest_tok, zero_info, h2, n_blocks * MOE_ROWS, tg)
        y_pad = _expert(block_expert, n_used, x_pad, w_gate[l], b_gate[l], w_up[l], b_up[l],
                        w_down[l], b_down[l])
        x2d = _combine(dest_tok, y_pad, x1, wts.T, g2, norm_final, seq, tg, l == depth - 1)
    return x2d.reshape(bsz, seq, d)
```

```python
import functools
import math

import numpy as np
import jax
import jax.numpy as jnp
from jax import lax
from jax.experimental import pallas as pl
from jax.experimental.pallas import tpu as pltpu

F32 = jnp.float32
BF16 = jnp.bfloat16
I32 = jnp.int32
HIGHEST = lax.Precision.HIGHEST

D_MODEL = 1024
MOBA_HEADS = 8
MOBA_HEAD_DIM = 64
MOBA_WIDTH = MOBA_HEADS * MOBA_HEAD_DIM
MOBA_BLOCK = 256
MOBA_TOPK = 3
MOBA_MAX_BLOCKS = 32
REL_BUCKETS = 32
REL_MAX_DIST = 128
GLA_HEADS = 4
GLA_KEY_DIM = D_MODEL // 2
GLA_VALUE_DIM = D_MODEL
GLA_DK = GLA_KEY_DIM // GLA_HEADS
GLA_DV = GLA_VALUE_DIM // GLA_HEADS
GLA_GATE_RANK = 16
GLA_GATE_NORMALIZER = 16.0
GLA_CHUNK = 64
N_EXPERTS = 32
TOP_K = 4
D_FF = D_MODEL
SWIGLU_ALPHA = 1.702
SWIGLU_LIMIT = 7.0
MOE_ROWS = 512
ROW_UNROLL = 8
EPS = 1e-6
LANES = 128
NEG_BIG = -1e30
LOG2E = math.log2(math.e)
VMEM_LIMIT = 56 * 1024 * 1024


def _cparams(sem, vmem=None):
    return pltpu.CompilerParams(dimension_semantics=sem,
                                vmem_limit_bytes=vmem or VMEM_LIMIT)


def _nt_dot(a, b, **kw):
    return lax.dot_general(a, b, (((1,), (1,)), ((), ())),
                           preferred_element_type=F32, **kw)


def _rms(x):
    return x * lax.rsqrt(jnp.mean(x * x, axis=-1, keepdims=True) + EPS)


def _sigmoid(x):
    return 1.0 / (1.0 + jnp.exp(-x))


U32 = jnp.uint32
_HI16 = 0xFFFF0000


def _pack_halves(x):
    n = x.shape[1] // 2
    lo = pltpu.bitcast(x[:, :n].astype(BF16).astype(F32), U32)
    hi = pltpu.bitcast(x[:, n:].astype(BF16).astype(F32), U32)
    return (hi & U32(_HI16)) | (lo >> 16)


def _unpack_halves(w):
    return (pltpu.bitcast(w << 16, F32), pltpu.bitcast(w & U32(_HI16), F32))


def _ada_kernel(c_ref, w_ref, b_ref, o_ref):
    c = c_ref[...]
    s = c * _sigmoid(c)
    o_ref[...] = jnp.dot(s, w_ref[...], precision=HIGHEST,
                         preferred_element_type=F32) + b_ref[...]


def _ada(c, w, b):
    bsz, d = c.shape
    n = w.shape[1]
    rows = -(-bsz // 8) * 8
    cp = jnp.zeros((rows, d), F32).at[:bsz].set(c)
    tn = 768
    out = pl.pallas_call(
        _ada_kernel,
        grid=(n // tn,),
        in_specs=[pl.BlockSpec((rows, d), lambda j: (0, 0)),
                  pl.BlockSpec((d, tn), lambda j: (0, j)),
                  pl.BlockSpec((1, tn), lambda j: (0, j))],
        out_specs=pl.BlockSpec((rows, tn), lambda j: (0, j)),
        out_shape=jax.ShapeDtypeStruct((rows, n), F32),
        compiler_params=_cparams(("arbitrary",)),
        name="ada",
    )(cp, w, b.reshape(1, n))
    return out[:bsz]


_OFF_QA = 0
_OFF_KA = _OFF_QA + MOBA_WIDTH
_OFF_VA = _OFF_KA + MOBA_HEADS * LANES
_OFF_QKB = _OFF_VA + MOBA_WIDTH
_OFF_VB = _OFF_QKB + 2 * GLA_KEY_DIM
_OFF_GK = _OFF_VB + GLA_VALUE_DIM
_OFF_R = _OFF_GK + LANES
_OFF_G = _OFF_R + GLA_VALUE_DIM
_W_CAT = _OFF_G + 2 * D_MODEL


def _regroup_w_in(w):
    d = w.shape[0]
    hd = MOBA_HEAD_DIM
    o_k, o_v = MOBA_WIDTH, 2 * MOBA_WIDTH
    o_qkb = 3 * MOBA_WIDTH
    o_gk = o_qkb + 2 * GLA_KEY_DIM + GLA_VALUE_DIM
    k_heads = w[:, o_k:o_v].reshape(d, MOBA_HEADS, hd)
    k_groups = jnp.pad(k_heads, ((0, 0), (0, 0), (0, LANES - hd))).reshape(d, -1)
    gk = jnp.pad(w[:, o_gk:o_gk + GLA_GATE_RANK], ((0, 0), (0, LANES - GLA_GATE_RANK)))
    return jnp.concatenate(
        [w[:, :o_k], k_groups, w[:, o_v:o_qkb], w[:, o_qkb:o_gk], gk,
         w[:, o_gk + GLA_GATE_RANK:]], axis=1).astype(BF16)


def _inproj_kernel(x_ref, nw_ref, sc_ref, sh_ref, w_ref,
                   qt_ref, ka_ref, vt_ref, qkb_ref, vb_ref, gk_ref, r_ref, g_ref, *, tpb):
    tm = x_ref.shape[0]
    hd = MOBA_HEAD_DIM
    nbt = tm // MOBA_BLOCK
    h = _rms(x_ref[...]) * nw_ref[...]
    h = h * (1.0 + sc_ref[0]) + sh_ref[0]
    hb = h.astype(BF16)

    def mm(a, b):
        return jnp.dot(hb, w_ref[:, a:b], preferred_element_type=F32)

    q_t = (mm(_OFF_QA, _OFF_KA) * (hd ** -0.5 * LOG2E)).T
    v_t = mm(_OFF_VA, _OFF_QKB).T
    blk0 = (pl.program_id(0) % tpb) * nbt
    lane = lax.broadcasted_iota(I32, (MOBA_BLOCK, LANES), 1)
    ones_rows = jnp.where(
        lax.broadcasted_iota(I32, (MOBA_VT_ROWS - hd, MOBA_BLOCK), 0) == 0, 1.0, 0.0)
    for hh in range(MOBA_HEADS):
        qt_ref[0, hh] = q_t[hh * hd:(hh + 1) * hd].astype(BF16)
        k_h = mm(_OFF_KA + hh * LANES, _OFF_KA + (hh + 1) * LANES)
        for j in range(nbt):
            rows = slice(j * MOBA_BLOCK, (j + 1) * MOBA_BLOCK)
            ka_ref[0, hh, j] = jnp.where(lane == hd + blk0 + j, 1.0, k_h[rows]).astype(BF16)
            vt_ref[0, hh, j, 0:hd, :] = v_t[hh * hd:(hh + 1) * hd, rows].astype(BF16)
            vt_ref[0, hh, j, hd:MOBA_VT_ROWS, :] = ones_rows.astype(BF16)
    qkb_ref[...] = mm(_OFF_QKB, _OFF_VB).astype(BF16)
    vb_ref[...] = mm(_OFF_VB, _OFF_GK).astype(BF16)
    gk_ref[...] = mm(_OFF_GK, _OFF_R)
    r = mm(_OFF_R, _OFF_G)
    r_ref[...] = (r * _sigmoid(r)).astype(BF16)
    g_ref[...] = _sigmoid(mm(_OFF_G, _W_CAT)).astype(BF16)


def _inproj(x2d, nw, sc, sh, w_cat, bsz, seq, tm):
    t, d = x2d.shape
    tpb = seq // tm
    nbt = tm // MOBA_BLOCK
    nh, hd = MOBA_HEADS, MOBA_HEAD_DIM
    row = lambda w: pl.BlockSpec((tm, w), lambda i: (i, 0))
    per_b = pl.BlockSpec((1, 1, d), lambda i: (i // tpb, 0, 0))
    rows_out = [(2 * GLA_KEY_DIM, BF16), (GLA_VALUE_DIM, BF16), (LANES, F32),
                (GLA_VALUE_DIM, BF16), (2 * D_MODEL, BF16)]
    return pl.pallas_call(
        functools.partial(_inproj_kernel, tpb=tpb),
        grid=(t // tm,),
        in_specs=[row(d), pl.BlockSpec((1, d), lambda i: (0, 0)), per_b, per_b,
                  pl.BlockSpec((d, _W_CAT), lambda i: (0, 0), pipeline_mode=pl.Buffered(1))],
        out_specs=[pl.BlockSpec((1, nh, hd, tm), lambda i: (i // tpb, 0, 0, i % tpb)),
                   pl.BlockSpec((1, nh, nbt, MOBA_BLOCK, LANES),
                                lambda i: (i // tpb, 0, i % tpb, 0, 0)),
                   pl.BlockSpec((1, nh, nbt, MOBA_VT_ROWS, MOBA_BLOCK),
                                lambda i: (i // tpb, 0, i % tpb, 0, 0))]
                  + [row(w) for w, _ in rows_out],
        out_shape=[jax.ShapeDtypeStruct((bsz, nh, hd, seq), BF16),
                   jax.ShapeDtypeStruct((bsz, nh, seq // MOBA_BLOCK, MOBA_BLOCK, LANES), BF16),
                   jax.ShapeDtypeStruct((bsz, nh, seq // MOBA_BLOCK, MOBA_VT_ROWS, MOBA_BLOCK),
                                        BF16)]
                  + [jax.ShapeDtypeStruct((t, w), dt) for w, dt in rows_out],
        compiler_params=_cparams(("arbitrary",)),
        name="inproj",
    )(x2d, nw.reshape(1, d), sc, sh, w_cat)


def _t5_bucket_np(n):
    n = np.maximum(n, 0)
    max_exact = REL_BUCKETS // 2
    nf = np.maximum(n, max_exact).astype(np.float32)
    large = max_exact + (np.log(nf / max_exact) / math.log(REL_MAX_DIST / max_exact)
                         * (REL_BUCKETS - max_exact)).astype(np.int32)
    large = np.minimum(large, REL_BUCKETS - 1)
    return np.where(n < max_exact, n, large).astype(np.int32)


def _bucket_table():
    kj = np.arange(MOBA_BLOCK)[:, None]
    qi = np.arange(2 * MOBA_BLOCK)[None, :] % MOBA_BLOCK
    prev = np.arange(2 * MOBA_BLOCK)[None, :] < MOBA_BLOCK
    return _t5_bucket_np(qi - kj + np.where(prev, MOBA_BLOCK, 0))


def _bias_kernel(rb_ref, bucket_ref, o_ref):
    h = pl.program_id(0)
    bk = bucket_ref[...]
    far = rb_ref[(REL_BUCKETS - 1) * MOBA_HEADS + h]
    acc = jnp.zeros(bk.shape, F32)
    for b in range(REL_BUCKETS):
        acc = jnp.where(bk == b, rb_ref[b * MOBA_HEADS + h] - far, acc)
    o_ref[0] = acc * LOG2E


def _bias_tiles(rel_bias):
    bucket = jnp.asarray(_bucket_table())
    return pl.pallas_call(
        _bias_kernel,
        grid=(MOBA_HEADS,),
        in_specs=[pl.BlockSpec(memory_space=pltpu.SMEM),
                  pl.BlockSpec(bucket.shape, lambda h: (0, 0))],
        out_specs=pl.BlockSpec((1,) + bucket.shape, lambda h: (h, 0, 0)),
        out_shape=jax.ShapeDtypeStruct((MOBA_HEADS,) + bucket.shape, F32),
        compiler_params=_cparams(("arbitrary",)),
        name="bias",
    )(rel_bias.reshape(-1), bucket)


MOBA_HEADS_PER_STEP = 8
MOBA_FAR_GROUP = 2
MOBA_FAR_LAGS = (0, 3, 6)
MOBA_VT_ROWS = MOBA_HEAD_DIM + 16


def _moba_kernel(qt_ref, k_ref, vt_ref, bias_ref, o_ref, kmean_ref, qa_ref):
    blk = MOBA_BLOCK
    nb = MOBA_MAX_BLOCKS
    hd = MOBA_HEAD_DIM
    hp = MOBA_HEADS_PER_STEP
    grp = MOBA_FAR_GROUP
    qi = pl.program_id(2)
    neg = -jnp.inf

    @pl.when(qi == 0)
    def _():
        kmean_ref[...] = jnp.zeros(kmean_ref.shape, F32)

    row = lax.broadcasted_iota(I32, (nb, blk), 0)
    rowf = row.astype(F32)
    key_i = lax.broadcasted_iota(I32, (blk, blk), 0)
    qry_i = lax.broadcasted_iota(I32, (blk, blk), 1)
    pad = jnp.zeros((LANES - hd, blk), BF16)
    pad_hi = jnp.zeros((LANES - hd - nb, blk), BF16)
    prev_j = jnp.maximum(qi - 1, 0)

    def skewed(stages, lags):
        vals = [None] * hp
        for step in range(hp + lags[-1]):
            for stage, lag in zip(stages, lags):
                h = step - lag
                if 0 <= h < hp:
                    vals[h] = stage(h, vals[h])
        return vals

    def select(h, _):
        qt = qt_ref[0, h]
        gate = jnp.dot(kmean_ref[h, :, 0:hd], qt.astype(F32), precision=HIGHEST,
                       preferred_element_type=F32)
        g = jnp.where(row < qi, gate, neg)
        sel = jnp.zeros((nb, blk), F32)
        for _ in range(MOBA_TOPK):
            mx = jnp.max(g, axis=0, keepdims=True)
            first = jnp.min(jnp.where(g == mx, rowf, float(nb)), axis=0, keepdims=True)
            pick = rowf == jnp.where(mx > neg, first, -1.0)
            sel = jnp.where(pick, 1.0, sel)
            g = jnp.where(pick, neg, g)
        mask_prev = jnp.where(sel > 0.0, jnp.where(row == qi - 1, 0.0, NEG_BIG), NEG_BIG)
        mask_far = jnp.where(sel > 0.0, jnp.where(row < qi - 1, 0.0, NEG_BIG), NEG_BIG)
        qa_ref[h] = jnp.concatenate([qt, mask_far.astype(BF16), pad_hi], axis=0)
        return (jnp.concatenate([qt, pad], axis=0),
                jnp.concatenate([qt, mask_prev.astype(BF16), pad_hi], axis=0))

    def near_scores(h, qa):
        qa_own, qa_prev = qa
        s_own = jnp.dot(k_ref[0, h, qi], qa_own, preferred_element_type=F32)
        s_prev = jnp.dot(k_ref[0, h, prev_j], qa_prev, preferred_element_type=F32)
        return s_own, s_prev

    def near_softmax(h, ss):
        s_own, s_prev = ss
        s_own = jnp.where(key_i <= qry_i, s_own + bias_ref[h, :, blk:2 * blk], neg)
        s = jnp.concatenate([s_own, s_prev + bias_ref[h, :, 0:blk]], axis=0)
        m0 = jnp.max(s, axis=0, keepdims=True)
        return m0, jnp.exp2(s - m0)

    def near_pv(h, mp):
        m0, p = mp
        pb = p.astype(BF16)
        acc = (jnp.dot(vt_ref[0, h, qi], pb[0:blk], preferred_element_type=F32)
               + jnp.dot(vt_ref[0, h, prev_j], pb[blk:2 * blk], preferred_element_type=F32))
        return m0, acc

    states = tuple(skewed([select, near_scores, near_softmax, near_pv], (0, 1, 2, 3)))

    def far(gi, states):
        j0 = gi * grp

        def qk(h, _):
            kt = k_ref[0, h, pl.ds(j0, grp)].reshape(grp * blk, LANES)
            return jnp.dot(kt, qa_ref[h], preferred_element_type=F32)

        def softmax(h, s):
            m_old = states[h][0]
            m_new = jnp.maximum(m_old, jnp.max(s, axis=0, keepdims=True))
            return m_new, jnp.exp2(m_old - m_new), jnp.exp2(s - m_new)

        def pv(h, sm):
            m_new, a, p = sm
            pb = p.astype(BF16)
            tot = a * states[h][1]
            for i in range(grp):
                tot = tot + jnp.dot(vt_ref[0, h, j0 + i], pb[i * blk:(i + 1) * blk],
                                    preferred_element_type=F32)
            return m_new, tot

        return tuple(skewed([qk, softmax, pv], MOBA_FAR_LAGS))

    states = lax.fori_loop(0, (prev_j + grp - 1) // grp, far, tuple(states))

    for h in range(hp):
        acc = states[h][1]
        o_ref[0, h * hd:(h + 1) * hd, :] = (acc[0:hd] / acc[hd:hd + 1]).astype(o_ref.dtype)
        kmean_ref[h, pl.ds(qi, 1), :] = jnp.mean(k_ref[0, h, qi].astype(F32), axis=0,
                                                 keepdims=True)


def _moba(qt, k_aug, vt, bias):
    bsz, nh, hd, s = qt.shape
    blk = MOBA_BLOCK
    hp = MOBA_HEADS_PER_STEP
    nblk = s // blk
    assert nh % hp == 0 and (nblk % MOBA_FAR_GROUP == 0 or nblk == 1)
    return pl.pallas_call(
        _moba_kernel,
        grid=(bsz, nh // hp, nblk),
        in_specs=[pl.BlockSpec((1, hp, hd, blk), lambda b, g, i: (b, g, 0, i)),
                  pl.BlockSpec((1, hp, nblk, blk, LANES), lambda b, g, i: (b, g, 0, 0, 0),
                               pipeline_mode=pl.Buffered(1)),
                  pl.BlockSpec((1, hp, nblk, MOBA_VT_ROWS, blk), lambda b, g, i: (b, g, 0, 0, 0),
                               pipeline_mode=pl.Buffered(1)),
                  pl.BlockSpec((hp, blk, 2 * blk), lambda b, g, i: (g, 0, 0),
                               pipeline_mode=pl.Buffered(1))],
        out_specs=pl.BlockSpec((1, hp * hd, blk), lambda b, g, i: (b, g, i)),
        out_shape=jax.ShapeDtypeStruct((bsz, nh * hd, s), BF16),
        scratch_shapes=[pltpu.VMEM((hp, MOBA_MAX_BLOCKS, LANES), F32),
                        pltpu.VMEM((hp, LANES, blk), BF16)],
        compiler_params=_cparams(("arbitrary", "arbitrary", "arbitrary")),
        name="moba",
    )(qt, k_aug, vt, bias)


def _gla_kernel(q_ref, k_ref, v_ref, gk_ref, wup_ref, bgk_ref, r_ref, gn_ref, o_ref,
                state_ref, *, nchunk):
    ch = GLA_CHUNK
    tc = nchunk * ch

    @pl.when(pl.program_id(2) == 0)
    def _():
        state_ref[...] = jnp.zeros(state_ref.shape, F32)

    z = jnp.dot(gk_ref[...], wup_ref[...], precision=HIGHEST,
                preferred_element_type=F32) + bgk_ref[...]
    log_a = (jnp.minimum(z, 0.0) - jnp.log(1.0 + jnp.exp(-jnp.abs(z)))) / GLA_GATE_NORMALIZER
    rin = lax.broadcasted_iota(I32, (tc, GLA_DK), 0) & (ch - 1)
    b = log_a
    sh = 1
    while sh < ch:
        b = b + jnp.where(rin >= sh, pltpu.roll(b, sh, axis=0), 0.0)
        sh *= 2
    q = q_ref[...].astype(F32) * (GLA_DK ** -0.5)
    k = k_ref[...].astype(F32)
    q_g = (q * jnp.exp(b)).astype(BF16)
    k_g = (k * jnp.exp(-b)).astype(BF16)
    causal = (lax.broadcasted_iota(I32, (ch, ch), 1) <= lax.broadcasted_iota(I32, (ch, ch), 0))
    eye = (lax.broadcasted_iota(I32, (GLA_DK, GLA_DK), 0)
           == lax.broadcasted_iota(I32, (GLA_DK, GLA_DK), 1))
    b3 = b.reshape(nchunk, ch, GLA_DK)
    b_last = b3[:, ch - 1:ch, :]
    k_end = (k * jnp.exp(jnp.broadcast_to(b_last, b3.shape) - b3).reshape(tc, GLA_DK)
             ).astype(BF16)
    decay = jnp.exp(b_last)
    chunks = [slice(n * ch, (n + 1) * ch) for n in range(nchunk)]
    o_intra, kv, decay_col = [], [], []
    for n, sl in enumerate(chunks):
        v_c = v_ref[sl, :]
        att = jnp.where(causal, _nt_dot(q_g[sl], k_g[sl]), 0.0)
        o_intra.append(jnp.dot(att.astype(BF16), v_c, preferred_element_type=F32))
        kv.append(lax.dot_general(k_end[sl], v_c, (((0,), (0,)), ((), ())),
                                  preferred_element_type=F32))
        decay_col.append(jnp.sum(
            jnp.where(eye, jnp.broadcast_to(decay[n], (GLA_DK, GLA_DK)), 0.0),
            axis=1, keepdims=True))
    state = state_ref[...]
    o_inter = []
    for n, sl in enumerate(chunks):
        o_inter.append(jnp.dot(q_g[sl], state.astype(BF16), preferred_element_type=F32))
        state = decay_col[n] * state + kv[n]
    state_ref[...] = state
    o = jnp.concatenate([a + c for a, c in zip(o_intra, o_inter)], axis=0)
    o_ref[...] = (_rms(o) * gn_ref[...] * r_ref[...].astype(F32)).astype(o_ref.dtype)


def _gla(qkb, vb, gk, wup, bgk, r_act, gn, bsz, seq, nchunk):
    t = qkb.shape[0]
    tc = nchunk * GLA_CHUNK
    nc = seq // tc
    rowblk = lambda w, off: pl.BlockSpec((tc, w), lambda b, h, c: (b * nc + c, h + off))
    return pl.pallas_call(
        functools.partial(_gla_kernel, nchunk=nchunk),
        grid=(bsz, GLA_HEADS, nc),
        in_specs=[rowblk(GLA_DK, 0), rowblk(GLA_DK, GLA_HEADS), rowblk(GLA_DV, 0),
                  pl.BlockSpec((tc, LANES), lambda b, h, c: (b * nc + c, 0)),
                  pl.BlockSpec((LANES, GLA_DK), lambda b, h, c: (0, h)),
                  pl.BlockSpec((1, GLA_DK), lambda b, h, c: (0, h)),
                  rowblk(GLA_DV, 0),
                  pl.BlockSpec((1, GLA_DV), lambda b, h, c: (0, 0))],
        out_specs=rowblk(GLA_DV, 0),
        out_shape=jax.ShapeDtypeStruct((t, GLA_VALUE_DIM), BF16),
        scratch_shapes=[pltpu.VMEM((GLA_DK, GLA_DV), F32)],
        compiler_params=_cparams(("arbitrary", "arbitrary", "arbitrary")),
        name="gla",
    )(qkb, qkb, vb, gk, wup, bgk, r_act, gn)


def _merge_kernel(ya_ref, yb_ref, g_ref, x_ref, wpa_ref, wpb_ref, wout_ref, g1_ref,
                  nw_ref, sc_ref, sh_ref, wr_ref, br_ref,
                  x1_ref, h2_ref, idx_ref, wts_ref):
    pa = lax.dot_general(ya_ref[0], wpa_ref[...], (((0,), (0,)), ((), ())),
                         preferred_element_type=F32)
    pb = jnp.dot(yb_ref[...], wpb_ref[...], preferred_element_type=F32)
    mixed = (g_ref[:, 0:D_MODEL].astype(F32) * pa
             + g_ref[:, D_MODEL:2 * D_MODEL].astype(F32) * pb)
    y = jnp.dot(mixed.astype(BF16), wout_ref[...], preferred_element_type=F32)
    x1 = x_ref[...] + g1_ref[0] * y
    x1_ref[...] = x1
    h2 = _rms(x1) * nw_ref[...]
    h2 = h2 * (1.0 + sc_ref[0]) + sh_ref[0]
    h2_ref[...] = _pack_halves(h2)
    logits = _nt_dot(wr_ref[...], h2, precision=HIGHEST) + br_ref[...]
    rowf = lax.broadcasted_iota(I32, logits.shape, 0).astype(F32)
    vals, idxs = [], []
    cur = logits
    for _ in range(TOP_K):
        mx = jnp.max(cur, axis=0, keepdims=True)
        first = jnp.min(jnp.where(cur == mx, rowf, float(N_EXPERTS)), axis=0, keepdims=True)
        vals.append(mx)
        idxs.append(first)
        cur = jnp.where(rowf == first, -jnp.inf, cur)
    es = [jnp.exp(v - vals[0]) for v in vals]
    tot = es[0]
    for e in es[1:]:
        tot = tot + e
    idx_ref[...] = jnp.concatenate(idxs, axis=0).astype(I32)
    wts_ref[...] = jnp.concatenate([e / tot for e in es], axis=0)


def _merge(ya, yb, gates, x2d, wpa, wpb, wout, g1, nw, sc, sh, wr_t, br, seq, tm):
    t, d = x2d.shape
    tpb = seq // tm
    row = lambda w: pl.BlockSpec((tm, w), lambda i: (i, 0))
    full = lambda a: pl.BlockSpec(a.shape, lambda i: (0,) * a.ndim)
    per_b = pl.BlockSpec((1, 1, d), lambda i: (i // tpb, 0, 0))
    colblk = pl.BlockSpec((TOP_K, tm), lambda i: (0, i))
    return pl.pallas_call(
        _merge_kernel,
        grid=(t // tm,),
        in_specs=[pl.BlockSpec((1, MOBA_WIDTH, tm), lambda i: (i // tpb, 0, i % tpb)),
                  row(GLA_VALUE_DIM), row(2 * D_MODEL), row(d),
                  full(wpa), full(wpb), full(wout), per_b,
                  pl.BlockSpec((1, d), lambda i: (0, 0)), per_b, per_b,
                  full(wr_t), full(br)],
        out_specs=[row(d), row(d // 2), colblk, colblk],
        out_shape=[jax.ShapeDtypeStruct((t, d), F32), jax.ShapeDtypeStruct((t, d // 2), U32),
                   jax.ShapeDtypeStruct((TOP_K, t), I32), jax.ShapeDtypeStruct((TOP_K, t), F32)],
        compiler_params=_cparams(("arbitrary",)),
        name="merge",
    )(ya, yb, gates, x2d, wpa, wpb, wout, g1, nw.reshape(1, d), sc, sh, wr_t, br)


def _rank_kernel(idx_ref, rank_ref, cnt_ref, carry_ref):
    tm = idx_ref.shape[1]

    @pl.when(pl.program_id(0) == 0)
    def _():
        carry_ref[...] = jnp.zeros(carry_ref.shape, F32)

    rows = lax.broadcasted_iota(I32, (N_EXPERTS, tm), 0)
    before = (lax.broadcasted_iota(I32, (tm, tm), 0)
              < lax.broadcasted_iota(I32, (tm, tm), 1))
    upper = jnp.where(before, 1.0, 0.0).astype(BF16)
    carry = carry_ref[:, 0:1]
    ranks = []
    for k in range(TOP_K):
        onehot = idx_ref[k:k + 1, :] == rows
        onef = jnp.where(onehot, 1.0, 0.0)
        earlier = jnp.dot(onef.astype(BF16), upper, preferred_element_type=F32) + carry
        ranks.append(jnp.sum(jnp.where(onehot, earlier, 0.0), axis=0, keepdims=True))
        carry = carry + jnp.sum(onef, axis=1, keepdims=True)
    rank_ref[...] = jnp.concatenate(ranks, axis=0).astype(I32)
    total = jnp.broadcast_to(carry, carry_ref.shape)
    carry_ref[...] = total
    cnt_ref[...] = total


def _rank(idx, tm):
    t = idx.shape[1]
    return pl.pallas_call(
        _rank_kernel,
        grid=(t // tm,),
        in_specs=[pl.BlockSpec((TOP_K, tm), lambda i: (0, i))],
        out_specs=[pl.BlockSpec((TOP_K, tm), lambda i: (0, i)),
                   pl.BlockSpec((N_EXPERTS, LANES), lambda i: (0, 0))],
        out_shape=[jax.ShapeDtypeStruct((TOP_K, t), I32),
                   jax.ShapeDtypeStruct((N_EXPERTS, LANES), F32)],
        scratch_shapes=[pltpu.VMEM((N_EXPERTS, LANES), F32)],
        compiler_params=_cparams(("arbitrary",)),
        name="rank",
    )(idx)


def _dest_kernel(pstart_ref, idx_ref, rank_ref, dest_ref):
    idx = idx_ref[...]
    off = jnp.zeros(idx.shape, I32)
    for e in range(N_EXPERTS):
        off = jnp.where(idx == e, pstart_ref[e], off)
    dest_ref[...] = rank_ref[...] + off


def _dest(pstart, idx, rank, tm):
    t = idx.shape[1]
    blk = pl.BlockSpec((TOP_K, tm), lambda i: (0, i))
    return pl.pallas_call(
        _dest_kernel,
        grid=(t // tm,),
        in_specs=[pl.BlockSpec(memory_space=pltpu.SMEM), blk, blk],
        out_specs=blk,
        out_shape=jax.ShapeDtypeStruct((TOP_K, t), I32),
        compiler_params=_cparams(("arbitrary",)),
        name="dest",
    )(pstart, idx, rank)


def _scatter_kernel(dest_ref, zrow_ref, h_ref, xout_hbm, zbuf, sem, zsem, *, tm):
    @pl.when(pl.program_id(0) == 0)
    def _():
        zbuf[...] = jnp.zeros(zbuf.shape, zbuf.dtype)

        def zero_rows(row0):
            row0 = pl.multiple_of(row0, MOE_ROWS)
            return pltpu.make_async_copy(zbuf, xout_hbm.at[pl.ds(row0, MOE_ROWS)], zsem)

        for e in range(N_EXPERTS):
            zero_rows(zrow_ref[e]).start()
        for e in range(N_EXPERTS):
            zero_rows(zrow_ref[e]).wait()

        def start_tail(j, carry):
            zero_rows(j * MOE_ROWS).start()
            return carry

        def wait_tail(j, carry):
            zero_rows(j * MOE_ROWS).wait()
            return carry

        n_all = xout_hbm.shape[0] // MOE_ROWS
        lax.fori_loop(zrow_ref[N_EXPERTS], n_all, start_tail, 0)
        lax.fori_loop(zrow_ref[N_EXPERTS], n_all, wait_tail, 0)

    def issue(g, carry):
        base = pl.multiple_of(g * ROW_UNROLL, ROW_UNROLL)
        for j in range(ROW_UNROLL):
            for k in range(TOP_K):
                d = dest_ref[(base + j) * TOP_K + k]
                pltpu.make_async_copy(h_ref.at[pl.ds(base + j, 1)], xout_hbm.at[pl.ds(d, 1)],
                                      sem).start()
        return carry

    lax.fori_loop(0, tm // ROW_UNROLL, issue, 0)
    for k in range(TOP_K):
        pltpu.make_async_copy(h_ref, xout_hbm.at[pl.ds(0, tm)], sem).wait()


def _scatter(dest_tok, last_block_row, h2, n_pad, tm):
    t, d = h2.shape
    return pl.pallas_call(
        functools.partial(_scatter_kernel, tm=tm),
        grid=(t // tm,),
        in_specs=[pl.BlockSpec((tm * TOP_K,), lambda i: (i,), memory_space=pltpu.SMEM),
                  pl.BlockSpec(memory_space=pltpu.SMEM),
                  pl.BlockSpec((tm, d), lambda i: (i, 0))],
        out_specs=pl.BlockSpec(memory_space=pl.ANY),
        out_shape=jax.ShapeDtypeStruct((n_pad, d), h2.dtype),
        scratch_shapes=[pltpu.VMEM((MOE_ROWS, d), h2.dtype), pltpu.SemaphoreType.DMA,
                        pltpu.SemaphoreType.DMA],
        compiler_params=_cparams(("arbitrary",)),
        name="scatter",
    )(dest_tok, last_block_row, h2)


def _expert_kernel(be_ref, nused_ref, x_ref, wg_ref, bg_ref, wu_ref, bu_ref, wd_ref, bd_ref,
                   o_ref, wgb_ref, wub_ref, wdb_ref):
    i = pl.program_id(0)
    prev = be_ref[jnp.maximum(i - 1, 0)]

    @pl.when(jnp.logical_or(i == 0, be_ref[i] != prev))
    def _():
        wgb_ref[...] = wg_ref[0].astype(BF16)
        wub_ref[...] = wu_ref[0].astype(BF16)
        wdb_ref[...] = wd_ref[0].astype(BF16)

    @pl.when(i < nused_ref[0])
    def _():
        x_lo, x_hi = [v.astype(BF16) for v in _unpack_halves(x_ref[...])]
        half = x_lo.shape[1]

        def in_dot(w_ref):
            return (jnp.dot(x_lo, w_ref[0:half, :], preferred_element_type=F32)
                    + jnp.dot(x_hi, w_ref[half:2 * half, :], preferred_element_type=F32))

        g = jnp.minimum(in_dot(wgb_ref) + bg_ref[0], SWIGLU_LIMIT)
        u = jnp.clip(in_dot(wub_ref) + bu_ref[0], -SWIGLU_LIMIT, SWIGLU_LIMIT)
        act = g * _sigmoid(SWIGLU_ALPHA * g) * (u + 1.0)
        o_ref[...] = _pack_halves(
            jnp.dot(act.astype(BF16), wdb_ref[...], preferred_element_type=F32) + bd_ref[0])

    @pl.when(i >= nused_ref[0])
    def _():
        o_ref[...] = jnp.zeros(o_ref.shape, o_ref.dtype)


def _expert(block_expert, n_used, x_pad, wg, bg, wu, bu, wd, bd):
    n_pad, dp = x_pad.shape
    _, d, f = wg.shape
    nblk = n_pad // MOE_ROWS
    wspec = lambda a, b: pl.BlockSpec((1, a, b), lambda i, be, nu: (be[i], 0, 0))
    grid_spec = pltpu.PrefetchScalarGridSpec(
        num_scalar_prefetch=2,
        grid=(nblk,),
        in_specs=[pl.BlockSpec((MOE_ROWS, dp), lambda i, be, nu: (jnp.minimum(i, nu[0] - 1), 0)),
                  wspec(d, f), wspec(1, f), wspec(d, f), wspec(1, f), wspec(f, d), wspec(1, d)],
        out_specs=pl.BlockSpec((MOE_ROWS, dp), lambda i, be, nu: (i, 0)),
        scratch_shapes=[pltpu.VMEM((d, f), BF16), pltpu.VMEM((d, f), BF16),
                        pltpu.VMEM((f, d), BF16)])
    return pl.pallas_call(
        _expert_kernel,
        grid_spec=grid_spec,
        out_shape=jax.ShapeDtypeStruct((n_pad, dp), U32),
        compiler_params=_cparams(("arbitrary",)),
        name="expert",
    )(block_expert, n_used, x_pad, wg, bg.reshape(N_EXPERTS, 1, f), wu,
      bu.reshape(N_EXPERTS, 1, f), wd, bd.reshape(N_EXPERTS, 1, d))


def _combine_kernel(dest_ref, y_hbm, x1_ref, wts_ref, g2_ref, nf_ref, o_ref, ybuf, sem, *,
                    tm, final):
    def row_copy(tt, k):
        d = dest_ref[tt * TOP_K + k]
        return pltpu.make_async_copy(y_hbm.at[pl.ds(d, 1)], ybuf.at[k, pl.ds(tt, 1)], sem)

    def issue(g, carry):
        base = pl.multiple_of(g * ROW_UNROLL, ROW_UNROLL)
        for j in range(ROW_UNROLL):
            for k in range(TOP_K):
                row_copy(base + j, k).start()
        return carry

    lax.fori_loop(0, tm // ROW_UNROLL, issue, 0)
    for k in range(TOP_K):
        pltpu.make_async_copy(y_hbm.at[pl.ds(0, tm)], ybuf.at[k], sem).wait()

    w = wts_ref[...]
    lo, hi = None, None
    for k in range(TOP_K):
        y_lo, y_hi = _unpack_halves(ybuf[k])
        lo = w[:, k:k + 1] * y_lo + (0.0 if lo is None else lo)
        hi = w[:, k:k + 1] * y_hi + (0.0 if hi is None else hi)
    moe = jnp.concatenate([lo, hi], axis=1)
    x2 = x1_ref[...] + g2_ref[0] * moe
    o_ref[...] = _rms(x2) * nf_ref[...] if final else x2


def _combine(dest_tok, y_pad, x1, wts_tok, g2, nf, seq, tm, final):
    t, d = x1.shape
    tpb = seq // tm
    return pl.pallas_call(
        functools.partial(_combine_kernel, tm=tm, final=final),
        grid=(t // tm,),
        in_specs=[pl.BlockSpec((tm * TOP_K,), lambda i: (i,), memory_space=pltpu.SMEM),
                  pl.BlockSpec(memory_space=pl.ANY),
                  pl.BlockSpec((tm, d), lambda i: (i, 0)),
                  pl.BlockSpec((tm, TOP_K), lambda i: (i, 0)),
                  pl.BlockSpec((1, 1, d), lambda i: (i // tpb, 0, 0)),
                  pl.BlockSpec((1, d), lambda i: (0, 0))],
        out_specs=pl.BlockSpec((tm, d), lambda i: (i, 0)),
        out_shape=jax.ShapeDtypeStruct((t, d), F32),
        scratch_shapes=[pltpu.VMEM((TOP_K, tm) + y_pad.shape[1:], y_pad.dtype),
                        pltpu.SemaphoreType.DMA],
        compiler_params=_cparams(("arbitrary",)),
        name="combine",
    )(dest_tok, y_pad, x1, wts_tok, g2, nf.reshape(1, d))


def _pick(n, cands):
    for c in cands:
        if n % c == 0:
            return c
    raise ValueError(f"no tile in {cands} divides {n}")


def kernel(x, c, rel_bias, w_ada, b_ada, norm_mix, w_in, w_gk_up, b_gk, gla_norm,
           w_proj_moba, w_proj_gla, w_out, norm_ffn, w_router, b_router,
           w_gate, b_gate, w_up, b_up, w_down, b_down, norm_final):
    bsz, seq, d = x.shape
    depth = w_ada.shape[0]
    assert d == D_MODEL and seq % MOBA_BLOCK == 0 and seq // MOBA_BLOCK <= MOBA_MAX_BLOCKS
    t = bsz * seq
    tm = _pick(seq, (512, 256))
    nchunk = _pick(seq // GLA_CHUNK, (8, 4))
    n_blk = seq // MOBA_BLOCK
    x2d = x.reshape(t, d)
    bias = _bias_tiles(rel_bias)
    per_b = lambda v: v.reshape(bsz, 1, d)

    for l in range(depth):
        mod = _ada(c, w_ada[l], b_ada[l])
        sh1, sc1, g1, sh2, sc2, g2 = [per_b(m) for m in jnp.split(mod, 6, axis=-1)]
        qt, k_aug, vt, qkb, vb, gk, r_act, gates = _inproj(
            x2d, norm_mix[l], sc1, sh1, _regroup_w_in(w_in[l]), bsz, seq, tm)
        ya = _moba(qt, k_aug, vt, bias)
        wup = jnp.pad(w_gk_up[l], ((0, LANES - GLA_GATE_RANK), (0, 0)))
        yb = _gla(qkb, vb, gk, wup, b_gk[l].reshape(1, -1), r_act,
                  gla_norm[l].reshape(1, -1), bsz, seq, nchunk)
        x1, h2, idx, wts = _merge(
            ya, yb, gates, x2d, w_proj_moba[l].astype(BF16), w_proj_gla[l].astype(BF16),
            w_out[l].astype(BF16), g1, norm_ffn[l], sc2, sh2,
            w_router[l].T, b_router[l].reshape(N_EXPERTS, 1), seq, tm)
        rank, cnt = _rank(idx, tm)
        counts = cnt[:, 0].astype(I32)
        padded = (counts + MOE_ROWS - 1) // MOE_ROWS * MOE_ROWS
        pcum = jnp.cumsum(padded)
        pstart = (pcum - padded).astype(I32)
        n_blocks = (t * TOP_K + MOE_ROWS - 1) // MOE_ROWS + N_EXPERTS
        block_row0 = jnp.arange(n_blocks, dtype=I32) * MOE_ROWS
        block_expert = jnp.minimum(
            jnp.sum((pcum[None, :] <= block_row0[:, None]).astype(I32), axis=1),
            N_EXPERTS - 1).astype(I32)
        n_used = (pcum[-1:] // MOE_ROWS).astype(I32)
        dest = _dest(pstart, idx, rank, tm)
        dest_tok = dest.T.reshape(t * TOP_K)
        tg = _pick(t, (256,))
        zero_info = jnp.concatenate([jnp.maximum(pcum - MOE_ROWS, 0).astype(I32), n_used])
        x_pad = _scatter(dest_tok, zero_info, h2, n_blocks * MOE_ROWS, tg)
        y_pad = _expert(block_expert, n_used, x_pad, w_gate[l], b_gate[l], w_up[l], b_up[l],
                        w_down[l], b_down[l])
        x2d = _combine(dest_tok, y_pad, x1, wts.T, g2, norm_final, seq, tg, l == depth - 1)
    return x2d.reshape(bsz, seq, d)
```

```python
import functools
import math

import numpy as np
import jax
import jax.numpy as jnp
from jax import lax
from jax.experimental import pallas as pl
from jax.experimental.pallas import tpu as pltpu

F32 = jnp.float32
BF16 = jnp.bfloat16
I32 = jnp.int32
HIGHEST = lax.Precision.HIGHEST

D_MODEL = 1024
MOBA_HEADS = 8
MOBA_HEAD_DIM = 64
MOBA_WIDTH = MOBA_HEADS * MOBA_HEAD_DIM
MOBA_BLOCK = 256
MOBA_TOPK = 3
MOBA_MAX_BLOCKS = 32
REL_BUCKETS = 32
REL_MAX_DIST = 128
GLA_HEADS = 4
GLA_KEY_DIM = D_MODEL // 2
GLA_VALUE_DIM = D_MODEL
GLA_DK = GLA_KEY_DIM // GLA_HEADS
GLA_DV = GLA_VALUE_DIM // GLA_HEADS
GLA_GATE_RANK = 16
GLA_GATE_NORMALIZER = 16.0
GLA_CHUNK = 64
N_EXPERTS = 32
TOP_K = 4
D_FF = D_MODEL
SWIGLU_ALPHA = 1.702
SWIGLU_LIMIT = 7.0
MOE_ROWS = 512
ROW_UNROLL = 8
EPS = 1e-6
LANES = 128
NEG_BIG = -1e30
LOG2E = math.log2(math.e)
VMEM_LIMIT = 56 * 1024 * 1024


def _cparams(sem, vmem=None):
    return pltpu.CompilerParams(dimension_semantics=sem,
                                vmem_limit_bytes=vmem or VMEM_LIMIT)


def _nt_dot(a, b, **kw):
    return lax.dot_general(a, b, (((1,), (1,)), ((), ())),
                           preferred_element_type=F32, **kw)


def _rms(x):
    return x * lax.rsqrt(jnp.mean(x * x, axis=-1, keepdims=True) + EPS)


def _sigmoid(x):
    return 1.0 / (1.0 + jnp.exp(-x))


U32 = jnp.uint32
_HI16 = 0xFFFF0000


def _pack_halves(x):
    n = x.shape[1] // 2
    lo = pltpu.bitcast(x[:, :n].astype(BF16).astype(F32), U32)
    hi = pltpu.bitcast(x[:, n:].astype(BF16).astype(F32), U32)
    return (hi & U32(_HI16)) | (lo >> 16)


def _unpack_halves(w):
    return (pltpu.bitcast(w << 16, F32), pltpu.bitcast(w & U32(_HI16), F32))


ROW_SLABS = D_MODEL // 2 // LANES


def _store_rows(ref, words):
    m = words.shape[0]
    for c in range(ROW_SLABS):
        ref[pl.ds(c, m, stride=ROW_SLABS), :] = words[:, c * LANES:(c + 1) * LANES]


def _load_rows(ref):
    m = ref.shape[0] // ROW_SLABS
    return jnp.concatenate(
        [ref[pl.ds(c, m, stride=ROW_SLABS), :] for c in range(ROW_SLABS)], axis=1)


def _row(ref, i):
    return ref.at[pl.ds(pl.multiple_of(i * ROW_SLABS, ROW_SLABS), ROW_SLABS)]


def _ada_kernel(c_ref, w_ref, b_ref, o_ref):
    c = c_ref[...]
    s = c * _sigmoid(c)
    o_ref[...] = jnp.dot(s, w_ref[...], precision=HIGHEST,
                         preferred_element_type=F32) + b_ref[...]


def _ada(c, w, b):
    bsz, d = c.shape
    n = w.shape[1]
    rows = -(-bsz // 8) * 8
    cp = jnp.zeros((rows, d), F32).at[:bsz].set(c)
    tn = 768
    out = pl.pallas_call(
        _ada_kernel,
        grid=(n // tn,),
        in_specs=[pl.BlockSpec((rows, d), lambda j: (0, 0)),
                  pl.BlockSpec((d, tn), lambda j: (0, j)),
                  pl.BlockSpec((1, tn), lambda j: (0, j))],
        out_specs=pl.BlockSpec((rows, tn), lambda j: (0, j)),
        out_shape=jax.ShapeDtypeStruct((rows, n), F32),
        compiler_params=_cparams(("arbitrary",)),
        name="ada",
    )(cp, w, b.reshape(1, n))
    return out[:bsz]


_OFF_QA = 0
_OFF_KA = _OFF_QA + MOBA_WIDTH
_OFF_VA = _OFF_KA + MOBA_HEADS * LANES
_OFF_QKB = _OFF_VA + MOBA_WIDTH
_OFF_VB = _OFF_QKB + 2 * GLA_KEY_DIM
_OFF_GK = _OFF_VB + GLA_VALUE_DIM
_OFF_R = _OFF_GK + LANES
_OFF_G = _OFF_R + GLA_VALUE_DIM
_W_CAT = _OFF_G + 2 * D_MODEL


def _regroup_w_in(w):
    d = w.shape[0]
    hd = MOBA_HEAD_DIM
    o_k, o_v = MOBA_WIDTH, 2 * MOBA_WIDTH
    o_qkb = 3 * MOBA_WIDTH
    o_gk = o_qkb + 2 * GLA_KEY_DIM + GLA_VALUE_DIM
    k_heads = w[:, o_k:o_v].reshape(d, MOBA_HEADS, hd)
    k_groups = jnp.pad(k_heads, ((0, 0), (0, 0), (0, LANES - hd))).reshape(d, -1)
    gk = jnp.pad(w[:, o_gk:o_gk + GLA_GATE_RANK], ((0, 0), (0, LANES - GLA_GATE_RANK)))
    return jnp.concatenate(
        [w[:, :o_k], k_groups, w[:, o_v:o_qkb], w[:, o_qkb:o_gk], gk,
         w[:, o_gk + GLA_GATE_RANK:]], axis=1).astype(BF16)


def _inproj_kernel(x_ref, nw_ref, sc_ref, sh_ref, w_ref,
                   qt_ref, ka_ref, vt_ref, qkb_ref, vb_ref, gk_ref, r_ref, g_ref, *, tpb):
    tm = x_ref.shape[0]
    hd = MOBA_HEAD_DIM
    nbt = tm // MOBA_BLOCK
    h = _rms(x_ref[...]) * nw_ref[...]
    h = h * (1.0 + sc_ref[0]) + sh_ref[0]
    hb = h.astype(BF16)

    def mm(a, b):
        return jnp.dot(hb, w_ref[:, a:b], preferred_element_type=F32)

    q_t = (mm(_OFF_QA, _OFF_KA) * (hd ** -0.5 * LOG2E)).T
    v_t = mm(_OFF_VA, _OFF_QKB).T
    blk0 = (pl.program_id(0) % tpb) * nbt
    lane = lax.broadcasted_iota(I32, (MOBA_BLOCK, LANES), 1)
    ones_rows = jnp.where(
        lax.broadcasted_iota(I32, (MOBA_VT_ROWS - hd, MOBA_BLOCK), 0) == 0, 1.0, 0.0)
    for hh in range(MOBA_HEADS):
        qt_ref[0, hh] = q_t[hh * hd:(hh + 1) * hd].astype(BF16)
        k_h = mm(_OFF_KA + hh * LANES, _OFF_KA + (hh + 1) * LANES)
        for j in range(nbt):
            rows = slice(j * MOBA_BLOCK, (j + 1) * MOBA_BLOCK)
            ka_ref[0, hh, j] = jnp.where(lane == hd + blk0 + j, 1.0, k_h[rows]).astype(BF16)
            vt_ref[0, hh, j, 0:hd, :] = v_t[hh * hd:(hh + 1) * hd, rows].astype(BF16)
            vt_ref[0, hh, j, hd:MOBA_VT_ROWS, :] = ones_rows.astype(BF16)
    qkb_ref[...] = mm(_OFF_QKB, _OFF_VB).astype(BF16)
    vb_ref[...] = mm(_OFF_VB, _OFF_GK).astype(BF16)
    gk_ref[...] = mm(_OFF_GK, _OFF_R)
    r = mm(_OFF_R, _OFF_G)
    r_ref[...] = (r * _sigmoid(r)).astype(BF16)
    g_ref[...] = _sigmoid(mm(_OFF_G, _W_CAT)).astype(BF16)


def _inproj(x2d, nw, sc, sh, w_cat, bsz, seq, tm):
    t, d = x2d.shape
    tpb = seq // tm
    nbt = tm // MOBA_BLOCK
    nh, hd = MOBA_HEADS, MOBA_HEAD_DIM
    row = lambda w: pl.BlockSpec((tm, w), lambda i: (i, 0))
    per_b = pl.BlockSpec((1, 1, d), lambda i: (i // tpb, 0, 0))
    rows_out = [(2 * GLA_KEY_DIM, BF16), (GLA_VALUE_DIM, BF16), (LANES, F32),
                (GLA_VALUE_DIM, BF16), (2 * D_MODEL, BF16)]
    return pl.pallas_call(
        functools.partial(_inproj_kernel, tpb=tpb),
        grid=(t // tm,),
        in_specs=[row(d), pl.BlockSpec((1, d), lambda i: (0, 0)), per_b, per_b,
                  pl.BlockSpec((d, _W_CAT), lambda i: (0, 0), pipeline_mode=pl.Buffered(1))],
        out_specs=[pl.BlockSpec((1, nh, hd, tm), lambda i: (i // tpb, 0, 0, i % tpb)),
                   pl.BlockSpec((1, nh, nbt, MOBA_BLOCK, LANES),
                                lambda i: (i // tpb, 0, i % tpb, 0, 0)),
                   pl.BlockSpec((1, nh, nbt, MOBA_VT_ROWS, MOBA_BLOCK),
                                lambda i: (i // tpb, 0, i % tpb, 0, 0))]
                  + [row(w) for w, _ in rows_out],
        out_shape=[jax.ShapeDtypeStruct((bsz, nh, hd, seq), BF16),
                   jax.ShapeDtypeStruct((bsz, nh, seq // MOBA_BLOCK, MOBA_BLOCK, LANES), BF16),
                   jax.ShapeDtypeStruct((bsz, nh, seq // MOBA_BLOCK, MOBA_VT_ROWS, MOBA_BLOCK),
                                        BF16)]
                  + [jax.ShapeDtypeStruct((t, w), dt) for w, dt in rows_out],
        compiler_params=_cparams(("arbitrary",)),
        name="inproj",
    )(x2d, nw.reshape(1, d), sc, sh, w_cat)


def _t5_bucket_np(n):
    n = np.maximum(n, 0)
    max_exact = REL_BUCKETS // 2
    nf = np.maximum(n, max_exact).astype(np.float32)
    large = max_exact + (np.log(nf / max_exact) / math.log(REL_MAX_DIST / max_exact)
                         * (REL_BUCKETS - max_exact)).astype(np.int32)
    large = np.minimum(large, REL_BUCKETS - 1)
    return np.where(n < max_exact, n, large).astype(np.int32)


def _bucket_table():
    kj = np.arange(MOBA_BLOCK)[:, None]
    qi = np.arange(2 * MOBA_BLOCK)[None, :] % MOBA_BLOCK
    prev = np.arange(2 * MOBA_BLOCK)[None, :] < MOBA_BLOCK
    return _t5_bucket_np(qi - kj + np.where(prev, MOBA_BLOCK, 0))


def _bias_kernel(rb_ref, bucket_ref, o_ref):
    h = pl.program_id(0)
    bk = bucket_ref[...]
    far = rb_ref[(REL_BUCKETS - 1) * MOBA_HEADS + h]
    acc = jnp.zeros(bk.shape, F32)
    for b in range(REL_BUCKETS):
        acc = jnp.where(bk == b, rb_ref[b * MOBA_HEADS + h] - far, acc)
    o_ref[0] = acc * LOG2E


def _bias_tiles(rel_bias):
    bucket = jnp.asarray(_bucket_table())
    return pl.pallas_call(
        _bias_kernel,
        grid=(MOBA_HEADS,),
        in_specs=[pl.BlockSpec(memory_space=pltpu.SMEM),
                  pl.BlockSpec(bucket.shape, lambda h: (0, 0))],
        out_specs=pl.BlockSpec((1,) + bucket.shape, lambda h: (h, 0, 0)),
        out_shape=jax.ShapeDtypeStruct((MOBA_HEADS,) + bucket.shape, F32),
        compiler_params=_cparams(("arbitrary",)),
        name="bias",
    )(rel_bias.reshape(-1), bucket)


MOBA_HEADS_PER_STEP = 8
MOBA_FAR_GROUP = 2
MOBA_FAR_LAGS = (0, 3, 6)
MOBA_VT_ROWS = MOBA_HEAD_DIM + 16


def _moba_kernel(qt_ref, k_ref, vt_ref, bias_ref, o_ref, kmean_ref, qa_ref):
    blk = MOBA_BLOCK
    nb = MOBA_MAX_BLOCKS
    hd = MOBA_HEAD_DIM
    hp = MOBA_HEADS_PER_STEP
    grp = MOBA_FAR_GROUP
    qi = pl.program_id(2)
    neg = -jnp.inf

    @pl.when(qi == 0)
    def _():
        kmean_ref[...] = jnp.zeros(kmean_ref.shape, F32)

    row = lax.broadcasted_iota(I32, (nb, blk), 0)
    rowf = row.astype(F32)
    key_i = lax.broadcasted_iota(I32, (blk, blk), 0)
    qry_i = lax.broadcasted_iota(I32, (blk, blk), 1)
    pad = jnp.zeros((LANES - hd, blk), BF16)
    pad_hi = jnp.zeros((LANES - hd - nb, blk), BF16)
    prev_j = jnp.maximum(qi - 1, 0)

    def skewed(stages, lags):
        vals = [None] * hp
        for step in range(hp + lags[-1]):
            for stage, lag in zip(stages, lags):
                h = step - lag
                if 0 <= h < hp:
                    vals[h] = stage(h, vals[h])
        return vals

    def select(h, _):
        qt = qt_ref[0, h]
        gate = jnp.dot(kmean_ref[h, :, 0:hd], qt.astype(F32), precision=HIGHEST,
                       preferred_element_type=F32)
        g = jnp.where(row < qi, gate, neg)
        sel = jnp.zeros((nb, blk), F32)
        for _ in range(MOBA_TOPK):
            mx = jnp.max(g, axis=0, keepdims=True)
            first = jnp.min(jnp.where(g == mx, rowf, float(nb)), axis=0, keepdims=True)
            pick = rowf == jnp.where(mx > neg, first, -1.0)
            sel = jnp.where(pick, 1.0, sel)
            g = jnp.where(pick, neg, g)
        mask_prev = jnp.where(sel > 0.0, jnp.where(row == qi - 1, 0.0, NEG_BIG), NEG_BIG)
        mask_far = jnp.where(sel > 0.0, jnp.where(row < qi - 1, 0.0, NEG_BIG), NEG_BIG)
        qa_ref[h] = jnp.concatenate([qt, mask_far.astype(BF16), pad_hi], axis=0)
        return (jnp.concatenate([qt, pad], axis=0),
                jnp.concatenate([qt, mask_prev.astype(BF16), pad_hi], axis=0))

    def near_scores(h, qa):
        qa_own, qa_prev = qa
        s_own = jnp.dot(k_ref[0, h, qi], qa_own, preferred_element_type=F32)
        s_prev = jnp.dot(k_ref[0, h, prev_j], qa_prev, preferred_element_type=F32)
        return s_own, s_prev

    def near_softmax(h, ss):
        s_own, s_prev = ss
        s_own = jnp.where(key_i <= qry_i, s_own + bias_ref[h, :, blk:2 * blk], neg)
        s = jnp.concatenate([s_own, s_prev + bias_ref[h, :, 0:blk]], axis=0)
        m0 = jnp.max(s, axis=0, keepdims=True)
        return m0, jnp.exp2(s - m0)

    def near_pv(h, mp):
        m0, p = mp
        pb = p.astype(BF16)
        acc = (jnp.dot(vt_ref[0, h, qi], pb[0:blk], preferred_element_type=F32)
               + jnp.dot(vt_ref[0, h, prev_j], pb[blk:2 * blk], preferred_element_type=F32))
        return m0, acc

    states = tuple(skewed([select, near_scores, near_softmax, near_pv], (0, 1, 2, 3)))

    def far(gi, states):
        j0 = gi * grp

        def qk(h, _):
            kt = k_ref[0, h, pl.ds(j0, grp)].reshape(grp * blk, LANES)
            return jnp.dot(kt, qa_ref[h], preferred_element_type=F32)

        def softmax(h, s):
            m_old = states[h][0]
            m_new = jnp.maximum(m_old, jnp.max(s, axis=0, keepdims=True))
            return m_new, jnp.exp2(m_old - m_new), jnp.exp2(s - m_new)

        def pv(h, sm):
            m_new, a, p = sm
            pb = p.astype(BF16)
            tot = a * states[h][1]
            for i in range(grp):
                tot = tot + jnp.dot(vt_ref[0, h, j0 + i], pb[i * blk:(i + 1) * blk],
                                    preferred_element_type=F32)
            return m_new, tot

        return tuple(skewed([qk, softmax, pv], MOBA_FAR_LAGS))

    states = lax.fori_loop(0, (prev_j + grp - 1) // grp, far, tuple(states))

    for h in range(hp):
        acc = states[h][1]
        o_ref[0, h * hd:(h + 1) * hd, :] = (acc[0:hd] / acc[hd:hd + 1]).astype(o_ref.dtype)
        kmean_ref[h, pl.ds(qi, 1), :] = jnp.mean(k_ref[0, h, qi].astype(F32), axis=0,
                                                 keepdims=True)


def _moba(qt, k_aug, vt, bias):
    bsz, nh, hd, s = qt.shape
    blk = MOBA_BLOCK
    hp = MOBA_HEADS_PER_STEP
    nblk = s // blk
    assert nh % hp == 0 and (nblk % MOBA_FAR_GROUP == 0 or nblk == 1)
    return pl.pallas_call(
        _moba_kernel,
        grid=(bsz, nh // hp, nblk),
        in_specs=[pl.BlockSpec((1, hp, hd, blk), lambda b, g, i: (b, g, 0, i)),
                  pl.BlockSpec((1, hp, nblk, blk, LANES), lambda b, g, i: (b, g, 0, 0, 0),
                               pipeline_mode=pl.Buffered(1)),
                  pl.BlockSpec((1, hp, nblk, MOBA_VT_ROWS, blk), lambda b, g, i: (b, g, 0, 0, 0),
                               pipeline_mode=pl.Buffered(1)),
                  pl.BlockSpec((hp, blk, 2 * blk), lambda b, g, i: (g, 0, 0),
                               pipeline_mode=pl.Buffered(1))],
        out_specs=pl.BlockSpec((1, hp * hd, blk), lambda b, g, i: (b, g, i)),
        out_shape=jax.ShapeDtypeStruct((bsz, nh * hd, s), BF16),
        scratch_shapes=[pltpu.VMEM((hp, MOBA_MAX_BLOCKS, LANES), F32),
                        pltpu.VMEM((hp, LANES, blk), BF16)],
        compiler_params=_cparams(("arbitrary", "arbitrary", "arbitrary")),
        name="moba",
    )(qt, k_aug, vt, bias)


def _gla_kernel(q_ref, k_ref, v_ref, gk_ref, wup_ref, bgk_ref, r_ref, gn_ref, o_ref,
                state_ref, *, nchunk):
    ch = GLA_CHUNK
    tc = nchunk * ch

    @pl.when(pl.program_id(2) == 0)
    def _():
        state_ref[...] = jnp.zeros(state_ref.shape, F32)

    z = jnp.dot(gk_ref[...], wup_ref[...], precision=HIGHEST,
                preferred_element_type=F32) + bgk_ref[...]
    log_a = (jnp.minimum(z, 0.0) - jnp.log(1.0 + jnp.exp(-jnp.abs(z)))) / GLA_GATE_NORMALIZER
    rin = lax.broadcasted_iota(I32, (tc, GLA_DK), 0) & (ch - 1)
    b = log_a
    sh = 1
    while sh < ch:
        b = b + jnp.where(rin >= sh, pltpu.roll(b, sh, axis=0), 0.0)
        sh *= 2
    q = q_ref[...].astype(F32) * (GLA_DK ** -0.5)
    k = k_ref[...].astype(F32)
    q_g = (q * jnp.exp(b)).astype(BF16)
    k_g = (k * jnp.exp(-b)).astype(BF16)
    causal = (lax.broadcasted_iota(I32, (ch, ch), 1) <= lax.broadcasted_iota(I32, (ch, ch), 0))
    eye = (lax.broadcasted_iota(I32, (GLA_DK, GLA_DK), 0)
           == lax.broadcasted_iota(I32, (GLA_DK, GLA_DK), 1))
    b3 = b.reshape(nchunk, ch, GLA_DK)
    b_last = b3[:, ch - 1:ch, :]
    k_end = (k * jnp.exp(jnp.broadcast_to(b_last, b3.shape) - b3).reshape(tc, GLA_DK)
             ).astype(BF16)
    decay = jnp.exp(b_last)
    chunks = [slice(n * ch, (n + 1) * ch) for n in range(nchunk)]
    o_intra, kv, decay_col = [], [], []
    for n, sl in enumerate(chunks):
        v_c = v_ref[sl, :]
        att = jnp.where(causal, _nt_dot(q_g[sl], k_g[sl]), 0.0)
        o_intra.append(jnp.dot(att.astype(BF16), v_c, preferred_element_type=F32))
        kv.append(lax.dot_general(k_end[sl], v_c, (((0,), (0,)), ((), ())),
                                  preferred_element_type=F32))
        decay_col.append(jnp.sum(
            jnp.where(eye, jnp.broadcast_to(decay[n], (GLA_DK, GLA_DK)), 0.0),
            axis=1, keepdims=True))
    state = state_ref[...]
    o_inter = []
    for n, sl in enumerate(chunks):
        o_inter.append(jnp.dot(q_g[sl], state.astype(BF16), preferred_element_type=F32))
        state = decay_col[n] * state + kv[n]
    state_ref[...] = state
    o = jnp.concatenate([a + c for a, c in zip(o_intra, o_inter)], axis=0)
    o_ref[...] = (_rms(o) * gn_ref[...] * r_ref[...].astype(F32)).astype(o_ref.dtype)


def _gla(qkb, vb, gk, wup, bgk, r_act, gn, bsz, seq, nchunk):
    t = qkb.shape[0]
    tc = nchunk * GLA_CHUNK
    nc = seq // tc
    rowblk = lambda w, off: pl.BlockSpec((tc, w), lambda b, h, c: (b * nc + c, h + off))
    return pl.pallas_call(
        functools.partial(_gla_kernel, nchunk=nchunk),
        grid=(bsz, GLA_HEADS, nc),
        in_specs=[rowblk(GLA_DK, 0), rowblk(GLA_DK, GLA_HEADS), rowblk(GLA_DV, 0),
                  pl.BlockSpec((tc, LANES), lambda b, h, c: (b * nc + c, 0)),
                  pl.BlockSpec((LANES, GLA_DK), lambda b, h, c: (0, h)),
                  pl.BlockSpec((1, GLA_DK), lambda b, h, c: (0, h)),
                  rowblk(GLA_DV, 0),
                  pl.BlockSpec((1, GLA_DV), lambda b, h, c: (0, 0))],
        out_specs=rowblk(GLA_DV, 0),
        out_shape=jax.ShapeDtypeStruct((t, GLA_VALUE_DIM), BF16),
        scratch_shapes=[pltpu.VMEM((GLA_DK, GLA_DV), F32)],
        compiler_params=_cparams(("arbitrary", "arbitrary", "arbitrary")),
        name="gla",
    )(qkb, qkb, vb, gk, wup, bgk, r_act, gn)


def _merge_kernel(ya_ref, yb_ref, g_ref, x_ref, wpa_ref, wpb_ref, wout_ref, g1_ref,
                  nw_ref, sc_ref, sh_ref, wr_ref, br_ref,
                  x1_ref, h2_ref, idx_ref, wts_ref):
    pa = lax.dot_general(ya_ref[0], wpa_ref[...], (((0,), (0,)), ((), ())),
                         preferred_element_type=F32)
    pb = jnp.dot(yb_ref[...], wpb_ref[...], preferred_element_type=F32)
    mixed = (g_ref[:, 0:D_MODEL].astype(F32) * pa
             + g_ref[:, D_MODEL:2 * D_MODEL].astype(F32) * pb)
    y = jnp.dot(mixed.astype(BF16), wout_ref[...], preferred_element_type=F32)
    x1 = x_ref[...] + g1_ref[0] * y
    x1_ref[...] = x1
    h2 = _rms(x1) * nw_ref[...]
    h2 = h2 * (1.0 + sc_ref[0]) + sh_ref[0]
    _store_rows(h2_ref, _pack_halves(h2))
    logits = _nt_dot(wr_ref[...], h2, precision=HIGHEST) + br_ref[...]
    rowf = lax.broadcasted_iota(I32, logits.shape, 0).astype(F32)
    vals, idxs = [], []
    cur = logits
    for _ in range(TOP_K):
        mx = jnp.max(cur, axis=0, keepdims=True)
        first = jnp.min(jnp.where(cur == mx, rowf, float(N_EXPERTS)), axis=0, keepdims=True)
        vals.append(mx)
        idxs.append(first)
        cur = jnp.where(rowf == first, -jnp.inf, cur)
    es = [jnp.exp(v - vals[0]) for v in vals]
    tot = es[0]
    for e in es[1:]:
        tot = tot + e
    idx_ref[...] = jnp.concatenate(idxs, axis=0).astype(I32)
    wts_ref[...] = jnp.concatenate([e / tot for e in es], axis=0)


def _merge(ya, yb, gates, x2d, wpa, wpb, wout, g1, nw, sc, sh, wr_t, br, seq, tm):
    t, d = x2d.shape
    tpb = seq // tm
    row = lambda w: pl.BlockSpec((tm, w), lambda i: (i, 0))
    full = lambda a: pl.BlockSpec(a.shape, lambda i: (0,) * a.ndim)
    per_b = pl.BlockSpec((1, 1, d), lambda i: (i // tpb, 0, 0))
    colblk = pl.BlockSpec((TOP_K, tm), lambda i: (0, i))
    return pl.pallas_call(
        _merge_kernel,
        grid=(t // tm,),
        in_specs=[pl.BlockSpec((1, MOBA_WIDTH, tm), lambda i: (i // tpb, 0, i % tpb)),
                  row(GLA_VALUE_DIM), row(2 * D_MODEL), row(d),
                  full(wpa), full(wpb), full(wout), per_b,
                  pl.BlockSpec((1, d), lambda i: (0, 0)), per_b, per_b,
                  full(wr_t), full(br)],
        out_specs=[row(d), pl.BlockSpec((tm * ROW_SLABS, LANES), lambda i: (i, 0)),
                   colblk, colblk],
        out_shape=[jax.ShapeDtypeStruct((t, d), F32),
                   jax.ShapeDtypeStruct((t * ROW_SLABS, LANES), U32),
                   jax.ShapeDtypeStruct((TOP_K, t), I32), jax.ShapeDtypeStruct((TOP_K, t), F32)],
        compiler_params=_cparams(("arbitrary",)),
        name="merge",
    )(ya, yb, gates, x2d, wpa, wpb, wout, g1, nw.reshape(1, d), sc, sh, wr_t, br)


def _rank_kernel(idx_ref, rank_ref, cnt_ref, carry_ref):
    tm = idx_ref.shape[1]

    @pl.when(pl.program_id(0) == 0)
    def _():
        carry_ref[...] = jnp.zeros(carry_ref.shape, F32)

    rows = lax.broadcasted_iota(I32, (N_EXPERTS, tm), 0)
    before = (lax.broadcasted_iota(I32, (tm, tm), 0)
              < lax.broadcasted_iota(I32, (tm, tm), 1))
    upper = jnp.where(before, 1.0, 0.0).astype(BF16)
    carry = carry_ref[:, 0:1]
    ranks = []
    for k in range(TOP_K):
        onehot = idx_ref[k:k + 1, :] == rows
        onef = jnp.where(onehot, 1.0, 0.0)
        earlier = jnp.dot(onef.astype(BF16), upper, preferred_element_type=F32) + carry
        ranks.append(jnp.sum(jnp.where(onehot, earlier, 0.0), axis=0, keepdims=True))
        carry = carry + jnp.sum(onef, axis=1, keepdims=True)
    rank_ref[...] = jnp.concatenate(ranks, axis=0).astype(I32)
    total = jnp.broadcast_to(carry, carry_ref.shape)
    carry_ref[...] = total
    cnt_ref[...] = total


def _rank(idx, tm):
    t = idx.shape[1]
    return pl.pallas_call(
        _rank_kernel,
        grid=(t // tm,),
        in_specs=[pl.BlockSpec((TOP_K, tm), lambda i: (0, i))],
        out_specs=[pl.BlockSpec((TOP_K, tm), lambda i: (0, i)),
                   pl.BlockSpec((N_EXPERTS, LANES), lambda i: (0, 0))],
        out_shape=[jax.ShapeDtypeStruct((TOP_K, t), I32),
                   jax.ShapeDtypeStruct((N_EXPERTS, LANES), F32)],
        scratch_shapes=[pltpu.VMEM((N_EXPERTS, LANES), F32)],
        compiler_params=_cparams(("arbitrary",)),
        name="rank",
    )(idx)


def _dest_kernel(pstart_ref, idx_ref, rank_ref, dest_ref):
    idx = idx_ref[...]
    off = jnp.zeros(idx.shape, I32)
    for e in range(N_EXPERTS):
        off = jnp.where(idx == e, pstart_ref[e], off)
    dest_ref[...] = rank_ref[...] + off


def _dest(pstart, idx, rank, tm):
    t = idx.shape[1]
    blk = pl.BlockSpec((TOP_K, tm), lambda i: (0, i))
    return pl.pallas_call(
        _dest_kernel,
        grid=(t // tm,),
        in_specs=[pl.BlockSpec(memory_space=pltpu.SMEM), blk, blk],
        out_specs=blk,
        out_shape=jax.ShapeDtypeStruct((TOP_K, t), I32),
        compiler_params=_cparams(("arbitrary",)),
        name="dest",
    )(pstart, idx, rank)


def _scatter_kernel(dest_ref, zrow_ref, h_ref, xout_hbm, zbuf, sem, zsem, *, tm):
    @pl.when(pl.program_id(0) == 0)
    def _():
        zbuf[...] = jnp.zeros(zbuf.shape, zbuf.dtype)

        def zero_rows(row0):
            line0 = pl.multiple_of(row0 * ROW_SLABS, MOE_ROWS * ROW_SLABS)
            return pltpu.make_async_copy(
                zbuf, xout_hbm.at[pl.ds(line0, MOE_ROWS * ROW_SLABS)], zsem)

        for e in range(N_EXPERTS):
            zero_rows(zrow_ref[e]).start()
        for e in range(N_EXPERTS):
            zero_rows(zrow_ref[e]).wait()

        def start_tail(j, carry):
            zero_rows(j * MOE_ROWS).start()
            return carry

        def wait_tail(j, carry):
            zero_rows(j * MOE_ROWS).wait()
            return carry

        n_all = xout_hbm.shape[0] // (MOE_ROWS * ROW_SLABS)
        lax.fori_loop(zrow_ref[N_EXPERTS], n_all, start_tail, 0)
        lax.fori_loop(zrow_ref[N_EXPERTS], n_all, wait_tail, 0)

    def issue(g, carry):
        base = pl.multiple_of(g * ROW_UNROLL, ROW_UNROLL)
        for j in range(ROW_UNROLL):
            for k in range(TOP_K):
                d = dest_ref[(base + j) * TOP_K + k]
                pltpu.make_async_copy(_row(h_ref, base + j), _row(xout_hbm, d), sem).start(
                    priority=k % 2)
        return carry

    lax.fori_loop(0, tm // ROW_UNROLL, issue, 0)
    for k in range(TOP_K):
        pltpu.make_async_copy(h_ref, xout_hbm.at[pl.ds(0, tm * ROW_SLABS)], sem).wait()


def _scatter(dest_tok, last_block_row, h2, n_pad, tm):
    t = h2.shape[0] // ROW_SLABS
    return pl.pallas_call(
        functools.partial(_scatter_kernel, tm=tm),
        grid=(t // tm,),
        in_specs=[pl.BlockSpec((tm * TOP_K,), lambda i: (i,), memory_space=pltpu.SMEM),
                  pl.BlockSpec(memory_space=pltpu.SMEM),
                  pl.BlockSpec((tm * ROW_SLABS, LANES), lambda i: (i, 0))],
        out_specs=pl.BlockSpec(memory_space=pl.ANY),
        out_shape=jax.ShapeDtypeStruct((n_pad * ROW_SLABS, LANES), h2.dtype),
        scratch_shapes=[pltpu.VMEM((MOE_ROWS * ROW_SLABS, LANES), h2.dtype),
                        pltpu.SemaphoreType.DMA, pltpu.SemaphoreType.DMA],
        compiler_params=_cparams(("arbitrary",)),
        name="scatter",
    )(dest_tok, last_block_row, h2)


def _expert_kernel(be_ref, nused_ref, x_ref, wg_ref, bg_ref, wu_ref, bu_ref, wd_ref, bd_ref,
                   o_ref, wgb_ref, wub_ref, wdb_ref):
    i = pl.program_id(0)
    prev = be_ref[jnp.maximum(i - 1, 0)]

    @pl.when(jnp.logical_or(i == 0, be_ref[i] != prev))
    def _():
        wgb_ref[...] = wg_ref[0].astype(BF16)
        wub_ref[...] = wu_ref[0].astype(BF16)
        wdb_ref[...] = wd_ref[0].astype(BF16)

    @pl.when(i < nused_ref[0])
    def _():
        x_lo, x_hi = [v.astype(BF16) for v in _unpack_halves(_load_rows(x_ref))]
        half = x_lo.shape[1]

        def in_dot(w_ref):
            return (jnp.dot(x_lo, w_ref[0:half, :], preferred_element_type=F32)
                    + jnp.dot(x_hi, w_ref[half:2 * half, :], preferred_element_type=F32))

        g = jnp.minimum(in_dot(wgb_ref) + bg_ref[0], SWIGLU_LIMIT)
        u = jnp.clip(in_dot(wub_ref) + bu_ref[0], -SWIGLU_LIMIT, SWIGLU_LIMIT)
        act = g * _sigmoid(SWIGLU_ALPHA * g) * (u + 1.0)
        _store_rows(o_ref, _pack_halves(
            jnp.dot(act.astype(BF16), wdb_ref[...], preferred_element_type=F32) + bd_ref[0]))

    @pl.when(i >= nused_ref[0])
    def _():
        o_ref[...] = jnp.zeros(o_ref.shape, o_ref.dtype)


def _expert(block_expert, n_used, x_pad, wg, bg, wu, bu, wd, bd):
    lines = MOE_ROWS * ROW_SLABS
    _, d, f = wg.shape
    nblk = x_pad.shape[0] // lines
    wspec = lambda a, b: pl.BlockSpec((1, a, b), lambda i, be, nu: (be[i], 0, 0))
    grid_spec = pltpu.PrefetchScalarGridSpec(
        num_scalar_prefetch=2,
        grid=(nblk,),
        in_specs=[pl.BlockSpec((lines, LANES),
                               lambda i, be, nu: (jnp.minimum(i, nu[0] - 1), 0)),
                  wspec(d, f), wspec(1, f), wspec(d, f), wspec(1, f), wspec(f, d), wspec(1, d)],
        out_specs=pl.BlockSpec((lines, LANES), lambda i, be, nu: (i, 0)),
        scratch_shapes=[pltpu.VMEM((d, f), BF16), pltpu.VMEM((d, f), BF16),
                        pltpu.VMEM((f, d), BF16)])
    return pl.pallas_call(
        _expert_kernel,
        grid_spec=grid_spec,
        out_shape=jax.ShapeDtypeStruct(x_pad.shape, U32),
        compiler_params=_cparams(("arbitrary",)),
        name="expert",
    )(block_expert, n_used, x_pad, wg, bg.reshape(N_EXPERTS, 1, f), wu,
      bu.reshape(N_EXPERTS, 1, f), wd, bd.reshape(N_EXPERTS, 1, d))


def _combine_kernel(dest_ref, y_hbm, x1_ref, wts_ref, g2_ref, nf_ref, o_ref, ybuf, sem, *,
                    tm, final):
    def row_copy(tt, k):
        d = dest_ref[tt * TOP_K + k]
        return pltpu.make_async_copy(_row(y_hbm, d), _row(ybuf.at[k], tt), sem)

    def issue(g, carry):
        base = pl.multiple_of(g * ROW_UNROLL, ROW_UNROLL)
        for j in range(ROW_UNROLL):
            for k in range(TOP_K):
                row_copy(base + j, k).start(priority=k % 2)
        return carry

    lax.fori_loop(0, tm // ROW_UNROLL, issue, 0)
    for k in range(TOP_K):
        pltpu.make_async_copy(y_hbm.at[pl.ds(0, tm * ROW_SLABS)], ybuf.at[k], sem).wait()

    w = wts_ref[...]
    lo, hi = None, None
    for k in range(TOP_K):
        y_lo, y_hi = _unpack_halves(_load_rows(ybuf.at[k]))
        lo = w[:, k:k + 1] * y_lo + (0.0 if lo is None else lo)
        hi = w[:, k:k + 1] * y_hi + (0.0 if hi is None else hi)
    moe = jnp.concatenate([lo, hi], axis=1)
    x2 = x1_ref[...] + g2_ref[0] * moe
    o_ref[...] = _rms(x2) * nf_ref[...] if final else x2


def _combine(dest_tok, y_pad, x1, wts_tok, g2, nf, seq, tm, final):
    t, d = x1.shape
    tpb = seq // tm
    return pl.pallas_call(
        functools.partial(_combine_kernel, tm=tm, final=final),
        grid=(t // tm,),
        in_specs=[pl.BlockSpec((tm * TOP_K,), lambda i: (i,), memory_space=pltpu.SMEM),
                  pl.BlockSpec(memory_space=pl.ANY),
                  pl.BlockSpec((tm, d), lambda i: (i, 0)),
                  pl.BlockSpec((tm, TOP_K), lambda i: (i, 0)),
                  pl.BlockSpec((1, 1, d), lambda i: (i // tpb, 0, 0)),
                  pl.BlockSpec((1, d), lambda i: (0, 0))],
        out_specs=pl.BlockSpec((tm, d), lambda i: (i, 0)),
        out_shape=jax.ShapeDtypeStruct((t, d), F32),
        scratch_shapes=[pltpu.VMEM((TOP_K, tm * ROW_SLABS, LANES), y_pad.dtype),
                        pltpu.SemaphoreType.DMA],
        compiler_params=_cparams(("arbitrary",)),
        name="combine",
    )(dest_tok, y_pad, x1, wts_tok, g2, nf.reshape(1, d))


def _pick(n, cands):
    for c in cands:
        if n % c == 0:
            return c
    raise ValueError(f"no tile in {cands} divides {n}")


def kernel(x, c, rel_bias, w_ada, b_ada, norm_mix, w_in, w_gk_up, b_gk, gla_norm,
           w_proj_moba, w_proj_gla, w_out, norm_ffn, w_router, b_router,
           w_gate, b_gate, w_up, b_up, w_down, b_down, norm_final):
    bsz, seq, d = x.shape
    depth = w_ada.shape[0]
    assert d == D_MODEL and seq % MOBA_BLOCK == 0 and seq // MOBA_BLOCK <= MOBA_MAX_BLOCKS
    t = bsz * seq
    tm = _pick(seq, (512, 256))
    nchunk = _pick(seq // GLA_CHUNK, (8, 4))
    n_blk = seq // MOBA_BLOCK
    x2d = x.reshape(t, d)
    bias = _bias_tiles(rel_bias)
    per_b = lambda v: v.reshape(bsz, 1, d)

    for l in range(depth):
        mod = _ada(c, w_ada[l], b_ada[l])
        sh1, sc1, g1, sh2, sc2, g2 = [per_b(m) for m in jnp.split(mod, 6, axis=-1)]
        qt, k_aug, vt, qkb, vb, gk, r_act, gates = _inproj(
            x2d, norm_mix[l], sc1, sh1, _regroup_w_in(w_in[l]), bsz, seq, tm)
        ya = _moba(qt, k_aug, vt, bias)
        wup = jnp.pad(w_gk_up[l], ((0, LANES - GLA_GATE_RANK), (0, 0)))
        yb = _gla(qkb, vb, gk, wup, b_gk[l].reshape(1, -1), r_act,
                  gla_norm[l].reshape(1, -1), bsz, seq, nchunk)
        x1, h2, idx, wts = _merge(
            ya, yb, gates, x2d, w_proj_moba[l].astype(BF16), w_proj_gla[l].astype(BF16),
            w_out[l].astype(BF16), g1, norm_ffn[l], sc2, sh2,
            w_router[l].T, b_router[l].reshape(N_EXPERTS, 1), seq, tm)
        rank, cnt = _rank(idx, tm)
        counts = cnt[:, 0].astype(I32)
        padded = (counts + MOE_ROWS - 1) // MOE_ROWS * MOE_ROWS
        pcum = jnp.cumsum(padded)
        pstart = (pcum - padded).astype(I32)
        n_blocks = (t * TOP_K + MOE_ROWS - 1) // MOE_ROWS + N_EXPERTS
        block_row0 = jnp.arange(n_blocks, dtype=I32) * MOE_ROWS
        block_expert = jnp.minimum(
            jnp.sum((pcum[None, :] <= block_row0[:, None]).astype(I32), axis=1),
            N_EXPERTS - 1).astype(I32)
        n_used = (pcum[-1:] // MOE_ROWS).astype(I32)
        dest = _dest(pstart, idx, rank, tm)
        dest_tok = dest.T.reshape(t * TOP_K)
        tg = _pick(t, (256,))
        zero_info = jnp.concatenate([jnp.maximum(pcum - MOE_ROWS, 0).astype(I32), n_used])
        x_pad = _scatter(dest_tok, zero_info, h2, n_blocks * MOE_ROWS, tg)
        y_pad = _expert(block_expert, n_used, x_pad, w_gate[l], b_gate[l], w_up[l], b_up[l],
                        w_down[l], b_down[l])
        x2d = _combine(dest_tok, y_pad, x1, wts.T, g2, norm_final, seq, tg, l == depth - 1)
    return x2d.reshape(bsz, seq, d)
```

```python
import functools
import math

import numpy as np
import jax
import jax.numpy as jnp
from jax import lax
from jax.experimental import pallas as pl
from jax.experimental.pallas import tpu as pltpu

F32 = jnp.float32
BF16 = jnp.bfloat16
I32 = jnp.int32
HIGHEST = lax.Precision.HIGHEST

D_MODEL = 1024
MOBA_HEADS = 8
MOBA_HEAD_DIM = 64
MOBA_WIDTH = MOBA_HEADS * MOBA_HEAD_DIM
MOBA_BLOCK = 256
MOBA_TOPK = 3
MOBA_MAX_BLOCKS = 32
REL_BUCKETS = 32
REL_MAX_DIST = 128
GLA_HEADS = 4
GLA_KEY_DIM = D_MODEL // 2
GLA_VALUE_DIM = D_MODEL
GLA_DK = GLA_KEY_DIM // GLA_HEADS
GLA_DV = GLA_VALUE_DIM // GLA_HEADS
GLA_GATE_RANK = 16
GLA_GATE_NORMALIZER = 16.0
GLA_CHUNK = 64
N_EXPERTS = 32
TOP_K = 4
D_FF = D_MODEL
SWIGLU_ALPHA = 1.702
SWIGLU_LIMIT = 7.0
MOE_ROWS = 512
ROW_UNROLL = 8
EPS = 1e-6
LANES = 128
NEG_BIG = -1e30
LOG2E = math.log2(math.e)
VMEM_LIMIT = 56 * 1024 * 1024


def _cparams(sem, vmem=None):
    return pltpu.CompilerParams(dimension_semantics=sem,
                                vmem_limit_bytes=vmem or VMEM_LIMIT)


def _nt_dot(a, b, **kw):
    return lax.dot_general(a, b, (((1,), (1,)), ((), ())),
                           preferred_element_type=F32, **kw)


def _rms(x):
    return x * lax.rsqrt(jnp.mean(x * x, axis=-1, keepdims=True) + EPS)


def _sigmoid(x):
    return 1.0 / (1.0 + jnp.exp(-x))


U32 = jnp.uint32
_HI16 = 0xFFFF0000


def _pack_halves(x):
    n = x.shape[1] // 2
    lo = pltpu.bitcast(x[:, :n].astype(BF16).astype(F32), U32)
    hi = pltpu.bitcast(x[:, n:].astype(BF16).astype(F32), U32)
    return (hi & U32(_HI16)) | (lo >> 16)


def _unpack_halves(w):
    return (pltpu.bitcast(w << 16, F32), pltpu.bitcast(w & U32(_HI16), F32))


ROW_SLABS = D_MODEL // 2 // LANES


def _store_rows(ref, words):
    m = words.shape[0]
    for c in range(ROW_SLABS):
        ref[pl.ds(c, m, stride=ROW_SLABS), :] = words[:, c * LANES:(c + 1) * LANES]


def _load_rows(ref):
    m = ref.shape[0] // ROW_SLABS
    return jnp.concatenate(
        [ref[pl.ds(c, m, stride=ROW_SLABS), :] for c in range(ROW_SLABS)], axis=1)


def _row(ref, i):
    return ref.at[pl.ds(pl.multiple_of(i * ROW_SLABS, ROW_SLABS), ROW_SLABS)]


def _ada_kernel(c_ref, w_ref, b_ref, o_ref):
    c = c_ref[...]
    s = c * _sigmoid(c)
    o_ref[...] = jnp.dot(s, w_ref[...], precision=HIGHEST,
                         preferred_element_type=F32) + b_ref[...]


def _ada(c, w, b):
    bsz, d = c.shape
    n = w.shape[1]
    rows = -(-bsz // 8) * 8
    cp = jnp.zeros((rows, d), F32).at[:bsz].set(c)
    tn = 768
    out = pl.pallas_call(
        _ada_kernel,
        grid=(n // tn,),
        in_specs=[pl.BlockSpec((rows, d), lambda j: (0, 0)),
                  pl.BlockSpec((d, tn), lambda j: (0, j)),
                  pl.BlockSpec((1, tn), lambda j: (0, j))],
        out_specs=pl.BlockSpec((rows, tn), lambda j: (0, j)),
        out_shape=jax.ShapeDtypeStruct((rows, n), F32),
        compiler_params=_cparams(("arbitrary",)),
        name="ada",
    )(cp, w, b.reshape(1, n))
    return out[:bsz]


_OFF_QA = 0
_OFF_KA = _OFF_QA + MOBA_WIDTH
_OFF_VA = _OFF_KA + MOBA_HEADS * LANES
_OFF_QKB = _OFF_VA + MOBA_WIDTH
_OFF_VB = _OFF_QKB + 2 * GLA_KEY_DIM
_OFF_GK = _OFF_VB + GLA_VALUE_DIM
_OFF_R = _OFF_GK + LANES
_OFF_G = _OFF_R + GLA_VALUE_DIM
_W_CAT = _OFF_G + 2 * D_MODEL


def _regroup_w_in(w):
    d = w.shape[0]
    hd = MOBA_HEAD_DIM
    o_k, o_v = MOBA_WIDTH, 2 * MOBA_WIDTH
    o_qkb = 3 * MOBA_WIDTH
    o_gk = o_qkb + 2 * GLA_KEY_DIM + GLA_VALUE_DIM
    k_heads = w[:, o_k:o_v].reshape(d, MOBA_HEADS, hd)
    k_groups = jnp.pad(k_heads, ((0, 0), (0, 0), (0, LANES - hd))).reshape(d, -1)
    gk = jnp.pad(w[:, o_gk:o_gk + GLA_GATE_RANK], ((0, 0), (0, LANES - GLA_GATE_RANK)))
    return jnp.concatenate(
        [w[:, :o_k], k_groups, w[:, o_v:o_qkb], w[:, o_qkb:o_gk], gk,
         w[:, o_gk + GLA_GATE_RANK:]], axis=1).astype(BF16)


def _inproj_kernel(x_ref, nw_ref, sc_ref, sh_ref, w_ref,
                   qt_ref, ka_ref, vt_ref, qkb_ref, vb_ref, gk_ref, r_ref, g_ref, *, tpb):
    tm = x_ref.shape[0]
    hd = MOBA_HEAD_DIM
    nbt = tm // MOBA_BLOCK
    h = _rms(x_ref[...]) * nw_ref[...]
    h = h * (1.0 + sc_ref[0]) + sh_ref[0]
    hb = h.astype(BF16)

    def mm(a, b):
        return jnp.dot(hb, w_ref[:, a:b], preferred_element_type=F32)

    q_t = (mm(_OFF_QA, _OFF_KA) * (hd ** -0.5 * LOG2E)).T
    v_t = mm(_OFF_VA, _OFF_QKB).T
    blk0 = (pl.program_id(0) % tpb) * nbt
    lane = lax.broadcasted_iota(I32, (MOBA_BLOCK, LANES), 1)
    ones_rows = jnp.where(
        lax.broadcasted_iota(I32, (MOBA_VT_ROWS - hd, MOBA_BLOCK), 0) == 0, 1.0, 0.0)
    for hh in range(MOBA_HEADS):
        qt_ref[0, hh] = q_t[hh * hd:(hh + 1) * hd].astype(BF16)
        k_h = mm(_OFF_KA + hh * LANES, _OFF_KA + (hh + 1) * LANES)
        for j in range(nbt):
            rows = slice(j * MOBA_BLOCK, (j + 1) * MOBA_BLOCK)
            ka_ref[0, hh, j] = jnp.where(lane == hd + blk0 + j, 1.0, k_h[rows]).astype(BF16)
            vt_ref[0, hh, j, 0:hd, :] = v_t[hh * hd:(hh + 1) * hd, rows].astype(BF16)
            vt_ref[0, hh, j, hd:MOBA_VT_ROWS, :] = ones_rows.astype(BF16)
    qkb_ref[...] = mm(_OFF_QKB, _OFF_VB).astype(BF16)
    vb_ref[...] = mm(_OFF_VB, _OFF_GK).astype(BF16)
    gk_ref[...] = mm(_OFF_GK, _OFF_R)
    r = mm(_OFF_R, _OFF_G)
    r_ref[...] = (r * _sigmoid(r)).astype(BF16)
    g_ref[...] = _sigmoid(mm(_OFF_G, _W_CAT)).astype(BF16)


def _inproj(x2d, nw, sc, sh, w_cat, bsz, seq, tm):
    t, d = x2d.shape
    tpb = seq // tm
    nbt = tm // MOBA_BLOCK
    nh, hd = MOBA_HEADS, MOBA_HEAD_DIM
    row = lambda w: pl.BlockSpec((tm, w), lambda i: (i, 0))
    per_b = pl.BlockSpec((1, 1, d), lambda i: (i // tpb, 0, 0))
    rows_out = [(2 * GLA_KEY_DIM, BF16), (GLA_VALUE_DIM, BF16), (LANES, F32),
                (GLA_VALUE_DIM, BF16), (2 * D_MODEL, BF16)]
    return pl.pallas_call(
        functools.partial(_inproj_kernel, tpb=tpb),
        grid=(t // tm,),
        in_specs=[row(d), pl.BlockSpec((1, d), lambda i: (0, 0)), per_b, per_b,
                  pl.BlockSpec((d, _W_CAT), lambda i: (0, 0), pipeline_mode=pl.Buffered(1))],
        out_specs=[pl.BlockSpec((1, nh, hd, tm), lambda i: (i // tpb, 0, 0, i % tpb)),
                   pl.BlockSpec((1, nh, nbt, MOBA_BLOCK, LANES),
                                lambda i: (i // tpb, 0, i % tpb, 0, 0)),
                   pl.BlockSpec((1, nh, nbt, MOBA_VT_ROWS, MOBA_BLOCK),
                                lambda i: (i // tpb, 0, i % tpb, 0, 0))]
                  + [row(w) for w, _ in rows_out],
        out_shape=[jax.ShapeDtypeStruct((bsz, nh, hd, seq), BF16),
                   jax.ShapeDtypeStruct((bsz, nh, seq // MOBA_BLOCK, MOBA_BLOCK, LANES), BF16),
                   jax.ShapeDtypeStruct((bsz, nh, seq // MOBA_BLOCK, MOBA_VT_ROWS, MOBA_BLOCK),
                                        BF16)]
                  + [jax.ShapeDtypeStruct((t, w), dt) for w, dt in rows_out],
        compiler_params=_cparams(("arbitrary",)),
        name="inproj",
    )(x2d, nw.reshape(1, d), sc, sh, w_cat)


def _t5_bucket_np(n):
    n = np.maximum(n, 0)
    max_exact = REL_BUCKETS // 2
    nf = np.maximum(n, max_exact).astype(np.float32)
    large = max_exact + (np.log(nf / max_exact) / math.log(REL_MAX_DIST / max_exact)
                         * (REL_BUCKETS - max_exact)).astype(np.int32)
    large = np.minimum(large, REL_BUCKETS - 1)
    return np.where(n < max_exact, n, large).astype(np.int32)


def _bucket_table():
    kj = np.arange(MOBA_BLOCK)[:, None]
    qi = np.arange(2 * MOBA_BLOCK)[None, :] % MOBA_BLOCK
    prev = np.arange(2 * MOBA_BLOCK)[None, :] < MOBA_BLOCK
    return _t5_bucket_np(qi - kj + np.where(prev, MOBA_BLOCK, 0))


def _bias_kernel(rb_ref, bucket_ref, o_ref):
    h = pl.program_id(0)
    bk = bucket_ref[...]
    far = rb_ref[(REL_BUCKETS - 1) * MOBA_HEADS + h]
    acc = jnp.zeros(bk.shape, F32)
    for b in range(REL_BUCKETS):
        acc = jnp.where(bk == b, rb_ref[b * MOBA_HEADS + h] - far, acc)
    o_ref[0] = acc * LOG2E


def _bias_tiles(rel_bias):
    bucket = jnp.asarray(_bucket_table())
    return pl.pallas_call(
        _bias_kernel,
        grid=(MOBA_HEADS,),
        in_specs=[pl.BlockSpec(memory_space=pltpu.SMEM),
                  pl.BlockSpec(bucket.shape, lambda h: (0, 0))],
        out_specs=pl.BlockSpec((1,) + bucket.shape, lambda h: (h, 0, 0)),
        out_shape=jax.ShapeDtypeStruct((MOBA_HEADS,) + bucket.shape, F32),
        compiler_params=_cparams(("arbitrary",)),
        name="bias",
    )(rel_bias.reshape(-1), bucket)


MOBA_HEADS_PER_STEP = 8
MOBA_FAR_GROUP = 2
MOBA_QBLOCKS_PER_STEP = 2
MOBA_FAR_LAGS = (0, 3, 6)
MOBA_NEAR_LAGS = (0, 3, 6, 9)
MOBA_VT_ROWS = MOBA_HEAD_DIM + 16


def _moba_kernel(qt_ref, k_ref, vt_ref, bias_ref, o_ref, kmean_ref, qa_ref):
    blk = MOBA_BLOCK
    nb = MOBA_MAX_BLOCKS
    hd = MOBA_HEAD_DIM
    hp = MOBA_HEADS_PER_STEP
    grp = MOBA_FAR_GROUP
    nq = MOBA_QBLOCKS_PER_STEP
    pair = pl.program_id(2)
    neg = -jnp.inf
    items = [(h, j) for j in range(nq) for h in range(hp)]

    @pl.when(pair == 0)
    def _():
        kmean_ref[...] = jnp.zeros(kmean_ref.shape, F32)

    for h in range(hp):
        for j in range(nq):
            kmean_ref[h, pl.ds(pair * nq + j, 1), :] = jnp.mean(
                k_ref[0, h, pair * nq + j].astype(F32), axis=0, keepdims=True)

    row = lax.broadcasted_iota(I32, (nb, blk), 0)
    rowf = row.astype(F32)
    key_i = lax.broadcasted_iota(I32, (blk, blk), 0)
    qry_i = lax.broadcasted_iota(I32, (blk, blk), 1)
    pad = jnp.zeros((LANES - hd, blk), BF16)
    pad_hi = jnp.zeros((LANES - hd - nb, blk), BF16)

    def skewed(stages, lags):
        vals = [None] * len(items)
        for step in range(len(items) + lags[-1]):
            for stage, lag in zip(stages, lags):
                n = step - lag
                if 0 <= n < len(items):
                    vals[n] = stage(n, vals[n])
        return vals

    def select(n, _):
        h, j = items[n]
        qi = pair * nq + j
        qt = qt_ref[0, h, :, j * blk:(j + 1) * blk]
        gate = jnp.dot(kmean_ref[h, :, 0:hd], qt.astype(F32), precision=HIGHEST,
                       preferred_element_type=F32)
        g = jnp.where(row < qi, gate, neg)
        sel = jnp.zeros((nb, blk), F32)
        for _ in range(MOBA_TOPK):
            mx = jnp.max(g, axis=0, keepdims=True)
            first = jnp.min(jnp.where(g == mx, rowf, float(nb)), axis=0, keepdims=True)
            pick = rowf == jnp.where(mx > neg, first, -1.0)
            sel = jnp.where(pick, 1.0, sel)
            g = jnp.where(pick, neg, g)
        mask_prev = jnp.where(sel > 0.0, jnp.where(row == qi - 1, 0.0, NEG_BIG), NEG_BIG)
        mask_far = jnp.where(sel > 0.0, jnp.where(row < qi - 1, 0.0, NEG_BIG), NEG_BIG)
        qa_ref[n] = jnp.concatenate([qt, mask_far.astype(BF16), pad_hi], axis=0)
        return (jnp.concatenate([qt, pad], axis=0),
                jnp.concatenate([qt, mask_prev.astype(BF16), pad_hi], axis=0))

    def own_prev(n):
        h, j = items[n]
        qi = pair * nq + j
        return h, qi, jnp.maximum(qi - 1, 0)

    def near_scores(n, qa):
        h, qi, prev_j = own_prev(n)
        qa_own, qa_prev = qa
        s_own = jnp.dot(k_ref[0, h, qi], qa_own, preferred_element_type=F32)
        s_prev = jnp.dot(k_ref[0, h, prev_j], qa_prev, preferred_element_type=F32)
        return s_own, s_prev

    def near_softmax(n, ss):
        h = items[n][0]
        s_own, s_prev = ss
        s_own = jnp.where(key_i <= qry_i, s_own + bias_ref[h, :, blk:2 * blk], neg)
        s = jnp.concatenate([s_own, s_prev + bias_ref[h, :, 0:blk]], axis=0)
        m0 = jnp.max(s, axis=0, keepdims=True)
        return m0, jnp.exp2(s - m0)

    def near_pv(n, mp):
        h, qi, prev_j = own_prev(n)
        m0, p = mp
        pb = p.astype(BF16)
        acc = (jnp.dot(vt_ref[0, h, qi], pb[0:blk], preferred_element_type=F32)
               + jnp.dot(vt_ref[0, h, prev_j], pb[blk:2 * blk], preferred_element_type=F32))
        return m0, acc

    states = tuple(skewed([select, near_scores, near_softmax, near_pv], MOBA_NEAR_LAGS))

    def far(gi, states):
        j0 = gi * grp

        def qk(n, _):
            kt = k_ref[0, items[n][0], pl.ds(j0, grp)].reshape(grp * blk, LANES)
            return jnp.dot(kt, qa_ref[n], preferred_element_type=F32)

        def softmax(n, s):
            m_old = states[n][0]
            m_new = jnp.maximum(m_old, jnp.max(s, axis=0, keepdims=True))
            return m_new, jnp.exp2(m_old - m_new), jnp.exp2(s - m_new)

        def pv(n, sm):
            m_new, a, p = sm
            pb = p.astype(BF16)
            tot = a * states[n][1]
            for i in range(grp):
                tot = tot + jnp.dot(vt_ref[0, items[n][0], j0 + i], pb[i * blk:(i + 1) * blk],
                                    preferred_element_type=F32)
            return m_new, tot

        return tuple(skewed([qk, softmax, pv], MOBA_FAR_LAGS))

    states = lax.fori_loop(0, pair, far, tuple(states))

    for n, (h, j) in enumerate(items):
        acc = states[n][1]
        o_ref[0, h * hd:(h + 1) * hd, j * blk:(j + 1) * blk] = (
            acc[0:hd] / acc[hd:hd + 1]).astype(o_ref.dtype)


def _moba(qt, k_aug, vt, bias):
    bsz, nh, hd, s = qt.shape
    blk = MOBA_BLOCK
    hp = MOBA_HEADS_PER_STEP
    nq = MOBA_QBLOCKS_PER_STEP
    nblk = s // blk
    assert nh % hp == 0 and nblk % nq == 0 and nq == MOBA_FAR_GROUP
    return pl.pallas_call(
        _moba_kernel,
        grid=(bsz, nh // hp, nblk // nq),
        in_specs=[pl.BlockSpec((1, hp, hd, nq * blk), lambda b, g, i: (b, g, 0, i)),
                  pl.BlockSpec((1, hp, nblk, blk, LANES), lambda b, g, i: (b, g, 0, 0, 0),
                               pipeline_mode=pl.Buffered(1)),
                  pl.BlockSpec((1, hp, nblk, MOBA_VT_ROWS, blk), lambda b, g, i: (b, g, 0, 0, 0),
                               pipeline_mode=pl.Buffered(1)),
                  pl.BlockSpec((hp, blk, 2 * blk), lambda b, g, i: (g, 0, 0),
                               pipeline_mode=pl.Buffered(1))],
        out_specs=pl.BlockSpec((1, hp * hd, nq * blk), lambda b, g, i: (b, g, i)),
        out_shape=jax.ShapeDtypeStruct((bsz, nh * hd, s), BF16),
        scratch_shapes=[pltpu.VMEM((hp, MOBA_MAX_BLOCKS, LANES), F32),
                        pltpu.VMEM((hp * nq, LANES, blk), BF16)],
        compiler_params=_cparams(("arbitrary", "arbitrary", "arbitrary")),
        name="moba",
    )(qt, k_aug, vt, bias)


def _gla_kernel(q_ref, k_ref, v_ref, gk_ref, wup_ref, bgk_ref, r_ref, gn_ref, o_ref,
                state_ref, *, nchunk):
    ch = GLA_CHUNK
    tc = nchunk * ch

    @pl.when(pl.program_id(2) == 0)
    def _():
        state_ref[...] = jnp.zeros(state_ref.shape, F32)

    z = jnp.dot(gk_ref[...], wup_ref[...], precision=HIGHEST,
                preferred_element_type=F32) + bgk_ref[...]
    log_a = (jnp.minimum(z, 0.0) - jnp.log(1.0 + jnp.exp(-jnp.abs(z)))) / GLA_GATE_NORMALIZER
    rin = lax.broadcasted_iota(I32, (tc, GLA_DK), 0) & (ch - 1)
    b = log_a
    sh = 1
    while sh < ch:
        b = b + jnp.where(rin >= sh, pltpu.roll(b, sh, axis=0), 0.0)
        sh *= 2
    q = q_ref[...].astype(F32) * (GLA_DK ** -0.5)
    k = k_ref[...].astype(F32)
    q_g = (q * jnp.exp(b)).astype(BF16)
    k_g = (k * jnp.exp(-b)).astype(BF16)
    causal = (lax.broadcasted_iota(I32, (ch, ch), 1) <= lax.broadcasted_iota(I32, (ch, ch), 0))
    eye = (lax.broadcasted_iota(I32, (GLA_DK, GLA_DK), 0)
           == lax.broadcasted_iota(I32, (GLA_DK, GLA_DK), 1))
    b3 = b.reshape(nchunk, ch, GLA_DK)
    b_last = b3[:, ch - 1:ch, :]
    k_end = (k * jnp.exp(jnp.broadcast_to(b_last, b3.shape) - b3).reshape(tc, GLA_DK)
             ).astype(BF16)
    decay = jnp.exp(b_last)
    chunks = [slice(n * ch, (n + 1) * ch) for n in range(nchunk)]
    o_intra, kv, decay_col = [], [], []
    for n, sl in enumerate(chunks):
        v_c = v_ref[sl, :]
        att = jnp.where(causal, _nt_dot(q_g[sl], k_g[sl]), 0.0)
        o_intra.append(jnp.dot(att.astype(BF16), v_c, preferred_element_type=F32))
        kv.append(lax.dot_general(k_end[sl], v_c, (((0,), (0,)), ((), ())),
                                  preferred_element_type=F32))
        decay_col.append(jnp.sum(
            jnp.where(eye, jnp.broadcast_to(decay[n], (GLA_DK, GLA_DK)), 0.0),
            axis=1, keepdims=True))
    state = state_ref[...]
    o_inter = []
    for n, sl in enumerate(chunks):
        o_inter.append(jnp.dot(q_g[sl], state.astype(BF16), preferred_element_type=F32))
        state = decay_col[n] * state + kv[n]
    state_ref[...] = state
    o = jnp.concatenate([a + c for a, c in zip(o_intra, o_inter)], axis=0)
    o_ref[...] = (_rms(o) * gn_ref[...] * r_ref[...].astype(F32)).astype(o_ref.dtype)


def _gla(qkb, vb, gk, wup, bgk, r_act, gn, bsz, seq, nchunk):
    t = qkb.shape[0]
    tc = nchunk * GLA_CHUNK
    nc = seq // tc
    rowblk = lambda w, off: pl.BlockSpec((tc, w), lambda b, h, c: (b * nc + c, h + off))
    return pl.pallas_call(
        functools.partial(_gla_kernel, nchunk=nchunk),
        grid=(bsz, GLA_HEADS, nc),
        in_specs=[rowblk(GLA_DK, 0), rowblk(GLA_DK, GLA_HEADS), rowblk(GLA_DV, 0),
                  pl.BlockSpec((tc, LANES), lambda b, h, c: (b * nc + c, 0)),
                  pl.BlockSpec((LANES, GLA_DK), lambda b, h, c: (0, h)),
                  pl.BlockSpec((1, GLA_DK), lambda b, h, c: (0, h)),
                  rowblk(GLA_DV, 0),
                  pl.BlockSpec((1, GLA_DV), lambda b, h, c: (0, 0))],
        out_specs=rowblk(GLA_DV, 0),
        out_shape=jax.ShapeDtypeStruct((t, GLA_VALUE_DIM), BF16),
        scratch_shapes=[pltpu.VMEM((GLA_DK, GLA_DV), F32)],
        compiler_params=_cparams(("arbitrary", "arbitrary", "arbitrary")),
        name="gla",
    )(qkb, qkb, vb, gk, wup, bgk, r_act, gn)


def _merge_kernel(ya_ref, yb_ref, g_ref, x_ref, wpa_ref, wpb_ref, wout_ref, g1_ref,
                  nw_ref, sc_ref, sh_ref, wr_ref, br_ref,
                  x1_ref, h2_ref, idx_ref, wts_ref):
    pa = lax.dot_general(ya_ref[0], wpa_ref[...], (((0,), (0,)), ((), ())),
                         preferred_element_type=F32)
    pb = jnp.dot(yb_ref[...], wpb_ref[...], preferred_element_type=F32)
    mixed = (g_ref[:, 0:D_MODEL].astype(F32) * pa
             + g_ref[:, D_MODEL:2 * D_MODEL].astype(F32) * pb)
    y = jnp.dot(mixed.astype(BF16), wout_ref[...], preferred_element_type=F32)
    x1 = x_ref[...] + g1_ref[0] * y
    x1_ref[...] = x1
    h2 = _rms(x1) * nw_ref[...]
    h2 = h2 * (1.0 + sc_ref[0]) + sh_ref[0]
    _store_rows(h2_ref, _pack_halves(h2))
    logits = _nt_dot(wr_ref[...], h2, precision=HIGHEST) + br_ref[...]
    rowf = lax.broadcasted_iota(I32, logits.shape, 0).astype(F32)
    vals, idxs = [], []
    cur = logits
    for _ in range(TOP_K):
        mx = jnp.max(cur, axis=0, keepdims=True)
        first = jnp.min(jnp.where(cur == mx, rowf, float(N_EXPERTS)), axis=0, keepdims=True)
        vals.append(mx)
        idxs.append(first)
        cur = jnp.where(rowf == first, -jnp.inf, cur)
    es = [jnp.exp(v - vals[0]) for v in vals]
    tot = es[0]
    for e in es[1:]:
        tot = tot + e
    idx_ref[...] = jnp.concatenate(idxs, axis=0).astype(I32)
    wts_ref[...] = jnp.concatenate([e / tot for e in es], axis=0)


def _merge(ya, yb, gates, x2d, wpa, wpb, wout, g1, nw, sc, sh, wr_t, br, seq, tm):
    t, d = x2d.shape
    tpb = seq // tm
    row = lambda w: pl.BlockSpec((tm, w), lambda i: (i, 0))
    full = lambda a: pl.BlockSpec(a.shape, lambda i: (0,) * a.ndim)
    per_b = pl.BlockSpec((1, 1, d), lambda i: (i // tpb, 0, 0))
    colblk = pl.BlockSpec((TOP_K, tm), lambda i: (0, i))
    return pl.pallas_call(
        _merge_kernel,
        grid=(t // tm,),
        in_specs=[pl.BlockSpec((1, MOBA_WIDTH, tm), lambda i: (i // tpb, 0, i % tpb)),
                  row(GLA_VALUE_DIM), row(2 * D_MODEL), row(d),
                  full(wpa), full(wpb), full(wout), per_b,
                  pl.BlockSpec((1, d), lambda i: (0, 0)), per_b, per_b,
                  full(wr_t), full(br)],
        out_specs=[row(d), pl.BlockSpec((tm * ROW_SLABS, LANES), lambda i: (i, 0)),
                   colblk, colblk],
        out_shape=[jax.ShapeDtypeStruct((t, d), F32),
                   jax.ShapeDtypeStruct((t * ROW_SLABS, LANES), U32),
                   jax.ShapeDtypeStruct((TOP_K, t), I32), jax.ShapeDtypeStruct((TOP_K, t), F32)],
        compiler_params=_cparams(("arbitrary",)),
        name="merge",
    )(ya, yb, gates, x2d, wpa, wpb, wout, g1, nw.reshape(1, d), sc, sh, wr_t, br)


def _rank_kernel(idx_ref, rank_ref, cnt_ref, carry_ref):
    tm = idx_ref.shape[1]

    @pl.when(pl.program_id(0) == 0)
    def _():
        carry_ref[...] = jnp.zeros(carry_ref.shape, F32)

    rows = lax.broadcasted_iota(I32, (N_EXPERTS, tm), 0)
    before = (lax.broadcasted_iota(I32, (tm, tm), 0)
              < lax.broadcasted_iota(I32, (tm, tm), 1))
    upper = jnp.where(before, 1.0, 0.0).astype(BF16)
    carry = carry_ref[:, 0:1]
    ranks = []
    for k in range(TOP_K):
        onehot = idx_ref[k:k + 1, :] == rows
        onef = jnp.where(onehot, 1.0, 0.0)
        earlier = jnp.dot(onef.astype(BF16), upper, preferred_element_type=F32) + carry
        ranks.append(jnp.sum(jnp.where(onehot, earlier, 0.0), axis=0, keepdims=True))
        carry = carry + jnp.sum(onef, axis=1, keepdims=True)
    rank_ref[...] = jnp.concatenate(ranks, axis=0).astype(I32)
    total = jnp.broadcast_to(carry, carry_ref.shape)
    carry_ref[...] = total
    cnt_ref[...] = total


def _rank(idx, tm):
    t = idx.shape[1]
    return pl.pallas_call(
        _rank_kernel,
        grid=(t // tm,),
        in_specs=[pl.BlockSpec((TOP_K, tm), lambda i: (0, i))],
        out_specs=[pl.BlockSpec((TOP_K, tm), lambda i: (0, i)),
                   pl.BlockSpec((N_EXPERTS, LANES), lambda i: (0, 0))],
        out_shape=[jax.ShapeDtypeStruct((TOP_K, t), I32),
                   jax.ShapeDtypeStruct((N_EXPERTS, LANES), F32)],
        scratch_shapes=[pltpu.VMEM((N_EXPERTS, LANES), F32)],
        compiler_params=_cparams(("arbitrary",)),
        name="rank",
    )(idx)


def _dest_kernel(pstart_ref, idx_ref, rank_ref, dest_ref):
    idx = idx_ref[...]
    off = jnp.zeros(idx.shape, I32)
    for e in range(N_EXPERTS):
        off = jnp.where(idx == e, pstart_ref[e], off)
    dest_ref[...] = rank_ref[...] + off


def _dest(pstart, idx, rank, tm):
    t = idx.shape[1]
    blk = pl.BlockSpec((TOP_K, tm), lambda i: (0, i))
    return pl.pallas_call(
        _dest_kernel,
        grid=(t // tm,),
        in_specs=[pl.BlockSpec(memory_space=pltpu.SMEM), blk, blk],
        out_specs=blk,
        out_shape=jax.ShapeDtypeStruct((TOP_K, t), I32),
        compiler_params=_cparams(("arbitrary",)),
        name="dest",
    )(pstart, idx, rank)


def _scatter_kernel(dest_ref, zrow_ref, h_ref, xout_hbm, zbuf, sem, zsem, *, tm):
    @pl.when(pl.program_id(0) == 0)
    def _():
        zbuf[...] = jnp.zeros(zbuf.shape, zbuf.dtype)

        def zero_rows(row0):
            line0 = pl.multiple_of(row0 * ROW_SLABS, MOE_ROWS * ROW_SLABS)
            return pltpu.make_async_copy(
                zbuf, xout_hbm.at[pl.ds(line0, MOE_ROWS * ROW_SLABS)], zsem)

        for e in range(N_EXPERTS):
            zero_rows(zrow_ref[e]).start()
        for e in range(N_EXPERTS):
            zero_rows(zrow_ref[e]).wait()

        def start_tail(j, carry):
            zero_rows(j * MOE_ROWS).start()
            return carry

        def wait_tail(j, carry):
            zero_rows(j * MOE_ROWS).wait()
            return carry

        n_all = xout_hbm.shape[0] // (MOE_ROWS * ROW_SLABS)
        lax.fori_loop(zrow_ref[N_EXPERTS], n_all, start_tail, 0)
        lax.fori_loop(zrow_ref[N_EXPERTS], n_all, wait_tail, 0)

    def issue(g, carry):
        base = pl.multiple_of(g * ROW_UNROLL, ROW_UNROLL)
        for j in range(ROW_UNROLL):
            for k in range(TOP_K):
                d = dest_ref[(base + j) * TOP_K + k]
                pltpu.make_async_copy(_row(h_ref, base + j), _row(xout_hbm, d), sem).start(
                    priority=k % 2)
        return carry

    lax.fori_loop(0, tm // ROW_UNROLL, issue, 0)
    for k in range(TOP_K):
        pltpu.make_async_copy(h_ref, xout_hbm.at[pl.ds(0, tm * ROW_SLABS)], sem).wait()


def _scatter(dest_tok, last_block_row, h2, n_pad, tm):
    t = h2.shape[0] // ROW_SLABS
    return pl.pallas_call(
        functools.partial(_scatter_kernel, tm=tm),
        grid=(t // tm,),
        in_specs=[pl.BlockSpec((tm * TOP_K,), lambda i: (i,), memory_space=pltpu.SMEM),
                  pl.BlockSpec(memory_space=pltpu.SMEM),
                  pl.BlockSpec((tm * ROW_SLABS, LANES), lambda i: (i, 0))],
        out_specs=pl.BlockSpec(memory_space=pl.ANY),
        out_shape=jax.ShapeDtypeStruct((n_pad * ROW_SLABS, LANES), h2.dtype),
        scratch_shapes=[pltpu.VMEM((MOE_ROWS * ROW_SLABS, LANES), h2.dtype),
                        pltpu.SemaphoreType.DMA, pltpu.SemaphoreType.DMA],
        compiler_params=_cparams(("arbitrary",)),
        name="scatter",
    )(dest_tok, last_block_row, h2)


def _expert_kernel(be_ref, nused_ref, x_ref, wg_ref, bg_ref, wu_ref, bu_ref, wd_ref, bd_ref,
                   o_ref, wgb_ref, wub_ref, wdb_ref):
    i = pl.program_id(0)
    prev = be_ref[jnp.maximum(i - 1, 0)]

    @pl.when(jnp.logical_or(i == 0, be_ref[i] != prev))
    def _():
        wgb_ref[...] = wg_ref[0].astype(BF16)
        wub_ref[...] = wu_ref[0].astype(BF16)
        wdb_ref[...] = wd_ref[0].astype(BF16)

    @pl.when(i < nused_ref[0])
    def _():
        x_lo, x_hi = [v.astype(BF16) for v in _unpack_halves(_load_rows(x_ref))]
        half = x_lo.shape[1]

        def in_dot(w_ref):
            return (jnp.dot(x_lo, w_ref[0:half, :], preferred_element_type=F32)
                    + jnp.dot(x_hi, w_ref[half:2 * half, :], preferred_element_type=F32))

        g = jnp.minimum(in_dot(wgb_ref) + bg_ref[0], SWIGLU_LIMIT)
        u = jnp.clip(in_dot(wub_ref) + bu_ref[0], -SWIGLU_LIMIT, SWIGLU_LIMIT)
        act = g * _sigmoid(SWIGLU_ALPHA * g) * (u + 1.0)
        _store_rows(o_ref, _pack_halves(
            jnp.dot(act.astype(BF16), wdb_ref[...], preferred_element_type=F32) + bd_ref[0]))

    @pl.when(i >= nused_ref[0])
    def _():
        o_ref[...] = jnp.zeros(o_ref.shape, o_ref.dtype)


def _expert(block_expert, n_used, x_pad, wg, bg, wu, bu, wd, bd):
    lines = MOE_ROWS * ROW_SLABS
    _, d, f = wg.shape
    nblk = x_pad.shape[0] // lines
    wspec = lambda a, b: pl.BlockSpec((1, a, b), lambda i, be, nu: (be[i], 0, 0))
    grid_spec = pltpu.PrefetchScalarGridSpec(
        num_scalar_prefetch=2,
        grid=(nblk,),
        in_specs=[pl.BlockSpec((lines, LANES),
                               lambda i, be, nu: (jnp.minimum(i, nu[0] - 1), 0)),
                  wspec(d, f), wspec(1, f), wspec(d, f), wspec(1, f), wspec(f, d), wspec(1, d)],
        out_specs=pl.BlockSpec((lines, LANES), lambda i, be, nu: (i, 0)),
        scratch_shapes=[pltpu.VMEM((d, f), BF16), pltpu.VMEM((d, f), BF16),
                        pltpu.VMEM((f, d), BF16)])
    return pl.pallas_call(
        _expert_kernel,
        grid_spec=grid_spec,
        out_shape=jax.ShapeDtypeStruct(x_pad.shape, U32),
        compiler_params=_cparams(("arbitrary",)),
        name="expert",
    )(block_expert, n_used, x_pad, wg, bg.reshape(N_EXPERTS, 1, f), wu,
      bu.reshape(N_EXPERTS, 1, f), wd, bd.reshape(N_EXPERTS, 1, d))


def _combine_kernel(dest_ref, y_hbm, x1_ref, wts_ref, g2_ref, nf_ref, o_ref, ybuf, sem, *,
                    tm, final):
    def row_copy(tt, k):
        d = dest_ref[tt * TOP_K + k]
        return pltpu.make_async_copy(_row(y_hbm, d), _row(ybuf.at[k], tt), sem)

    def issue(g, carry):
        base = pl.multiple_of(g * ROW_UNROLL, ROW_UNROLL)
        for j in range(ROW_UNROLL):
            for k in range(TOP_K):
                row_copy(base + j, k).start(priority=k % 2)
        return carry

    lax.fori_loop(0, tm // ROW_UNROLL, issue, 0)
    for k in range(TOP_K):
        pltpu.make_async_copy(y_hbm.at[pl.ds(0, tm * ROW_SLABS)], ybuf.at[k], sem).wait()

    w = wts_ref[...]
    lo, hi = None, None
    for k in range(TOP_K):
        y_lo, y_hi = _unpack_halves(_load_rows(ybuf.at[k]))
        lo = w[:, k:k + 1] * y_lo + (0.0 if lo is None else lo)
        hi = w[:, k:k + 1] * y_hi + (0.0 if hi is None else hi)
    moe = jnp.concatenate([lo, hi], axis=1)
    x2 = x1_ref[...] + g2_ref[0] * moe
    o_ref[...] = _rms(x2) * nf_ref[...] if final else x2


def _combine(dest_tok, y_pad, x1, wts_tok, g2, nf, seq, tm, final):
    t, d = x1.shape
    tpb = seq // tm
    return pl.pallas_call(
        functools.partial(_combine_kernel, tm=tm, final=final),
        grid=(t // tm,),
        in_specs=[pl.BlockSpec((tm * TOP_K,), lambda i: (i,), memory_space=pltpu.SMEM),
                  pl.BlockSpec(memory_space=pl.ANY),
                  pl.BlockSpec((tm, d), lambda i: (i, 0)),
                  pl.BlockSpec((tm, TOP_K), lambda i: (i, 0)),
                  pl.BlockSpec((1, 1, d), lambda i: (i // tpb, 0, 0)),
                  pl.BlockSpec((1, d), lambda i: (0, 0))],
        out_specs=pl.BlockSpec((tm, d), lambda i: (i, 0)),
        out_shape=jax.ShapeDtypeStruct((t, d), F32),
        scratch_shapes=[pltpu.VMEM((TOP_K, tm * ROW_SLABS, LANES), y_pad.dtype),
                        pltpu.SemaphoreType.DMA],
        compiler_params=_cparams(("arbitrary",)),
        name="combine",
    )(dest_tok, y_pad, x1, wts_tok, g2, nf.reshape(1, d))


def _pick(n, cands):
    for c in cands:
        if n % c == 0:
            return c
    raise ValueError(f"no tile in {cands} divides {n}")


def kernel(x, c, rel_bias, w_ada, b_ada, norm_mix, w_in, w_gk_up, b_gk, gla_norm,
           w_proj_moba, w_proj_gla, w_out, norm_ffn, w_router, b_router,
           w_gate, b_gate, w_up, b_up, w_down, b_down, norm_final):
    bsz, seq, d = x.shape
    depth = w_ada.shape[0]
    assert d == D_MODEL and seq % MOBA_BLOCK == 0 and seq // MOBA_BLOCK <= MOBA_MAX_BLOCKS
    t = bsz * seq
    tm = _pick(seq, (512, 256))
    nchunk = _pick(seq // GLA_CHUNK, (8, 4))
    n_blk = seq // MOBA_BLOCK
    x2d = x.reshape(t, d)
    bias = _bias_tiles(rel_bias)
    per_b = lambda v: v.reshape(bsz, 1, d)

    for l in range(depth):
        mod = _ada(c, w_ada[l], b_ada[l])
        sh1, sc1, g1, sh2, sc2, g2 = [per_b(m) for m in jnp.split(mod, 6, axis=-1)]
        qt, k_aug, vt, qkb, vb, gk, r_act, gates = _inproj(
            x2d, norm_mix[l], sc1, sh1, _regroup_w_in(w_in[l]), bsz, seq, tm)
        ya = _moba(qt, k_aug, vt, bias)
        wup = jnp.pad(w_gk_up[l], ((0, LANES - GLA_GATE_RANK), (0, 0)))
        yb = _gla(qkb, vb, gk, wup, b_gk[l].reshape(1, -1), r_act,
                  gla_norm[l].reshape(1, -1), bsz, seq, nchunk)
        x1, h2, idx, wts = _merge(
            ya, yb, gates, x2d, w_proj_moba[l].astype(BF16), w_proj_gla[l].astype(BF16),
            w_out[l].astype(BF16), g1, norm_ffn[l], sc2, sh2,
            w_router[l].T, b_router[l].reshape(N_EXPERTS, 1), seq, tm)
        rank, cnt = _rank(idx, tm)
        counts = cnt[:, 0].astype(I32)
        padded = (counts + MOE_ROWS - 1) // MOE_ROWS * MOE_ROWS
        pcum = jnp.cumsum(padded)
        pstart = (pcum - padded).astype(I32)
        n_blocks = (t * TOP_K + MOE_ROWS - 1) // MOE_ROWS + N_EXPERTS
        block_row0 = jnp.arange(n_blocks, dtype=I32) * MOE_ROWS
        block_expert = jnp.minimum(
            jnp.sum((pcum[None, :] <= block_row0[:, None]).astype(I32), axis=1),
            N_EXPERTS - 1).astype(I32)
        n_used = (pcum[-1:] // MOE_ROWS).astype(I32)
        dest = _dest(pstart, idx, rank, tm)
        dest_tok = dest.T.reshape(t * TOP_K)
        tg = _pick(t, (256,))
        zero_info = jnp.concatenate([jnp.maximum(pcum - MOE_ROWS, 0).astype(I32), n_used])
        x_pad = _scatter(dest_tok, zero_info, h2, n_blocks * MOE_ROWS, tg)
        y_pad = _expert(block_expert, n_used, x_pad, w_gate[l], b_gate[l], w_up[l], b_up[l],
                        w_down[l], b_down[l])
        x2d = _combine(dest_tok, y_pad, x1, wts.T, g2, norm_final, seq, tg, l == depth - 1)
    return x2d.reshape(bsz, seq, d)
```

```python
import functools
import math

import numpy as np
import jax
import jax.numpy as jnp
from jax import lax
from jax.experimental import pallas as pl
from jax.experimental.pallas import tpu as pltpu

F32 = jnp.float32
BF16 = jnp.bfloat16
I32 = jnp.int32
HIGHEST = lax.Precision.HIGHEST

D_MODEL = 1024
MOBA_HEADS = 8
MOBA_HEAD_DIM = 64
MOBA_WIDTH = MOBA_HEADS * MOBA_HEAD_DIM
MOBA_BLOCK = 256
MOBA_TOPK = 3
MOBA_MAX_BLOCKS = 32
REL_BUCKETS = 32
REL_MAX_DIST = 128
GLA_HEADS = 4
GLA_KEY_DIM = D_MODEL // 2
GLA_VALUE_DIM = D_MODEL
GLA_DK = GLA_KEY_DIM // GLA_HEADS
GLA_DV = GLA_VALUE_DIM // GLA_HEADS
GLA_GATE_RANK = 16
GLA_GATE_NORMALIZER = 16.0
GLA_CHUNK = 64
N_EXPERTS = 32
TOP_K = 4
D_FF = D_MODEL
SWIGLU_ALPHA = 1.702
SWIGLU_LIMIT = 7.0
MOE_ROWS = 512
ROW_UNROLL = 8
EPS = 1e-6
LANES = 128
NEG_BIG = -1e30
LOG2E = math.log2(math.e)
VMEM_LIMIT = 56 * 1024 * 1024


def _cparams(sem, vmem=None):
    return pltpu.CompilerParams(dimension_semantics=sem,
                                vmem_limit_bytes=vmem or VMEM_LIMIT)


def _nt_dot(a, b, **kw):
    return lax.dot_general(a, b, (((1,), (1,)), ((), ())),
                           preferred_element_type=F32, **kw)


def _rms(x):
    return x * lax.rsqrt(jnp.mean(x * x, axis=-1, keepdims=True) + EPS)


def _sigmoid(x):
    return 1.0 / (1.0 + jnp.exp(-x))


U32 = jnp.uint32
_HI16 = 0xFFFF0000


def _pack_halves(x):
    n = x.shape[1] // 2
    lo = pltpu.bitcast(x[:, :n].astype(BF16).astype(F32), U32)
    hi = pltpu.bitcast(x[:, n:].astype(BF16).astype(F32), U32)
    return (hi & U32(_HI16)) | (lo >> 16)


def _unpack_halves(w):
    return (pltpu.bitcast(w << 16, F32), pltpu.bitcast(w & U32(_HI16), F32))


ROW_SLABS = D_MODEL // 2 // LANES


def _store_rows(ref, words):
    m = words.shape[0]
    for c in range(ROW_SLABS):
        ref[pl.ds(c, m, stride=ROW_SLABS), :] = words[:, c * LANES:(c + 1) * LANES]


def _load_rows(ref):
    m = ref.shape[0] // ROW_SLABS
    return jnp.concatenate(
        [ref[pl.ds(c, m, stride=ROW_SLABS), :] for c in range(ROW_SLABS)], axis=1)


def _row(ref, i):
    return ref.at[pl.ds(pl.multiple_of(i * ROW_SLABS, ROW_SLABS), ROW_SLABS)]


def _ada_kernel(c_ref, w_ref, b_ref, o_ref):
    c = c_ref[...]
    s = c * _sigmoid(c)
    o_ref[...] = jnp.dot(s, w_ref[...], precision=HIGHEST,
                         preferred_element_type=F32) + b_ref[...]


def _ada(c, w, b):
    bsz, d = c.shape
    n = w.shape[1]
    rows = -(-bsz // 8) * 8
    cp = jnp.zeros((rows, d), F32).at[:bsz].set(c)
    tn = 768
    out = pl.pallas_call(
        _ada_kernel,
        grid=(n // tn,),
        in_specs=[pl.BlockSpec((rows, d), lambda j: (0, 0)),
                  pl.BlockSpec((d, tn), lambda j: (0, j)),
                  pl.BlockSpec((1, tn), lambda j: (0, j))],
        out_specs=pl.BlockSpec((rows, tn), lambda j: (0, j)),
        out_shape=jax.ShapeDtypeStruct((rows, n), F32),
        compiler_params=_cparams(("arbitrary",)),
        name="ada",
    )(cp, w, b.reshape(1, n))
    return out[:bsz]


_OFF_QA = 0
_OFF_KA = _OFF_QA + MOBA_WIDTH
_OFF_VA = _OFF_KA + MOBA_HEADS * LANES
_OFF_QKB = _OFF_VA + MOBA_WIDTH
_OFF_VB = _OFF_QKB + 2 * GLA_KEY_DIM
_OFF_GK = _OFF_VB + GLA_VALUE_DIM
_OFF_R = _OFF_GK + LANES
_OFF_G = _OFF_R + GLA_VALUE_DIM
_W_CAT = _OFF_G + 2 * D_MODEL


def _regroup_w_in(w):
    d = w.shape[0]
    hd = MOBA_HEAD_DIM
    o_k, o_v = MOBA_WIDTH, 2 * MOBA_WIDTH
    o_qkb = 3 * MOBA_WIDTH
    o_gk = o_qkb + 2 * GLA_KEY_DIM + GLA_VALUE_DIM
    k_heads = w[:, o_k:o_v].reshape(d, MOBA_HEADS, hd)
    k_groups = jnp.pad(k_heads, ((0, 0), (0, 0), (0, LANES - hd))).reshape(d, -1)
    gk = jnp.pad(w[:, o_gk:o_gk + GLA_GATE_RANK], ((0, 0), (0, LANES - GLA_GATE_RANK)))
    return jnp.concatenate(
        [w[:, :o_k], k_groups, w[:, o_v:o_qkb], w[:, o_qkb:o_gk], gk,
         w[:, o_gk + GLA_GATE_RANK:]], axis=1).astype(BF16)


def _inproj_kernel(x_ref, nw_ref, sc_ref, sh_ref, w_ref,
                   qt_ref, ka_ref, vt_ref, qkb_ref, vb_ref, gk_ref, r_ref, g_ref, *, tpb):
    tm = x_ref.shape[0]
    hd = MOBA_HEAD_DIM
    nbt = tm // MOBA_BLOCK
    h = _rms(x_ref[...]) * nw_ref[...]
    h = h * (1.0 + sc_ref[0]) + sh_ref[0]
    hb = h.astype(BF16)

    def mm(a, b):
        return jnp.dot(hb, w_ref[:, a:b], preferred_element_type=F32)

    q_t = (mm(_OFF_QA, _OFF_KA) * (hd ** -0.5 * LOG2E)).T
    v_t = mm(_OFF_VA, _OFF_QKB).T
    blk0 = (pl.program_id(0) % tpb) * nbt
    lane = lax.broadcasted_iota(I32, (MOBA_BLOCK, LANES), 1)
    ones_rows = jnp.where(
        lax.broadcasted_iota(I32, (MOBA_VT_ROWS - hd, MOBA_BLOCK), 0) == 0, 1.0, 0.0)
    for hh in range(MOBA_HEADS):
        qt_ref[0, hh] = q_t[hh * hd:(hh + 1) * hd].astype(BF16)
        k_h = mm(_OFF_KA + hh * LANES, _OFF_KA + (hh + 1) * LANES)
        for j in range(nbt):
            rows = slice(j * MOBA_BLOCK, (j + 1) * MOBA_BLOCK)
            ka_ref[0, hh, j] = jnp.where(lane == hd + blk0 + j, 1.0, k_h[rows]).astype(BF16)
            vt_ref[0, hh, j, 0:hd, :] = v_t[hh * hd:(hh + 1) * hd, rows].astype(BF16)
            vt_ref[0, hh, j, hd:MOBA_VT_ROWS, :] = ones_rows.astype(BF16)
    qkb_ref[...] = mm(_OFF_QKB, _OFF_VB).astype(BF16)
    vb_ref[...] = mm(_OFF_VB, _OFF_GK).astype(BF16)
    gk_ref[...] = mm(_OFF_GK, _OFF_R)
    r = mm(_OFF_R, _OFF_G)
    r_ref[...] = (r * _sigmoid(r)).astype(BF16)
    g_ref[...] = _sigmoid(mm(_OFF_G, _W_CAT)).astype(BF16)


def _inproj(x2d, nw, sc, sh, w_cat, bsz, seq, tm):
    t, d = x2d.shape
    tpb = seq // tm
    nbt = tm // MOBA_BLOCK
    nh, hd = MOBA_HEADS, MOBA_HEAD_DIM
    row = lambda w: pl.BlockSpec((tm, w), lambda i: (i, 0))
    per_b = pl.BlockSpec((1, 1, d), lambda i: (i // tpb, 0, 0))
    rows_out = [(2 * GLA_KEY_DIM, BF16), (GLA_VALUE_DIM, BF16), (LANES, F32),
                (GLA_VALUE_DIM, BF16), (2 * D_MODEL, BF16)]
    return pl.pallas_call(
        functools.partial(_inproj_kernel, tpb=tpb),
        grid=(t // tm,),
        in_specs=[row(d), pl.BlockSpec((1, d), lambda i: (0, 0)), per_b, per_b,
                  pl.BlockSpec((d, _W_CAT), lambda i: (0, 0), pipeline_mode=pl.Buffered(1))],
        out_specs=[pl.BlockSpec((1, nh, hd, tm), lambda i: (i // tpb, 0, 0, i % tpb)),
                   pl.BlockSpec((1, nh, nbt, MOBA_BLOCK, LANES),
                                lambda i: (i // tpb, 0, i % tpb, 0, 0)),
                   pl.BlockSpec((1, nh, nbt, MOBA_VT_ROWS, MOBA_BLOCK),
                                lambda i: (i // tpb, 0, i % tpb, 0, 0))]
                  + [row(w) for w, _ in rows_out],
        out_shape=[jax.ShapeDtypeStruct((bsz, nh, hd, seq), BF16),
                   jax.ShapeDtypeStruct((bsz, nh, seq // MOBA_BLOCK, MOBA_BLOCK, LANES), BF16),
                   jax.ShapeDtypeStruct((bsz, nh, seq // MOBA_BLOCK, MOBA_VT_ROWS, MOBA_BLOCK),
                                        BF16)]
                  + [jax.ShapeDtypeStruct((t, w), dt) for w, dt in rows_out],
        compiler_params=_cparams(("arbitrary",)),
        name="inproj",
    )(x2d, nw.reshape(1, d), sc, sh, w_cat)


def _t5_bucket_np(n):
    n = np.maximum(n, 0)
    max_exact = REL_BUCKETS // 2
    nf = np.maximum(n, max_exact).astype(np.float32)
    large = max_exact + (np.log(nf / max_exact) / math.log(REL_MAX_DIST / max_exact)
                         * (REL_BUCKETS - max_exact)).astype(np.int32)
    large = np.minimum(large, REL_BUCKETS - 1)
    return np.where(n < max_exact, n, large).astype(np.int32)


def _bucket_table():
    kj = np.arange(MOBA_BLOCK)[:, None]
    qi = np.arange(2 * MOBA_BLOCK)[None, :] % MOBA_BLOCK
    prev = np.arange(2 * MOBA_BLOCK)[None, :] < MOBA_BLOCK
    return _t5_bucket_np(qi - kj + np.where(prev, MOBA_BLOCK, 0))


def _bias_kernel(rb_ref, bucket_ref, o_ref):
    h = pl.program_id(0)
    bk = bucket_ref[...]
    far = rb_ref[(REL_BUCKETS - 1) * MOBA_HEADS + h]
    acc = jnp.zeros(bk.shape, F32)
    for b in range(REL_BUCKETS):
        acc = jnp.where(bk == b, rb_ref[b * MOBA_HEADS + h] - far, acc)
    o_ref[0] = acc * LOG2E


def _bias_tiles(rel_bias):
    bucket = jnp.asarray(_bucket_table())
    return pl.pallas_call(
        _bias_kernel,
        grid=(MOBA_HEADS,),
        in_specs=[pl.BlockSpec(memory_space=pltpu.SMEM),
                  pl.BlockSpec(bucket.shape, lambda h: (0, 0))],
        out_specs=pl.BlockSpec((1,) + bucket.shape, lambda h: (h, 0, 0)),
        out_shape=jax.ShapeDtypeStruct((MOBA_HEADS,) + bucket.shape, F32),
        compiler_params=_cparams(("arbitrary",)),
        name="bias",
    )(rel_bias.reshape(-1), bucket)


MOBA_HEADS_PER_STEP = 8
MOBA_FAR_GROUP = 2
MOBA_QBLOCKS_PER_STEP = 2
MOBA_FAR_LAGS = (0, 3, 6)
MOBA_NEAR_LAGS = (0, 3, 6, 9)
MOBA_VT_ROWS = MOBA_HEAD_DIM + 16


def _moba_kernel(qt_ref, k_ref, vt_ref, bias_ref, o_ref, kmean_ref, qa_ref):
    blk = MOBA_BLOCK
    nb = MOBA_MAX_BLOCKS
    hd = MOBA_HEAD_DIM
    hp = MOBA_HEADS_PER_STEP
    grp = MOBA_FAR_GROUP
    nq = MOBA_QBLOCKS_PER_STEP
    pair = pl.program_id(2)
    neg = -jnp.inf
    items = [(h, j) for j in range(nq) for h in range(hp)]

    @pl.when(pair == 0)
    def _():
        kmean_ref[...] = jnp.zeros(kmean_ref.shape, F32)

    for h in range(hp):
        for j in range(nq):
            kmean_ref[h, pl.ds(pair * nq + j, 1), :] = jnp.mean(
                k_ref[0, h, pair * nq + j].astype(F32), axis=0, keepdims=True)

    row = lax.broadcasted_iota(I32, (nb, blk), 0)
    rowf = row.astype(F32)
    key_i = lax.broadcasted_iota(I32, (blk, blk), 0)
    qry_i = lax.broadcasted_iota(I32, (blk, blk), 1)
    pad = jnp.zeros((LANES - hd, blk), BF16)
    pad_hi = jnp.zeros((LANES - hd - nb, blk), BF16)

    def skewed(stages, lags):
        vals = [None] * len(items)
        for step in range(len(items) + lags[-1]):
            for stage, lag in zip(stages, lags):
                n = step - lag
                if 0 <= n < len(items):
                    vals[n] = stage(n, vals[n])
        return vals

    def select(n, _):
        h, j = items[n]
        qi = pair * nq + j
        qt = qt_ref[0, h, :, j * blk:(j + 1) * blk]
        gate = jnp.dot(kmean_ref[h, :, 0:hd], qt.astype(F32), precision=HIGHEST,
                       preferred_element_type=F32)
        g = jnp.where(row < qi, gate, neg)
        sel = jnp.zeros((nb, blk), F32)
        for _ in range(MOBA_TOPK):
            mx = jnp.max(g, axis=0, keepdims=True)
            first = jnp.min(jnp.where(g == mx, rowf, float(nb)), axis=0, keepdims=True)
            pick = rowf == jnp.where(mx > neg, first, -1.0)
            sel = jnp.where(pick, 1.0, sel)
            g = jnp.where(pick, neg, g)
        mask_prev = jnp.where(sel > 0.0, jnp.where(row == qi - 1, 0.0, NEG_BIG), NEG_BIG)
        mask_far = jnp.where(sel > 0.0, jnp.where(row < qi - 1, 0.0, NEG_BIG), NEG_BIG)
        qa_ref[n] = jnp.concatenate([qt, mask_far.astype(BF16), pad_hi], axis=0)
        return (jnp.concatenate([qt, pad], axis=0),
                jnp.concatenate([qt, mask_prev.astype(BF16), pad_hi], axis=0))

    def own_prev(n):
        h, j = items[n]
        qi = pair * nq + j
        return h, qi, jnp.maximum(qi - 1, 0)

    def near_scores(n, qa):
        h, qi, prev_j = own_prev(n)
        qa_own, qa_prev = qa
        s_own = jnp.dot(k_ref[0, h, qi], qa_own, preferred_element_type=F32)
        s_prev = jnp.dot(k_ref[0, h, prev_j], qa_prev, preferred_element_type=F32)
        return s_own, s_prev

    def near_softmax(n, ss):
        h = items[n][0]
        s_own, s_prev = ss
        s_own = jnp.where(key_i <= qry_i, s_own + bias_ref[h, :, blk:2 * blk], neg)
        s = jnp.concatenate([s_own, s_prev + bias_ref[h, :, 0:blk]], axis=0)
        m0 = jnp.max(s, axis=0, keepdims=True)
        return m0, jnp.exp2(s - m0)

    def near_pv(n, mp):
        h, qi, prev_j = own_prev(n)
        m0, p = mp
        pb = p.astype(BF16)
        acc = (jnp.dot(vt_ref[0, h, qi], pb[0:blk], preferred_element_type=F32)
               + jnp.dot(vt_ref[0, h, prev_j], pb[blk:2 * blk], preferred_element_type=F32))
        return m0, acc

    states = tuple(skewed([select, near_scores, near_softmax, near_pv], MOBA_NEAR_LAGS))

    def far(gi, states):
        j0 = gi * grp

        def qk(n, _):
            kt = k_ref[0, items[n][0], pl.ds(j0, grp)].reshape(grp * blk, LANES)
            return jnp.dot(kt, qa_ref[n], preferred_element_type=F32)

        def softmax(n, s):
            m_old = states[n][0]
            m_new = jnp.maximum(m_old, jnp.max(s, axis=0, keepdims=True))
            return m_new, jnp.exp2(m_old - m_new), jnp.exp2(s - m_new)

        def pv(n, sm):
            m_new, a, p = sm
            pb = p.astype(BF16)
            tot = a * states[n][1]
            for i in range(grp):
                tot = tot + jnp.dot(vt_ref[0, items[n][0], j0 + i], pb[i * blk:(i + 1) * blk],
                                    preferred_element_type=F32)
            return m_new, tot

        return tuple(skewed([qk, softmax, pv], MOBA_FAR_LAGS))

    states = lax.fori_loop(0, pair, far, tuple(states))

    for n, (h, j) in enumerate(items):
        acc = states[n][1]
        o_ref[0, h * hd:(h + 1) * hd, j * blk:(j + 1) * blk] = (
            acc[0:hd] / acc[hd:hd + 1]).astype(o_ref.dtype)


def _moba(qt, k_aug, vt, bias):
    bsz, nh, hd, s = qt.shape
    blk = MOBA_BLOCK
    hp = MOBA_HEADS_PER_STEP
    nq = MOBA_QBLOCKS_PER_STEP
    nblk = s // blk
    assert nh % hp == 0 and nblk % nq == 0 and nq == MOBA_FAR_GROUP
    return pl.pallas_call(
        _moba_kernel,
        grid=(bsz, nh // hp, nblk // nq),
        in_specs=[pl.BlockSpec((1, hp, hd, nq * blk), lambda b, g, i: (b, g, 0, i)),
                  pl.BlockSpec((1, hp, nblk, blk, LANES), lambda b, g, i: (b, g, 0, 0, 0),
                               pipeline_mode=pl.Buffered(1)),
                  pl.BlockSpec((1, hp, nblk, MOBA_VT_ROWS, blk), lambda b, g, i: (b, g, 0, 0, 0),
                               pipeline_mode=pl.Buffered(1)),
                  pl.BlockSpec((hp, blk, 2 * blk), lambda b, g, i: (g, 0, 0),
                               pipeline_mode=pl.Buffered(1))],
        out_specs=pl.BlockSpec((1, hp * hd, nq * blk), lambda b, g, i: (b, g, i)),
        out_shape=jax.ShapeDtypeStruct((bsz, nh * hd, s), BF16),
        scratch_shapes=[pltpu.VMEM((hp, MOBA_MAX_BLOCKS, LANES), F32),
                        pltpu.VMEM((hp * nq, LANES, blk), BF16)],
        compiler_params=_cparams(("arbitrary", "arbitrary", "arbitrary")),
        name="moba",
    )(qt, k_aug, vt, bias)


def _gla_kernel(q_ref, k_ref, v_ref, gk_ref, wup_ref, bgk_ref, r_ref, gn_ref, o_ref,
                state_ref, *, nchunk):
    ch = GLA_CHUNK
    tc = nchunk * ch
    dk, dv = GLA_DK, GLA_DV

    @pl.when(pl.program_id(1) == 0)
    def _():
        state_ref[...] = jnp.zeros(state_ref.shape, F32)

    rin = lax.broadcasted_iota(I32, (tc, dk), 0) & (ch - 1)
    causal = (lax.broadcasted_iota(I32, (ch, ch), 1) <= lax.broadcasted_iota(I32, (ch, ch), 0))
    eye = (lax.broadcasted_iota(I32, (dk, dk), 0) == lax.broadcasted_iota(I32, (dk, dk), 1))
    chunks = [slice(n * ch, (n + 1) * ch) for n in range(nchunk)]
    gk = gk_ref[...]

    def prep(h, _):
        ks = slice(h * dk, (h + 1) * dk)
        z = jnp.dot(gk, wup_ref[:, ks], precision=HIGHEST,
                    preferred_element_type=F32) + bgk_ref[:, ks]
        log_a = ((jnp.minimum(z, 0.0) - jnp.log(1.0 + jnp.exp(-jnp.abs(z))))
                 / GLA_GATE_NORMALIZER)
        b = log_a
        sh = 1
        while sh < ch:
            b = b + jnp.where(rin >= sh, pltpu.roll(b, sh, axis=0), 0.0)
            sh *= 2
        q = q_ref[:, ks].astype(F32) * (dk ** -0.5)
        k = k_ref[:, ks].astype(F32)
        q_g = (q * jnp.exp(b)).astype(BF16)
        k_g = (k * jnp.exp(-b)).astype(BF16)
        b3 = b.reshape(nchunk, ch, dk)
        b_last = b3[:, ch - 1:ch, :]
        k_end = (k * jnp.exp(jnp.broadcast_to(b_last, b3.shape) - b3).reshape(tc, dk)
                 ).astype(BF16)
        return q_g, k_g, k_end, jnp.exp(b_last)

    def local(h, pre):
        q_g, k_g, k_end, decay = pre
        o_intra, kv, decay_col = [], [], []
        for n, sl in enumerate(chunks):
            v_c = v_ref[sl, h * dv:(h + 1) * dv]
            att = jnp.where(causal, _nt_dot(q_g[sl], k_g[sl]), 0.0)
            o_intra.append(jnp.dot(att.astype(BF16), v_c, preferred_element_type=F32))
            kv.append(lax.dot_general(k_end[sl], v_c, (((0,), (0,)), ((), ())),
                                      preferred_element_type=F32))
            decay_col.append(jnp.sum(
                jnp.where(eye, jnp.broadcast_to(decay[n], (dk, dk)), 0.0),
                axis=1, keepdims=True))
        return q_g, o_intra, kv, decay_col

    def chain(h, loc):
        q_g, o_intra, kv, decay_col = loc
        state = state_ref[h]
        outs = []
        for n, sl in enumerate(chunks):
            outs.append(o_intra[n] + jnp.dot(q_g[sl], state.astype(BF16),
                                             preferred_element_type=F32))
            state = decay_col[n] * state + kv[n]
        state_ref[h] = state
        return jnp.concatenate(outs, axis=0)

    def finish(h, o):
        vs = slice(h * dv, (h + 1) * dv)
        o_ref[:, vs] = (_rms(o) * gn_ref[...] * r_ref[:, vs].astype(F32)).astype(o_ref.dtype)
        return None

    stages = [prep, local, chain, finish]
    vals = [None] * GLA_HEADS
    for step in range(GLA_HEADS + len(stages) - 1):
        for si, stage in enumerate(stages):
            h = step - si
            if 0 <= h < GLA_HEADS:
                vals[h] = stage(h, vals[h])


def _gla(qkb, vb, gk, wup, bgk, r_act, gn, bsz, seq, nchunk):
    t = qkb.shape[0]
    tc = nchunk * GLA_CHUNK
    nc = seq // tc
    rowblk = lambda w, off: pl.BlockSpec((tc, w), lambda b, c: (b * nc + c, off))
    full = lambda a: pl.BlockSpec(a.shape, lambda b, c: (0, 0))
    return pl.pallas_call(
        functools.partial(_gla_kernel, nchunk=nchunk),
        grid=(bsz, nc),
        in_specs=[rowblk(GLA_KEY_DIM, 0), rowblk(GLA_KEY_DIM, 1), rowblk(GLA_VALUE_DIM, 0),
                  rowblk(LANES, 0), full(wup), full(bgk), rowblk(GLA_VALUE_DIM, 0), full(gn)],
        out_specs=rowblk(GLA_VALUE_DIM, 0),
        out_shape=jax.ShapeDtypeStruct((t, GLA_VALUE_DIM), BF16),
        scratch_shapes=[pltpu.VMEM((GLA_HEADS, GLA_DK, GLA_DV), F32)],
        compiler_params=_cparams(("arbitrary", "arbitrary")),
        name="gla",
    )(qkb, qkb, vb, gk, wup, bgk, r_act, gn)


def _merge_kernel(ya_ref, yb_ref, g_ref, x_ref, wpa_ref, wpb_ref, wout_ref, g1_ref,
                  nw_ref, sc_ref, sh_ref, wr_ref, br_ref,
                  x1_ref, h2_ref, idx_ref, wts_ref):
    pa = lax.dot_general(ya_ref[0], wpa_ref[...], (((0,), (0,)), ((), ())),
                         preferred_element_type=F32)
    pb = jnp.dot(yb_ref[...], wpb_ref[...], preferred_element_type=F32)
    mixed = (g_ref[:, 0:D_MODEL].astype(F32) * pa
             + g_ref[:, D_MODEL:2 * D_MODEL].astype(F32) * pb)
    y = jnp.dot(mixed.astype(BF16), wout_ref[...], preferred_element_type=F32)
    x1 = x_ref[...] + g1_ref[0] * y
    x1_ref[...] = x1
    h2 = _rms(x1) * nw_ref[...]
    h2 = h2 * (1.0 + sc_ref[0]) + sh_ref[0]
    _store_rows(h2_ref, _pack_halves(h2))
    logits = _nt_dot(wr_ref[...], h2, precision=HIGHEST) + br_ref[...]
    rowf = lax.broadcasted_iota(I32, logits.shape, 0).astype(F32)
    vals, idxs = [], []
    cur = logits
    for _ in range(TOP_K):
        mx = jnp.max(cur, axis=0, keepdims=True)
        first = jnp.min(jnp.where(cur == mx, rowf, float(N_EXPERTS)), axis=0, keepdims=True)
        vals.append(mx)
        idxs.append(first)
        cur = jnp.where(rowf == first, -jnp.inf, cur)
    es = [jnp.exp(v - vals[0]) for v in vals]
    tot = es[0]
    for e in es[1:]:
        tot = tot + e
    idx_ref[...] = jnp.concatenate(idxs, axis=0).astype(I32)
    wts_ref[...] = jnp.concatenate([e / tot for e in es], axis=0)


def _merge(ya, yb, gates, x2d, wpa, wpb, wout, g1, nw, sc, sh, wr_t, br, seq, tm):
    t, d = x2d.shape
    tpb = seq // tm
    row = lambda w: pl.BlockSpec((tm, w), lambda i: (i, 0))
    full = lambda a: pl.BlockSpec(a.shape, lambda i: (0,) * a.ndim)
    per_b = pl.BlockSpec((1, 1, d), lambda i: (i // tpb, 0, 0))
    colblk = pl.BlockSpec((TOP_K, tm), lambda i: (0, i))
    return pl.pallas_call(
        _merge_kernel,
        grid=(t // tm,),
        in_specs=[pl.BlockSpec((1, MOBA_WIDTH, tm), lambda i: (i // tpb, 0, i % tpb)),
                  row(GLA_VALUE_DIM), row(2 * D_MODEL), row(d),
                  full(wpa), full(wpb), full(wout), per_b,
                  pl.BlockSpec((1, d), lambda i: (0, 0)), per_b, per_b,
                  full(wr_t), full(br)],
        out_specs=[row(d), pl.BlockSpec((tm * ROW_SLABS, LANES), lambda i: (i, 0)),
                   colblk, colblk],
        out_shape=[jax.ShapeDtypeStruct((t, d), F32),
                   jax.ShapeDtypeStruct((t * ROW_SLABS, LANES), U32),
                   jax.ShapeDtypeStruct((TOP_K, t), I32), jax.ShapeDtypeStruct((TOP_K, t), F32)],
        compiler_params=_cparams(("arbitrary",)),
        name="merge",
    )(ya, yb, gates, x2d, wpa, wpb, wout, g1, nw.reshape(1, d), sc, sh, wr_t, br)


def _rank_kernel(idx_ref, rank_ref, cnt_ref, carry_ref):
    tm = idx_ref.shape[1]

    @pl.when(pl.program_id(0) == 0)
    def _():
        carry_ref[...] = jnp.zeros(carry_ref.shape, F32)

    rows = lax.broadcasted_iota(I32, (N_EXPERTS, tm), 0)
    before = (lax.broadcasted_iota(I32, (tm, tm), 0)
              < lax.broadcasted_iota(I32, (tm, tm), 1))
    upper = jnp.where(before, 1.0, 0.0).astype(BF16)
    carry = carry_ref[:, 0:1]
    ranks = []
    for k in range(TOP_K):
        onehot = idx_ref[k:k + 1, :] == rows
        onef = jnp.where(onehot, 1.0, 0.0)
        earlier = jnp.dot(onef.astype(BF16), upper, preferred_element_type=F32) + carry
        ranks.append(jnp.sum(jnp.where(onehot, earlier, 0.0), axis=0, keepdims=True))
        carry = carry + jnp.sum(onef, axis=1, keepdims=True)
    rank_ref[...] = jnp.concatenate(ranks, axis=0).astype(I32)
    total = jnp.broadcast_to(carry, carry_ref.shape)
    carry_ref[...] = total
    cnt_ref[...] = total


def _rank(idx, tm):
    t = idx.shape[1]
    return pl.pallas_call(
        _rank_kernel,
        grid=(t // tm,),
        in_specs=[pl.BlockSpec((TOP_K, tm), lambda i: (0, i))],
        out_specs=[pl.BlockSpec((TOP_K, tm), lambda i: (0, i)),
                   pl.BlockSpec((N_EXPERTS, LANES), lambda i: (0, 0))],
        out_shape=[jax.ShapeDtypeStruct((TOP_K, t), I32),
                   jax.ShapeDtypeStruct((N_EXPERTS, LANES), F32)],
        scratch_shapes=[pltpu.VMEM((N_EXPERTS, LANES), F32)],
        compiler_params=_cparams(("arbitrary",)),
        name="rank",
    )(idx)


def _dest_kernel(pstart_ref, idx_ref, rank_ref, dest_ref):
    idx = idx_ref[...]
    off = jnp.zeros(idx.shape, I32)
    for e in range(N_EXPERTS):
        off = jnp.where(idx == e, pstart_ref[e], off)
    dest_ref[...] = rank_ref[...] + off


def _dest(pstart, idx, rank, tm):
    t = idx.shape[1]
    blk = pl.BlockSpec((TOP_K, tm), lambda i: (0, i))
    return pl.pallas_call(
        _dest_kernel,
        grid=(t // tm,),
        in_specs=[pl.BlockSpec(memory_space=pltpu.SMEM), blk, blk],
        out_specs=blk,
        out_shape=jax.ShapeDtypeStruct((TOP_K, t), I32),
        compiler_params=_cparams(("arbitrary",)),
        name="dest",
    )(pstart, idx, rank)


def _scatter_kernel(dest_ref, zrow_ref, h_ref, xout_hbm, zbuf, stage, sems, zsem, *, tm):
    @pl.when(pl.program_id(0) == 0)
    def _():
        zbuf[...] = jnp.zeros(zbuf.shape, zbuf.dtype)

        def zero_rows(row0):
            line0 = pl.multiple_of(row0 * ROW_SLABS, MOE_ROWS * ROW_SLABS)
            return pltpu.make_async_copy(
                zbuf, xout_hbm.at[pl.ds(line0, MOE_ROWS * ROW_SLABS)], zsem)

        for e in range(N_EXPERTS):
            zero_rows(zrow_ref[e]).start()
        for e in range(N_EXPERTS):
            zero_rows(zrow_ref[e]).wait()

        def start_tail(j, carry):
            zero_rows(j * MOE_ROWS).start()
            return carry

        def wait_tail(j, carry):
            zero_rows(j * MOE_ROWS).wait()
            return carry

        n_all = xout_hbm.shape[0] // (MOE_ROWS * ROW_SLABS)
        lax.fori_loop(zrow_ref[N_EXPERTS], n_all, start_tail, 0)
        lax.fori_loop(zrow_ref[N_EXPERTS], n_all, wait_tail, 0)

    i = pl.program_id(0)
    slot = i % 2
    src = stage.at[slot]
    src[...] = h_ref[...]

    def issue(g, carry):
        base = pl.multiple_of(g * ROW_UNROLL, ROW_UNROLL)
        for j in range(ROW_UNROLL):
            for k in range(TOP_K):
                d = dest_ref[(base + j) * TOP_K + k]
                pltpu.make_async_copy(_row(src, base + j), _row(xout_hbm, d),
                                      sems.at[slot]).start(priority=k % 2)
        return carry

    lax.fori_loop(0, tm // ROW_UNROLL, issue, 0)

    def wait_tile(s):
        for k in range(TOP_K):
            pltpu.make_async_copy(stage.at[s], xout_hbm.at[pl.ds(0, tm * ROW_SLABS)],
                                  sems.at[s]).wait()

    @pl.when(i > 0)
    def _():
        wait_tile(1 - slot)

    @pl.when(i == pl.num_programs(0) - 1)
    def _():
        wait_tile(slot)


def _scatter(dest_tok, last_block_row, h2, n_pad, tm):
    t = h2.shape[0] // ROW_SLABS
    return pl.pallas_call(
        functools.partial(_scatter_kernel, tm=tm),
        grid=(t // tm,),
        in_specs=[pl.BlockSpec((tm * TOP_K,), lambda i: (i,), memory_space=pltpu.SMEM),
                  pl.BlockSpec(memory_space=pltpu.SMEM),
                  pl.BlockSpec((tm * ROW_SLABS, LANES), lambda i: (i, 0))],
        out_specs=pl.BlockSpec(memory_space=pl.ANY),
        out_shape=jax.ShapeDtypeStruct((n_pad * ROW_SLABS, LANES), h2.dtype),
        scratch_shapes=[pltpu.VMEM((MOE_ROWS * ROW_SLABS, LANES), h2.dtype),
                        pltpu.VMEM((2, tm * ROW_SLABS, LANES), h2.dtype),
                        pltpu.SemaphoreType.DMA((2,)), pltpu.SemaphoreType.DMA],
        compiler_params=_cparams(("arbitrary",)),
        name="scatter",
    )(dest_tok, last_block_row, h2)


def _expert_kernel(be_ref, nused_ref, x_ref, wg_ref, bg_ref, wu_ref, bu_ref, wd_ref, bd_ref,
                   o_ref, wgb_ref, wub_ref, wdb_ref):
    i = pl.program_id(0)
    prev = be_ref[jnp.maximum(i - 1, 0)]

    @pl.when(jnp.logical_or(i == 0, be_ref[i] != prev))
    def _():
        wgb_ref[...] = wg_ref[0].astype(BF16)
        wub_ref[...] = wu_ref[0].astype(BF16)
        wdb_ref[...] = wd_ref[0].astype(BF16)

    @pl.when(i < nused_ref[0])
    def _():
        x_lo, x_hi = [v.astype(BF16) for v in _unpack_halves(_load_rows(x_ref))]
        half = x_lo.shape[1]

        def in_dot(w_ref):
            return (jnp.dot(x_lo, w_ref[0:half, :], preferred_element_type=F32)
                    + jnp.dot(x_hi, w_ref[half:2 * half, :], preferred_element_type=F32))

        g = jnp.minimum(in_dot(wgb_ref) + bg_ref[0], SWIGLU_LIMIT)
        u = jnp.clip(in_dot(wub_ref) + bu_ref[0], -SWIGLU_LIMIT, SWIGLU_LIMIT)
        act = g * _sigmoid(SWIGLU_ALPHA * g) * (u + 1.0)
        _store_rows(o_ref, _pack_halves(
            jnp.dot(act.astype(BF16), wdb_ref[...], preferred_element_type=F32) + bd_ref[0]))

    @pl.when(i >= nused_ref[0])
    def _():
        o_ref[...] = jnp.zeros(o_ref.shape, o_ref.dtype)


def _expert(block_expert, n_used, x_pad, wg, bg, wu, bu, wd, bd):
    lines = MOE_ROWS * ROW_SLABS
    _, d, f = wg.shape
    nblk = x_pad.shape[0] // lines
    wspec = lambda a, b: pl.BlockSpec((1, a, b), lambda i, be, nu: (be[i], 0, 0))
    grid_spec = pltpu.PrefetchScalarGridSpec(
        num_scalar_prefetch=2,
        grid=(nblk,),
        in_specs=[pl.BlockSpec((lines, LANES),
                               lambda i, be, nu: (jnp.minimum(i, nu[0] - 1), 0)),
                  wspec(d, f), wspec(1, f), wspec(d, f), wspec(1, f), wspec(f, d), wspec(1, d)],
        out_specs=pl.BlockSpec((lines, LANES), lambda i, be, nu: (i, 0)),
        scratch_shapes=[pltpu.VMEM((d, f), BF16), pltpu.VMEM((d, f), BF16),
                        pltpu.VMEM((f, d), BF16)])
    return pl.pallas_call(
        _expert_kernel,
        grid_spec=grid_spec,
        out_shape=jax.ShapeDtypeStruct(x_pad.shape, U32),
        compiler_params=_cparams(("arbitrary",)),
        name="expert",
    )(block_expert, n_used, x_pad, wg, bg.reshape(N_EXPERTS, 1, f), wu,
      bu.reshape(N_EXPERTS, 1, f), wd, bd.reshape(N_EXPERTS, 1, d))


def _combine_kernel(dest_ref, y_hbm, x1_ref, wts_ref, g2_ref, nf_ref, o_ref, ybuf, sems, *,
                    tm, final):
    i = pl.program_id(0)
    n_tiles = pl.num_programs(0) - 1
    slot = i % 2

    @pl.when(i < n_tiles)
    def _():
        def row_copy(tt, k):
            d = dest_ref[tt * TOP_K + k]
            return pltpu.make_async_copy(_row(y_hbm, d), _row(ybuf.at[slot, k], tt),
                                         sems.at[slot])

        def issue(g, carry):
            base = pl.multiple_of(g * ROW_UNROLL, ROW_UNROLL)
            for j in range(ROW_UNROLL):
                for k in range(TOP_K):
                    row_copy(base + j, k).start(priority=k % 2)
            return carry

        lax.fori_loop(0, tm // ROW_UNROLL, issue, 0)

    @pl.when(i > 0)
    def _():
        done = 1 - slot
        for k in range(TOP_K):
            pltpu.make_async_copy(y_hbm.at[pl.ds(0, tm * ROW_SLABS)], ybuf.at[done, k],
                                  sems.at[done]).wait()
        w = wts_ref[...]
        lo, hi = None, None
        for k in range(TOP_K):
            y_lo, y_hi = _unpack_halves(_load_rows(ybuf.at[done, k]))
            lo = w[:, k:k + 1] * y_lo + (0.0 if lo is None else lo)
            hi = w[:, k:k + 1] * y_hi + (0.0 if hi is None else hi)
        moe = jnp.concatenate([lo, hi], axis=1)
        x2 = x1_ref[...] + g2_ref[0] * moe
        o_ref[...] = _rms(x2) * nf_ref[...] if final else x2


def _combine(dest_tok, y_pad, x1, wts_tok, g2, nf, seq, tm, final):
    t, d = x1.shape
    tpb = seq // tm
    n_tiles = t // tm
    lag = lambda i: jnp.maximum(i - 1, 0)
    return pl.pallas_call(
        functools.partial(_combine_kernel, tm=tm, final=final),
        grid=(n_tiles + 1,),
        in_specs=[pl.BlockSpec((tm * TOP_K,), lambda i: (jnp.minimum(i, n_tiles - 1),),
                               memory_space=pltpu.SMEM),
                  pl.BlockSpec(memory_space=pl.ANY),
                  pl.BlockSpec((tm, d), lambda i: (lag(i), 0)),
                  pl.BlockSpec((tm, TOP_K), lambda i: (lag(i), 0)),
                  pl.BlockSpec((1, 1, d), lambda i: (lag(i) // tpb, 0, 0)),
                  pl.BlockSpec((1, d), lambda i: (0, 0))],
        out_specs=pl.BlockSpec((tm, d), lambda i: (lag(i), 0)),
        out_shape=jax.ShapeDtypeStruct((t, d), F32),
        scratch_shapes=[pltpu.VMEM((2, TOP_K, tm * ROW_SLABS, LANES), y_pad.dtype),
                        pltpu.SemaphoreType.DMA((2,))],
        compiler_params=_cparams(("arbitrary",)),
        name="combine",
    )(dest_tok, y_pad, x1, wts_tok, g2, nf.reshape(1, d))


def _pick(n, cands):
    for c in cands:
        if n % c == 0:
            return c
    raise ValueError(f"no tile in {cands} divides {n}")


def kernel(x, c, rel_bias, w_ada, b_ada, norm_mix, w_in, w_gk_up, b_gk, gla_norm,
           w_proj_moba, w_proj_gla, w_out, norm_ffn, w_router, b_router,
           w_gate, b_gate, w_up, b_up, w_down, b_down, norm_final):
    bsz, seq, d = x.shape
    depth = w_ada.shape[0]
    assert d == D_MODEL and seq % MOBA_BLOCK == 0 and seq // MOBA_BLOCK <= MOBA_MAX_BLOCKS
    t = bsz * seq
    tm = _pick(seq, (512, 256))
    nchunk = _pick(seq // GLA_CHUNK, (8, 4))
    n_blk = seq // MOBA_BLOCK
    x2d = x.reshape(t, d)
    bias = _bias_tiles(rel_bias)
    per_b = lambda v: v.reshape(bsz, 1, d)

    for l in range(depth):
        mod = _ada(c, w_ada[l], b_ada[l])
        sh1, sc1, g1, sh2, sc2, g2 = [per_b(m) for m in jnp.split(mod, 6, axis=-1)]
        qt, k_aug, vt, qkb, vb, gk, r_act, gates = _inproj(
            x2d, norm_mix[l], sc1, sh1, _regroup_w_in(w_in[l]), bsz, seq, tm)
        ya = _moba(qt, k_aug, vt, bias)
        wup = jnp.pad(w_gk_up[l], ((0, LANES - GLA_GATE_RANK), (0, 0)))
        yb = _gla(qkb, vb, gk, wup, b_gk[l].reshape(1, -1), r_act,
                  gla_norm[l].reshape(1, -1), bsz, seq, nchunk)
        x1, h2, idx, wts = _merge(
            ya, yb, gates, x2d, w_proj_moba[l].astype(BF16), w_proj_gla[l].astype(BF16),
            w_out[l].astype(BF16), g1, norm_ffn[l], sc2, sh2,
            w_router[l].T, b_router[l].reshape(N_EXPERTS, 1), seq, tm)
        rank, cnt = _rank(idx, tm)
        counts = cnt[:, 0].astype(I32)
        padded = (counts + MOE_ROWS - 1) // MOE_ROWS * MOE_ROWS
        pcum = jnp.cumsum(padded)
        pstart = (pcum - padded).astype(I32)
        n_blocks = (t * TOP_K + MOE_ROWS - 1) // MOE_ROWS + N_EXPERTS
        block_row0 = jnp.arange(n_blocks, dtype=I32) * MOE_ROWS
        block_expert = jnp.minimum(
            jnp.sum((pcum[None, :] <= block_row0[:, None]).astype(I32), axis=1),
            N_EXPERTS - 1).astype(I32)
        n_used = (pcum[-1:] // MOE_ROWS).astype(I32)
        dest = _dest(pstart, idx, rank, tm)
        dest_tok = dest.T.reshape(t * TOP_K)
        tg = _pick(t, (256,))
        zero_info = jnp.concatenate([jnp.maximum(pcum - MOE_ROWS, 0).astype(I32), n_used])
        x_pad = _scatter(dest_tok, zero_info, h2, n_blocks * MOE_ROWS, tg)
        y_pad = _expert(block_expert, n_used, x_pad, w_gate[l], b_gate[l], w_up[l], b_up[l],
                        w_down[l], b_down[l])
        x2d = _combine(dest_tok, y_pad, x1, wts.T, g2, norm_final, seq, tg, l == depth - 1)
    return x2d.reshape(bsz, seq, d)
```

```python
import functools
import math

import numpy as np
import jax
import jax.numpy as jnp
from jax import lax
from jax.experimental import pallas as pl
from jax.experimental.pallas import tpu as pltpu

F32 = jnp.float32
BF16 = jnp.bfloat16
I32 = jnp.int32
HIGHEST = lax.Precision.HIGHEST

D_MODEL = 1024
MOBA_HEADS = 8
MOBA_HEAD_DIM = 64
MOBA_WIDTH = MOBA_HEADS * MOBA_HEAD_DIM
MOBA_BLOCK = 256
MOBA_TOPK = 3
MOBA_MAX_BLOCKS = 32
REL_BUCKETS = 32
REL_MAX_DIST = 128
GLA_HEADS = 4
GLA_KEY_DIM = D_MODEL // 2
GLA_VALUE_DIM = D_MODEL
GLA_DK = GLA_KEY_DIM // GLA_HEADS
GLA_DV = GLA_VALUE_DIM // GLA_HEADS
GLA_GATE_RANK = 16
GLA_GATE_NORMALIZER = 16.0
GLA_CHUNK = 64
N_EXPERTS = 32
TOP_K = 4
D_FF = D_MODEL
SWIGLU_ALPHA = 1.702
SWIGLU_LIMIT = 7.0
MOE_ROWS = 512
ROW_UNROLL = 8
EPS = 1e-6
LANES = 128
NEG_BIG = -1e30
LOG2E = math.log2(math.e)
VMEM_LIMIT = 56 * 1024 * 1024


def _cparams(sem, vmem=None):
    return pltpu.CompilerParams(dimension_semantics=sem,
                                vmem_limit_bytes=vmem or VMEM_LIMIT)


def _nt_dot(a, b, **kw):
    return lax.dot_general(a, b, (((1,), (1,)), ((), ())),
                           preferred_element_type=F32, **kw)


def _rms(x):
    return x * lax.rsqrt(jnp.mean(x * x, axis=-1, keepdims=True) + EPS)


def _sigmoid(x):
    return 1.0 / (1.0 + jnp.exp(-x))


U32 = jnp.uint32
_HI16 = 0xFFFF0000


def _pack_halves(x):
    n = x.shape[1] // 2
    lo = pltpu.bitcast(x[:, :n].astype(BF16).astype(F32), U32)
    hi = pltpu.bitcast(x[:, n:].astype(BF16).astype(F32), U32)
    return (hi & U32(_HI16)) | (lo >> 16)


def _unpack_halves(w):
    return (pltpu.bitcast(w << 16, F32), pltpu.bitcast(w & U32(_HI16), F32))


ROW_SLABS = D_MODEL // 2 // LANES


def _store_rows(ref, words):
    m = words.shape[0]
    for c in range(ROW_SLABS):
        ref[pl.ds(c, m, stride=ROW_SLABS), :] = words[:, c * LANES:(c + 1) * LANES]


def _load_rows(ref):
    m = ref.shape[0] // ROW_SLABS
    return jnp.concatenate(
        [ref[pl.ds(c, m, stride=ROW_SLABS), :] for c in range(ROW_SLABS)], axis=1)


def _row(ref, i):
    return ref.at[pl.ds(pl.multiple_of(i * ROW_SLABS, ROW_SLABS), ROW_SLABS)]


def _ada_kernel(c_ref, w_ref, b_ref, o_ref):
    c = c_ref[...]
    s = c * _sigmoid(c)
    o_ref[...] = jnp.dot(s, w_ref[...], precision=HIGHEST,
                         preferred_element_type=F32) + b_ref[...]


def _ada(c, w, b):
    bsz, d = c.shape
    n = w.shape[1]
    rows = -(-bsz // 8) * 8
    cp = jnp.zeros((rows, d), F32).at[:bsz].set(c)
    tn = 768
    out = pl.pallas_call(
        _ada_kernel,
        grid=(n // tn,),
        in_specs=[pl.BlockSpec((rows, d), lambda j: (0, 0)),
                  pl.BlockSpec((d, tn), lambda j: (0, j)),
                  pl.BlockSpec((1, tn), lambda j: (0, j))],
        out_specs=pl.BlockSpec((rows, tn), lambda j: (0, j)),
        out_shape=jax.ShapeDtypeStruct((rows, n), F32),
        compiler_params=_cparams(("arbitrary",)),
        name="ada",
    )(cp, w, b.reshape(1, n))
    return out[:bsz]


_OFF_QA = 0
_OFF_KA = _OFF_QA + MOBA_WIDTH
_OFF_VA = _OFF_KA + MOBA_HEADS * LANES
_OFF_QKB = _OFF_VA + MOBA_WIDTH
_OFF_VB = _OFF_QKB + 2 * GLA_KEY_DIM
_OFF_GK = _OFF_VB + GLA_VALUE_DIM
_OFF_R = _OFF_GK + LANES
_OFF_G = _OFF_R + GLA_VALUE_DIM
_W_CAT = _OFF_G + 2 * D_MODEL


def _regroup_w_in(w):
    d = w.shape[0]
    hd = MOBA_HEAD_DIM
    o_k, o_v = MOBA_WIDTH, 2 * MOBA_WIDTH
    o_qkb = 3 * MOBA_WIDTH
    o_gk = o_qkb + 2 * GLA_KEY_DIM + GLA_VALUE_DIM
    k_heads = w[:, o_k:o_v].reshape(d, MOBA_HEADS, hd)
    k_groups = jnp.pad(k_heads, ((0, 0), (0, 0), (0, LANES - hd))).reshape(d, -1)
    gk = jnp.pad(w[:, o_gk:o_gk + GLA_GATE_RANK], ((0, 0), (0, LANES - GLA_GATE_RANK)))
    return jnp.concatenate(
        [w[:, :o_k], k_groups, w[:, o_v:o_qkb], w[:, o_qkb:o_gk], gk,
         w[:, o_gk + GLA_GATE_RANK:]], axis=1).astype(BF16)


def _inproj_kernel(x_ref, nw_ref, sc_ref, sh_ref, w_ref,
                   qt_ref, ka_ref, vt_ref, qkb_ref, vb_ref, gk_ref, r_ref, g_ref, *, tpb):
    tm = x_ref.shape[0]
    hd = MOBA_HEAD_DIM
    nbt = tm // MOBA_BLOCK
    h = _rms(x_ref[...]) * nw_ref[...]
    h = h * (1.0 + sc_ref[0]) + sh_ref[0]
    hb = h.astype(BF16)

    def mm(a, b):
        return jnp.dot(hb, w_ref[:, a:b], preferred_element_type=F32)

    q_t = (mm(_OFF_QA, _OFF_KA) * (hd ** -0.5 * LOG2E)).T
    v_t = mm(_OFF_VA, _OFF_QKB).T
    blk0 = (pl.program_id(0) % tpb) * nbt
    lane = lax.broadcasted_iota(I32, (MOBA_BLOCK, LANES), 1)
    ones_rows = jnp.where(
        lax.broadcasted_iota(I32, (MOBA_VT_ROWS - hd, MOBA_BLOCK), 0) == 0, 1.0, 0.0)
    for hh in range(MOBA_HEADS):
        qt_ref[0, hh] = q_t[hh * hd:(hh + 1) * hd].astype(BF16)
        k_h = mm(_OFF_KA + hh * LANES, _OFF_KA + (hh + 1) * LANES)
        for j in range(nbt):
            rows = slice(j * MOBA_BLOCK, (j + 1) * MOBA_BLOCK)
            ka_ref[0, hh, j] = jnp.where(lane == hd + blk0 + j, 1.0, k_h[rows]).astype(BF16)
            vt_ref[0, hh, j, 0:hd, :] = v_t[hh * hd:(hh + 1) * hd, rows].astype(BF16)
            vt_ref[0, hh, j, hd:MOBA_VT_ROWS, :] = ones_rows.astype(BF16)
    qkb_ref[...] = mm(_OFF_QKB, _OFF_VB).astype(BF16)
    vb_ref[...] = mm(_OFF_VB, _OFF_GK).astype(BF16)
    gk_ref[...] = mm(_OFF_GK, _OFF_R)
    r = mm(_OFF_R, _OFF_G)
    r_ref[...] = (r * _sigmoid(r)).astype(BF16)
    g_ref[...] = _sigmoid(mm(_OFF_G, _W_CAT)).astype(BF16)


def _inproj(x2d, nw, sc, sh, w_cat, bsz, seq, tm):
    t, d = x2d.shape
    tpb = seq // tm
    nbt = tm // MOBA_BLOCK
    nh, hd = MOBA_HEADS, MOBA_HEAD_DIM
    row = lambda w: pl.BlockSpec((tm, w), lambda i: (i, 0))
    per_b = pl.BlockSpec((1, 1, d), lambda i: (i // tpb, 0, 0))
    rows_out = [(2 * GLA_KEY_DIM, BF16), (GLA_VALUE_DIM, BF16), (LANES, F32),
                (GLA_VALUE_DIM, BF16), (2 * D_MODEL, BF16)]
    return pl.pallas_call(
        functools.partial(_inproj_kernel, tpb=tpb),
        grid=(t // tm,),
        in_specs=[row(d), pl.BlockSpec((1, d), lambda i: (0, 0)), per_b, per_b,
                  pl.BlockSpec((d, _W_CAT), lambda i: (0, 0), pipeline_mode=pl.Buffered(1))],
        out_specs=[pl.BlockSpec((1, nh, hd, tm), lambda i: (i // tpb, 0, 0, i % tpb)),
                   pl.BlockSpec((1, nh, nbt, MOBA_BLOCK, LANES),
                                lambda i: (i // tpb, 0, i % tpb, 0, 0)),
                   pl.BlockSpec((1, nh, nbt, MOBA_VT_ROWS, MOBA_BLOCK),
                                lambda i: (i // tpb, 0, i % tpb, 0, 0))]
                  + [row(w) for w, _ in rows_out],
        out_shape=[jax.ShapeDtypeStruct((bsz, nh, hd, seq), BF16),
                   jax.ShapeDtypeStruct((bsz, nh, seq // MOBA_BLOCK, MOBA_BLOCK, LANES), BF16),
                   jax.ShapeDtypeStruct((bsz, nh, seq // MOBA_BLOCK, MOBA_VT_ROWS, MOBA_BLOCK),
                                        BF16)]
                  + [jax.ShapeDtypeStruct((t, w), dt) for w, dt in rows_out],
        compiler_params=_cparams(("arbitrary",)),
        name="inproj",
    )(x2d, nw.reshape(1, d), sc, sh, w_cat)


def _t5_bucket_np(n):
    n = np.maximum(n, 0)
    max_exact = REL_BUCKETS // 2
    nf = np.maximum(n, max_exact).astype(np.float32)
    large = max_exact + (np.log(nf / max_exact) / math.log(REL_MAX_DIST / max_exact)
                         * (REL_BUCKETS - max_exact)).astype(np.int32)
    large = np.minimum(large, REL_BUCKETS - 1)
    return np.where(n < max_exact, n, large).astype(np.int32)


def _bucket_table():
    kj = np.arange(MOBA_BLOCK)[:, None]
    qi = np.arange(2 * MOBA_BLOCK)[None, :] % MOBA_BLOCK
    prev = np.arange(2 * MOBA_BLOCK)[None, :] < MOBA_BLOCK
    return _t5_bucket_np(qi - kj + np.where(prev, MOBA_BLOCK, 0))


def _bias_kernel(rb_ref, bucket_ref, o_ref):
    h = pl.program_id(0)
    bk = bucket_ref[...]
    far = rb_ref[(REL_BUCKETS - 1) * MOBA_HEADS + h]
    acc = jnp.zeros(bk.shape, F32)
    for b in range(REL_BUCKETS):
        acc = jnp.where(bk == b, rb_ref[b * MOBA_HEADS + h] - far, acc)
    o_ref[0] = acc * LOG2E


def _bias_tiles(rel_bias):
    bucket = jnp.asarray(_bucket_table())
    return pl.pallas_call(
        _bias_kernel,
        grid=(MOBA_HEADS,),
        in_specs=[pl.BlockSpec(memory_space=pltpu.SMEM),
                  pl.BlockSpec(bucket.shape, lambda h: (0, 0))],
        out_specs=pl.BlockSpec((1,) + bucket.shape, lambda h: (h, 0, 0)),
        out_shape=jax.ShapeDtypeStruct((MOBA_HEADS,) + bucket.shape, F32),
        compiler_params=_cparams(("arbitrary",)),
        name="bias",
    )(rel_bias.reshape(-1), bucket)


MOBA_HEADS_PER_STEP = 8
MOBA_FAR_GROUP = 2
MOBA_QBLOCKS_PER_STEP = 4
MOBA_FAR_LAGS = (0, 3, 6)
MOBA_NEAR_LAGS = (0, 3, 6, 9)
MOBA_VT_ROWS = MOBA_HEAD_DIM + 16


def _moba_kernel(qt_ref, k_ref, vt_ref, bias_ref, o_ref, kmean_ref, qa_ref):
    blk = MOBA_BLOCK
    nb = MOBA_MAX_BLOCKS
    hd = MOBA_HEAD_DIM
    hp = MOBA_HEADS_PER_STEP
    grp = MOBA_FAR_GROUP
    nq = MOBA_QBLOCKS_PER_STEP
    pair = pl.program_id(2)
    neg = -jnp.inf
    items = [(h, j) for j in range(nq) for h in range(hp)]

    @pl.when(pair == 0)
    def _():
        kmean_ref[...] = jnp.zeros(kmean_ref.shape, F32)

    for h in range(hp):
        for j in range(nq):
            kmean_ref[h, pl.ds(pair * nq + j, 1), :] = jnp.mean(
                k_ref[0, h, pair * nq + j].astype(F32), axis=0, keepdims=True)

    row = lax.broadcasted_iota(I32, (nb, blk), 0)
    rowf = row.astype(F32)
    key_i = lax.broadcasted_iota(I32, (blk, blk), 0)
    qry_i = lax.broadcasted_iota(I32, (blk, blk), 1)
    pad = jnp.zeros((LANES - hd, blk), BF16)
    pad_hi = jnp.zeros((LANES - hd - nb, blk), BF16)

    def skewed(stages, lags, todo=None):
        todo = list(range(len(items))) if todo is None else todo
        vals = {}
        for step in range(len(todo) + lags[-1]):
            for stage, lag in zip(stages, lags):
                pos = step - lag
                if 0 <= pos < len(todo):
                    vals[todo[pos]] = stage(todo[pos], vals.get(todo[pos]))
        return vals

    def select(n, _):
        h, j = items[n]
        qi = pair * nq + j
        qt = qt_ref[0, h, :, j * blk:(j + 1) * blk]
        gate = jnp.dot(kmean_ref[h, :, 0:hd], qt.astype(F32), precision=HIGHEST,
                       preferred_element_type=F32)
        g = jnp.where(row < qi, gate, neg)
        sel = jnp.zeros((nb, blk), F32)
        for _ in range(MOBA_TOPK):
            mx = jnp.max(g, axis=0, keepdims=True)
            first = jnp.min(jnp.where(g == mx, rowf, float(nb)), axis=0, keepdims=True)
            pick = rowf == jnp.where(mx > neg, first, -1.0)
            sel = jnp.where(pick, 1.0, sel)
            g = jnp.where(pick, neg, g)
        mask_prev = jnp.where(sel > 0.0, jnp.where(row == qi - 1, 0.0, NEG_BIG), NEG_BIG)
        mask_far = jnp.where(sel > 0.0, jnp.where(row < qi - 1, 0.0, NEG_BIG), NEG_BIG)
        qa_ref[n] = jnp.concatenate([qt, mask_far.astype(BF16), pad_hi], axis=0)
        return (jnp.concatenate([qt, pad], axis=0),
                jnp.concatenate([qt, mask_prev.astype(BF16), pad_hi], axis=0))

    def own_prev(n):
        h, j = items[n]
        qi = pair * nq + j
        return h, qi, jnp.maximum(qi - 1, 0)

    def near_scores(n, qa):
        h, qi, prev_j = own_prev(n)
        qa_own, qa_prev = qa
        s_own = jnp.dot(k_ref[0, h, qi], qa_own, preferred_element_type=F32)
        s_prev = jnp.dot(k_ref[0, h, prev_j], qa_prev, preferred_element_type=F32)
        return s_own, s_prev

    def near_softmax(n, ss):
        h = items[n][0]
        s_own, s_prev = ss
        s_own = jnp.where(key_i <= qry_i, s_own + bias_ref[h, :, blk:2 * blk], neg)
        s = jnp.concatenate([s_own, s_prev + bias_ref[h, :, 0:blk]], axis=0)
        m0 = jnp.max(s, axis=0, keepdims=True)
        return m0, jnp.exp2(s - m0)

    def near_pv(n, mp):
        h, qi, prev_j = own_prev(n)
        m0, p = mp
        pb = p.astype(BF16)
        acc = (jnp.dot(vt_ref[0, h, qi], pb[0:blk], preferred_element_type=F32)
               + jnp.dot(vt_ref[0, h, prev_j], pb[blk:2 * blk], preferred_element_type=F32))
        return m0, acc

    near = skewed([select, near_scores, near_softmax, near_pv], MOBA_NEAR_LAGS)
    states = tuple(near[n] for n in range(len(items)))

    def far(gi, states, todo=None):
        j0 = gi * grp

        def qk(n, _):
            kt = k_ref[0, items[n][0], pl.ds(j0, grp)].reshape(grp * blk, LANES)
            return jnp.dot(kt, qa_ref[n], preferred_element_type=F32)

        def softmax(n, s):
            m_old = states[n][0]
            m_new = jnp.maximum(m_old, jnp.max(s, axis=0, keepdims=True))
            return m_new, jnp.exp2(m_old - m_new), jnp.exp2(s - m_new)

        def pv(n, sm):
            m_new, a, p = sm
            pb = p.astype(BF16)
            tot = a * states[n][1]
            for i in range(grp):
                tot = tot + jnp.dot(vt_ref[0, items[n][0], j0 + i], pb[i * blk:(i + 1) * blk],
                                    preferred_element_type=F32)
            return m_new, tot

        new = skewed([qk, softmax, pv], MOBA_FAR_LAGS, todo)
        return tuple(new.get(n, states[n]) for n in range(len(items)))

    sub = nq // grp
    states = lax.fori_loop(0, pair * sub, far, tuple(states))
    for extra in range(1, sub):
        later = [n for n, (_, j) in enumerate(items) if j // grp >= extra]
        states = far(pair * sub + extra - 1, states, later)

    for n, (h, j) in enumerate(items):
        acc = states[n][1]
        o_ref[0, h * hd:(h + 1) * hd, j * blk:(j + 1) * blk] = (
            acc[0:hd] / acc[hd:hd + 1]).astype(o_ref.dtype)


def _moba(qt, k_aug, vt, bias):
    bsz, nh, hd, s = qt.shape
    blk = MOBA_BLOCK
    hp = MOBA_HEADS_PER_STEP
    nq = MOBA_QBLOCKS_PER_STEP
    nblk = s // blk
    assert nh % hp == 0 and nblk % nq == 0 and nq % MOBA_FAR_GROUP == 0
    return pl.pallas_call(
        _moba_kernel,
        grid=(bsz, nh // hp, nblk // nq),
        in_specs=[pl.BlockSpec((1, hp, hd, nq * blk), lambda b, g, i: (b, g, 0, i)),
                  pl.BlockSpec((1, hp, nblk, blk, LANES), lambda b, g, i: (b, g, 0, 0, 0),
                               pipeline_mode=pl.Buffered(1)),
                  pl.BlockSpec((1, hp, nblk, MOBA_VT_ROWS, blk), lambda b, g, i: (b, g, 0, 0, 0),
                               pipeline_mode=pl.Buffered(1)),
                  pl.BlockSpec((hp, blk, 2 * blk), lambda b, g, i: (g, 0, 0),
                               pipeline_mode=pl.Buffered(1))],
        out_specs=pl.BlockSpec((1, hp * hd, nq * blk), lambda b, g, i: (b, g, i)),
        out_shape=jax.ShapeDtypeStruct((bsz, nh * hd, s), BF16),
        scratch_shapes=[pltpu.VMEM((hp, MOBA_MAX_BLOCKS, LANES), F32),
                        pltpu.VMEM((hp * nq, LANES, blk), BF16)],
        compiler_params=_cparams(("arbitrary", "arbitrary", "arbitrary")),
        name="moba",
    )(qt, k_aug, vt, bias)


def _gla_kernel(q_ref, k_ref, v_ref, gk_ref, wup_ref, bgk_ref, r_ref, gn_ref, o_ref,
                state_ref, *, nchunk):
    ch = GLA_CHUNK
    tc = nchunk * ch
    dk, dv = GLA_DK, GLA_DV

    @pl.when(pl.program_id(1) == 0)
    def _():
        state_ref[...] = jnp.zeros(state_ref.shape, F32)

    rin = lax.broadcasted_iota(I32, (tc, dk), 0) & (ch - 1)
    causal = (lax.broadcasted_iota(I32, (ch, ch), 1) <= lax.broadcasted_iota(I32, (ch, ch), 0))
    eye = (lax.broadcasted_iota(I32, (dk, dk), 0) == lax.broadcasted_iota(I32, (dk, dk), 1))
    chunks = [slice(n * ch, (n + 1) * ch) for n in range(nchunk)]
    gk = gk_ref[...]
    gk_hi = gk.astype(BF16)
    gk_lo = (gk - gk_hi.astype(F32)).astype(BF16)

    def prep(h, _):
        ks = slice(h * dk, (h + 1) * dk)
        w = wup_ref[:, ks]
        w_hi = w.astype(BF16)
        w_lo = (w - w_hi.astype(F32)).astype(BF16)
        z = (jnp.dot(gk_hi, w_hi, preferred_element_type=F32)
             + jnp.dot(gk_lo, w_hi, preferred_element_type=F32)
             + jnp.dot(gk_hi, w_lo, preferred_element_type=F32) + bgk_ref[:, ks])
        log_a = ((jnp.minimum(z, 0.0) - jnp.log(1.0 + jnp.exp(-jnp.abs(z))))
                 / GLA_GATE_NORMALIZER)
        b = log_a
        sh = 1
        while sh < ch:
            b = b + jnp.where(rin >= sh, pltpu.roll(b, sh, axis=0), 0.0)
            sh *= 2
        q = q_ref[:, ks].astype(F32) * (dk ** -0.5)
        k = k_ref[:, ks].astype(F32)
        q_g = (q * jnp.exp(b)).astype(BF16)
        k_g = (k * jnp.exp(-b)).astype(BF16)
        b3 = b.reshape(nchunk, ch, dk)
        b_last = b3[:, ch - 1:ch, :]
        k_end = (k * jnp.exp(jnp.broadcast_to(b_last, b3.shape) - b3).reshape(tc, dk)
                 ).astype(BF16)
        return q_g, k_g, k_end, jnp.exp(b_last)

    def local(h, pre):
        q_g, k_g, k_end, decay = pre
        o_intra, kv, decay_col = [], [], []
        for n, sl in enumerate(chunks):
            v_c = v_ref[sl, h * dv:(h + 1) * dv]
            att = jnp.where(causal, _nt_dot(q_g[sl], k_g[sl]), 0.0)
            o_intra.append(jnp.dot(att.astype(BF16), v_c, preferred_element_type=F32))
            kv.append(lax.dot_general(k_end[sl], v_c, (((0,), (0,)), ((), ())),
                                      preferred_element_type=F32))
            decay_col.append(jnp.sum(
                jnp.where(eye, jnp.broadcast_to(decay[n], (dk, dk)), 0.0),
                axis=1, keepdims=True))
        return q_g, o_intra, kv, decay_col

    def chain(h, loc):
        q_g, o_intra, kv, decay_col = loc
        state = state_ref[h]
        outs = []
        for n, sl in enumerate(chunks):
            outs.append(o_intra[n] + jnp.dot(q_g[sl], state.astype(BF16),
                                             preferred_element_type=F32))
            state = decay_col[n] * state + kv[n]
        state_ref[h] = state
        return jnp.concatenate(outs, axis=0)

    def finish(h, o):
        vs = slice(h * dv, (h + 1) * dv)
        o_ref[:, vs] = (_rms(o) * gn_ref[...] * r_ref[:, vs].astype(F32)).astype(o_ref.dtype)
        return None

    stages = [prep, local, chain, finish]
    vals = [None] * GLA_HEADS
    for step in range(GLA_HEADS + len(stages) - 1):
        for si, stage in enumerate(stages):
            h = step - si
            if 0 <= h < GLA_HEADS:
                vals[h] = stage(h, vals[h])


def _gla(qkb, vb, gk, wup, bgk, r_act, gn, bsz, seq, nchunk):
    t = qkb.shape[0]
    tc = nchunk * GLA_CHUNK
    nc = seq // tc
    rowblk = lambda w, off: pl.BlockSpec((tc, w), lambda b, c: (b * nc + c, off))
    full = lambda a: pl.BlockSpec(a.shape, lambda b, c: (0, 0))
    return pl.pallas_call(
        functools.partial(_gla_kernel, nchunk=nchunk),
        grid=(bsz, nc),
        in_specs=[rowblk(GLA_KEY_DIM, 0), rowblk(GLA_KEY_DIM, 1), rowblk(GLA_VALUE_DIM, 0),
                  rowblk(LANES, 0), full(wup), full(bgk), rowblk(GLA_VALUE_DIM, 0), full(gn)],
        out_specs=rowblk(GLA_VALUE_DIM, 0),
        out_shape=jax.ShapeDtypeStruct((t, GLA_VALUE_DIM), BF16),
        scratch_shapes=[pltpu.VMEM((GLA_HEADS, GLA_DK, GLA_DV), F32)],
        compiler_params=_cparams(("arbitrary", "arbitrary")),
        name="gla",
    )(qkb, qkb, vb, gk, wup, bgk, r_act, gn)


def _merge_kernel(ya_ref, yb_ref, g_ref, x_ref, wpa_ref, wpb_ref, wout_ref, g1_ref,
                  nw_ref, sc_ref, sh_ref, wr_ref, br_ref,
                  x1_ref, h2_ref, idx_ref, wts_ref):
    pa = lax.dot_general(ya_ref[0], wpa_ref[...], (((0,), (0,)), ((), ())),
                         preferred_element_type=F32)
    pb = jnp.dot(yb_ref[...], wpb_ref[...], preferred_element_type=F32)
    mixed = (g_ref[:, 0:D_MODEL].astype(F32) * pa
             + g_ref[:, D_MODEL:2 * D_MODEL].astype(F32) * pb)
    y = jnp.dot(mixed.astype(BF16), wout_ref[...], preferred_element_type=F32)
    x1 = x_ref[...] + g1_ref[0] * y
    x1_ref[...] = x1
    h2 = _rms(x1) * nw_ref[...]
    h2 = h2 * (1.0 + sc_ref[0]) + sh_ref[0]
    _store_rows(h2_ref, _pack_halves(h2))
    h_hi = h2.astype(BF16)
    h_lo = (h2 - h_hi.astype(F32)).astype(BF16)
    w = wr_ref[...]
    w_hi = w.astype(BF16)
    w_lo = (w - w_hi.astype(F32)).astype(BF16)
    logits = (_nt_dot(w_hi, h_hi) + _nt_dot(w_hi, h_lo) + _nt_dot(w_lo, h_hi)
              + br_ref[...])
    rowf = lax.broadcasted_iota(I32, logits.shape, 0).astype(F32)
    vals, idxs = [], []
    cur = logits
    for _ in range(TOP_K):
        mx = jnp.max(cur, axis=0, keepdims=True)
        first = jnp.min(jnp.where(cur == mx, rowf, float(N_EXPERTS)), axis=0, keepdims=True)
        vals.append(mx)
        idxs.append(first)
        cur = jnp.where(rowf == first, -jnp.inf, cur)
    es = [jnp.exp(v - vals[0]) for v in vals]
    tot = es[0]
    for e in es[1:]:
        tot = tot + e
    idx_ref[...] = jnp.concatenate(idxs, axis=0).astype(I32)
    wts_ref[...] = jnp.concatenate([e / tot for e in es], axis=0)


def _merge(ya, yb, gates, x2d, wpa, wpb, wout, g1, nw, sc, sh, wr_t, br, seq, tm):
    t, d = x2d.shape
    tpb = seq // tm
    row = lambda w: pl.BlockSpec((tm, w), lambda i: (i, 0))
    full = lambda a: pl.BlockSpec(a.shape, lambda i: (0,) * a.ndim)
    per_b = pl.BlockSpec((1, 1, d), lambda i: (i // tpb, 0, 0))
    colblk = pl.BlockSpec((TOP_K, tm), lambda i: (0, i))
    return pl.pallas_call(
        _merge_kernel,
        grid=(t // tm,),
        in_specs=[pl.BlockSpec((1, MOBA_WIDTH, tm), lambda i: (i // tpb, 0, i % tpb)),
                  row(GLA_VALUE_DIM), row(2 * D_MODEL), row(d),
                  full(wpa), full(wpb), full(wout), per_b,
                  pl.BlockSpec((1, d), lambda i: (0, 0)), per_b, per_b,
                  full(wr_t), full(br)],
        out_specs=[row(d), pl.BlockSpec((tm * ROW_SLABS, LANES), lambda i: (i, 0)),
                   colblk, colblk],
        out_shape=[jax.ShapeDtypeStruct((t, d), F32),
                   jax.ShapeDtypeStruct((t * ROW_SLABS, LANES), U32),
                   jax.ShapeDtypeStruct((TOP_K, t), I32), jax.ShapeDtypeStruct((TOP_K, t), F32)],
        compiler_params=_cparams(("arbitrary",)),
        name="merge",
    )(ya, yb, gates, x2d, wpa, wpb, wout, g1, nw.reshape(1, d), sc, sh, wr_t, br)


def _rank_kernel(idx_ref, rank_ref, cnt_ref, carry_ref):
    tm = idx_ref.shape[1]

    @pl.when(pl.program_id(0) == 0)
    def _():
        carry_ref[...] = jnp.zeros(carry_ref.shape, F32)

    rows = lax.broadcasted_iota(I32, (N_EXPERTS, tm), 0)
    before = (lax.broadcasted_iota(I32, (tm, tm), 0)
              < lax.broadcasted_iota(I32, (tm, tm), 1))
    upper = jnp.where(before, 1.0, 0.0).astype(BF16)
    carry = carry_ref[:, 0:1]
    ranks = []
    for k in range(TOP_K):
        onehot = idx_ref[k:k + 1, :] == rows
        onef = jnp.where(onehot, 1.0, 0.0)
        earlier = jnp.dot(onef.astype(BF16), upper, preferred_element_type=F32) + carry
        ranks.append(jnp.sum(jnp.where(onehot, earlier, 0.0), axis=0, keepdims=True))
        carry = carry + jnp.sum(onef, axis=1, keepdims=True)
    rank_ref[...] = jnp.concatenate(ranks, axis=0).astype(I32)
    total = jnp.broadcast_to(carry, carry_ref.shape)
    carry_ref[...] = total
    cnt_ref[...] = total


def _rank(idx, tm):
    t = idx.shape[1]
    return pl.pallas_call(
        _rank_kernel,
        grid=(t // tm,),
        in_specs=[pl.BlockSpec((TOP_K, tm), lambda i: (0, i))],
        out_specs=[pl.BlockSpec((TOP_K, tm), lambda i: (0, i)),
                   pl.BlockSpec((N_EXPERTS, LANES), lambda i: (0, 0))],
        out_shape=[jax.ShapeDtypeStruct((TOP_K, t), I32),
                   jax.ShapeDtypeStruct((N_EXPERTS, LANES), F32)],
        scratch_shapes=[pltpu.VMEM((N_EXPERTS, LANES), F32)],
        compiler_params=_cparams(("arbitrary",)),
        name="rank",
    )(idx)


def _dest_kernel(pstart_ref, idx_ref, rank_ref, dest_ref):
    idx = idx_ref[...]
    off = jnp.zeros(idx.shape, I32)
    for e in range(N_EXPERTS):
        off = jnp.where(idx == e, pstart_ref[e], off)
    dest_ref[...] = rank_ref[...] + off


def _dest(pstart, idx, rank, tm):
    t = idx.shape[1]
    blk = pl.BlockSpec((TOP_K, tm), lambda i: (0, i))
    return pl.pallas_call(
        _dest_kernel,
        grid=(t // tm,),
        in_specs=[pl.BlockSpec(memory_space=pltpu.SMEM), blk, blk],
        out_specs=blk,
        out_shape=jax.ShapeDtypeStruct((TOP_K, t), I32),
        compiler_params=_cparams(("arbitrary",)),
        name="dest",
    )(pstart, idx, rank)


def _scatter_kernel(dest_ref, zrow_ref, h_ref, xout_hbm, zbuf, stage, sems, zsem, *, tm):
    @pl.when(pl.program_id(0) == 0)
    def _():
        zbuf[...] = jnp.zeros(zbuf.shape, zbuf.dtype)

        def zero_rows(row0):
            line0 = pl.multiple_of(row0 * ROW_SLABS, MOE_ROWS * ROW_SLABS)
            return pltpu.make_async_copy(
                zbuf, xout_hbm.at[pl.ds(line0, MOE_ROWS * ROW_SLABS)], zsem)

        for e in range(N_EXPERTS):
            zero_rows(zrow_ref[e]).start()
        for e in range(N_EXPERTS):
            zero_rows(zrow_ref[e]).wait()

        def start_tail(j, carry):
            zero_rows(j * MOE_ROWS).start()
            return carry

        def wait_tail(j, carry):
            zero_rows(j * MOE_ROWS).wait()
            return carry

        n_all = xout_hbm.shape[0] // (MOE_ROWS * ROW_SLABS)
        lax.fori_loop(zrow_ref[N_EXPERTS], n_all, start_tail, 0)
        lax.fori_loop(zrow_ref[N_EXPERTS], n_all, wait_tail, 0)

    i = pl.program_id(0)
    slot = i % 2
    src = stage.at[slot]
    src[...] = h_ref[...]

    def issue(g, carry):
        base = pl.multiple_of(g * ROW_UNROLL, ROW_UNROLL)
        for j in range(ROW_UNROLL):
            for k in range(TOP_K):
                d = dest_ref[(base + j) * TOP_K + k]
                pltpu.make_async_copy(_row(src, base + j), _row(xout_hbm, d),
                                      sems.at[slot]).start(priority=k % 2)
        return carry

    lax.fori_loop(0, tm // ROW_UNROLL, issue, 0)

    def wait_tile(s):
        for k in range(TOP_K):
            pltpu.make_async_copy(stage.at[s], xout_hbm.at[pl.ds(0, tm * ROW_SLABS)],
                                  sems.at[s]).wait()

    @pl.when(i > 0)
    def _():
        wait_tile(1 - slot)

    @pl.when(i == pl.num_programs(0) - 1)
    def _():
        wait_tile(slot)


def _scatter(dest_tok, last_block_row, h2, n_pad, tm):
    t = h2.shape[0] // ROW_SLABS
    return pl.pallas_call(
        functools.partial(_scatter_kernel, tm=tm),
        grid=(t // tm,),
        in_specs=[pl.BlockSpec((tm * TOP_K,), lambda i: (i,), memory_space=pltpu.SMEM),
                  pl.BlockSpec(memory_space=pltpu.SMEM),
                  pl.BlockSpec((tm * ROW_SLABS, LANES), lambda i: (i, 0))],
        out_specs=pl.BlockSpec(memory_space=pl.ANY),
        out_shape=jax.ShapeDtypeStruct((n_pad * ROW_SLABS, LANES), h2.dtype),
        scratch_shapes=[pltpu.VMEM((MOE_ROWS * ROW_SLABS, LANES), h2.dtype),
                        pltpu.VMEM((2, tm * ROW_SLABS, LANES), h2.dtype),
                        pltpu.SemaphoreType.DMA((2,)), pltpu.SemaphoreType.DMA],
        compiler_params=_cparams(("arbitrary",)),
        name="scatter",
    )(dest_tok, last_block_row, h2)


def _expert_kernel(be_ref, nused_ref, x_ref, wg_ref, bg_ref, wu_ref, bu_ref, wd_ref, bd_ref,
                   o_ref, wgb_ref, wub_ref, wdb_ref):
    i = pl.program_id(0)
    prev = be_ref[jnp.maximum(i - 1, 0)]

    @pl.when(jnp.logical_or(i == 0, be_ref[i] != prev))
    def _():
        wgb_ref[...] = wg_ref[0].astype(BF16)
        wub_ref[...] = wu_ref[0].astype(BF16)
        wdb_ref[...] = wd_ref[0].astype(BF16)

    @pl.when(i < nused_ref[0])
    def _():
        x_lo, x_hi = [v.astype(BF16) for v in _unpack_halves(_load_rows(x_ref))]
        half = x_lo.shape[1]

        def in_dot(w_ref):
            return (jnp.dot(x_lo, w_ref[0:half, :], preferred_element_type=F32)
                    + jnp.dot(x_hi, w_ref[half:2 * half, :], preferred_element_type=F32))

        g = jnp.minimum(in_dot(wgb_ref) + bg_ref[0], SWIGLU_LIMIT)
        u = jnp.clip(in_dot(wub_ref) + bu_ref[0], -SWIGLU_LIMIT, SWIGLU_LIMIT)
        act = g * _sigmoid(SWIGLU_ALPHA * g) * (u + 1.0)
        _store_rows(o_ref, _pack_halves(
            jnp.dot(act.astype(BF16), wdb_ref[...], preferred_element_type=F32) + bd_ref[0]))

    @pl.when(i >= nused_ref[0])
    def _():
        o_ref[...] = jnp.zeros(o_ref.shape, o_ref.dtype)


def _expert(block_expert, n_used, x_pad, wg, bg, wu, bu, wd, bd):
    lines = MOE_ROWS * ROW_SLABS
    _, d, f = wg.shape
    nblk = x_pad.shape[0] // lines
    wspec = lambda a, b: pl.BlockSpec((1, a, b), lambda i, be, nu: (be[i], 0, 0))
    grid_spec = pltpu.PrefetchScalarGridSpec(
        num_scalar_prefetch=2,
        grid=(nblk,),
        in_specs=[pl.BlockSpec((lines, LANES),
                               lambda i, be, nu: (jnp.minimum(i, nu[0] - 1), 0)),
                  wspec(d, f), wspec(1, f), wspec(d, f), wspec(1, f), wspec(f, d), wspec(1, d)],
        out_specs=pl.BlockSpec((lines, LANES), lambda i, be, nu: (i, 0)),
        scratch_shapes=[pltpu.VMEM((d, f), BF16), pltpu.VMEM((d, f), BF16),
                        pltpu.VMEM((f, d), BF16)])
    return pl.pallas_call(
        _expert_kernel,
        grid_spec=grid_spec,
        out_shape=jax.ShapeDtypeStruct(x_pad.shape, U32),
        compiler_params=_cparams(("arbitrary",)),
        name="expert",
    )(block_expert, n_used, x_pad, wg, bg.reshape(N_EXPERTS, 1, f), wu,
      bu.reshape(N_EXPERTS, 1, f), wd, bd.reshape(N_EXPERTS, 1, d))


def _combine_kernel(dest_ref, y_hbm, x1_ref, wts_ref, g2_ref, nf_ref, o_ref, ybuf, sems, *,
                    tm, final):
    i = pl.program_id(0)
    n_tiles = pl.num_programs(0) - 1
    slot = i % 2

    @pl.when(i < n_tiles)
    def _():
        def row_copy(tt, k):
            d = dest_ref[tt * TOP_K + k]
            return pltpu.make_async_copy(_row(y_hbm, d), _row(ybuf.at[slot, k], tt),
                                         sems.at[slot])

        def issue(g, carry):
            base = pl.multiple_of(g * ROW_UNROLL, ROW_UNROLL)
            for j in range(ROW_UNROLL):
                for k in range(TOP_K):
                    row_copy(base + j, k).start(priority=k % 2)
            return carry

        lax.fori_loop(0, tm // ROW_UNROLL, issue, 0)

    @pl.when(i > 0)
    def _():
        done = 1 - slot
        for k in range(TOP_K):
            pltpu.make_async_copy(y_hbm.at[pl.ds(0, tm * ROW_SLABS)], ybuf.at[done, k],
                                  sems.at[done]).wait()
        w = wts_ref[...]
        lo, hi = None, None
        for k in range(TOP_K):
            y_lo, y_hi = _unpack_halves(_load_rows(ybuf.at[done, k]))
            lo = w[:, k:k + 1] * y_lo + (0.0 if lo is None else lo)
            hi = w[:, k:k + 1] * y_hi + (0.0 if hi is None else hi)
        moe = jnp.concatenate([lo, hi], axis=1)
        x2 = x1_ref[...] + g2_ref[0] * moe
        o_ref[...] = _rms(x2) * nf_ref[...] if final else x2


def _combine(dest_tok, y_pad, x1, wts_tok, g2, nf, seq, tm, final):
    t, d = x1.shape
    tpb = seq // tm
    n_tiles = t // tm
    lag = lambda i: jnp.maximum(i - 1, 0)
    return pl.pallas_call(
        functools.partial(_combine_kernel, tm=tm, final=final),
        grid=(n_tiles + 1,),
        in_specs=[pl.BlockSpec((tm * TOP_K,), lambda i: (jnp.minimum(i, n_tiles - 1),),
                               memory_space=pltpu.SMEM),
                  pl.BlockSpec(memory_space=pl.ANY),
                  pl.BlockSpec((tm, d), lambda i: (lag(i), 0)),
                  pl.BlockSpec((tm, TOP_K), lambda i: (lag(i), 0)),
                  pl.BlockSpec((1, 1, d), lambda i: (lag(i) // tpb, 0, 0)),
                  pl.BlockSpec((1, d), lambda i: (0, 0))],
        out_specs=pl.BlockSpec((tm, d), lambda i: (lag(i), 0)),
        out_shape=jax.ShapeDtypeStruct((t, d), F32),
        scratch_shapes=[pltpu.VMEM((2, TOP_K, tm * ROW_SLABS, LANES), y_pad.dtype),
                        pltpu.SemaphoreType.DMA((2,))],
        compiler_params=_cparams(("arbitrary",)),
        name="combine",
    )(dest_tok, y_pad, x1, wts_tok, g2, nf.reshape(1, d))


def _pick(n, cands):
    for c in cands:
        if n % c == 0:
            return c
    raise ValueError(f"no tile in {cands} divides {n}")


def kernel(x, c, rel_bias, w_ada, b_ada, norm_mix, w_in, w_gk_up, b_gk, gla_norm,
           w_proj_moba, w_proj_gla, w_out, norm_ffn, w_router, b_router,
           w_gate, b_gate, w_up, b_up, w_down, b_down, norm_final):
    bsz, seq, d = x.shape
    depth = w_ada.shape[0]
    assert d == D_MODEL and seq % MOBA_BLOCK == 0 and seq // MOBA_BLOCK <= MOBA_MAX_BLOCKS
    t = bsz * seq
    tm = _pick(seq, (512, 256))
    nchunk = _pick(seq // GLA_CHUNK, (8, 4))
    n_blk = seq // MOBA_BLOCK
    x2d = x.reshape(t, d)
    bias = _bias_tiles(rel_bias)
    per_b = lambda v: v.reshape(bsz, 1, d)

    for l in range(depth):
        mod = _ada(c, w_ada[l], b_ada[l])
        sh1, sc1, g1, sh2, sc2, g2 = [per_b(m) for m in jnp.split(mod, 6, axis=-1)]
        qt, k_aug, vt, qkb, vb, gk, r_act, gates = _inproj(
            x2d, norm_mix[l], sc1, sh1, _regroup_w_in(w_in[l]), bsz, seq, tm)
        ya = _moba(qt, k_aug, vt, bias)
        wup = jnp.pad(w_gk_up[l], ((0, LANES - GLA_GATE_RANK), (0, 0)))
        yb = _gla(qkb, vb, gk, wup, b_gk[l].reshape(1, -1), r_act,
                  gla_norm[l].reshape(1, -1), bsz, seq, nchunk)
        x1, h2, idx, wts = _merge(
            ya, yb, gates, x2d, w_proj_moba[l].astype(BF16), w_proj_gla[l].astype(BF16),
            w_out[l].astype(BF16), g1, norm_ffn[l], sc2, sh2,
            w_router[l].T, b_router[l].reshape(N_EXPERTS, 1), seq, tm)
        rank, cnt = _rank(idx, tm)
        counts = cnt[:, 0].astype(I32)
        padded = (counts + MOE_ROWS - 1) // MOE_ROWS * MOE_ROWS
        pcum = jnp.cumsum(padded)
        pstart = (pcum - padded).astype(I32)
        n_blocks = (t * TOP_K + MOE_ROWS - 1) // MOE_ROWS + N_EXPERTS
        block_row0 = jnp.arange(n_blocks, dtype=I32) * MOE_ROWS
        block_expert = jnp.minimum(
            jnp.sum((pcum[None, :] <= block_row0[:, None]).astype(I32), axis=1),
            N_EXPERTS - 1).astype(I32)
        n_used = (pcum[-1:] // MOE_ROWS).astype(I32)
        dest = _dest(pstart, idx, rank, tm)
        dest_tok = dest.T.reshape(t * TOP_K)
        tg = _pick(t, (256,))
        zero_info = jnp.concatenate([jnp.maximum(pcum - MOE_ROWS, 0).astype(I32), n_used])
        x_pad = _scatter(dest_tok, zero_info, h2, n_blocks * MOE_ROWS, tg)
        y_pad = _expert(block_expert, n_used, x_pad, w_gate[l], b_gate[l], w_up[l], b_up[l],
                        w_down[l], b_down[l])
        x2d = _combine(dest_tok, y_pad, x1, wts.T, g2, norm_final, seq, tg, l == depth - 1)
    return x2d.reshape(bsz, seq, d)
```

```python
import functools
import math

import numpy as np
import jax
import jax.numpy as jnp
from jax import lax
from jax.experimental import pallas as pl
from jax.experimental.pallas import tpu as pltpu

F32 = jnp.float32
BF16 = jnp.bfloat16
I32 = jnp.int32
HIGHEST = lax.Precision.HIGHEST

D_MODEL = 1024
MOBA_HEADS = 8
MOBA_HEAD_DIM = 64
MOBA_WIDTH = MOBA_HEADS * MOBA_HEAD_DIM
MOBA_BLOCK = 256
MOBA_TOPK = 3
MOBA_MAX_BLOCKS = 32
REL_BUCKETS = 32
REL_MAX_DIST = 128
GLA_HEADS = 4
GLA_KEY_DIM = D_MODEL // 2
GLA_VALUE_DIM = D_MODEL
GLA_DK = GLA_KEY_DIM // GLA_HEADS
GLA_DV = GLA_VALUE_DIM // GLA_HEADS
GLA_GATE_RANK = 16
GLA_GATE_NORMALIZER = 16.0
GLA_CHUNK = 64
N_EXPERTS = 32
TOP_K = 4
D_FF = D_MODEL
SWIGLU_ALPHA = 1.702
SWIGLU_LIMIT = 7.0
MOE_ROWS = 512
ROW_UNROLL = 8
EPS = 1e-6
LANES = 128
NEG_BIG = -1e30
LOG2E = math.log2(math.e)
VMEM_LIMIT = 56 * 1024 * 1024


def _cparams(sem, vmem=None):
    return pltpu.CompilerParams(dimension_semantics=sem,
                                vmem_limit_bytes=vmem or VMEM_LIMIT)


def _nt_dot(a, b, **kw):
    return lax.dot_general(a, b, (((1,), (1,)), ((), ())),
                           preferred_element_type=F32, **kw)


def _rms(x):
    return x * lax.rsqrt(jnp.mean(x * x, axis=-1, keepdims=True) + EPS)


def _sigmoid(x):
    return 1.0 / (1.0 + jnp.exp(-x))


U32 = jnp.uint32
_HI16 = 0xFFFF0000


def _pack_halves(x):
    n = x.shape[1] // 2
    lo = pltpu.bitcast(x[:, :n].astype(BF16).astype(F32), U32)
    hi = pltpu.bitcast(x[:, n:].astype(BF16).astype(F32), U32)
    return (hi & U32(_HI16)) | (lo >> 16)


def _unpack_halves(w):
    return (pltpu.bitcast(w << 16, F32), pltpu.bitcast(w & U32(_HI16), F32))


ROW_SLABS = D_MODEL // 2 // LANES


def _store_rows(ref, words):
    m = words.shape[0]
    for c in range(ROW_SLABS):
        ref[pl.ds(c, m, stride=ROW_SLABS), :] = words[:, c * LANES:(c + 1) * LANES]


def _load_rows(ref):
    m = ref.shape[0] // ROW_SLABS
    return jnp.concatenate(
        [ref[pl.ds(c, m, stride=ROW_SLABS), :] for c in range(ROW_SLABS)], axis=1)


def _row(ref, i):
    return ref.at[pl.ds(pl.multiple_of(i * ROW_SLABS, ROW_SLABS), ROW_SLABS)]


def _ada_kernel(c_ref, w_ref, b_ref, o_ref):
    c = c_ref[...]
    s = c * _sigmoid(c)
    o_ref[...] = jnp.dot(s, w_ref[...], precision=HIGHEST,
                         preferred_element_type=F32) + b_ref[...]


def _ada(c, w, b):
    bsz, d = c.shape
    n = w.shape[1]
    rows = -(-bsz // 8) * 8
    cp = jnp.zeros((rows, d), F32).at[:bsz].set(c)
    tn = 768
    out = pl.pallas_call(
        _ada_kernel,
        grid=(n // tn,),
        in_specs=[pl.BlockSpec((rows, d), lambda j: (0, 0)),
                  pl.BlockSpec((d, tn), lambda j: (0, j)),
                  pl.BlockSpec((1, tn), lambda j: (0, j))],
        out_specs=pl.BlockSpec((rows, tn), lambda j: (0, j)),
        out_shape=jax.ShapeDtypeStruct((rows, n), F32),
        compiler_params=_cparams(("arbitrary",)),
        name="ada",
    )(cp, w, b.reshape(1, n))
    return out[:bsz]


_OFF_QA = 0
_OFF_KA = _OFF_QA + MOBA_WIDTH
_OFF_VA = _OFF_KA + MOBA_WIDTH
_OFF_QKB = _OFF_VA + MOBA_WIDTH
_OFF_VB = _OFF_QKB + 2 * GLA_KEY_DIM
_OFF_GK = _OFF_VB + GLA_VALUE_DIM
_OFF_R = _OFF_GK + LANES
_OFF_G = _OFF_R + GLA_VALUE_DIM
_W_CAT = _OFF_G + 2 * D_MODEL


def _regroup_w_in(w):
    o_gk = 3 * MOBA_WIDTH + 2 * GLA_KEY_DIM + GLA_VALUE_DIM
    gk = jnp.pad(w[:, o_gk:o_gk + GLA_GATE_RANK], ((0, 0), (0, LANES - GLA_GATE_RANK)))
    return jnp.concatenate([w[:, :o_gk], gk, w[:, o_gk + GLA_GATE_RANK:]],
                           axis=1).astype(BF16)


def _inproj_kernel(x_ref, nw_ref, sc_ref, sh_ref, w_ref,
                   qt_ref, ka_ref, vt_ref, qkb_ref, vb_ref, gk_ref, r_ref, g_ref, *, tpb):
    tm = x_ref.shape[0]
    hd = MOBA_HEAD_DIM
    nbt = tm // MOBA_BLOCK
    h = _rms(x_ref[...]) * nw_ref[...]
    h = h * (1.0 + sc_ref[0]) + sh_ref[0]
    hb = h.astype(BF16)

    def mm(a, b):
        return jnp.dot(hb, w_ref[:, a:b], preferred_element_type=F32)

    q_t = (mm(_OFF_QA, _OFF_KA) * (hd ** -0.5 * LOG2E)).T
    v_t = mm(_OFF_VA, _OFF_QKB).T
    k_all = mm(_OFF_KA, _OFF_VA)
    blk0 = (pl.program_id(0) % tpb) * nbt
    lane = lax.broadcasted_iota(I32, (MOBA_BLOCK, LANES), 1)
    ones_rows = jnp.where(
        lax.broadcasted_iota(I32, (MOBA_VT_ROWS - hd, MOBA_BLOCK), 0) == 0, 1.0, 0.0)
    heads_per_tile = LANES // hd
    for hh in range(MOBA_HEADS):
        qt_ref[0, hh] = q_t[hh * hd:(hh + 1) * hd].astype(BF16)
        tile = hh // heads_per_tile
        k_h = k_all[:, tile * LANES:(tile + 1) * LANES]
        if hh % heads_per_tile:
            k_h = pltpu.roll(k_h, LANES - (hh % heads_per_tile) * hd, axis=1)
        for j in range(nbt):
            rows = slice(j * MOBA_BLOCK, (j + 1) * MOBA_BLOCK)
            onehot = jnp.where(lane == hd + blk0 + j, 1.0, 0.0)
            ka_ref[0, hh, j] = jnp.where(lane < hd, k_h[rows], onehot).astype(BF16)
            vt_ref[0, hh, j, 0:hd, :] = v_t[hh * hd:(hh + 1) * hd, rows].astype(BF16)
            vt_ref[0, hh, j, hd:MOBA_VT_ROWS, :] = ones_rows.astype(BF16)
    qkb_ref[...] = mm(_OFF_QKB, _OFF_VB).astype(BF16)
    vb_ref[...] = mm(_OFF_VB, _OFF_GK).astype(BF16)
    gk_ref[...] = mm(_OFF_GK, _OFF_R)
    r = mm(_OFF_R, _OFF_G)
    r_ref[...] = (r * _sigmoid(r)).astype(BF16)
    g_ref[...] = _sigmoid(mm(_OFF_G, _W_CAT)).astype(BF16)


def _inproj(x2d, nw, sc, sh, w_cat, bsz, seq, tm):
    t, d = x2d.shape
    tpb = seq // tm
    nbt = tm // MOBA_BLOCK
    nh, hd = MOBA_HEADS, MOBA_HEAD_DIM
    row = lambda w: pl.BlockSpec((tm, w), lambda i: (i, 0))
    per_b = pl.BlockSpec((1, 1, d), lambda i: (i // tpb, 0, 0))
    rows_out = [(2 * GLA_KEY_DIM, BF16), (GLA_VALUE_DIM, BF16), (LANES, F32),
                (GLA_VALUE_DIM, BF16), (2 * D_MODEL, BF16)]
    return pl.pallas_call(
        functools.partial(_inproj_kernel, tpb=tpb),
        grid=(t // tm,),
        in_specs=[row(d), pl.BlockSpec((1, d), lambda i: (0, 0)), per_b, per_b,
                  pl.BlockSpec((d, _W_CAT), lambda i: (0, 0), pipeline_mode=pl.Buffered(1))],
        out_specs=[pl.BlockSpec((1, nh, hd, tm), lambda i: (i // tpb, 0, 0, i % tpb)),
                   pl.BlockSpec((1, nh, nbt, MOBA_BLOCK, LANES),
                                lambda i: (i // tpb, 0, i % tpb, 0, 0)),
                   pl.BlockSpec((1, nh, nbt, MOBA_VT_ROWS, MOBA_BLOCK),
                                lambda i: (i // tpb, 0, i % tpb, 0, 0))]
                  + [row(w) for w, _ in rows_out],
        out_shape=[jax.ShapeDtypeStruct((bsz, nh, hd, seq), BF16),
                   jax.ShapeDtypeStruct((bsz, nh, seq // MOBA_BLOCK, MOBA_BLOCK, LANES), BF16),
                   jax.ShapeDtypeStruct((bsz, nh, seq // MOBA_BLOCK, MOBA_VT_ROWS, MOBA_BLOCK),
                                        BF16)]
                  + [jax.ShapeDtypeStruct((t, w), dt) for w, dt in rows_out],
        compiler_params=_cparams(("arbitrary",)),
        name="inproj",
    )(x2d, nw.reshape(1, d), sc, sh, w_cat)


def _t5_bucket_np(n):
    n = np.maximum(n, 0)
    max_exact = REL_BUCKETS // 2
    nf = np.maximum(n, max_exact).astype(np.float32)
    large = max_exact + (np.log(nf / max_exact) / math.log(REL_MAX_DIST / max_exact)
                         * (REL_BUCKETS - max_exact)).astype(np.int32)
    large = np.minimum(large, REL_BUCKETS - 1)
    return np.where(n < max_exact, n, large).astype(np.int32)


def _bucket_table():
    kj = np.arange(MOBA_BLOCK)[:, None]
    qi = np.arange(2 * MOBA_BLOCK)[None, :] % MOBA_BLOCK
    prev = np.arange(2 * MOBA_BLOCK)[None, :] < MOBA_BLOCK
    bucket = _t5_bucket_np(qi - kj + np.where(prev, MOBA_BLOCK, 0))
    return np.where(prev | (kj <= qi), bucket, -1).astype(np.int32)


def _bias_kernel(rb_ref, bucket_ref, o_ref):
    h = pl.program_id(0)
    bk = bucket_ref[...]
    far = rb_ref[(REL_BUCKETS - 1) * MOBA_HEADS + h]
    acc = jnp.zeros(bk.shape, F32)
    for b in range(REL_BUCKETS):
        acc = jnp.where(bk == b, rb_ref[b * MOBA_HEADS + h] - far, acc)
    o_ref[0] = jnp.where(bk < 0, NEG_BIG, acc * LOG2E)


def _bias_tiles(rel_bias):
    bucket = jnp.asarray(_bucket_table())
    return pl.pallas_call(
        _bias_kernel,
        grid=(MOBA_HEADS,),
        in_specs=[pl.BlockSpec(memory_space=pltpu.SMEM),
                  pl.BlockSpec(bucket.shape, lambda h: (0, 0))],
        out_specs=pl.BlockSpec((1,) + bucket.shape, lambda h: (h, 0, 0)),
        out_shape=jax.ShapeDtypeStruct((MOBA_HEADS,) + bucket.shape, F32),
        compiler_params=_cparams(("arbitrary",)),
        name="bias",
    )(rel_bias.reshape(-1), bucket)


MOBA_HEADS_PER_STEP = 8
MOBA_FAR_GROUP = 2
MOBA_QBLOCKS_PER_STEP = 4
MOBA_FAR_LAGS = (0, 3, 6)
MOBA_NEAR_LAGS = (0, 3, 6, 9)
MOBA_VT_ROWS = MOBA_HEAD_DIM + 16


def _moba_kernel(qt_ref, k_ref, vt_ref, bias_ref, o_ref, kmean_ref, qa_ref):
    blk = MOBA_BLOCK
    nb = MOBA_MAX_BLOCKS
    hd = MOBA_HEAD_DIM
    hp = MOBA_HEADS_PER_STEP
    grp = MOBA_FAR_GROUP
    nq = MOBA_QBLOCKS_PER_STEP
    pair = pl.program_id(2)
    neg = -jnp.inf
    items = [(h, j) for j in range(nq) for h in range(hp)]

    @pl.when(pair == 0)
    def _():
        kmean_ref[...] = jnp.zeros(kmean_ref.shape, F32)

    for h in range(hp):
        for j in range(nq):
            kmean_ref[h, pl.ds(pair * nq + j, 1), :] = jnp.mean(
                k_ref[0, h, pair * nq + j].astype(F32), axis=0, keepdims=True)

    row = lax.broadcasted_iota(I32, (nb, blk), 0)
    rowf = row.astype(F32)
    pad = jnp.zeros((LANES - hd, blk), BF16)
    pad_hi = jnp.zeros((LANES - hd - nb, blk), BF16)

    def skewed(stages, lags, todo=None):
        todo = list(range(len(items))) if todo is None else todo
        vals = {}
        for step in range(len(todo) + lags[-1]):
            for stage, lag in zip(stages, lags):
                pos = step - lag
                if 0 <= pos < len(todo):
                    vals[todo[pos]] = stage(todo[pos], vals.get(todo[pos]))
        return vals

    def select(n, _):
        h, j = items[n]
        qi = pair * nq + j
        qt = qt_ref[0, h, :, j * blk:(j + 1) * blk]
        gate = jnp.dot(kmean_ref[h, :, 0:hd], qt.astype(F32), precision=HIGHEST,
                       preferred_element_type=F32)
        g = jnp.where(row < qi, gate, neg)
        sel = jnp.zeros((nb, blk), F32)
        for _ in range(MOBA_TOPK):
            mx = jnp.max(g, axis=0, keepdims=True)
            first = jnp.min(jnp.where(g == mx, rowf, float(nb)), axis=0, keepdims=True)
            pick = rowf == jnp.where(mx > neg, first, -1.0)
            sel = jnp.where(pick, 1.0, sel)
            g = jnp.where(pick, neg, g)
        mask_prev = jnp.where(sel > 0.0, jnp.where(row == qi - 1, 0.0, NEG_BIG), NEG_BIG)
        mask_far = jnp.where(sel > 0.0, jnp.where(row < qi - 1, 0.0, NEG_BIG), NEG_BIG)
        qa_ref[n] = jnp.concatenate([qt, mask_far.astype(BF16), pad_hi], axis=0)
        return (jnp.concatenate([qt, pad], axis=0),
                jnp.concatenate([qt, mask_prev.astype(BF16), pad_hi], axis=0))

    def own_prev(n):
        h, j = items[n]
        qi = pair * nq + j
        return h, qi, jnp.maximum(qi - 1, 0)

    def near_scores(n, qa):
        h, qi, prev_j = own_prev(n)
        qa_own, qa_prev = qa
        s_own = jnp.dot(k_ref[0, h, qi], qa_own, preferred_element_type=F32)
        s_prev = jnp.dot(k_ref[0, h, prev_j], qa_prev, preferred_element_type=F32)
        return s_own, s_prev

    def near_softmax(n, ss):
        h = items[n][0]
        s_own, s_prev = ss
        s = jnp.concatenate([s_own + bias_ref[h, :, blk:2 * blk],
                             s_prev + bias_ref[h, :, 0:blk]], axis=0)
        m0 = jnp.max(s, axis=0, keepdims=True)
        return m0, jnp.exp2(s - m0)

    def near_pv(n, mp):
        h, qi, prev_j = own_prev(n)
        m0, p = mp
        pb = p.astype(BF16)
        acc = (jnp.dot(vt_ref[0, h, qi], pb[0:blk], preferred_element_type=F32)
               + jnp.dot(vt_ref[0, h, prev_j], pb[blk:2 * blk], preferred_element_type=F32))
        return m0, acc

    near = skewed([select, near_scores, near_softmax, near_pv], MOBA_NEAR_LAGS)
    states = tuple(near[n] for n in range(len(items)))

    def far(gi, states, todo=None):
        j0 = gi * grp

        def qk(n, _):
            kt = k_ref[0, items[n][0], pl.ds(j0, grp)].reshape(grp * blk, LANES)
            return jnp.dot(kt, qa_ref[n], preferred_element_type=F32)

        def softmax(n, s):
            m_old = states[n][0]
            m_new = jnp.maximum(m_old, jnp.max(s, axis=0, keepdims=True))
            return m_new, jnp.exp2(m_old - m_new), jnp.exp2(s - m_new)

        def pv(n, sm):
            m_new, a, p = sm
            pb = p.astype(BF16)
            tot = a * states[n][1]
            for i in range(grp):
                tot = tot + jnp.dot(vt_ref[0, items[n][0], j0 + i], pb[i * blk:(i + 1) * blk],
                                    preferred_element_type=F32)
            return m_new, tot

        new = skewed([qk, softmax, pv], MOBA_FAR_LAGS, todo)
        return tuple(new.get(n, states[n]) for n in range(len(items)))

    sub = nq // grp
    states = lax.fori_loop(0, pair * sub, far, tuple(states))
    for extra in range(1, sub):
        later = [n for n, (_, j) in enumerate(items) if j // grp >= extra]
        states = far(pair * sub + extra - 1, states, later)

    for n, (h, j) in enumerate(items):
        acc = states[n][1]
        o_ref[0, h * hd:(h + 1) * hd, j * blk:(j + 1) * blk] = (
            acc[0:hd] / acc[hd:hd + 1]).astype(o_ref.dtype)


def _moba(qt, k_aug, vt, bias):
    bsz, nh, hd, s = qt.shape
    blk = MOBA_BLOCK
    hp = MOBA_HEADS_PER_STEP
    nq = MOBA_QBLOCKS_PER_STEP
    nblk = s // blk
    assert nh % hp == 0 and nblk % nq == 0 and nq % MOBA_FAR_GROUP == 0
    return pl.pallas_call(
        _moba_kernel,
        grid=(bsz, nh // hp, nblk // nq),
        in_specs=[pl.BlockSpec((1, hp, hd, nq * blk), lambda b, g, i: (b, g, 0, i)),
                  pl.BlockSpec((1, hp, nblk, blk, LANES), lambda b, g, i: (b, g, 0, 0, 0),
                               pipeline_mode=pl.Buffered(1)),
                  pl.BlockSpec((1, hp, nblk, MOBA_VT_ROWS, blk), lambda b, g, i: (b, g, 0, 0, 0),
                               pipeline_mode=pl.Buffered(1)),
                  pl.BlockSpec((hp, blk, 2 * blk), lambda b, g, i: (g, 0, 0),
                               pipeline_mode=pl.Buffered(1))],
        out_specs=pl.BlockSpec((1, hp * hd, nq * blk), lambda b, g, i: (b, g, i)),
        out_shape=jax.ShapeDtypeStruct((bsz, nh * hd, s), BF16),
        scratch_shapes=[pltpu.VMEM((hp, MOBA_MAX_BLOCKS, LANES), F32),
                        pltpu.VMEM((hp * nq, LANES, blk), BF16)],
        compiler_params=_cparams(("arbitrary", "arbitrary", "arbitrary")),
        name="moba",
    )(qt, k_aug, vt, bias)


def _gla_kernel(q_ref, k_ref, v_ref, gk_ref, wup_ref, bgk_ref, r_ref, gn_ref, o_ref,
                state_ref, *, nchunk):
    ch = GLA_CHUNK
    tc = nchunk * ch
    dk, dv = GLA_DK, GLA_DV

    @pl.when(pl.program_id(1) == 0)
    def _():
        state_ref[...] = jnp.zeros(state_ref.shape, F32)

    rin = lax.broadcasted_iota(I32, (tc, dk), 0) & (ch - 1)
    causal = (lax.broadcasted_iota(I32, (ch, ch), 1) <= lax.broadcasted_iota(I32, (ch, ch), 0))
    eye = (lax.broadcasted_iota(I32, (dk, dk), 0) == lax.broadcasted_iota(I32, (dk, dk), 1))
    chunks = [slice(n * ch, (n + 1) * ch) for n in range(nchunk)]
    gk = gk_ref[...]
    gk_hi = gk.astype(BF16)
    gk_lo = (gk - gk_hi.astype(F32)).astype(BF16)

    def prep(h, _):
        ks = slice(h * dk, (h + 1) * dk)
        w = wup_ref[:, ks]
        w_hi = w.astype(BF16)
        w_lo = (w - w_hi.astype(F32)).astype(BF16)
        z = (jnp.dot(gk_hi, w_hi, preferred_element_type=F32)
             + jnp.dot(gk_lo, w_hi, preferred_element_type=F32)
             + jnp.dot(gk_hi, w_lo, preferred_element_type=F32) + bgk_ref[:, ks])
        log_a = ((jnp.minimum(z, 0.0) - jnp.log(1.0 + jnp.exp(-jnp.abs(z))))
                 / GLA_GATE_NORMALIZER)
        b = log_a
        sh = 1
        while sh < ch:
            b = b + jnp.where(rin >= sh, pltpu.roll(b, sh, axis=0), 0.0)
            sh *= 2
        q = q_ref[:, ks].astype(F32) * (dk ** -0.5)
        k = k_ref[:, ks].astype(F32)
        q_g = (q * jnp.exp(b)).astype(BF16)
        k_g = (k * jnp.exp(-b)).astype(BF16)
        b3 = b.reshape(nchunk, ch, dk)
        b_last = b3[:, ch - 1:ch, :]
        k_end = (k * jnp.exp(jnp.broadcast_to(b_last, b3.shape) - b3).reshape(tc, dk)
                 ).astype(BF16)
        return q_g, k_g, k_end, jnp.exp(b_last)

    def local(h, pre):
        q_g, k_g, k_end, decay = pre
        o_intra, kv, decay_col = [], [], []
        for n, sl in enumerate(chunks):
            v_c = v_ref[sl, h * dv:(h + 1) * dv]
            att = jnp.where(causal, _nt_dot(q_g[sl], k_g[sl]), 0.0)
            o_intra.append(jnp.dot(att.astype(BF16), v_c, preferred_element_type=F32))
            kv.append(lax.dot_general(k_end[sl], v_c, (((0,), (0,)), ((), ())),
                                      preferred_element_type=F32))
            decay_col.append(jnp.sum(
                jnp.where(eye, jnp.broadcast_to(decay[n], (dk, dk)), 0.0),
                axis=1, keepdims=True))
        return q_g, o_intra, kv, decay_col

    def chain(h, loc):
        q_g, o_intra, kv, decay_col = loc
        state = state_ref[h]
        outs = []
        for n, sl in enumerate(chunks):
            outs.append(o_intra[n] + jnp.dot(q_g[sl], state.astype(BF16),
                                             preferred_element_type=F32))
            state = decay_col[n] * state + kv[n]
        state_ref[h] = state
        return jnp.concatenate(outs, axis=0)

    def finish(h, o):
        vs = slice(h * dv, (h + 1) * dv)
        o_ref[:, vs] = (_rms(o) * gn_ref[...] * r_ref[:, vs].astype(F32)).astype(o_ref.dtype)
        return None

    stages = [prep, local, chain, finish]
    vals = [None] * GLA_HEADS
    for step in range(GLA_HEADS + len(stages) - 1):
        for si, stage in enumerate(stages):
            h = step - si
            if 0 <= h < GLA_HEADS:
                vals[h] = stage(h, vals[h])


def _gla(qkb, vb, gk, wup, bgk, r_act, gn, bsz, seq, nchunk):
    t = qkb.shape[0]
    tc = nchunk * GLA_CHUNK
    nc = seq // tc
    rowblk = lambda w, off: pl.BlockSpec((tc, w), lambda b, c: (b * nc + c, off))
    full = lambda a: pl.BlockSpec(a.shape, lambda b, c: (0, 0))
    return pl.pallas_call(
        functools.partial(_gla_kernel, nchunk=nchunk),
        grid=(bsz, nc),
        in_specs=[rowblk(GLA_KEY_DIM, 0), rowblk(GLA_KEY_DIM, 1), rowblk(GLA_VALUE_DIM, 0),
                  rowblk(LANES, 0), full(wup), full(bgk), rowblk(GLA_VALUE_DIM, 0), full(gn)],
        out_specs=rowblk(GLA_VALUE_DIM, 0),
        out_shape=jax.ShapeDtypeStruct((t, GLA_VALUE_DIM), BF16),
        scratch_shapes=[pltpu.VMEM((GLA_HEADS, GLA_DK, GLA_DV), F32)],
        compiler_params=_cparams(("arbitrary", "arbitrary")),
        name="gla",
    )(qkb, qkb, vb, gk, wup, bgk, r_act, gn)


def _merge_kernel(ya_ref, yb_ref, g_ref, x_ref, wpa_ref, wpb_ref, wout_ref, g1_ref,
                  nw_ref, sc_ref, sh_ref, wr_ref, br_ref,
                  x1_ref, h2_ref, idx_ref, wts_ref):
    pa = lax.dot_general(ya_ref[0], wpa_ref[...], (((0,), (0,)), ((), ())),
                         preferred_element_type=F32)
    pb = jnp.dot(yb_ref[...], wpb_ref[...], preferred_element_type=F32)
    mixed = (g_ref[:, 0:D_MODEL].astype(F32) * pa
             + g_ref[:, D_MODEL:2 * D_MODEL].astype(F32) * pb)
    y = jnp.dot(mixed.astype(BF16), wout_ref[...], preferred_element_type=F32)
    x1 = x_ref[...] + g1_ref[0] * y
    x1_ref[...] = x1
    h2 = _rms(x1) * nw_ref[...]
    h2 = h2 * (1.0 + sc_ref[0]) + sh_ref[0]
    _store_rows(h2_ref, _pack_halves(h2))
    h_hi = h2.astype(BF16)
    h_lo = (h2 - h_hi.astype(F32)).astype(BF16)
    w = wr_ref[...]
    w_hi = w.astype(BF16)
    w_lo = (w - w_hi.astype(F32)).astype(BF16)
    logits = (_nt_dot(w_hi, h_hi) + _nt_dot(w_hi, h_lo) + _nt_dot(w_lo, h_hi)
              + br_ref[...])
    rowf = lax.broadcasted_iota(I32, logits.shape, 0).astype(F32)
    vals, idxs = [], []
    cur = logits
    for _ in range(TOP_K):
        mx = jnp.max(cur, axis=0, keepdims=True)
        first = jnp.min(jnp.where(cur == mx, rowf, float(N_EXPERTS)), axis=0, keepdims=True)
        vals.append(mx)
        idxs.append(first)
        cur = jnp.where(rowf == first, -jnp.inf, cur)
    es = [jnp.exp(v - vals[0]) for v in vals]
    tot = es[0]
    for e in es[1:]:
        tot = tot + e
    idx_ref[...] = jnp.concatenate(idxs, axis=0).astype(I32)
    wts_ref[...] = jnp.concatenate([e / tot for e in es], axis=0)


def _merge(ya, yb, gates, x2d, wpa, wpb, wout, g1, nw, sc, sh, wr_t, br, seq, tm):
    t, d = x2d.shape
    tpb = seq // tm
    row = lambda w: pl.BlockSpec((tm, w), lambda i: (i, 0))
    full = lambda a: pl.BlockSpec(a.shape, lambda i: (0,) * a.ndim)
    per_b = pl.BlockSpec((1, 1, d), lambda i: (i // tpb, 0, 0))
    colblk = pl.BlockSpec((TOP_K, tm), lambda i: (0, i))
    return pl.pallas_call(
        _merge_kernel,
        grid=(t // tm,),
        in_specs=[pl.BlockSpec((1, MOBA_WIDTH, tm), lambda i: (i // tpb, 0, i % tpb)),
                  row(GLA_VALUE_DIM), row(2 * D_MODEL), row(d),
                  full(wpa), full(wpb), full(wout), per_b,
                  pl.BlockSpec((1, d), lambda i: (0, 0)), per_b, per_b,
                  full(wr_t), full(br)],
        out_specs=[row(d), pl.BlockSpec((tm * ROW_SLABS, LANES), lambda i: (i, 0)),
                   colblk, colblk],
        out_shape=[jax.ShapeDtypeStruct((t, d), F32),
                   jax.ShapeDtypeStruct((t * ROW_SLABS, LANES), U32),
                   jax.ShapeDtypeStruct((TOP_K, t), I32), jax.ShapeDtypeStruct((TOP_K, t), F32)],
        compiler_params=_cparams(("arbitrary",)),
        name="merge",
    )(ya, yb, gates, x2d, wpa, wpb, wout, g1, nw.reshape(1, d), sc, sh, wr_t, br)


def _rank_kernel(idx_ref, rank_ref, cnt_ref, carry_ref):
    tm = idx_ref.shape[1]

    @pl.when(pl.program_id(0) == 0)
    def _():
        carry_ref[...] = jnp.zeros(carry_ref.shape, F32)

    rows = lax.broadcasted_iota(I32, (N_EXPERTS, tm), 0)
    before = (lax.broadcasted_iota(I32, (tm, tm), 0)
              < lax.broadcasted_iota(I32, (tm, tm), 1))
    upper = jnp.where(before, 1.0, 0.0).astype(BF16)
    carry = carry_ref[:, 0:1]
    ranks = []
    for k in range(TOP_K):
        onehot = idx_ref[k:k + 1, :] == rows
        onef = jnp.where(onehot, 1.0, 0.0)
        earlier = jnp.dot(onef.astype(BF16), upper, preferred_element_type=F32) + carry
        ranks.append(jnp.sum(jnp.where(onehot, earlier, 0.0), axis=0, keepdims=True))
        carry = carry + jnp.sum(onef, axis=1, keepdims=True)
    rank_ref[...] = jnp.concatenate(ranks, axis=0).astype(I32)
    total = jnp.broadcast_to(carry, carry_ref.shape)
    carry_ref[...] = total
    cnt_ref[...] = total


def _rank(idx, tm):
    t = idx.shape[1]
    return pl.pallas_call(
        _rank_kernel,
        grid=(t // tm,),
        in_specs=[pl.BlockSpec((TOP_K, tm), lambda i: (0, i))],
        out_specs=[pl.BlockSpec((TOP_K, tm), lambda i: (0, i)),
                   pl.BlockSpec((N_EXPERTS, LANES), lambda i: (0, 0))],
        out_shape=[jax.ShapeDtypeStruct((TOP_K, t), I32),
                   jax.ShapeDtypeStruct((N_EXPERTS, LANES), F32)],
        scratch_shapes=[pltpu.VMEM((N_EXPERTS, LANES), F32)],
        compiler_params=_cparams(("arbitrary",)),
        name="rank",
    )(idx)


def _dest_kernel(pstart_ref, idx_ref, rank_ref, dest_ref):
    idx = idx_ref[...]
    off = jnp.zeros(idx.shape, I32)
    for e in range(N_EXPERTS):
        off = jnp.where(idx == e, pstart_ref[e], off)
    dest_ref[...] = rank_ref[...] + off


def _dest(pstart, idx, rank, tm):
    t = idx.shape[1]
    blk = pl.BlockSpec((TOP_K, tm), lambda i: (0, i))
    return pl.pallas_call(
        _dest_kernel,
        grid=(t // tm,),
        in_specs=[pl.BlockSpec(memory_space=pltpu.SMEM), blk, blk],
        out_specs=blk,
        out_shape=jax.ShapeDtypeStruct((TOP_K, t), I32),
        compiler_params=_cparams(("arbitrary",)),
        name="dest",
    )(pstart, idx, rank)


def _scatter_kernel(dest_ref, zrow_ref, h_ref, xout_hbm, zbuf, stage, sems, zsem, *, tm):
    @pl.when(pl.program_id(0) == 0)
    def _():
        zbuf[...] = jnp.zeros(zbuf.shape, zbuf.dtype)

        def zero_rows(row0):
            line0 = pl.multiple_of(row0 * ROW_SLABS, MOE_ROWS * ROW_SLABS)
            return pltpu.make_async_copy(
                zbuf, xout_hbm.at[pl.ds(line0, MOE_ROWS * ROW_SLABS)], zsem)

        for e in range(N_EXPERTS):
            zero_rows(zrow_ref[e]).start()
        for e in range(N_EXPERTS):
            zero_rows(zrow_ref[e]).wait()

        def start_tail(j, carry):
            zero_rows(j * MOE_ROWS).start()
            return carry

        def wait_tail(j, carry):
            zero_rows(j * MOE_ROWS).wait()
            return carry

        n_all = xout_hbm.shape[0] // (MOE_ROWS * ROW_SLABS)
        lax.fori_loop(zrow_ref[N_EXPERTS], n_all, start_tail, 0)
        lax.fori_loop(zrow_ref[N_EXPERTS], n_all, wait_tail, 0)

    i = pl.program_id(0)
    slot = i % 2
    src = stage.at[slot]
    src[...] = h_ref[...]

    def issue(g, carry):
        base = pl.multiple_of(g * ROW_UNROLL, ROW_UNROLL)
        for j in range(ROW_UNROLL):
            for k in range(TOP_K):
                d = dest_ref[(base + j) * TOP_K + k]
                pltpu.make_async_copy(_row(src, base + j), _row(xout_hbm, d),
                                      sems.at[slot]).start(priority=k % 2)
        return carry

    lax.fori_loop(0, tm // ROW_UNROLL, issue, 0)

    def wait_tile(s):
        for k in range(TOP_K):
            pltpu.make_async_copy(stage.at[s], xout_hbm.at[pl.ds(0, tm * ROW_SLABS)],
                                  sems.at[s]).wait()

    @pl.when(i > 0)
    def _():
        wait_tile(1 - slot)

    @pl.when(i == pl.num_programs(0) - 1)
    def _():
        wait_tile(slot)


def _scatter(dest_tok, last_block_row, h2, n_pad, tm):
    t = h2.shape[0] // ROW_SLABS
    return pl.pallas_call(
        functools.partial(_scatter_kernel, tm=tm),
        grid=(t // tm,),
        in_specs=[pl.BlockSpec((tm * TOP_K,), lambda i: (i,), memory_space=pltpu.SMEM),
                  pl.BlockSpec(memory_space=pltpu.SMEM),
                  pl.BlockSpec((tm * ROW_SLABS, LANES), lambda i: (i, 0))],
        out_specs=pl.BlockSpec(memory_space=pl.ANY),
        out_shape=jax.ShapeDtypeStruct((n_pad * ROW_SLABS, LANES), h2.dtype),
        scratch_shapes=[pltpu.VMEM((MOE_ROWS * ROW_SLABS, LANES), h2.dtype),
                        pltpu.VMEM((2, tm * ROW_SLABS, LANES), h2.dtype),
                        pltpu.SemaphoreType.DMA((2,)), pltpu.SemaphoreType.DMA],
        compiler_params=_cparams(("arbitrary",)),
        name="scatter",
    )(dest_tok, last_block_row, h2)


def _expert_kernel(be_ref, nused_ref, x_ref, wg_ref, bg_ref, wu_ref, bu_ref, wd_ref, bd_ref,
                   o_ref, wgb_ref, wub_ref, wdb_ref):
    i = pl.program_id(0)
    prev = be_ref[jnp.maximum(i - 1, 0)]

    @pl.when(jnp.logical_or(i == 0, be_ref[i] != prev))
    def _():
        wgb_ref[...] = wg_ref[0].astype(BF16)
        wub_ref[...] = wu_ref[0].astype(BF16)
        wdb_ref[...] = wd_ref[0].astype(BF16)

    @pl.when(i < nused_ref[0])
    def _():
        x_lo, x_hi = [v.astype(BF16) for v in _unpack_halves(_load_rows(x_ref))]
        half = x_lo.shape[1]

        def in_dot(w_ref):
            return (jnp.dot(x_lo, w_ref[0:half, :], preferred_element_type=F32)
                    + jnp.dot(x_hi, w_ref[half:2 * half, :], preferred_element_type=F32))

        g = jnp.minimum(in_dot(wgb_ref) + bg_ref[0], SWIGLU_LIMIT)
        u = jnp.clip(in_dot(wub_ref) + bu_ref[0], -SWIGLU_LIMIT, SWIGLU_LIMIT)
        act = g * _sigmoid(SWIGLU_ALPHA * g) * (u + 1.0)
        _store_rows(o_ref, _pack_halves(
            jnp.dot(act.astype(BF16), wdb_ref[...], preferred_element_type=F32) + bd_ref[0]))

    @pl.when(i >= nused_ref[0])
    def _():
        o_ref[...] = jnp.zeros(o_ref.shape, o_ref.dtype)


def _expert(block_expert, n_used, x_pad, wg, bg, wu, bu, wd, bd):
    lines = MOE_ROWS * ROW_SLABS
    _, d, f = wg.shape
    nblk = x_pad.shape[0] // lines
    wspec = lambda a, b: pl.BlockSpec((1, a, b), lambda i, be, nu: (be[i], 0, 0))
    grid_spec = pltpu.PrefetchScalarGridSpec(
        num_scalar_prefetch=2,
        grid=(nblk,),
        in_specs=[pl.BlockSpec((lines, LANES),
                               lambda i, be, nu: (jnp.minimum(i, nu[0] - 1), 0)),
                  wspec(d, f), wspec(1, f), wspec(d, f), wspec(1, f), wspec(f, d), wspec(1, d)],
        out_specs=pl.BlockSpec((lines, LANES), lambda i, be, nu: (i, 0)),
        scratch_shapes=[pltpu.VMEM((d, f), BF16), pltpu.VMEM((d, f), BF16),
                        pltpu.VMEM((f, d), BF16)])
    return pl.pallas_call(
        _expert_kernel,
        grid_spec=grid_spec,
        out_shape=jax.ShapeDtypeStruct(x_pad.shape, U32),
        compiler_params=_cparams(("arbitrary",)),
        name="expert",
    )(block_expert, n_used, x_pad, wg, bg.reshape(N_EXPERTS, 1, f), wu,
      bu.reshape(N_EXPERTS, 1, f), wd, bd.reshape(N_EXPERTS, 1, d))


def _combine_kernel(dest_ref, y_hbm, x1_ref, wts_ref, g2_ref, nf_ref, o_ref, ybuf, sems, *,
                    tm, final):
    i = pl.program_id(0)
    n_tiles = pl.num_programs(0) - 1
    slot = i % 2

    @pl.when(i < n_tiles)
    def _():
        def row_copy(tt, k):
            d = dest_ref[tt * TOP_K + k]
            return pltpu.make_async_copy(_row(y_hbm, d), _row(ybuf.at[slot, k], tt),
                                         sems.at[slot])

        def issue(g, carry):
            base = pl.multiple_of(g * ROW_UNROLL, ROW_UNROLL)
            for j in range(ROW_UNROLL):
                for k in range(TOP_K):
                    row_copy(base + j, k).start(priority=k % 2)
            return carry

        lax.fori_loop(0, tm // ROW_UNROLL, issue, 0)

    @pl.when(i > 0)
    def _():
        done = 1 - slot
        for k in range(TOP_K):
            pltpu.make_async_copy(y_hbm.at[pl.ds(0, tm * ROW_SLABS)], ybuf.at[done, k],
                                  sems.at[done]).wait()
        w = wts_ref[...]
        lo, hi = None, None
        for k in range(TOP_K):
            y_lo, y_hi = _unpack_halves(_load_rows(ybuf.at[done, k]))
            lo = w[:, k:k + 1] * y_lo + (0.0 if lo is None else lo)
            hi = w[:, k:k + 1] * y_hi + (0.0 if hi is None else hi)
        moe = jnp.concatenate([lo, hi], axis=1)
        x2 = x1_ref[...] + g2_ref[0] * moe
        o_ref[...] = _rms(x2) * nf_ref[...] if final else x2


def _combine(dest_tok, y_pad, x1, wts_tok, g2, nf, seq, tm, final):
    t, d = x1.shape
    tpb = seq // tm
    n_tiles = t // tm
    lag = lambda i: jnp.maximum(i - 1, 0)
    return pl.pallas_call(
        functools.partial(_combine_kernel, tm=tm, final=final),
        grid=(n_tiles + 1,),
        in_specs=[pl.BlockSpec((tm * TOP_K,), lambda i: (jnp.minimum(i, n_tiles - 1),),
                               memory_space=pltpu.SMEM),
                  pl.BlockSpec(memory_space=pl.ANY),
                  pl.BlockSpec((tm, d), lambda i: (lag(i), 0)),
                  pl.BlockSpec((tm, TOP_K), lambda i: (lag(i), 0)),
                  pl.BlockSpec((1, 1, d), lambda i: (lag(i) // tpb, 0, 0)),
                  pl.BlockSpec((1, d), lambda i: (0, 0))],
        out_specs=pl.BlockSpec((tm, d), lambda i: (lag(i), 0)),
        out_shape=jax.ShapeDtypeStruct((t, d), F32),
        scratch_shapes=[pltpu.VMEM((2, TOP_K, tm * ROW_SLABS, LANES), y_pad.dtype),
                        pltpu.SemaphoreType.DMA((2,))],
        compiler_params=_cparams(("arbitrary",)),
        name="combine",
    )(dest_tok, y_pad, x1, wts_tok, g2, nf.reshape(1, d))


def _pick(n, cands):
    for c in cands:
        if n % c == 0:
            return c
    raise ValueError(f"no tile in {cands} divides {n}")


def kernel(x, c, rel_bias, w_ada, b_ada, norm_mix, w_in, w_gk_up, b_gk, gla_norm,
           w_proj_moba, w_proj_gla, w_out, norm_ffn, w_router, b_router,
           w_gate, b_gate, w_up, b_up, w_down, b_down, norm_final):
    bsz, seq, d = x.shape
    depth = w_ada.shape[0]
    assert d == D_MODEL and seq % MOBA_BLOCK == 0 and seq // MOBA_BLOCK <= MOBA_MAX_BLOCKS
    t = bsz * seq
    tm = _pick(seq, (512, 256))
    nchunk = _pick(seq // GLA_CHUNK, (8, 4))
    n_blk = seq // MOBA_BLOCK
    x2d = x.reshape(t, d)
    bias = _bias_tiles(rel_bias)
    per_b = lambda v: v.reshape(bsz, 1, d)

    for l in range(depth):
        mod = _ada(c, w_ada[l], b_ada[l])
        sh1, sc1, g1, sh2, sc2, g2 = [per_b(m) for m in jnp.split(mod, 6, axis=-1)]
        qt, k_aug, vt, qkb, vb, gk, r_act, gates = _inproj(
            x2d, norm_mix[l], sc1, sh1, _regroup_w_in(w_in[l]), bsz, seq, tm)
        ya = _moba(qt, k_aug, vt, bias)
        wup = jnp.pad(w_gk_up[l], ((0, LANES - GLA_GATE_RANK), (0, 0)))
        yb = _gla(qkb, vb, gk, wup, b_gk[l].reshape(1, -1), r_act,
                  gla_norm[l].reshape(1, -1), bsz, seq, nchunk)
        x1, h2, idx, wts = _merge(
            ya, yb, gates, x2d, w_proj_moba[l].astype(BF16), w_proj_gla[l].astype(BF16),
            w_out[l].astype(BF16), g1, norm_ffn[l], sc2, sh2,
            w_router[l].T, b_router[l].reshape(N_EXPERTS, 1), seq, tm)
        rank, cnt = _rank(idx, tm)
        counts = cnt[:, 0].astype(I32)
        padded = (counts + MOE_ROWS - 1) // MOE_ROWS * MOE_ROWS
        pcum = jnp.cumsum(padded)
        pstart = (pcum - padded).astype(I32)
        n_blocks = (t * TOP_K + MOE_ROWS - 1) // MOE_ROWS + N_EXPERTS
        block_row0 = jnp.arange(n_blocks, dtype=I32) * MOE_ROWS
        block_expert = jnp.minimum(
            jnp.sum((pcum[None, :] <= block_row0[:, None]).astype(I32), axis=1),
            N_EXPERTS - 1).astype(I32)
        n_used = (pcum[-1:] // MOE_ROWS).astype(I32)
        dest = _dest(pstart, idx, rank, tm)
        dest_tok = dest.T.reshape(t * TOP_K)
        tg = _pick(t, (256,))
        zero_info = jnp.concatenate([jnp.maximum(pcum - MOE_ROWS, 0).astype(I32), n_used])
        x_pad = _scatter(dest_tok, zero_info, h2, n_blocks * MOE_ROWS, tg)
        y_pad = _expert(block_expert, n_used, x_pad, w_gate[l], b_gate[l], w_up[l], b_up[l],
                        w_down[l], b_down[l])
        x2d = _combine(dest_tok, y_pad, x1, wts.T, g2, norm_final, seq, tg, l == depth - 1)
    return x2d.reshape(bsz, seq, d)
```

```python
import functools
import math

import numpy as np
import jax
import jax.numpy as jnp
from jax import lax
from jax.experimental import pallas as pl
from jax.experimental.pallas import tpu as pltpu

F32 = jnp.float32
BF16 = jnp.bfloat16
I32 = jnp.int32
HIGHEST = lax.Precision.HIGHEST

D_MODEL = 1024
MOBA_HEADS = 8
MOBA_HEAD_DIM = 64
MOBA_WIDTH = MOBA_HEADS * MOBA_HEAD_DIM
MOBA_BLOCK = 256
MOBA_TOPK = 3
MOBA_MAX_BLOCKS = 32
REL_BUCKETS = 32
REL_MAX_DIST = 128
GLA_HEADS = 4
GLA_KEY_DIM = D_MODEL // 2
GLA_VALUE_DIM = D_MODEL
GLA_DK = GLA_KEY_DIM // GLA_HEADS
GLA_DV = GLA_VALUE_DIM // GLA_HEADS
GLA_GATE_RANK = 16
GLA_GATE_NORMALIZER = 16.0
GLA_CHUNK = 64
N_EXPERTS = 32
TOP_K = 4
D_FF = D_MODEL
SWIGLU_ALPHA = 1.702
SWIGLU_LIMIT = 7.0
MOE_ROWS = 512
ROW_UNROLL = 8
EPS = 1e-6
LANES = 128
NEG_BIG = -1e30
LOG2E = math.log2(math.e)
VMEM_LIMIT = 56 * 1024 * 1024


def _cparams(sem, vmem=None):
    return pltpu.CompilerParams(dimension_semantics=sem,
                                vmem_limit_bytes=vmem or VMEM_LIMIT)


def _nt_dot(a, b, **kw):
    return lax.dot_general(a, b, (((1,), (1,)), ((), ())),
                           preferred_element_type=F32, **kw)


def _rms(x):
    return x * lax.rsqrt(jnp.mean(x * x, axis=-1, keepdims=True) + EPS)


def _sigmoid(x):
    return 1.0 / (1.0 + jnp.exp(-x))


U32 = jnp.uint32
_HI16 = 0xFFFF0000


def _pack_halves(x):
    n = x.shape[1] // 2
    lo = pltpu.bitcast(x[:, :n].astype(BF16).astype(F32), U32)
    hi = pltpu.bitcast(x[:, n:].astype(BF16).astype(F32), U32)
    return (hi & U32(_HI16)) | (lo >> 16)


def _unpack_halves(w):
    return (pltpu.bitcast(w << 16, F32), pltpu.bitcast(w & U32(_HI16), F32))


ROW_SLABS = D_MODEL // 2 // LANES


def _store_rows(ref, words):
    m = words.shape[0]
    for c in range(ROW_SLABS):
        ref[pl.ds(c, m, stride=ROW_SLABS), :] = words[:, c * LANES:(c + 1) * LANES]


def _load_rows(ref):
    m = ref.shape[0] // ROW_SLABS
    return jnp.concatenate(
        [ref[pl.ds(c, m, stride=ROW_SLABS), :] for c in range(ROW_SLABS)], axis=1)


def _row(ref, i):
    return ref.at[pl.ds(pl.multiple_of(i * ROW_SLABS, ROW_SLABS), ROW_SLABS)]


def _ada_kernel(c_ref, w_ref, b_ref, o_ref):
    c = c_ref[...]
    s = c * _sigmoid(c)
    o_ref[...] = jnp.dot(s, w_ref[...], precision=HIGHEST,
                         preferred_element_type=F32) + b_ref[...]


def _ada(c, w, b):
    bsz, d = c.shape
    n = w.shape[1]
    rows = -(-bsz // 8) * 8
    cp = jnp.zeros((rows, d), F32).at[:bsz].set(c)
    tn = 768
    out = pl.pallas_call(
        _ada_kernel,
        grid=(n // tn,),
        in_specs=[pl.BlockSpec((rows, d), lambda j: (0, 0)),
                  pl.BlockSpec((d, tn), lambda j: (0, j)),
                  pl.BlockSpec((1, tn), lambda j: (0, j))],
        out_specs=pl.BlockSpec((rows, tn), lambda j: (0, j)),
        out_shape=jax.ShapeDtypeStruct((rows, n), F32),
        compiler_params=_cparams(("arbitrary",)),
        name="ada",
    )(cp, w, b.reshape(1, n))
    return out[:bsz]


_OFF_QA = 0
_OFF_KA = _OFF_QA + MOBA_WIDTH
_OFF_VA = _OFF_KA + MOBA_WIDTH
_OFF_QKB = _OFF_VA + MOBA_WIDTH
_OFF_VB = _OFF_QKB + 2 * GLA_KEY_DIM
_OFF_GK = _OFF_VB + GLA_VALUE_DIM
_OFF_R = _OFF_GK + LANES
_OFF_G = _OFF_R + GLA_VALUE_DIM
_W_CAT = _OFF_G + 2 * D_MODEL


def _regroup_w_in(w):
    o_gk = 3 * MOBA_WIDTH + 2 * GLA_KEY_DIM + GLA_VALUE_DIM
    gk = jnp.pad(w[:, o_gk:o_gk + GLA_GATE_RANK], ((0, 0), (0, LANES - GLA_GATE_RANK)))
    return jnp.concatenate([w[:, :o_gk], gk, w[:, o_gk + GLA_GATE_RANK:]],
                           axis=1).astype(BF16)


def _inproj_kernel(x_ref, nw_ref, sc_ref, sh_ref, w_ref,
                   qt_ref, ka_ref, vt_ref, qkb_ref, vb_ref, gk_ref, r_ref, g_ref, *, tpb):
    tm = x_ref.shape[0]
    hd = MOBA_HEAD_DIM
    nbt = tm // MOBA_BLOCK
    h = _rms(x_ref[...]) * nw_ref[...]
    h = h * (1.0 + sc_ref[0]) + sh_ref[0]
    hb = h.astype(BF16)

    def mm(a, b):
        return jnp.dot(hb, w_ref[:, a:b], preferred_element_type=F32)

    q_t = (mm(_OFF_QA, _OFF_KA) * (hd ** -0.5 * LOG2E)).T
    v_t = mm(_OFF_VA, _OFF_QKB).T
    k_all = mm(_OFF_KA, _OFF_VA)
    blk0 = (pl.program_id(0) % tpb) * nbt
    lane = lax.broadcasted_iota(I32, (MOBA_BLOCK, LANES), 1)
    ones_rows = jnp.where(
        lax.broadcasted_iota(I32, (MOBA_VT_ROWS - hd, MOBA_BLOCK), 0) == 0, 1.0, 0.0)
    heads_per_tile = LANES // hd
    for hh in range(MOBA_HEADS):
        qt_ref[0, hh] = q_t[hh * hd:(hh + 1) * hd].astype(BF16)
        tile = hh // heads_per_tile
        k_h = k_all[:, tile * LANES:(tile + 1) * LANES]
        if hh % heads_per_tile:
            k_h = pltpu.roll(k_h, LANES - (hh % heads_per_tile) * hd, axis=1)
        for j in range(nbt):
            rows = slice(j * MOBA_BLOCK, (j + 1) * MOBA_BLOCK)
            onehot = jnp.where(lane == hd + blk0 + j, 1.0, 0.0)
            ka_ref[0, hh, j] = jnp.where(lane < hd, k_h[rows], onehot).astype(BF16)
            vt_ref[0, hh, j, 0:hd, :] = v_t[hh * hd:(hh + 1) * hd, rows].astype(BF16)
            vt_ref[0, hh, j, hd:MOBA_VT_ROWS, :] = ones_rows.astype(BF16)
    qkb_ref[...] = mm(_OFF_QKB, _OFF_VB).astype(BF16)
    vb_ref[...] = mm(_OFF_VB, _OFF_GK).astype(BF16)
    gk_ref[...] = mm(_OFF_GK, _OFF_R)
    r = mm(_OFF_R, _OFF_G)
    r_ref[...] = (r * _sigmoid(r)).astype(BF16)
    g_ref[...] = _sigmoid(mm(_OFF_G, _W_CAT)).astype(BF16)


def _inproj(x2d, nw, sc, sh, w_cat, bsz, seq, tm):
    t, d = x2d.shape
    tpb = seq // tm
    nbt = tm // MOBA_BLOCK
    nh, hd = MOBA_HEADS, MOBA_HEAD_DIM
    row = lambda w: pl.BlockSpec((tm, w), lambda i: (i, 0))
    per_b = pl.BlockSpec((1, 1, d), lambda i: (i // tpb, 0, 0))
    rows_out = [(2 * GLA_KEY_DIM, BF16), (GLA_VALUE_DIM, BF16), (LANES, F32),
                (GLA_VALUE_DIM, BF16), (2 * D_MODEL, BF16)]
    return pl.pallas_call(
        functools.partial(_inproj_kernel, tpb=tpb),
        grid=(t // tm,),
        in_specs=[row(d), pl.BlockSpec((1, d), lambda i: (0, 0)), per_b, per_b,
                  pl.BlockSpec((d, _W_CAT), lambda i: (0, 0), pipeline_mode=pl.Buffered(1))],
        out_specs=[pl.BlockSpec((1, nh, hd, tm), lambda i: (i // tpb, 0, 0, i % tpb)),
                   pl.BlockSpec((1, nh, nbt, MOBA_BLOCK, LANES),
                                lambda i: (i // tpb, 0, i % tpb, 0, 0)),
                   pl.BlockSpec((1, nh, nbt, MOBA_VT_ROWS, MOBA_BLOCK),
                                lambda i: (i // tpb, 0, i % tpb, 0, 0))]
                  + [row(w) for w, _ in rows_out],
        out_shape=[jax.ShapeDtypeStruct((bsz, nh, hd, seq), BF16),
                   jax.ShapeDtypeStruct((bsz, nh, seq // MOBA_BLOCK, MOBA_BLOCK, LANES), BF16),
                   jax.ShapeDtypeStruct((bsz, nh, seq // MOBA_BLOCK, MOBA_VT_ROWS, MOBA_BLOCK),
                                        BF16)]
                  + [jax.ShapeDtypeStruct((t, w), dt) for w, dt in rows_out],
        compiler_params=_cparams(("arbitrary",)),
        name="inproj",
    )(x2d, nw.reshape(1, d), sc, sh, w_cat)


def _t5_bucket_np(n):
    n = np.maximum(n, 0)
    max_exact = REL_BUCKETS // 2
    nf = np.maximum(n, max_exact).astype(np.float32)
    large = max_exact + (np.log(nf / max_exact) / math.log(REL_MAX_DIST / max_exact)
                         * (REL_BUCKETS - max_exact)).astype(np.int32)
    large = np.minimum(large, REL_BUCKETS - 1)
    return np.where(n < max_exact, n, large).astype(np.int32)


def _bucket_table():
    kj = np.arange(MOBA_BLOCK)[:, None]
    qi = np.arange(2 * MOBA_BLOCK)[None, :] % MOBA_BLOCK
    prev = np.arange(2 * MOBA_BLOCK)[None, :] < MOBA_BLOCK
    bucket = _t5_bucket_np(qi - kj + np.where(prev, MOBA_BLOCK, 0))
    return np.where(prev | (kj <= qi), bucket, -1).astype(np.int32)


def _bias_kernel(rb_ref, bucket_ref, o_ref):
    h = pl.program_id(0)
    bk = bucket_ref[...]
    far = rb_ref[(REL_BUCKETS - 1) * MOBA_HEADS + h]
    acc = jnp.zeros(bk.shape, F32)
    for b in range(REL_BUCKETS):
        acc = jnp.where(bk == b, rb_ref[b * MOBA_HEADS + h] - far, acc)
    o_ref[0] = jnp.where(bk < 0, NEG_BIG, acc * LOG2E)


def _bias_tiles(rel_bias):
    bucket = jnp.asarray(_bucket_table())
    return pl.pallas_call(
        _bias_kernel,
        grid=(MOBA_HEADS,),
        in_specs=[pl.BlockSpec(memory_space=pltpu.SMEM),
                  pl.BlockSpec(bucket.shape, lambda h: (0, 0))],
        out_specs=pl.BlockSpec((1,) + bucket.shape, lambda h: (h, 0, 0)),
        out_shape=jax.ShapeDtypeStruct((MOBA_HEADS,) + bucket.shape, F32),
        compiler_params=_cparams(("arbitrary",)),
        name="bias",
    )(rel_bias.reshape(-1), bucket)


MOBA_HEADS_PER_STEP = 8
MOBA_FAR_GROUP = 2
MOBA_QBLOCKS_PER_STEP = 4
MOBA_FAR_LAGS = (0, 3, 6)
MOBA_NEAR_LAGS = (0, 3, 6, 9)
MOBA_VT_ROWS = MOBA_HEAD_DIM + 16


def _moba_kernel(qt_ref, k_ref, vt_ref, bias_ref, o_ref, kmean_ref, qa_ref):
    blk = MOBA_BLOCK
    nb = MOBA_MAX_BLOCKS
    hd = MOBA_HEAD_DIM
    hp = MOBA_HEADS_PER_STEP
    grp = MOBA_FAR_GROUP
    nq = MOBA_QBLOCKS_PER_STEP
    pair = pl.program_id(2)
    neg = -jnp.inf
    items = [(h, j) for j in range(nq) for h in range(hp)]

    @pl.when(pair == 0)
    def _():
        kmean_ref[...] = jnp.zeros(kmean_ref.shape, F32)

    for h in range(hp):
        for j in range(nq):
            kmean_ref[h, pl.ds(pair * nq + j, 1), :] = jnp.mean(
                k_ref[0, h, pair * nq + j].astype(F32), axis=0, keepdims=True)

    row = lax.broadcasted_iota(I32, (nb, blk), 0)
    rowf = row.astype(F32)
    pad = jnp.zeros((LANES - hd, blk), BF16)
    pad_hi = jnp.zeros((LANES - hd - nb, blk), BF16)

    def skewed(stages, lags, todo=None):
        todo = list(range(len(items))) if todo is None else todo
        vals = {}
        for step in range(len(todo) + lags[-1]):
            for stage, lag in zip(stages, lags):
                pos = step - lag
                if 0 <= pos < len(todo):
                    vals[todo[pos]] = stage(todo[pos], vals.get(todo[pos]))
        return vals

    def select(n, _):
        h, j = items[n]
        qi = pair * nq + j
        qt = qt_ref[0, h, :, j * blk:(j + 1) * blk]
        km = kmean_ref[h, :, 0:hd]
        km_hi = km.astype(BF16)
        km_lo = (km - km_hi.astype(F32)).astype(BF16)
        gate = (jnp.dot(km_hi, qt, preferred_element_type=F32)
                + jnp.dot(km_lo, qt, preferred_element_type=F32))
        g = jnp.where(row < qi, gate, neg)
        sel = jnp.zeros((nb, blk), F32)
        for _ in range(MOBA_TOPK):
            mx = jnp.max(g, axis=0, keepdims=True)
            first = jnp.min(jnp.where(g == mx, rowf, float(nb)), axis=0, keepdims=True)
            pick = rowf == jnp.where(mx > neg, first, -1.0)
            sel = jnp.where(pick, 1.0, sel)
            g = jnp.where(pick, neg, g)
        mask_prev = jnp.where(sel > 0.0, jnp.where(row == qi - 1, 0.0, NEG_BIG), NEG_BIG)
        mask_far = jnp.where(sel > 0.0, jnp.where(row < qi - 1, 0.0, NEG_BIG), NEG_BIG)
        qa_ref[n] = jnp.concatenate([qt, mask_far.astype(BF16), pad_hi], axis=0)
        return (jnp.concatenate([qt, pad], axis=0),
                jnp.concatenate([qt, mask_prev.astype(BF16), pad_hi], axis=0))

    def own_prev(n):
        h, j = items[n]
        qi = pair * nq + j
        return h, qi, jnp.maximum(qi - 1, 0)

    def near_scores(n, qa):
        h, qi, prev_j = own_prev(n)
        qa_own, qa_prev = qa
        s_own = jnp.dot(k_ref[0, h, qi], qa_own, preferred_element_type=F32)
        s_prev = jnp.dot(k_ref[0, h, prev_j], qa_prev, preferred_element_type=F32)
        return s_own, s_prev

    def near_softmax(n, ss):
        h = items[n][0]
        s_own, s_prev = ss
        s = jnp.concatenate([s_own + bias_ref[h, :, blk:2 * blk],
                             s_prev + bias_ref[h, :, 0:blk]], axis=0)
        m0 = jnp.max(s, axis=0, keepdims=True)
        return m0, jnp.exp2(s - m0)

    def near_pv(n, mp):
        h, qi, prev_j = own_prev(n)
        m0, p = mp
        pb = p.astype(BF16)
        acc = (jnp.dot(vt_ref[0, h, qi], pb[0:blk], preferred_element_type=F32)
               + jnp.dot(vt_ref[0, h, prev_j], pb[blk:2 * blk], preferred_element_type=F32))
        return m0, acc

    near = skewed([select, near_scores, near_softmax, near_pv], MOBA_NEAR_LAGS)
    states = tuple(near[n] for n in range(len(items)))

    def far(gi, states, todo=None):
        j0 = gi * grp

        def qk(n, _):
            kt = k_ref[0, items[n][0], pl.ds(j0, grp)].reshape(grp * blk, LANES)
            return jnp.dot(kt, qa_ref[n], preferred_element_type=F32)

        def softmax(n, s):
            m_old = states[n][0]
            m_new = jnp.maximum(m_old, jnp.max(s, axis=0, keepdims=True))
            return m_new, jnp.exp2(m_old - m_new), jnp.exp2(s - m_new)

        def pv(n, sm):
            m_new, a, p = sm
            pb = p.astype(BF16)
            tot = a * states[n][1]
            for i in range(grp):
                tot = tot + jnp.dot(vt_ref[0, items[n][0], j0 + i], pb[i * blk:(i + 1) * blk],
                                    preferred_element_type=F32)
            return m_new, tot

        new = skewed([qk, softmax, pv], MOBA_FAR_LAGS, todo)
        return tuple(new.get(n, states[n]) for n in range(len(items)))

    sub = nq // grp
    states = lax.fori_loop(0, pair * sub, far, tuple(states))
    for extra in range(1, sub):
        later = [n for n, (_, j) in enumerate(items) if j // grp >= extra]
        states = far(pair * sub + extra - 1, states, later)

    for n, (h, j) in enumerate(items):
        acc = states[n][1]
        o_ref[0, h * hd:(h + 1) * hd, j * blk:(j + 1) * blk] = (
            acc[0:hd] / acc[hd:hd + 1]).astype(o_ref.dtype)


def _moba(qt, k_aug, vt, bias):
    bsz, nh, hd, s = qt.shape
    blk = MOBA_BLOCK
    hp = MOBA_HEADS_PER_STEP
    nq = MOBA_QBLOCKS_PER_STEP
    nblk = s // blk
    assert nh % hp == 0 and nblk % nq == 0 and nq % MOBA_FAR_GROUP == 0
    return pl.pallas_call(
        _moba_kernel,
        grid=(bsz, nh // hp, nblk // nq),
        in_specs=[pl.BlockSpec((1, hp, hd, nq * blk), lambda b, g, i: (b, g, 0, i)),
                  pl.BlockSpec((1, hp, nblk, blk, LANES), lambda b, g, i: (b, g, 0, 0, 0),
                               pipeline_mode=pl.Buffered(1)),
                  pl.BlockSpec((1, hp, nblk, MOBA_VT_ROWS, blk), lambda b, g, i: (b, g, 0, 0, 0),
                               pipeline_mode=pl.Buffered(1)),
                  pl.BlockSpec((hp, blk, 2 * blk), lambda b, g, i: (g, 0, 0),
                               pipeline_mode=pl.Buffered(1))],
        out_specs=pl.BlockSpec((1, hp * hd, nq * blk), lambda b, g, i: (b, g, i)),
        out_shape=jax.ShapeDtypeStruct((bsz, nh * hd, s), BF16),
        scratch_shapes=[pltpu.VMEM((hp, MOBA_MAX_BLOCKS, LANES), F32),
                        pltpu.VMEM((hp * nq, LANES, blk), BF16)],
        compiler_params=_cparams(("arbitrary", "arbitrary", "arbitrary")),
        name="moba",
    )(qt, k_aug, vt, bias)


def _gla_kernel(q_ref, k_ref, v_ref, gk_ref, wup_ref, bgk_ref, r_ref, gn_ref, o_ref,
                state_ref, *, nchunk):
    ch = GLA_CHUNK
    tc = nchunk * ch
    dk, dv = GLA_DK, GLA_DV

    @pl.when(pl.program_id(1) == 0)
    def _():
        state_ref[...] = jnp.zeros(state_ref.shape, F32)

    rin = lax.broadcasted_iota(I32, (tc, dk), 0) & (ch - 1)
    causal = (lax.broadcasted_iota(I32, (ch, ch), 1) <= lax.broadcasted_iota(I32, (ch, ch), 0))
    eye = (lax.broadcasted_iota(I32, (dk, dk), 0) == lax.broadcasted_iota(I32, (dk, dk), 1))
    chunks = [slice(n * ch, (n + 1) * ch) for n in range(nchunk)]
    gk = gk_ref[...]
    gk_hi = gk.astype(BF16)
    gk_lo = (gk - gk_hi.astype(F32)).astype(BF16)

    def prep(h, _):
        ks = slice(h * dk, (h + 1) * dk)
        w = wup_ref[:, ks]
        w_hi = w.astype(BF16)
        w_lo = (w - w_hi.astype(F32)).astype(BF16)
        z = (jnp.dot(gk_hi, w_hi, preferred_element_type=F32)
             + jnp.dot(gk_lo, w_hi, preferred_element_type=F32)
             + jnp.dot(gk_hi, w_lo, preferred_element_type=F32) + bgk_ref[:, ks])
        log_a = ((jnp.minimum(z, 0.0) - jnp.log(1.0 + jnp.exp(-jnp.abs(z))))
                 / GLA_GATE_NORMALIZER)
        b = log_a
        sh = 1
        while sh < ch:
            b = b + jnp.where(rin >= sh, pltpu.roll(b, sh, axis=0), 0.0)
            sh *= 2
        q = q_ref[:, ks].astype(F32) * (dk ** -0.5)
        k = k_ref[:, ks].astype(F32)
        q_g = (q * jnp.exp(b)).astype(BF16)
        k_g = (k * jnp.exp(-b)).astype(BF16)
        b3 = b.reshape(nchunk, ch, dk)
        b_last = b3[:, ch - 1:ch, :]
        k_end = (k * jnp.exp(jnp.broadcast_to(b_last, b3.shape) - b3).reshape(tc, dk)
                 ).astype(BF16)
        return q_g, k_g, k_end, jnp.exp(b_last)

    def local(h, pre):
        q_g, k_g, k_end, decay = pre
        o_intra, kv, decay_col = [], [], []
        for n, sl in enumerate(chunks):
            v_c = v_ref[sl, h * dv:(h + 1) * dv]
            att = jnp.where(causal, _nt_dot(q_g[sl], k_g[sl]), 0.0)
            o_intra.append(jnp.dot(att.astype(BF16), v_c, preferred_element_type=F32))
            kv.append(lax.dot_general(k_end[sl], v_c, (((0,), (0,)), ((), ())),
                                      preferred_element_type=F32))
            decay_col.append(jnp.sum(
                jnp.where(eye, jnp.broadcast_to(decay[n], (dk, dk)), 0.0),
                axis=1, keepdims=True))
        return q_g, o_intra, kv, decay_col

    def chain(h, loc):
        q_g, o_intra, kv, decay_col = loc
        state = state_ref[h]
        outs = []
        for n, sl in enumerate(chunks):
            outs.append(o_intra[n] + jnp.dot(q_g[sl], state.astype(BF16),
                                             preferred_element_type=F32))
            state = decay_col[n] * state + kv[n]
        state_ref[h] = state
        return jnp.concatenate(outs, axis=0)

    def finish(h, o):
        vs = slice(h * dv, (h + 1) * dv)
        o_ref[:, vs] = (_rms(o) * gn_ref[...] * r_ref[:, vs].astype(F32)).astype(o_ref.dtype)
        return None

    stages = [prep, local, chain, finish]
    vals = [None] * GLA_HEADS
    for step in range(GLA_HEADS + len(stages) - 1):
        for si, stage in enumerate(stages):
            h = step - si
            if 0 <= h < GLA_HEADS:
                vals[h] = stage(h, vals[h])


def _gla(qkb, vb, gk, wup, bgk, r_act, gn, bsz, seq, nchunk):
    t = qkb.shape[0]
    tc = nchunk * GLA_CHUNK
    nc = seq // tc
    rowblk = lambda w, off: pl.BlockSpec((tc, w), lambda b, c: (b * nc + c, off))
    full = lambda a: pl.BlockSpec(a.shape, lambda b, c: (0, 0))
    return pl.pallas_call(
        functools.partial(_gla_kernel, nchunk=nchunk),
        grid=(bsz, nc),
        in_specs=[rowblk(GLA_KEY_DIM, 0), rowblk(GLA_KEY_DIM, 1), rowblk(GLA_VALUE_DIM, 0),
                  rowblk(LANES, 0), full(wup), full(bgk), rowblk(GLA_VALUE_DIM, 0), full(gn)],
        out_specs=rowblk(GLA_VALUE_DIM, 0),
        out_shape=jax.ShapeDtypeStruct((t, GLA_VALUE_DIM), BF16),
        scratch_shapes=[pltpu.VMEM((GLA_HEADS, GLA_DK, GLA_DV), F32)],
        compiler_params=_cparams(("arbitrary", "arbitrary")),
        name="gla",
    )(qkb, qkb, vb, gk, wup, bgk, r_act, gn)


def _merge_kernel(ya_ref, yb_ref, g_ref, x_ref, wpa_ref, wpb_ref, wout_ref, g1_ref,
                  nw_ref, sc_ref, sh_ref, wr_ref, br_ref,
                  x1_ref, h2_ref, idx_ref, wts_ref):
    pa = lax.dot_general(ya_ref[0], wpa_ref[...], (((0,), (0,)), ((), ())),
                         preferred_element_type=F32)
    pb = jnp.dot(yb_ref[...], wpb_ref[...], preferred_element_type=F32)
    mixed = (g_ref[:, 0:D_MODEL].astype(F32) * pa
             + g_ref[:, D_MODEL:2 * D_MODEL].astype(F32) * pb)
    y = jnp.dot(mixed.astype(BF16), wout_ref[...], preferred_element_type=F32)
    x1 = x_ref[...] + g1_ref[0] * y
    x1_ref[...] = x1
    h2 = _rms(x1) * nw_ref[...]
    h2 = h2 * (1.0 + sc_ref[0]) + sh_ref[0]
    _store_rows(h2_ref, _pack_halves(h2))
    h_hi = h2.astype(BF16)
    h_lo = (h2 - h_hi.astype(F32)).astype(BF16)
    w = wr_ref[...]
    w_hi = w.astype(BF16)
    w_lo = (w - w_hi.astype(F32)).astype(BF16)
    logits = (_nt_dot(w_hi, h_hi) + _nt_dot(w_hi, h_lo) + _nt_dot(w_lo, h_hi)
              + br_ref[...])
    rowf = lax.broadcasted_iota(I32, logits.shape, 0).astype(F32)
    vals, idxs = [], []
    cur = logits
    for _ in range(TOP_K):
        mx = jnp.max(cur, axis=0, keepdims=True)
        first = jnp.min(jnp.where(cur == mx, rowf, float(N_EXPERTS)), axis=0, keepdims=True)
        vals.append(mx)
        idxs.append(first)
        cur = jnp.where(rowf == first, -jnp.inf, cur)
    es = [jnp.exp(v - vals[0]) for v in vals]
    tot = es[0]
    for e in es[1:]:
        tot = tot + e
    idx_ref[...] = jnp.concatenate(idxs, axis=0).astype(I32)
    wts_ref[...] = jnp.concatenate([e / tot for e in es], axis=0)


def _merge(ya, yb, gates, x2d, wpa, wpb, wout, g1, nw, sc, sh, wr_t, br, seq, tm):
    t, d = x2d.shape
    tpb = seq // tm
    row = lambda w: pl.BlockSpec((tm, w), lambda i: (i, 0))
    full = lambda a: pl.BlockSpec(a.shape, lambda i: (0,) * a.ndim)
    per_b = pl.BlockSpec((1, 1, d), lambda i: (i // tpb, 0, 0))
    colblk = pl.BlockSpec((TOP_K, tm), lambda i: (0, i))
    return pl.pallas_call(
        _merge_kernel,
        grid=(t // tm,),
        in_specs=[pl.BlockSpec((1, MOBA_WIDTH, tm), lambda i: (i // tpb, 0, i % tpb)),
                  row(GLA_VALUE_DIM), row(2 * D_MODEL), row(d),
                  full(wpa), full(wpb), full(wout), per_b,
                  pl.BlockSpec((1, d), lambda i: (0, 0)), per_b, per_b,
                  full(wr_t), full(br)],
        out_specs=[row(d), pl.BlockSpec((tm * ROW_SLABS, LANES), lambda i: (i, 0)),
                   colblk, colblk],
        out_shape=[jax.ShapeDtypeStruct((t, d), F32),
                   jax.ShapeDtypeStruct((t * ROW_SLABS, LANES), U32),
                   jax.ShapeDtypeStruct((TOP_K, t), I32), jax.ShapeDtypeStruct((TOP_K, t), F32)],
        compiler_params=_cparams(("arbitrary",)),
        name="merge",
    )(ya, yb, gates, x2d, wpa, wpb, wout, g1, nw.reshape(1, d), sc, sh, wr_t, br)


def _rank_kernel(idx_ref, rank_ref, cnt_ref, carry_ref):
    tm = idx_ref.shape[1]

    @pl.when(pl.program_id(0) == 0)
    def _():
        carry_ref[...] = jnp.zeros(carry_ref.shape, F32)

    rows = lax.broadcasted_iota(I32, (N_EXPERTS, tm), 0)
    before = (lax.broadcasted_iota(I32, (tm, tm), 0)
              < lax.broadcasted_iota(I32, (tm, tm), 1))
    upper = jnp.where(before, 1.0, 0.0).astype(BF16)
    carry = carry_ref[:, 0:1]
    ranks = []
    for k in range(TOP_K):
        onehot = idx_ref[k:k + 1, :] == rows
        onef = jnp.where(onehot, 1.0, 0.0)
        earlier = jnp.dot(onef.astype(BF16), upper, preferred_element_type=F32) + carry
        ranks.append(jnp.sum(jnp.where(onehot, earlier, 0.0), axis=0, keepdims=True))
        carry = carry + jnp.sum(onef, axis=1, keepdims=True)
    rank_ref[...] = jnp.concatenate(ranks, axis=0).astype(I32)
    total = jnp.broadcast_to(carry, carry_ref.shape)
    carry_ref[...] = total
    cnt_ref[...] = total


def _rank(idx, tm):
    t = idx.shape[1]
    return pl.pallas_call(
        _rank_kernel,
        grid=(t // tm,),
        in_specs=[pl.BlockSpec((TOP_K, tm), lambda i: (0, i))],
        out_specs=[pl.BlockSpec((TOP_K, tm), lambda i: (0, i)),
                   pl.BlockSpec((N_EXPERTS, LANES), lambda i: (0, 0))],
        out_shape=[jax.ShapeDtypeStruct((TOP_K, t), I32),
                   jax.ShapeDtypeStruct((N_EXPERTS, LANES), F32)],
        scratch_shapes=[pltpu.VMEM((N_EXPERTS, LANES), F32)],
        compiler_params=_cparams(("arbitrary",)),
        name="rank",
    )(idx)


def _dest_kernel(pstart_ref, idx_ref, rank_ref, dest_ref):
    idx = idx_ref[...]
    off = jnp.zeros(idx.shape, I32)
    for e in range(N_EXPERTS):
        off = jnp.where(idx == e, pstart_ref[e], off)
    dest_ref[...] = rank_ref[...] + off


def _dest(pstart, idx, rank, tm):
    t = idx.shape[1]
    blk = pl.BlockSpec((TOP_K, tm), lambda i: (0, i))
    return pl.pallas_call(
        _dest_kernel,
        grid=(t // tm,),
        in_specs=[pl.BlockSpec(memory_space=pltpu.SMEM), blk, blk],
        out_specs=blk,
        out_shape=jax.ShapeDtypeStruct((TOP_K, t), I32),
        compiler_params=_cparams(("arbitrary",)),
        name="dest",
    )(pstart, idx, rank)


def _scatter_kernel(dest_ref, zrow_ref, h_ref, xout_hbm, zbuf, stage, sems, zsem, *, tm):
    @pl.when(pl.program_id(0) == 0)
    def _():
        zbuf[...] = jnp.zeros(zbuf.shape, zbuf.dtype)

        def zero_rows(row0):
            line0 = pl.multiple_of(row0 * ROW_SLABS, MOE_ROWS * ROW_SLABS)
            return pltpu.make_async_copy(
                zbuf, xout_hbm.at[pl.ds(line0, MOE_ROWS * ROW_SLABS)], zsem)

        for e in range(N_EXPERTS):
            zero_rows(zrow_ref[e]).start()
        for e in range(N_EXPERTS):
            zero_rows(zrow_ref[e]).wait()

        def start_tail(j, carry):
            zero_rows(j * MOE_ROWS).start()
            return carry

        def wait_tail(j, carry):
            zero_rows(j * MOE_ROWS).wait()
            return carry

        n_all = xout_hbm.shape[0] // (MOE_ROWS * ROW_SLABS)
        lax.fori_loop(zrow_ref[N_EXPERTS], n_all, start_tail, 0)
        lax.fori_loop(zrow_ref[N_EXPERTS], n_all, wait_tail, 0)

    i = pl.program_id(0)
    slot = i % 2
    src = stage.at[slot]
    src[...] = h_ref[...]

    def issue(g, carry):
        base = pl.multiple_of(g * ROW_UNROLL, ROW_UNROLL)
        for j in range(ROW_UNROLL):
            for k in range(TOP_K):
                d = dest_ref[(base + j) * TOP_K + k]
                pltpu.make_async_copy(_row(src, base + j), _row(xout_hbm, d),
                                      sems.at[slot]).start(priority=k % 2)
        return carry

    lax.fori_loop(0, tm // ROW_UNROLL, issue, 0)

    def wait_tile(s):
        for k in range(TOP_K):
            pltpu.make_async_copy(stage.at[s], xout_hbm.at[pl.ds(0, tm * ROW_SLABS)],
                                  sems.at[s]).wait()

    @pl.when(i > 0)
    def _():
        wait_tile(1 - slot)

    @pl.when(i == pl.num_programs(0) - 1)
    def _():
        wait_tile(slot)


def _scatter(dest_tok, last_block_row, h2, n_pad, tm):
    t = h2.shape[0] // ROW_SLABS
    return pl.pallas_call(
        functools.partial(_scatter_kernel, tm=tm),
        grid=(t // tm,),
        in_specs=[pl.BlockSpec((tm * TOP_K,), lambda i: (i,), memory_space=pltpu.SMEM),
                  pl.BlockSpec(memory_space=pltpu.SMEM),
                  pl.BlockSpec((tm * ROW_SLABS, LANES), lambda i: (i, 0))],
        out_specs=pl.BlockSpec(memory_space=pl.ANY),
        out_shape=jax.ShapeDtypeStruct((n_pad * ROW_SLABS, LANES), h2.dtype),
        scratch_shapes=[pltpu.VMEM((MOE_ROWS * ROW_SLABS, LANES), h2.dtype),
                        pltpu.VMEM((2, tm * ROW_SLABS, LANES), h2.dtype),
                        pltpu.SemaphoreType.DMA((2,)), pltpu.SemaphoreType.DMA],
        compiler_params=_cparams(("arbitrary",)),
        name="scatter",
    )(dest_tok, last_block_row, h2)


def _expert_kernel(be_ref, nused_ref, x_ref, wg_ref, bg_ref, wu_ref, bu_ref, wd_ref, bd_ref,
                   o_ref, wgb_ref, wub_ref, wdb_ref):
    i = pl.program_id(0)
    prev = be_ref[jnp.maximum(i - 1, 0)]

    @pl.when(jnp.logical_or(i == 0, be_ref[i] != prev))
    def _():
        wgb_ref[...] = wg_ref[0].astype(BF16)
        wub_ref[...] = wu_ref[0].astype(BF16)
        wdb_ref[...] = wd_ref[0].astype(BF16)

    @pl.when(i < nused_ref[0])
    def _():
        xb = jnp.concatenate([v.astype(BF16) for v in _unpack_halves(_load_rows(x_ref))],
                             axis=1)
        g = jnp.minimum(jnp.dot(xb, wgb_ref[...], preferred_element_type=F32) + bg_ref[0],
                        SWIGLU_LIMIT)
        u = jnp.clip(jnp.dot(xb, wub_ref[...], preferred_element_type=F32) + bu_ref[0],
                     -SWIGLU_LIMIT, SWIGLU_LIMIT)
        act = g * _sigmoid(SWIGLU_ALPHA * g) * (u + 1.0)
        _store_rows(o_ref, _pack_halves(
            jnp.dot(act.astype(BF16), wdb_ref[...], preferred_element_type=F32) + bd_ref[0]))

    @pl.when(i >= nused_ref[0])
    def _():
        o_ref[...] = jnp.zeros(o_ref.shape, o_ref.dtype)


def _expert(block_expert, n_used, x_pad, wg, bg, wu, bu, wd, bd):
    lines = MOE_ROWS * ROW_SLABS
    _, d, f = wg.shape
    nblk = x_pad.shape[0] // lines
    wspec = lambda a, b: pl.BlockSpec((1, a, b), lambda i, be, nu: (be[i], 0, 0))
    grid_spec = pltpu.PrefetchScalarGridSpec(
        num_scalar_prefetch=2,
        grid=(nblk,),
        in_specs=[pl.BlockSpec((lines, LANES),
                               lambda i, be, nu: (jnp.minimum(i, nu[0] - 1), 0)),
                  wspec(d, f), wspec(1, f), wspec(d, f), wspec(1, f), wspec(f, d), wspec(1, d)],
        out_specs=pl.BlockSpec((lines, LANES), lambda i, be, nu: (i, 0)),
        scratch_shapes=[pltpu.VMEM((d, f), BF16), pltpu.VMEM((d, f), BF16),
                        pltpu.VMEM((f, d), BF16)])
    return pl.pallas_call(
        _expert_kernel,
        grid_spec=grid_spec,
        out_shape=jax.ShapeDtypeStruct(x_pad.shape, U32),
        compiler_params=_cparams(("arbitrary",)),
        name="expert",
    )(block_expert, n_used, x_pad, wg, bg.reshape(N_EXPERTS, 1, f), wu,
      bu.reshape(N_EXPERTS, 1, f), wd, bd.reshape(N_EXPERTS, 1, d))


def _combine_kernel(dest_ref, y_hbm, x1_ref, wts_ref, g2_ref, nf_ref, o_ref, ybuf, sems, *,
                    tm, final):
    i = pl.program_id(0)
    n_tiles = pl.num_programs(0) - 1
    slot = i % 2

    @pl.when(i < n_tiles)
    def _():
        def row_copy(tt, k):
            d = dest_ref[tt * TOP_K + k]
            return pltpu.make_async_copy(_row(y_hbm, d), _row(ybuf.at[slot, k], tt),
                                         sems.at[slot])

        def issue(g, carry):
            base = pl.multiple_of(g * ROW_UNROLL, ROW_UNROLL)
            for j in range(ROW_UNROLL):
                for k in range(TOP_K):
                    row_copy(base + j, k).start(priority=k % 2)
            return carry

        lax.fori_loop(0, tm // ROW_UNROLL, issue, 0)

    @pl.when(i > 0)
    def _():
        done = 1 - slot
        for k in range(TOP_K):
            pltpu.make_async_copy(y_hbm.at[pl.ds(0, tm * ROW_SLABS)], ybuf.at[done, k],
                                  sems.at[done]).wait()
        w = wts_ref[...]
        lo, hi = None, None
        for k in range(TOP_K):
            y_lo, y_hi = _unpack_halves(_load_rows(ybuf.at[done, k]))
            lo = w[:, k:k + 1] * y_lo + (0.0 if lo is None else lo)
            hi = w[:, k:k + 1] * y_hi + (0.0 if hi is None else hi)
        moe = jnp.concatenate([lo, hi], axis=1)
        x2 = x1_ref[...] + g2_ref[0] * moe
        o_ref[...] = _rms(x2) * nf_ref[...] if final else x2


def _combine(dest_tok, y_pad, x1, wts_tok, g2, nf, seq, tm, final):
    t, d = x1.shape
    tpb = seq // tm
    n_tiles = t // tm
    lag = lambda i: jnp.maximum(i - 1, 0)
    return pl.pallas_call(
        functools.partial(_combine_kernel, tm=tm, final=final),
        grid=(n_tiles + 1,),
        in_specs=[pl.BlockSpec((tm * TOP_K,), lambda i: (jnp.minimum(i, n_tiles - 1),),
                               memory_space=pltpu.SMEM),
                  pl.BlockSpec(memory_space=pl.ANY),
                  pl.BlockSpec((tm, d), lambda i: (lag(i), 0)),
                  pl.BlockSpec((tm, TOP_K), lambda i: (lag(i), 0)),
                  pl.BlockSpec((1, 1, d), lambda i: (lag(i) // tpb, 0, 0)),
                  pl.BlockSpec((1, d), lambda i: (0, 0))],
        out_specs=pl.BlockSpec((tm, d), lambda i: (lag(i), 0)),
        out_shape=jax.ShapeDtypeStruct((t, d), F32),
        scratch_shapes=[pltpu.VMEM((2, TOP_K, tm * ROW_SLABS, LANES), y_pad.dtype),
                        pltpu.SemaphoreType.DMA((2,))],
        compiler_params=_cparams(("arbitrary",)),
        name="combine",
    )(dest_tok, y_pad, x1, wts_tok, g2, nf.reshape(1, d))


def _pick(n, cands):
    for c in cands:
        if n % c == 0:
            return c
    raise ValueError(f"no tile in {cands} divides {n}")


def kernel(x, c, rel_bias, w_ada, b_ada, norm_mix, w_in, w_gk_up, b_gk, gla_norm,
           w_proj_moba, w_proj_gla, w_out, norm_ffn, w_router, b_router,
           w_gate, b_gate, w_up, b_up, w_down, b_down, norm_final):
    bsz, seq, d = x.shape
    depth = w_ada.shape[0]
    assert d == D_MODEL and seq % MOBA_BLOCK == 0 and seq // MOBA_BLOCK <= MOBA_MAX_BLOCKS
    t = bsz * seq
    tm = _pick(seq, (512, 256))
    nchunk = _pick(seq // GLA_CHUNK, (8, 4))
    n_blk = seq // MOBA_BLOCK
    x2d = x.reshape(t, d)
    bias = _bias_tiles(rel_bias)
    per_b = lambda v: v.reshape(bsz, 1, d)

    for l in range(depth):
        mod = _ada(c, w_ada[l], b_ada[l])
        sh1, sc1, g1, sh2, sc2, g2 = [per_b(m) for m in jnp.split(mod, 6, axis=-1)]
        qt, k_aug, vt, qkb, vb, gk, r_act, gates = _inproj(
            x2d, norm_mix[l], sc1, sh1, _regroup_w_in(w_in[l]), bsz, seq, tm)
        ya = _moba(qt, k_aug, vt, bias)
        wup = jnp.pad(w_gk_up[l], ((0, LANES - GLA_GATE_RANK), (0, 0)))
        yb = _gla(qkb, vb, gk, wup, b_gk[l].reshape(1, -1), r_act,
                  gla_norm[l].reshape(1, -1), bsz, seq, nchunk)
        x1, h2, idx, wts = _merge(
            ya, yb, gates, x2d, w_proj_moba[l].astype(BF16), w_proj_gla[l].astype(BF16),
            w_out[l].astype(BF16), g1, norm_ffn[l], sc2, sh2,
            w_router[l].T, b_router[l].reshape(N_EXPERTS, 1), seq, tm)
        rank, cnt = _rank(idx, tm)
        counts = cnt[:, 0].astype(I32)
        padded = (counts + MOE_ROWS - 1) // MOE_ROWS * MOE_ROWS
        pcum = jnp.cumsum(padded)
        pstart = (pcum - padded).astype(I32)
        n_blocks = (t * TOP_K + MOE_ROWS - 1) // MOE_ROWS + N_EXPERTS
        block_row0 = jnp.arange(n_blocks, dtype=I32) * MOE_ROWS
        block_expert = jnp.minimum(
            jnp.sum((pcum[None, :] <= block_row0[:, None]).astype(I32), axis=1),
            N_EXPERTS - 1).astype(I32)
        n_used = (pcum[-1:] // MOE_ROWS).astype(I32)
        dest = _dest(pstart, idx, rank, tm)
        dest_tok = dest.T.reshape(t * TOP_K)
        tg = _pick(t, (256,))
        zero_info = jnp.concatenate([jnp.maximum(pcum - MOE_ROWS, 0).astype(I32), n_used])
        x_pad = _scatter(dest_tok, zero_info, h2, n_blocks * MOE_ROWS, tg)
        y_pad = _expert(block_expert, n_used, x_pad, w_gate[l], b_gate[l], w_up[l], b_up[l],
                        w_down[l], b_down[l])
        x2d = _combine(dest_tok, y_pad, x1, wts.T, g2, norm_final, seq, tg, l == depth - 1)
    return x2d.reshape(bsz, seq, d)
```

```python
import functools
import math

import numpy as np
import jax
import jax.numpy as jnp
from jax import lax
from jax.experimental import pallas as pl
from jax.experimental.pallas import tpu as pltpu

F32 = jnp.float32
BF16 = jnp.bfloat16
I32 = jnp.int32
HIGHEST = lax.Precision.HIGHEST

D_MODEL = 1024
MOBA_HEADS = 8
MOBA_HEAD_DIM = 64
MOBA_WIDTH = MOBA_HEADS * MOBA_HEAD_DIM
MOBA_BLOCK = 256
MOBA_TOPK = 3
MOBA_MAX_BLOCKS = 32
REL_BUCKETS = 32
REL_MAX_DIST = 128
GLA_HEADS = 4
GLA_KEY_DIM = D_MODEL // 2
GLA_VALUE_DIM = D_MODEL
GLA_DK = GLA_KEY_DIM // GLA_HEADS
GLA_DV = GLA_VALUE_DIM // GLA_HEADS
GLA_GATE_RANK = 16
GLA_GATE_NORMALIZER = 16.0
GLA_CHUNK = 64
N_EXPERTS = 32
TOP_K = 4
D_FF = D_MODEL
SWIGLU_ALPHA = 1.702
SWIGLU_LIMIT = 7.0
MOE_ROWS = 512
ROW_UNROLL = 8
EPS = 1e-6
LANES = 128
NEG_BIG = -1e30
LOG2E = math.log2(math.e)
VMEM_LIMIT = 56 * 1024 * 1024


def _cparams(sem, vmem=None):
    return pltpu.CompilerParams(dimension_semantics=sem,
                                vmem_limit_bytes=vmem or VMEM_LIMIT)


def _nt_dot(a, b, **kw):
    return lax.dot_general(a, b, (((1,), (1,)), ((), ())),
                           preferred_element_type=F32, **kw)


def _rms(x):
    return x * lax.rsqrt(jnp.mean(x * x, axis=-1, keepdims=True) + EPS)


def _sigmoid(x):
    return 1.0 / (1.0 + jnp.exp(-x))


U32 = jnp.uint32
_HI16 = 0xFFFF0000


def _pack_halves(x):
    n = x.shape[1] // 2
    lo = pltpu.bitcast(x[:, :n].astype(BF16).astype(F32), U32)
    hi = pltpu.bitcast(x[:, n:].astype(BF16).astype(F32), U32)
    return (hi & U32(_HI16)) | (lo >> 16)


def _unpack_halves(w):
    return (pltpu.bitcast(w << 16, F32), pltpu.bitcast(w & U32(_HI16), F32))


ROW_SLABS = D_MODEL // 2 // LANES


def _store_rows(ref, words):
    m = words.shape[0]
    for c in range(ROW_SLABS):
        ref[pl.ds(c, m, stride=ROW_SLABS), :] = words[:, c * LANES:(c + 1) * LANES]


def _load_rows(ref):
    m = ref.shape[0] // ROW_SLABS
    return jnp.concatenate(
        [ref[pl.ds(c, m, stride=ROW_SLABS), :] for c in range(ROW_SLABS)], axis=1)


def _row(ref, i):
    return ref.at[pl.ds(pl.multiple_of(i * ROW_SLABS, ROW_SLABS), ROW_SLABS)]


def _ada_kernel(c_ref, w_ref, b_ref, o_ref):
    c = c_ref[...]
    s = c * _sigmoid(c)
    o_ref[...] = jnp.dot(s, w_ref[...], precision=HIGHEST,
                         preferred_element_type=F32) + b_ref[...]


def _ada(c, w, b):
    bsz, d = c.shape
    n = w.shape[1]
    rows = -(-bsz // 8) * 8
    cp = jnp.zeros((rows, d), F32).at[:bsz].set(c)
    tn = 768
    out = pl.pallas_call(
        _ada_kernel,
        grid=(n // tn,),
        in_specs=[pl.BlockSpec((rows, d), lambda j: (0, 0)),
                  pl.BlockSpec((d, tn), lambda j: (0, j)),
                  pl.BlockSpec((1, tn), lambda j: (0, j))],
        out_specs=pl.BlockSpec((rows, tn), lambda j: (0, j)),
        out_shape=jax.ShapeDtypeStruct((rows, n), F32),
        compiler_params=_cparams(("arbitrary",)),
        name="ada",
    )(cp, w, b.reshape(1, n))
    return out[:bsz]


_OFF_QA = 0
_OFF_KA = _OFF_QA + MOBA_WIDTH
_OFF_VA = _OFF_KA + MOBA_WIDTH
_OFF_QKB = _OFF_VA + MOBA_WIDTH
_OFF_VB = _OFF_QKB + 2 * GLA_KEY_DIM
_OFF_GK = _OFF_VB + GLA_VALUE_DIM
_OFF_R = _OFF_GK + LANES
_OFF_G = _OFF_R + GLA_VALUE_DIM
_W_CAT = _OFF_G + 2 * D_MODEL


def _regroup_w_in(w):
    o_gk = 3 * MOBA_WIDTH + 2 * GLA_KEY_DIM + GLA_VALUE_DIM
    gk = jnp.pad(w[:, o_gk:o_gk + GLA_GATE_RANK], ((0, 0), (0, LANES - GLA_GATE_RANK)))
    return jnp.concatenate([w[:, :o_gk], gk, w[:, o_gk + GLA_GATE_RANK:]],
                           axis=1).astype(BF16)


def _inproj_kernel(x_ref, nw_ref, sc_ref, sh_ref, w_ref,
                   qt_ref, ka_ref, vt_ref, qkb_ref, vb_ref, gk_ref, r_ref, g_ref, *, tpb):
    tm = x_ref.shape[0]
    hd = MOBA_HEAD_DIM
    nbt = tm // MOBA_BLOCK
    h = _rms(x_ref[...]) * nw_ref[...]
    h = h * (1.0 + sc_ref[0]) + sh_ref[0]
    hb = h.astype(BF16)

    def mm(a, b):
        return jnp.dot(hb, w_ref[:, a:b], preferred_element_type=F32)

    q_t = (mm(_OFF_QA, _OFF_KA) * (hd ** -0.5 * LOG2E)).T
    v_t = mm(_OFF_VA, _OFF_QKB).T
    k_all = mm(_OFF_KA, _OFF_VA)
    blk0 = (pl.program_id(0) % tpb) * nbt
    lane = lax.broadcasted_iota(I32, (MOBA_BLOCK, LANES), 1)
    ones_rows = jnp.where(
        lax.broadcasted_iota(I32, (MOBA_VT_ROWS - hd, MOBA_BLOCK), 0) == 0, 1.0, 0.0)
    heads_per_tile = LANES // hd
    for hh in range(MOBA_HEADS):
        qt_ref[0, hh] = q_t[hh * hd:(hh + 1) * hd].astype(BF16)
        tile = hh // heads_per_tile
        k_h = k_all[:, tile * LANES:(tile + 1) * LANES]
        if hh % heads_per_tile:
            k_h = pltpu.roll(k_h, LANES - (hh % heads_per_tile) * hd, axis=1)
        for j in range(nbt):
            rows = slice(j * MOBA_BLOCK, (j + 1) * MOBA_BLOCK)
            onehot = jnp.where(lane == hd + blk0 + j, 1.0, 0.0)
            ka_ref[0, hh, j] = jnp.where(lane < hd, k_h[rows], onehot).astype(BF16)
            vt_ref[0, hh, j, 0:hd, :] = v_t[hh * hd:(hh + 1) * hd, rows].astype(BF16)
            vt_ref[0, hh, j, hd:MOBA_VT_ROWS, :] = ones_rows.astype(BF16)
    qkb_ref[...] = mm(_OFF_QKB, _OFF_VB).astype(BF16)
    vb_ref[...] = mm(_OFF_VB, _OFF_GK).astype(BF16)
    gk_ref[...] = mm(_OFF_GK, _OFF_R)
    r = mm(_OFF_R, _OFF_G)
    r_ref[...] = (r * _sigmoid(r)).astype(BF16)
    g_ref[...] = _sigmoid(mm(_OFF_G, _W_CAT)).astype(BF16)


def _inproj(x2d, nw, sc, sh, w_cat, bsz, seq, tm):
    t, d = x2d.shape
    tpb = seq // tm
    nbt = tm // MOBA_BLOCK
    nh, hd = MOBA_HEADS, MOBA_HEAD_DIM
    row = lambda w: pl.BlockSpec((tm, w), lambda i: (i, 0))
    per_b = pl.BlockSpec((1, 1, d), lambda i: (i // tpb, 0, 0))
    rows_out = [(2 * GLA_KEY_DIM, BF16), (GLA_VALUE_DIM, BF16), (LANES, F32),
                (GLA_VALUE_DIM, BF16), (2 * D_MODEL, BF16)]
    return pl.pallas_call(
        functools.partial(_inproj_kernel, tpb=tpb),
        grid=(t // tm,),
        in_specs=[row(d), pl.BlockSpec((1, d), lambda i: (0, 0)), per_b, per_b,
                  pl.BlockSpec((d, _W_CAT), lambda i: (0, 0), pipeline_mode=pl.Buffered(1))],
        out_specs=[pl.BlockSpec((1, nh, hd, tm), lambda i: (i // tpb, 0, 0, i % tpb)),
                   pl.BlockSpec((1, nh, nbt, MOBA_BLOCK, LANES),
                                lambda i: (i // tpb, 0, i % tpb, 0, 0)),
                   pl.BlockSpec((1, nh, nbt, MOBA_VT_ROWS, MOBA_BLOCK),
                                lambda i: (i // tpb, 0, i % tpb, 0, 0))]
                  + [row(w) for w, _ in rows_out],
        out_shape=[jax.ShapeDtypeStruct((bsz, nh, hd, seq), BF16),
                   jax.ShapeDtypeStruct((bsz, nh, seq // MOBA_BLOCK, MOBA_BLOCK, LANES), BF16),
                   jax.ShapeDtypeStruct((bsz, nh, seq // MOBA_BLOCK, MOBA_VT_ROWS, MOBA_BLOCK),
                                        BF16)]
                  + [jax.ShapeDtypeStruct((t, w), dt) for w, dt in rows_out],
        compiler_params=_cparams(("arbitrary",)),
        name="inproj",
    )(x2d, nw.reshape(1, d), sc, sh, w_cat)


def _t5_bucket_np(n):
    n = np.maximum(n, 0)
    max_exact = REL_BUCKETS // 2
    nf = np.maximum(n, max_exact).astype(np.float32)
    large = max_exact + (np.log(nf / max_exact) / math.log(REL_MAX_DIST / max_exact)
                         * (REL_BUCKETS - max_exact)).astype(np.int32)
    large = np.minimum(large, REL_BUCKETS - 1)
    return np.where(n < max_exact, n, large).astype(np.int32)


def _bucket_table():
    kj = np.arange(MOBA_BLOCK)[:, None]
    qi = np.arange(2 * MOBA_BLOCK)[None, :] % MOBA_BLOCK
    prev = np.arange(2 * MOBA_BLOCK)[None, :] < MOBA_BLOCK
    bucket = _t5_bucket_np(qi - kj + np.where(prev, MOBA_BLOCK, 0))
    return np.where(prev | (kj <= qi), bucket, -1).astype(np.int32)


def _bias_kernel(rb_ref, bucket_ref, o_ref):
    h = pl.program_id(0)
    bk = bucket_ref[...]
    far = rb_ref[(REL_BUCKETS - 1) * MOBA_HEADS + h]
    acc = jnp.zeros(bk.shape, F32)
    for b in range(REL_BUCKETS):
        acc = jnp.where(bk == b, rb_ref[b * MOBA_HEADS + h] - far, acc)
    o_ref[0] = jnp.where(bk < 0, NEG_BIG, acc * LOG2E)


def _bias_tiles(rel_bias):
    bucket = jnp.asarray(_bucket_table())
    return pl.pallas_call(
        _bias_kernel,
        grid=(MOBA_HEADS,),
        in_specs=[pl.BlockSpec(memory_space=pltpu.SMEM),
                  pl.BlockSpec(bucket.shape, lambda h: (0, 0))],
        out_specs=pl.BlockSpec((1,) + bucket.shape, lambda h: (h, 0, 0)),
        out_shape=jax.ShapeDtypeStruct((MOBA_HEADS,) + bucket.shape, F32),
        compiler_params=_cparams(("arbitrary",)),
        name="bias",
    )(rel_bias.reshape(-1), bucket)


MOBA_HEADS_PER_STEP = 8
MOBA_FAR_GROUP = 2
MOBA_QBLOCKS_PER_STEP = 4
MOBA_FAR_LAGS = (0, 3, 6)
MOBA_NEAR_LAGS = (0, 3, 6, 9)
MOBA_VT_ROWS = MOBA_HEAD_DIM + 16


def _moba_kernel(qt_ref, k_ref, vt_ref, bias_ref, o_ref, kmean_ref, qa_ref):
    blk = MOBA_BLOCK
    nb = MOBA_MAX_BLOCKS
    hd = MOBA_HEAD_DIM
    hp = MOBA_HEADS_PER_STEP
    grp = MOBA_FAR_GROUP
    nq = MOBA_QBLOCKS_PER_STEP
    pair = pl.program_id(2)
    neg = -jnp.inf
    items = [(h, j) for j in range(nq) for h in range(hp)]

    @pl.when(pair == 0)
    def _():
        kmean_ref[...] = jnp.zeros(kmean_ref.shape, F32)

    for h in range(hp):
        for j in range(nq):
            kmean_ref[h, pl.ds(pair * nq + j, 1), :] = jnp.mean(
                k_ref[0, h, pair * nq + j].astype(F32), axis=0, keepdims=True)

    row = lax.broadcasted_iota(I32, (nb, blk), 0)
    rowf = row.astype(F32)
    pad = jnp.zeros((LANES - hd, blk), BF16)
    pad_hi = jnp.zeros((LANES - hd - nb, blk), BF16)

    def skewed(stages, lags, todo=None):
        todo = list(range(len(items))) if todo is None else todo
        vals = {}
        for step in range(len(todo) + lags[-1]):
            for stage, lag in zip(stages, lags):
                pos = step - lag
                if 0 <= pos < len(todo):
                    vals[todo[pos]] = stage(todo[pos], vals.get(todo[pos]))
        return vals

    def select(n, _):
        h, j = items[n]
        qi = pair * nq + j
        qt = qt_ref[0, h, :, j * blk:(j + 1) * blk]
        km = kmean_ref[h, :, 0:hd]
        km_hi = km.astype(BF16)
        km_lo = (km - km_hi.astype(F32)).astype(BF16)
        gate = (jnp.dot(km_hi, qt, preferred_element_type=F32)
                + jnp.dot(km_lo, qt, preferred_element_type=F32))
        g = jnp.where(row < qi, gate, neg)
        sel = jnp.zeros((nb, blk), F32)
        for _ in range(MOBA_TOPK):
            mx = jnp.max(g, axis=0, keepdims=True)
            first = jnp.min(jnp.where(g == mx, rowf, float(nb)), axis=0, keepdims=True)
            pick = rowf == jnp.where(mx > neg, first, -1.0)
            sel = jnp.where(pick, 1.0, sel)
            g = jnp.where(pick, neg, g)
        mask_prev = jnp.where(sel > 0.0, jnp.where(row == qi - 1, 0.0, NEG_BIG), NEG_BIG)
        mask_far = jnp.where(sel > 0.0, jnp.where(row < qi - 1, 0.0, NEG_BIG), NEG_BIG)
        qa_ref[n] = jnp.concatenate([qt, mask_far.astype(BF16), pad_hi], axis=0)
        return (jnp.concatenate([qt, pad], axis=0),
                jnp.concatenate([qt, mask_prev.astype(BF16), pad_hi], axis=0))

    def own_prev(n):
        h, j = items[n]
        qi = pair * nq + j
        return h, qi, jnp.maximum(qi - 1, 0)

    def near_scores(n, qa):
        h, qi, prev_j = own_prev(n)
        qa_own, qa_prev = qa
        s_own = jnp.dot(k_ref[0, h, qi], qa_own, preferred_element_type=F32)
        s_prev = jnp.dot(k_ref[0, h, prev_j], qa_prev, preferred_element_type=F32)
        return s_own, s_prev

    def near_softmax(n, ss):
        h = items[n][0]
        s_own, s_prev = ss
        s = jnp.concatenate([s_own + bias_ref[h, :, blk:2 * blk],
                             s_prev + bias_ref[h, :, 0:blk]], axis=0)
        m0 = jnp.max(s, axis=0, keepdims=True)
        return m0, jnp.exp2(s - m0)

    def near_pv(n, mp):
        h, qi, prev_j = own_prev(n)
        m0, p = mp
        pb = p.astype(BF16)
        acc = (jnp.dot(vt_ref[0, h, qi], pb[0:blk], preferred_element_type=F32)
               + jnp.dot(vt_ref[0, h, prev_j], pb[blk:2 * blk], preferred_element_type=F32))
        return m0, acc

    near = skewed([select, near_scores, near_softmax, near_pv], MOBA_NEAR_LAGS)
    states = tuple(near[n] for n in range(len(items)))

    def far(gi, states, todo=None):
        j0 = gi * grp

        def qk(n, _):
            kt = k_ref[0, items[n][0], pl.ds(j0, grp)].reshape(grp * blk, LANES)
            return jnp.dot(kt, qa_ref[n], preferred_element_type=F32)

        def softmax(n, s):
            m_old = states[n][0]
            m_new = jnp.maximum(m_old, jnp.max(s, axis=0, keepdims=True))
            return m_new, jnp.exp2(m_old - m_new), jnp.exp2(s - m_new)

        def pv(n, sm):
            m_new, a, p = sm
            pb = p.astype(BF16)
            tot = a * states[n][1]
            for i in range(grp):
                tot = tot + jnp.dot(vt_ref[0, items[n][0], j0 + i], pb[i * blk:(i + 1) * blk],
                                    preferred_element_type=F32)
            return m_new, tot

        new = skewed([qk, softmax, pv], MOBA_FAR_LAGS, todo)
        return tuple(new.get(n, states[n]) for n in range(len(items)))

    sub = nq // grp
    states = lax.fori_loop(0, pair * sub, far, tuple(states))
    for extra in range(1, sub):
        later = [n for n, (_, j) in enumerate(items) if j // grp >= extra]
        states = far(pair * sub + extra - 1, states, later)

    for n, (h, j) in enumerate(items):
        acc = states[n][1]
        o_ref[0, h * hd:(h + 1) * hd, j * blk:(j + 1) * blk] = (
            acc[0:hd] / acc[hd:hd + 1]).astype(o_ref.dtype)


def _moba(qt, k_aug, vt, bias):
    bsz, nh, hd, s = qt.shape
    blk = MOBA_BLOCK
    hp = MOBA_HEADS_PER_STEP
    nq = MOBA_QBLOCKS_PER_STEP
    nblk = s // blk
    assert nh % hp == 0 and nblk % nq == 0 and nq % MOBA_FAR_GROUP == 0
    return pl.pallas_call(
        _moba_kernel,
        grid=(bsz, nh // hp, nblk // nq),
        in_specs=[pl.BlockSpec((1, hp, hd, nq * blk), lambda b, g, i: (b, g, 0, i)),
                  pl.BlockSpec((1, hp, nblk, blk, LANES), lambda b, g, i: (b, g, 0, 0, 0),
                               pipeline_mode=pl.Buffered(1)),
                  pl.BlockSpec((1, hp, nblk, MOBA_VT_ROWS, blk), lambda b, g, i: (b, g, 0, 0, 0),
                               pipeline_mode=pl.Buffered(1)),
                  pl.BlockSpec((hp, blk, 2 * blk), lambda b, g, i: (g, 0, 0),
                               pipeline_mode=pl.Buffered(1))],
        out_specs=pl.BlockSpec((1, hp * hd, nq * blk), lambda b, g, i: (b, g, i)),
        out_shape=jax.ShapeDtypeStruct((bsz, nh * hd, s), BF16),
        scratch_shapes=[pltpu.VMEM((hp, MOBA_MAX_BLOCKS, LANES), F32),
                        pltpu.VMEM((hp * nq, LANES, blk), BF16)],
        compiler_params=_cparams(("arbitrary", "arbitrary", "arbitrary")),
        name="moba",
    )(qt, k_aug, vt, bias)


def _gla_kernel(q_ref, k_ref, v_ref, gk_ref, wup_ref, bgk_ref, r_ref, gn_ref, o_ref,
                state_ref, *, nchunk):
    ch = GLA_CHUNK
    tc = nchunk * ch
    dk, dv = GLA_DK, GLA_DV

    @pl.when(pl.program_id(1) == 0)
    def _():
        state_ref[...] = jnp.zeros(state_ref.shape, F32)

    rin = lax.broadcasted_iota(I32, (tc, dk), 0) & (ch - 1)
    causal = (lax.broadcasted_iota(I32, (ch, ch), 1) <= lax.broadcasted_iota(I32, (ch, ch), 0))
    eye = (lax.broadcasted_iota(I32, (dk, dk), 0) == lax.broadcasted_iota(I32, (dk, dk), 1))
    chunks = [slice(n * ch, (n + 1) * ch) for n in range(nchunk)]
    gk = gk_ref[...]
    gk_hi = gk.astype(BF16)
    gk_lo = (gk - gk_hi.astype(F32)).astype(BF16)

    def prep(h, _):
        ks = slice(h * dk, (h + 1) * dk)
        w = wup_ref[:, ks]
        w_hi = w.astype(BF16)
        w_lo = (w - w_hi.astype(F32)).astype(BF16)
        z = (jnp.dot(gk_hi, w_hi, preferred_element_type=F32)
             + jnp.dot(gk_lo, w_hi, preferred_element_type=F32)
             + jnp.dot(gk_hi, w_lo, preferred_element_type=F32) + bgk_ref[:, ks])
        log_a = ((jnp.minimum(z, 0.0) - jnp.log(1.0 + jnp.exp(-jnp.abs(z))))
                 / GLA_GATE_NORMALIZER)
        b = log_a
        sh = 1
        while sh < ch:
            b = b + jnp.where(rin >= sh, pltpu.roll(b, sh, axis=0), 0.0)
            sh *= 2
        q = q_ref[:, ks].astype(F32) * (dk ** -0.5)
        k = k_ref[:, ks].astype(F32)
        q_g = (q * jnp.exp(b)).astype(BF16)
        k_g = (k * jnp.exp(-b)).astype(BF16)
        b3 = b.reshape(nchunk, ch, dk)
        b_last = b3[:, ch - 1:ch, :]
        k_end = (k * jnp.exp(jnp.broadcast_to(b_last, b3.shape) - b3).reshape(tc, dk)
                 ).astype(BF16)
        return q_g, k_g, k_end, jnp.exp(b_last)

    def local(h, pre):
        q_g, k_g, k_end, decay = pre
        o_intra, kv, decay_col = [], [], []
        for n, sl in enumerate(chunks):
            v_c = v_ref[sl, h * dv:(h + 1) * dv]
            att = jnp.where(causal, _nt_dot(q_g[sl], k_g[sl]), 0.0)
            o_intra.append(jnp.dot(att.astype(BF16), v_c, preferred_element_type=F32))
            kv.append(lax.dot_general(k_end[sl], v_c, (((0,), (0,)), ((), ())),
                                      preferred_element_type=F32))
            decay_col.append(jnp.sum(
                jnp.where(eye, jnp.broadcast_to(decay[n], (dk, dk)), 0.0),
                axis=1, keepdims=True))
        return q_g, o_intra, kv, decay_col

    def chain(h, loc):
        q_g, o_intra, kv, decay_col = loc
        state = state_ref[h]
        outs = []
        for n, sl in enumerate(chunks):
            outs.append(o_intra[n] + jnp.dot(q_g[sl], state.astype(BF16),
                                             preferred_element_type=F32))
            state = decay_col[n] * state + kv[n]
        state_ref[h] = state
        return jnp.concatenate(outs, axis=0)

    def finish(h, o):
        vs = slice(h * dv, (h + 1) * dv)
        o_ref[:, vs] = (_rms(o) * gn_ref[...] * r_ref[:, vs].astype(F32)).astype(o_ref.dtype)
        return None

    stages = [prep, local, chain, finish]
    vals = [None] * GLA_HEADS
    for step in range(GLA_HEADS + len(stages) - 1):
        for si, stage in enumerate(stages):
            h = step - si
            if 0 <= h < GLA_HEADS:
                vals[h] = stage(h, vals[h])


def _gla(qkb, vb, gk, wup, bgk, r_act, gn, bsz, seq, nchunk):
    t = qkb.shape[0]
    tc = nchunk * GLA_CHUNK
    nc = seq // tc
    rowblk = lambda w, off: pl.BlockSpec((tc, w), lambda b, c: (b * nc + c, off))
    full = lambda a: pl.BlockSpec(a.shape, lambda b, c: (0, 0))
    return pl.pallas_call(
        functools.partial(_gla_kernel, nchunk=nchunk),
        grid=(bsz, nc),
        in_specs=[rowblk(GLA_KEY_DIM, 0), rowblk(GLA_KEY_DIM, 1), rowblk(GLA_VALUE_DIM, 0),
                  rowblk(LANES, 0), full(wup), full(bgk), rowblk(GLA_VALUE_DIM, 0), full(gn)],
        out_specs=rowblk(GLA_VALUE_DIM, 0),
        out_shape=jax.ShapeDtypeStruct((t, GLA_VALUE_DIM), BF16),
        scratch_shapes=[pltpu.VMEM((GLA_HEADS, GLA_DK, GLA_DV), F32)],
        compiler_params=_cparams(("arbitrary", "arbitrary")),
        name="gla",
    )(qkb, qkb, vb, gk, wup, bgk, r_act, gn)


def _merge_kernel(ya_ref, yb_ref, g_ref, x_ref, wpa_ref, wpb_ref, wout_ref, g1_ref,
                  nw_ref, sc_ref, sh_ref, wr_ref, br_ref,
                  x1_ref, h2_ref, idx_ref, wts_ref):
    pa = lax.dot_general(ya_ref[0], wpa_ref[...], (((0,), (0,)), ((), ())),
                         preferred_element_type=F32)
    pb = jnp.dot(yb_ref[...], wpb_ref[...], preferred_element_type=F32)
    mixed = (g_ref[:, 0:D_MODEL].astype(F32) * pa
             + g_ref[:, D_MODEL:2 * D_MODEL].astype(F32) * pb)
    y = jnp.dot(mixed.astype(BF16), wout_ref[...], preferred_element_type=F32)
    x1 = x_ref[...] + g1_ref[0] * y
    x1_ref[...] = x1
    h2 = _rms(x1) * nw_ref[...]
    h2 = h2 * (1.0 + sc_ref[0]) + sh_ref[0]
    _store_rows(h2_ref, _pack_halves(h2))
    h_hi = h2.astype(BF16)
    h_lo = (h2 - h_hi.astype(F32)).astype(BF16)
    w = wr_ref[...]
    w_hi = w.astype(BF16)
    w_lo = (w - w_hi.astype(F32)).astype(BF16)
    logits = (_nt_dot(w_hi, h_hi) + _nt_dot(w_hi, h_lo) + _nt_dot(w_lo, h_hi)
              + br_ref[...])
    rowf = lax.broadcasted_iota(I32, logits.shape, 0).astype(F32)
    vals, idxs = [], []
    cur = logits
    for _ in range(TOP_K):
        mx = jnp.max(cur, axis=0, keepdims=True)
        first = jnp.min(jnp.where(cur == mx, rowf, float(N_EXPERTS)), axis=0, keepdims=True)
        vals.append(mx)
        idxs.append(first)
        cur = jnp.where(rowf == first, -jnp.inf, cur)
    es = [jnp.exp(v - vals[0]) for v in vals]
    tot = es[0]
    for e in es[1:]:
        tot = tot + e
    idx_ref[...] = jnp.concatenate(idxs, axis=0).astype(I32)
    wts_ref[...] = jnp.concatenate([e / tot for e in es], axis=0)


def _merge(ya, yb, gates, x2d, wpa, wpb, wout, g1, nw, sc, sh, wr_t, br, seq, tm):
    t, d = x2d.shape
    tpb = seq // tm
    row = lambda w: pl.BlockSpec((tm, w), lambda i: (i, 0))
    full = lambda a: pl.BlockSpec(a.shape, lambda i: (0,) * a.ndim)
    per_b = pl.BlockSpec((1, 1, d), lambda i: (i // tpb, 0, 0))
    colblk = pl.BlockSpec((TOP_K, tm), lambda i: (0, i))
    return pl.pallas_call(
        _merge_kernel,
        grid=(t // tm,),
        in_specs=[pl.BlockSpec((1, MOBA_WIDTH, tm), lambda i: (i // tpb, 0, i % tpb)),
                  row(GLA_VALUE_DIM), row(2 * D_MODEL), row(d),
                  full(wpa), full(wpb), full(wout), per_b,
                  pl.BlockSpec((1, d), lambda i: (0, 0)), per_b, per_b,
                  full(wr_t), full(br)],
        out_specs=[row(d), pl.BlockSpec((tm * ROW_SLABS, LANES), lambda i: (i, 0)),
                   colblk, colblk],
        out_shape=[jax.ShapeDtypeStruct((t, d), F32),
                   jax.ShapeDtypeStruct((t * ROW_SLABS, LANES), U32),
                   jax.ShapeDtypeStruct((TOP_K, t), I32), jax.ShapeDtypeStruct((TOP_K, t), F32)],
        compiler_params=_cparams(("arbitrary",)),
        name="merge",
    )(ya, yb, gates, x2d, wpa, wpb, wout, g1, nw.reshape(1, d), sc, sh, wr_t, br)


def _rank_kernel(idx_ref, rank_ref, cnt_ref, carry_ref):
    tm = idx_ref.shape[1]

    @pl.when(pl.program_id(0) == 0)
    def _():
        carry_ref[...] = jnp.zeros(carry_ref.shape, F32)

    rows = lax.broadcasted_iota(I32, (N_EXPERTS, tm), 0)
    before = (lax.broadcasted_iota(I32, (tm, tm), 0)
              < lax.broadcasted_iota(I32, (tm, tm), 1))
    upper = jnp.where(before, 1.0, 0.0).astype(BF16)
    carry = carry_ref[:, 0:1]
    ranks = []
    for k in range(TOP_K):
        onehot = idx_ref[k:k + 1, :] == rows
        onef = jnp.where(onehot, 1.0, 0.0)
        earlier = jnp.dot(onef.astype(BF16), upper, preferred_element_type=F32) + carry
        ranks.append(jnp.sum(jnp.where(onehot, earlier, 0.0), axis=0, keepdims=True))
        carry = carry + jnp.sum(onef, axis=1, keepdims=True)
    rank_ref[...] = jnp.concatenate(ranks, axis=0).astype(I32)
    total = jnp.broadcast_to(carry, carry_ref.shape)
    carry_ref[...] = total
    cnt_ref[...] = total


def _rank(idx, tm):
    t = idx.shape[1]
    return pl.pallas_call(
        _rank_kernel,
        grid=(t // tm,),
        in_specs=[pl.BlockSpec((TOP_K, tm), lambda i: (0, i))],
        out_specs=[pl.BlockSpec((TOP_K, tm), lambda i: (0, i)),
                   pl.BlockSpec((N_EXPERTS, LANES), lambda i: (0, 0))],
        out_shape=[jax.ShapeDtypeStruct((TOP_K, t), I32),
                   jax.ShapeDtypeStruct((N_EXPERTS, LANES), F32)],
        scratch_shapes=[pltpu.VMEM((N_EXPERTS, LANES), F32)],
        compiler_params=_cparams(("arbitrary",)),
        name="rank",
    )(idx)


def _dest_kernel(pstart_ref, idx_ref, rank_ref, dest_ref):
    idx = idx_ref[...]
    off = jnp.zeros(idx.shape, I32)
    for e in range(N_EXPERTS):
        off = jnp.where(idx == e, pstart_ref[e], off)
    dest_ref[...] = rank_ref[...] + off


def _dest(pstart, idx, rank, tm):
    t = idx.shape[1]
    blk = pl.BlockSpec((TOP_K, tm), lambda i: (0, i))
    return pl.pallas_call(
        _dest_kernel,
        grid=(t // tm,),
        in_specs=[pl.BlockSpec(memory_space=pltpu.SMEM), blk, blk],
        out_specs=blk,
        out_shape=jax.ShapeDtypeStruct((TOP_K, t), I32),
        compiler_params=_cparams(("arbitrary",)),
        name="dest",
    )(pstart, idx, rank)


def _scatter_kernel(dest_ref, zrow_ref, h_ref, xout_hbm, zbuf, stage, sems, zsem, *, tm):
    @pl.when(pl.program_id(0) == 0)
    def _():
        zbuf[...] = jnp.zeros(zbuf.shape, zbuf.dtype)

        def zero_rows(row0):
            line0 = pl.multiple_of(row0 * ROW_SLABS, MOE_ROWS * ROW_SLABS)
            return pltpu.make_async_copy(
                zbuf, xout_hbm.at[pl.ds(line0, MOE_ROWS * ROW_SLABS)], zsem)

        for e in range(N_EXPERTS):
            zero_rows(zrow_ref[e]).start()
        for e in range(N_EXPERTS):
            zero_rows(zrow_ref[e]).wait()

        def start_tail(j, carry):
            zero_rows(j * MOE_ROWS).start()
            return carry

        def wait_tail(j, carry):
            zero_rows(j * MOE_ROWS).wait()
            return carry

        n_all = xout_hbm.shape[0] // (MOE_ROWS * ROW_SLABS)
        lax.fori_loop(zrow_ref[N_EXPERTS], n_all, start_tail, 0)
        lax.fori_loop(zrow_ref[N_EXPERTS], n_all, wait_tail, 0)

    i = pl.program_id(0)
    slot = i % 2
    src = stage.at[slot]
    src[...] = h_ref[...]

    def issue(g, carry):
        base = pl.multiple_of(g * ROW_UNROLL, ROW_UNROLL)
        for j in range(ROW_UNROLL):
            for k in range(TOP_K):
                d = dest_ref[(base + j) * TOP_K + k]
                pltpu.make_async_copy(_row(src, base + j), _row(xout_hbm, d),
                                      sems.at[slot]).start(priority=k % 2)
        return carry

    lax.fori_loop(0, tm // ROW_UNROLL, issue, 0)

    def wait_tile(s):
        for k in range(TOP_K):
            pltpu.make_async_copy(stage.at[s], xout_hbm.at[pl.ds(0, tm * ROW_SLABS)],
                                  sems.at[s]).wait()

    @pl.when(i > 0)
    def _():
        wait_tile(1 - slot)

    @pl.when(i == pl.num_programs(0) - 1)
    def _():
        wait_tile(slot)


def _scatter(dest_tok, last_block_row, h2, n_pad, tm):
    t = h2.shape[0] // ROW_SLABS
    return pl.pallas_call(
        functools.partial(_scatter_kernel, tm=tm),
        grid=(t // tm,),
        in_specs=[pl.BlockSpec((tm * TOP_K,), lambda i: (i,), memory_space=pltpu.SMEM),
                  pl.BlockSpec(memory_space=pltpu.SMEM),
                  pl.BlockSpec((tm * ROW_SLABS, LANES), lambda i: (i, 0))],
        out_specs=pl.BlockSpec(memory_space=pl.ANY),
        out_shape=jax.ShapeDtypeStruct((n_pad * ROW_SLABS, LANES), h2.dtype),
        scratch_shapes=[pltpu.VMEM((MOE_ROWS * ROW_SLABS, LANES), h2.dtype),
                        pltpu.VMEM((2, tm * ROW_SLABS, LANES), h2.dtype),
                        pltpu.SemaphoreType.DMA((2,)), pltpu.SemaphoreType.DMA],
        compiler_params=_cparams(("arbitrary",)),
        name="scatter",
    )(dest_tok, last_block_row, h2)


def _expert_kernel(be_ref, nused_ref, x_ref, wg_ref, bg_ref, wu_ref, bu_ref, wd_ref, bd_ref,
                   o_ref, wgb_ref, wub_ref, wdb_ref):
    i = pl.program_id(0)
    prev = be_ref[jnp.maximum(i - 1, 0)]

    @pl.when(jnp.logical_or(i == 0, be_ref[i] != prev))
    def _():
        wgb_ref[...] = wg_ref[0].astype(BF16)
        wub_ref[...] = wu_ref[0].astype(BF16)
        wdb_ref[...] = wd_ref[0].astype(BF16)

    @pl.when(i < nused_ref[0])
    def _():
        xb = jnp.concatenate([v.astype(BF16) for v in _unpack_halves(_load_rows(x_ref))],
                             axis=1)
        g = jnp.minimum(jnp.dot(xb, wgb_ref[...], preferred_element_type=F32) + bg_ref[0],
                        SWIGLU_LIMIT)
        u = jnp.clip(jnp.dot(xb, wub_ref[...], preferred_element_type=F32) + bu_ref[0],
                     -SWIGLU_LIMIT, SWIGLU_LIMIT)
        act = g * _sigmoid(SWIGLU_ALPHA * g) * (u + 1.0)
        _store_rows(o_ref, _pack_halves(
            jnp.dot(act.astype(BF16), wdb_ref[...], preferred_element_type=F32) + bd_ref[0]))

    @pl.when(i >= nused_ref[0])
    def _():
        o_ref[...] = jnp.zeros(o_ref.shape, o_ref.dtype)


def _expert(block_expert, n_used, x_pad, wg, bg, wu, bu, wd, bd):
    lines = MOE_ROWS * ROW_SLABS
    _, d, f = wg.shape
    nblk = x_pad.shape[0] // lines
    wspec = lambda a, b: pl.BlockSpec((1, a, b), lambda i, be, nu: (be[i], 0, 0))
    grid_spec = pltpu.PrefetchScalarGridSpec(
        num_scalar_prefetch=2,
        grid=(nblk,),
        in_specs=[pl.BlockSpec((lines, LANES),
                               lambda i, be, nu: (jnp.minimum(i, nu[0] - 1), 0)),
                  wspec(d, f), wspec(1, f), wspec(d, f), wspec(1, f), wspec(f, d), wspec(1, d)],
        out_specs=pl.BlockSpec((lines, LANES), lambda i, be, nu: (i, 0)),
        scratch_shapes=[pltpu.VMEM((d, f), BF16), pltpu.VMEM((d, f), BF16),
                        pltpu.VMEM((f, d), BF16)])
    return pl.pallas_call(
        _expert_kernel,
        grid_spec=grid_spec,
        out_shape=jax.ShapeDtypeStruct(x_pad.shape, U32),
        compiler_params=_cparams(("arbitrary",)),
        name="expert",
    )(block_expert, n_used, x_pad, wg, bg.reshape(N_EXPERTS, 1, f), wu,
      bu.reshape(N_EXPERTS, 1, f), wd, bd.reshape(N_EXPERTS, 1, d))


def _combine_kernel(dest_ref, y_hbm, x1_ref, wts_ref, g2_ref, nf_ref, o_ref, ybuf, sems, *,
                    tm, final):
    i = pl.program_id(0)
    n_tiles = pl.num_programs(0) - 1
    slot = i % 2

    @pl.when(i < n_tiles)
    def _():
        def row_copy(tt, k):
            d = dest_ref[tt * TOP_K + k]
            return pltpu.make_async_copy(_row(y_hbm, d), _row(ybuf.at[slot, k], tt),
                                         sems.at[slot])

        def issue(g, carry):
            base = pl.multiple_of(g * ROW_UNROLL, ROW_UNROLL)
            for j in range(ROW_UNROLL):
                for k in range(TOP_K):
                    row_copy(base + j, k).start(priority=k % 2)
            return carry

        lax.fori_loop(0, tm // ROW_UNROLL, issue, 0)

    @pl.when(i > 0)
    def _():
        done = 1 - slot
        for k in range(TOP_K):
            pltpu.make_async_copy(y_hbm.at[pl.ds(0, tm * ROW_SLABS)], ybuf.at[done, k],
                                  sems.at[done]).wait()
        w = wts_ref[...]
        lo, hi = None, None
        for k in range(TOP_K):
            y_lo, y_hi = _unpack_halves(_load_rows(ybuf.at[done, k]))
            lo = w[:, k:k + 1] * y_lo + (0.0 if lo is None else lo)
            hi = w[:, k:k + 1] * y_hi + (0.0 if hi is None else hi)
        moe = jnp.concatenate([lo, hi], axis=1)
        x2 = x1_ref[...] + g2_ref[0] * moe
        o_ref[...] = _rms(x2) * nf_ref[...] if final else x2


def _combine(dest_tok, y_pad, x1, wts_tok, g2, nf, seq, tm, final):
    t, d = x1.shape
    tpb = seq // tm
    n_tiles = t // tm
    lag = lambda i: jnp.maximum(i - 1, 0)
    return pl.pallas_call(
        functools.partial(_combine_kernel, tm=tm, final=final),
        grid=(n_tiles + 1,),
        in_specs=[pl.BlockSpec((tm * TOP_K,), lambda i: (jnp.minimum(i, n_tiles - 1),),
                               memory_space=pltpu.SMEM),
                  pl.BlockSpec(memory_space=pl.ANY),
                  pl.BlockSpec((tm, d), lambda i: (lag(i), 0)),
                  pl.BlockSpec((tm, TOP_K), lambda i: (lag(i), 0)),
                  pl.BlockSpec((1, 1, d), lambda i: (lag(i) // tpb, 0, 0)),
                  pl.BlockSpec((1, d), lambda i: (0, 0))],
        out_specs=pl.BlockSpec((tm, d), lambda i: (lag(i), 0)),
        out_shape=jax.ShapeDtypeStruct((t, d), F32),
        scratch_shapes=[pltpu.VMEM((2, TOP_K, tm * ROW_SLABS, LANES), y_pad.dtype),
                        pltpu.SemaphoreType.DMA((2,))],
        compiler_params=_cparams(("arbitrary",)),
        name="combine",
    )(dest_tok, y_pad, x1, wts_tok, g2, nf.reshape(1, d))


def _pick(n, cands):
    for c in cands:
        if n % c == 0:
            return c
    raise ValueError(f"no tile in {cands} divides {n}")


def kernel(x, c, rel_bias, w_ada, b_ada, norm_mix, w_in, w_gk_up, b_gk, gla_norm,
           w_proj_moba, w_proj_gla, w_out, norm_ffn, w_router, b_router,
           w_gate, b_gate, w_up, b_up, w_down, b_down, norm_final):
    bsz, seq, d = x.shape
    depth = w_ada.shape[0]
    assert d == D_MODEL and seq % MOBA_BLOCK == 0 and seq // MOBA_BLOCK <= MOBA_MAX_BLOCKS
    t = bsz * seq
    tm = _pick(seq, (512, 256))
    nchunk = _pick(seq // GLA_CHUNK, (8, 4))
    n_blk = seq // MOBA_BLOCK
    x2d = x.reshape(t, d)
    bias = _bias_tiles(rel_bias)
    per_b = lambda v: v.reshape(bsz, 1, d)

    for l in range(depth):
        mod = _ada(c, w_ada[l], b_ada[l])
        sh1, sc1, g1, sh2, sc2, g2 = [per_b(m) for m in jnp.split(mod, 6, axis=-1)]
        qt, k_aug, vt, qkb, vb, gk, r_act, gates = _inproj(
            x2d, norm_mix[l], sc1, sh1, _regroup_w_in(w_in[l]), bsz, seq, tm)
        ya = _moba(qt, k_aug, vt, bias)
        wup = jnp.pad(w_gk_up[l], ((0, LANES - GLA_GATE_RANK), (0, 0)))
        yb = _gla(qkb, vb, gk, wup, b_gk[l].reshape(1, -1), r_act,
                  gla_norm[l].reshape(1, -1), bsz, seq, nchunk)
        x1, h2, idx, wts = _merge(
            ya, yb, gates, x2d, w_proj_moba[l].astype(BF16), w_proj_gla[l].astype(BF16),
            w_out[l].astype(BF16), g1, norm_ffn[l], sc2, sh2,
            w_router[l].T, b_router[l].reshape(N_EXPERTS, 1), seq, tm)
        rank, cnt = _rank(idx, tm)
        counts = cnt[:, 0].astype(I32)
        padded = (counts + MOE_ROWS - 1) // MOE_ROWS * MOE_ROWS
        pcum = jnp.cumsum(padded)
        pstart = (pcum - padded).astype(I32)
        n_blocks = (t * TOP_K + MOE_ROWS - 1) // MOE_ROWS + N_EXPERTS
        block_row0 = jnp.arange(n_blocks, dtype=I32) * MOE_ROWS
        block_expert = jnp.minimum(
            jnp.sum((pcum[None, :] <= block_row0[:, None]).astype(I32), axis=1),
            N_EXPERTS - 1).astype(I32)
        n_used = (pcum[-1:] // MOE_ROWS).astype(I32)
        dest = _dest(pstart, idx, rank, tm)
        dest_tok = dest.T.reshape(t * TOP_K)
        tg = _pick(seq, (512, 256))
        zero_info = jnp.concatenate([jnp.maximum(pcum - MOE_ROWS, 0).astype(I32), n_used])
        x_pad = _scatter(dest_tok, zero_info, h2, n_blocks * MOE_ROWS, tg)
        y_pad = _expert(block_expert, n_used, x_pad, w_gate[l], b_gate[l], w_up[l], b_up[l],
                        w_down[l], b_down[l])
        x2d = _combine(dest_tok, y_pad, x1, wts.T, g2, norm_final, seq, tg, l == depth - 1)
    return x2d.reshape(bsz, seq, d)
```

```python
import functools
import math

import numpy as np
import jax
import jax.numpy as jnp
from jax import lax
from jax.experimental import pallas as pl
from jax.experimental.pallas import tpu as pltpu

F32 = jnp.float32
BF16 = jnp.bfloat16
I32 = jnp.int32
HIGHEST = lax.Precision.HIGHEST

D_MODEL = 1024
MOBA_HEADS = 8
MOBA_HEAD_DIM = 64
MOBA_WIDTH = MOBA_HEADS * MOBA_HEAD_DIM
MOBA_BLOCK = 256
MOBA_TOPK = 3
MOBA_MAX_BLOCKS = 32
REL_BUCKETS = 32
REL_MAX_DIST = 128
GLA_HEADS = 4
GLA_KEY_DIM = D_MODEL // 2
GLA_VALUE_DIM = D_MODEL
GLA_DK = GLA_KEY_DIM // GLA_HEADS
GLA_DV = GLA_VALUE_DIM // GLA_HEADS
GLA_GATE_RANK = 16
GLA_GATE_NORMALIZER = 16.0
GLA_CHUNK = 64
N_EXPERTS = 32
TOP_K = 4
D_FF = D_MODEL
SWIGLU_ALPHA = 1.702
SWIGLU_LIMIT = 7.0
MOE_ROWS = 512
ROW_UNROLL = 8
EPS = 1e-6
LANES = 128
SUBLANES = 8
BF16_SUBLANES = 16
ADA_COLS = 768
NEG_BIG = -1e30
LOG2E = math.log2(math.e)
VMEM_LIMIT = 56 * 1024 * 1024


def _cparams(sem, vmem=None):
    return pltpu.CompilerParams(dimension_semantics=sem,
                                vmem_limit_bytes=vmem or VMEM_LIMIT)


def _nt_dot(a, b, **kw):
    return lax.dot_general(a, b, (((1,), (1,)), ((), ())),
                           preferred_element_type=F32, **kw)


def _rms(x):
    return x * lax.rsqrt(jnp.mean(x * x, axis=-1, keepdims=True) + EPS)


def _sigmoid(x):
    return 1.0 / (1.0 + jnp.exp(-x))


U32 = jnp.uint32
_HI16 = 0xFFFF0000


def _pack_halves(x):
    n = x.shape[1] // 2
    lo = pltpu.bitcast(x[:, :n].astype(BF16).astype(F32), U32)
    hi = pltpu.bitcast(x[:, n:].astype(BF16).astype(F32), U32)
    return (hi & U32(_HI16)) | (lo >> 16)


def _unpack_halves(w):
    return (pltpu.bitcast(w << 16, F32), pltpu.bitcast(w & U32(_HI16), F32))


ROW_SLABS = D_MODEL // 2 // LANES


def _store_rows(ref, words):
    m = words.shape[0]
    for c in range(ROW_SLABS):
        ref[pl.ds(c, m, stride=ROW_SLABS), :] = words[:, c * LANES:(c + 1) * LANES]


def _load_rows(ref):
    m = ref.shape[0] // ROW_SLABS
    return jnp.concatenate(
        [ref[pl.ds(c, m, stride=ROW_SLABS), :] for c in range(ROW_SLABS)], axis=1)


def _row(ref, i):
    return ref.at[pl.ds(pl.multiple_of(i * ROW_SLABS, ROW_SLABS), ROW_SLABS)]


def _ada_kernel(c_ref, w_ref, b_ref, o_ref):
    c = c_ref[...]
    s = c * _sigmoid(c)
    o_ref[...] = jnp.dot(s, w_ref[...], precision=HIGHEST,
                         preferred_element_type=F32) + b_ref[...]


def _ada(c, w, b):
    bsz, d = c.shape
    n = w.shape[1]
    rows = -(-bsz // SUBLANES) * SUBLANES
    cp = jnp.zeros((rows, d), F32).at[:bsz].set(c)
    tn = ADA_COLS
    out = pl.pallas_call(
        _ada_kernel,
        grid=(n // tn,),
        in_specs=[pl.BlockSpec((rows, d), lambda j: (0, 0)),
                  pl.BlockSpec((d, tn), lambda j: (0, j)),
                  pl.BlockSpec((1, tn), lambda j: (0, j))],
        out_specs=pl.BlockSpec((rows, tn), lambda j: (0, j)),
        out_shape=jax.ShapeDtypeStruct((rows, n), F32),
        compiler_params=_cparams(("arbitrary",)),
        name="ada",
    )(cp, w, b.reshape(1, n))
    return out[:bsz]


_OFF_QA = 0
_OFF_KA = _OFF_QA + MOBA_WIDTH
_OFF_VA = _OFF_KA + MOBA_WIDTH
_OFF_QKB = _OFF_VA + MOBA_WIDTH
_OFF_VB = _OFF_QKB + 2 * GLA_KEY_DIM
_OFF_GK = _OFF_VB + GLA_VALUE_DIM
_OFF_R = _OFF_GK + LANES
_OFF_G = _OFF_R + GLA_VALUE_DIM
_W_CAT = _OFF_G + 2 * D_MODEL


def _regroup_w_in(w):
    o_gk = 3 * MOBA_WIDTH + 2 * GLA_KEY_DIM + GLA_VALUE_DIM
    gk = jnp.pad(w[:, o_gk:o_gk + GLA_GATE_RANK], ((0, 0), (0, LANES - GLA_GATE_RANK)))
    return jnp.concatenate([w[:, :o_gk], gk, w[:, o_gk + GLA_GATE_RANK:]],
                           axis=1).astype(BF16)


def _inproj_kernel(x_ref, nw_ref, sc_ref, sh_ref, w_ref,
                   qt_ref, ka_ref, vt_ref, qkb_ref, vb_ref, gk_ref, r_ref, g_ref, *, tpb):
    tm = x_ref.shape[0]
    hd = MOBA_HEAD_DIM
    nbt = tm // MOBA_BLOCK
    h = _rms(x_ref[...]) * nw_ref[...]
    h = h * (1.0 + sc_ref[0]) + sh_ref[0]
    hb = h.astype(BF16)

    def mm(a, b):
        return jnp.dot(hb, w_ref[:, a:b], preferred_element_type=F32)

    q_t = (mm(_OFF_QA, _OFF_KA) * (hd ** -0.5 * LOG2E)).T
    v_t = mm(_OFF_VA, _OFF_QKB).T
    k_all = mm(_OFF_KA, _OFF_VA)
    blk0 = (pl.program_id(0) % tpb) * nbt
    lane = lax.broadcasted_iota(I32, (MOBA_BLOCK, LANES), 1)
    ones_rows = jnp.where(
        lax.broadcasted_iota(I32, (MOBA_VT_ROWS - hd, MOBA_BLOCK), 0) == 0, 1.0, 0.0)
    heads_per_tile = LANES // hd
    for hh in range(MOBA_HEADS):
        qt_ref[0, hh] = q_t[hh * hd:(hh + 1) * hd].astype(BF16)
        tile = hh // heads_per_tile
        k_h = k_all[:, tile * LANES:(tile + 1) * LANES]
        if hh % heads_per_tile:
            k_h = pltpu.roll(k_h, LANES - (hh % heads_per_tile) * hd, axis=1)
        for j in range(nbt):
            rows = slice(j * MOBA_BLOCK, (j + 1) * MOBA_BLOCK)
            onehot = jnp.where(lane == hd + blk0 + j, 1.0, 0.0)
            ka_ref[0, hh, j] = jnp.where(lane < hd, k_h[rows], onehot).astype(BF16)
            vt_ref[0, hh, j, 0:hd, :] = v_t[hh * hd:(hh + 1) * hd, rows].astype(BF16)
            vt_ref[0, hh, j, hd:MOBA_VT_ROWS, :] = ones_rows.astype(BF16)
    qkb_ref[...] = mm(_OFF_QKB, _OFF_VB).astype(BF16)
    vb_ref[...] = mm(_OFF_VB, _OFF_GK).astype(BF16)
    gk_ref[...] = mm(_OFF_GK, _OFF_R)
    r = mm(_OFF_R, _OFF_G)
    r_ref[...] = (r * _sigmoid(r)).astype(BF16)
    g_ref[...] = _sigmoid(mm(_OFF_G, _W_CAT)).astype(BF16)


def _inproj(x2d, nw, sc, sh, w_cat, bsz, seq, tm):
    t, d = x2d.shape
    tpb = seq // tm
    nbt = tm // MOBA_BLOCK
    nh, hd = MOBA_HEADS, MOBA_HEAD_DIM
    row = lambda w: pl.BlockSpec((tm, w), lambda i: (i, 0))
    per_b = pl.BlockSpec((1, 1, d), lambda i: (i // tpb, 0, 0))
    rows_out = [(2 * GLA_KEY_DIM, BF16), (GLA_VALUE_DIM, BF16), (LANES, F32),
                (GLA_VALUE_DIM, BF16), (2 * D_MODEL, BF16)]
    return pl.pallas_call(
        functools.partial(_inproj_kernel, tpb=tpb),
        grid=(t // tm,),
        in_specs=[row(d), pl.BlockSpec((1, d), lambda i: (0, 0)), per_b, per_b,
                  pl.BlockSpec((d, _W_CAT), lambda i: (0, 0), pipeline_mode=pl.Buffered(1))],
        out_specs=[pl.BlockSpec((1, nh, hd, tm), lambda i: (i // tpb, 0, 0, i % tpb)),
                   pl.BlockSpec((1, nh, nbt, MOBA_BLOCK, LANES),
                                lambda i: (i // tpb, 0, i % tpb, 0, 0)),
                   pl.BlockSpec((1, nh, nbt, MOBA_VT_ROWS, MOBA_BLOCK),
                                lambda i: (i // tpb, 0, i % tpb, 0, 0))]
                  + [row(w) for w, _ in rows_out],
        out_shape=[jax.ShapeDtypeStruct((bsz, nh, hd, seq), BF16),
                   jax.ShapeDtypeStruct((bsz, nh, seq // MOBA_BLOCK, MOBA_BLOCK, LANES), BF16),
                   jax.ShapeDtypeStruct((bsz, nh, seq // MOBA_BLOCK, MOBA_VT_ROWS, MOBA_BLOCK),
                                        BF16)]
                  + [jax.ShapeDtypeStruct((t, w), dt) for w, dt in rows_out],
        compiler_params=_cparams(("arbitrary",)),
        name="inproj",
    )(x2d, nw.reshape(1, d), sc, sh, w_cat)


def _t5_bucket_np(n):
    n = np.maximum(n, 0)
    max_exact = REL_BUCKETS // 2
    nf = np.maximum(n, max_exact).astype(np.float32)
    large = max_exact + (np.log(nf / max_exact) / math.log(REL_MAX_DIST / max_exact)
                         * (REL_BUCKETS - max_exact)).astype(np.int32)
    large = np.minimum(large, REL_BUCKETS - 1)
    return np.where(n < max_exact, n, large).astype(np.int32)


def _bucket_table():
    kj = np.arange(MOBA_BLOCK)[:, None]
    qi = np.arange(2 * MOBA_BLOCK)[None, :] % MOBA_BLOCK
    prev = np.arange(2 * MOBA_BLOCK)[None, :] < MOBA_BLOCK
    bucket = _t5_bucket_np(qi - kj + np.where(prev, MOBA_BLOCK, 0))
    return np.where(prev | (kj <= qi), bucket, -1).astype(np.int32)


def _bias_kernel(rb_ref, bucket_ref, o_ref):
    h = pl.program_id(0)
    bk = bucket_ref[...]
    far = rb_ref[(REL_BUCKETS - 1) * MOBA_HEADS + h]
    acc = jnp.zeros(bk.shape, F32)
    for b in range(REL_BUCKETS):
        acc = jnp.where(bk == b, rb_ref[b * MOBA_HEADS + h] - far, acc)
    o_ref[0] = jnp.where(bk < 0, NEG_BIG, acc * LOG2E)


def _bias_tiles(rel_bias):
    bucket = jnp.asarray(_bucket_table())
    return pl.pallas_call(
        _bias_kernel,
        grid=(MOBA_HEADS,),
        in_specs=[pl.BlockSpec(memory_space=pltpu.SMEM),
                  pl.BlockSpec(bucket.shape, lambda h: (0, 0))],
        out_specs=pl.BlockSpec((1,) + bucket.shape, lambda h: (h, 0, 0)),
        out_shape=jax.ShapeDtypeStruct((MOBA_HEADS,) + bucket.shape, F32),
        compiler_params=_cparams(("arbitrary",)),
        name="bias",
    )(rel_bias.reshape(-1), bucket)


MOBA_HEADS_PER_STEP = 8
MOBA_FAR_GROUP = 2
MOBA_QBLOCKS_PER_STEP = 4
MOBA_FAR_LAGS = (0, 3, 6)
MOBA_NEAR_LAGS = (0, 3, 6, 9)
MOBA_VT_ROWS = MOBA_HEAD_DIM + BF16_SUBLANES


def _moba_kernel(qt_ref, k_ref, vt_ref, bias_ref, o_ref, kmean_ref, qa_ref):
    blk = MOBA_BLOCK
    nb = MOBA_MAX_BLOCKS
    hd = MOBA_HEAD_DIM
    hp = MOBA_HEADS_PER_STEP
    grp = MOBA_FAR_GROUP
    nq = MOBA_QBLOCKS_PER_STEP
    pair = pl.program_id(2)
    neg = -jnp.inf
    items = [(h, j) for j in range(nq) for h in range(hp)]

    @pl.when(pair == 0)
    def _():
        kmean_ref[...] = jnp.zeros(kmean_ref.shape, F32)

    for h in range(hp):
        for j in range(nq):
            kmean_ref[h, pl.ds(pair * nq + j, 1), :] = jnp.mean(
                k_ref[0, h, pair * nq + j].astype(F32), axis=0, keepdims=True)

    row = lax.broadcasted_iota(I32, (nb, blk), 0)
    rowf = row.astype(F32)
    pad = jnp.zeros((LANES - hd, blk), BF16)
    pad_hi = jnp.zeros((LANES - hd - nb, blk), BF16)

    def skewed(stages, lags, todo=None):
        todo = list(range(len(items))) if todo is None else todo
        vals = {}
        for step in range(len(todo) + lags[-1]):
            for stage, lag in zip(stages, lags):
                pos = step - lag
                if 0 <= pos < len(todo):
                    vals[todo[pos]] = stage(todo[pos], vals.get(todo[pos]))
        return vals

    def select(n, _):
        h, j = items[n]
        qi = pair * nq + j
        qt = qt_ref[0, h, :, j * blk:(j + 1) * blk]
        km = kmean_ref[h, :, 0:hd]
        km_hi = km.astype(BF16)
        km_lo = (km - km_hi.astype(F32)).astype(BF16)
        gate = (jnp.dot(km_hi, qt, preferred_element_type=F32)
                + jnp.dot(km_lo, qt, preferred_element_type=F32))
        g = jnp.where(row < qi, gate, neg)
        sel = jnp.zeros((nb, blk), F32)
        for _ in range(MOBA_TOPK):
            mx = jnp.max(g, axis=0, keepdims=True)
            first = jnp.min(jnp.where(g == mx, rowf, float(nb)), axis=0, keepdims=True)
            pick = rowf == jnp.where(mx > neg, first, -1.0)
            sel = jnp.where(pick, 1.0, sel)
            g = jnp.where(pick, neg, g)
        mask_prev = jnp.where(sel > 0.0, jnp.where(row == qi - 1, 0.0, NEG_BIG), NEG_BIG)
        mask_far = jnp.where(sel > 0.0, jnp.where(row < qi - 1, 0.0, NEG_BIG), NEG_BIG)
        qa_ref[n] = jnp.concatenate([qt, mask_far.astype(BF16), pad_hi], axis=0)
        return (jnp.concatenate([qt, pad], axis=0),
                jnp.concatenate([qt, mask_prev.astype(BF16), pad_hi], axis=0))

    def own_prev(n):
        h, j = items[n]
        qi = pair * nq + j
        return h, qi, jnp.maximum(qi - 1, 0)

    def near_scores(n, qa):
        h, qi, prev_j = own_prev(n)
        qa_own, qa_prev = qa
        s_own = jnp.dot(k_ref[0, h, qi], qa_own, preferred_element_type=F32)
        s_prev = jnp.dot(k_ref[0, h, prev_j], qa_prev, preferred_element_type=F32)
        return s_own, s_prev

    def near_softmax(n, ss):
        h = items[n][0]
        s_own, s_prev = ss
        s = jnp.concatenate([s_own + bias_ref[h, :, blk:2 * blk],
                             s_prev + bias_ref[h, :, 0:blk]], axis=0)
        m0 = jnp.max(s, axis=0, keepdims=True)
        return m0, jnp.exp2(s - m0)

    def near_pv(n, mp):
        h, qi, prev_j = own_prev(n)
        m0, p = mp
        pb = p.astype(BF16)
        acc = (jnp.dot(vt_ref[0, h, qi], pb[0:blk], preferred_element_type=F32)
               + jnp.dot(vt_ref[0, h, prev_j], pb[blk:2 * blk], preferred_element_type=F32))
        return m0, acc

    near = skewed([select, near_scores, near_softmax, near_pv], MOBA_NEAR_LAGS)
    states = tuple(near[n] for n in range(len(items)))

    def far(gi, states, todo=None):
        j0 = gi * grp

        def qk(n, _):
            kt = k_ref[0, items[n][0], pl.ds(j0, grp)].reshape(grp * blk, LANES)
            return jnp.dot(kt, qa_ref[n], preferred_element_type=F32)

        def softmax(n, s):
            m_old = states[n][0]
            m_new = jnp.maximum(m_old, jnp.max(s, axis=0, keepdims=True))
            return m_new, jnp.exp2(m_old - m_new), jnp.exp2(s - m_new)

        def pv(n, sm):
            m_new, a, p = sm
            pb = p.astype(BF16)
            tot = a * states[n][1]
            for i in range(grp):
                tot = tot + jnp.dot(vt_ref[0, items[n][0], j0 + i], pb[i * blk:(i + 1) * blk],
                                    preferred_element_type=F32)
            return m_new, tot

        new = skewed([qk, softmax, pv], MOBA_FAR_LAGS, todo)
        return tuple(new.get(n, states[n]) for n in range(len(items)))

    sub = nq // grp
    states = lax.fori_loop(0, pair * sub, far, tuple(states))
    for extra in range(1, sub):
        later = [n for n, (_, j) in enumerate(items) if j // grp >= extra]
        states = far(pair * sub + extra - 1, states, later)

    for n, (h, j) in enumerate(items):
        acc = states[n][1]
        o_ref[0, h * hd:(h + 1) * hd, j * blk:(j + 1) * blk] = (
            acc[0:hd] / acc[hd:hd + 1]).astype(o_ref.dtype)


def _moba(qt, k_aug, vt, bias):
    bsz, nh, hd, s = qt.shape
    blk = MOBA_BLOCK
    hp = MOBA_HEADS_PER_STEP
    nq = MOBA_QBLOCKS_PER_STEP
    nblk = s // blk
    assert nh % hp == 0 and nblk % nq == 0 and nq % MOBA_FAR_GROUP == 0
    return pl.pallas_call(
        _moba_kernel,
        grid=(bsz, nh // hp, nblk // nq),
        in_specs=[pl.BlockSpec((1, hp, hd, nq * blk), lambda b, g, i: (b, g, 0, i)),
                  pl.BlockSpec((1, hp, nblk, blk, LANES), lambda b, g, i: (b, g, 0, 0, 0),
                               pipeline_mode=pl.Buffered(1)),
                  pl.BlockSpec((1, hp, nblk, MOBA_VT_ROWS, blk), lambda b, g, i: (b, g, 0, 0, 0),
                               pipeline_mode=pl.Buffered(1)),
                  pl.BlockSpec((hp, blk, 2 * blk), lambda b, g, i: (g, 0, 0),
                               pipeline_mode=pl.Buffered(1))],
        out_specs=pl.BlockSpec((1, hp * hd, nq * blk), lambda b, g, i: (b, g, i)),
        out_shape=jax.ShapeDtypeStruct((bsz, nh * hd, s), BF16),
        scratch_shapes=[pltpu.VMEM((hp, MOBA_MAX_BLOCKS, LANES), F32),
                        pltpu.VMEM((hp * nq, LANES, blk), BF16)],
        compiler_params=_cparams(("arbitrary", "arbitrary", "arbitrary")),
        name="moba",
    )(qt, k_aug, vt, bias)


def _gla_kernel(q_ref, k_ref, v_ref, gk_ref, wup_ref, bgk_ref, r_ref, gn_ref, o_ref,
                state_ref, *, nchunk):
    ch = GLA_CHUNK
    tc = nchunk * ch
    dk, dv = GLA_DK, GLA_DV

    @pl.when(pl.program_id(1) == 0)
    def _():
        state_ref[...] = jnp.zeros(state_ref.shape, F32)

    rin = lax.broadcasted_iota(I32, (tc, dk), 0) & (ch - 1)
    causal = (lax.broadcasted_iota(I32, (ch, ch), 1) <= lax.broadcasted_iota(I32, (ch, ch), 0))
    eye = (lax.broadcasted_iota(I32, (dk, dk), 0) == lax.broadcasted_iota(I32, (dk, dk), 1))
    chunks = [slice(n * ch, (n + 1) * ch) for n in range(nchunk)]
    gk = gk_ref[...]
    gk_hi = gk.astype(BF16)
    gk_lo = (gk - gk_hi.astype(F32)).astype(BF16)

    def prep(h, _):
        ks = slice(h * dk, (h + 1) * dk)
        w = wup_ref[:, ks]
        w_hi = w.astype(BF16)
        w_lo = (w - w_hi.astype(F32)).astype(BF16)
        z = (jnp.dot(gk_hi, w_hi, preferred_element_type=F32)
             + jnp.dot(gk_lo, w_hi, preferred_element_type=F32)
             + jnp.dot(gk_hi, w_lo, preferred_element_type=F32) + bgk_ref[:, ks])
        log_a = ((jnp.minimum(z, 0.0) - jnp.log(1.0 + jnp.exp(-jnp.abs(z))))
                 / GLA_GATE_NORMALIZER)
        b = log_a
        sh = 1
        while sh < ch:
            b = b + jnp.where(rin >= sh, pltpu.roll(b, sh, axis=0), 0.0)
            sh *= 2
        q = q_ref[:, ks].astype(F32) * (dk ** -0.5)
        k = k_ref[:, ks].astype(F32)
        q_g = (q * jnp.exp(b)).astype(BF16)
        k_g = (k * jnp.exp(-b)).astype(BF16)
        b3 = b.reshape(nchunk, ch, dk)
        b_last = b3[:, ch - 1:ch, :]
        k_end = (k * jnp.exp(jnp.broadcast_to(b_last, b3.shape) - b3).reshape(tc, dk)
                 ).astype(BF16)
        return q_g, k_g, k_end, jnp.exp(b_last)

    def local(h, pre):
        q_g, k_g, k_end, decay = pre
        o_intra, kv, decay_col = [], [], []
        for n, sl in enumerate(chunks):
            v_c = v_ref[sl, h * dv:(h + 1) * dv]
            att = jnp.where(causal, _nt_dot(q_g[sl], k_g[sl]), 0.0)
            o_intra.append(jnp.dot(att.astype(BF16), v_c, preferred_element_type=F32))
            kv.append(lax.dot_general(k_end[sl], v_c, (((0,), (0,)), ((), ())),
                                      preferred_element_type=F32))
            decay_col.append(jnp.sum(
                jnp.where(eye, jnp.broadcast_to(decay[n], (dk, dk)), 0.0),
                axis=1, keepdims=True))
        return q_g, o_intra, kv, decay_col

    def chain(h, loc):
        q_g, o_intra, kv, decay_col = loc
        state = state_ref[h]
        outs = []
        for n, sl in enumerate(chunks):
            outs.append(o_intra[n] + jnp.dot(q_g[sl], state.astype(BF16),
                                             preferred_element_type=F32))
            state = decay_col[n] * state + kv[n]
        state_ref[h] = state
        return jnp.concatenate(outs, axis=0)

    def finish(h, o):
        vs = slice(h * dv, (h + 1) * dv)
        o_ref[:, vs] = (_rms(o) * gn_ref[...] * r_ref[:, vs].astype(F32)).astype(o_ref.dtype)
        return None

    stages = [prep, local, chain, finish]
    vals = [None] * GLA_HEADS
    for step in range(GLA_HEADS + len(stages) - 1):
        for si, stage in enumerate(stages):
            h = step - si
            if 0 <= h < GLA_HEADS:
                vals[h] = stage(h, vals[h])


def _gla(qkb, vb, gk, wup, bgk, r_act, gn, bsz, seq, nchunk):
    t = qkb.shape[0]
    tc = nchunk * GLA_CHUNK
    nc = seq // tc
    rowblk = lambda w, off: pl.BlockSpec((tc, w), lambda b, c: (b * nc + c, off))
    full = lambda a: pl.BlockSpec(a.shape, lambda b, c: (0, 0))
    return pl.pallas_call(
        functools.partial(_gla_kernel, nchunk=nchunk),
        grid=(bsz, nc),
        in_specs=[rowblk(GLA_KEY_DIM, 0), rowblk(GLA_KEY_DIM, 1), rowblk(GLA_VALUE_DIM, 0),
                  rowblk(LANES, 0), full(wup), full(bgk), rowblk(GLA_VALUE_DIM, 0), full(gn)],
        out_specs=rowblk(GLA_VALUE_DIM, 0),
        out_shape=jax.ShapeDtypeStruct((t, GLA_VALUE_DIM), BF16),
        scratch_shapes=[pltpu.VMEM((GLA_HEADS, GLA_DK, GLA_DV), F32)],
        compiler_params=_cparams(("arbitrary", "arbitrary")),
        name="gla",
    )(qkb, qkb, vb, gk, wup, bgk, r_act, gn)


def _merge_kernel(ya_ref, yb_ref, g_ref, x_ref, wpa_ref, wpb_ref, wout_ref, g1_ref,
                  nw_ref, sc_ref, sh_ref, wr_ref, br_ref,
                  x1_ref, h2_ref, idx_ref, wts_ref):
    pa = lax.dot_general(ya_ref[0], wpa_ref[...], (((0,), (0,)), ((), ())),
                         preferred_element_type=F32)
    pb = jnp.dot(yb_ref[...], wpb_ref[...], preferred_element_type=F32)
    mixed = (g_ref[:, 0:D_MODEL].astype(F32) * pa
             + g_ref[:, D_MODEL:2 * D_MODEL].astype(F32) * pb)
    y = jnp.dot(mixed.astype(BF16), wout_ref[...], preferred_element_type=F32)
    x1 = x_ref[...] + g1_ref[0] * y
    x1_ref[...] = x1
    h2 = _rms(x1) * nw_ref[...]
    h2 = h2 * (1.0 + sc_ref[0]) + sh_ref[0]
    _store_rows(h2_ref, _pack_halves(h2))
    h_hi = h2.astype(BF16)
    h_lo = (h2 - h_hi.astype(F32)).astype(BF16)
    w = wr_ref[...]
    w_hi = w.astype(BF16)
    w_lo = (w - w_hi.astype(F32)).astype(BF16)
    logits = (_nt_dot(w_hi, h_hi) + _nt_dot(w_hi, h_lo) + _nt_dot(w_lo, h_hi)
              + br_ref[...])
    rowf = lax.broadcasted_iota(I32, logits.shape, 0).astype(F32)
    vals, idxs = [], []
    cur = logits
    for _ in range(TOP_K):
        mx = jnp.max(cur, axis=0, keepdims=True)
        first = jnp.min(jnp.where(cur == mx, rowf, float(N_EXPERTS)), axis=0, keepdims=True)
        vals.append(mx)
        idxs.append(first)
        cur = jnp.where(rowf == first, -jnp.inf, cur)
    es = [jnp.exp(v - vals[0]) for v in vals]
    tot = es[0]
    for e in es[1:]:
        tot = tot + e
    idx_ref[...] = jnp.concatenate(idxs, axis=0).astype(I32)
    wts_ref[...] = jnp.concatenate([e / tot for e in es], axis=0)


def _merge(ya, yb, gates, x2d, wpa, wpb, wout, g1, nw, sc, sh, wr_t, br, seq, tm):
    t, d = x2d.shape
    tpb = seq // tm
    row = lambda w: pl.BlockSpec((tm, w), lambda i: (i, 0))
    full = lambda a: pl.BlockSpec(a.shape, lambda i: (0,) * a.ndim)
    per_b = pl.BlockSpec((1, 1, d), lambda i: (i // tpb, 0, 0))
    colblk = pl.BlockSpec((TOP_K, tm), lambda i: (0, i))
    return pl.pallas_call(
        _merge_kernel,
        grid=(t // tm,),
        in_specs=[pl.BlockSpec((1, MOBA_WIDTH, tm), lambda i: (i // tpb, 0, i % tpb)),
                  row(GLA_VALUE_DIM), row(2 * D_MODEL), row(d),
                  full(wpa), full(wpb), full(wout), per_b,
                  pl.BlockSpec((1, d), lambda i: (0, 0)), per_b, per_b,
                  full(wr_t), full(br)],
        out_specs=[row(d), pl.BlockSpec((tm * ROW_SLABS, LANES), lambda i: (i, 0)),
                   colblk, colblk],
        out_shape=[jax.ShapeDtypeStruct((t, d), F32),
                   jax.ShapeDtypeStruct((t * ROW_SLABS, LANES), U32),
                   jax.ShapeDtypeStruct((TOP_K, t), I32), jax.ShapeDtypeStruct((TOP_K, t), F32)],
        compiler_params=_cparams(("arbitrary",)),
        name="merge",
    )(ya, yb, gates, x2d, wpa, wpb, wout, g1, nw.reshape(1, d), sc, sh, wr_t, br)


def _rank_kernel(idx_ref, rank_ref, cnt_ref, carry_ref):
    tm = idx_ref.shape[1]

    @pl.when(pl.program_id(0) == 0)
    def _():
        carry_ref[...] = jnp.zeros(carry_ref.shape, F32)

    rows = lax.broadcasted_iota(I32, (N_EXPERTS, tm), 0)
    before = (lax.broadcasted_iota(I32, (tm, tm), 0)
              < lax.broadcasted_iota(I32, (tm, tm), 1))
    upper = jnp.where(before, 1.0, 0.0).astype(BF16)
    carry = carry_ref[:, 0:1]
    ranks = []
    for k in range(TOP_K):
        onehot = idx_ref[k:k + 1, :] == rows
        onef = jnp.where(onehot, 1.0, 0.0)
        earlier = jnp.dot(onef.astype(BF16), upper, preferred_element_type=F32) + carry
        ranks.append(jnp.sum(jnp.where(onehot, earlier, 0.0), axis=0, keepdims=True))
        carry = carry + jnp.sum(onef, axis=1, keepdims=True)
    rank_ref[...] = jnp.concatenate(ranks, axis=0).astype(I32)
    total = jnp.broadcast_to(carry, carry_ref.shape)
    carry_ref[...] = total
    cnt_ref[...] = total


def _rank(idx, tm):
    t = idx.shape[1]
    return pl.pallas_call(
        _rank_kernel,
        grid=(t // tm,),
        in_specs=[pl.BlockSpec((TOP_K, tm), lambda i: (0, i))],
        out_specs=[pl.BlockSpec((TOP_K, tm), lambda i: (0, i)),
                   pl.BlockSpec((N_EXPERTS, LANES), lambda i: (0, 0))],
        out_shape=[jax.ShapeDtypeStruct((TOP_K, t), I32),
                   jax.ShapeDtypeStruct((N_EXPERTS, LANES), F32)],
        scratch_shapes=[pltpu.VMEM((N_EXPERTS, LANES), F32)],
        compiler_params=_cparams(("arbitrary",)),
        name="rank",
    )(idx)


def _dest_kernel(pstart_ref, idx_ref, rank_ref, dest_ref):
    idx = idx_ref[...]
    off = jnp.zeros(idx.shape, I32)
    for e in range(N_EXPERTS):
        off = jnp.where(idx == e, pstart_ref[e], off)
    dest_ref[...] = rank_ref[...] + off


def _dest(pstart, idx, rank, tm):
    t = idx.shape[1]
    blk = pl.BlockSpec((TOP_K, tm), lambda i: (0, i))
    return pl.pallas_call(
        _dest_kernel,
        grid=(t // tm,),
        in_specs=[pl.BlockSpec(memory_space=pltpu.SMEM), blk, blk],
        out_specs=blk,
        out_shape=jax.ShapeDtypeStruct((TOP_K, t), I32),
        compiler_params=_cparams(("arbitrary",)),
        name="dest",
    )(pstart, idx, rank)


def _scatter_kernel(dest_ref, zrow_ref, h_ref, xout_hbm, zbuf, stage, sems, zsem, *, tm):
    @pl.when(pl.program_id(0) == 0)
    def _():
        zbuf[...] = jnp.zeros(zbuf.shape, zbuf.dtype)

        def zero_rows(row0):
            line0 = pl.multiple_of(row0 * ROW_SLABS, MOE_ROWS * ROW_SLABS)
            return pltpu.make_async_copy(
                zbuf, xout_hbm.at[pl.ds(line0, MOE_ROWS * ROW_SLABS)], zsem)

        for e in range(N_EXPERTS):
            zero_rows(zrow_ref[e]).start()
        for e in range(N_EXPERTS):
            zero_rows(zrow_ref[e]).wait()

        def start_tail(j, carry):
            zero_rows(j * MOE_ROWS).start()
            return carry

        def wait_tail(j, carry):
            zero_rows(j * MOE_ROWS).wait()
            return carry

        n_all = xout_hbm.shape[0] // (MOE_ROWS * ROW_SLABS)
        lax.fori_loop(zrow_ref[N_EXPERTS], n_all, start_tail, 0)
        lax.fori_loop(zrow_ref[N_EXPERTS], n_all, wait_tail, 0)

    i = pl.program_id(0)
    slot = i % 2
    src = stage.at[slot]
    src[...] = h_ref[...]

    def issue(g, carry):
        base = pl.multiple_of(g * ROW_UNROLL, ROW_UNROLL)
        for j in range(ROW_UNROLL):
            for k in range(TOP_K):
                d = dest_ref[(base + j) * TOP_K + k]
                pltpu.make_async_copy(_row(src, base + j), _row(xout_hbm, d),
                                      sems.at[slot]).start(priority=k % 2)
        return carry

    lax.fori_loop(0, tm // ROW_UNROLL, issue, 0)

    def wait_tile(s):
        for k in range(TOP_K):
            pltpu.make_async_copy(stage.at[s], xout_hbm.at[pl.ds(0, tm * ROW_SLABS)],
                                  sems.at[s]).wait()

    @pl.when(i > 0)
    def _():
        wait_tile(1 - slot)

    @pl.when(i == pl.num_programs(0) - 1)
    def _():
        wait_tile(slot)


def _scatter(dest_tok, last_block_row, h2, n_pad, tm):
    t = h2.shape[0] // ROW_SLABS
    return pl.pallas_call(
        functools.partial(_scatter_kernel, tm=tm),
        grid=(t // tm,),
        in_specs=[pl.BlockSpec((tm * TOP_K,), lambda i: (i,), memory_space=pltpu.SMEM),
                  pl.BlockSpec(memory_space=pltpu.SMEM),
                  pl.BlockSpec((tm * ROW_SLABS, LANES), lambda i: (i, 0))],
        out_specs=pl.BlockSpec(memory_space=pl.ANY),
        out_shape=jax.ShapeDtypeStruct((n_pad * ROW_SLABS, LANES), h2.dtype),
        scratch_shapes=[pltpu.VMEM((MOE_ROWS * ROW_SLABS, LANES), h2.dtype),
                        pltpu.VMEM((2, tm * ROW_SLABS, LANES), h2.dtype),
                        pltpu.SemaphoreType.DMA((2,)), pltpu.SemaphoreType.DMA],
        compiler_params=_cparams(("arbitrary",)),
        name="scatter",
    )(dest_tok, last_block_row, h2)


def _expert_kernel(be_ref, nused_ref, x_ref, wg_ref, bg_ref, wu_ref, bu_ref, wd_ref, bd_ref,
                   o_ref, wgb_ref, wub_ref, wdb_ref):
    i = pl.program_id(0)
    prev = be_ref[jnp.maximum(i - 1, 0)]

    @pl.when(jnp.logical_or(i == 0, be_ref[i] != prev))
    def _():
        wgb_ref[...] = wg_ref[0].astype(BF16)
        wub_ref[...] = wu_ref[0].astype(BF16)
        wdb_ref[...] = wd_ref[0].astype(BF16)

    @pl.when(i < nused_ref[0])
    def _():
        xb = jnp.concatenate([v.astype(BF16) for v in _unpack_halves(_load_rows(x_ref))],
                             axis=1)
        g = jnp.minimum(jnp.dot(xb, wgb_ref[...], preferred_element_type=F32) + bg_ref[0],
                        SWIGLU_LIMIT)
        u = jnp.clip(jnp.dot(xb, wub_ref[...], preferred_element_type=F32) + bu_ref[0],
                     -SWIGLU_LIMIT, SWIGLU_LIMIT)
        act = g * _sigmoid(SWIGLU_ALPHA * g) * (u + 1.0)
        _store_rows(o_ref, _pack_halves(
            jnp.dot(act.astype(BF16), wdb_ref[...], preferred_element_type=F32) + bd_ref[0]))

    @pl.when(i >= nused_ref[0])
    def _():
        o_ref[...] = jnp.zeros(o_ref.shape, o_ref.dtype)


def _expert(block_expert, n_used, x_pad, wg, bg, wu, bu, wd, bd):
    lines = MOE_ROWS * ROW_SLABS
    _, d, f = wg.shape
    nblk = x_pad.shape[0] // lines
    wspec = lambda a, b: pl.BlockSpec((1, a, b), lambda i, be, nu: (be[i], 0, 0))
    grid_spec = pltpu.PrefetchScalarGridSpec(
        num_scalar_prefetch=2,
        grid=(nblk,),
        in_specs=[pl.BlockSpec((lines, LANES),
                               lambda i, be, nu: (jnp.minimum(i, nu[0] - 1), 0)),
                  wspec(d, f), wspec(1, f), wspec(d, f), wspec(1, f), wspec(f, d), wspec(1, d)],
        out_specs=pl.BlockSpec((lines, LANES), lambda i, be, nu: (i, 0)),
        scratch_shapes=[pltpu.VMEM((d, f), BF16), pltpu.VMEM((d, f), BF16),
                        pltpu.VMEM((f, d), BF16)])
    return pl.pallas_call(
        _expert_kernel,
        grid_spec=grid_spec,
        out_shape=jax.ShapeDtypeStruct(x_pad.shape, U32),
        compiler_params=_cparams(("arbitrary",)),
        name="expert",
    )(block_expert, n_used, x_pad, wg, bg.reshape(N_EXPERTS, 1, f), wu,
      bu.reshape(N_EXPERTS, 1, f), wd, bd.reshape(N_EXPERTS, 1, d))


def _combine_kernel(dest_ref, y_hbm, x1_ref, wts_ref, g2_ref, nf_ref, o_ref, ybuf_a, ybuf_b,
                    sems, *, tm, final, n_tiles):
    i = pl.program_id(0)
    bufs = (ybuf_a, ybuf_b)

    def issue(slot):
        for tt in range(tm):
            for k in range(TOP_K):
                d = dest_ref[tt * TOP_K + k]
                pltpu.make_async_copy(_row(y_hbm, d), _row(bufs[slot].at[k], tt),
                                      sems.at[slot]).start(priority=k % 2)

    def wait(slot):
        for k in range(TOP_K):
            pltpu.make_async_copy(y_hbm.at[pl.ds(0, tm * ROW_SLABS)], bufs[slot].at[k],
                                  sems.at[slot]).wait()

    def finish(slot):
        w = wts_ref[...]
        lo, hi = None, None
        for k in range(TOP_K):
            y_lo, y_hi = _unpack_halves(_load_rows(bufs[slot].at[k]))
            lo = w[:, k:k + 1] * y_lo + (0.0 if lo is None else lo)
            hi = w[:, k:k + 1] * y_hi + (0.0 if hi is None else hi)
        moe = jnp.concatenate([lo, hi], axis=1)
        x2 = x1_ref[...] + g2_ref[0] * moe
        o_ref[...] = _rms(x2) * nf_ref[...] if final else x2

    @pl.when(i == 0)
    def _():
        issue(0)

    for parity in range(2):
        @pl.when(jnp.logical_and(jnp.logical_and(i > 0, i < n_tiles), i % 2 == parity))
        def _(parity=parity):
            wait(1 - parity)
            issue(parity)
            finish(1 - parity)

    @pl.when(i == n_tiles)
    def _():
        wait((n_tiles - 1) % 2)
        finish((n_tiles - 1) % 2)


def _combine(dest_tok, y_pad, x1, wts_tok, g2, nf, seq, tm, final):
    t, d = x1.shape
    tpb = seq // tm
    n_tiles = t // tm
    lag = lambda i: jnp.maximum(i - 1, 0)
    return pl.pallas_call(
        functools.partial(_combine_kernel, tm=tm, final=final, n_tiles=n_tiles),
        grid=(n_tiles + 1,),
        in_specs=[pl.BlockSpec((tm * TOP_K,), lambda i: (jnp.minimum(i, n_tiles - 1),),
                               memory_space=pltpu.SMEM),
                  pl.BlockSpec(memory_space=pl.ANY),
                  pl.BlockSpec((tm, d), lambda i: (lag(i), 0)),
                  pl.BlockSpec((tm, TOP_K), lambda i: (lag(i), 0)),
                  pl.BlockSpec((1, 1, d), lambda i: (lag(i) // tpb, 0, 0)),
                  pl.BlockSpec((1, d), lambda i: (0, 0))],
        out_specs=pl.BlockSpec((tm, d), lambda i: (lag(i), 0)),
        out_shape=jax.ShapeDtypeStruct((t, d), F32),
        scratch_shapes=[pltpu.VMEM((TOP_K, tm * ROW_SLABS, LANES), y_pad.dtype),
                        pltpu.VMEM((TOP_K, tm * ROW_SLABS, LANES), y_pad.dtype),
                        pltpu.SemaphoreType.DMA((2,))],
        compiler_params=_cparams(("arbitrary",)),
        name="combine",
    )(dest_tok, y_pad, x1, wts_tok, g2, nf.reshape(1, d))


def _pick(n, cands):
    for c in cands:
        if n % c == 0:
            return c
    raise ValueError(f"no tile in {cands} divides {n}")


def kernel(x, c, rel_bias, w_ada, b_ada, norm_mix, w_in, w_gk_up, b_gk, gla_norm,
           w_proj_moba, w_proj_gla, w_out, norm_ffn, w_router, b_router,
           w_gate, b_gate, w_up, b_up, w_down, b_down, norm_final):
    bsz, seq, d = x.shape
    depth = w_ada.shape[0]
    assert d == D_MODEL and seq % MOBA_BLOCK == 0 and seq // MOBA_BLOCK <= MOBA_MAX_BLOCKS
    t = bsz * seq
    tm = _pick(seq, (512, 256))
    nchunk = _pick(seq // GLA_CHUNK, (8, 4))
    n_blk = seq // MOBA_BLOCK
    x2d = x.reshape(t, d)
    bias = _bias_tiles(rel_bias)
    per_b = lambda v: v.reshape(bsz, 1, d)

    for l in range(depth):
        mod = _ada(c, w_ada[l], b_ada[l])
        sh1, sc1, g1, sh2, sc2, g2 = [per_b(m) for m in jnp.split(mod, 6, axis=-1)]
        qt, k_aug, vt, qkb, vb, gk, r_act, gates = _inproj(
            x2d, norm_mix[l], sc1, sh1, _regroup_w_in(w_in[l]), bsz, seq, tm)
        ya = _moba(qt, k_aug, vt, bias)
        wup = jnp.pad(w_gk_up[l], ((0, LANES - GLA_GATE_RANK), (0, 0)))
        yb = _gla(qkb, vb, gk, wup, b_gk[l].reshape(1, -1), r_act,
                  gla_norm[l].reshape(1, -1), bsz, seq, nchunk)
        x1, h2, idx, wts = _merge(
            ya, yb, gates, x2d, w_proj_moba[l].astype(BF16), w_proj_gla[l].astype(BF16),
            w_out[l].astype(BF16), g1, norm_ffn[l], sc2, sh2,
            w_router[l].T, b_router[l].reshape(N_EXPERTS, 1), seq, tm)
        rank, cnt = _rank(idx, tm)
        counts = cnt[:, 0].astype(I32)
        padded = (counts + MOE_ROWS - 1) // MOE_ROWS * MOE_ROWS
        pcum = jnp.cumsum(padded)
        pstart = (pcum - padded).astype(I32)
        n_blocks = (t * TOP_K + MOE_ROWS - 1) // MOE_ROWS + N_EXPERTS
        block_row0 = jnp.arange(n_blocks, dtype=I32) * MOE_ROWS
        block_expert = jnp.minimum(
            jnp.sum((pcum[None, :] <= block_row0[:, None]).astype(I32), axis=1),
            N_EXPERTS - 1).astype(I32)
        n_used = (pcum[-1:] // MOE_ROWS).astype(I32)
        dest = _dest(pstart, idx, rank, tm)
        dest_tok = dest.T.reshape(t * TOP_K)
        tg = _pick(seq, (256,))
        zero_info = jnp.concatenate([jnp.maximum(pcum - MOE_ROWS, 0).astype(I32), n_used])
        x_pad = _scatter(dest_tok, zero_info, h2, n_blocks * MOE_ROWS, tg)
        y_pad = _expert(block_expert, n_used, x_pad, w_gate[l], b_gate[l], w_up[l], b_up[l],
                        w_down[l], b_down[l])
        x2d = _combine(dest_tok, y_pad, x1, wts.T, g2, norm_final, seq, tg, l == depth - 1)
    return x2d.reshape(bsz, seq, d)
```

```python
import functools
import math

import numpy as np
import jax
import jax.numpy as jnp
from jax import lax
from jax.experimental import pallas as pl
from jax.experimental.pallas import tpu as pltpu

F32 = jnp.float32
BF16 = jnp.bfloat16
I32 = jnp.int32
HIGHEST = lax.Precision.HIGHEST

D_MODEL = 1024
MOBA_HEADS = 8
MOBA_HEAD_DIM = 64
MOBA_WIDTH = MOBA_HEADS * MOBA_HEAD_DIM
MOBA_BLOCK = 256
MOBA_TOPK = 3
MOBA_MAX_BLOCKS = 32
REL_BUCKETS = 32
REL_MAX_DIST = 128
GLA_HEADS = 4
GLA_KEY_DIM = D_MODEL // 2
GLA_VALUE_DIM = D_MODEL
GLA_DK = GLA_KEY_DIM // GLA_HEADS
GLA_DV = GLA_VALUE_DIM // GLA_HEADS
GLA_GATE_RANK = 16
GLA_GATE_NORMALIZER = 16.0
GLA_CHUNK = 64
N_EXPERTS = 32
TOP_K = 4
D_FF = D_MODEL
SWIGLU_ALPHA = 1.702
SWIGLU_LIMIT = 7.0
MOE_ROWS = 512
ROW_UNROLL = 8
EPS = 1e-6
LANES = 128
SUBLANES = 8
BF16_SUBLANES = 16
ADA_COLS = 768
NEG_BIG = -1e30
LOG2E = math.log2(math.e)
VMEM_LIMIT = 56 * 1024 * 1024


def _cparams(sem, vmem=None):
    return pltpu.CompilerParams(dimension_semantics=sem,
                                vmem_limit_bytes=vmem or VMEM_LIMIT)


def _nt_dot(a, b, **kw):
    return lax.dot_general(a, b, (((1,), (1,)), ((), ())),
                           preferred_element_type=F32, **kw)


def _rms(x):
    return x * lax.rsqrt(jnp.mean(x * x, axis=-1, keepdims=True) + EPS)


def _sigmoid(x):
    return 1.0 / (1.0 + jnp.exp(-x))


U32 = jnp.uint32
_HI16 = 0xFFFF0000


def _pack_halves(x):
    n = x.shape[1] // 2
    lo = pltpu.bitcast(x[:, :n].astype(BF16).astype(F32), U32)
    hi = pltpu.bitcast(x[:, n:].astype(BF16).astype(F32), U32)
    return (hi & U32(_HI16)) | (lo >> 16)


def _unpack_halves(w):
    return (pltpu.bitcast(w << 16, F32), pltpu.bitcast(w & U32(_HI16), F32))


ROW_SLABS = D_MODEL // 2 // LANES


def _store_rows(ref, words):
    m = words.shape[0]
    for c in range(ROW_SLABS):
        ref[pl.ds(c, m, stride=ROW_SLABS), :] = words[:, c * LANES:(c + 1) * LANES]


def _load_rows(ref):
    m = ref.shape[0] // ROW_SLABS
    return jnp.concatenate(
        [ref[pl.ds(c, m, stride=ROW_SLABS), :] for c in range(ROW_SLABS)], axis=1)


def _row(ref, i):
    return ref.at[pl.ds(pl.multiple_of(i * ROW_SLABS, ROW_SLABS), ROW_SLABS)]


def _ada_kernel(c_ref, w_ref, b_ref, o_ref):
    c = c_ref[...]
    s = c * _sigmoid(c)
    o_ref[...] = jnp.dot(s, w_ref[...], precision=HIGHEST,
                         preferred_element_type=F32) + b_ref[...]


def _ada(c, w, b):
    bsz, d = c.shape
    n = w.shape[1]
    rows = -(-bsz // SUBLANES) * SUBLANES
    cp = jnp.zeros((rows, d), F32).at[:bsz].set(c)
    tn = ADA_COLS
    out = pl.pallas_call(
        _ada_kernel,
        grid=(n // tn,),
        in_specs=[pl.BlockSpec((rows, d), lambda j: (0, 0)),
                  pl.BlockSpec((d, tn), lambda j: (0, j)),
                  pl.BlockSpec((1, tn), lambda j: (0, j))],
        out_specs=pl.BlockSpec((rows, tn), lambda j: (0, j)),
        out_shape=jax.ShapeDtypeStruct((rows, n), F32),
        compiler_params=_cparams(("arbitrary",)),
        name="ada",
    )(cp, w, b.reshape(1, n))
    return out[:bsz]


_OFF_QA = 0
_OFF_KA = _OFF_QA + MOBA_WIDTH
_OFF_VA = _OFF_KA + MOBA_WIDTH
_OFF_QKB = _OFF_VA + MOBA_WIDTH
_OFF_VB = _OFF_QKB + 2 * GLA_KEY_DIM
_OFF_GK = _OFF_VB + GLA_VALUE_DIM
_OFF_R = _OFF_GK + LANES
_OFF_G = _OFF_R + GLA_VALUE_DIM
_W_CAT = _OFF_G + 2 * D_MODEL


def _regroup_w_in(w):
    o_gk = 3 * MOBA_WIDTH + 2 * GLA_KEY_DIM + GLA_VALUE_DIM
    gk = jnp.pad(w[:, o_gk:o_gk + GLA_GATE_RANK], ((0, 0), (0, LANES - GLA_GATE_RANK)))
    return jnp.concatenate([w[:, :o_gk], gk, w[:, o_gk + GLA_GATE_RANK:]],
                           axis=1).astype(BF16)


def _inproj_kernel(x_ref, nw_ref, sc_ref, sh_ref, w_ref,
                   qt_ref, ka_ref, vt_ref, qkb_ref, vb_ref, gk_ref, r_ref, g_ref, *, tpb):
    tm = x_ref.shape[0]
    hd = MOBA_HEAD_DIM
    nbt = tm // MOBA_BLOCK
    h = _rms(x_ref[...]) * nw_ref[...]
    h = h * (1.0 + sc_ref[0]) + sh_ref[0]
    hb = h.astype(BF16)

    def mm(a, b):
        return jnp.dot(hb, w_ref[:, a:b], preferred_element_type=F32)

    q_t = (mm(_OFF_QA, _OFF_KA) * (hd ** -0.5 * LOG2E)).T
    v_t = mm(_OFF_VA, _OFF_QKB).T
    k_all = mm(_OFF_KA, _OFF_VA)
    blk0 = (pl.program_id(0) % tpb) * nbt
    lane = lax.broadcasted_iota(I32, (MOBA_BLOCK, LANES), 1)
    ones_rows = jnp.where(
        lax.broadcasted_iota(I32, (MOBA_VT_ROWS - hd, MOBA_BLOCK), 0) == 0, 1.0, 0.0)
    heads_per_tile = LANES // hd
    for hh in range(MOBA_HEADS):
        qt_ref[0, hh] = q_t[hh * hd:(hh + 1) * hd].astype(BF16)
        tile = hh // heads_per_tile
        k_h = k_all[:, tile * LANES:(tile + 1) * LANES]
        if hh % heads_per_tile:
            k_h = pltpu.roll(k_h, LANES - (hh % heads_per_tile) * hd, axis=1)
        for j in range(nbt):
            rows = slice(j * MOBA_BLOCK, (j + 1) * MOBA_BLOCK)
            onehot = jnp.where(lane == hd + blk0 + j, 1.0, 0.0)
            ka_ref[0, hh, j] = jnp.where(lane < hd, k_h[rows], onehot).astype(BF16)
            vt_ref[0, hh, j, 0:hd, :] = v_t[hh * hd:(hh + 1) * hd, rows].astype(BF16)
            vt_ref[0, hh, j, hd:MOBA_VT_ROWS, :] = ones_rows.astype(BF16)
    qkb_ref[...] = mm(_OFF_QKB, _OFF_VB).astype(BF16)
    vb_ref[...] = mm(_OFF_VB, _OFF_GK).astype(BF16)
    gk_ref[...] = mm(_OFF_GK, _OFF_R)
    r = mm(_OFF_R, _OFF_G)
    r_ref[...] = (r * _sigmoid(r)).astype(BF16)
    g_ref[...] = _sigmoid(mm(_OFF_G, _W_CAT)).astype(BF16)


def _inproj(x2d, nw, sc, sh, w_cat, bsz, seq, tm):
    t, d = x2d.shape
    tpb = seq // tm
    nbt = tm // MOBA_BLOCK
    nh, hd = MOBA_HEADS, MOBA_HEAD_DIM
    row = lambda w: pl.BlockSpec((tm, w), lambda i: (i, 0))
    per_b = pl.BlockSpec((1, 1, d), lambda i: (i // tpb, 0, 0))
    rows_out = [(2 * GLA_KEY_DIM, BF16), (GLA_VALUE_DIM, BF16), (LANES, F32),
                (GLA_VALUE_DIM, BF16), (2 * D_MODEL, BF16)]
    return pl.pallas_call(
        functools.partial(_inproj_kernel, tpb=tpb),
        grid=(t // tm,),
        in_specs=[row(d), pl.BlockSpec((1, d), lambda i: (0, 0)), per_b, per_b,
                  pl.BlockSpec((d, _W_CAT), lambda i: (0, 0), pipeline_mode=pl.Buffered(1))],
        out_specs=[pl.BlockSpec((1, nh, hd, tm), lambda i: (i // tpb, 0, 0, i % tpb)),
                   pl.BlockSpec((1, nh, nbt, MOBA_BLOCK, LANES),
                                lambda i: (i // tpb, 0, i % tpb, 0, 0)),
                   pl.BlockSpec((1, nh, nbt, MOBA_VT_ROWS, MOBA_BLOCK),
                                lambda i: (i // tpb, 0, i % tpb, 0, 0))]
                  + [row(w) for w, _ in rows_out],
        out_shape=[jax.ShapeDtypeStruct((bsz, nh, hd, seq), BF16),
                   jax.ShapeDtypeStruct((bsz, nh, seq // MOBA_BLOCK, MOBA_BLOCK, LANES), BF16),
                   jax.ShapeDtypeStruct((bsz, nh, seq // MOBA_BLOCK, MOBA_VT_ROWS, MOBA_BLOCK),
                                        BF16)]
                  + [jax.ShapeDtypeStruct((t, w), dt) for w, dt in rows_out],
        compiler_params=_cparams(("arbitrary",)),
        name="inproj",
    )(x2d, nw.reshape(1, d), sc, sh, w_cat)


def _t5_bucket_np(n):
    n = np.maximum(n, 0)
    max_exact = REL_BUCKETS // 2
    nf = np.maximum(n, max_exact).astype(np.float32)
    large = max_exact + (np.log(nf / max_exact) / math.log(REL_MAX_DIST / max_exact)
                         * (REL_BUCKETS - max_exact)).astype(np.int32)
    large = np.minimum(large, REL_BUCKETS - 1)
    return np.where(n < max_exact, n, large).astype(np.int32)


def _bucket_table():
    kj = np.arange(MOBA_BLOCK)[:, None]
    qi = np.arange(2 * MOBA_BLOCK)[None, :] % MOBA_BLOCK
    prev = np.arange(2 * MOBA_BLOCK)[None, :] < MOBA_BLOCK
    bucket = _t5_bucket_np(qi - kj + np.where(prev, MOBA_BLOCK, 0))
    return np.where(prev | (kj <= qi), bucket, -1).astype(np.int32)


def _bias_kernel(rb_ref, bucket_ref, o_ref):
    h = pl.program_id(0)
    bk = bucket_ref[...]
    far = rb_ref[(REL_BUCKETS - 1) * MOBA_HEADS + h]
    acc = jnp.zeros(bk.shape, F32)
    for b in range(REL_BUCKETS):
        acc = jnp.where(bk == b, rb_ref[b * MOBA_HEADS + h] - far, acc)
    o_ref[0] = jnp.where(bk < 0, NEG_BIG, acc * LOG2E)


def _bias_tiles(rel_bias):
    bucket = jnp.asarray(_bucket_table())
    return pl.pallas_call(
        _bias_kernel,
        grid=(MOBA_HEADS,),
        in_specs=[pl.BlockSpec(memory_space=pltpu.SMEM),
                  pl.BlockSpec(bucket.shape, lambda h: (0, 0))],
        out_specs=pl.BlockSpec((1,) + bucket.shape, lambda h: (h, 0, 0)),
        out_shape=jax.ShapeDtypeStruct((MOBA_HEADS,) + bucket.shape, F32),
        compiler_params=_cparams(("arbitrary",)),
        name="bias",
    )(rel_bias.reshape(-1), bucket)


MOBA_HEADS_PER_STEP = 8
MOBA_FAR_GROUP = 2
MOBA_QBLOCKS_PER_STEP = 4
MOBA_FAR_LAGS = (0, 3, 6)
MOBA_NEAR_LAGS = (0, 3, 6, 9)
MOBA_VT_ROWS = MOBA_HEAD_DIM + BF16_SUBLANES


def _moba_kernel(qt_ref, k_ref, vt_ref, bias_ref, o_ref, kmean_ref, qa_ref):
    blk = MOBA_BLOCK
    nb = MOBA_MAX_BLOCKS
    hd = MOBA_HEAD_DIM
    hp = MOBA_HEADS_PER_STEP
    grp = MOBA_FAR_GROUP
    nq = MOBA_QBLOCKS_PER_STEP
    pair = pl.program_id(2)
    neg = -jnp.inf
    items = [(h, j) for j in range(nq) for h in range(hp)]

    @pl.when(pair == 0)
    def _():
        kmean_ref[...] = jnp.zeros(kmean_ref.shape, F32)

    for h in range(hp):
        for j in range(nq):
            kmean_ref[h, pl.ds(pair * nq + j, 1), :] = jnp.mean(
                k_ref[0, h, pair * nq + j].astype(F32), axis=0, keepdims=True)

    row = lax.broadcasted_iota(I32, (nb, blk), 0)
    rowf = row.astype(F32)
    pad = jnp.zeros((LANES - hd, blk), BF16)
    pad_hi = jnp.zeros((LANES - hd - nb, blk), BF16)

    def skewed(stages, lags, todo=None):
        todo = list(range(len(items))) if todo is None else todo
        vals = {}
        for step in range(len(todo) + lags[-1]):
            for stage, lag in zip(stages, lags):
                pos = step - lag
                if 0 <= pos < len(todo):
                    vals[todo[pos]] = stage(todo[pos], vals.get(todo[pos]))
        return vals

    def select(n, _):
        h, j = items[n]
        qi = pair * nq + j
        qt = qt_ref[0, h, :, j * blk:(j + 1) * blk]
        km = kmean_ref[h, :, 0:hd]
        km_hi = km.astype(BF16)
        km_lo = (km - km_hi.astype(F32)).astype(BF16)
        gate = (jnp.dot(km_hi, qt, preferred_element_type=F32)
                + jnp.dot(km_lo, qt, preferred_element_type=F32))
        g = jnp.where(row < qi, gate, neg)
        sel = jnp.zeros((nb, blk), F32)
        for _ in range(MOBA_TOPK):
            mx = jnp.max(g, axis=0, keepdims=True)
            first = jnp.min(jnp.where(g == mx, rowf, float(nb)), axis=0, keepdims=True)
            pick = rowf == jnp.where(mx > neg, first, -1.0)
            sel = jnp.where(pick, 1.0, sel)
            g = jnp.where(pick, neg, g)
        mask_prev = jnp.where(sel > 0.0, jnp.where(row == qi - 1, 0.0, NEG_BIG), NEG_BIG)
        mask_far = jnp.where(sel > 0.0, jnp.where(row < qi - 1, 0.0, NEG_BIG), NEG_BIG)
        qa_ref[n] = jnp.concatenate([qt, mask_far.astype(BF16), pad_hi], axis=0)
        return (jnp.concatenate([qt, pad], axis=0),
                jnp.concatenate([qt, mask_prev.astype(BF16), pad_hi], axis=0))

    def own_prev(n):
        h, j = items[n]
        qi = pair * nq + j
        return h, qi, jnp.maximum(qi - 1, 0)

    def near_scores(n, qa):
        h, qi, prev_j = own_prev(n)
        qa_own, qa_prev = qa
        s_own = jnp.dot(k_ref[0, h, qi], qa_own, preferred_element_type=F32)
        s_prev = jnp.dot(k_ref[0, h, prev_j], qa_prev, preferred_element_type=F32)
        return s_own, s_prev

    def near_softmax(n, ss):
        h = items[n][0]
        s_own, s_prev = ss
        s = jnp.concatenate([s_own + bias_ref[h, :, blk:2 * blk],
                             s_prev + bias_ref[h, :, 0:blk]], axis=0)
        m0 = jnp.max(s, axis=0, keepdims=True)
        return m0, jnp.exp2(s - m0)

    def near_pv(n, mp):
        h, qi, prev_j = own_prev(n)
        m0, p = mp
        pb = p.astype(BF16)
        acc = (jnp.dot(vt_ref[0, h, qi], pb[0:blk], preferred_element_type=F32)
               + jnp.dot(vt_ref[0, h, prev_j], pb[blk:2 * blk], preferred_element_type=F32))
        return m0, acc

    near = skewed([select, near_scores, near_softmax, near_pv], MOBA_NEAR_LAGS)
    states = tuple(near[n] for n in range(len(items)))

    def far(gi, states, todo=None):
        j0 = gi * grp

        def qk(n, _):
            kt = k_ref[0, items[n][0], pl.ds(j0, grp)].reshape(grp * blk, LANES)
            return jnp.dot(kt, qa_ref[n], preferred_element_type=F32)

        def softmax(n, s):
            m_old = states[n][0]
            m_new = jnp.maximum(m_old, jnp.max(s, axis=0, keepdims=True))
            return m_new, jnp.exp2(m_old - m_new), jnp.exp2(s - m_new)

        def pv(n, sm):
            m_new, a, p = sm
            pb = p.astype(BF16)
            tot = a * states[n][1]
            for i in range(grp):
                tot = tot + jnp.dot(vt_ref[0, items[n][0], j0 + i], pb[i * blk:(i + 1) * blk],
                                    preferred_element_type=F32)
            return m_new, tot

        new = skewed([qk, softmax, pv], MOBA_FAR_LAGS, todo)
        return tuple(new.get(n, states[n]) for n in range(len(items)))

    sub = nq // grp
    states = lax.fori_loop(0, pair * sub, far, tuple(states))
    for extra in range(1, sub):
        later = [n for n, (_, j) in enumerate(items) if j // grp >= extra]
        states = far(pair * sub + extra - 1, states, later)

    for n, (h, j) in enumerate(items):
        acc = states[n][1]
        o_ref[0, h * hd:(h + 1) * hd, j * blk:(j + 1) * blk] = (
            acc[0:hd] / acc[hd:hd + 1]).astype(o_ref.dtype)


def _moba(qt, k_aug, vt, bias):
    bsz, nh, hd, s = qt.shape
    blk = MOBA_BLOCK
    hp = MOBA_HEADS_PER_STEP
    nq = MOBA_QBLOCKS_PER_STEP
    nblk = s // blk
    assert nh % hp == 0 and nblk % nq == 0 and nq % MOBA_FAR_GROUP == 0
    return pl.pallas_call(
        _moba_kernel,
        grid=(bsz, nh // hp, nblk // nq),
        in_specs=[pl.BlockSpec((1, hp, hd, nq * blk), lambda b, g, i: (b, g, 0, i)),
                  pl.BlockSpec((1, hp, nblk, blk, LANES), lambda b, g, i: (b, g, 0, 0, 0),
                               pipeline_mode=pl.Buffered(1)),
                  pl.BlockSpec((1, hp, nblk, MOBA_VT_ROWS, blk), lambda b, g, i: (b, g, 0, 0, 0),
                               pipeline_mode=pl.Buffered(1)),
                  pl.BlockSpec((hp, blk, 2 * blk), lambda b, g, i: (g, 0, 0),
                               pipeline_mode=pl.Buffered(1))],
        out_specs=pl.BlockSpec((1, hp * hd, nq * blk), lambda b, g, i: (b, g, i)),
        out_shape=jax.ShapeDtypeStruct((bsz, nh * hd, s), BF16),
        scratch_shapes=[pltpu.VMEM((hp, MOBA_MAX_BLOCKS, LANES), F32),
                        pltpu.VMEM((hp * nq, LANES, blk), BF16)],
        compiler_params=_cparams(("arbitrary", "arbitrary", "arbitrary")),
        name="moba",
    )(qt, k_aug, vt, bias)


def _gla_kernel(q_ref, k_ref, v_ref, gk_ref, wup_ref, bgk_ref, r_ref, gn_ref, o_ref,
                state_ref, *, nchunk):
    ch = GLA_CHUNK
    tc = nchunk * ch
    dk, dv = GLA_DK, GLA_DV

    @pl.when(pl.program_id(1) == 0)
    def _():
        state_ref[...] = jnp.zeros(state_ref.shape, F32)

    rin = lax.broadcasted_iota(I32, (tc, dk), 0) & (ch - 1)
    causal = (lax.broadcasted_iota(I32, (ch, ch), 1) <= lax.broadcasted_iota(I32, (ch, ch), 0))
    eye = (lax.broadcasted_iota(I32, (dk, dk), 0) == lax.broadcasted_iota(I32, (dk, dk), 1))
    chunks = [slice(n * ch, (n + 1) * ch) for n in range(nchunk)]
    gk = gk_ref[...]
    gk_hi = gk.astype(BF16)
    gk_lo = (gk - gk_hi.astype(F32)).astype(BF16)

    def prep(h, _):
        ks = slice(h * dk, (h + 1) * dk)
        w = wup_ref[:, ks]
        w_hi = w.astype(BF16)
        w_lo = (w - w_hi.astype(F32)).astype(BF16)
        z = (jnp.dot(gk_hi, w_hi, preferred_element_type=F32)
             + jnp.dot(gk_lo, w_hi, preferred_element_type=F32)
             + jnp.dot(gk_hi, w_lo, preferred_element_type=F32) + bgk_ref[:, ks])
        log_a = ((jnp.minimum(z, 0.0) - jnp.log(1.0 + jnp.exp(-jnp.abs(z))))
                 / GLA_GATE_NORMALIZER)
        b = log_a
        sh = 1
        while sh < ch:
            b = b + jnp.where(rin >= sh, pltpu.roll(b, sh, axis=0), 0.0)
            sh *= 2
        q = q_ref[:, ks].astype(F32) * (dk ** -0.5)
        k = k_ref[:, ks].astype(F32)
        q_g = (q * jnp.exp(b)).astype(BF16)
        k_g = (k * jnp.exp(-b)).astype(BF16)
        b3 = b.reshape(nchunk, ch, dk)
        b_last = b3[:, ch - 1:ch, :]
        k_end = (k * jnp.exp(jnp.broadcast_to(b_last, b3.shape) - b3).reshape(tc, dk)
                 ).astype(BF16)
        return q_g, k_g, k_end, jnp.exp(b_last)

    def local(h, pre):
        q_g, k_g, k_end, decay = pre
        o_intra, kv, decay_col = [], [], []
        for n, sl in enumerate(chunks):
            v_c = v_ref[sl, h * dv:(h + 1) * dv]
            att = jnp.where(causal, _nt_dot(q_g[sl], k_g[sl]), 0.0)
            o_intra.append(jnp.dot(att.astype(BF16), v_c, preferred_element_type=F32))
            kv.append(lax.dot_general(k_end[sl], v_c, (((0,), (0,)), ((), ())),
                                      preferred_element_type=F32))
            decay_col.append(jnp.sum(
                jnp.where(eye, jnp.broadcast_to(decay[n], (dk, dk)), 0.0),
                axis=1, keepdims=True))
        return q_g, o_intra, kv, decay_col

    def chain(h, loc):
        q_g, o_intra, kv, decay_col = loc
        state = state_ref[h]
        outs = []
        for n, sl in enumerate(chunks):
            outs.append(o_intra[n] + jnp.dot(q_g[sl], state.astype(BF16),
                                             preferred_element_type=F32))
            state = decay_col[n] * state + kv[n]
        state_ref[h] = state
        return jnp.concatenate(outs, axis=0)

    def finish(h, o):
        vs = slice(h * dv, (h + 1) * dv)
        o_ref[:, vs] = (_rms(o) * gn_ref[...] * r_ref[:, vs].astype(F32)).astype(o_ref.dtype)
        return None

    stages = [prep, local, chain, finish]
    vals = [None] * GLA_HEADS
    for step in range(GLA_HEADS + len(stages) - 1):
        for si, stage in enumerate(stages):
            h = step - si
            if 0 <= h < GLA_HEADS:
                vals[h] = stage(h, vals[h])


def _gla(qkb, vb, gk, wup, bgk, r_act, gn, bsz, seq, nchunk):
    t = qkb.shape[0]
    tc = nchunk * GLA_CHUNK
    nc = seq // tc
    rowblk = lambda w, off: pl.BlockSpec((tc, w), lambda b, c: (b * nc + c, off))
    full = lambda a: pl.BlockSpec(a.shape, lambda b, c: (0, 0))
    return pl.pallas_call(
        functools.partial(_gla_kernel, nchunk=nchunk),
        grid=(bsz, nc),
        in_specs=[rowblk(GLA_KEY_DIM, 0), rowblk(GLA_KEY_DIM, 1), rowblk(GLA_VALUE_DIM, 0),
                  rowblk(LANES, 0), full(wup), full(bgk), rowblk(GLA_VALUE_DIM, 0), full(gn)],
        out_specs=rowblk(GLA_VALUE_DIM, 0),
        out_shape=jax.ShapeDtypeStruct((t, GLA_VALUE_DIM), BF16),
        scratch_shapes=[pltpu.VMEM((GLA_HEADS, GLA_DK, GLA_DV), F32)],
        compiler_params=_cparams(("arbitrary", "arbitrary")),
        name="gla",
    )(qkb, qkb, vb, gk, wup, bgk, r_act, gn)


def _merge_kernel(ya_ref, yb_ref, g_ref, x_ref, wpa_ref, wpb_ref, wout_ref, g1_ref,
                  nw_ref, sc_ref, sh_ref, wr_ref, br_ref,
                  x1_ref, h2_ref, idx_ref, wts_ref):
    pa = lax.dot_general(ya_ref[0], wpa_ref[...], (((0,), (0,)), ((), ())),
                         preferred_element_type=F32)
    pb = jnp.dot(yb_ref[...], wpb_ref[...], preferred_element_type=F32)
    mixed = (g_ref[:, 0:D_MODEL].astype(F32) * pa
             + g_ref[:, D_MODEL:2 * D_MODEL].astype(F32) * pb)
    y = jnp.dot(mixed.astype(BF16), wout_ref[...], preferred_element_type=F32)
    x1 = x_ref[...] + g1_ref[0] * y
    x1_ref[...] = x1
    h2 = _rms(x1) * nw_ref[...]
    h2 = h2 * (1.0 + sc_ref[0]) + sh_ref[0]
    _store_rows(h2_ref, _pack_halves(h2))
    h_hi = h2.astype(BF16)
    h_lo = (h2 - h_hi.astype(F32)).astype(BF16)
    w = wr_ref[...]
    w_hi = w.astype(BF16)
    w_lo = (w - w_hi.astype(F32)).astype(BF16)
    logits = (_nt_dot(w_hi, h_hi) + _nt_dot(w_hi, h_lo) + _nt_dot(w_lo, h_hi)
              + br_ref[...])
    rowf = lax.broadcasted_iota(I32, logits.shape, 0).astype(F32)
    vals, idxs = [], []
    cur = logits
    for _ in range(TOP_K):
        mx = jnp.max(cur, axis=0, keepdims=True)
        first = jnp.min(jnp.where(cur == mx, rowf, float(N_EXPERTS)), axis=0, keepdims=True)
        vals.append(mx)
        idxs.append(first)
        cur = jnp.where(rowf == first, -jnp.inf, cur)
    es = [jnp.exp(v - vals[0]) for v in vals]
    tot = es[0]
    for e in es[1:]:
        tot = tot + e
    idx_ref[...] = jnp.concatenate(idxs, axis=0).astype(I32)
    wts_ref[...] = jnp.concatenate([e / tot for e in es], axis=0)


def _merge(ya, yb, gates, x2d, wpa, wpb, wout, g1, nw, sc, sh, wr_t, br, seq, tm):
    t, d = x2d.shape
    tpb = seq // tm
    row = lambda w: pl.BlockSpec((tm, w), lambda i: (i, 0))
    full = lambda a: pl.BlockSpec(a.shape, lambda i: (0,) * a.ndim)
    per_b = pl.BlockSpec((1, 1, d), lambda i: (i // tpb, 0, 0))
    colblk = pl.BlockSpec((TOP_K, tm), lambda i: (0, i))
    return pl.pallas_call(
        _merge_kernel,
        grid=(t // tm,),
        in_specs=[pl.BlockSpec((1, MOBA_WIDTH, tm), lambda i: (i // tpb, 0, i % tpb)),
                  row(GLA_VALUE_DIM), row(2 * D_MODEL), row(d),
                  full(wpa), full(wpb), full(wout), per_b,
                  pl.BlockSpec((1, d), lambda i: (0, 0)), per_b, per_b,
                  full(wr_t), full(br)],
        out_specs=[row(d), pl.BlockSpec((tm * ROW_SLABS, LANES), lambda i: (i, 0)),
                   colblk, colblk],
        out_shape=[jax.ShapeDtypeStruct((t, d), F32),
                   jax.ShapeDtypeStruct((t * ROW_SLABS, LANES), U32),
                   jax.ShapeDtypeStruct((TOP_K, t), I32), jax.ShapeDtypeStruct((TOP_K, t), F32)],
        compiler_params=_cparams(("arbitrary",)),
        name="merge",
    )(ya, yb, gates, x2d, wpa, wpb, wout, g1, nw.reshape(1, d), sc, sh, wr_t, br)


def _rank_kernel(idx_ref, rank_ref, cnt_ref, carry_ref):
    tm = idx_ref.shape[1]

    @pl.when(pl.program_id(0) == 0)
    def _():
        carry_ref[...] = jnp.zeros(carry_ref.shape, F32)

    rows = lax.broadcasted_iota(I32, (N_EXPERTS, tm), 0)
    before = (lax.broadcasted_iota(I32, (tm, tm), 0)
              < lax.broadcasted_iota(I32, (tm, tm), 1))
    upper = jnp.where(before, 1.0, 0.0).astype(BF16)
    carry = carry_ref[:, 0:1]
    ranks = []
    for k in range(TOP_K):
        onehot = idx_ref[k:k + 1, :] == rows
        onef = jnp.where(onehot, 1.0, 0.0)
        earlier = jnp.dot(onef.astype(BF16), upper, preferred_element_type=F32) + carry
        ranks.append(jnp.sum(jnp.where(onehot, earlier, 0.0), axis=0, keepdims=True))
        carry = carry + jnp.sum(onef, axis=1, keepdims=True)
    rank_ref[...] = jnp.concatenate(ranks, axis=0).astype(I32)
    total = jnp.broadcast_to(carry, carry_ref.shape)
    carry_ref[...] = total
    cnt_ref[...] = total


def _rank(idx, tm):
    t = idx.shape[1]
    return pl.pallas_call(
        _rank_kernel,
        grid=(t // tm,),
        in_specs=[pl.BlockSpec((TOP_K, tm), lambda i: (0, i))],
        out_specs=[pl.BlockSpec((TOP_K, tm), lambda i: (0, i)),
                   pl.BlockSpec((N_EXPERTS, LANES), lambda i: (0, 0))],
        out_shape=[jax.ShapeDtypeStruct((TOP_K, t), I32),
                   jax.ShapeDtypeStruct((N_EXPERTS, LANES), F32)],
        scratch_shapes=[pltpu.VMEM((N_EXPERTS, LANES), F32)],
        compiler_params=_cparams(("arbitrary",)),
        name="rank",
    )(idx)


def _dest_kernel(pstart_ref, idx_ref, rank_ref, dest_ref):
    idx = idx_ref[...]
    off = jnp.zeros(idx.shape, I32)
    for e in range(N_EXPERTS):
        off = jnp.where(idx == e, pstart_ref[e], off)
    dest_ref[...] = rank_ref[...] + off


def _dest(pstart, idx, rank, tm):
    t = idx.shape[1]
    blk = pl.BlockSpec((TOP_K, tm), lambda i: (0, i))
    return pl.pallas_call(
        _dest_kernel,
        grid=(t // tm,),
        in_specs=[pl.BlockSpec(memory_space=pltpu.SMEM), blk, blk],
        out_specs=blk,
        out_shape=jax.ShapeDtypeStruct((TOP_K, t), I32),
        compiler_params=_cparams(("arbitrary",)),
        name="dest",
    )(pstart, idx, rank)


def _scatter_kernel(dest_ref, zrow_ref, h_ref, xout_hbm, zbuf, stage, sems, zsem, *, tm):
    @pl.when(pl.program_id(0) == 0)
    def _():
        zbuf[...] = jnp.zeros(zbuf.shape, zbuf.dtype)

        def zero_rows(row0):
            line0 = pl.multiple_of(row0 * ROW_SLABS, MOE_ROWS * ROW_SLABS)
            return pltpu.make_async_copy(
                zbuf, xout_hbm.at[pl.ds(line0, MOE_ROWS * ROW_SLABS)], zsem)

        for e in range(N_EXPERTS):
            zero_rows(zrow_ref[e]).start()
        for e in range(N_EXPERTS):
            zero_rows(zrow_ref[e]).wait()

        def start_tail(j, carry):
            zero_rows(j * MOE_ROWS).start()
            return carry

        def wait_tail(j, carry):
            zero_rows(j * MOE_ROWS).wait()
            return carry

        n_all = xout_hbm.shape[0] // (MOE_ROWS * ROW_SLABS)
        lax.fori_loop(zrow_ref[N_EXPERTS], n_all, start_tail, 0)
        lax.fori_loop(zrow_ref[N_EXPERTS], n_all, wait_tail, 0)

    i = pl.program_id(0)
    slot = i % 2
    src = stage.at[slot]
    src[...] = h_ref[...]

    def issue(g, carry):
        base = pl.multiple_of(g * ROW_UNROLL, ROW_UNROLL)
        for j in range(ROW_UNROLL):
            for k in range(TOP_K):
                d = dest_ref[(base + j) * TOP_K + k]
                pltpu.make_async_copy(_row(src, base + j), _row(xout_hbm, d),
                                      sems.at[slot]).start(priority=k % 2)
        return carry

    lax.fori_loop(0, tm // ROW_UNROLL, issue, 0)

    def wait_tile(s):
        for k in range(TOP_K):
            pltpu.make_async_copy(stage.at[s], xout_hbm.at[pl.ds(0, tm * ROW_SLABS)],
                                  sems.at[s]).wait()

    @pl.when(i > 0)
    def _():
        wait_tile(1 - slot)

    @pl.when(i == pl.num_programs(0) - 1)
    def _():
        wait_tile(slot)


def _scatter(dest_tok, last_block_row, h2, n_pad, tm):
    t = h2.shape[0] // ROW_SLABS
    return pl.pallas_call(
        functools.partial(_scatter_kernel, tm=tm),
        grid=(t // tm,),
        in_specs=[pl.BlockSpec((tm * TOP_K,), lambda i: (i,), memory_space=pltpu.SMEM),
                  pl.BlockSpec(memory_space=pltpu.SMEM),
                  pl.BlockSpec((tm * ROW_SLABS, LANES), lambda i: (i, 0))],
        out_specs=pl.BlockSpec(memory_space=pl.ANY),
        out_shape=jax.ShapeDtypeStruct((n_pad * ROW_SLABS, LANES), h2.dtype),
        scratch_shapes=[pltpu.VMEM((MOE_ROWS * ROW_SLABS, LANES), h2.dtype),
                        pltpu.VMEM((2, tm * ROW_SLABS, LANES), h2.dtype),
                        pltpu.SemaphoreType.DMA((2,)), pltpu.SemaphoreType.DMA],
        compiler_params=_cparams(("arbitrary",)),
        name="scatter",
    )(dest_tok, last_block_row, h2)


def _expert_kernel(be_ref, nused_ref, x_ref, wg_ref, bg_ref, wu_ref, bu_ref, wd_ref, bd_ref,
                   o_ref, wgb_ref, wub_ref, wdb_ref):
    i = pl.program_id(0)
    prev = be_ref[jnp.maximum(i - 1, 0)]

    @pl.when(jnp.logical_or(i == 0, be_ref[i] != prev))
    def _():
        wgb_ref[...] = wg_ref[0].astype(BF16)
        wub_ref[...] = wu_ref[0].astype(BF16)
        wdb_ref[...] = wd_ref[0].astype(BF16)

    @pl.when(i < nused_ref[0])
    def _():
        xb = jnp.concatenate([v.astype(BF16) for v in _unpack_halves(_load_rows(x_ref))],
                             axis=1)
        g = jnp.minimum(jnp.dot(xb, wgb_ref[...], preferred_element_type=F32) + bg_ref[0],
                        SWIGLU_LIMIT)
        u = jnp.clip(jnp.dot(xb, wub_ref[...], preferred_element_type=F32) + bu_ref[0],
                     -SWIGLU_LIMIT, SWIGLU_LIMIT)
        act = g * _sigmoid(SWIGLU_ALPHA * g) * (u + 1.0)
        _store_rows(o_ref, _pack_halves(
            jnp.dot(act.astype(BF16), wdb_ref[...], preferred_element_type=F32) + bd_ref[0]))

    @pl.when(i >= nused_ref[0])
    def _():
        o_ref[...] = jnp.zeros(o_ref.shape, o_ref.dtype)


def _expert(block_expert, n_used, x_pad, wg, bg, wu, bu, wd, bd):
    lines = MOE_ROWS * ROW_SLABS
    _, d, f = wg.shape
    nblk = x_pad.shape[0] // lines
    wspec = lambda a, b: pl.BlockSpec((1, a, b), lambda i, be, nu: (be[i], 0, 0))
    grid_spec = pltpu.PrefetchScalarGridSpec(
        num_scalar_prefetch=2,
        grid=(nblk,),
        in_specs=[pl.BlockSpec((lines, LANES),
                               lambda i, be, nu: (jnp.minimum(i, nu[0] - 1), 0)),
                  wspec(d, f), wspec(1, f), wspec(d, f), wspec(1, f), wspec(f, d), wspec(1, d)],
        out_specs=pl.BlockSpec((lines, LANES), lambda i, be, nu: (i, 0)),
        scratch_shapes=[pltpu.VMEM((d, f), BF16), pltpu.VMEM((d, f), BF16),
                        pltpu.VMEM((f, d), BF16)])
    return pl.pallas_call(
        _expert_kernel,
        grid_spec=grid_spec,
        out_shape=jax.ShapeDtypeStruct(x_pad.shape, U32),
        compiler_params=_cparams(("arbitrary",)),
        name="expert",
    )(block_expert, n_used, x_pad, wg, bg.reshape(N_EXPERTS, 1, f), wu,
      bu.reshape(N_EXPERTS, 1, f), wd, bd.reshape(N_EXPERTS, 1, d))


def _combine_kernel(dest_ref, y_hbm, x1_ref, wts_ref, g2_ref, nf_ref, o_ref, ybuf, sems, *,
                    tm, final):
    i = pl.program_id(0)
    n_tiles = pl.num_programs(0) - 1
    slot = i % 2

    @pl.when(i < n_tiles)
    def _():
        def row_copy(tt, k):
            d = dest_ref[tt * TOP_K + k]
            return pltpu.make_async_copy(_row(y_hbm, d), _row(ybuf.at[slot, k], tt),
                                         sems.at[slot])

        def issue(g, carry):
            base = pl.multiple_of(g * ROW_UNROLL, ROW_UNROLL)
            for j in range(ROW_UNROLL):
                for k in range(TOP_K):
                    row_copy(base + j, k).start(priority=k % 2)
            return carry

        lax.fori_loop(0, tm // ROW_UNROLL, issue, 0)

    @pl.when(i > 0)
    def _():
        done = 1 - slot
        for k in range(TOP_K):
            pltpu.make_async_copy(y_hbm.at[pl.ds(0, tm * ROW_SLABS)], ybuf.at[done, k],
                                  sems.at[done]).wait()
        w = wts_ref[...]
        lo, hi = None, None
        for k in range(TOP_K):
            y_lo, y_hi = _unpack_halves(_load_rows(ybuf.at[done, k]))
            lo = w[:, k:k + 1] * y_lo + (0.0 if lo is None else lo)
            hi = w[:, k:k + 1] * y_hi + (0.0 if hi is None else hi)
        moe = jnp.concatenate([lo, hi], axis=1)
        x2 = x1_ref[...] + g2_ref[0] * moe
        o_ref[...] = _rms(x2) * nf_ref[...] if final else x2


def _combine(dest_tok, y_pad, x1, wts_tok, g2, nf, seq, tm, final):
    t, d = x1.shape
    tpb = seq // tm
    n_tiles = t // tm
    lag = lambda i: jnp.maximum(i - 1, 0)
    return pl.pallas_call(
        functools.partial(_combine_kernel, tm=tm, final=final),
        grid=(n_tiles + 1,),
        in_specs=[pl.BlockSpec((tm * TOP_K,), lambda i: (jnp.minimum(i, n_tiles - 1),),
                               memory_space=pltpu.SMEM),
                  pl.BlockSpec(memory_space=pl.ANY),
                  pl.BlockSpec((tm, d), lambda i: (lag(i), 0)),
                  pl.BlockSpec((tm, TOP_K), lambda i: (lag(i), 0)),
                  pl.BlockSpec((1, 1, d), lambda i: (lag(i) // tpb, 0, 0)),
                  pl.BlockSpec((1, d), lambda i: (0, 0))],
        out_specs=pl.BlockSpec((tm, d), lambda i: (lag(i), 0)),
        out_shape=jax.ShapeDtypeStruct((t, d), F32),
        scratch_shapes=[pltpu.VMEM((2, TOP_K, tm * ROW_SLABS, LANES), y_pad.dtype),
                        pltpu.SemaphoreType.DMA((2,))],
        compiler_params=_cparams(("arbitrary",)),
        name="combine",
    )(dest_tok, y_pad, x1, wts_tok, g2, nf.reshape(1, d))


def _pick(n, cands):
    for c in cands:
        if n % c == 0:
            return c
    raise ValueError(f"no tile in {cands} divides {n}")


def kernel(x, c, rel_bias, w_ada, b_ada, norm_mix, w_in, w_gk_up, b_gk, gla_norm,
           w_proj_moba, w_proj_gla, w_out, norm_ffn, w_router, b_router,
           w_gate, b_gate, w_up, b_up, w_down, b_down, norm_final):
    bsz, seq, d = x.shape
    depth = w_ada.shape[0]
    assert d == D_MODEL and seq % MOBA_BLOCK == 0 and seq // MOBA_BLOCK <= MOBA_MAX_BLOCKS
    t = bsz * seq
    tm = _pick(seq, (512, 256))
    nchunk = _pick(seq // GLA_CHUNK, (8, 4))
    n_blk = seq // MOBA_BLOCK
    x2d = x.reshape(t, d)
    bias = _bias_tiles(rel_bias)
    per_b = lambda v: v.reshape(bsz, 1, d)

    for l in range(depth):
        mod = _ada(c, w_ada[l], b_ada[l])
        sh1, sc1, g1, sh2, sc2, g2 = [per_b(m) for m in jnp.split(mod, 6, axis=-1)]
        qt, k_aug, vt, qkb, vb, gk, r_act, gates = _inproj(
            x2d, norm_mix[l], sc1, sh1, _regroup_w_in(w_in[l]), bsz, seq, tm)
        ya = _moba(qt, k_aug, vt, bias)
        wup = jnp.pad(w_gk_up[l], ((0, LANES - GLA_GATE_RANK), (0, 0)))
        yb = _gla(qkb, vb, gk, wup, b_gk[l].reshape(1, -1), r_act,
                  gla_norm[l].reshape(1, -1), bsz, seq, nchunk)
        x1, h2, idx, wts = _merge(
            ya, yb, gates, x2d, w_proj_moba[l].astype(BF16), w_proj_gla[l].astype(BF16),
            w_out[l].astype(BF16), g1, norm_ffn[l], sc2, sh2,
            w_router[l].T, b_router[l].reshape(N_EXPERTS, 1), seq, tm)
        rank, cnt = _rank(idx, tm)
        counts = cnt[:, 0].astype(I32)
        padded = (counts + MOE_ROWS - 1) // MOE_ROWS * MOE_ROWS
        pcum = jnp.cumsum(padded)
        pstart = (pcum - padded).astype(I32)
        n_blocks = (t * TOP_K + MOE_ROWS - 1) // MOE_ROWS + N_EXPERTS
        block_row0 = jnp.arange(n_blocks, dtype=I32) * MOE_ROWS
        block_expert = jnp.minimum(
            jnp.sum((pcum[None, :] <= block_row0[:, None]).astype(I32), axis=1),
            N_EXPERTS - 1).astype(I32)
        n_used = (pcum[-1:] // MOE_ROWS).astype(I32)
        dest = _dest(pstart, idx, rank, tm)
        dest_tok = dest.T.reshape(t * TOP_K)
        tg = _pick(seq, (256,))
        zero_info = jnp.concatenate([jnp.maximum(pcum - MOE_ROWS, 0).astype(I32), n_used])
        x_pad = _scatter(dest_tok, zero_info, h2, n_blocks * MOE_ROWS, tg)
        y_pad = _expert(block_expert, n_used, x_pad, w_gate[l], b_gate[l], w_up[l], b_up[l],
                        w_down[l], b_down[l])
        x2d = _combine(dest_tok, y_pad, x1, wts.T, g2, norm_final, seq, tg, l == depth - 1)
    return x2d.reshape(bsz, seq, d)
```

```python
import functools
import math

import numpy as np
import jax
import jax.numpy as jnp
from jax import lax
from jax.experimental import pallas as pl
from jax.experimental.pallas import tpu as pltpu

F32 = jnp.float32
BF16 = jnp.bfloat16
I32 = jnp.int32
HIGHEST = lax.Precision.HIGHEST

D_MODEL = 1024
MOBA_HEADS = 8
MOBA_HEAD_DIM = 64
MOBA_WIDTH = MOBA_HEADS * MOBA_HEAD_DIM
MOBA_BLOCK = 256
MOBA_TOPK = 3
MOBA_MAX_BLOCKS = 32
REL_BUCKETS = 32
REL_MAX_DIST = 128
GLA_HEADS = 4
GLA_KEY_DIM = D_MODEL // 2
GLA_VALUE_DIM = D_MODEL
GLA_DK = GLA_KEY_DIM // GLA_HEADS
GLA_DV = GLA_VALUE_DIM // GLA_HEADS
GLA_GATE_RANK = 16
GLA_GATE_NORMALIZER = 16.0
GLA_CHUNK = 64
N_EXPERTS = 32
TOP_K = 4
D_FF = D_MODEL
SWIGLU_ALPHA = 1.702
SWIGLU_LIMIT = 7.0
MOE_ROWS = 512
ROW_UNROLL = 8
EPS = 1e-6
LANES = 128
SUBLANES = 8
BF16_SUBLANES = 16
ADA_COLS = 768
NEG_BIG = -1e30
LOG2E = math.log2(math.e)
VMEM_LIMIT = 56 * 1024 * 1024


def _cparams(sem, vmem=None):
    return pltpu.CompilerParams(dimension_semantics=sem,
                                vmem_limit_bytes=vmem or VMEM_LIMIT)


def _nt_dot(a, b, **kw):
    return lax.dot_general(a, b, (((1,), (1,)), ((), ())),
                           preferred_element_type=F32, **kw)


def _rms(x):
    return x * lax.rsqrt(jnp.mean(x * x, axis=-1, keepdims=True) + EPS)


def _sigmoid(x):
    return 1.0 / (1.0 + jnp.exp(-x))


U32 = jnp.uint32
_HI16 = 0xFFFF0000


def _pack_halves(x):
    n = x.shape[1] // 2
    lo = pltpu.bitcast(x[:, :n].astype(BF16).astype(F32), U32)
    hi = pltpu.bitcast(x[:, n:].astype(BF16).astype(F32), U32)
    return (hi & U32(_HI16)) | (lo >> 16)


def _unpack_halves(w):
    return (pltpu.bitcast(w << 16, F32), pltpu.bitcast(w & U32(_HI16), F32))


ROW_SLABS = D_MODEL // 2 // LANES


def _store_rows(ref, words):
    m = words.shape[0]
    for c in range(ROW_SLABS):
        ref[pl.ds(c, m, stride=ROW_SLABS), :] = words[:, c * LANES:(c + 1) * LANES]


def _load_rows(ref):
    m = ref.shape[0] // ROW_SLABS
    return jnp.concatenate(
        [ref[pl.ds(c, m, stride=ROW_SLABS), :] for c in range(ROW_SLABS)], axis=1)


def _row(ref, i):
    return ref.at[pl.ds(pl.multiple_of(i * ROW_SLABS, ROW_SLABS), ROW_SLABS)]


def _ada_kernel(c_ref, w_ref, b_ref, o_ref):
    c = c_ref[...]
    s = c * _sigmoid(c)
    o_ref[...] = jnp.dot(s, w_ref[...], precision=HIGHEST,
                         preferred_element_type=F32) + b_ref[...]


def _ada(c, w, b):
    bsz, d = c.shape
    n = w.shape[1]
    rows = -(-bsz // SUBLANES) * SUBLANES
    cp = jnp.zeros((rows, d), F32).at[:bsz].set(c)
    tn = ADA_COLS
    out = pl.pallas_call(
        _ada_kernel,
        grid=(n // tn,),
        in_specs=[pl.BlockSpec((rows, d), lambda j: (0, 0)),
                  pl.BlockSpec((d, tn), lambda j: (0, j)),
                  pl.BlockSpec((1, tn), lambda j: (0, j))],
        out_specs=pl.BlockSpec((rows, tn), lambda j: (0, j)),
        out_shape=jax.ShapeDtypeStruct((rows, n), F32),
        compiler_params=_cparams(("arbitrary",)),
        name="ada",
    )(cp, w, b.reshape(1, n))
    return out[:bsz]


_OFF_QA = 0
_OFF_KA = _OFF_QA + MOBA_WIDTH
_OFF_VA = _OFF_KA + MOBA_WIDTH
_OFF_QKB = _OFF_VA + MOBA_WIDTH
_OFF_VB = _OFF_QKB + 2 * GLA_KEY_DIM
_OFF_GK = _OFF_VB + GLA_VALUE_DIM
_OFF_R = _OFF_GK + LANES
_OFF_G = _OFF_R + GLA_VALUE_DIM
_W_CAT = _OFF_G + 2 * D_MODEL


def _regroup_w_in(w):
    o_gk = 3 * MOBA_WIDTH + 2 * GLA_KEY_DIM + GLA_VALUE_DIM
    gk = jnp.pad(w[:, o_gk:o_gk + GLA_GATE_RANK], ((0, 0), (0, LANES - GLA_GATE_RANK)))
    return jnp.concatenate([w[:, :o_gk], gk, w[:, o_gk + GLA_GATE_RANK:]],
                           axis=1).astype(BF16)


def _inproj_kernel(x_ref, nw_ref, sc_ref, sh_ref, w_ref,
                   qt_ref, ka_ref, vt_ref, qkb_ref, vb_ref, gk_ref, r_ref, g_ref, *, tpb):
    tm = x_ref.shape[0]
    hd = MOBA_HEAD_DIM
    nbt = tm // MOBA_BLOCK
    h = _rms(x_ref[...]) * nw_ref[...]
    h = h * (1.0 + sc_ref[0]) + sh_ref[0]
    hb = h.astype(BF16)

    def mm(a, b):
        return jnp.dot(hb, w_ref[:, a:b], preferred_element_type=F32)

    q_t = (mm(_OFF_QA, _OFF_KA) * (hd ** -0.5 * LOG2E)).T
    v_t = mm(_OFF_VA, _OFF_QKB).T
    k_all = mm(_OFF_KA, _OFF_VA)
    blk0 = (pl.program_id(0) % tpb) * nbt
    lane = lax.broadcasted_iota(I32, (MOBA_BLOCK, LANES), 1)
    ones_rows = jnp.where(
        lax.broadcasted_iota(I32, (MOBA_VT_ROWS - hd, MOBA_BLOCK), 0) == 0, 1.0, 0.0)
    heads_per_tile = LANES // hd
    for hh in range(MOBA_HEADS):
        qt_ref[0, hh] = q_t[hh * hd:(hh + 1) * hd].astype(BF16)
        tile = hh // heads_per_tile
        k_h = k_all[:, tile * LANES:(tile + 1) * LANES]
        if hh % heads_per_tile:
            k_h = pltpu.roll(k_h, LANES - (hh % heads_per_tile) * hd, axis=1)
        for j in range(nbt):
            rows = slice(j * MOBA_BLOCK, (j + 1) * MOBA_BLOCK)
            onehot = jnp.where(lane == hd + blk0 + j, 1.0, 0.0)
            ka_ref[0, hh, j] = jnp.where(lane < hd, k_h[rows], onehot).astype(BF16)
            vt_ref[0, hh, j, 0:hd, :] = v_t[hh * hd:(hh + 1) * hd, rows].astype(BF16)
            vt_ref[0, hh, j, hd:MOBA_VT_ROWS, :] = ones_rows.astype(BF16)
    qkb_ref[...] = mm(_OFF_QKB, _OFF_VB).astype(BF16)
    vb_ref[...] = mm(_OFF_VB, _OFF_GK).astype(BF16)
    gk_ref[...] = mm(_OFF_GK, _OFF_R)
    r = mm(_OFF_R, _OFF_G)
    r_ref[...] = (r * _sigmoid(r)).astype(BF16)
    g_ref[...] = _sigmoid(mm(_OFF_G, _W_CAT)).astype(BF16)


def _inproj(x2d, nw, sc, sh, w_cat, bsz, seq, tm):
    t, d = x2d.shape
    tpb = seq // tm
    nbt = tm // MOBA_BLOCK
    nh, hd = MOBA_HEADS, MOBA_HEAD_DIM
    row = lambda w: pl.BlockSpec((tm, w), lambda i: (i, 0))
    per_b = pl.BlockSpec((1, 1, d), lambda i: (i // tpb, 0, 0))
    rows_out = [(2 * GLA_KEY_DIM, BF16), (GLA_VALUE_DIM, BF16), (LANES, F32),
                (GLA_VALUE_DIM, BF16), (2 * D_MODEL, BF16)]
    return pl.pallas_call(
        functools.partial(_inproj_kernel, tpb=tpb),
        grid=(t // tm,),
        in_specs=[row(d), pl.BlockSpec((1, d), lambda i: (0, 0)), per_b, per_b,
                  pl.BlockSpec((d, _W_CAT), lambda i: (0, 0), pipeline_mode=pl.Buffered(1))],
        out_specs=[pl.BlockSpec((1, nh, hd, tm), lambda i: (i // tpb, 0, 0, i % tpb)),
                   pl.BlockSpec((1, nh, nbt, MOBA_BLOCK, LANES),
                                lambda i: (i // tpb, 0, i % tpb, 0, 0)),
                   pl.BlockSpec((1, nh, nbt, MOBA_VT_ROWS, MOBA_BLOCK),
                                lambda i: (i // tpb, 0, i % tpb, 0, 0))]
                  + [row(w) for w, _ in rows_out],
        out_shape=[jax.ShapeDtypeStruct((bsz, nh, hd, seq), BF16),
                   jax.ShapeDtypeStruct((bsz, nh, seq // MOBA_BLOCK, MOBA_BLOCK, LANES), BF16),
                   jax.ShapeDtypeStruct((bsz, nh, seq // MOBA_BLOCK, MOBA_VT_ROWS, MOBA_BLOCK),
                                        BF16)]
                  + [jax.ShapeDtypeStruct((t, w), dt) for w, dt in rows_out],
        compiler_params=_cparams(("arbitrary",)),
        name="inproj",
    )(x2d, nw.reshape(1, d), sc, sh, w_cat)


def _t5_bucket_np(n):
    n = np.maximum(n, 0)
    max_exact = REL_BUCKETS // 2
    nf = np.maximum(n, max_exact).astype(np.float32)
    large = max_exact + (np.log(nf / max_exact) / math.log(REL_MAX_DIST / max_exact)
                         * (REL_BUCKETS - max_exact)).astype(np.int32)
    large = np.minimum(large, REL_BUCKETS - 1)
    return np.where(n < max_exact, n, large).astype(np.int32)


def _bucket_table():
    kj = np.arange(MOBA_BLOCK)[:, None]
    qi = np.arange(2 * MOBA_BLOCK)[None, :] % MOBA_BLOCK
    prev = np.arange(2 * MOBA_BLOCK)[None, :] < MOBA_BLOCK
    bucket = _t5_bucket_np(qi - kj + np.where(prev, MOBA_BLOCK, 0))
    return np.where(prev | (kj <= qi), bucket, -1).astype(np.int32)


def _bias_kernel(rb_ref, bucket_ref, o_ref):
    h = pl.program_id(0)
    bk = bucket_ref[...]
    far = rb_ref[(REL_BUCKETS - 1) * MOBA_HEADS + h]
    acc = jnp.zeros(bk.shape, F32)
    for b in range(REL_BUCKETS):
        acc = jnp.where(bk == b, rb_ref[b * MOBA_HEADS + h] - far, acc)
    o_ref[0] = jnp.where(bk < 0, NEG_BIG, acc * LOG2E)


def _bias_tiles(rel_bias):
    bucket = jnp.asarray(_bucket_table())
    return pl.pallas_call(
        _bias_kernel,
        grid=(MOBA_HEADS,),
        in_specs=[pl.BlockSpec(memory_space=pltpu.SMEM),
                  pl.BlockSpec(bucket.shape, lambda h: (0, 0))],
        out_specs=pl.BlockSpec((1,) + bucket.shape, lambda h: (h, 0, 0)),
        out_shape=jax.ShapeDtypeStruct((MOBA_HEADS,) + bucket.shape, F32),
        compiler_params=_cparams(("arbitrary",)),
        name="bias",
    )(rel_bias.reshape(-1), bucket)


MOBA_HEADS_PER_STEP = 8
MOBA_FAR_GROUP = 2
MOBA_QBLOCKS_PER_STEP = 4
MOBA_FAR_LAGS = (0, 3, 6)
MOBA_NEAR_LAGS = (0, 3, 6, 9)
MOBA_VT_ROWS = MOBA_HEAD_DIM + BF16_SUBLANES


def _moba_kernel(qt_ref, k_ref, vt_ref, bias_ref, o_ref, kmean_ref, qa_ref):
    blk = MOBA_BLOCK
    nb = MOBA_MAX_BLOCKS
    hd = MOBA_HEAD_DIM
    hp = MOBA_HEADS_PER_STEP
    grp = MOBA_FAR_GROUP
    nq = MOBA_QBLOCKS_PER_STEP
    pair = pl.program_id(2)
    neg = -jnp.inf
    items = [(h, j) for j in range(nq) for h in range(hp)]

    @pl.when(pair == 0)
    def _():
        kmean_ref[...] = jnp.zeros(kmean_ref.shape, F32)

    for h in range(hp):
        for j in range(nq):
            kmean_ref[h, pl.ds(pair * nq + j, 1), :] = jnp.mean(
                k_ref[0, h, pair * nq + j].astype(F32), axis=0, keepdims=True)

    row = lax.broadcasted_iota(I32, (nb, blk), 0)
    rowf = row.astype(F32)
    pad = jnp.zeros((LANES - hd, blk), BF16)
    pad_hi = jnp.zeros((LANES - hd - nb, blk), BF16)

    def skewed(stages, lags, todo=None):
        todo = list(range(len(items))) if todo is None else todo
        vals = {}
        for step in range(len(todo) + lags[-1]):
            for stage, lag in zip(stages, lags):
                pos = step - lag
                if 0 <= pos < len(todo):
                    vals[todo[pos]] = stage(todo[pos], vals.get(todo[pos]))
        return vals

    def select(n, _):
        h, j = items[n]
        qi = pair * nq + j
        qt = qt_ref[0, h, :, j * blk:(j + 1) * blk]
        km = kmean_ref[h, :, 0:hd]
        km_hi = km.astype(BF16)
        km_lo = (km - km_hi.astype(F32)).astype(BF16)
        gate = (jnp.dot(km_hi, qt, preferred_element_type=F32)
                + jnp.dot(km_lo, qt, preferred_element_type=F32))
        g = jnp.where(row < qi, gate, neg)
        sel = jnp.zeros((nb, blk), F32)
        for _ in range(MOBA_TOPK):
            mx = jnp.max(g, axis=0, keepdims=True)
            first = jnp.min(jnp.where(g == mx, rowf, float(nb)), axis=0, keepdims=True)
            pick = rowf == jnp.where(mx > neg, first, -1.0)
            sel = jnp.where(pick, 1.0, sel)
            g = jnp.where(pick, neg, g)
        mask_prev = jnp.where(sel > 0.0, jnp.where(row == qi - 1, 0.0, NEG_BIG), NEG_BIG)
        mask_far = jnp.where(sel > 0.0, jnp.where(row < qi - 1, 0.0, NEG_BIG), NEG_BIG)
        qa_ref[n] = jnp.concatenate([qt, mask_far.astype(BF16), pad_hi], axis=0)
        return (jnp.concatenate([qt, pad], axis=0),
                jnp.concatenate([qt, mask_prev.astype(BF16), pad_hi], axis=0))

    def own_prev(n):
        h, j = items[n]
        qi = pair * nq + j
        return h, qi, jnp.maximum(qi - 1, 0)

    def near_scores(n, qa):
        h, qi, prev_j = own_prev(n)
        qa_own, qa_prev = qa
        s_own = jnp.dot(k_ref[0, h, qi], qa_own, preferred_element_type=F32)
        s_prev = jnp.dot(k_ref[0, h, prev_j], qa_prev, preferred_element_type=F32)
        return s_own, s_prev

    def near_softmax(n, ss):
        h = items[n][0]
        s_own, s_prev = ss
        s = jnp.concatenate([s_own + bias_ref[h, :, blk:2 * blk],
                             s_prev + bias_ref[h, :, 0:blk]], axis=0)
        m0 = jnp.max(s, axis=0, keepdims=True)
        return m0, jnp.exp2(s - m0)

    def near_pv(n, mp):
        h, qi, prev_j = own_prev(n)
        m0, p = mp
        pb = p.astype(BF16)
        acc = (jnp.dot(vt_ref[0, h, qi], pb[0:blk], preferred_element_type=F32)
               + jnp.dot(vt_ref[0, h, prev_j], pb[blk:2 * blk], preferred_element_type=F32))
        return m0, acc

    near = skewed([select, near_scores, near_softmax, near_pv], MOBA_NEAR_LAGS)
    states = tuple(near[n] for n in range(len(items)))

    def far(gi, states, todo=None):
        j0 = gi * grp

        def qk(n, _):
            kt = k_ref[0, items[n][0], pl.ds(j0, grp)].reshape(grp * blk, LANES)
            return jnp.dot(kt, qa_ref[n], preferred_element_type=F32)

        def softmax(n, s):
            m_old = states[n][0]
            m_new = jnp.maximum(m_old, jnp.max(s, axis=0, keepdims=True))
            return m_new, jnp.exp2(m_old - m_new), jnp.exp2(s - m_new)

        def pv(n, sm):
            m_new, a, p = sm
            pb = p.astype(BF16)
            tot = a * states[n][1]
            for i in range(grp):
                tot = tot + jnp.dot(vt_ref[0, items[n][0], j0 + i], pb[i * blk:(i + 1) * blk],
                                    preferred_element_type=F32)
            return m_new, tot

        new = skewed([qk, softmax, pv], MOBA_FAR_LAGS, todo)
        return tuple(new.get(n, states[n]) for n in range(len(items)))

    sub = nq // grp
    states = lax.fori_loop(0, pair * sub, far, tuple(states))
    for extra in range(1, sub):
        later = [n for n, (_, j) in enumerate(items) if j // grp >= extra]
        states = far(pair * sub + extra - 1, states, later)

    for n, (h, j) in enumerate(items):
        acc = states[n][1]
        o_ref[0, h * hd:(h + 1) * hd, j * blk:(j + 1) * blk] = (
            acc[0:hd] / acc[hd:hd + 1]).astype(o_ref.dtype)


def _moba(qt, k_aug, vt, bias):
    bsz, nh, hd, s = qt.shape
    blk = MOBA_BLOCK
    hp = MOBA_HEADS_PER_STEP
    nq = MOBA_QBLOCKS_PER_STEP
    nblk = s // blk
    assert nh % hp == 0 and nblk % nq == 0 and nq % MOBA_FAR_GROUP == 0
    return pl.pallas_call(
        _moba_kernel,
        grid=(bsz, nh // hp, nblk // nq),
        in_specs=[pl.BlockSpec((1, hp, hd, nq * blk), lambda b, g, i: (b, g, 0, i)),
                  pl.BlockSpec((1, hp, nblk, blk, LANES), lambda b, g, i: (b, g, 0, 0, 0),
                               pipeline_mode=pl.Buffered(1)),
                  pl.BlockSpec((1, hp, nblk, MOBA_VT_ROWS, blk), lambda b, g, i: (b, g, 0, 0, 0),
                               pipeline_mode=pl.Buffered(1)),
                  pl.BlockSpec((hp, blk, 2 * blk), lambda b, g, i: (g, 0, 0),
                               pipeline_mode=pl.Buffered(1))],
        out_specs=pl.BlockSpec((1, hp * hd, nq * blk), lambda b, g, i: (b, g, i)),
        out_shape=jax.ShapeDtypeStruct((bsz, nh * hd, s), BF16),
        scratch_shapes=[pltpu.VMEM((hp, MOBA_MAX_BLOCKS, LANES), F32),
                        pltpu.VMEM((hp * nq, LANES, blk), BF16)],
        compiler_params=_cparams(("arbitrary", "arbitrary", "arbitrary")),
        name="moba",
    )(qt, k_aug, vt, bias)


def _gla_kernel(q_ref, k_ref, v_ref, gk_ref, wup_ref, bgk_ref, r_ref, gn_ref, o_ref,
                state_ref, *, nchunk):
    ch = GLA_CHUNK
    tc = nchunk * ch
    dk, dv = GLA_DK, GLA_DV

    @pl.when(pl.program_id(1) == 0)
    def _():
        state_ref[...] = jnp.zeros(state_ref.shape, F32)

    rin = lax.broadcasted_iota(I32, (tc, dk), 0) & (ch - 1)
    causal = (lax.broadcasted_iota(I32, (ch, ch), 1) <= lax.broadcasted_iota(I32, (ch, ch), 0))
    eye = (lax.broadcasted_iota(I32, (dk, dk), 0) == lax.broadcasted_iota(I32, (dk, dk), 1))
    chunks = [slice(n * ch, (n + 1) * ch) for n in range(nchunk)]
    gk = gk_ref[...]
    gk_hi = gk.astype(BF16)
    gk_lo = (gk - gk_hi.astype(F32)).astype(BF16)

    def prep(h, _):
        ks = slice(h * dk, (h + 1) * dk)
        w = wup_ref[:, ks]
        w_hi = w.astype(BF16)
        w_lo = (w - w_hi.astype(F32)).astype(BF16)
        z = (jnp.dot(gk_hi, w_hi, preferred_element_type=F32)
             + jnp.dot(gk_lo, w_hi, preferred_element_type=F32)
             + jnp.dot(gk_hi, w_lo, preferred_element_type=F32) + bgk_ref[:, ks])
        log_a = ((jnp.minimum(z, 0.0) - jnp.log(1.0 + jnp.exp(-jnp.abs(z))))
                 / GLA_GATE_NORMALIZER)
        b = log_a
        sh = 1
        while sh < ch:
            b = b + jnp.where(rin >= sh, pltpu.roll(b, sh, axis=0), 0.0)
            sh *= 2
        q = q_ref[:, ks].astype(F32) * (dk ** -0.5)
        k = k_ref[:, ks].astype(F32)
        q_g = (q * jnp.exp(b)).astype(BF16)
        k_g = (k * jnp.exp(-b)).astype(BF16)
        b3 = b.reshape(nchunk, ch, dk)
        b_last = b3[:, ch - 1:ch, :]
        k_end = (k * jnp.exp(jnp.broadcast_to(b_last, b3.shape) - b3).reshape(tc, dk)
                 ).astype(BF16)
        return q_g, k_g, k_end, jnp.exp(b_last)

    def local(h, pre):
        q_g, k_g, k_end, decay = pre
        o_intra, kv, decay_col = [], [], []
        for n, sl in enumerate(chunks):
            v_c = v_ref[sl, h * dv:(h + 1) * dv]
            att = jnp.where(causal, _nt_dot(q_g[sl], k_g[sl]), 0.0)
            o_intra.append(jnp.dot(att.astype(BF16), v_c, preferred_element_type=F32))
            kv.append(lax.dot_general(k_end[sl], v_c, (((0,), (0,)), ((), ())),
                                      preferred_element_type=F32))
            decay_col.append(jnp.sum(
                jnp.where(eye, jnp.broadcast_to(decay[n], (dk, dk)), 0.0),
                axis=1, keepdims=True))
        return q_g, o_intra, kv, decay_col

    def chain(h, loc):
        q_g, o_intra, kv, decay_col = loc
        state = state_ref[h]
        outs = []
        for n, sl in enumerate(chunks):
            outs.append(o_intra[n] + jnp.dot(q_g[sl], state.astype(BF16),
                                             preferred_element_type=F32))
            state = decay_col[n] * state + kv[n]
        state_ref[h] = state
        return jnp.concatenate(outs, axis=0)

    def finish(h, o):
        vs = slice(h * dv, (h + 1) * dv)
        o_ref[:, vs] = (_rms(o) * gn_ref[...] * r_ref[:, vs].astype(F32)).astype(o_ref.dtype)
        return None

    stages = [prep, local, chain, finish]
    vals = [None] * GLA_HEADS
    for step in range(GLA_HEADS + len(stages) - 1):
        for si, stage in enumerate(stages):
            h = step - si
            if 0 <= h < GLA_HEADS:
                vals[h] = stage(h, vals[h])


def _gla(qkb, vb, gk, wup, bgk, r_act, gn, bsz, seq, nchunk):
    t = qkb.shape[0]
    tc = nchunk * GLA_CHUNK
    nc = seq // tc
    rowblk = lambda w, off: pl.BlockSpec((tc, w), lambda b, c: (b * nc + c, off))
    full = lambda a: pl.BlockSpec(a.shape, lambda b, c: (0, 0))
    return pl.pallas_call(
        functools.partial(_gla_kernel, nchunk=nchunk),
        grid=(bsz, nc),
        in_specs=[rowblk(GLA_KEY_DIM, 0), rowblk(GLA_KEY_DIM, 1), rowblk(GLA_VALUE_DIM, 0),
                  rowblk(LANES, 0), full(wup), full(bgk), rowblk(GLA_VALUE_DIM, 0), full(gn)],
        out_specs=rowblk(GLA_VALUE_DIM, 0),
        out_shape=jax.ShapeDtypeStruct((t, GLA_VALUE_DIM), BF16),
        scratch_shapes=[pltpu.VMEM((GLA_HEADS, GLA_DK, GLA_DV), F32)],
        compiler_params=_cparams(("arbitrary", "arbitrary")),
        name="gla",
    )(qkb, qkb, vb, gk, wup, bgk, r_act, gn)


def _merge_kernel(ya_ref, yb_ref, g_ref, x_ref, wpa_ref, wpb_ref, wout_ref, g1_ref,
                  nw_ref, sc_ref, sh_ref, wr_ref, br_ref,
                  x1_ref, h2_ref, idx_ref, wts_ref):
    pa = lax.dot_general(ya_ref[0], wpa_ref[...], (((0,), (0,)), ((), ())),
                         preferred_element_type=F32)
    pb = jnp.dot(yb_ref[...], wpb_ref[...], preferred_element_type=F32)
    mixed = (g_ref[:, 0:D_MODEL].astype(F32) * pa
             + g_ref[:, D_MODEL:2 * D_MODEL].astype(F32) * pb)
    y = jnp.dot(mixed.astype(BF16), wout_ref[...], preferred_element_type=F32)
    x1 = x_ref[...] + g1_ref[0] * y
    x1_ref[...] = x1
    h2 = _rms(x1) * nw_ref[...]
    h2 = h2 * (1.0 + sc_ref[0]) + sh_ref[0]
    _store_rows(h2_ref, _pack_halves(h2))
    h_hi = h2.astype(BF16)
    h_lo = (h2 - h_hi.astype(F32)).astype(BF16)
    w = wr_ref[...]
    w_hi = w.astype(BF16)
    w_lo = (w - w_hi.astype(F32)).astype(BF16)
    logits = (_nt_dot(w_hi, h_hi) + _nt_dot(w_hi, h_lo) + _nt_dot(w_lo, h_hi)
              + br_ref[...])
    rowf = lax.broadcasted_iota(I32, logits.shape, 0).astype(F32)
    vals, idxs = [], []
    cur = logits
    for _ in range(TOP_K):
        mx = jnp.max(cur, axis=0, keepdims=True)
        first = jnp.min(jnp.where(cur == mx, rowf, float(N_EXPERTS)), axis=0, keepdims=True)
        vals.append(mx)
        idxs.append(first)
        cur = jnp.where(rowf == first, -jnp.inf, cur)
    es = [jnp.exp(v - vals[0]) for v in vals]
    tot = es[0]
    for e in es[1:]:
        tot = tot + e
    idx_ref[...] = jnp.concatenate(idxs, axis=0).astype(I32)
    wts_ref[...] = jnp.concatenate([e / tot for e in es], axis=0)


def _merge(ya, yb, gates, x2d, wpa, wpb, wout, g1, nw, sc, sh, wr_t, br, seq, tm):
    t, d = x2d.shape
    tpb = seq // tm
    row = lambda w: pl.BlockSpec((tm, w), lambda i: (i, 0))
    full = lambda a: pl.BlockSpec(a.shape, lambda i: (0,) * a.ndim)
    per_b = pl.BlockSpec((1, 1, d), lambda i: (i // tpb, 0, 0))
    colblk = pl.BlockSpec((TOP_K, tm), lambda i: (0, i))
    return pl.pallas_call(
        _merge_kernel,
        grid=(t // tm,),
        in_specs=[pl.BlockSpec((1, MOBA_WIDTH, tm), lambda i: (i // tpb, 0, i % tpb)),
                  row(GLA_VALUE_DIM), row(2 * D_MODEL), row(d),
                  full(wpa), full(wpb), full(wout), per_b,
                  pl.BlockSpec((1, d), lambda i: (0, 0)), per_b, per_b,
                  full(wr_t), full(br)],
        out_specs=[row(d), pl.BlockSpec((tm * ROW_SLABS, LANES), lambda i: (i, 0)),
                   colblk, colblk],
        out_shape=[jax.ShapeDtypeStruct((t, d), F32),
                   jax.ShapeDtypeStruct((t * ROW_SLABS, LANES), U32),
                   jax.ShapeDtypeStruct((TOP_K, t), I32), jax.ShapeDtypeStruct((TOP_K, t), F32)],
        compiler_params=_cparams(("arbitrary",)),
        name="merge",
    )(ya, yb, gates, x2d, wpa, wpb, wout, g1, nw.reshape(1, d), sc, sh, wr_t, br)


def _rank_kernel(idx_ref, rank_ref, cnt_ref, carry_ref):
    tm = idx_ref.shape[1]

    @pl.when(pl.program_id(0) == 0)
    def _():
        carry_ref[...] = jnp.zeros(carry_ref.shape, F32)

    rows = lax.broadcasted_iota(I32, (N_EXPERTS, tm), 0)
    before = (lax.broadcasted_iota(I32, (tm, tm), 0)
              < lax.broadcasted_iota(I32, (tm, tm), 1))
    upper = jnp.where(before, 1.0, 0.0).astype(BF16)
    carry = carry_ref[:, 0:1]
    ranks = []
    for k in range(TOP_K):
        onehot = idx_ref[k:k + 1, :] == rows
        onef = jnp.where(onehot, 1.0, 0.0)
        earlier = jnp.dot(onef.astype(BF16), upper, preferred_element_type=F32) + carry
        ranks.append(jnp.sum(jnp.where(onehot, earlier, 0.0), axis=0, keepdims=True))
        carry = carry + jnp.sum(onef, axis=1, keepdims=True)
    rank_ref[...] = jnp.concatenate(ranks, axis=0).astype(I32)
    total = jnp.broadcast_to(carry, carry_ref.shape)
    carry_ref[...] = total
    cnt_ref[...] = total


def _rank(idx, tm):
    t = idx.shape[1]
    return pl.pallas_call(
        _rank_kernel,
        grid=(t // tm,),
        in_specs=[pl.BlockSpec((TOP_K, tm), lambda i: (0, i))],
        out_specs=[pl.BlockSpec((TOP_K, tm), lambda i: (0, i)),
                   pl.BlockSpec((N_EXPERTS, LANES), lambda i: (0, 0))],
        out_shape=[jax.ShapeDtypeStruct((TOP_K, t), I32),
                   jax.ShapeDtypeStruct((N_EXPERTS, LANES), F32)],
        scratch_shapes=[pltpu.VMEM((N_EXPERTS, LANES), F32)],
        compiler_params=_cparams(("arbitrary",)),
        name="rank",
    )(idx)


def _dest_kernel(pstart_ref, idx_ref, rank_ref, dest_ref):
    idx = idx_ref[...]
    off = jnp.zeros(idx.shape, I32)
    for e in range(N_EXPERTS):
        off = jnp.where(idx == e, pstart_ref[e], off)
    dest_ref[...] = rank_ref[...] + off


def _dest(pstart, idx, rank, tm):
    t = idx.shape[1]
    blk = pl.BlockSpec((TOP_K, tm), lambda i: (0, i))
    return pl.pallas_call(
        _dest_kernel,
        grid=(t // tm,),
        in_specs=[pl.BlockSpec(memory_space=pltpu.SMEM), blk, blk],
        out_specs=blk,
        out_shape=jax.ShapeDtypeStruct((TOP_K, t), I32),
        compiler_params=_cparams(("arbitrary",)),
        name="dest",
    )(pstart, idx, rank)


def _scatter_kernel(dest_ref, zrow_ref, h_ref, xout_hbm, zbuf, stage, sems, zsem, *, tm):
    @pl.when(pl.program_id(0) == 0)
    def _():
        zbuf[...] = jnp.zeros(zbuf.shape, zbuf.dtype)

        def zero_rows(row0):
            line0 = pl.multiple_of(row0 * ROW_SLABS, MOE_ROWS * ROW_SLABS)
            return pltpu.make_async_copy(
                zbuf, xout_hbm.at[pl.ds(line0, MOE_ROWS * ROW_SLABS)], zsem)

        for e in range(N_EXPERTS):
            zero_rows(zrow_ref[e]).start()
        for e in range(N_EXPERTS):
            zero_rows(zrow_ref[e]).wait()

        def start_tail(j, carry):
            zero_rows(j * MOE_ROWS).start()
            return carry

        def wait_tail(j, carry):
            zero_rows(j * MOE_ROWS).wait()
            return carry

        n_all = xout_hbm.shape[0] // (MOE_ROWS * ROW_SLABS)
        lax.fori_loop(zrow_ref[N_EXPERTS], n_all, start_tail, 0)
        lax.fori_loop(zrow_ref[N_EXPERTS], n_all, wait_tail, 0)

    i = pl.program_id(0)
    slot = i % 2
    src = stage.at[slot]
    src[...] = h_ref[...]

    def issue(g, carry):
        base = pl.multiple_of(g * ROW_UNROLL, ROW_UNROLL)
        for j in range(ROW_UNROLL):
            for k in range(TOP_K):
                d = dest_ref[(base + j) * TOP_K + k]
                pltpu.make_async_copy(_row(src, base + j), _row(xout_hbm, d),
                                      sems.at[slot]).start(priority=k % 2)
        return carry

    lax.fori_loop(0, tm // ROW_UNROLL, issue, 0)

    def wait_tile(s):
        for k in range(TOP_K):
            pltpu.make_async_copy(stage.at[s], xout_hbm.at[pl.ds(0, tm * ROW_SLABS)],
                                  sems.at[s]).wait()

    @pl.when(i > 0)
    def _():
        wait_tile(1 - slot)

    @pl.when(i == pl.num_programs(0) - 1)
    def _():
        wait_tile(slot)


def _scatter(dest_tok, last_block_row, h2, n_pad, tm):
    t = h2.shape[0] // ROW_SLABS
    return pl.pallas_call(
        functools.partial(_scatter_kernel, tm=tm),
        grid=(t // tm,),
        in_specs=[pl.BlockSpec((tm * TOP_K,), lambda i: (i,), memory_space=pltpu.SMEM),
                  pl.BlockSpec(memory_space=pltpu.SMEM),
                  pl.BlockSpec((tm * ROW_SLABS, LANES), lambda i: (i, 0))],
        out_specs=pl.BlockSpec(memory_space=pl.ANY),
        out_shape=jax.ShapeDtypeStruct((n_pad * ROW_SLABS, LANES), h2.dtype),
        scratch_shapes=[pltpu.VMEM((MOE_ROWS * ROW_SLABS, LANES), h2.dtype),
                        pltpu.VMEM((2, tm * ROW_SLABS, LANES), h2.dtype),
                        pltpu.SemaphoreType.DMA((2,)), pltpu.SemaphoreType.DMA],
        compiler_params=_cparams(("arbitrary",)),
        name="scatter",
    )(dest_tok, last_block_row, h2)


def _expert_kernel(be_ref, nused_ref, x_ref, wg_ref, bg_ref, wu_ref, bu_ref, wd_ref, bd_ref,
                   o_ref, wgb_ref, wub_ref, wdb_ref):
    i = pl.program_id(0)
    prev = be_ref[jnp.maximum(i - 1, 0)]

    @pl.when(jnp.logical_or(i == 0, be_ref[i] != prev))
    def _():
        wgb_ref[...] = wg_ref[0].astype(BF16)
        wub_ref[...] = wu_ref[0].astype(BF16)
        wdb_ref[...] = wd_ref[0].astype(BF16)

    @pl.when(i < nused_ref[0])
    def _():
        xb = jnp.concatenate([v.astype(BF16) for v in _unpack_halves(_load_rows(x_ref))],
                             axis=1)
        g = jnp.minimum(jnp.dot(xb, wgb_ref[...], preferred_element_type=F32) + bg_ref[0],
                        SWIGLU_LIMIT)
        u = jnp.clip(jnp.dot(xb, wub_ref[...], preferred_element_type=F32) + bu_ref[0],
                     -SWIGLU_LIMIT, SWIGLU_LIMIT)
        act = g * _sigmoid(SWIGLU_ALPHA * g) * (u + 1.0)
        _store_rows(o_ref, _pack_halves(
            jnp.dot(act.astype(BF16), wdb_ref[...], preferred_element_type=F32) + bd_ref[0]))

    @pl.when(i >= nused_ref[0])
    def _():
        o_ref[...] = jnp.zeros(o_ref.shape, o_ref.dtype)


def _expert(block_expert, n_used, x_pad, wg, bg, wu, bu, wd, bd):
    lines = MOE_ROWS * ROW_SLABS
    _, d, f = wg.shape
    nblk = x_pad.shape[0] // lines
    wspec = lambda a, b: pl.BlockSpec((1, a, b), lambda i, be, nu: (be[i], 0, 0))
    grid_spec = pltpu.PrefetchScalarGridSpec(
        num_scalar_prefetch=2,
        grid=(nblk,),
        in_specs=[pl.BlockSpec((lines, LANES),
                               lambda i, be, nu: (jnp.minimum(i, nu[0] - 1), 0)),
                  wspec(d, f), wspec(1, f), wspec(d, f), wspec(1, f), wspec(f, d), wspec(1, d)],
        out_specs=pl.BlockSpec((lines, LANES), lambda i, be, nu: (i, 0)),
        scratch_shapes=[pltpu.VMEM((d, f), BF16), pltpu.VMEM((d, f), BF16),
                        pltpu.VMEM((f, d), BF16)])
    return pl.pallas_call(
        _expert_kernel,
        grid_spec=grid_spec,
        out_shape=jax.ShapeDtypeStruct(x_pad.shape, U32),
        compiler_params=_cparams(("arbitrary",)),
        name="expert",
    )(block_expert, n_used, x_pad, wg, bg.reshape(N_EXPERTS, 1, f), wu,
      bu.reshape(N_EXPERTS, 1, f), wd, bd.reshape(N_EXPERTS, 1, d))


def _combine_kernel(dest_ref, y_hbm, x1_ref, wts_ref, g2_ref, nf_ref, o_ref, ybuf_a, ybuf_b,
                    sems, *, tm, final, n_tiles):
    i = pl.program_id(0)
    bufs = (ybuf_a, ybuf_b)

    def issue(slot):
        for tt in range(tm):
            for k in range(TOP_K):
                d = dest_ref[tt * TOP_K + k]
                pltpu.make_async_copy(_row(y_hbm, d), _row(bufs[slot].at[k], tt),
                                      sems.at[slot]).start(priority=k % 2)

    def wait(slot):
        for k in range(TOP_K):
            pltpu.make_async_copy(y_hbm.at[pl.ds(0, tm * ROW_SLABS)], bufs[slot].at[k],
                                  sems.at[slot]).wait()

    def finish(slot):
        w = wts_ref[...]
        lo, hi = None, None
        for k in range(TOP_K):
            y_lo, y_hi = _unpack_halves(_load_rows(bufs[slot].at[k]))
            lo = w[:, k:k + 1] * y_lo + (0.0 if lo is None else lo)
            hi = w[:, k:k + 1] * y_hi + (0.0 if hi is None else hi)
        moe = jnp.concatenate([lo, hi], axis=1)
        x2 = x1_ref[...] + g2_ref[0] * moe
        o_ref[...] = _rms(x2) * nf_ref[...] if final else x2

    @pl.when(i == 0)
    def _():
        issue(0)

    for parity in range(2):
        @pl.when(jnp.logical_and(jnp.logical_and(i > 0, i < n_tiles), i % 2 == parity))
        def _(parity=parity):
            wait(1 - parity)
            issue(parity)
            finish(1 - parity)

    @pl.when(i == n_tiles)
    def _():
        wait((n_tiles - 1) % 2)
        finish((n_tiles - 1) % 2)


def _combine(dest_tok, y_pad, x1, wts_tok, g2, nf, seq, tm, final):
    t, d = x1.shape
    tpb = seq // tm
    n_tiles = t // tm
    lag = lambda i: jnp.maximum(i - 1, 0)
    return pl.pallas_call(
        functools.partial(_combine_kernel, tm=tm, final=final, n_tiles=n_tiles),
        grid=(n_tiles + 1,),
        in_specs=[pl.BlockSpec((tm * TOP_K,), lambda i: (jnp.minimum(i, n_tiles - 1),),
                               memory_space=pltpu.SMEM),
                  pl.BlockSpec(memory_space=pl.ANY),
                  pl.BlockSpec((tm, d), lambda i: (lag(i), 0)),
                  pl.BlockSpec((tm, TOP_K), lambda i: (lag(i), 0)),
                  pl.BlockSpec((1, 1, d), lambda i: (lag(i) // tpb, 0, 0)),
                  pl.BlockSpec((1, d), lambda i: (0, 0))],
        out_specs=pl.BlockSpec((tm, d), lambda i: (lag(i), 0)),
        out_shape=jax.ShapeDtypeStruct((t, d), F32),
        scratch_shapes=[pltpu.VMEM((TOP_K, tm * ROW_SLABS, LANES), y_pad.dtype),
                        pltpu.VMEM((TOP_K, tm * ROW_SLABS, LANES), y_pad.dtype),
                        pltpu.SemaphoreType.DMA((2,))],
        compiler_params=_cparams(("arbitrary",)),
        name="combine",
    )(dest_tok, y_pad, x1, wts_tok, g2, nf.reshape(1, d))


def _pick(n, cands):
    for c in cands:
        if n % c == 0:
            return c
    raise ValueError(f"no tile in {cands} divides {n}")


def kernel(x, c, rel_bias, w_ada, b_ada, norm_mix, w_in, w_gk_up, b_gk, gla_norm,
           w_proj_moba, w_proj_gla, w_out, norm_ffn, w_router, b_router,
           w_gate, b_gate, w_up, b_up, w_down, b_down, norm_final):
    bsz, seq, d = x.shape
    depth = w_ada.shape[0]
    assert d == D_MODEL and seq % MOBA_BLOCK == 0 and seq // MOBA_BLOCK <= MOBA_MAX_BLOCKS
    t = bsz * seq
    tm = _pick(seq, (512, 256))
    nchunk = _pick(seq // GLA_CHUNK, (16, 8, 4))
    n_blk = seq // MOBA_BLOCK
    x2d = x.reshape(t, d)
    bias = _bias_tiles(rel_bias)
    per_b = lambda v: v.reshape(bsz, 1, d)

    for l in range(depth):
        mod = _ada(c, w_ada[l], b_ada[l])
        sh1, sc1, g1, sh2, sc2, g2 = [per_b(m) for m in jnp.split(mod, 6, axis=-1)]
        qt, k_aug, vt, qkb, vb, gk, r_act, gates = _inproj(
            x2d, norm_mix[l], sc1, sh1, _regroup_w_in(w_in[l]), bsz, seq, tm)
        ya = _moba(qt, k_aug, vt, bias)
        wup = jnp.pad(w_gk_up[l], ((0, LANES - GLA_GATE_RANK), (0, 0)))
        yb = _gla(qkb, vb, gk, wup, b_gk[l].reshape(1, -1), r_act,
                  gla_norm[l].reshape(1, -1), bsz, seq, nchunk)
        x1, h2, idx, wts = _merge(
            ya, yb, gates, x2d, w_proj_moba[l].astype(BF16), w_proj_gla[l].astype(BF16),
            w_out[l].astype(BF16), g1, norm_ffn[l], sc2, sh2,
            w_router[l].T, b_router[l].reshape(N_EXPERTS, 1), seq, tm)
        rank, cnt = _rank(idx, tm)
        counts = cnt[:, 0].astype(I32)
        padded = (counts + MOE_ROWS - 1) // MOE_ROWS * MOE_ROWS
        pcum = jnp.cumsum(padded)
        pstart = (pcum - padded).astype(I32)
        n_blocks = (t * TOP_K + MOE_ROWS - 1) // MOE_ROWS + N_EXPERTS
        block_row0 = jnp.arange(n_blocks, dtype=I32) * MOE_ROWS
        block_expert = jnp.minimum(
            jnp.sum((pcum[None, :] <= block_row0[:, None]).astype(I32), axis=1),
            N_EXPERTS - 1).astype(I32)
        n_used = (pcum[-1:] // MOE_ROWS).astype(I32)
        dest = _dest(pstart, idx, rank, _pick(t, (8192, 4096, 2048, 1024, 512, 256)))
        dest_tok = dest.T.reshape(t * TOP_K)
        tg = _pick(seq, (256,))
        zero_info = jnp.concatenate([jnp.maximum(pcum - MOE_ROWS, 0).astype(I32), n_used])
        x_pad = _scatter(dest_tok, zero_info, h2, n_blocks * MOE_ROWS, tg)
        y_pad = _expert(block_expert, n_used, x_pad, w_gate[l], b_gate[l], w_up[l], b_up[l],
                        w_down[l], b_down[l])
        x2d = _combine(dest_tok, y_pad, x1, wts.T, g2, norm_final, seq, tg, l == depth - 1)
    return x2d.reshape(bsz, seq, d)
```

```python
import functools
import math

import numpy as np
import jax
import jax.numpy as jnp
from jax import lax
from jax.experimental import pallas as pl
from jax.experimental.pallas import tpu as pltpu

F32 = jnp.float32
BF16 = jnp.bfloat16
I32 = jnp.int32
HIGHEST = lax.Precision.HIGHEST

D_MODEL = 1024
MOBA_HEADS = 8
MOBA_HEAD_DIM = 64
MOBA_WIDTH = MOBA_HEADS * MOBA_HEAD_DIM
MOBA_BLOCK = 256
MOBA_TOPK = 3
MOBA_MAX_BLOCKS = 32
REL_BUCKETS = 32
REL_MAX_DIST = 128
GLA_HEADS = 4
GLA_KEY_DIM = D_MODEL // 2
GLA_VALUE_DIM = D_MODEL
GLA_DK = GLA_KEY_DIM // GLA_HEADS
GLA_DV = GLA_VALUE_DIM // GLA_HEADS
GLA_GATE_RANK = 16
GLA_GATE_NORMALIZER = 16.0
GLA_CHUNK = 64
N_EXPERTS = 32
TOP_K = 4
D_FF = D_MODEL
SWIGLU_ALPHA = 1.702
SWIGLU_LIMIT = 7.0
MOE_ROWS = 512
ROW_UNROLL = 8
EPS = 1e-6
LANES = 128
SUBLANES = 8
BF16_SUBLANES = 16
ADA_COLS = 768
NEG_BIG = -1e30
LOG2E = math.log2(math.e)
VMEM_LIMIT = 56 * 1024 * 1024


def _cparams(sem, vmem=None):
    return pltpu.CompilerParams(dimension_semantics=sem,
                                vmem_limit_bytes=vmem or VMEM_LIMIT)


def _nt_dot(a, b, **kw):
    return lax.dot_general(a, b, (((1,), (1,)), ((), ())),
                           preferred_element_type=F32, **kw)


def _rms(x):
    return x * lax.rsqrt(jnp.mean(x * x, axis=-1, keepdims=True) + EPS)


def _sigmoid(x):
    return 1.0 / (1.0 + jnp.exp(-x))


U32 = jnp.uint32
_HI16 = 0xFFFF0000


def _pack_halves(x):
    n = x.shape[1] // 2
    lo = pltpu.bitcast(x[:, :n].astype(BF16).astype(F32), U32)
    hi = pltpu.bitcast(x[:, n:].astype(BF16).astype(F32), U32)
    return (hi & U32(_HI16)) | (lo >> 16)


def _unpack_halves(w):
    return (pltpu.bitcast(w << 16, F32), pltpu.bitcast(w & U32(_HI16), F32))


ROW_SLABS = D_MODEL // 2 // LANES


def _store_rows(ref, words):
    m = words.shape[0]
    for c in range(ROW_SLABS):
        ref[pl.ds(c, m, stride=ROW_SLABS), :] = words[:, c * LANES:(c + 1) * LANES]


def _load_rows(ref):
    m = ref.shape[0] // ROW_SLABS
    return jnp.concatenate(
        [ref[pl.ds(c, m, stride=ROW_SLABS), :] for c in range(ROW_SLABS)], axis=1)


def _row(ref, i):
    return ref.at[pl.ds(pl.multiple_of(i * ROW_SLABS, ROW_SLABS), ROW_SLABS)]


def _ada_kernel(c_ref, w_ref, b_ref, o_ref):
    c = c_ref[...]
    s = c * _sigmoid(c)
    o_ref[...] = jnp.dot(s, w_ref[...], precision=HIGHEST,
                         preferred_element_type=F32) + b_ref[...]


def _ada(c, w, b):
    bsz, d = c.shape
    n = w.shape[1]
    rows = -(-bsz // SUBLANES) * SUBLANES
    cp = jnp.zeros((rows, d), F32).at[:bsz].set(c)
    tn = ADA_COLS
    out = pl.pallas_call(
        _ada_kernel,
        grid=(n // tn,),
        in_specs=[pl.BlockSpec((rows, d), lambda j: (0, 0)),
                  pl.BlockSpec((d, tn), lambda j: (0, j)),
                  pl.BlockSpec((1, tn), lambda j: (0, j))],
        out_specs=pl.BlockSpec((rows, tn), lambda j: (0, j)),
        out_shape=jax.ShapeDtypeStruct((rows, n), F32),
        compiler_params=_cparams(("arbitrary",)),
        name="ada",
    )(cp, w, b.reshape(1, n))
    return out[:bsz]


_OFF_QA = 0
_OFF_KA = _OFF_QA + MOBA_WIDTH
_OFF_VA = _OFF_KA + MOBA_WIDTH
_OFF_QKB = _OFF_VA + MOBA_WIDTH
_OFF_VB = _OFF_QKB + 2 * GLA_KEY_DIM
_OFF_GK = _OFF_VB + GLA_VALUE_DIM
_OFF_R = _OFF_GK + LANES
_OFF_G = _OFF_R + GLA_VALUE_DIM
_W_CAT = _OFF_G + 2 * D_MODEL


def _regroup_w_in(w):
    o_gk = 3 * MOBA_WIDTH + 2 * GLA_KEY_DIM + GLA_VALUE_DIM
    gk = jnp.pad(w[:, o_gk:o_gk + GLA_GATE_RANK], ((0, 0), (0, LANES - GLA_GATE_RANK)))
    return jnp.concatenate([w[:, :o_gk], gk, w[:, o_gk + GLA_GATE_RANK:]],
                           axis=1).astype(BF16)


def _inproj_kernel(x_ref, nw_ref, sc_ref, sh_ref, w_ref,
                   qt_ref, ka_ref, vt_ref, qkb_ref, vb_ref, gk_ref, r_ref, g_ref, *, tpb):
    tm = x_ref.shape[0]
    hd = MOBA_HEAD_DIM
    nbt = tm // MOBA_BLOCK
    h = _rms(x_ref[...]) * nw_ref[...]
    h = h * (1.0 + sc_ref[0]) + sh_ref[0]
    hb = h.astype(BF16)

    def mm(a, b):
        return jnp.dot(hb, w_ref[:, a:b], preferred_element_type=F32)

    q_t = (mm(_OFF_QA, _OFF_KA) * (hd ** -0.5 * LOG2E)).T
    v_t = mm(_OFF_VA, _OFF_QKB).T
    k_all = mm(_OFF_KA, _OFF_VA)
    blk0 = (pl.program_id(0) % tpb) * nbt
    lane = lax.broadcasted_iota(I32, (MOBA_BLOCK, LANES), 1)
    ones_rows = jnp.where(
        lax.broadcasted_iota(I32, (MOBA_VT_ROWS - hd, MOBA_BLOCK), 0) == 0, 1.0, 0.0)
    heads_per_tile = LANES // hd
    for hh in range(MOBA_HEADS):
        qt_ref[0, hh] = q_t[hh * hd:(hh + 1) * hd].astype(BF16)
        tile = hh // heads_per_tile
        k_h = k_all[:, tile * LANES:(tile + 1) * LANES]
        if hh % heads_per_tile:
            k_h = pltpu.roll(k_h, LANES - (hh % heads_per_tile) * hd, axis=1)
        for j in range(nbt):
            rows = slice(j * MOBA_BLOCK, (j + 1) * MOBA_BLOCK)
            onehot = jnp.where(lane == hd + blk0 + j, 1.0, 0.0)
            ka_ref[0, hh, j] = jnp.where(lane < hd, k_h[rows], onehot).astype(BF16)
            vt_ref[0, hh, j, 0:hd, :] = v_t[hh * hd:(hh + 1) * hd, rows].astype(BF16)
            vt_ref[0, hh, j, hd:MOBA_VT_ROWS, :] = ones_rows.astype(BF16)
    qkb_ref[...] = mm(_OFF_QKB, _OFF_VB).astype(BF16)
    vb_ref[...] = mm(_OFF_VB, _OFF_GK).astype(BF16)
    gk_ref[...] = mm(_OFF_GK, _OFF_R)
    r = mm(_OFF_R, _OFF_G)
    r_ref[...] = (r * _sigmoid(r)).astype(BF16)
    g_ref[...] = _sigmoid(mm(_OFF_G, _W_CAT)).astype(BF16)


def _inproj(x2d, nw, sc, sh, w_cat, bsz, seq, tm):
    t, d = x2d.shape
    tpb = seq // tm
    nbt = tm // MOBA_BLOCK
    nh, hd = MOBA_HEADS, MOBA_HEAD_DIM
    row = lambda w: pl.BlockSpec((tm, w), lambda i: (i, 0))
    per_b = pl.BlockSpec((1, 1, d), lambda i: (i // tpb, 0, 0))
    rows_out = [(2 * GLA_KEY_DIM, BF16), (GLA_VALUE_DIM, BF16), (LANES, F32),
                (GLA_VALUE_DIM, BF16), (2 * D_MODEL, BF16)]
    return pl.pallas_call(
        functools.partial(_inproj_kernel, tpb=tpb),
        grid=(t // tm,),
        in_specs=[row(d), pl.BlockSpec((1, d), lambda i: (0, 0)), per_b, per_b,
                  pl.BlockSpec((d, _W_CAT), lambda i: (0, 0), pipeline_mode=pl.Buffered(1))],
        out_specs=[pl.BlockSpec((1, nh, hd, tm), lambda i: (i // tpb, 0, 0, i % tpb)),
                   pl.BlockSpec((1, nh, nbt, MOBA_BLOCK, LANES),
                                lambda i: (i // tpb, 0, i % tpb, 0, 0)),
                   pl.BlockSpec((1, nh, nbt, MOBA_VT_ROWS, MOBA_BLOCK),
                                lambda i: (i // tpb, 0, i % tpb, 0, 0))]
                  + [row(w) for w, _ in rows_out],
        out_shape=[jax.ShapeDtypeStruct((bsz, nh, hd, seq), BF16),
                   jax.ShapeDtypeStruct((bsz, nh, seq // MOBA_BLOCK, MOBA_BLOCK, LANES), BF16),
                   jax.ShapeDtypeStruct((bsz, nh, seq // MOBA_BLOCK, MOBA_VT_ROWS, MOBA_BLOCK),
                                        BF16)]
                  + [jax.ShapeDtypeStruct((t, w), dt) for w, dt in rows_out],
        compiler_params=_cparams(("arbitrary",)),
        name="inproj",
    )(x2d, nw.reshape(1, d), sc, sh, w_cat)


def _t5_bucket_np(n):
    n = np.maximum(n, 0)
    max_exact = REL_BUCKETS // 2
    nf = np.maximum(n, max_exact).astype(np.float32)
    large = max_exact + (np.log(nf / max_exact) / math.log(REL_MAX_DIST / max_exact)
                         * (REL_BUCKETS - max_exact)).astype(np.int32)
    large = np.minimum(large, REL_BUCKETS - 1)
    return np.where(n < max_exact, n, large).astype(np.int32)


def _bucket_table():
    kj = np.arange(MOBA_BLOCK)[:, None]
    qi = np.arange(2 * MOBA_BLOCK)[None, :] % MOBA_BLOCK
    prev = np.arange(2 * MOBA_BLOCK)[None, :] < MOBA_BLOCK
    bucket = _t5_bucket_np(qi - kj + np.where(prev, MOBA_BLOCK, 0))
    return np.where(prev | (kj <= qi), bucket, -1).astype(np.int32)


def _bias_kernel(rb_ref, bucket_ref, o_ref):
    h = pl.program_id(0)
    bk = bucket_ref[...]
    far = rb_ref[(REL_BUCKETS - 1) * MOBA_HEADS + h]
    acc = jnp.zeros(bk.shape, F32)
    for b in range(REL_BUCKETS):
        acc = jnp.where(bk == b, rb_ref[b * MOBA_HEADS + h] - far, acc)
    o_ref[0] = jnp.where(bk < 0, NEG_BIG, acc * LOG2E)


def _bias_tiles(rel_bias):
    bucket = jnp.asarray(_bucket_table())
    return pl.pallas_call(
        _bias_kernel,
        grid=(MOBA_HEADS,),
        in_specs=[pl.BlockSpec(memory_space=pltpu.SMEM),
                  pl.BlockSpec(bucket.shape, lambda h: (0, 0))],
        out_specs=pl.BlockSpec((1,) + bucket.shape, lambda h: (h, 0, 0)),
        out_shape=jax.ShapeDtypeStruct((MOBA_HEADS,) + bucket.shape, F32),
        compiler_params=_cparams(("arbitrary",)),
        name="bias",
    )(rel_bias.reshape(-1), bucket)


MOBA_HEADS_PER_STEP = 8
MOBA_FAR_GROUP = 2
MOBA_QBLOCKS_PER_STEP = 4
MOBA_FAR_LAGS = (0, 3, 6)
MOBA_NEAR_LAGS = (0, 3, 6, 9)
MOBA_VT_ROWS = MOBA_HEAD_DIM + BF16_SUBLANES


def _moba_kernel(qt_ref, k_ref, vt_ref, bias_ref, o_ref, kmean_ref, qa_ref):
    blk = MOBA_BLOCK
    nb = MOBA_MAX_BLOCKS
    hd = MOBA_HEAD_DIM
    hp = MOBA_HEADS_PER_STEP
    grp = MOBA_FAR_GROUP
    nq = MOBA_QBLOCKS_PER_STEP
    pair = pl.program_id(2)
    neg = -jnp.inf
    items = [(h, j) for j in range(nq) for h in range(hp)]

    @pl.when(pair == 0)
    def _():
        kmean_ref[...] = jnp.zeros(kmean_ref.shape, F32)

    for h in range(hp):
        for j in range(nq):
            kmean_ref[h, pl.ds(pair * nq + j, 1), :] = jnp.mean(
                k_ref[0, h, pair * nq + j].astype(F32), axis=0, keepdims=True)

    row = lax.broadcasted_iota(I32, (nb, blk), 0)
    rowf = row.astype(F32)
    pad = jnp.zeros((LANES - hd, blk), BF16)
    pad_hi = jnp.zeros((LANES - hd - nb, blk), BF16)

    def skewed(stages, lags, todo=None):
        todo = list(range(len(items))) if todo is None else todo
        vals = {}
        for step in range(len(todo) + lags[-1]):
            for stage, lag in zip(stages, lags):
                pos = step - lag
                if 0 <= pos < len(todo):
                    vals[todo[pos]] = stage(todo[pos], vals.get(todo[pos]))
        return vals

    def select(n, _):
        h, j = items[n]
        qi = pair * nq + j
        qt = qt_ref[0, h, :, j * blk:(j + 1) * blk]
        km = kmean_ref[h, :, 0:hd]
        km_hi = km.astype(BF16)
        km_lo = (km - km_hi.astype(F32)).astype(BF16)
        gate = (jnp.dot(km_hi, qt, preferred_element_type=F32)
                + jnp.dot(km_lo, qt, preferred_element_type=F32))
        g = jnp.where(row < qi, gate, neg)
        sel = jnp.zeros((nb, blk), F32)
        for _ in range(MOBA_TOPK):
            mx = jnp.max(g, axis=0, keepdims=True)
            first = jnp.min(jnp.where(g == mx, rowf, float(nb)), axis=0, keepdims=True)
            pick = rowf == jnp.where(mx > neg, first, -1.0)
            sel = jnp.where(pick, 1.0, sel)
            g = jnp.where(pick, neg, g)
        mask_prev = jnp.where(sel > 0.0, jnp.where(row == qi - 1, 0.0, NEG_BIG), NEG_BIG)
        mask_far = jnp.where(sel > 0.0, jnp.where(row < qi - 1, 0.0, NEG_BIG), NEG_BIG)
        qa_ref[n] = jnp.concatenate([qt, mask_far.astype(BF16), pad_hi], axis=0)
        return (jnp.concatenate([qt, pad], axis=0),
                jnp.concatenate([qt, mask_prev.astype(BF16), pad_hi], axis=0))

    def own_prev(n):
        h, j = items[n]
        qi = pair * nq + j
        return h, qi, jnp.maximum(qi - 1, 0)

    def near_scores(n, qa):
        h, qi, prev_j = own_prev(n)
        qa_own, qa_prev = qa
        s_own = jnp.dot(k_ref[0, h, qi], qa_own, preferred_element_type=F32)
        s_prev = jnp.dot(k_ref[0, h, prev_j], qa_prev, preferred_element_type=F32)
        return s_own, s_prev

    def near_softmax(n, ss):
        h = items[n][0]
        s_own, s_prev = ss
        s = jnp.concatenate([s_own + bias_ref[h, :, blk:2 * blk],
                             s_prev + bias_ref[h, :, 0:blk]], axis=0)
        m0 = jnp.max(s, axis=0, keepdims=True)
        return m0, jnp.exp2(s - m0)

    def near_pv(n, mp):
        h, qi, prev_j = own_prev(n)
        m0, p = mp
        pb = p.astype(BF16)
        acc = (jnp.dot(vt_ref[0, h, qi], pb[0:blk], preferred_element_type=F32)
               + jnp.dot(vt_ref[0, h, prev_j], pb[blk:2 * blk], preferred_element_type=F32))
        return m0, acc

    near = skewed([select, near_scores, near_softmax, near_pv], MOBA_NEAR_LAGS)
    states = tuple(near[n] for n in range(len(items)))

    def far(gi, states, todo=None):
        j0 = gi * grp

        def qk(n, _):
            kt = k_ref[0, items[n][0], pl.ds(j0, grp)].reshape(grp * blk, LANES)
            return jnp.dot(kt, qa_ref[n], preferred_element_type=F32)

        def softmax(n, s):
            m_old = states[n][0]
            m_new = jnp.maximum(m_old, jnp.max(s, axis=0, keepdims=True))
            return m_new, jnp.exp2(m_old - m_new), jnp.exp2(s - m_new)

        def pv(n, sm):
            m_new, a, p = sm
            pb = p.astype(BF16)
            tot = a * states[n][1]
            for i in range(grp):
                tot = tot + jnp.dot(vt_ref[0, items[n][0], j0 + i], pb[i * blk:(i + 1) * blk],
                                    preferred_element_type=F32)
            return m_new, tot

        new = skewed([qk, softmax, pv], MOBA_FAR_LAGS, todo)
        return tuple(new.get(n, states[n]) for n in range(len(items)))

    sub = nq // grp
    states = lax.fori_loop(0, pair * sub, far, tuple(states))
    for extra in range(1, sub):
        later = [n for n, (_, j) in enumerate(items) if j // grp >= extra]
        states = far(pair * sub + extra - 1, states, later)

    for n, (h, j) in enumerate(items):
        acc = states[n][1]
        o_ref[0, h * hd:(h + 1) * hd, j * blk:(j + 1) * blk] = (
            acc[0:hd] / acc[hd:hd + 1]).astype(o_ref.dtype)


def _moba(qt, k_aug, vt, bias):
    bsz, nh, hd, s = qt.shape
    blk = MOBA_BLOCK
    hp = MOBA_HEADS_PER_STEP
    nq = MOBA_QBLOCKS_PER_STEP
    nblk = s // blk
    assert nh % hp == 0 and nblk % nq == 0 and nq % MOBA_FAR_GROUP == 0
    return pl.pallas_call(
        _moba_kernel,
        grid=(bsz, nh // hp, nblk // nq),
        in_specs=[pl.BlockSpec((1, hp, hd, nq * blk), lambda b, g, i: (b, g, 0, i)),
                  pl.BlockSpec((1, hp, nblk, blk, LANES), lambda b, g, i: (b, g, 0, 0, 0),
                               pipeline_mode=pl.Buffered(1)),
                  pl.BlockSpec((1, hp, nblk, MOBA_VT_ROWS, blk), lambda b, g, i: (b, g, 0, 0, 0),
                               pipeline_mode=pl.Buffered(1)),
                  pl.BlockSpec((hp, blk, 2 * blk), lambda b, g, i: (g, 0, 0),
                               pipeline_mode=pl.Buffered(1))],
        out_specs=pl.BlockSpec((1, hp * hd, nq * blk), lambda b, g, i: (b, g, i)),
        out_shape=jax.ShapeDtypeStruct((bsz, nh * hd, s), BF16),
        scratch_shapes=[pltpu.VMEM((hp, MOBA_MAX_BLOCKS, LANES), F32),
                        pltpu.VMEM((hp * nq, LANES, blk), BF16)],
        compiler_params=_cparams(("arbitrary", "arbitrary", "arbitrary")),
        name="moba",
    )(qt, k_aug, vt, bias)


def _gla_kernel(q_ref, k_ref, v_ref, gk_ref, wup_ref, bgk_ref, r_ref, gn_ref, o_ref,
                state_ref, *, nchunk):
    ch = GLA_CHUNK
    tc = nchunk * ch
    dk, dv = GLA_DK, GLA_DV

    @pl.when(pl.program_id(1) == 0)
    def _():
        state_ref[...] = jnp.zeros(state_ref.shape, F32)

    rin = lax.broadcasted_iota(I32, (tc, dk), 0) & (ch - 1)
    causal = (lax.broadcasted_iota(I32, (ch, ch), 1) <= lax.broadcasted_iota(I32, (ch, ch), 0))
    eye = (lax.broadcasted_iota(I32, (dk, dk), 0) == lax.broadcasted_iota(I32, (dk, dk), 1))
    chunks = [slice(n * ch, (n + 1) * ch) for n in range(nchunk)]
    gk = gk_ref[...]
    gk_hi = gk.astype(BF16)
    gk_lo = (gk - gk_hi.astype(F32)).astype(BF16)

    def prep(h, _):
        ks = slice(h * dk, (h + 1) * dk)
        w = wup_ref[:, ks]
        w_hi = w.astype(BF16)
        w_lo = (w - w_hi.astype(F32)).astype(BF16)
        z = (jnp.dot(gk_hi, w_hi, preferred_element_type=F32)
             + jnp.dot(gk_lo, w_hi, preferred_element_type=F32)
             + jnp.dot(gk_hi, w_lo, preferred_element_type=F32) + bgk_ref[:, ks])
        log_a = ((jnp.minimum(z, 0.0) - jnp.log(1.0 + jnp.exp(-jnp.abs(z))))
                 / GLA_GATE_NORMALIZER)
        b = log_a
        sh = 1
        while sh < ch:
            b = b + jnp.where(rin >= sh, pltpu.roll(b, sh, axis=0), 0.0)
            sh *= 2
        q = q_ref[:, ks].astype(F32) * (dk ** -0.5)
        k = k_ref[:, ks].astype(F32)
        q_g = (q * jnp.exp(b)).astype(BF16)
        k_g = (k * jnp.exp(-b)).astype(BF16)
        b3 = b.reshape(nchunk, ch, dk)
        b_last = b3[:, ch - 1:ch, :]
        k_end = (k * jnp.exp(jnp.broadcast_to(b_last, b3.shape) - b3).reshape(tc, dk)
                 ).astype(BF16)
        return q_g, k_g, k_end, jnp.exp(b_last)

    def local(h, pre):
        q_g, k_g, k_end, decay = pre
        o_intra, kv, decay_col = [], [], []
        for n, sl in enumerate(chunks):
            v_c = v_ref[sl, h * dv:(h + 1) * dv]
            att = jnp.where(causal, _nt_dot(q_g[sl], k_g[sl]), 0.0)
            o_intra.append(jnp.dot(att.astype(BF16), v_c, preferred_element_type=F32))
            kv.append(lax.dot_general(k_end[sl], v_c, (((0,), (0,)), ((), ())),
                                      preferred_element_type=F32))
            decay_col.append(jnp.sum(
                jnp.where(eye, jnp.broadcast_to(decay[n], (dk, dk)), 0.0),
                axis=1, keepdims=True))
        return q_g, o_intra, kv, decay_col

    def chain(h, loc):
        q_g, o_intra, kv, decay_col = loc
        state = state_ref[h]
        outs = []
        for n, sl in enumerate(chunks):
            outs.append(o_intra[n] + jnp.dot(q_g[sl], state.astype(BF16),
                                             preferred_element_type=F32))
            state = decay_col[n] * state + kv[n]
        state_ref[h] = state
        return jnp.concatenate(outs, axis=0)

    def finish(h, o):
        vs = slice(h * dv, (h + 1) * dv)
        o_ref[:, vs] = (_rms(o) * gn_ref[...] * r_ref[:, vs].astype(F32)).astype(o_ref.dtype)
        return None

    stages = [prep, local, chain, finish]
    vals = [None] * GLA_HEADS
    for step in range(GLA_HEADS + len(stages) - 1):
        for si, stage in enumerate(stages):
            h = step - si
            if 0 <= h < GLA_HEADS:
                vals[h] = stage(h, vals[h])


def _gla(qkb, vb, gk, wup, bgk, r_act, gn, bsz, seq, nchunk):
    t = qkb.shape[0]
    tc = nchunk * GLA_CHUNK
    nc = seq // tc
    rowblk = lambda w, off: pl.BlockSpec((tc, w), lambda b, c: (b * nc + c, off))
    full = lambda a: pl.BlockSpec(a.shape, lambda b, c: (0, 0))
    return pl.pallas_call(
        functools.partial(_gla_kernel, nchunk=nchunk),
        grid=(bsz, nc),
        in_specs=[rowblk(GLA_KEY_DIM, 0), rowblk(GLA_KEY_DIM, 1), rowblk(GLA_VALUE_DIM, 0),
                  rowblk(LANES, 0), full(wup), full(bgk), rowblk(GLA_VALUE_DIM, 0), full(gn)],
        out_specs=rowblk(GLA_VALUE_DIM, 0),
        out_shape=jax.ShapeDtypeStruct((t, GLA_VALUE_DIM), BF16),
        scratch_shapes=[pltpu.VMEM((GLA_HEADS, GLA_DK, GLA_DV), F32)],
        compiler_params=_cparams(("arbitrary", "arbitrary")),
        name="gla",
    )(qkb, qkb, vb, gk, wup, bgk, r_act, gn)


def _merge_kernel(ya_ref, yb_ref, g_ref, x_ref, wpa_ref, wpb_ref, wout_ref, g1_ref,
                  nw_ref, sc_ref, sh_ref, wr_ref, br_ref,
                  x1_ref, h2_ref, idx_ref, wts_ref):
    pa = lax.dot_general(ya_ref[0], wpa_ref[...], (((0,), (0,)), ((), ())),
                         preferred_element_type=F32)
    pb = jnp.dot(yb_ref[...], wpb_ref[...], preferred_element_type=F32)
    mixed = (g_ref[:, 0:D_MODEL].astype(F32) * pa
             + g_ref[:, D_MODEL:2 * D_MODEL].astype(F32) * pb)
    y = jnp.dot(mixed.astype(BF16), wout_ref[...], preferred_element_type=F32)
    x1 = x_ref[...] + g1_ref[0] * y
    x1_ref[...] = x1
    h2 = _rms(x1) * nw_ref[...]
    h2 = h2 * (1.0 + sc_ref[0]) + sh_ref[0]
    _store_rows(h2_ref, _pack_halves(h2))
    h_hi = h2.astype(BF16)
    h_lo = (h2 - h_hi.astype(F32)).astype(BF16)
    w = wr_ref[...]
    w_hi = w.astype(BF16)
    w_lo = (w - w_hi.astype(F32)).astype(BF16)
    logits = (_nt_dot(w_hi, h_hi) + _nt_dot(w_hi, h_lo) + _nt_dot(w_lo, h_hi)
              + br_ref[...])
    rowf = lax.broadcasted_iota(I32, logits.shape, 0).astype(F32)
    vals, idxs = [], []
    cur = logits
    for _ in range(TOP_K):
        mx = jnp.max(cur, axis=0, keepdims=True)
        first = jnp.min(jnp.where(cur == mx, rowf, float(N_EXPERTS)), axis=0, keepdims=True)
        vals.append(mx)
        idxs.append(first)
        cur = jnp.where(rowf == first, -jnp.inf, cur)
    es = [jnp.exp(v - vals[0]) for v in vals]
    tot = es[0]
    for e in es[1:]:
        tot = tot + e
    idx_ref[...] = jnp.concatenate(idxs, axis=0).astype(I32)
    wts_ref[...] = jnp.concatenate([e / tot for e in es], axis=0)


def _merge(ya, yb, gates, x2d, wpa, wpb, wout, g1, nw, sc, sh, wr_t, br, seq, tm):
    t, d = x2d.shape
    tpb = seq // tm
    row = lambda w: pl.BlockSpec((tm, w), lambda i: (i, 0))
    full = lambda a: pl.BlockSpec(a.shape, lambda i: (0,) * a.ndim)
    per_b = pl.BlockSpec((1, 1, d), lambda i: (i // tpb, 0, 0))
    colblk = pl.BlockSpec((TOP_K, tm), lambda i: (0, i))
    return pl.pallas_call(
        _merge_kernel,
        grid=(t // tm,),
        in_specs=[pl.BlockSpec((1, MOBA_WIDTH, tm), lambda i: (i // tpb, 0, i % tpb)),
                  row(GLA_VALUE_DIM), row(2 * D_MODEL), row(d),
                  full(wpa), full(wpb), full(wout), per_b,
                  pl.BlockSpec((1, d), lambda i: (0, 0)), per_b, per_b,
                  full(wr_t), full(br)],
        out_specs=[row(d), pl.BlockSpec((tm * ROW_SLABS, LANES), lambda i: (i, 0)),
                   colblk, colblk],
        out_shape=[jax.ShapeDtypeStruct((t, d), F32),
                   jax.ShapeDtypeStruct((t * ROW_SLABS, LANES), U32),
                   jax.ShapeDtypeStruct((TOP_K, t), I32), jax.ShapeDtypeStruct((TOP_K, t), F32)],
        compiler_params=_cparams(("arbitrary",)),
        name="merge",
    )(ya, yb, gates, x2d, wpa, wpb, wout, g1, nw.reshape(1, d), sc, sh, wr_t, br)


def _rank_kernel(idx_ref, rank_ref, cnt_ref, carry_ref):
    tm = idx_ref.shape[1]

    @pl.when(pl.program_id(0) == 0)
    def _():
        carry_ref[...] = jnp.zeros(carry_ref.shape, F32)

    rows = lax.broadcasted_iota(I32, (N_EXPERTS, tm), 0)
    before = (lax.broadcasted_iota(I32, (tm, tm), 0)
              < lax.broadcasted_iota(I32, (tm, tm), 1))
    upper = jnp.where(before, 1.0, 0.0).astype(BF16)
    carry = carry_ref[:, 0:1]
    ranks = []
    for k in range(TOP_K):
        onehot = idx_ref[k:k + 1, :] == rows
        onef = jnp.where(onehot, 1.0, 0.0)
        earlier = jnp.dot(onef.astype(BF16), upper, preferred_element_type=F32) + carry
        ranks.append(jnp.sum(jnp.where(onehot, earlier, 0.0), axis=0, keepdims=True))
        carry = carry + jnp.sum(onef, axis=1, keepdims=True)
    rank_ref[...] = jnp.concatenate(ranks, axis=0).astype(I32)
    total = jnp.broadcast_to(carry, carry_ref.shape)
    carry_ref[...] = total
    cnt_ref[...] = total


def _rank(idx, tm):
    t = idx.shape[1]
    return pl.pallas_call(
        _rank_kernel,
        grid=(t // tm,),
        in_specs=[pl.BlockSpec((TOP_K, tm), lambda i: (0, i))],
        out_specs=[pl.BlockSpec((TOP_K, tm), lambda i: (0, i)),
                   pl.BlockSpec((N_EXPERTS, LANES), lambda i: (0, 0))],
        out_shape=[jax.ShapeDtypeStruct((TOP_K, t), I32),
                   jax.ShapeDtypeStruct((N_EXPERTS, LANES), F32)],
        scratch_shapes=[pltpu.VMEM((N_EXPERTS, LANES), F32)],
        compiler_params=_cparams(("arbitrary",)),
        name="rank",
    )(idx)


def _dest_kernel(pstart_ref, idx_ref, rank_ref, dest_ref):
    idx = idx_ref[...]
    off = jnp.zeros(idx.shape, I32)
    for e in range(N_EXPERTS):
        off = jnp.where(idx == e, pstart_ref[e], off)
    dest_ref[...] = rank_ref[...] + off


def _dest(pstart, idx, rank, tm):
    t = idx.shape[1]
    blk = pl.BlockSpec((TOP_K, tm), lambda i: (0, i))
    return pl.pallas_call(
        _dest_kernel,
        grid=(t // tm,),
        in_specs=[pl.BlockSpec(memory_space=pltpu.SMEM), blk, blk],
        out_specs=blk,
        out_shape=jax.ShapeDtypeStruct((TOP_K, t), I32),
        compiler_params=_cparams(("arbitrary",)),
        name="dest",
    )(pstart, idx, rank)


def _scatter_kernel(dest_ref, zrow_ref, h_ref, xout_hbm, zbuf, stage, sems, zsem, *, tm):
    @pl.when(pl.program_id(0) == 0)
    def _():
        zbuf[...] = jnp.zeros(zbuf.shape, zbuf.dtype)

        def zero_rows(row0):
            line0 = pl.multiple_of(row0 * ROW_SLABS, MOE_ROWS * ROW_SLABS)
            return pltpu.make_async_copy(
                zbuf, xout_hbm.at[pl.ds(line0, MOE_ROWS * ROW_SLABS)], zsem)

        for e in range(N_EXPERTS):
            zero_rows(zrow_ref[e]).start()
        for e in range(N_EXPERTS):
            zero_rows(zrow_ref[e]).wait()

        def start_tail(j, carry):
            zero_rows(j * MOE_ROWS).start()
            return carry

        def wait_tail(j, carry):
            zero_rows(j * MOE_ROWS).wait()
            return carry

        n_all = xout_hbm.shape[0] // (MOE_ROWS * ROW_SLABS)
        lax.fori_loop(zrow_ref[N_EXPERTS], n_all, start_tail, 0)
        lax.fori_loop(zrow_ref[N_EXPERTS], n_all, wait_tail, 0)

    i = pl.program_id(0)
    slot = i % 2
    src = stage.at[slot]
    src[...] = h_ref[...]

    def issue(g, carry):
        base = pl.multiple_of(g * ROW_UNROLL, ROW_UNROLL)
        for j in range(ROW_UNROLL):
            for k in range(TOP_K):
                d = dest_ref[k, base + j]
                pltpu.make_async_copy(_row(src, base + j), _row(xout_hbm, d),
                                      sems.at[slot]).start(priority=k % 2)
        return carry

    lax.fori_loop(0, tm // ROW_UNROLL, issue, 0)

    def wait_tile(s):
        for k in range(TOP_K):
            pltpu.make_async_copy(stage.at[s], xout_hbm.at[pl.ds(0, tm * ROW_SLABS)],
                                  sems.at[s]).wait()

    @pl.when(i > 0)
    def _():
        wait_tile(1 - slot)

    @pl.when(i == pl.num_programs(0) - 1)
    def _():
        wait_tile(slot)


def _scatter(dest, last_block_row, h2, n_pad, tm):
    t = h2.shape[0] // ROW_SLABS
    return pl.pallas_call(
        functools.partial(_scatter_kernel, tm=tm),
        grid=(t // tm,),
        in_specs=[pl.BlockSpec((TOP_K, tm), lambda i: (0, i), memory_space=pltpu.SMEM),
                  pl.BlockSpec(memory_space=pltpu.SMEM),
                  pl.BlockSpec((tm * ROW_SLABS, LANES), lambda i: (i, 0))],
        out_specs=pl.BlockSpec(memory_space=pl.ANY),
        out_shape=jax.ShapeDtypeStruct((n_pad * ROW_SLABS, LANES), h2.dtype),
        scratch_shapes=[pltpu.VMEM((MOE_ROWS * ROW_SLABS, LANES), h2.dtype),
                        pltpu.VMEM((2, tm * ROW_SLABS, LANES), h2.dtype),
                        pltpu.SemaphoreType.DMA((2,)), pltpu.SemaphoreType.DMA],
        compiler_params=_cparams(("arbitrary",)),
        name="scatter",
    )(dest, last_block_row, h2)


def _expert_kernel(be_ref, nused_ref, x_ref, wg_ref, bg_ref, wu_ref, bu_ref, wd_ref, bd_ref,
                   o_ref, wgb_ref, wub_ref, wdb_ref):
    i = pl.program_id(0)
    prev = be_ref[jnp.maximum(i - 1, 0)]

    @pl.when(jnp.logical_or(i == 0, be_ref[i] != prev))
    def _():
        wgb_ref[...] = wg_ref[0].astype(BF16)
        wub_ref[...] = wu_ref[0].astype(BF16)
        wdb_ref[...] = wd_ref[0].astype(BF16)

    @pl.when(i < nused_ref[0])
    def _():
        xb = jnp.concatenate([v.astype(BF16) for v in _unpack_halves(_load_rows(x_ref))],
                             axis=1)
        g = jnp.minimum(jnp.dot(xb, wgb_ref[...], preferred_element_type=F32) + bg_ref[0],
                        SWIGLU_LIMIT)
        u = jnp.clip(jnp.dot(xb, wub_ref[...], preferred_element_type=F32) + bu_ref[0],
                     -SWIGLU_LIMIT, SWIGLU_LIMIT)
        act = g * _sigmoid(SWIGLU_ALPHA * g) * (u + 1.0)
        _store_rows(o_ref, _pack_halves(
            jnp.dot(act.astype(BF16), wdb_ref[...], preferred_element_type=F32) + bd_ref[0]))

    @pl.when(i >= nused_ref[0])
    def _():
        o_ref[...] = jnp.zeros(o_ref.shape, o_ref.dtype)


def _expert(block_expert, n_used, x_pad, wg, bg, wu, bu, wd, bd):
    lines = MOE_ROWS * ROW_SLABS
    _, d, f = wg.shape
    nblk = x_pad.shape[0] // lines
    wspec = lambda a, b: pl.BlockSpec((1, a, b), lambda i, be, nu: (be[i], 0, 0))
    grid_spec = pltpu.PrefetchScalarGridSpec(
        num_scalar_prefetch=2,
        grid=(nblk,),
        in_specs=[pl.BlockSpec((lines, LANES),
                               lambda i, be, nu: (jnp.minimum(i, nu[0] - 1), 0)),
                  wspec(d, f), wspec(1, f), wspec(d, f), wspec(1, f), wspec(f, d), wspec(1, d)],
        out_specs=pl.BlockSpec((lines, LANES), lambda i, be, nu: (i, 0)),
        scratch_shapes=[pltpu.VMEM((d, f), BF16), pltpu.VMEM((d, f), BF16),
                        pltpu.VMEM((f, d), BF16)])
    return pl.pallas_call(
        _expert_kernel,
        grid_spec=grid_spec,
        out_shape=jax.ShapeDtypeStruct(x_pad.shape, U32),
        compiler_params=_cparams(("arbitrary",)),
        name="expert",
    )(block_expert, n_used, x_pad, wg, bg.reshape(N_EXPERTS, 1, f), wu,
      bu.reshape(N_EXPERTS, 1, f), wd, bd.reshape(N_EXPERTS, 1, d))


def _combine_kernel(dest_ref, y_hbm, x1_ref, wts_ref, g2_ref, nf_ref, o_ref, ybuf_a, ybuf_b,
                    sems, *, tm, final, n_tiles):
    i = pl.program_id(0)
    bufs = (ybuf_a, ybuf_b)

    def issue(slot):
        for tt in range(tm):
            for k in range(TOP_K):
                d = dest_ref[k, tt]
                pltpu.make_async_copy(_row(y_hbm, d), _row(bufs[slot].at[k], tt),
                                      sems.at[slot]).start(priority=k % 2)

    def wait(slot):
        for k in range(TOP_K):
            pltpu.make_async_copy(y_hbm.at[pl.ds(0, tm * ROW_SLABS)], bufs[slot].at[k],
                                  sems.at[slot]).wait()

    def finish(slot):
        w = wts_ref[...].T
        lo, hi = None, None
        for k in range(TOP_K):
            y_lo, y_hi = _unpack_halves(_load_rows(bufs[slot].at[k]))
            lo = w[:, k:k + 1] * y_lo + (0.0 if lo is None else lo)
            hi = w[:, k:k + 1] * y_hi + (0.0 if hi is None else hi)
        moe = jnp.concatenate([lo, hi], axis=1)
        x2 = x1_ref[...] + g2_ref[0] * moe
        o_ref[...] = _rms(x2) * nf_ref[...] if final else x2

    @pl.when(i == 0)
    def _():
        issue(0)

    for parity in range(2):
        @pl.when(jnp.logical_and(jnp.logical_and(i > 0, i < n_tiles), i % 2 == parity))
        def _(parity=parity):
            wait(1 - parity)
            issue(parity)
            finish(1 - parity)

    @pl.when(i == n_tiles)
    def _():
        wait((n_tiles - 1) % 2)
        finish((n_tiles - 1) % 2)


def _combine(dest, y_pad, x1, wts, g2, nf, seq, tm, final):
    t, d = x1.shape
    tpb = seq // tm
    n_tiles = t // tm
    lag = lambda i: jnp.maximum(i - 1, 0)
    return pl.pallas_call(
        functools.partial(_combine_kernel, tm=tm, final=final, n_tiles=n_tiles),
        grid=(n_tiles + 1,),
        in_specs=[pl.BlockSpec((TOP_K, tm), lambda i: (0, jnp.minimum(i, n_tiles - 1)),
                               memory_space=pltpu.SMEM),
                  pl.BlockSpec(memory_space=pl.ANY),
                  pl.BlockSpec((tm, d), lambda i: (lag(i), 0)),
                  pl.BlockSpec((TOP_K, tm), lambda i: (0, lag(i))),
                  pl.BlockSpec((1, 1, d), lambda i: (lag(i) // tpb, 0, 0)),
                  pl.BlockSpec((1, d), lambda i: (0, 0))],
        out_specs=pl.BlockSpec((tm, d), lambda i: (lag(i), 0)),
        out_shape=jax.ShapeDtypeStruct((t, d), F32),
        scratch_shapes=[pltpu.VMEM((TOP_K, tm * ROW_SLABS, LANES), y_pad.dtype),
                        pltpu.VMEM((TOP_K, tm * ROW_SLABS, LANES), y_pad.dtype),
                        pltpu.SemaphoreType.DMA((2,))],
        compiler_params=_cparams(("arbitrary",)),
        name="combine",
    )(dest, y_pad, x1, wts, g2, nf.reshape(1, d))


def _pick(n, cands):
    for c in cands:
        if n % c == 0:
            return c
    raise ValueError(f"no tile in {cands} divides {n}")


def kernel(x, c, rel_bias, w_ada, b_ada, norm_mix, w_in, w_gk_up, b_gk, gla_norm,
           w_proj_moba, w_proj_gla, w_out, norm_ffn, w_router, b_router,
           w_gate, b_gate, w_up, b_up, w_down, b_down, norm_final):
    bsz, seq, d = x.shape
    depth = w_ada.shape[0]
    assert d == D_MODEL and seq % MOBA_BLOCK == 0 and seq // MOBA_BLOCK <= MOBA_MAX_BLOCKS
    t = bsz * seq
    tm = _pick(seq, (512, 256))
    nchunk = _pick(seq // GLA_CHUNK, (16, 8, 4))
    n_blk = seq // MOBA_BLOCK
    x2d = x.reshape(t, d)
    bias = _bias_tiles(rel_bias)
    per_b = lambda v: v.reshape(bsz, 1, d)

    for l in range(depth):
        mod = _ada(c, w_ada[l], b_ada[l])
        sh1, sc1, g1, sh2, sc2, g2 = [per_b(m) for m in jnp.split(mod, 6, axis=-1)]
        qt, k_aug, vt, qkb, vb, gk, r_act, gates = _inproj(
            x2d, norm_mix[l], sc1, sh1, _regroup_w_in(w_in[l]), bsz, seq, tm)
        ya = _moba(qt, k_aug, vt, bias)
        wup = jnp.pad(w_gk_up[l], ((0, LANES - GLA_GATE_RANK), (0, 0)))
        yb = _gla(qkb, vb, gk, wup, b_gk[l].reshape(1, -1), r_act,
                  gla_norm[l].reshape(1, -1), bsz, seq, nchunk)
        x1, h2, idx, wts = _merge(
            ya, yb, gates, x2d, w_proj_moba[l].astype(BF16), w_proj_gla[l].astype(BF16),
            w_out[l].astype(BF16), g1, norm_ffn[l], sc2, sh2,
            w_router[l].T, b_router[l].reshape(N_EXPERTS, 1), seq, tm)
        rank, cnt = _rank(idx, tm)
        counts = cnt[:, 0].astype(I32)
        padded = (counts + MOE_ROWS - 1) // MOE_ROWS * MOE_ROWS
        pcum = jnp.cumsum(padded)
        pstart = (pcum - padded).astype(I32)
        n_blocks = (t * TOP_K + MOE_ROWS - 1) // MOE_ROWS + N_EXPERTS
        block_row0 = jnp.arange(n_blocks, dtype=I32) * MOE_ROWS
        block_expert = jnp.minimum(
            jnp.sum((pcum[None, :] <= block_row0[:, None]).astype(I32), axis=1),
            N_EXPERTS - 1).astype(I32)
        n_used = (pcum[-1:] // MOE_ROWS).astype(I32)
        dest = _dest(pstart, idx, rank, _pick(t, (8192, 4096, 2048, 1024, 512, 256)))
        tg = _pick(seq, (256,))
        zero_info = jnp.concatenate([jnp.maximum(pcum - MOE_ROWS, 0).astype(I32), n_used])
        x_pad = _scatter(dest, zero_info, h2, n_blocks * MOE_ROWS, tg)
        y_pad = _expert(block_expert, n_used, x_pad, w_gate[l], b_gate[l], w_up[l], b_up[l],
                        w_down[l], b_down[l])
        x2d = _combine(dest, y_pad, x1, wts, g2, norm_final, seq, tg, l == depth - 1)
    return x2d.reshape(bsz, seq, d)
```

```python
import functools
import math

import numpy as np
import jax
import jax.numpy as jnp
from jax import lax
from jax.experimental import pallas as pl
from jax.experimental.pallas import tpu as pltpu

F32 = jnp.float32
BF16 = jnp.bfloat16
I32 = jnp.int32
HIGHEST = lax.Precision.HIGHEST

D_MODEL = 1024
MOBA_HEADS = 8
MOBA_HEAD_DIM = 64
MOBA_WIDTH = MOBA_HEADS * MOBA_HEAD_DIM
MOBA_BLOCK = 256
MOBA_TOPK = 3
MOBA_MAX_BLOCKS = 32
REL_BUCKETS = 32
REL_MAX_DIST = 128
GLA_HEADS = 4
GLA_KEY_DIM = D_MODEL // 2
GLA_VALUE_DIM = D_MODEL
GLA_DK = GLA_KEY_DIM // GLA_HEADS
GLA_DV = GLA_VALUE_DIM // GLA_HEADS
GLA_GATE_RANK = 16
GLA_GATE_NORMALIZER = 16.0
GLA_CHUNK = 64
N_EXPERTS = 32
TOP_K = 4
D_FF = D_MODEL
SWIGLU_ALPHA = 1.702
SWIGLU_LIMIT = 7.0
MOE_ROWS = 512
ROW_UNROLL = 8
EPS = 1e-6
LANES = 128
SUBLANES = 8
BF16_SUBLANES = 16
ADA_COLS = 768
NEG_BIG = -1e30
LOG2E = math.log2(math.e)
VMEM_LIMIT = 56 * 1024 * 1024


def _cparams(sem, vmem=None):
    return pltpu.CompilerParams(dimension_semantics=sem,
                                vmem_limit_bytes=vmem or VMEM_LIMIT)


def _nt_dot(a, b, **kw):
    return lax.dot_general(a, b, (((1,), (1,)), ((), ())),
                           preferred_element_type=F32, **kw)


def _rms(x):
    return x * lax.rsqrt(jnp.mean(x * x, axis=-1, keepdims=True) + EPS)


def _sigmoid(x):
    return 1.0 / (1.0 + jnp.exp(-x))


U32 = jnp.uint32
_HI16 = 0xFFFF0000


def _pack_halves(x):
    n = x.shape[1] // 2
    lo = pltpu.bitcast(x[:, :n].astype(BF16).astype(F32), U32)
    hi = pltpu.bitcast(x[:, n:].astype(BF16).astype(F32), U32)
    return (hi & U32(_HI16)) | (lo >> 16)


def _unpack_halves(w):
    return (pltpu.bitcast(w << 16, F32), pltpu.bitcast(w & U32(_HI16), F32))


ROW_SLABS = D_MODEL // 2 // LANES


def _store_rows(ref, words):
    m = words.shape[0]
    for c in range(ROW_SLABS):
        ref[pl.ds(c, m, stride=ROW_SLABS), :] = words[:, c * LANES:(c + 1) * LANES]


def _load_rows(ref):
    m = ref.shape[0] // ROW_SLABS
    return jnp.concatenate(
        [ref[pl.ds(c, m, stride=ROW_SLABS), :] for c in range(ROW_SLABS)], axis=1)


def _row(ref, i):
    return ref.at[pl.ds(pl.multiple_of(i * ROW_SLABS, ROW_SLABS), ROW_SLABS)]


def _ada_kernel(c_ref, w_ref, b_ref, o_ref):
    c = c_ref[...]
    s = c * _sigmoid(c)
    o_ref[...] = jnp.dot(s, w_ref[...], precision=HIGHEST,
                         preferred_element_type=F32) + b_ref[...]


def _ada(c, w, b):
    bsz, d = c.shape
    n = w.shape[1]
    rows = -(-bsz // SUBLANES) * SUBLANES
    cp = jnp.zeros((rows, d), F32).at[:bsz].set(c)
    tn = ADA_COLS
    out = pl.pallas_call(
        _ada_kernel,
        grid=(n // tn,),
        in_specs=[pl.BlockSpec((rows, d), lambda j: (0, 0)),
                  pl.BlockSpec((d, tn), lambda j: (0, j)),
                  pl.BlockSpec((1, tn), lambda j: (0, j))],
        out_specs=pl.BlockSpec((rows, tn), lambda j: (0, j)),
        out_shape=jax.ShapeDtypeStruct((rows, n), F32),
        compiler_params=_cparams(("arbitrary",)),
        name="ada",
    )(cp, w, b.reshape(1, n))
    return out[:bsz]


_OFF_QA = 0
_OFF_KA = _OFF_QA + MOBA_WIDTH
_OFF_VA = _OFF_KA + MOBA_WIDTH
_OFF_QKB = _OFF_VA + MOBA_WIDTH
_OFF_VB = _OFF_QKB + 2 * GLA_KEY_DIM
_OFF_GK = _OFF_VB + GLA_VALUE_DIM
_OFF_R = _OFF_GK + LANES
_OFF_G = _OFF_R + GLA_VALUE_DIM
_W_CAT = _OFF_G + 2 * D_MODEL


def _regroup_w_in(w):
    o_gk = 3 * MOBA_WIDTH + 2 * GLA_KEY_DIM + GLA_VALUE_DIM
    gk = jnp.pad(w[:, o_gk:o_gk + GLA_GATE_RANK], ((0, 0), (0, LANES - GLA_GATE_RANK)))
    return jnp.concatenate([w[:, :o_gk], gk, w[:, o_gk + GLA_GATE_RANK:]],
                           axis=1).astype(BF16)


def _inproj_kernel(x_ref, nw_ref, sc_ref, sh_ref, w_ref,
                   qt_ref, ka_ref, vt_ref, qkb_ref, vb_ref, gk_ref, r_ref, g_ref, *, tpb):
    tm = x_ref.shape[0]
    hd = MOBA_HEAD_DIM
    nbt = tm // MOBA_BLOCK
    h = _rms(x_ref[...]) * nw_ref[...]
    h = h * (1.0 + sc_ref[0]) + sh_ref[0]
    hb = h.astype(BF16)

    def mm(a, b):
        return jnp.dot(hb, w_ref[:, a:b], preferred_element_type=F32)

    q_t = (mm(_OFF_QA, _OFF_KA) * (hd ** -0.5 * LOG2E)).T
    v_t = mm(_OFF_VA, _OFF_QKB).T
    k_all = mm(_OFF_KA, _OFF_VA)
    blk0 = (pl.program_id(0) % tpb) * nbt
    lane = lax.broadcasted_iota(I32, (MOBA_BLOCK, LANES), 1)
    ones_rows = jnp.where(
        lax.broadcasted_iota(I32, (MOBA_VT_ROWS - hd, MOBA_BLOCK), 0) == 0, 1.0, 0.0)
    heads_per_tile = LANES // hd
    for hh in range(MOBA_HEADS):
        qt_ref[0, hh] = q_t[hh * hd:(hh + 1) * hd].astype(BF16)
        tile = hh // heads_per_tile
        k_h = k_all[:, tile * LANES:(tile + 1) * LANES]
        if hh % heads_per_tile:
            k_h = pltpu.roll(k_h, LANES - (hh % heads_per_tile) * hd, axis=1)
        for j in range(nbt):
            rows = slice(j * MOBA_BLOCK, (j + 1) * MOBA_BLOCK)
            onehot = jnp.where(lane == hd + blk0 + j, 1.0, 0.0)
            ka_ref[0, hh, j] = jnp.where(lane < hd, k_h[rows], onehot).astype(BF16)
            vt_ref[0, hh, j, 0:hd, :] = v_t[hh * hd:(hh + 1) * hd, rows].astype(BF16)
            vt_ref[0, hh, j, hd:MOBA_VT_ROWS, :] = ones_rows.astype(BF16)
    qkb_ref[...] = mm(_OFF_QKB, _OFF_VB).astype(BF16)
    vb_ref[...] = mm(_OFF_VB, _OFF_GK).astype(BF16)
    gk_ref[...] = mm(_OFF_GK, _OFF_R)
    r = mm(_OFF_R, _OFF_G)
    r_ref[...] = (r * _sigmoid(r)).astype(BF16)
    g_ref[...] = _sigmoid(mm(_OFF_G, _W_CAT)).astype(BF16)


def _inproj(x2d, nw, sc, sh, w_cat, bsz, seq, tm):
    t, d = x2d.shape
    tpb = seq // tm
    nbt = tm // MOBA_BLOCK
    nh, hd = MOBA_HEADS, MOBA_HEAD_DIM
    row = lambda w: pl.BlockSpec((tm, w), lambda i: (i, 0))
    per_b = pl.BlockSpec((1, 1, d), lambda i: (i // tpb, 0, 0))
    rows_out = [(2 * GLA_KEY_DIM, BF16), (GLA_VALUE_DIM, BF16), (LANES, F32),
                (GLA_VALUE_DIM, BF16), (2 * D_MODEL, BF16)]
    return pl.pallas_call(
        functools.partial(_inproj_kernel, tpb=tpb),
        grid=(t // tm,),
        in_specs=[row(d), pl.BlockSpec((1, d), lambda i: (0, 0)), per_b, per_b,
                  pl.BlockSpec((d, _W_CAT), lambda i: (0, 0), pipeline_mode=pl.Buffered(1))],
        out_specs=[pl.BlockSpec((1, nh, hd, tm), lambda i: (i // tpb, 0, 0, i % tpb)),
                   pl.BlockSpec((1, nh, nbt, MOBA_BLOCK, LANES),
                                lambda i: (i // tpb, 0, i % tpb, 0, 0)),
                   pl.BlockSpec((1, nh, nbt, MOBA_VT_ROWS, MOBA_BLOCK),
                                lambda i: (i // tpb, 0, i % tpb, 0, 0))]
                  + [row(w) for w, _ in rows_out],
        out_shape=[jax.ShapeDtypeStruct((bsz, nh, hd, seq), BF16),
                   jax.ShapeDtypeStruct((bsz, nh, seq // MOBA_BLOCK, MOBA_BLOCK, LANES), BF16),
                   jax.ShapeDtypeStruct((bsz, nh, seq // MOBA_BLOCK, MOBA_VT_ROWS, MOBA_BLOCK),
                                        BF16)]
                  + [jax.ShapeDtypeStruct((t, w), dt) for w, dt in rows_out],
        compiler_params=_cparams(("arbitrary",)),
        name="inproj",
    )(x2d, nw.reshape(1, d), sc, sh, w_cat)


def _t5_bucket_np(n):
    n = np.maximum(n, 0)
    max_exact = REL_BUCKETS // 2
    nf = np.maximum(n, max_exact).astype(np.float32)
    large = max_exact + (np.log(nf / max_exact) / math.log(REL_MAX_DIST / max_exact)
                         * (REL_BUCKETS - max_exact)).astype(np.int32)
    large = np.minimum(large, REL_BUCKETS - 1)
    return np.where(n < max_exact, n, large).astype(np.int32)


def _bucket_table():
    kj = np.arange(MOBA_BLOCK)[:, None]
    qi = np.arange(2 * MOBA_BLOCK)[None, :] % MOBA_BLOCK
    prev = np.arange(2 * MOBA_BLOCK)[None, :] < MOBA_BLOCK
    bucket = _t5_bucket_np(qi - kj + np.where(prev, MOBA_BLOCK, 0))
    return np.where(prev | (kj <= qi), bucket, -1).astype(np.int32)


def _bias_kernel(rb_ref, bucket_ref, o_ref):
    h = pl.program_id(0)
    bk = bucket_ref[...]
    far = rb_ref[(REL_BUCKETS - 1) * MOBA_HEADS + h]
    acc = jnp.zeros(bk.shape, F32)
    for b in range(REL_BUCKETS):
        acc = jnp.where(bk == b, rb_ref[b * MOBA_HEADS + h] - far, acc)
    o_ref[0] = jnp.where(bk < 0, NEG_BIG, acc * LOG2E)


def _bias_tiles(rel_bias):
    bucket = jnp.asarray(_bucket_table())
    return pl.pallas_call(
        _bias_kernel,
        grid=(MOBA_HEADS,),
        in_specs=[pl.BlockSpec(memory_space=pltpu.SMEM),
                  pl.BlockSpec(bucket.shape, lambda h: (0, 0))],
        out_specs=pl.BlockSpec((1,) + bucket.shape, lambda h: (h, 0, 0)),
        out_shape=jax.ShapeDtypeStruct((MOBA_HEADS,) + bucket.shape, F32),
        compiler_params=_cparams(("arbitrary",)),
        name="bias",
    )(rel_bias.reshape(-1), bucket)


MOBA_HEADS_PER_STEP = 8
MOBA_FAR_GROUP = 2
MOBA_QBLOCKS_PER_STEP = 4
MOBA_FAR_LAGS = (0, 3, 6)
MOBA_NEAR_LAGS = (0, 3, 6, 9)
MOBA_VT_ROWS = MOBA_HEAD_DIM + BF16_SUBLANES


def _moba_kernel(qt_ref, k_ref, vt_ref, bias_ref, o_ref, kmean_ref, qa_ref):
    blk = MOBA_BLOCK
    nb = MOBA_MAX_BLOCKS
    hd = MOBA_HEAD_DIM
    hp = MOBA_HEADS_PER_STEP
    grp = MOBA_FAR_GROUP
    nq = MOBA_QBLOCKS_PER_STEP
    pair = pl.program_id(2)
    neg = -jnp.inf
    items = [(h, j) for j in range(nq) for h in range(hp)]

    @pl.when(pair == 0)
    def _():
        kmean_ref[...] = jnp.zeros(kmean_ref.shape, F32)

    for h in range(hp):
        for j in range(nq):
            kmean_ref[h, pl.ds(pair * nq + j, 1), :] = jnp.mean(
                k_ref[0, h, pair * nq + j].astype(F32), axis=0, keepdims=True)

    row = lax.broadcasted_iota(I32, (nb, blk), 0)
    rowf = row.astype(F32)
    pad = jnp.zeros((LANES - hd, blk), BF16)
    pad_hi = jnp.zeros((LANES - hd - nb, blk), BF16)

    def skewed(stages, lags, todo=None):
        todo = list(range(len(items))) if todo is None else todo
        vals = {}
        for step in range(len(todo) + lags[-1]):
            for stage, lag in zip(stages, lags):
                pos = step - lag
                if 0 <= pos < len(todo):
                    vals[todo[pos]] = stage(todo[pos], vals.get(todo[pos]))
        return vals

    def select(n, _):
        h, j = items[n]
        qi = pair * nq + j
        qt = qt_ref[0, h, :, j * blk:(j + 1) * blk]
        km = kmean_ref[h, :, 0:hd]
        km_hi = km.astype(BF16)
        km_lo = (km - km_hi.astype(F32)).astype(BF16)
        gate = (jnp.dot(km_hi, qt, preferred_element_type=F32)
                + jnp.dot(km_lo, qt, preferred_element_type=F32))
        g = jnp.where(row < qi, gate, neg)
        sel = jnp.zeros((nb, blk), F32)
        for _ in range(MOBA_TOPK):
            mx = jnp.max(g, axis=0, keepdims=True)
            first = jnp.min(jnp.where(g == mx, rowf, float(nb)), axis=0, keepdims=True)
            pick = rowf == jnp.where(mx > neg, first, -1.0)
            sel = jnp.where(pick, 1.0, sel)
            g = jnp.where(pick, neg, g)
        mask_prev = jnp.where(sel > 0.0, jnp.where(row == qi - 1, 0.0, NEG_BIG), NEG_BIG)
        mask_far = jnp.where(sel > 0.0, jnp.where(row < qi - 1, 0.0, NEG_BIG), NEG_BIG)
        qa_ref[n] = jnp.concatenate([qt, mask_far.astype(BF16), pad_hi], axis=0)
        return (jnp.concatenate([qt, pad], axis=0),
                jnp.concatenate([qt, mask_prev.astype(BF16), pad_hi], axis=0))

    def own_prev(n):
        h, j = items[n]
        qi = pair * nq + j
        return h, qi, jnp.maximum(qi - 1, 0)

    def near_scores(n, qa):
        h, qi, prev_j = own_prev(n)
        qa_own, qa_prev = qa
        s_own = jnp.dot(k_ref[0, h, qi], qa_own, preferred_element_type=F32)
        s_prev = jnp.dot(k_ref[0, h, prev_j], qa_prev, preferred_element_type=F32)
        return s_own, s_prev

    def near_softmax(n, ss):
        h = items[n][0]
        s_own, s_prev = ss
        s = jnp.concatenate([s_own + bias_ref[h, :, blk:2 * blk],
                             s_prev + bias_ref[h, :, 0:blk]], axis=0)
        m0 = jnp.max(s, axis=0, keepdims=True)
        return m0, jnp.exp2(s - m0)

    def near_pv(n, mp):
        h, qi, prev_j = own_prev(n)
        m0, p = mp
        pb = p.astype(BF16)
        acc = (jnp.dot(vt_ref[0, h, qi], pb[0:blk], preferred_element_type=F32)
               + jnp.dot(vt_ref[0, h, prev_j], pb[blk:2 * blk], preferred_element_type=F32))
        return m0, acc

    near = skewed([select, near_scores, near_softmax, near_pv], MOBA_NEAR_LAGS)
    states = tuple(near[n] for n in range(len(items)))

    def far(gi, states, todo=None):
        j0 = gi * grp

        def qk(n, _):
            kt = k_ref[0, items[n][0], pl.ds(j0, grp)].reshape(grp * blk, LANES)
            return jnp.dot(kt, qa_ref[n], preferred_element_type=F32)

        def softmax(n, s):
            m_old = states[n][0]
            m_new = jnp.maximum(m_old, jnp.max(s, axis=0, keepdims=True))
            return m_new, jnp.exp2(m_old - m_new), jnp.exp2(s - m_new)

        def pv(n, sm):
            m_new, a, p = sm
            pb = p.astype(BF16)
            tot = a * states[n][1]
            for i in range(grp):
                tot = tot + jnp.dot(vt_ref[0, items[n][0], j0 + i], pb[i * blk:(i + 1) * blk],
                                    preferred_element_type=F32)
            return m_new, tot

        new = skewed([qk, softmax, pv], MOBA_FAR_LAGS, todo)
        return tuple(new.get(n, states[n]) for n in range(len(items)))

    sub = nq // grp
    states = lax.fori_loop(0, pair * sub, far, tuple(states))
    for extra in range(1, sub):
        later = [n for n, (_, j) in enumerate(items) if j // grp >= extra]
        states = far(pair * sub + extra - 1, states, later)

    for n, (h, j) in enumerate(items):
        acc = states[n][1]
        o_ref[0, h * hd:(h + 1) * hd, j * blk:(j + 1) * blk] = (
            acc[0:hd] / acc[hd:hd + 1]).astype(o_ref.dtype)


def _moba(qt, k_aug, vt, bias):
    bsz, nh, hd, s = qt.shape
    blk = MOBA_BLOCK
    hp = MOBA_HEADS_PER_STEP
    nq = MOBA_QBLOCKS_PER_STEP
    nblk = s // blk
    assert nh % hp == 0 and nblk % nq == 0 and nq % MOBA_FAR_GROUP == 0
    return pl.pallas_call(
        _moba_kernel,
        grid=(bsz, nh // hp, nblk // nq),
        in_specs=[pl.BlockSpec((1, hp, hd, nq * blk), lambda b, g, i: (b, g, 0, i)),
                  pl.BlockSpec((1, hp, nblk, blk, LANES), lambda b, g, i: (b, g, 0, 0, 0),
                               pipeline_mode=pl.Buffered(1)),
                  pl.BlockSpec((1, hp, nblk, MOBA_VT_ROWS, blk), lambda b, g, i: (b, g, 0, 0, 0),
                               pipeline_mode=pl.Buffered(1)),
                  pl.BlockSpec((hp, blk, 2 * blk), lambda b, g, i: (g, 0, 0),
                               pipeline_mode=pl.Buffered(1))],
        out_specs=pl.BlockSpec((1, hp * hd, nq * blk), lambda b, g, i: (b, g, i)),
        out_shape=jax.ShapeDtypeStruct((bsz, nh * hd, s), BF16),
        scratch_shapes=[pltpu.VMEM((hp, MOBA_MAX_BLOCKS, LANES), F32),
                        pltpu.VMEM((hp * nq, LANES, blk), BF16)],
        compiler_params=_cparams(("arbitrary", "arbitrary", "arbitrary")),
        name="moba",
    )(qt, k_aug, vt, bias)


def _gla_kernel(q_ref, k_ref, v_ref, gk_ref, wup_ref, bgk_ref, r_ref, gn_ref, o_ref,
                state_ref, *, nchunk):
    ch = GLA_CHUNK
    tc = nchunk * ch
    dk, dv = GLA_DK, GLA_DV

    @pl.when(pl.program_id(1) == 0)
    def _():
        state_ref[...] = jnp.zeros(state_ref.shape, F32)

    rin = lax.broadcasted_iota(I32, (tc, dk), 0) & (ch - 1)
    causal = (lax.broadcasted_iota(I32, (ch, ch), 1) <= lax.broadcasted_iota(I32, (ch, ch), 0))
    eye = (lax.broadcasted_iota(I32, (dk, dk), 0) == lax.broadcasted_iota(I32, (dk, dk), 1))
    chunks = [slice(n * ch, (n + 1) * ch) for n in range(nchunk)]
    gk = gk_ref[...]
    gk_hi = gk.astype(BF16)
    gk_lo = (gk - gk_hi.astype(F32)).astype(BF16)

    def prep(h, _):
        ks = slice(h * dk, (h + 1) * dk)
        w = wup_ref[:, ks]
        w_hi = w.astype(BF16)
        w_lo = (w - w_hi.astype(F32)).astype(BF16)
        z = (jnp.dot(gk_hi, w_hi, preferred_element_type=F32)
             + jnp.dot(gk_lo, w_hi, preferred_element_type=F32)
             + jnp.dot(gk_hi, w_lo, preferred_element_type=F32) + bgk_ref[:, ks])
        log_a = ((jnp.minimum(z, 0.0) - jnp.log(1.0 + jnp.exp(-jnp.abs(z))))
                 / GLA_GATE_NORMALIZER)
        b = log_a
        sh = 1
        while sh < ch:
            b = b + jnp.where(rin >= sh, pltpu.roll(b, sh, axis=0), 0.0)
            sh *= 2
        q = q_ref[:, ks].astype(F32) * (dk ** -0.5)
        k = k_ref[:, ks].astype(F32)
        q_g = (q * jnp.exp(b)).astype(BF16)
        k_g = (k * jnp.exp(-b)).astype(BF16)
        b3 = b.reshape(nchunk, ch, dk)
        b_last = b3[:, ch - 1:ch, :]
        k_end = (k * jnp.exp(jnp.broadcast_to(b_last, b3.shape) - b3).reshape(tc, dk)
                 ).astype(BF16)
        return q_g, k_g, k_end, jnp.exp(b_last)

    def local(h, pre):
        q_g, k_g, k_end, decay = pre
        o_intra, kv, decay_col = [], [], []
        for n, sl in enumerate(chunks):
            v_c = v_ref[sl, h * dv:(h + 1) * dv]
            att = jnp.where(causal, _nt_dot(q_g[sl], k_g[sl]), 0.0)
            o_intra.append(jnp.dot(att.astype(BF16), v_c, preferred_element_type=F32))
            kv.append(lax.dot_general(k_end[sl], v_c, (((0,), (0,)), ((), ())),
                                      preferred_element_type=F32))
            decay_col.append(jnp.sum(
                jnp.where(eye, jnp.broadcast_to(decay[n], (dk, dk)), 0.0),
                axis=1, keepdims=True))
        return q_g, o_intra, kv, decay_col

    def chain(h, loc):
        q_g, o_intra, kv, decay_col = loc
        state = state_ref[h]
        outs = []
        for n, sl in enumerate(chunks):
            outs.append(o_intra[n] + jnp.dot(q_g[sl], state.astype(BF16),
                                             preferred_element_type=F32))
            state = decay_col[n] * state + kv[n]
        state_ref[h] = state
        return jnp.concatenate(outs, axis=0)

    def finish(h, o):
        vs = slice(h * dv, (h + 1) * dv)
        o_ref[:, vs] = (_rms(o) * gn_ref[...] * r_ref[:, vs].astype(F32)).astype(o_ref.dtype)
        return None

    stages = [prep, local, chain, finish]
    vals = [None] * GLA_HEADS
    for step in range(GLA_HEADS + len(stages) - 1):
        for si, stage in enumerate(stages):
            h = step - si
            if 0 <= h < GLA_HEADS:
                vals[h] = stage(h, vals[h])


def _gla(qkb, vb, gk, wup, bgk, r_act, gn, bsz, seq, nchunk):
    t = qkb.shape[0]
    tc = nchunk * GLA_CHUNK
    nc = seq // tc
    rowblk = lambda w, off: pl.BlockSpec((tc, w), lambda b, c: (b * nc + c, off))
    full = lambda a: pl.BlockSpec(a.shape, lambda b, c: (0, 0))
    return pl.pallas_call(
        functools.partial(_gla_kernel, nchunk=nchunk),
        grid=(bsz, nc),
        in_specs=[rowblk(GLA_KEY_DIM, 0), rowblk(GLA_KEY_DIM, 1), rowblk(GLA_VALUE_DIM, 0),
                  rowblk(LANES, 0), full(wup), full(bgk), rowblk(GLA_VALUE_DIM, 0), full(gn)],
        out_specs=rowblk(GLA_VALUE_DIM, 0),
        out_shape=jax.ShapeDtypeStruct((t, GLA_VALUE_DIM), BF16),
        scratch_shapes=[pltpu.VMEM((GLA_HEADS, GLA_DK, GLA_DV), F32)],
        compiler_params=_cparams(("arbitrary", "arbitrary")),
        name="gla",
    )(qkb, qkb, vb, gk, wup, bgk, r_act, gn)


def _merge_kernel(ya_ref, yb_ref, g_ref, x_ref, wpa_ref, wpb_ref, wout_ref, g1_ref,
                  nw_ref, sc_ref, sh_ref, wr_ref, br_ref,
                  x1_ref, h2_ref, idx_ref, wts_ref, rank_ref, cnt_ref, carry_ref):
    @pl.when(pl.program_id(0) == 0)
    def _():
        carry_ref[...] = jnp.zeros(carry_ref.shape, F32)

    pa = lax.dot_general(ya_ref[0], wpa_ref[...], (((0,), (0,)), ((), ())),
                         preferred_element_type=F32)
    pb = jnp.dot(yb_ref[...], wpb_ref[...], preferred_element_type=F32)
    mixed = (g_ref[:, 0:D_MODEL].astype(F32) * pa
             + g_ref[:, D_MODEL:2 * D_MODEL].astype(F32) * pb)
    y = jnp.dot(mixed.astype(BF16), wout_ref[...], preferred_element_type=F32)
    x1 = x_ref[...] + g1_ref[0] * y
    x1_ref[...] = x1
    h2 = _rms(x1) * nw_ref[...]
    h2 = h2 * (1.0 + sc_ref[0]) + sh_ref[0]
    _store_rows(h2_ref, _pack_halves(h2))
    h_hi = h2.astype(BF16)
    h_lo = (h2 - h_hi.astype(F32)).astype(BF16)
    w = wr_ref[...]
    w_hi = w.astype(BF16)
    w_lo = (w - w_hi.astype(F32)).astype(BF16)
    logits = (_nt_dot(w_hi, h_hi) + _nt_dot(w_hi, h_lo) + _nt_dot(w_lo, h_hi)
              + br_ref[...])
    rowf = lax.broadcasted_iota(I32, logits.shape, 0).astype(F32)
    vals, idxs = [], []
    cur = logits
    for _ in range(TOP_K):
        mx = jnp.max(cur, axis=0, keepdims=True)
        first = jnp.min(jnp.where(cur == mx, rowf, float(N_EXPERTS)), axis=0, keepdims=True)
        vals.append(mx)
        idxs.append(first)
        cur = jnp.where(rowf == first, -jnp.inf, cur)
    es = [jnp.exp(v - vals[0]) for v in vals]
    tot = es[0]
    for e in es[1:]:
        tot = tot + e
    idx_ref[...] = jnp.concatenate(idxs, axis=0).astype(I32)
    wts_ref[...] = jnp.concatenate([e / tot for e in es], axis=0)
    tm = logits.shape[1]
    before = (lax.broadcasted_iota(I32, (tm, tm), 0) < lax.broadcasted_iota(I32, (tm, tm), 1))
    upper = jnp.where(before, 1.0, 0.0).astype(BF16)
    carry = carry_ref[:, 0:1]
    ranks = []
    for k in range(TOP_K):
        onehot = idxs[k] == rowf
        onef = jnp.where(onehot, 1.0, 0.0)
        earlier = jnp.dot(onef.astype(BF16), upper, preferred_element_type=F32) + carry
        ranks.append(jnp.sum(jnp.where(onehot, earlier, 0.0), axis=0, keepdims=True))
        carry = carry + jnp.sum(onef, axis=1, keepdims=True)
    rank_ref[...] = jnp.concatenate(ranks, axis=0).astype(I32)
    total = jnp.broadcast_to(carry, carry_ref.shape)
    carry_ref[...] = total
    cnt_ref[...] = total


def _merge(ya, yb, gates, x2d, wpa, wpb, wout, g1, nw, sc, sh, wr_t, br, seq, tm):
    t, d = x2d.shape
    tpb = seq // tm
    row = lambda w: pl.BlockSpec((tm, w), lambda i: (i, 0))
    full = lambda a: pl.BlockSpec(a.shape, lambda i: (0,) * a.ndim)
    per_b = pl.BlockSpec((1, 1, d), lambda i: (i // tpb, 0, 0))
    colblk = pl.BlockSpec((TOP_K, tm), lambda i: (0, i))
    return pl.pallas_call(
        _merge_kernel,
        grid=(t // tm,),
        in_specs=[pl.BlockSpec((1, MOBA_WIDTH, tm), lambda i: (i // tpb, 0, i % tpb)),
                  row(GLA_VALUE_DIM), row(2 * D_MODEL), row(d),
                  full(wpa), full(wpb), full(wout), per_b,
                  pl.BlockSpec((1, d), lambda i: (0, 0)), per_b, per_b,
                  full(wr_t), full(br)],
        out_specs=[row(d), pl.BlockSpec((tm * ROW_SLABS, LANES), lambda i: (i, 0)),
                   colblk, colblk, colblk,
                   pl.BlockSpec((N_EXPERTS, LANES), lambda i: (0, 0))],
        out_shape=[jax.ShapeDtypeStruct((t, d), F32),
                   jax.ShapeDtypeStruct((t * ROW_SLABS, LANES), U32),
                   jax.ShapeDtypeStruct((TOP_K, t), I32), jax.ShapeDtypeStruct((TOP_K, t), F32),
                   jax.ShapeDtypeStruct((TOP_K, t), I32),
                   jax.ShapeDtypeStruct((N_EXPERTS, LANES), F32)],
        scratch_shapes=[pltpu.VMEM((N_EXPERTS, LANES), F32)],
        compiler_params=_cparams(("arbitrary",)),
        name="merge",
    )(ya, yb, gates, x2d, wpa, wpb, wout, g1, nw.reshape(1, d), sc, sh, wr_t, br)


def _dest_kernel(pstart_ref, idx_ref, rank_ref, dest_ref):
    idx = idx_ref[...]
    off = jnp.zeros(idx.shape, I32)
    for e in range(N_EXPERTS):
        off = jnp.where(idx == e, pstart_ref[e], off)
    dest_ref[...] = rank_ref[...] + off


def _dest(pstart, idx, rank, tm):
    t = idx.shape[1]
    blk = pl.BlockSpec((TOP_K, tm), lambda i: (0, i))
    return pl.pallas_call(
        _dest_kernel,
        grid=(t // tm,),
        in_specs=[pl.BlockSpec(memory_space=pltpu.SMEM), blk, blk],
        out_specs=blk,
        out_shape=jax.ShapeDtypeStruct((TOP_K, t), I32),
        compiler_params=_cparams(("arbitrary",)),
        name="dest",
    )(pstart, idx, rank)


def _scatter_kernel(dest_ref, zrow_ref, h_ref, xout_hbm, zbuf, stage, sems, zsem, *, tm):
    @pl.when(pl.program_id(0) == 0)
    def _():
        zbuf[...] = jnp.zeros(zbuf.shape, zbuf.dtype)

        def zero_rows(row0):
            line0 = pl.multiple_of(row0 * ROW_SLABS, MOE_ROWS * ROW_SLABS)
            return pltpu.make_async_copy(
                zbuf, xout_hbm.at[pl.ds(line0, MOE_ROWS * ROW_SLABS)], zsem)

        for e in range(N_EXPERTS):
            zero_rows(zrow_ref[e]).start()
        for e in range(N_EXPERTS):
            zero_rows(zrow_ref[e]).wait()

        def start_tail(j, carry):
            zero_rows(j * MOE_ROWS).start()
            return carry

        def wait_tail(j, carry):
            zero_rows(j * MOE_ROWS).wait()
            return carry

        n_all = xout_hbm.shape[0] // (MOE_ROWS * ROW_SLABS)
        lax.fori_loop(zrow_ref[N_EXPERTS], n_all, start_tail, 0)
        lax.fori_loop(zrow_ref[N_EXPERTS], n_all, wait_tail, 0)

    i = pl.program_id(0)
    slot = i % 2
    src = stage.at[slot]
    src[...] = h_ref[...]

    def issue(g, carry):
        base = pl.multiple_of(g * ROW_UNROLL, ROW_UNROLL)
        for j in range(ROW_UNROLL):
            for k in range(TOP_K):
                d = dest_ref[k, base + j]
                pltpu.make_async_copy(_row(src, base + j), _row(xout_hbm, d),
                                      sems.at[slot]).start(priority=k % 2)
        return carry

    lax.fori_loop(0, tm // ROW_UNROLL, issue, 0)

    def wait_tile(s):
        for k in range(TOP_K):
            pltpu.make_async_copy(stage.at[s], xout_hbm.at[pl.ds(0, tm * ROW_SLABS)],
                                  sems.at[s]).wait()

    @pl.when(i > 0)
    def _():
        wait_tile(1 - slot)

    @pl.when(i == pl.num_programs(0) - 1)
    def _():
        wait_tile(slot)


def _scatter(dest, last_block_row, h2, n_pad, tm):
    t = h2.shape[0] // ROW_SLABS
    return pl.pallas_call(
        functools.partial(_scatter_kernel, tm=tm),
        grid=(t // tm,),
        in_specs=[pl.BlockSpec((TOP_K, tm), lambda i: (0, i), memory_space=pltpu.SMEM),
                  pl.BlockSpec(memory_space=pltpu.SMEM),
                  pl.BlockSpec((tm * ROW_SLABS, LANES), lambda i: (i, 0))],
        out_specs=pl.BlockSpec(memory_space=pl.ANY),
        out_shape=jax.ShapeDtypeStruct((n_pad * ROW_SLABS, LANES), h2.dtype),
        scratch_shapes=[pltpu.VMEM((MOE_ROWS * ROW_SLABS, LANES), h2.dtype),
                        pltpu.VMEM((2, tm * ROW_SLABS, LANES), h2.dtype),
                        pltpu.SemaphoreType.DMA((2,)), pltpu.SemaphoreType.DMA],
        compiler_params=_cparams(("arbitrary",)),
        name="scatter",
    )(dest, last_block_row, h2)


def _expert_kernel(be_ref, nused_ref, x_ref, wg_ref, bg_ref, wu_ref, bu_ref, wd_ref, bd_ref,
                   o_ref, wgb_ref, wub_ref, wdb_ref):
    i = pl.program_id(0)
    prev = be_ref[jnp.maximum(i - 1, 0)]

    @pl.when(jnp.logical_or(i == 0, be_ref[i] != prev))
    def _():
        wgb_ref[...] = wg_ref[0].astype(BF16)
        wub_ref[...] = wu_ref[0].astype(BF16)
        wdb_ref[...] = wd_ref[0].astype(BF16)

    @pl.when(i < nused_ref[0])
    def _():
        xb = jnp.concatenate([v.astype(BF16) for v in _unpack_halves(_load_rows(x_ref))],
                             axis=1)
        g = jnp.minimum(jnp.dot(xb, wgb_ref[...], preferred_element_type=F32) + bg_ref[0],
                        SWIGLU_LIMIT)
        u = jnp.clip(jnp.dot(xb, wub_ref[...], preferred_element_type=F32) + bu_ref[0],
                     -SWIGLU_LIMIT, SWIGLU_LIMIT)
        act = g * _sigmoid(SWIGLU_ALPHA * g) * (u + 1.0)
        _store_rows(o_ref, _pack_halves(
            jnp.dot(act.astype(BF16), wdb_ref[...], preferred_element_type=F32) + bd_ref[0]))

    @pl.when(i >= nused_ref[0])
    def _():
        o_ref[...] = jnp.zeros(o_ref.shape, o_ref.dtype)


def _expert(block_expert, n_used, x_pad, wg, bg, wu, bu, wd, bd):
    lines = MOE_ROWS * ROW_SLABS
    _, d, f = wg.shape
    nblk = x_pad.shape[0] // lines
    wspec = lambda a, b: pl.BlockSpec((1, a, b), lambda i, be, nu: (be[i], 0, 0))
    grid_spec = pltpu.PrefetchScalarGridSpec(
        num_scalar_prefetch=2,
        grid=(nblk,),
        in_specs=[pl.BlockSpec((lines, LANES),
                               lambda i, be, nu: (jnp.minimum(i, nu[0] - 1), 0)),
                  wspec(d, f), wspec(1, f), wspec(d, f), wspec(1, f), wspec(f, d), wspec(1, d)],
        out_specs=pl.BlockSpec((lines, LANES), lambda i, be, nu: (i, 0)),
        scratch_shapes=[pltpu.VMEM((d, f), BF16), pltpu.VMEM((d, f), BF16),
                        pltpu.VMEM((f, d), BF16)])
    return pl.pallas_call(
        _expert_kernel,
        grid_spec=grid_spec,
        out_shape=jax.ShapeDtypeStruct(x_pad.shape, U32),
        compiler_params=_cparams(("arbitrary",)),
        name="expert",
    )(block_expert, n_used, x_pad, wg, bg.reshape(N_EXPERTS, 1, f), wu,
      bu.reshape(N_EXPERTS, 1, f), wd, bd.reshape(N_EXPERTS, 1, d))


def _combine_kernel(dest_ref, y_hbm, x1_ref, wts_ref, g2_ref, nf_ref, o_ref, ybuf_a, ybuf_b,
                    sems, *, tm, final, n_tiles):
    i = pl.program_id(0)
    bufs = (ybuf_a, ybuf_b)

    def issue(slot):
        for tt in range(tm):
            for k in range(TOP_K):
                d = dest_ref[k, tt]
                pltpu.make_async_copy(_row(y_hbm, d), _row(bufs[slot].at[k], tt),
                                      sems.at[slot]).start(priority=k % 2)

    def wait(slot):
        for k in range(TOP_K):
            pltpu.make_async_copy(y_hbm.at[pl.ds(0, tm * ROW_SLABS)], bufs[slot].at[k],
                                  sems.at[slot]).wait()

    def finish(slot):
        w = wts_ref[...].T
        lo, hi = None, None
        for k in range(TOP_K):
            y_lo, y_hi = _unpack_halves(_load_rows(bufs[slot].at[k]))
            lo = w[:, k:k + 1] * y_lo + (0.0 if lo is None else lo)
            hi = w[:, k:k + 1] * y_hi + (0.0 if hi is None else hi)
        moe = jnp.concatenate([lo, hi], axis=1)
        x2 = x1_ref[...] + g2_ref[0] * moe
        o_ref[...] = _rms(x2) * nf_ref[...] if final else x2

    @pl.when(i == 0)
    def _():
        issue(0)

    for parity in range(2):
        @pl.when(jnp.logical_and(jnp.logical_and(i > 0, i < n_tiles), i % 2 == parity))
        def _(parity=parity):
            wait(1 - parity)
            issue(parity)
            finish(1 - parity)

    @pl.when(i == n_tiles)
    def _():
        wait((n_tiles - 1) % 2)
        finish((n_tiles - 1) % 2)


def _combine(dest, y_pad, x1, wts, g2, nf, seq, tm, final):
    t, d = x1.shape
    tpb = seq // tm
    n_tiles = t // tm
    lag = lambda i: jnp.maximum(i - 1, 0)
    return pl.pallas_call(
        functools.partial(_combine_kernel, tm=tm, final=final, n_tiles=n_tiles),
        grid=(n_tiles + 1,),
        in_specs=[pl.BlockSpec((TOP_K, tm), lambda i: (0, jnp.minimum(i, n_tiles - 1)),
                               memory_space=pltpu.SMEM),
                  pl.BlockSpec(memory_space=pl.ANY),
                  pl.BlockSpec((tm, d), lambda i: (lag(i), 0)),
                  pl.BlockSpec((TOP_K, tm), lambda i: (0, lag(i))),
                  pl.BlockSpec((1, 1, d), lambda i: (lag(i) // tpb, 0, 0)),
                  pl.BlockSpec((1, d), lambda i: (0, 0))],
        out_specs=pl.BlockSpec((tm, d), lambda i: (lag(i), 0)),
        out_shape=jax.ShapeDtypeStruct((t, d), F32),
        scratch_shapes=[pltpu.VMEM((TOP_K, tm * ROW_SLABS, LANES), y_pad.dtype),
                        pltpu.VMEM((TOP_K, tm * ROW_SLABS, LANES), y_pad.dtype),
                        pltpu.SemaphoreType.DMA((2,))],
        compiler_params=_cparams(("arbitrary",)),
        name="combine",
    )(dest, y_pad, x1, wts, g2, nf.reshape(1, d))


def _pick(n, cands):
    for c in cands:
        if n % c == 0:
            return c
    raise ValueError(f"no tile in {cands} divides {n}")


def kernel(x, c, rel_bias, w_ada, b_ada, norm_mix, w_in, w_gk_up, b_gk, gla_norm,
           w_proj_moba, w_proj_gla, w_out, norm_ffn, w_router, b_router,
           w_gate, b_gate, w_up, b_up, w_down, b_down, norm_final):
    bsz, seq, d = x.shape
    depth = w_ada.shape[0]
    assert d == D_MODEL and seq % MOBA_BLOCK == 0 and seq // MOBA_BLOCK <= MOBA_MAX_BLOCKS
    t = bsz * seq
    tm = _pick(seq, (512, 256))
    nchunk = _pick(seq // GLA_CHUNK, (16, 8, 4))
    n_blk = seq // MOBA_BLOCK
    x2d = x.reshape(t, d)
    bias = _bias_tiles(rel_bias)
    per_b = lambda v: v.reshape(bsz, 1, d)

    for l in range(depth):
        mod = _ada(c, w_ada[l], b_ada[l])
        sh1, sc1, g1, sh2, sc2, g2 = [per_b(m) for m in jnp.split(mod, 6, axis=-1)]
        qt, k_aug, vt, qkb, vb, gk, r_act, gates = _inproj(
            x2d, norm_mix[l], sc1, sh1, _regroup_w_in(w_in[l]), bsz, seq, tm)
        ya = _moba(qt, k_aug, vt, bias)
        wup = jnp.pad(w_gk_up[l], ((0, LANES - GLA_GATE_RANK), (0, 0)))
        yb = _gla(qkb, vb, gk, wup, b_gk[l].reshape(1, -1), r_act,
                  gla_norm[l].reshape(1, -1), bsz, seq, nchunk)
        x1, h2, idx, wts, rank, cnt = _merge(
            ya, yb, gates, x2d, w_proj_moba[l].astype(BF16), w_proj_gla[l].astype(BF16),
            w_out[l].astype(BF16), g1, norm_ffn[l], sc2, sh2,
            w_router[l].T, b_router[l].reshape(N_EXPERTS, 1), seq, tm)
        counts = cnt[:, 0].astype(I32)
        padded = (counts + MOE_ROWS - 1) // MOE_ROWS * MOE_ROWS
        pcum = jnp.cumsum(padded)
        pstart = (pcum - padded).astype(I32)
        n_blocks = (t * TOP_K + MOE_ROWS - 1) // MOE_ROWS + N_EXPERTS
        block_row0 = jnp.arange(n_blocks, dtype=I32) * MOE_ROWS
        block_expert = jnp.minimum(
            jnp.sum((pcum[None, :] <= block_row0[:, None]).astype(I32), axis=1),
            N_EXPERTS - 1).astype(I32)
        n_used = (pcum[-1:] // MOE_ROWS).astype(I32)
        dest = _dest(pstart, idx, rank, _pick(t, (8192, 4096, 2048, 1024, 512, 256)))
        tg = _pick(seq, (256,))
        zero_info = jnp.concatenate([jnp.maximum(pcum - MOE_ROWS, 0).astype(I32), n_used])
        x_pad = _scatter(dest, zero_info, h2, n_blocks * MOE_ROWS, tg)
        y_pad = _expert(block_expert, n_used, x_pad, w_gate[l], b_gate[l], w_up[l], b_up[l],
                        w_down[l], b_down[l])
        x2d = _combine(dest, y_pad, x1, wts, g2, norm_final, seq, tg, l == depth - 1)
    return x2d.reshape(bsz, seq, d)
```

```python
import functools
import math

import numpy as np
import jax
import jax.numpy as jnp
from jax import lax
from jax.experimental import pallas as pl
from jax.experimental.pallas import tpu as pltpu

F32 = jnp.float32
BF16 = jnp.bfloat16
I32 = jnp.int32
HIGHEST = lax.Precision.HIGHEST

D_MODEL = 1024
MOBA_HEADS = 8
MOBA_HEAD_DIM = 64
MOBA_WIDTH = MOBA_HEADS * MOBA_HEAD_DIM
MOBA_BLOCK = 256
MOBA_TOPK = 3
MOBA_MAX_BLOCKS = 32
REL_BUCKETS = 32
REL_MAX_DIST = 128
GLA_HEADS = 4
GLA_KEY_DIM = D_MODEL // 2
GLA_VALUE_DIM = D_MODEL
GLA_DK = GLA_KEY_DIM // GLA_HEADS
GLA_DV = GLA_VALUE_DIM // GLA_HEADS
GLA_GATE_RANK = 16
GLA_GATE_NORMALIZER = 16.0
GLA_CHUNK = 64
N_EXPERTS = 32
TOP_K = 4
D_FF = D_MODEL
SWIGLU_ALPHA = 1.702
SWIGLU_LIMIT = 7.0
MOE_ROWS = 512
ROW_UNROLL = 8
EPS = 1e-6
LANES = 128
SUBLANES = 8
BF16_SUBLANES = 16
ADA_COLS = 768
NEG_BIG = -1e30
LOG2E = math.log2(math.e)
VMEM_LIMIT = 56 * 1024 * 1024


def _cparams(sem, vmem=None):
    return pltpu.CompilerParams(dimension_semantics=sem,
                                vmem_limit_bytes=vmem or VMEM_LIMIT)


def _nt_dot(a, b, **kw):
    return lax.dot_general(a, b, (((1,), (1,)), ((), ())),
                           preferred_element_type=F32, **kw)


def _rms(x):
    return x * lax.rsqrt(jnp.mean(x * x, axis=-1, keepdims=True) + EPS)


def _sigmoid(x):
    return 1.0 / (1.0 + jnp.exp(-x))


U32 = jnp.uint32
_HI16 = 0xFFFF0000


def _pack_halves(x):
    n = x.shape[1] // 2
    lo = pltpu.bitcast(x[:, :n].astype(BF16).astype(F32), U32)
    hi = pltpu.bitcast(x[:, n:].astype(BF16).astype(F32), U32)
    return (hi & U32(_HI16)) | (lo >> 16)


def _unpack_halves(w):
    return (pltpu.bitcast(w << 16, F32), pltpu.bitcast(w & U32(_HI16), F32))


ROW_SLABS = D_MODEL // 2 // LANES


def _store_rows(ref, words):
    m = words.shape[0]
    for c in range(ROW_SLABS):
        ref[pl.ds(c, m, stride=ROW_SLABS), :] = words[:, c * LANES:(c + 1) * LANES]


def _load_rows(ref):
    m = ref.shape[0] // ROW_SLABS
    return jnp.concatenate(
        [ref[pl.ds(c, m, stride=ROW_SLABS), :] for c in range(ROW_SLABS)], axis=1)


def _row(ref, i):
    return ref.at[pl.ds(pl.multiple_of(i * ROW_SLABS, ROW_SLABS), ROW_SLABS)]


def _ada_kernel(c_ref, w_ref, b_ref, o_ref):
    c = c_ref[...]
    s = c * _sigmoid(c)
    o_ref[...] = jnp.dot(s, w_ref[...], precision=HIGHEST,
                         preferred_element_type=F32) + b_ref[...]


def _ada(c, w, b):
    bsz, d = c.shape
    n = w.shape[1]
    rows = -(-bsz // SUBLANES) * SUBLANES
    cp = jnp.zeros((rows, d), F32).at[:bsz].set(c)
    tn = ADA_COLS
    out = pl.pallas_call(
        _ada_kernel,
        grid=(n // tn,),
        in_specs=[pl.BlockSpec((rows, d), lambda j: (0, 0)),
                  pl.BlockSpec((d, tn), lambda j: (0, j)),
                  pl.BlockSpec((1, tn), lambda j: (0, j))],
        out_specs=pl.BlockSpec((rows, tn), lambda j: (0, j)),
        out_shape=jax.ShapeDtypeStruct((rows, n), F32),
        compiler_params=_cparams(("arbitrary",)),
        name="ada",
    )(cp, w, b.reshape(1, n))
    return out[:bsz]


_OFF_QA = 0
_OFF_KA = _OFF_QA + MOBA_WIDTH
_OFF_VA = _OFF_KA + MOBA_WIDTH
_OFF_QKB = _OFF_VA + MOBA_WIDTH
_OFF_VB = _OFF_QKB + 2 * GLA_KEY_DIM
_OFF_GK = _OFF_VB + GLA_VALUE_DIM
_OFF_R = _OFF_GK + LANES
_OFF_G = _OFF_R + GLA_VALUE_DIM
_W_CAT = _OFF_G + 2 * D_MODEL


def _regroup_w_in(w):
    o_gk = 3 * MOBA_WIDTH + 2 * GLA_KEY_DIM + GLA_VALUE_DIM
    w = w.astype(BF16)
    gk = jnp.pad(w[:, o_gk:o_gk + GLA_GATE_RANK], ((0, 0), (0, LANES - GLA_GATE_RANK)))
    return jnp.concatenate([w[:, :o_gk], gk, w[:, o_gk + GLA_GATE_RANK:]], axis=1)


def _inproj_kernel(x_ref, nw_ref, sc_ref, sh_ref, w_ref,
                   qt_ref, ka_ref, vt_ref, qkb_ref, vb_ref, gk_ref, r_ref, g_ref, *, tpb):
    tm = x_ref.shape[0]
    hd = MOBA_HEAD_DIM
    nbt = tm // MOBA_BLOCK
    h = _rms(x_ref[...]) * nw_ref[...]
    h = h * (1.0 + sc_ref[0]) + sh_ref[0]
    hb = h.astype(BF16)

    def mm(a, b):
        return jnp.dot(hb, w_ref[:, a:b], preferred_element_type=F32)

    q_t = (mm(_OFF_QA, _OFF_KA) * (hd ** -0.5 * LOG2E)).T
    v_t = mm(_OFF_VA, _OFF_QKB).T
    k_all = mm(_OFF_KA, _OFF_VA)
    blk0 = (pl.program_id(0) % tpb) * nbt
    lane = lax.broadcasted_iota(I32, (MOBA_BLOCK, LANES), 1)
    ones_rows = jnp.where(
        lax.broadcasted_iota(I32, (MOBA_VT_ROWS - hd, MOBA_BLOCK), 0) == 0, 1.0, 0.0)
    heads_per_tile = LANES // hd
    for hh in range(MOBA_HEADS):
        qt_ref[0, hh] = q_t[hh * hd:(hh + 1) * hd].astype(BF16)
        tile = hh // heads_per_tile
        k_h = k_all[:, tile * LANES:(tile + 1) * LANES]
        if hh % heads_per_tile:
            k_h = pltpu.roll(k_h, LANES - (hh % heads_per_tile) * hd, axis=1)
        for j in range(nbt):
            rows = slice(j * MOBA_BLOCK, (j + 1) * MOBA_BLOCK)
            onehot = jnp.where(lane == hd + blk0 + j, 1.0, 0.0)
            ka_ref[0, hh, j] = jnp.where(lane < hd, k_h[rows], onehot).astype(BF16)
            vt_ref[0, hh, j, 0:hd, :] = v_t[hh * hd:(hh + 1) * hd, rows].astype(BF16)
            vt_ref[0, hh, j, hd:MOBA_VT_ROWS, :] = ones_rows.astype(BF16)
    qkb_ref[...] = mm(_OFF_QKB, _OFF_VB).astype(BF16)
    vb_ref[...] = mm(_OFF_VB, _OFF_GK).astype(BF16)
    gk_ref[...] = mm(_OFF_GK, _OFF_R)
    r = mm(_OFF_R, _OFF_G)
    r_ref[...] = (r * _sigmoid(r)).astype(BF16)
    g_ref[...] = _sigmoid(mm(_OFF_G, _W_CAT)).astype(BF16)


def _inproj(x2d, nw, sc, sh, w_cat, bsz, seq, tm):
    t, d = x2d.shape
    tpb = seq // tm
    nbt = tm // MOBA_BLOCK
    nh, hd = MOBA_HEADS, MOBA_HEAD_DIM
    row = lambda w: pl.BlockSpec((tm, w), lambda i: (i, 0))
    per_b = pl.BlockSpec((1, 1, d), lambda i: (i // tpb, 0, 0))
    rows_out = [(2 * GLA_KEY_DIM, BF16), (GLA_VALUE_DIM, BF16), (LANES, F32),
                (GLA_VALUE_DIM, BF16), (2 * D_MODEL, BF16)]
    return pl.pallas_call(
        functools.partial(_inproj_kernel, tpb=tpb),
        grid=(t // tm,),
        in_specs=[row(d), pl.BlockSpec((1, d), lambda i: (0, 0)), per_b, per_b,
                  pl.BlockSpec((d, _W_CAT), lambda i: (0, 0), pipeline_mode=pl.Buffered(1))],
        out_specs=[pl.BlockSpec((1, nh, hd, tm), lambda i: (i // tpb, 0, 0, i % tpb)),
                   pl.BlockSpec((1, nh, nbt, MOBA_BLOCK, LANES),
                                lambda i: (i // tpb, 0, i % tpb, 0, 0)),
                   pl.BlockSpec((1, nh, nbt, MOBA_VT_ROWS, MOBA_BLOCK),
                                lambda i: (i // tpb, 0, i % tpb, 0, 0))]
                  + [row(w) for w, _ in rows_out],
        out_shape=[jax.ShapeDtypeStruct((bsz, nh, hd, seq), BF16),
                   jax.ShapeDtypeStruct((bsz, nh, seq // MOBA_BLOCK, MOBA_BLOCK, LANES), BF16),
                   jax.ShapeDtypeStruct((bsz, nh, seq // MOBA_BLOCK, MOBA_VT_ROWS, MOBA_BLOCK),
                                        BF16)]
                  + [jax.ShapeDtypeStruct((t, w), dt) for w, dt in rows_out],
        compiler_params=_cparams(("arbitrary",)),
        name="inproj",
    )(x2d, nw.reshape(1, d), sc, sh, w_cat)


def _t5_bucket_np(n):
    n = np.maximum(n, 0)
    max_exact = REL_BUCKETS // 2
    nf = np.maximum(n, max_exact).astype(np.float32)
    large = max_exact + (np.log(nf / max_exact) / math.log(REL_MAX_DIST / max_exact)
                         * (REL_BUCKETS - max_exact)).astype(np.int32)
    large = np.minimum(large, REL_BUCKETS - 1)
    return np.where(n < max_exact, n, large).astype(np.int32)


def _bucket_table():
    kj = np.arange(MOBA_BLOCK)[:, None]
    qi = np.arange(2 * MOBA_BLOCK)[None, :] % MOBA_BLOCK
    prev = np.arange(2 * MOBA_BLOCK)[None, :] < MOBA_BLOCK
    bucket = _t5_bucket_np(qi - kj + np.where(prev, MOBA_BLOCK, 0))
    return np.where(prev | (kj <= qi), bucket, -1).astype(np.int32)


def _bias_kernel(rb_ref, bucket_ref, o_ref):
    h = pl.program_id(0)
    bk = bucket_ref[...]
    far = rb_ref[(REL_BUCKETS - 1) * MOBA_HEADS + h]
    acc = jnp.zeros(bk.shape, F32)
    for b in range(REL_BUCKETS):
        acc = jnp.where(bk == b, rb_ref[b * MOBA_HEADS + h] - far, acc)
    o_ref[0] = jnp.where(bk < 0, NEG_BIG, acc * LOG2E)


def _bias_tiles(rel_bias):
    bucket = jnp.asarray(_bucket_table())
    return pl.pallas_call(
        _bias_kernel,
        grid=(MOBA_HEADS,),
        in_specs=[pl.BlockSpec(memory_space=pltpu.SMEM),
                  pl.BlockSpec(bucket.shape, lambda h: (0, 0))],
        out_specs=pl.BlockSpec((1,) + bucket.shape, lambda h: (h, 0, 0)),
        out_shape=jax.ShapeDtypeStruct((MOBA_HEADS,) + bucket.shape, F32),
        compiler_params=_cparams(("arbitrary",)),
        name="bias",
    )(rel_bias.reshape(-1), bucket)


MOBA_HEADS_PER_STEP = 8
MOBA_FAR_GROUP = 2
MOBA_QBLOCKS_PER_STEP = 4
MOBA_FAR_LAGS = (0, 3, 6)
MOBA_NEAR_LAGS = (0, 3, 6, 9)
MOBA_VT_ROWS = MOBA_HEAD_DIM + BF16_SUBLANES


def _moba_kernel(qt_ref, k_hbm, vt_hbm, bias_ref, o_ref, kmean_ref, qa_ref, k_ref, vt_ref,
                 kv_sems):
    blk = MOBA_BLOCK
    nb = MOBA_MAX_BLOCKS
    hd = MOBA_HEAD_DIM
    hp = MOBA_HEADS_PER_STEP
    grp = MOBA_FAR_GROUP
    nq = MOBA_QBLOCKS_PER_STEP
    pair = pl.program_id(2)
    neg = -jnp.inf
    items = [(h, j) for j in range(nq) for h in range(hp)]

    @pl.when(pair == 0)
    def _():
        kmean_ref[...] = jnp.zeros(kmean_ref.shape, F32)

    def kv_copies(first_blk):
        src = (pl.program_id(0), pl.ds(pl.program_id(1) * hp, hp), pl.ds(first_blk, nq))
        dst = (0, slice(None), pl.ds(first_blk, nq))
        return (pltpu.make_async_copy(k_hbm.at[src], k_ref.at[dst], kv_sems.at[0]),
                pltpu.make_async_copy(vt_hbm.at[src], vt_ref.at[dst], kv_sems.at[1]))

    @pl.when(pair == 0)
    def _():
        for cp in kv_copies(0):
            cp.start()

    for cp in kv_copies(pair * nq):
        cp.wait()

    @pl.when(pair + 1 < pl.num_programs(2))
    def _():
        for cp in kv_copies((pair + 1) * nq):
            cp.start()

    for h in range(hp):
        for j in range(nq):
            kmean_ref[h, pl.ds(pair * nq + j, 1), :] = jnp.mean(
                k_ref[0, h, pair * nq + j].astype(F32), axis=0, keepdims=True)

    row = lax.broadcasted_iota(I32, (nb, blk), 0)
    rowf = row.astype(F32)
    pad = jnp.zeros((LANES - hd, blk), BF16)
    pad_hi = jnp.zeros((LANES - hd - nb, blk), BF16)

    def skewed(stages, lags, todo=None):
        todo = list(range(len(items))) if todo is None else todo
        vals = {}
        for step in range(len(todo) + lags[-1]):
            for stage, lag in zip(stages, lags):
                pos = step - lag
                if 0 <= pos < len(todo):
                    vals[todo[pos]] = stage(todo[pos], vals.get(todo[pos]))
        return vals

    def select(n, _):
        h, j = items[n]
        qi = pair * nq + j
        qt = qt_ref[0, h, :, j * blk:(j + 1) * blk]
        km = kmean_ref[h, :, 0:hd]
        km_hi = km.astype(BF16)
        km_lo = (km - km_hi.astype(F32)).astype(BF16)
        gate = (jnp.dot(km_hi, qt, preferred_element_type=F32)
                + jnp.dot(km_lo, qt, preferred_element_type=F32))
        g = jnp.where(row < qi, gate, neg)
        sel = jnp.zeros((nb, blk), F32)
        for _ in range(MOBA_TOPK):
            mx = jnp.max(g, axis=0, keepdims=True)
            first = jnp.min(jnp.where(g == mx, rowf, float(nb)), axis=0, keepdims=True)
            pick = rowf == jnp.where(mx > neg, first, -1.0)
            sel = jnp.where(pick, 1.0, sel)
            g = jnp.where(pick, neg, g)
        mask_prev = jnp.where(sel > 0.0, jnp.where(row == qi - 1, 0.0, NEG_BIG), NEG_BIG)
        mask_far = jnp.where(sel > 0.0, jnp.where(row < qi - 1, 0.0, NEG_BIG), NEG_BIG)
        qa_ref[n] = jnp.concatenate([qt, mask_far.astype(BF16), pad_hi], axis=0)
        return (jnp.concatenate([qt, pad], axis=0),
                jnp.concatenate([qt, mask_prev.astype(BF16), pad_hi], axis=0))

    def own_prev(n):
        h, j = items[n]
        qi = pair * nq + j
        return h, qi, jnp.maximum(qi - 1, 0)

    def near_scores(n, qa):
        h, qi, prev_j = own_prev(n)
        qa_own, qa_prev = qa
        s_own = jnp.dot(k_ref[0, h, qi], qa_own, preferred_element_type=F32)
        s_prev = jnp.dot(k_ref[0, h, prev_j], qa_prev, preferred_element_type=F32)
        return s_own, s_prev

    def near_softmax(n, ss):
        h = items[n][0]
        s_own, s_prev = ss
        s = jnp.concatenate([s_own + bias_ref[h, :, blk:2 * blk],
                             s_prev + bias_ref[h, :, 0:blk]], axis=0)
        m0 = jnp.max(s, axis=0, keepdims=True)
        return m0, jnp.exp2(s - m0)

    def near_pv(n, mp):
        h, qi, prev_j = own_prev(n)
        m0, p = mp
        pb = p.astype(BF16)
        acc = (jnp.dot(vt_ref[0, h, qi], pb[0:blk], preferred_element_type=F32)
               + jnp.dot(vt_ref[0, h, prev_j], pb[blk:2 * blk], preferred_element_type=F32))
        return m0, acc

    near = skewed([select, near_scores, near_softmax, near_pv], MOBA_NEAR_LAGS)
    states = tuple(near[n] for n in range(len(items)))

    def far(gi, states, todo=None):
        j0 = gi * grp

        def qk(n, _):
            kt = k_ref[0, items[n][0], pl.ds(j0, grp)].reshape(grp * blk, LANES)
            return jnp.dot(kt, qa_ref[n], preferred_element_type=F32)

        def softmax(n, s):
            m_old = states[n][0]
            m_new = jnp.maximum(m_old, jnp.max(s, axis=0, keepdims=True))
            return m_new, jnp.exp2(m_old - m_new), jnp.exp2(s - m_new)

        def pv(n, sm):
            m_new, a, p = sm
            pb = p.astype(BF16)
            tot = a * states[n][1]
            for i in range(grp):
                tot = tot + jnp.dot(vt_ref[0, items[n][0], j0 + i], pb[i * blk:(i + 1) * blk],
                                    preferred_element_type=F32)
            return m_new, tot

        new = skewed([qk, softmax, pv], MOBA_FAR_LAGS, todo)
        return tuple(new.get(n, states[n]) for n in range(len(items)))

    sub = nq // grp
    states = lax.fori_loop(0, pair * sub, far, tuple(states))
    for extra in range(1, sub):
        later = [n for n, (_, j) in enumerate(items) if j // grp >= extra]
        states = far(pair * sub + extra - 1, states, later)

    for n, (h, j) in enumerate(items):
        acc = states[n][1]
        o_ref[0, h * hd:(h + 1) * hd, j * blk:(j + 1) * blk] = (
            acc[0:hd] / acc[hd:hd + 1]).astype(o_ref.dtype)


def _moba(qt, k_aug, vt, bias):
    bsz, nh, hd, s = qt.shape
    blk = MOBA_BLOCK
    hp = MOBA_HEADS_PER_STEP
    nq = MOBA_QBLOCKS_PER_STEP
    nblk = s // blk
    assert nh % hp == 0 and nblk % nq == 0 and nq % MOBA_FAR_GROUP == 0
    return pl.pallas_call(
        _moba_kernel,
        grid=(bsz, nh // hp, nblk // nq),
        in_specs=[pl.BlockSpec((1, hp, hd, nq * blk), lambda b, g, i: (b, g, 0, i)),
                  pl.BlockSpec(memory_space=pl.ANY),
                  pl.BlockSpec(memory_space=pl.ANY),
                  pl.BlockSpec((hp, blk, 2 * blk), lambda b, g, i: (g, 0, 0),
                               pipeline_mode=pl.Buffered(1))],
        out_specs=pl.BlockSpec((1, hp * hd, nq * blk), lambda b, g, i: (b, g, i)),
        out_shape=jax.ShapeDtypeStruct((bsz, nh * hd, s), BF16),
        scratch_shapes=[pltpu.VMEM((hp, MOBA_MAX_BLOCKS, LANES), F32),
                        pltpu.VMEM((hp * nq, LANES, blk), BF16),
                        pltpu.VMEM((1, hp, nblk, blk, LANES), BF16),
                        pltpu.VMEM((1, hp, nblk, MOBA_VT_ROWS, blk), BF16),
                        pltpu.SemaphoreType.DMA((2,))],
        compiler_params=_cparams(("arbitrary", "arbitrary", "arbitrary")),
        name="moba",
    )(qt, k_aug, vt, bias)


def _gla_kernel(q_ref, k_ref, v_ref, gk_ref, wup_ref, bgk_ref, r_ref, gn_ref, o_ref,
                state_ref, *, nchunk):
    ch = GLA_CHUNK
    tc = nchunk * ch
    dk, dv = GLA_DK, GLA_DV

    @pl.when(pl.program_id(1) == 0)
    def _():
        state_ref[...] = jnp.zeros(state_ref.shape, F32)

    rin = lax.broadcasted_iota(I32, (tc, dk), 0) & (ch - 1)
    causal = (lax.broadcasted_iota(I32, (ch, ch), 1) <= lax.broadcasted_iota(I32, (ch, ch), 0))
    eye = (lax.broadcasted_iota(I32, (dk, dk), 0) == lax.broadcasted_iota(I32, (dk, dk), 1))
    chunks = [slice(n * ch, (n + 1) * ch) for n in range(nchunk)]
    gk = gk_ref[...]
    gk_hi = gk.astype(BF16)
    gk_lo = (gk - gk_hi.astype(F32)).astype(BF16)

    def prep(h, _):
        ks = slice(h * dk, (h + 1) * dk)
        w = wup_ref[:, ks]
        w_hi = w.astype(BF16)
        w_lo = (w - w_hi.astype(F32)).astype(BF16)
        z = (jnp.dot(gk_hi, w_hi, preferred_element_type=F32)
             + jnp.dot(gk_lo, w_hi, preferred_element_type=F32)
             + jnp.dot(gk_hi, w_lo, preferred_element_type=F32) + bgk_ref[:, ks])
        log_a = ((jnp.minimum(z, 0.0) - jnp.log(1.0 + jnp.exp(-jnp.abs(z))))
                 / GLA_GATE_NORMALIZER)
        b = log_a
        sh = 1
        while sh < ch:
            b = b + jnp.where(rin >= sh, pltpu.roll(b, sh, axis=0), 0.0)
            sh *= 2
        q = q_ref[:, ks].astype(F32) * (dk ** -0.5)
        k = k_ref[:, ks].astype(F32)
        q_g = (q * jnp.exp(b)).astype(BF16)
        k_g = (k * jnp.exp(-b)).astype(BF16)
        b3 = b.reshape(nchunk, ch, dk)
        b_last = b3[:, ch - 1:ch, :]
        k_end = (k * jnp.exp(jnp.broadcast_to(b_last, b3.shape) - b3).reshape(tc, dk)
                 ).astype(BF16)
        return q_g, k_g, k_end, jnp.exp(b_last)

    def local(h, pre):
        q_g, k_g, k_end, decay = pre
        o_intra, kv, decay_col = [], [], []
        for n, sl in enumerate(chunks):
            v_c = v_ref[sl, h * dv:(h + 1) * dv]
            att = jnp.where(causal, _nt_dot(q_g[sl], k_g[sl]), 0.0)
            o_intra.append(jnp.dot(att.astype(BF16), v_c, preferred_element_type=F32))
            kv.append(lax.dot_general(k_end[sl], v_c, (((0,), (0,)), ((), ())),
                                      preferred_element_type=F32))
            decay_col.append(jnp.sum(
                jnp.where(eye, jnp.broadcast_to(decay[n], (dk, dk)), 0.0),
                axis=1, keepdims=True))
        return q_g, o_intra, kv, decay_col

    def chain(h, loc):
        q_g, o_intra, kv, decay_col = loc
        state = state_ref[h]
        outs = []
        for n, sl in enumerate(chunks):
            outs.append(o_intra[n] + jnp.dot(q_g[sl], state.astype(BF16),
                                             preferred_element_type=F32))
            state = decay_col[n] * state + kv[n]
        state_ref[h] = state
        return jnp.concatenate(outs, axis=0)

    def finish(h, o):
        vs = slice(h * dv, (h + 1) * dv)
        o_ref[:, vs] = (_rms(o) * gn_ref[...] * r_ref[:, vs].astype(F32)).astype(o_ref.dtype)
        return None

    stages = [prep, local, chain, finish]
    vals = [None] * GLA_HEADS
    for step in range(GLA_HEADS + len(stages) - 1):
        for si, stage in enumerate(stages):
            h = step - si
            if 0 <= h < GLA_HEADS:
                vals[h] = stage(h, vals[h])


def _gla(qkb, vb, gk, wup, bgk, r_act, gn, bsz, seq, nchunk):
    t = qkb.shape[0]
    tc = nchunk * GLA_CHUNK
    nc = seq // tc
    rowblk = lambda w, off: pl.BlockSpec((tc, w), lambda b, c: (b * nc + c, off))
    full = lambda a: pl.BlockSpec(a.shape, lambda b, c: (0, 0))
    return pl.pallas_call(
        functools.partial(_gla_kernel, nchunk=nchunk),
        grid=(bsz, nc),
        in_specs=[rowblk(GLA_KEY_DIM, 0), rowblk(GLA_KEY_DIM, 1), rowblk(GLA_VALUE_DIM, 0),
                  rowblk(LANES, 0), full(wup), full(bgk), rowblk(GLA_VALUE_DIM, 0), full(gn)],
        out_specs=rowblk(GLA_VALUE_DIM, 0),
        out_shape=jax.ShapeDtypeStruct((t, GLA_VALUE_DIM), BF16),
        scratch_shapes=[pltpu.VMEM((GLA_HEADS, GLA_DK, GLA_DV), F32)],
        compiler_params=_cparams(("arbitrary", "arbitrary")),
        name="gla",
    )(qkb, qkb, vb, gk, wup, bgk, r_act, gn)


def _merge_kernel(ya_ref, yb_ref, g_ref, x_ref, wpa_ref, wpb_ref, wout_ref, g1_ref,
                  nw_ref, sc_ref, sh_ref, wr_ref, br_ref,
                  x1_ref, h2_ref, idx_ref, wts_ref, rank_ref, cnt_ref, carry_ref):
    @pl.when(pl.program_id(0) == 0)
    def _():
        carry_ref[...] = jnp.zeros(carry_ref.shape, F32)

    pa = lax.dot_general(ya_ref[0], wpa_ref[...], (((0,), (0,)), ((), ())),
                         preferred_element_type=F32)
    pb = jnp.dot(yb_ref[...], wpb_ref[...], preferred_element_type=F32)
    mixed = (g_ref[:, 0:D_MODEL].astype(F32) * pa
             + g_ref[:, D_MODEL:2 * D_MODEL].astype(F32) * pb)
    y = jnp.dot(mixed.astype(BF16), wout_ref[...], preferred_element_type=F32)
    x1 = x_ref[...] + g1_ref[0] * y
    x1_ref[...] = x1
    h2 = _rms(x1) * nw_ref[...]
    h2 = h2 * (1.0 + sc_ref[0]) + sh_ref[0]
    _store_rows(h2_ref, _pack_halves(h2))
    h_hi = h2.astype(BF16)
    h_lo = (h2 - h_hi.astype(F32)).astype(BF16)
    w = wr_ref[...]
    w_hi = w.astype(BF16)
    w_lo = (w - w_hi.astype(F32)).astype(BF16)
    logits = (_nt_dot(w_hi, h_hi) + _nt_dot(w_hi, h_lo) + _nt_dot(w_lo, h_hi)
              + br_ref[...])
    rowf = lax.broadcasted_iota(I32, logits.shape, 0).astype(F32)
    vals, idxs = [], []
    cur = logits
    for _ in range(TOP_K):
        mx = jnp.max(cur, axis=0, keepdims=True)
        first = jnp.min(jnp.where(cur == mx, rowf, float(N_EXPERTS)), axis=0, keepdims=True)
        vals.append(mx)
        idxs.append(first)
        cur = jnp.where(rowf == first, -jnp.inf, cur)
    es = [jnp.exp(v - vals[0]) for v in vals]
    tot = es[0]
    for e in es[1:]:
        tot = tot + e
    idx_ref[...] = jnp.concatenate(idxs, axis=0).astype(I32)
    wts_ref[...] = jnp.concatenate([e / tot for e in es], axis=0)
    tm = logits.shape[1]
    before = (lax.broadcasted_iota(I32, (tm, tm), 0) < lax.broadcasted_iota(I32, (tm, tm), 1))
    upper = jnp.where(before, 1.0, 0.0).astype(BF16)
    carry = carry_ref[:, 0:1]
    ranks = []
    for k in range(TOP_K):
        onehot = idxs[k] == rowf
        onef = jnp.where(onehot, 1.0, 0.0)
        earlier = jnp.dot(onef.astype(BF16), upper, preferred_element_type=F32) + carry
        ranks.append(jnp.sum(jnp.where(onehot, earlier, 0.0), axis=0, keepdims=True))
        carry = carry + jnp.sum(onef, axis=1, keepdims=True)
    rank_ref[...] = jnp.concatenate(ranks, axis=0).astype(I32)
    total = jnp.broadcast_to(carry, carry_ref.shape)
    carry_ref[...] = total
    cnt_ref[...] = total


def _merge(ya, yb, gates, x2d, wpa, wpb, wout, g1, nw, sc, sh, wr_t, br, seq, tm):
    t, d = x2d.shape
    tpb = seq // tm
    row = lambda w: pl.BlockSpec((tm, w), lambda i: (i, 0))
    full = lambda a: pl.BlockSpec(a.shape, lambda i: (0,) * a.ndim)
    per_b = pl.BlockSpec((1, 1, d), lambda i: (i // tpb, 0, 0))
    colblk = pl.BlockSpec((TOP_K, tm), lambda i: (0, i))
    return pl.pallas_call(
        _merge_kernel,
        grid=(t // tm,),
        in_specs=[pl.BlockSpec((1, MOBA_WIDTH, tm), lambda i: (i // tpb, 0, i % tpb)),
                  row(GLA_VALUE_DIM), row(2 * D_MODEL), row(d),
                  full(wpa), full(wpb), full(wout), per_b,
                  pl.BlockSpec((1, d), lambda i: (0, 0)), per_b, per_b,
                  full(wr_t), full(br)],
        out_specs=[row(d), pl.BlockSpec((tm * ROW_SLABS, LANES), lambda i: (i, 0)),
                   colblk, colblk, colblk,
                   pl.BlockSpec((N_EXPERTS, LANES), lambda i: (0, 0))],
        out_shape=[jax.ShapeDtypeStruct((t, d), F32),
                   jax.ShapeDtypeStruct((t * ROW_SLABS, LANES), U32),
                   jax.ShapeDtypeStruct((TOP_K, t), I32), jax.ShapeDtypeStruct((TOP_K, t), F32),
                   jax.ShapeDtypeStruct((TOP_K, t), I32),
                   jax.ShapeDtypeStruct((N_EXPERTS, LANES), F32)],
        scratch_shapes=[pltpu.VMEM((N_EXPERTS, LANES), F32)],
        compiler_params=_cparams(("arbitrary",)),
        name="merge",
    )(ya, yb, gates, x2d, wpa, wpb, wout, g1, nw.reshape(1, d), sc, sh, wr_t, br)


def _dest_kernel(pstart_ref, idx_ref, rank_ref, dest_ref):
    idx = idx_ref[...]
    off = jnp.zeros(idx.shape, I32)
    for e in range(N_EXPERTS):
        off = jnp.where(idx == e, pstart_ref[e], off)
    dest_ref[...] = rank_ref[...] + off


def _dest(pstart, idx, rank, tm):
    t = idx.shape[1]
    blk = pl.BlockSpec((TOP_K, tm), lambda i: (0, i))
    return pl.pallas_call(
        _dest_kernel,
        grid=(t // tm,),
        in_specs=[pl.BlockSpec(memory_space=pltpu.SMEM), blk, blk],
        out_specs=blk,
        out_shape=jax.ShapeDtypeStruct((TOP_K, t), I32),
        compiler_params=_cparams(("arbitrary",)),
        name="dest",
    )(pstart, idx, rank)


def _scatter_kernel(dest_ref, zrow_ref, h_ref, xout_hbm, zbuf, stage, sems, zsem, *, tm):
    @pl.when(pl.program_id(0) == 0)
    def _():
        zbuf[...] = jnp.zeros(zbuf.shape, zbuf.dtype)

        def zero_rows(row0):
            line0 = pl.multiple_of(row0 * ROW_SLABS, MOE_ROWS * ROW_SLABS)
            return pltpu.make_async_copy(
                zbuf, xout_hbm.at[pl.ds(line0, MOE_ROWS * ROW_SLABS)], zsem)

        for e in range(N_EXPERTS):
            zero_rows(zrow_ref[e]).start()
        for e in range(N_EXPERTS):
            zero_rows(zrow_ref[e]).wait()

        def start_tail(j, carry):
            zero_rows(j * MOE_ROWS).start()
            return carry

        def wait_tail(j, carry):
            zero_rows(j * MOE_ROWS).wait()
            return carry

        n_all = xout_hbm.shape[0] // (MOE_ROWS * ROW_SLABS)
        lax.fori_loop(zrow_ref[N_EXPERTS], n_all, start_tail, 0)
        lax.fori_loop(zrow_ref[N_EXPERTS], n_all, wait_tail, 0)

    i = pl.program_id(0)
    slot = i % 2
    src = stage.at[slot]
    src[...] = h_ref[...]

    def issue(g, carry):
        base = pl.multiple_of(g * ROW_UNROLL, ROW_UNROLL)
        for j in range(ROW_UNROLL):
            for k in range(TOP_K):
                d = dest_ref[k, base + j]
                pltpu.make_async_copy(_row(src, base + j), _row(xout_hbm, d),
                                      sems.at[slot]).start(priority=k % 2)
        return carry

    lax.fori_loop(0, tm // ROW_UNROLL, issue, 0)

    def wait_tile(s):
        for k in range(TOP_K):
            pltpu.make_async_copy(stage.at[s], xout_hbm.at[pl.ds(0, tm * ROW_SLABS)],
                                  sems.at[s]).wait()

    @pl.when(i > 0)
    def _():
        wait_tile(1 - slot)

    @pl.when(i == pl.num_programs(0) - 1)
    def _():
        wait_tile(slot)


def _scatter(dest, last_block_row, h2, n_pad, tm):
    t = h2.shape[0] // ROW_SLABS
    return pl.pallas_call(
        functools.partial(_scatter_kernel, tm=tm),
        grid=(t // tm,),
        in_specs=[pl.BlockSpec((TOP_K, tm), lambda i: (0, i), memory_space=pltpu.SMEM),
                  pl.BlockSpec(memory_space=pltpu.SMEM),
                  pl.BlockSpec((tm * ROW_SLABS, LANES), lambda i: (i, 0))],
        out_specs=pl.BlockSpec(memory_space=pl.ANY),
        out_shape=jax.ShapeDtypeStruct((n_pad * ROW_SLABS, LANES), h2.dtype),
        scratch_shapes=[pltpu.VMEM((MOE_ROWS * ROW_SLABS, LANES), h2.dtype),
                        pltpu.VMEM((2, tm * ROW_SLABS, LANES), h2.dtype),
                        pltpu.SemaphoreType.DMA((2,)), pltpu.SemaphoreType.DMA],
        compiler_params=_cparams(("arbitrary",)),
        name="scatter",
    )(dest, last_block_row, h2)


def _expert_kernel(be_ref, nused_ref, x_ref, wg_ref, bg_ref, wu_ref, bu_ref, wd_ref, bd_ref,
                   o_ref, wgb_ref, wub_ref, wdb_ref):
    i = pl.program_id(0)
    prev = be_ref[jnp.maximum(i - 1, 0)]

    @pl.when(jnp.logical_or(i == 0, be_ref[i] != prev))
    def _():
        wgb_ref[...] = wg_ref[0].astype(BF16)
        wub_ref[...] = wu_ref[0].astype(BF16)
        wdb_ref[...] = wd_ref[0].astype(BF16)

    @pl.when(i < nused_ref[0])
    def _():
        xb = jnp.concatenate([v.astype(BF16) for v in _unpack_halves(_load_rows(x_ref))],
                             axis=1)
        g = jnp.minimum(jnp.dot(xb, wgb_ref[...], preferred_element_type=F32) + bg_ref[0],
                        SWIGLU_LIMIT)
        u = jnp.clip(jnp.dot(xb, wub_ref[...], preferred_element_type=F32) + bu_ref[0],
                     -SWIGLU_LIMIT, SWIGLU_LIMIT)
        act = g * _sigmoid(SWIGLU_ALPHA * g) * (u + 1.0)
        _store_rows(o_ref, _pack_halves(
            jnp.dot(act.astype(BF16), wdb_ref[...], preferred_element_type=F32) + bd_ref[0]))

    @pl.when(i >= nused_ref[0])
    def _():
        o_ref[...] = jnp.zeros(o_ref.shape, o_ref.dtype)


def _expert(block_expert, n_used, x_pad, wg, bg, wu, bu, wd, bd):
    lines = MOE_ROWS * ROW_SLABS
    _, d, f = wg.shape
    nblk = x_pad.shape[0] // lines
    wspec = lambda a, b: pl.BlockSpec((1, a, b), lambda i, be, nu: (be[i], 0, 0))
    grid_spec = pltpu.PrefetchScalarGridSpec(
        num_scalar_prefetch=2,
        grid=(nblk,),
        in_specs=[pl.BlockSpec((lines, LANES),
                               lambda i, be, nu: (jnp.minimum(i, nu[0] - 1), 0)),
                  wspec(d, f), wspec(1, f), wspec(d, f), wspec(1, f), wspec(f, d), wspec(1, d)],
        out_specs=pl.BlockSpec((lines, LANES), lambda i, be, nu: (i, 0)),
        scratch_shapes=[pltpu.VMEM((d, f), BF16), pltpu.VMEM((d, f), BF16),
                        pltpu.VMEM((f, d), BF16)])
    return pl.pallas_call(
        _expert_kernel,
        grid_spec=grid_spec,
        out_shape=jax.ShapeDtypeStruct(x_pad.shape, U32),
        compiler_params=_cparams(("arbitrary",)),
        name="expert",
    )(block_expert, n_used, x_pad, wg, bg.reshape(N_EXPERTS, 1, f), wu,
      bu.reshape(N_EXPERTS, 1, f), wd, bd.reshape(N_EXPERTS, 1, d))


def _combine_kernel(dest_ref, y_hbm, x1_ref, wts_ref, g2_ref, nf_ref, o_ref, ybuf_a, ybuf_b,
                    sems, *, tm, final, n_tiles):
    i = pl.program_id(0)
    bufs = (ybuf_a, ybuf_b)

    def issue(slot):
        for tt in range(tm):
            for k in range(TOP_K):
                d = dest_ref[k, tt]
                pltpu.make_async_copy(_row(y_hbm, d), _row(bufs[slot].at[k], tt),
                                      sems.at[slot]).start(priority=k % 2)

    def wait(slot):
        for k in range(TOP_K):
            pltpu.make_async_copy(y_hbm.at[pl.ds(0, tm * ROW_SLABS)], bufs[slot].at[k],
                                  sems.at[slot]).wait()

    def finish(slot):
        w = wts_ref[...].T
        lo, hi = None, None
        for k in range(TOP_K):
            y_lo, y_hi = _unpack_halves(_load_rows(bufs[slot].at[k]))
            lo = w[:, k:k + 1] * y_lo + (0.0 if lo is None else lo)
            hi = w[:, k:k + 1] * y_hi + (0.0 if hi is None else hi)
        moe = jnp.concatenate([lo, hi], axis=1)
        x2 = x1_ref[...] + g2_ref[0] * moe
        o_ref[...] = _rms(x2) * nf_ref[...] if final else x2

    @pl.when(i == 0)
    def _():
        issue(0)

    for parity in range(2):
        @pl.when(jnp.logical_and(jnp.logical_and(i > 0, i < n_tiles), i % 2 == parity))
        def _(parity=parity):
            wait(1 - parity)
            issue(parity)
            finish(1 - parity)

    @pl.when(i == n_tiles)
    def _():
        wait((n_tiles - 1) % 2)
        finish((n_tiles - 1) % 2)


def _combine(dest, y_pad, x1, wts, g2, nf, seq, tm, final):
    t, d = x1.shape
    tpb = seq // tm
    n_tiles = t // tm
    lag = lambda i: jnp.maximum(i - 1, 0)
    return pl.pallas_call(
        functools.partial(_combine_kernel, tm=tm, final=final, n_tiles=n_tiles),
        grid=(n_tiles + 1,),
        in_specs=[pl.BlockSpec((TOP_K, tm), lambda i: (0, jnp.minimum(i, n_tiles - 1)),
                               memory_space=pltpu.SMEM),
                  pl.BlockSpec(memory_space=pl.ANY),
                  pl.BlockSpec((tm, d), lambda i: (lag(i), 0)),
                  pl.BlockSpec((TOP_K, tm), lambda i: (0, lag(i))),
                  pl.BlockSpec((1, 1, d), lambda i: (lag(i) // tpb, 0, 0)),
                  pl.BlockSpec((1, d), lambda i: (0, 0))],
        out_specs=pl.BlockSpec((tm, d), lambda i: (lag(i), 0)),
        out_shape=jax.ShapeDtypeStruct((t, d), F32),
        scratch_shapes=[pltpu.VMEM((TOP_K, tm * ROW_SLABS, LANES), y_pad.dtype),
                        pltpu.VMEM((TOP_K, tm * ROW_SLABS, LANES), y_pad.dtype),
                        pltpu.SemaphoreType.DMA((2,))],
        compiler_params=_cparams(("arbitrary",)),
        name="combine",
    )(dest, y_pad, x1, wts, g2, nf.reshape(1, d))


def _pick(n, cands):
    for c in cands:
        if n % c == 0:
            return c
    raise ValueError(f"no tile in {cands} divides {n}")


def kernel(x, c, rel_bias, w_ada, b_ada, norm_mix, w_in, w_gk_up, b_gk, gla_norm,
           w_proj_moba, w_proj_gla, w_out, norm_ffn, w_router, b_router,
           w_gate, b_gate, w_up, b_up, w_down, b_down, norm_final):
    bsz, seq, d = x.shape
    depth = w_ada.shape[0]
    assert d == D_MODEL and seq % MOBA_BLOCK == 0 and seq // MOBA_BLOCK <= MOBA_MAX_BLOCKS
    t = bsz * seq
    tm = _pick(seq, (512, 256))
    nchunk = _pick(seq // GLA_CHUNK, (16, 8, 4))
    n_blk = seq // MOBA_BLOCK
    x2d = x.reshape(t, d)
    bias = _bias_tiles(rel_bias)
    per_b = lambda v: v.reshape(bsz, 1, d)

    for l in range(depth):
        mod = _ada(c, w_ada[l], b_ada[l])
        sh1, sc1, g1, sh2, sc2, g2 = [per_b(m) for m in jnp.split(mod, 6, axis=-1)]
        qt, k_aug, vt, qkb, vb, gk, r_act, gates = _inproj(
            x2d, norm_mix[l], sc1, sh1, _regroup_w_in(w_in[l]), bsz, seq, tm)
        ya = _moba(qt, k_aug, vt, bias)
        wup = jnp.pad(w_gk_up[l], ((0, LANES - GLA_GATE_RANK), (0, 0)))
        yb = _gla(qkb, vb, gk, wup, b_gk[l].reshape(1, -1), r_act,
                  gla_norm[l].reshape(1, -1), bsz, seq, nchunk)
        x1, h2, idx, wts, rank, cnt = _merge(
            ya, yb, gates, x2d, w_proj_moba[l].astype(BF16), w_proj_gla[l].astype(BF16),
            w_out[l].astype(BF16), g1, norm_ffn[l], sc2, sh2,
            w_router[l].T, b_router[l].reshape(N_EXPERTS, 1), seq, tm)
        counts = cnt[:, 0].astype(I32)
        padded = (counts + MOE_ROWS - 1) // MOE_ROWS * MOE_ROWS
        pcum = jnp.cumsum(padded)
        pstart = (pcum - padded).astype(I32)
        n_blocks = (t * TOP_K + MOE_ROWS - 1) // MOE_ROWS + N_EXPERTS
        block_row0 = jnp.arange(n_blocks, dtype=I32) * MOE_ROWS
        block_expert = jnp.minimum(
            jnp.sum((pcum[None, :] <= block_row0[:, None]).astype(I32), axis=1),
            N_EXPERTS - 1).astype(I32)
        n_used = (pcum[-1:] // MOE_ROWS).astype(I32)
        dest = _dest(pstart, idx, rank, _pick(t, (8192, 4096, 2048, 1024, 512, 256)))
        tg = _pick(seq, (256,))
        zero_info = jnp.concatenate([jnp.maximum(pcum - MOE_ROWS, 0).astype(I32), n_used])
        x_pad = _scatter(dest, zero_info, h2, n_blocks * MOE_ROWS, tg)
        y_pad = _expert(block_expert, n_used, x_pad, w_gate[l], b_gate[l], w_up[l], b_up[l],
                        w_down[l], b_down[l])
        x2d = _combine(dest, y_pad, x1, wts, g2, norm_final, seq, tg, l == depth - 1)
    return x2d.reshape(bsz, seq, d)
```

```python
import functools
import math

import numpy as np
import jax
import jax.numpy as jnp
from jax import lax
from jax.experimental import pallas as pl
from jax.experimental.pallas import tpu as pltpu

F32 = jnp.float32
BF16 = jnp.bfloat16
I32 = jnp.int32
HIGHEST = lax.Precision.HIGHEST

D_MODEL = 1024
MOBA_HEADS = 8
MOBA_HEAD_DIM = 64
MOBA_WIDTH = MOBA_HEADS * MOBA_HEAD_DIM
MOBA_BLOCK = 256
MOBA_TOPK = 3
MOBA_MAX_BLOCKS = 32
REL_BUCKETS = 32
REL_MAX_DIST = 128
GLA_HEADS = 4
GLA_KEY_DIM = D_MODEL // 2
GLA_VALUE_DIM = D_MODEL
GLA_DK = GLA_KEY_DIM // GLA_HEADS
GLA_DV = GLA_VALUE_DIM // GLA_HEADS
GLA_GATE_RANK = 16
GLA_GATE_NORMALIZER = 16.0
GLA_CHUNK = 64
N_EXPERTS = 32
TOP_K = 4
D_FF = D_MODEL
SWIGLU_ALPHA = 1.702
SWIGLU_LIMIT = 7.0
MOE_ROWS = 512
ROW_UNROLL = 8
EPS = 1e-6
LANES = 128
SUBLANES = 8
BF16_SUBLANES = 16
ADA_COLS = 768
NEG_BIG = -1e30
LOG2E = math.log2(math.e)
VMEM_LIMIT = 56 * 1024 * 1024


def _cparams(sem, vmem=None):
    return pltpu.CompilerParams(dimension_semantics=sem,
                                vmem_limit_bytes=vmem or VMEM_LIMIT)


def _nt_dot(a, b, **kw):
    return lax.dot_general(a, b, (((1,), (1,)), ((), ())),
                           preferred_element_type=F32, **kw)


def _rms(x):
    return x * lax.rsqrt(jnp.mean(x * x, axis=-1, keepdims=True) + EPS)


def _sigmoid(x):
    return 1.0 / (1.0 + jnp.exp(-x))


U32 = jnp.uint32
_HI16 = 0xFFFF0000


def _pack_halves(x):
    n = x.shape[1] // 2
    lo = pltpu.bitcast(x[:, :n].astype(BF16).astype(F32), U32)
    hi = pltpu.bitcast(x[:, n:].astype(BF16).astype(F32), U32)
    return (hi & U32(_HI16)) | (lo >> 16)


def _unpack_halves(w):
    return (pltpu.bitcast(w << 16, F32), pltpu.bitcast(w & U32(_HI16), F32))


ROW_SLABS = D_MODEL // 2 // LANES


def _store_rows(ref, words):
    m = words.shape[0]
    for c in range(ROW_SLABS):
        ref[pl.ds(c, m, stride=ROW_SLABS), :] = words[:, c * LANES:(c + 1) * LANES]


def _load_rows(ref):
    m = ref.shape[0] // ROW_SLABS
    return jnp.concatenate(
        [ref[pl.ds(c, m, stride=ROW_SLABS), :] for c in range(ROW_SLABS)], axis=1)


def _row(ref, i):
    return ref.at[pl.ds(pl.multiple_of(i * ROW_SLABS, ROW_SLABS), ROW_SLABS)]


def _ada_kernel(c_ref, w_ref, b_ref, o_ref):
    c = c_ref[...]
    s = c * _sigmoid(c)
    o_ref[...] = jnp.dot(s, w_ref[...], precision=HIGHEST,
                         preferred_element_type=F32) + b_ref[...]


def _ada(c, w, b):
    bsz, d = c.shape
    n = w.shape[1]
    rows = -(-bsz // SUBLANES) * SUBLANES
    cp = jnp.zeros((rows, d), F32).at[:bsz].set(c)
    tn = ADA_COLS
    out = pl.pallas_call(
        _ada_kernel,
        grid=(n // tn,),
        in_specs=[pl.BlockSpec((rows, d), lambda j: (0, 0)),
                  pl.BlockSpec((d, tn), lambda j: (0, j)),
                  pl.BlockSpec((1, tn), lambda j: (0, j))],
        out_specs=pl.BlockSpec((rows, tn), lambda j: (0, j)),
        out_shape=jax.ShapeDtypeStruct((rows, n), F32),
        compiler_params=_cparams(("arbitrary",)),
        name="ada",
    )(cp, w, b.reshape(1, n))
    return out[:bsz]


_OFF_QA = 0
_OFF_KA = _OFF_QA + MOBA_WIDTH
_OFF_VA = _OFF_KA + MOBA_WIDTH
_OFF_QKB = _OFF_VA + MOBA_WIDTH
_OFF_VB = _OFF_QKB + 2 * GLA_KEY_DIM
_OFF_GK = _OFF_VB + GLA_VALUE_DIM
_OFF_R = _OFF_GK + LANES
_OFF_G = _OFF_R + GLA_VALUE_DIM
_W_CAT = _OFF_G + 2 * D_MODEL


def _regroup_w_in(w):
    o_gk = 3 * MOBA_WIDTH + 2 * GLA_KEY_DIM + GLA_VALUE_DIM
    w = w.astype(BF16)
    gk = jnp.pad(w[:, o_gk:o_gk + GLA_GATE_RANK], ((0, 0), (0, LANES - GLA_GATE_RANK)))
    return jnp.concatenate([w[:, :o_gk], gk, w[:, o_gk + GLA_GATE_RANK:]], axis=1)


def _inproj_kernel(x_ref, nw_ref, sc_ref, sh_ref, w_ref,
                   qt_ref, ka_ref, vt_ref, qkb_ref, vb_ref, gk_ref, r_ref, g_ref, *, tpb):
    tm = x_ref.shape[0]
    hd = MOBA_HEAD_DIM
    nbt = tm // MOBA_BLOCK
    h = _rms(x_ref[...]) * nw_ref[...]
    h = h * (1.0 + sc_ref[0]) + sh_ref[0]
    hb = h.astype(BF16)

    def mm(a, b):
        return jnp.dot(hb, w_ref[:, a:b], preferred_element_type=F32)

    q_t = (mm(_OFF_QA, _OFF_KA) * (hd ** -0.5 * LOG2E)).T
    v_t = mm(_OFF_VA, _OFF_QKB).T
    k_all = mm(_OFF_KA, _OFF_VA)
    blk0 = (pl.program_id(0) % tpb) * nbt
    lane = lax.broadcasted_iota(I32, (MOBA_BLOCK, LANES), 1)
    ones_rows = jnp.where(
        lax.broadcasted_iota(I32, (MOBA_VT_ROWS - hd, MOBA_BLOCK), 0) == 0, 1.0, 0.0)
    heads_per_tile = LANES // hd
    for hh in range(MOBA_HEADS):
        qt_ref[0, hh] = q_t[hh * hd:(hh + 1) * hd].astype(BF16)
        tile = hh // heads_per_tile
        k_h = k_all[:, tile * LANES:(tile + 1) * LANES]
        if hh % heads_per_tile:
            k_h = pltpu.roll(k_h, LANES - (hh % heads_per_tile) * hd, axis=1)
        for j in range(nbt):
            rows = slice(j * MOBA_BLOCK, (j + 1) * MOBA_BLOCK)
            onehot = jnp.where(lane == hd + blk0 + j, 1.0, 0.0)
            ka_ref[0, hh, j] = jnp.where(lane < hd, k_h[rows], onehot).astype(BF16)
            vt_ref[0, hh, j, 0:hd, :] = v_t[hh * hd:(hh + 1) * hd, rows].astype(BF16)
            vt_ref[0, hh, j, hd:MOBA_VT_ROWS, :] = ones_rows.astype(BF16)
    qkb_ref[...] = mm(_OFF_QKB, _OFF_VB).astype(BF16)
    vb_ref[...] = mm(_OFF_VB, _OFF_GK).astype(BF16)
    gk_ref[...] = mm(_OFF_GK, _OFF_R)
    r = mm(_OFF_R, _OFF_G)
    r_ref[...] = (r * _sigmoid(r)).astype(BF16)
    g_ref[...] = _sigmoid(mm(_OFF_G, _W_CAT)).astype(BF16)


def _inproj(x2d, nw, sc, sh, w_cat, bsz, seq, tm):
    t, d = x2d.shape
    tpb = seq // tm
    nbt = tm // MOBA_BLOCK
    nh, hd = MOBA_HEADS, MOBA_HEAD_DIM
    row = lambda w: pl.BlockSpec((tm, w), lambda i: (i, 0))
    per_b = pl.BlockSpec((1, 1, d), lambda i: (i // tpb, 0, 0))
    rows_out = [(2 * GLA_KEY_DIM, BF16), (GLA_VALUE_DIM, BF16), (LANES, F32),
                (GLA_VALUE_DIM, BF16), (2 * D_MODEL, BF16)]
    return pl.pallas_call(
        functools.partial(_inproj_kernel, tpb=tpb),
        grid=(t // tm,),
        in_specs=[row(d), pl.BlockSpec((1, d), lambda i: (0, 0)), per_b, per_b,
                  pl.BlockSpec((d, _W_CAT), lambda i: (0, 0), pipeline_mode=pl.Buffered(1))],
        out_specs=[pl.BlockSpec((1, nh, hd, tm), lambda i: (i // tpb, 0, 0, i % tpb)),
                   pl.BlockSpec((1, nh, nbt, MOBA_BLOCK, LANES),
                                lambda i: (i // tpb, 0, i % tpb, 0, 0)),
                   pl.BlockSpec((1, nh, nbt, MOBA_VT_ROWS, MOBA_BLOCK),
                                lambda i: (i // tpb, 0, i % tpb, 0, 0))]
                  + [row(w) for w, _ in rows_out],
        out_shape=[jax.ShapeDtypeStruct((bsz, nh, hd, seq), BF16),
                   jax.ShapeDtypeStruct((bsz, nh, seq // MOBA_BLOCK, MOBA_BLOCK, LANES), BF16),
                   jax.ShapeDtypeStruct((bsz, nh, seq // MOBA_BLOCK, MOBA_VT_ROWS, MOBA_BLOCK),
                                        BF16)]
                  + [jax.ShapeDtypeStruct((t, w), dt) for w, dt in rows_out],
        compiler_params=_cparams(("arbitrary",)),
        name="inproj",
    )(x2d, nw.reshape(1, d), sc, sh, w_cat)


def _t5_bucket_np(n):
    n = np.maximum(n, 0)
    max_exact = REL_BUCKETS // 2
    nf = np.maximum(n, max_exact).astype(np.float32)
    large = max_exact + (np.log(nf / max_exact) / math.log(REL_MAX_DIST / max_exact)
                         * (REL_BUCKETS - max_exact)).astype(np.int32)
    large = np.minimum(large, REL_BUCKETS - 1)
    return np.where(n < max_exact, n, large).astype(np.int32)


def _bucket_table():
    kj = np.arange(MOBA_BLOCK)[:, None]
    qi = np.arange(2 * MOBA_BLOCK)[None, :] % MOBA_BLOCK
    prev = np.arange(2 * MOBA_BLOCK)[None, :] < MOBA_BLOCK
    bucket = _t5_bucket_np(qi - kj + np.where(prev, MOBA_BLOCK, 0))
    return np.where(prev | (kj <= qi), bucket, -1).astype(np.int32)


def _bias_kernel(rb_ref, bucket_ref, o_ref):
    h = pl.program_id(0)
    bk = bucket_ref[...]
    far = rb_ref[(REL_BUCKETS - 1) * MOBA_HEADS + h]
    acc = jnp.zeros(bk.shape, F32)
    for b in range(REL_BUCKETS):
        acc = jnp.where(bk == b, rb_ref[b * MOBA_HEADS + h] - far, acc)
    o_ref[0] = jnp.where(bk < 0, NEG_BIG, acc * LOG2E)


def _bias_tiles(rel_bias):
    bucket = jnp.asarray(_bucket_table())
    return pl.pallas_call(
        _bias_kernel,
        grid=(MOBA_HEADS,),
        in_specs=[pl.BlockSpec(memory_space=pltpu.SMEM),
                  pl.BlockSpec(bucket.shape, lambda h: (0, 0))],
        out_specs=pl.BlockSpec((1,) + bucket.shape, lambda h: (h, 0, 0)),
        out_shape=jax.ShapeDtypeStruct((MOBA_HEADS,) + bucket.shape, F32),
        compiler_params=_cparams(("arbitrary",)),
        name="bias",
    )(rel_bias.reshape(-1), bucket)


MOBA_HEADS_PER_STEP = 8
MOBA_FAR_GROUP = 2
MOBA_QBLOCKS_PER_STEP = 4
MOBA_FAR_LAGS = (0, 3, 6)
MOBA_NEAR_LAGS = (0, 3, 6, 9)
MOBA_VT_ROWS = MOBA_HEAD_DIM + BF16_SUBLANES


def _moba_kernel(qt_ref, k_hbm, vt_hbm, bias_ref, o_ref, kmean_ref, qa_ref, k_ref, vt_ref,
                 kv_sems):
    blk = MOBA_BLOCK
    nb = MOBA_MAX_BLOCKS
    hd = MOBA_HEAD_DIM
    hp = MOBA_HEADS_PER_STEP
    grp = MOBA_FAR_GROUP
    nq = MOBA_QBLOCKS_PER_STEP
    pair = pl.program_id(2)
    neg = -jnp.inf
    items = [(h, j) for j in range(nq) for h in range(hp)]

    @pl.when(pair == 0)
    def _():
        kmean_ref[...] = jnp.zeros(kmean_ref.shape, F32)

    def kv_copies(first_blk):
        src = (pl.program_id(0), pl.ds(pl.program_id(1) * hp, hp), pl.ds(first_blk, nq))
        dst = (0, slice(None), pl.ds(first_blk, nq))
        return (pltpu.make_async_copy(k_hbm.at[src], k_ref.at[dst], kv_sems.at[0]),
                pltpu.make_async_copy(vt_hbm.at[src], vt_ref.at[dst], kv_sems.at[1]))

    @pl.when(pair == 0)
    def _():
        for cp in kv_copies(0):
            cp.start()

    for cp in kv_copies(pair * nq):
        cp.wait()

    @pl.when(pair + 1 < pl.num_programs(2))
    def _():
        for cp in kv_copies((pair + 1) * nq):
            cp.start()

    for h in range(hp):
        for j in range(nq):
            kmean_ref[h, pl.ds(pair * nq + j, 1), :] = jnp.mean(
                k_ref[0, h, pair * nq + j].astype(F32), axis=0, keepdims=True)

    row = lax.broadcasted_iota(I32, (nb, blk), 0)
    rowf = row.astype(F32)
    pad = jnp.zeros((LANES - hd, blk), BF16)
    pad_hi = jnp.zeros((LANES - hd - nb, blk), BF16)

    def skewed(stages, lags, todo=None):
        todo = list(range(len(items))) if todo is None else todo
        vals = {}
        for step in range(len(todo) + lags[-1]):
            for stage, lag in zip(stages, lags):
                pos = step - lag
                if 0 <= pos < len(todo):
                    vals[todo[pos]] = stage(todo[pos], vals.get(todo[pos]))
        return vals

    def select(n, _):
        h, j = items[n]
        qi = pair * nq + j
        qt = qt_ref[0, h, :, j * blk:(j + 1) * blk]
        km = kmean_ref[h, :, 0:hd]
        km_hi = km.astype(BF16)
        km_lo = (km - km_hi.astype(F32)).astype(BF16)
        gate = (jnp.dot(km_hi, qt, preferred_element_type=F32)
                + jnp.dot(km_lo, qt, preferred_element_type=F32))
        g = jnp.where(row < qi, gate, neg)
        sel = jnp.zeros((nb, blk), F32)
        for _ in range(MOBA_TOPK):
            mx = jnp.max(g, axis=0, keepdims=True)
            first = jnp.min(jnp.where(g == mx, rowf, float(nb)), axis=0, keepdims=True)
            pick = rowf == jnp.where(mx > neg, first, -1.0)
            sel = jnp.where(pick, 1.0, sel)
            g = jnp.where(pick, neg, g)
        mask_prev = jnp.where(sel > 0.0, jnp.where(row == qi - 1, 0.0, NEG_BIG), NEG_BIG)
        mask_far = jnp.where(sel > 0.0, jnp.where(row < qi - 1, 0.0, NEG_BIG), NEG_BIG)
        qa_ref[n] = jnp.concatenate([qt, mask_far.astype(BF16), pad_hi], axis=0)
        return (jnp.concatenate([qt, pad], axis=0),
                jnp.concatenate([qt, mask_prev.astype(BF16), pad_hi], axis=0))

    def own_prev(n):
        h, j = items[n]
        qi = pair * nq + j
        return h, qi, jnp.maximum(qi - 1, 0)

    def near_scores(n, qa):
        h, qi, prev_j = own_prev(n)
        qa_own, qa_prev = qa
        s_own = jnp.dot(k_ref[0, h, qi], qa_own, preferred_element_type=F32)
        s_prev = jnp.dot(k_ref[0, h, prev_j], qa_prev, preferred_element_type=F32)
        return s_own, s_prev

    def near_softmax(n, ss):
        h = items[n][0]
        s_own, s_prev = ss
        s = jnp.concatenate([s_own + bias_ref[h, :, blk:2 * blk],
                             s_prev + bias_ref[h, :, 0:blk]], axis=0)
        m0 = jnp.max(s, axis=0, keepdims=True)
        return m0, jnp.exp2(s - m0)

    def near_pv(n, mp):
        h, qi, prev_j = own_prev(n)
        m0, p = mp
        pb = p.astype(BF16)
        acc = (jnp.dot(vt_ref[0, h, qi], pb[0:blk], preferred_element_type=F32)
               + jnp.dot(vt_ref[0, h, prev_j], pb[blk:2 * blk], preferred_element_type=F32))
        return m0, acc

    near = skewed([select, near_scores, near_softmax, near_pv], MOBA_NEAR_LAGS)
    states = tuple(near[n] for n in range(len(items)))

    def far(gi, states, todo=None):
        j0 = gi * grp

        def qk(n, _):
            kt = k_ref[0, items[n][0], pl.ds(j0, grp)].reshape(grp * blk, LANES)
            return jnp.dot(kt, qa_ref[n], preferred_element_type=F32)

        def softmax(n, s):
            m_old = states[n][0]
            m_new = jnp.maximum(m_old, jnp.max(s, axis=0, keepdims=True))
            return m_new, jnp.exp2(m_old - m_new), jnp.exp2(s - m_new)

        def pv(n, sm):
            m_new, a, p = sm
            pb = p.astype(BF16)
            tot = a * states[n][1]
            for i in range(grp):
                tot = tot + jnp.dot(vt_ref[0, items[n][0], j0 + i], pb[i * blk:(i + 1) * blk],
                                    preferred_element_type=F32)
            return m_new, tot

        new = skewed([qk, softmax, pv], MOBA_FAR_LAGS, todo)
        return tuple(new.get(n, states[n]) for n in range(len(items)))

    sub = nq // grp
    states = lax.fori_loop(0, pair * sub, far, tuple(states))
    for extra in range(1, sub):
        later = [n for n, (_, j) in enumerate(items) if j // grp >= extra]
        states = far(pair * sub + extra - 1, states, later)

    for n, (h, j) in enumerate(items):
        acc = states[n][1]
        o_ref[0, h * hd:(h + 1) * hd, j * blk:(j + 1) * blk] = (
            acc[0:hd] / acc[hd:hd + 1]).astype(o_ref.dtype)


def _moba(qt, k_aug, vt, bias):
    bsz, nh, hd, s = qt.shape
    blk = MOBA_BLOCK
    hp = MOBA_HEADS_PER_STEP
    nq = MOBA_QBLOCKS_PER_STEP
    nblk = s // blk
    assert nh % hp == 0 and nblk % nq == 0 and nq % MOBA_FAR_GROUP == 0
    return pl.pallas_call(
        _moba_kernel,
        grid=(bsz, nh // hp, nblk // nq),
        in_specs=[pl.BlockSpec((1, hp, hd, nq * blk), lambda b, g, i: (b, g, 0, i)),
                  pl.BlockSpec(memory_space=pl.ANY),
                  pl.BlockSpec(memory_space=pl.ANY),
                  pl.BlockSpec((hp, blk, 2 * blk), lambda b, g, i: (g, 0, 0),
                               pipeline_mode=pl.Buffered(1))],
        out_specs=pl.BlockSpec((1, hp * hd, nq * blk), lambda b, g, i: (b, g, i)),
        out_shape=jax.ShapeDtypeStruct((bsz, nh * hd, s), BF16),
        scratch_shapes=[pltpu.VMEM((hp, MOBA_MAX_BLOCKS, LANES), F32),
                        pltpu.VMEM((hp * nq, LANES, blk), BF16),
                        pltpu.VMEM((1, hp, nblk, blk, LANES), BF16),
                        pltpu.VMEM((1, hp, nblk, MOBA_VT_ROWS, blk), BF16),
                        pltpu.SemaphoreType.DMA((2,))],
        compiler_params=_cparams(("arbitrary", "arbitrary", "arbitrary")),
        name="moba",
    )(qt, k_aug, vt, bias)


def _gla_kernel(q_ref, k_ref, v_ref, gk_ref, wup_ref, bgk_ref, r_ref, gn_ref, o_ref,
                state_ref, *, nchunk):
    ch = GLA_CHUNK
    tc = nchunk * ch
    dk, dv = GLA_DK, GLA_DV

    @pl.when(pl.program_id(1) == 0)
    def _():
        state_ref[...] = jnp.zeros(state_ref.shape, F32)

    rin = lax.broadcasted_iota(I32, (tc, dk), 0) & (ch - 1)
    causal = (lax.broadcasted_iota(I32, (ch, ch), 1) <= lax.broadcasted_iota(I32, (ch, ch), 0))
    eye = (lax.broadcasted_iota(I32, (dk, dk), 0) == lax.broadcasted_iota(I32, (dk, dk), 1))
    chunks = [slice(n * ch, (n + 1) * ch) for n in range(nchunk)]
    gk = gk_ref[...]
    gk_hi = gk.astype(BF16)
    gk_lo = (gk - gk_hi.astype(F32)).astype(BF16)

    def prep(h, _):
        ks = slice(h * dk, (h + 1) * dk)
        w = wup_ref[:, ks]
        w_hi = w.astype(BF16)
        w_lo = (w - w_hi.astype(F32)).astype(BF16)
        z = (jnp.dot(gk_hi, w_hi, preferred_element_type=F32)
             + jnp.dot(gk_lo, w_hi, preferred_element_type=F32)
             + jnp.dot(gk_hi, w_lo, preferred_element_type=F32) + bgk_ref[:, ks])
        log_a = ((jnp.minimum(z, 0.0) - jnp.log(1.0 + jnp.exp(-jnp.abs(z))))
                 / GLA_GATE_NORMALIZER)
        b = log_a
        sh = 1
        while sh < ch:
            b = b + jnp.where(rin >= sh, pltpu.roll(b, sh, axis=0), 0.0)
            sh *= 2
        q = q_ref[:, ks].astype(F32) * (dk ** -0.5)
        k = k_ref[:, ks].astype(F32)
        q_g = (q * jnp.exp(b)).astype(BF16)
        k_g = (k * jnp.exp(-b)).astype(BF16)
        b3 = b.reshape(nchunk, ch, dk)
        b_last = b3[:, ch - 1:ch, :]
        k_end = (k * jnp.exp(jnp.broadcast_to(b_last, b3.shape) - b3).reshape(tc, dk)
                 ).astype(BF16)
        return q_g, k_g, k_end, jnp.exp(b_last)

    def local(h, pre):
        q_g, k_g, k_end, decay = pre
        o_intra, kv, decay_col = [], [], []
        for n, sl in enumerate(chunks):
            v_c = v_ref[sl, h * dv:(h + 1) * dv]
            att = jnp.where(causal, _nt_dot(q_g[sl], k_g[sl]), 0.0)
            o_intra.append(jnp.dot(att.astype(BF16), v_c, preferred_element_type=F32))
            kv.append(lax.dot_general(k_end[sl], v_c, (((0,), (0,)), ((), ())),
                                      preferred_element_type=F32))
            decay_col.append(jnp.sum(
                jnp.where(eye, jnp.broadcast_to(decay[n], (dk, dk)), 0.0),
                axis=1, keepdims=True))
        return q_g, o_intra, kv, decay_col

    def chain(h, loc):
        q_g, o_intra, kv, decay_col = loc
        state = state_ref[h]
        outs = []
        for n, sl in enumerate(chunks):
            outs.append(o_intra[n] + jnp.dot(q_g[sl], state.astype(BF16),
                                             preferred_element_type=F32))
            state = decay_col[n] * state + kv[n]
        state_ref[h] = state
        return jnp.concatenate(outs, axis=0)

    def finish(h, o):
        vs = slice(h * dv, (h + 1) * dv)
        o_ref[:, vs] = (_rms(o) * gn_ref[...] * r_ref[:, vs].astype(F32)).astype(o_ref.dtype)
        return None

    stages = [prep, local, chain, finish]
    vals = [None] * GLA_HEADS
    for step in range(GLA_HEADS + len(stages) - 1):
        for si, stage in enumerate(stages):
            h = step - si
            if 0 <= h < GLA_HEADS:
                vals[h] = stage(h, vals[h])


def _gla(qkb, vb, gk, wup, bgk, r_act, gn, bsz, seq, nchunk):
    t = qkb.shape[0]
    tc = nchunk * GLA_CHUNK
    nc = seq // tc
    rowblk = lambda w, off: pl.BlockSpec((tc, w), lambda b, c: (b * nc + c, off))
    full = lambda a: pl.BlockSpec(a.shape, lambda b, c: (0, 0))
    return pl.pallas_call(
        functools.partial(_gla_kernel, nchunk=nchunk),
        grid=(bsz, nc),
        in_specs=[rowblk(GLA_KEY_DIM, 0), rowblk(GLA_KEY_DIM, 1), rowblk(GLA_VALUE_DIM, 0),
                  rowblk(LANES, 0), full(wup), full(bgk), rowblk(GLA_VALUE_DIM, 0), full(gn)],
        out_specs=rowblk(GLA_VALUE_DIM, 0),
        out_shape=jax.ShapeDtypeStruct((t, GLA_VALUE_DIM), BF16),
        scratch_shapes=[pltpu.VMEM((GLA_HEADS, GLA_DK, GLA_DV), F32)],
        compiler_params=_cparams(("arbitrary", "arbitrary")),
        name="gla",
    )(qkb, qkb, vb, gk, wup, bgk, r_act, gn)


def _merge_kernel(ya_ref, yb_ref, g_ref, x_ref, wpa_ref, wpb_ref, wout_ref, g1_ref,
                  nw_ref, sc_ref, sh_ref, wr_ref, br_ref,
                  x1_ref, h2_ref, idx_ref, wts_ref, rank_ref, cnt_ref, carry_ref):
    @pl.when(pl.program_id(0) == 0)
    def _():
        carry_ref[...] = jnp.zeros(carry_ref.shape, F32)

    pa = lax.dot_general(ya_ref[0], wpa_ref[...], (((0,), (0,)), ((), ())),
                         preferred_element_type=F32)
    pb = jnp.dot(yb_ref[...], wpb_ref[...], preferred_element_type=F32)
    mixed = (g_ref[:, 0:D_MODEL].astype(F32) * pa
             + g_ref[:, D_MODEL:2 * D_MODEL].astype(F32) * pb)
    y = jnp.dot(mixed.astype(BF16), wout_ref[...], preferred_element_type=F32)
    x1 = x_ref[...] + g1_ref[0] * y
    x1_ref[...] = x1
    h2 = _rms(x1) * nw_ref[...]
    h2 = h2 * (1.0 + sc_ref[0]) + sh_ref[0]
    _store_rows(h2_ref, _pack_halves(h2))
    h_hi = h2.astype(BF16)
    h_lo = (h2 - h_hi.astype(F32)).astype(BF16)
    w = wr_ref[...]
    w_hi = w.astype(BF16)
    w_lo = (w - w_hi.astype(F32)).astype(BF16)
    logits = (_nt_dot(w_hi, h_hi) + _nt_dot(w_hi, h_lo) + _nt_dot(w_lo, h_hi)
              + br_ref[...])
    rowf = lax.broadcasted_iota(I32, logits.shape, 0).astype(F32)
    vals, idxs = [], []
    cur = logits
    for _ in range(TOP_K):
        mx = jnp.max(cur, axis=0, keepdims=True)
        first = jnp.min(jnp.where(cur == mx, rowf, float(N_EXPERTS)), axis=0, keepdims=True)
        vals.append(mx)
        idxs.append(first)
        cur = jnp.where(rowf == first, -jnp.inf, cur)
    es = [jnp.exp(v - vals[0]) for v in vals]
    tot = es[0]
    for e in es[1:]:
        tot = tot + e
    idx_ref[...] = jnp.concatenate(idxs, axis=0).astype(I32)
    wts_ref[...] = jnp.concatenate([e / tot for e in es], axis=0)
    tm = logits.shape[1]
    before = (lax.broadcasted_iota(I32, (tm, tm), 0) < lax.broadcasted_iota(I32, (tm, tm), 1))
    upper = jnp.where(before, 1.0, 0.0).astype(BF16)
    carry = carry_ref[:, 0:1]
    ranks = []
    for k in range(TOP_K):
        onehot = idxs[k] == rowf
        onef = jnp.where(onehot, 1.0, 0.0)
        earlier = jnp.dot(onef.astype(BF16), upper, preferred_element_type=F32) + carry
        ranks.append(jnp.sum(jnp.where(onehot, earlier, 0.0), axis=0, keepdims=True))
        carry = carry + jnp.sum(onef, axis=1, keepdims=True)
    rank_ref[...] = jnp.concatenate(ranks, axis=0).astype(I32)
    total = jnp.broadcast_to(carry, carry_ref.shape)
    carry_ref[...] = total
    cnt_ref[...] = total


def _merge(ya, yb, gates, x2d, wpa, wpb, wout, g1, nw, sc, sh, wr_t, br, seq, tm):
    t, d = x2d.shape
    tpb = seq // tm
    row = lambda w: pl.BlockSpec((tm, w), lambda i: (i, 0))
    full = lambda a: pl.BlockSpec(a.shape, lambda i: (0,) * a.ndim)
    per_b = pl.BlockSpec((1, 1, d), lambda i: (i // tpb, 0, 0))
    colblk = pl.BlockSpec((TOP_K, tm), lambda i: (0, i))
    return pl.pallas_call(
        _merge_kernel,
        grid=(t // tm,),
        in_specs=[pl.BlockSpec((1, MOBA_WIDTH, tm), lambda i: (i // tpb, 0, i % tpb)),
                  row(GLA_VALUE_DIM), row(2 * D_MODEL), row(d),
                  full(wpa), full(wpb), full(wout), per_b,
                  pl.BlockSpec((1, d), lambda i: (0, 0)), per_b, per_b,
                  full(wr_t), full(br)],
        out_specs=[row(d), pl.BlockSpec((tm * ROW_SLABS, LANES), lambda i: (i, 0)),
                   colblk, colblk, colblk,
                   pl.BlockSpec((N_EXPERTS, LANES), lambda i: (0, 0))],
        out_shape=[jax.ShapeDtypeStruct((t, d), F32),
                   jax.ShapeDtypeStruct((t * ROW_SLABS, LANES), U32),
                   jax.ShapeDtypeStruct((TOP_K, t), I32), jax.ShapeDtypeStruct((TOP_K, t), F32),
                   jax.ShapeDtypeStruct((TOP_K, t), I32),
                   jax.ShapeDtypeStruct((N_EXPERTS, LANES), F32)],
        scratch_shapes=[pltpu.VMEM((N_EXPERTS, LANES), F32)],
        compiler_params=_cparams(("arbitrary",)),
        name="merge",
    )(ya, yb, gates, x2d, wpa, wpb, wout, g1, nw.reshape(1, d), sc, sh, wr_t, br)


def _dest_kernel(pstart_ref, idx_ref, rank_ref, dest_ref):
    idx = idx_ref[...]
    off = jnp.zeros(idx.shape, I32)
    for e in range(N_EXPERTS):
        off = jnp.where(idx == e, pstart_ref[e], off)
    dest_ref[...] = rank_ref[...] + off


def _dest(pstart, idx, rank, tm):
    t = idx.shape[1]
    blk = pl.BlockSpec((TOP_K, tm), lambda i: (0, i))
    return pl.pallas_call(
        _dest_kernel,
        grid=(t // tm,),
        in_specs=[pl.BlockSpec(memory_space=pltpu.SMEM), blk, blk],
        out_specs=blk,
        out_shape=jax.ShapeDtypeStruct((TOP_K, t), I32),
        compiler_params=_cparams(("arbitrary",)),
        name="dest",
    )(pstart, idx, rank)


def _scatter_kernel(dest_ref, zrow_ref, h_ref, xout_hbm, zbuf, stage, sems, zsem, *, tm):
    @pl.when(pl.program_id(0) == 0)
    def _():
        zbuf[...] = jnp.zeros(zbuf.shape, zbuf.dtype)

        def zero_rows(row0):
            line0 = pl.multiple_of(row0 * ROW_SLABS, MOE_ROWS * ROW_SLABS)
            return pltpu.make_async_copy(
                zbuf, xout_hbm.at[pl.ds(line0, MOE_ROWS * ROW_SLABS)], zsem)

        for e in range(N_EXPERTS):
            zero_rows(zrow_ref[e]).start()
        for e in range(N_EXPERTS):
            zero_rows(zrow_ref[e]).wait()

        def start_tail(j, carry):
            zero_rows(j * MOE_ROWS).start()
            return carry

        def wait_tail(j, carry):
            zero_rows(j * MOE_ROWS).wait()
            return carry

        n_all = xout_hbm.shape[0] // (MOE_ROWS * ROW_SLABS)
        lax.fori_loop(zrow_ref[N_EXPERTS], n_all, start_tail, 0)
        lax.fori_loop(zrow_ref[N_EXPERTS], n_all, wait_tail, 0)

    i = pl.program_id(0)
    slot = i % 2
    src = stage.at[slot]
    src[...] = h_ref[...]

    def issue(g, carry):
        base = pl.multiple_of(g * ROW_UNROLL, ROW_UNROLL)
        for j in range(ROW_UNROLL):
            for k in range(TOP_K):
                d = dest_ref[k, base + j]
                pltpu.make_async_copy(_row(src, base + j), _row(xout_hbm, d),
                                      sems.at[slot]).start(priority=k % 2)
        return carry

    lax.fori_loop(0, tm // ROW_UNROLL, issue, 0)

    def wait_tile(s):
        for k in range(TOP_K):
            pltpu.make_async_copy(stage.at[s], xout_hbm.at[pl.ds(0, tm * ROW_SLABS)],
                                  sems.at[s]).wait()

    @pl.when(i > 0)
    def _():
        wait_tile(1 - slot)

    @pl.when(i == pl.num_programs(0) - 1)
    def _():
        wait_tile(slot)


def _scatter(dest, last_block_row, h2, n_pad, tm):
    t = h2.shape[0] // ROW_SLABS
    return pl.pallas_call(
        functools.partial(_scatter_kernel, tm=tm),
        grid=(t // tm,),
        in_specs=[pl.BlockSpec((TOP_K, tm), lambda i: (0, i), memory_space=pltpu.SMEM),
                  pl.BlockSpec(memory_space=pltpu.SMEM),
                  pl.BlockSpec((tm * ROW_SLABS, LANES), lambda i: (i, 0))],
        out_specs=pl.BlockSpec(memory_space=pl.ANY),
        out_shape=jax.ShapeDtypeStruct((n_pad * ROW_SLABS, LANES), h2.dtype),
        scratch_shapes=[pltpu.VMEM((MOE_ROWS * ROW_SLABS, LANES), h2.dtype),
                        pltpu.VMEM((2, tm * ROW_SLABS, LANES), h2.dtype),
                        pltpu.SemaphoreType.DMA((2,)), pltpu.SemaphoreType.DMA],
        compiler_params=_cparams(("arbitrary",)),
        name="scatter",
    )(dest, last_block_row, h2)


def _expert_kernel(be_ref, nused_ref, nxt_ref, x_ref, wg_hbm, wu_hbm, wd_hbm, b_ref, o_ref,
                   sg_ref, su_ref, sd_ref, wgb_ref, wub_ref, wdb_ref, wsem):
    i = pl.program_id(0)
    e = be_ref[i]
    active = i < nused_ref[0]
    first = jnp.logical_and(active, jnp.logical_or(i == 0, e != be_ref[jnp.maximum(i - 1, 0)]))

    def weight_copies(expert):
        return (pltpu.make_async_copy(wg_hbm.at[expert], sg_ref, wsem),
                pltpu.make_async_copy(wu_hbm.at[expert], su_ref, wsem),
                pltpu.make_async_copy(wd_hbm.at[expert], sd_ref, wsem))

    @pl.when(i == 0)
    def _():
        for cp in weight_copies(e):
            cp.start()

    @pl.when(first)
    def _():
        for cp in weight_copies(e):
            cp.wait()
        wgb_ref[...] = sg_ref[...].astype(BF16)
        wub_ref[...] = su_ref[...].astype(BF16)
        wdb_ref[...] = sd_ref[...].astype(BF16)

    @pl.when(jnp.logical_and(first, nxt_ref[e] != e))
    def _():
        for cp in weight_copies(nxt_ref[e]):
            cp.start()

    @pl.when(active)
    def _():
        xb = jnp.concatenate([v.astype(BF16) for v in _unpack_halves(_load_rows(x_ref))],
                             axis=1)
        g = jnp.minimum(jnp.dot(xb, wgb_ref[...], preferred_element_type=F32) + b_ref[0, 0:1],
                        SWIGLU_LIMIT)
        u = jnp.clip(jnp.dot(xb, wub_ref[...], preferred_element_type=F32) + b_ref[0, 1:2],
                     -SWIGLU_LIMIT, SWIGLU_LIMIT)
        act = g * _sigmoid(SWIGLU_ALPHA * g) * (u + 1.0)
        _store_rows(o_ref, _pack_halves(
            jnp.dot(act.astype(BF16), wdb_ref[...], preferred_element_type=F32)
            + b_ref[0, 2:3]))

    @pl.when(i >= nused_ref[0])
    def _():
        o_ref[...] = jnp.zeros(o_ref.shape, o_ref.dtype)


def _expert(block_expert, n_used, next_expert, x_pad, wg, bg, wu, bu, wd, bd):
    lines = MOE_ROWS * ROW_SLABS
    _, d, f = wg.shape
    assert d == f, "the three bias vectors are passed as one (E, 3, d) array"
    nblk = x_pad.shape[0] // lines
    hbm = pl.BlockSpec(memory_space=pl.ANY)
    grid_spec = pltpu.PrefetchScalarGridSpec(
        num_scalar_prefetch=3,
        grid=(nblk,),
        in_specs=[pl.BlockSpec((lines, LANES),
                               lambda i, be, nu, nx: (jnp.minimum(i, nu[0] - 1), 0)),
                  hbm, hbm, hbm,
                  pl.BlockSpec((1, 3, f), lambda i, be, nu, nx: (be[i], 0, 0))],
        out_specs=pl.BlockSpec((lines, LANES), lambda i, be, nu, nx: (i, 0)),
        scratch_shapes=[pltpu.VMEM((d, f), F32), pltpu.VMEM((d, f), F32), pltpu.VMEM((f, d), F32),
                        pltpu.VMEM((d, f), BF16), pltpu.VMEM((d, f), BF16),
                        pltpu.VMEM((f, d), BF16), pltpu.SemaphoreType.DMA])
    return pl.pallas_call(
        _expert_kernel,
        grid_spec=grid_spec,
        out_shape=jax.ShapeDtypeStruct(x_pad.shape, U32),
        compiler_params=_cparams(("arbitrary",)),
        name="expert",
    )(block_expert, n_used, next_expert, x_pad, wg, wu, wd, jnp.stack([bg, bu, bd], axis=1))


def _combine_kernel(dest_ref, y_hbm, x1_ref, wts_ref, g2_ref, nf_ref, o_ref, ybuf_a, ybuf_b,
                    sems, *, tm, final, n_tiles):
    i = pl.program_id(0)
    bufs = (ybuf_a, ybuf_b)

    def issue(slot):
        for tt in range(tm):
            for k in range(TOP_K):
                d = dest_ref[k, tt]
                pltpu.make_async_copy(_row(y_hbm, d), _row(bufs[slot].at[k], tt),
                                      sems.at[slot]).start(priority=k % 2)

    def wait(slot):
        for k in range(TOP_K):
            pltpu.make_async_copy(y_hbm.at[pl.ds(0, tm * ROW_SLABS)], bufs[slot].at[k],
                                  sems.at[slot]).wait()

    def finish(slot):
        w = wts_ref[...].T
        lo, hi = None, None
        for k in range(TOP_K):
            y_lo, y_hi = _unpack_halves(_load_rows(bufs[slot].at[k]))
            lo = w[:, k:k + 1] * y_lo + (0.0 if lo is None else lo)
            hi = w[:, k:k + 1] * y_hi + (0.0 if hi is None else hi)
        moe = jnp.concatenate([lo, hi], axis=1)
        x2 = x1_ref[...] + g2_ref[0] * moe
        o_ref[...] = _rms(x2) * nf_ref[...] if final else x2

    @pl.when(i == 0)
    def _():
        issue(0)

    for parity in range(2):
        @pl.when(jnp.logical_and(jnp.logical_and(i > 0, i < n_tiles), i % 2 == parity))
        def _(parity=parity):
            wait(1 - parity)
            issue(parity)
            finish(1 - parity)

    @pl.when(i == n_tiles)
    def _():
        wait((n_tiles - 1) % 2)
        finish((n_tiles - 1) % 2)


def _combine(dest, y_pad, x1, wts, g2, nf, seq, tm, final):
    t, d = x1.shape
    tpb = seq // tm
    n_tiles = t // tm
    lag = lambda i: jnp.maximum(i - 1, 0)
    return pl.pallas_call(
        functools.partial(_combine_kernel, tm=tm, final=final, n_tiles=n_tiles),
        grid=(n_tiles + 1,),
        in_specs=[pl.BlockSpec((TOP_K, tm), lambda i: (0, jnp.minimum(i, n_tiles - 1)),
                               memory_space=pltpu.SMEM),
                  pl.BlockSpec(memory_space=pl.ANY),
                  pl.BlockSpec((tm, d), lambda i: (lag(i), 0)),
                  pl.BlockSpec((TOP_K, tm), lambda i: (0, lag(i))),
                  pl.BlockSpec((1, 1, d), lambda i: (lag(i) // tpb, 0, 0)),
                  pl.BlockSpec((1, d), lambda i: (0, 0))],
        out_specs=pl.BlockSpec((tm, d), lambda i: (lag(i), 0)),
        out_shape=jax.ShapeDtypeStruct((t, d), F32),
        scratch_shapes=[pltpu.VMEM((TOP_K, tm * ROW_SLABS, LANES), y_pad.dtype),
                        pltpu.VMEM((TOP_K, tm * ROW_SLABS, LANES), y_pad.dtype),
                        pltpu.SemaphoreType.DMA((2,))],
        compiler_params=_cparams(("arbitrary",)),
        name="combine",
    )(dest, y_pad, x1, wts, g2, nf.reshape(1, d))


def _pick(n, cands):
    for c in cands:
        if n % c == 0:
            return c
    raise ValueError(f"no tile in {cands} divides {n}")


def kernel(x, c, rel_bias, w_ada, b_ada, norm_mix, w_in, w_gk_up, b_gk, gla_norm,
           w_proj_moba, w_proj_gla, w_out, norm_ffn, w_router, b_router,
           w_gate, b_gate, w_up, b_up, w_down, b_down, norm_final):
    bsz, seq, d = x.shape
    depth = w_ada.shape[0]
    assert d == D_MODEL and seq % MOBA_BLOCK == 0 and seq // MOBA_BLOCK <= MOBA_MAX_BLOCKS
    t = bsz * seq
    tm = _pick(seq, (512, 256))
    nchunk = _pick(seq // GLA_CHUNK, (16, 8, 4))
    n_blk = seq // MOBA_BLOCK
    x2d = x.reshape(t, d)
    bias = _bias_tiles(rel_bias)
    per_b = lambda v: v.reshape(bsz, 1, d)

    for l in range(depth):
        mod = _ada(c, w_ada[l], b_ada[l])
        sh1, sc1, g1, sh2, sc2, g2 = [per_b(m) for m in jnp.split(mod, 6, axis=-1)]
        qt, k_aug, vt, qkb, vb, gk, r_act, gates = _inproj(
            x2d, norm_mix[l], sc1, sh1, _regroup_w_in(w_in[l]), bsz, seq, tm)
        ya = _moba(qt, k_aug, vt, bias)
        wup = jnp.pad(w_gk_up[l], ((0, LANES - GLA_GATE_RANK), (0, 0)))
        yb = _gla(qkb, vb, gk, wup, b_gk[l].reshape(1, -1), r_act,
                  gla_norm[l].reshape(1, -1), bsz, seq, nchunk)
        x1, h2, idx, wts, rank, cnt = _merge(
            ya, yb, gates, x2d, w_proj_moba[l].astype(BF16), w_proj_gla[l].astype(BF16),
            w_out[l].astype(BF16), g1, norm_ffn[l], sc2, sh2,
            w_router[l].T, b_router[l].reshape(N_EXPERTS, 1), seq, tm)
        counts = cnt[:, 0].astype(I32)
        padded = (counts + MOE_ROWS - 1) // MOE_ROWS * MOE_ROWS
        pcum = jnp.cumsum(padded)
        pstart = (pcum - padded).astype(I32)
        n_blocks = (t * TOP_K + MOE_ROWS - 1) // MOE_ROWS + N_EXPERTS
        block_row0 = jnp.arange(n_blocks, dtype=I32) * MOE_ROWS
        block_expert = jnp.minimum(
            jnp.sum((pcum[None, :] <= block_row0[:, None]).astype(I32), axis=1),
            N_EXPERTS - 1).astype(I32)
        n_used = (pcum[-1:] // MOE_ROWS).astype(I32)
        e_ids = jnp.arange(N_EXPERTS, dtype=I32)
        later = (e_ids[None, :] > e_ids[:, None]) & (padded[None, :] > 0)
        next_expert = jnp.where(later.any(axis=1), jnp.argmax(later, axis=1), e_ids).astype(I32)
        dest = _dest(pstart, idx, rank, _pick(t, (8192, 4096, 2048, 1024, 512, 256)))
        tg = _pick(seq, (256,))
        zero_info = jnp.concatenate([jnp.maximum(pcum - MOE_ROWS, 0).astype(I32), n_used])
        x_pad = _scatter(dest, zero_info, h2, n_blocks * MOE_ROWS, tg)
        y_pad = _expert(block_expert, n_used, next_expert, x_pad, w_gate[l], b_gate[l],
                        w_up[l], b_up[l], w_down[l], b_down[l])
        x2d = _combine(dest, y_pad, x1, wts, g2, norm_final, seq, tg, l == depth - 1)
    return x2d.reshape(bsz, seq, d)
```

```python
import functools
import math

import numpy as np
import jax
import jax.numpy as jnp
from jax import lax
from jax.experimental import pallas as pl
from jax.experimental.pallas import tpu as pltpu

F32 = jnp.float32
BF16 = jnp.bfloat16
I32 = jnp.int32
HIGHEST = lax.Precision.HIGHEST

D_MODEL = 1024
MOBA_HEADS = 8
MOBA_HEAD_DIM = 64
MOBA_WIDTH = MOBA_HEADS * MOBA_HEAD_DIM
MOBA_BLOCK = 256
MOBA_TOPK = 3
MOBA_MAX_BLOCKS = 32
REL_BUCKETS = 32
REL_MAX_DIST = 128
GLA_HEADS = 4
GLA_KEY_DIM = D_MODEL // 2
GLA_VALUE_DIM = D_MODEL
GLA_DK = GLA_KEY_DIM // GLA_HEADS
GLA_DV = GLA_VALUE_DIM // GLA_HEADS
GLA_GATE_RANK = 16
GLA_GATE_NORMALIZER = 16.0
GLA_CHUNK = 64
N_EXPERTS = 32
TOP_K = 4
SWIGLU_ALPHA = 1.702
SWIGLU_LIMIT = 7.0
MOE_ROWS = 512
ROW_UNROLL = 8
EPS = 1e-6
LANES = 128
SUBLANES = 8
BF16_SUBLANES = 16
ADA_COLS = 768
NEG_BIG = -1e30
LOG2E = math.log2(math.e)
VMEM_LIMIT = 56 * 1024 * 1024


def _cparams(sem, vmem=None):
    return pltpu.CompilerParams(dimension_semantics=sem,
                                vmem_limit_bytes=vmem or VMEM_LIMIT)


def _nt_dot(a, b, **kw):
    return lax.dot_general(a, b, (((1,), (1,)), ((), ())),
                           preferred_element_type=F32, **kw)


def _rms(x):
    return x * lax.rsqrt(jnp.mean(x * x, axis=-1, keepdims=True) + EPS)


def _sigmoid(x):
    return 1.0 / (1.0 + jnp.exp(-x))


U32 = jnp.uint32
_HI16 = 0xFFFF0000


def _pack_halves(x):
    n = x.shape[1] // 2
    lo = pltpu.bitcast(x[:, :n].astype(BF16).astype(F32), U32)
    hi = pltpu.bitcast(x[:, n:].astype(BF16).astype(F32), U32)
    return (hi & U32(_HI16)) | (lo >> 16)


def _unpack_halves(w):
    return (pltpu.bitcast(w << 16, F32), pltpu.bitcast(w & U32(_HI16), F32))


ROW_SLABS = D_MODEL // 2 // LANES


def _store_rows(ref, words):
    m = words.shape[0]
    for c in range(ROW_SLABS):
        ref[pl.ds(c, m, stride=ROW_SLABS), :] = words[:, c * LANES:(c + 1) * LANES]


def _load_rows(ref):
    m = ref.shape[0] // ROW_SLABS
    return jnp.concatenate(
        [ref[pl.ds(c, m, stride=ROW_SLABS), :] for c in range(ROW_SLABS)], axis=1)


def _row(ref, i):
    return ref.at[pl.ds(pl.multiple_of(i * ROW_SLABS, ROW_SLABS), ROW_SLABS)]


def _ada_kernel(c_ref, w_ref, b_ref, o_ref):
    c = c_ref[...]
    s = c * _sigmoid(c)
    o_ref[...] = jnp.dot(s, w_ref[...], precision=HIGHEST,
                         preferred_element_type=F32) + b_ref[...]


def _ada(c, w, b):
    bsz, d = c.shape
    n = w.shape[1]
    rows = -(-bsz // SUBLANES) * SUBLANES
    cp = jnp.zeros((rows, d), F32).at[:bsz].set(c)
    tn = ADA_COLS
    out = pl.pallas_call(
        _ada_kernel,
        grid=(n // tn,),
        in_specs=[pl.BlockSpec((rows, d), lambda j: (0, 0)),
                  pl.BlockSpec((d, tn), lambda j: (0, j)),
                  pl.BlockSpec((1, tn), lambda j: (0, j))],
        out_specs=pl.BlockSpec((rows, tn), lambda j: (0, j)),
        out_shape=jax.ShapeDtypeStruct((rows, n), F32),
        compiler_params=_cparams(("arbitrary",)),
        name="ada",
    )(cp, w, b.reshape(1, n))
    return out[:bsz]


_OFF_QA = 0
_OFF_KA = _OFF_QA + MOBA_WIDTH
_OFF_VA = _OFF_KA + MOBA_WIDTH
_OFF_QKB = _OFF_VA + MOBA_WIDTH
_OFF_VB = _OFF_QKB + 2 * GLA_KEY_DIM
_OFF_GK = _OFF_VB + GLA_VALUE_DIM
_OFF_R = _OFF_GK + LANES
_OFF_G = _OFF_R + GLA_VALUE_DIM
_W_CAT = _OFF_G + 2 * D_MODEL


def _regroup_w_in(w):
    o_gk = 3 * MOBA_WIDTH + 2 * GLA_KEY_DIM + GLA_VALUE_DIM
    w = w.astype(BF16)
    gk = jnp.pad(w[:, o_gk:o_gk + GLA_GATE_RANK], ((0, 0), (0, LANES - GLA_GATE_RANK)))
    return jnp.concatenate([w[:, :o_gk], gk, w[:, o_gk + GLA_GATE_RANK:]], axis=1)


def _inproj_kernel(x_ref, nw_ref, sc_ref, sh_ref, w_ref,
                   qt_ref, ka_ref, vt_ref, qkb_ref, vb_ref, gk_ref, r_ref, g_ref, *, tpb):
    tm = x_ref.shape[0]
    hd = MOBA_HEAD_DIM
    nbt = tm // MOBA_BLOCK
    h = _rms(x_ref[...]) * nw_ref[...]
    h = h * (1.0 + sc_ref[0]) + sh_ref[0]
    hb = h.astype(BF16)

    def mm(a, b):
        return jnp.dot(hb, w_ref[:, a:b], preferred_element_type=F32)

    q_t = (mm(_OFF_QA, _OFF_KA) * (hd ** -0.5 * LOG2E)).T
    v_t = mm(_OFF_VA, _OFF_QKB).T
    k_all = mm(_OFF_KA, _OFF_VA)
    blk0 = (pl.program_id(0) % tpb) * nbt
    lane = lax.broadcasted_iota(I32, (MOBA_BLOCK, LANES), 1)
    ones_rows = jnp.where(
        lax.broadcasted_iota(I32, (MOBA_VT_ROWS - hd, MOBA_BLOCK), 0) == 0, 1.0, 0.0)
    heads_per_tile = LANES // hd
    for hh in range(MOBA_HEADS):
        qt_ref[0, hh] = q_t[hh * hd:(hh + 1) * hd].astype(BF16)
        tile = hh // heads_per_tile
        k_h = k_all[:, tile * LANES:(tile + 1) * LANES]
        if hh % heads_per_tile:
            k_h = pltpu.roll(k_h, LANES - (hh % heads_per_tile) * hd, axis=1)
        for j in range(nbt):
            rows = slice(j * MOBA_BLOCK, (j + 1) * MOBA_BLOCK)
            onehot = jnp.where(lane == hd + blk0 + j, 1.0, 0.0)
            ka_ref[0, hh, j] = jnp.where(lane < hd, k_h[rows], onehot).astype(BF16)
            vt_ref[0, hh, j, 0:hd, :] = v_t[hh * hd:(hh + 1) * hd, rows].astype(BF16)
            vt_ref[0, hh, j, hd:MOBA_VT_ROWS, :] = ones_rows.astype(BF16)
    qkb_ref[...] = mm(_OFF_QKB, _OFF_VB).astype(BF16)
    vb_ref[...] = mm(_OFF_VB, _OFF_GK).astype(BF16)
    gk_ref[...] = mm(_OFF_GK, _OFF_R)
    r = mm(_OFF_R, _OFF_G)
    r_ref[...] = (r * _sigmoid(r)).astype(BF16)
    g_ref[...] = _sigmoid(mm(_OFF_G, _W_CAT)).astype(BF16)


def _inproj(x2d, nw, sc, sh, w_cat, bsz, seq, tm):
    t, d = x2d.shape
    tpb = seq // tm
    nbt = tm // MOBA_BLOCK
    nh, hd = MOBA_HEADS, MOBA_HEAD_DIM
    row = lambda w: pl.BlockSpec((tm, w), lambda i: (i, 0))
    per_b = pl.BlockSpec((1, 1, d), lambda i: (i // tpb, 0, 0))
    rows_out = [(2 * GLA_KEY_DIM, BF16), (GLA_VALUE_DIM, BF16), (LANES, F32),
                (GLA_VALUE_DIM, BF16), (2 * D_MODEL, BF16)]
    return pl.pallas_call(
        functools.partial(_inproj_kernel, tpb=tpb),
        grid=(t // tm,),
        in_specs=[row(d), pl.BlockSpec((1, d), lambda i: (0, 0)), per_b, per_b,
                  pl.BlockSpec((d, _W_CAT), lambda i: (0, 0), pipeline_mode=pl.Buffered(1))],
        out_specs=[pl.BlockSpec((1, nh, hd, tm), lambda i: (i // tpb, 0, 0, i % tpb)),
                   pl.BlockSpec((1, nh, nbt, MOBA_BLOCK, LANES),
                                lambda i: (i // tpb, 0, i % tpb, 0, 0)),
                   pl.BlockSpec((1, nh, nbt, MOBA_VT_ROWS, MOBA_BLOCK),
                                lambda i: (i // tpb, 0, i % tpb, 0, 0))]
                  + [row(w) for w, _ in rows_out],
        out_shape=[jax.ShapeDtypeStruct((bsz, nh, hd, seq), BF16),
                   jax.ShapeDtypeStruct((bsz, nh, seq // MOBA_BLOCK, MOBA_BLOCK, LANES), BF16),
                   jax.ShapeDtypeStruct((bsz, nh, seq // MOBA_BLOCK, MOBA_VT_ROWS, MOBA_BLOCK),
                                        BF16)]
                  + [jax.ShapeDtypeStruct((t, w), dt) for w, dt in rows_out],
        compiler_params=_cparams(("arbitrary",)),
        name="inproj",
    )(x2d, nw.reshape(1, d), sc, sh, w_cat)


def _t5_bucket_np(n):
    n = np.maximum(n, 0)
    max_exact = REL_BUCKETS // 2
    nf = np.maximum(n, max_exact).astype(np.float32)
    large = max_exact + (np.log(nf / max_exact) / math.log(REL_MAX_DIST / max_exact)
                         * (REL_BUCKETS - max_exact)).astype(np.int32)
    large = np.minimum(large, REL_BUCKETS - 1)
    return np.where(n < max_exact, n, large).astype(np.int32)


def _bucket_table():
    kj = np.arange(MOBA_BLOCK)[:, None]
    qi = np.arange(2 * MOBA_BLOCK)[None, :] % MOBA_BLOCK
    prev = np.arange(2 * MOBA_BLOCK)[None, :] < MOBA_BLOCK
    bucket = _t5_bucket_np(qi - kj + np.where(prev, MOBA_BLOCK, 0))
    return np.where(prev | (kj <= qi), bucket, -1).astype(np.int32)


def _bias_kernel(rb_ref, bucket_ref, o_ref):
    h = pl.program_id(0)
    bk = bucket_ref[...]
    far = rb_ref[(REL_BUCKETS - 1) * MOBA_HEADS + h]
    acc = jnp.zeros(bk.shape, F32)
    for b in range(REL_BUCKETS):
        acc = jnp.where(bk == b, rb_ref[b * MOBA_HEADS + h] - far, acc)
    o_ref[0] = jnp.where(bk < 0, NEG_BIG, acc * LOG2E)


def _bias_tiles(rel_bias):
    bucket = jnp.asarray(_bucket_table())
    return pl.pallas_call(
        _bias_kernel,
        grid=(MOBA_HEADS,),
        in_specs=[pl.BlockSpec(memory_space=pltpu.SMEM),
                  pl.BlockSpec(bucket.shape, lambda h: (0, 0))],
        out_specs=pl.BlockSpec((1,) + bucket.shape, lambda h: (h, 0, 0)),
        out_shape=jax.ShapeDtypeStruct((MOBA_HEADS,) + bucket.shape, F32),
        compiler_params=_cparams(("arbitrary",)),
        name="bias",
    )(rel_bias.reshape(-1), bucket)


MOBA_HEADS_PER_STEP = 8
MOBA_FAR_GROUP = 2
MOBA_QBLOCKS_PER_STEP = 4
MOBA_FAR_LAGS = (0, 3, 6)
MOBA_NEAR_LAGS = (0, 4, 8, 12)
MOBA_VT_ROWS = MOBA_HEAD_DIM + BF16_SUBLANES


def _moba_kernel(qt_ref, k_hbm, vt_hbm, bias_ref, o_ref, kmean_ref, qa_ref, k_ref, vt_ref,
                 kv_sems):
    blk = MOBA_BLOCK
    nb = MOBA_MAX_BLOCKS
    hd = MOBA_HEAD_DIM
    hp = MOBA_HEADS_PER_STEP
    grp = MOBA_FAR_GROUP
    nq = MOBA_QBLOCKS_PER_STEP
    pair = pl.program_id(2)
    neg = -jnp.inf
    items = [(h, j) for j in range(nq) for h in range(hp)]

    @pl.when(pair == 0)
    def _():
        kmean_ref[...] = jnp.zeros(kmean_ref.shape, F32)

    def kv_copies(first_blk):
        src = (pl.program_id(0), pl.ds(pl.program_id(1) * hp, hp), pl.ds(first_blk, nq))
        dst = (0, slice(None), pl.ds(first_blk, nq))
        return (pltpu.make_async_copy(k_hbm.at[src], k_ref.at[dst], kv_sems.at[0]),
                pltpu.make_async_copy(vt_hbm.at[src], vt_ref.at[dst], kv_sems.at[1]))

    @pl.when(pair == 0)
    def _():
        for cp in kv_copies(0):
            cp.start()

    for cp in kv_copies(pair * nq):
        cp.wait()

    @pl.when(pair + 1 < pl.num_programs(2))
    def _():
        for cp in kv_copies((pair + 1) * nq):
            cp.start()

    for h in range(hp):
        for j in range(nq):
            kmean_ref[h, pl.ds(pair * nq + j, 1), :] = jnp.mean(
                k_ref[0, h, pair * nq + j].astype(F32), axis=0, keepdims=True)

    row = lax.broadcasted_iota(I32, (nb, blk), 0)
    rowf = row.astype(F32)
    pad = jnp.zeros((LANES - hd, blk), BF16)
    pad_hi = jnp.zeros((LANES - hd - nb, blk), BF16)

    def skewed(stages, lags, todo=None):
        todo = list(range(len(items))) if todo is None else todo
        vals = {}
        for step in range(len(todo) + lags[-1]):
            for stage, lag in zip(stages, lags):
                pos = step - lag
                if 0 <= pos < len(todo):
                    vals[todo[pos]] = stage(todo[pos], vals.get(todo[pos]))
        return vals

    def select(n, _):
        h, j = items[n]
        qi = pair * nq + j
        qt = qt_ref[0, h, :, j * blk:(j + 1) * blk]
        km = kmean_ref[h, :, 0:hd]
        km_hi = km.astype(BF16)
        km_lo = (km - km_hi.astype(F32)).astype(BF16)
        gate = (jnp.dot(km_hi, qt, preferred_element_type=F32)
                + jnp.dot(km_lo, qt, preferred_element_type=F32))
        g = jnp.where(row < qi, gate, neg)
        sel = jnp.zeros((nb, blk), F32)
        for _ in range(MOBA_TOPK):
            mx = jnp.max(g, axis=0, keepdims=True)
            first = jnp.min(jnp.where(g == mx, rowf, float(nb)), axis=0, keepdims=True)
            pick = rowf == jnp.where(mx > neg, first, -1.0)
            sel = jnp.where(pick, 1.0, sel)
            g = jnp.where(pick, neg, g)
        mask_prev = jnp.where(sel > 0.0, jnp.where(row == qi - 1, 0.0, NEG_BIG), NEG_BIG)
        mask_far = jnp.where(sel > 0.0, jnp.where(row < qi - 1, 0.0, NEG_BIG), NEG_BIG)
        qa_ref[n] = jnp.concatenate([qt, mask_far.astype(BF16), pad_hi], axis=0)
        return (jnp.concatenate([qt, pad], axis=0),
                jnp.concatenate([qt, mask_prev.astype(BF16), pad_hi], axis=0))

    def own_prev(n):
        h, j = items[n]
        qi = pair * nq + j
        return h, qi, jnp.maximum(qi - 1, 0)

    def near_scores(n, qa):
        h, qi, prev_j = own_prev(n)
        qa_own, qa_prev = qa
        s_own = jnp.dot(k_ref[0, h, qi], qa_own, preferred_element_type=F32)
        s_prev = jnp.dot(k_ref[0, h, prev_j], qa_prev, preferred_element_type=F32)
        return s_own, s_prev

    def near_softmax(n, ss):
        h = items[n][0]
        s_own, s_prev = ss
        s = jnp.concatenate([s_own + bias_ref[h, :, blk:2 * blk],
                             s_prev + bias_ref[h, :, 0:blk]], axis=0)
        m0 = jnp.max(s, axis=0, keepdims=True)
        return m0, jnp.exp2(s - m0)

    def near_pv(n, mp):
        h, qi, prev_j = own_prev(n)
        m0, p = mp
        pb = p.astype(BF16)
        acc = (jnp.dot(vt_ref[0, h, qi], pb[0:blk], preferred_element_type=F32)
               + jnp.dot(vt_ref[0, h, prev_j], pb[blk:2 * blk], preferred_element_type=F32))
        return m0, acc

    near = skewed([select, near_scores, near_softmax, near_pv], MOBA_NEAR_LAGS)
    states = tuple(near[n] for n in range(len(items)))

    def far(gi, states, todo=None):
        j0 = gi * grp

        def qk(n, _):
            kt = k_ref[0, items[n][0], pl.ds(j0, grp)].reshape(grp * blk, LANES)
            return jnp.dot(kt, qa_ref[n], preferred_element_type=F32)

        def softmax(n, s):
            m_old = states[n][0]
            m_new = jnp.maximum(m_old, jnp.max(s, axis=0, keepdims=True))
            return m_new, jnp.exp2(m_old - m_new), jnp.exp2(s - m_new)

        def pv(n, sm):
            m_new, a, p = sm
            pb = p.astype(BF16)
            tot = a * states[n][1]
            for i in range(grp):
                tot = tot + jnp.dot(vt_ref[0, items[n][0], j0 + i], pb[i * blk:(i + 1) * blk],
                                    preferred_element_type=F32)
            return m_new, tot

        new = skewed([qk, softmax, pv], MOBA_FAR_LAGS, todo)
        return tuple(new.get(n, states[n]) for n in range(len(items)))

    sub = nq // grp
    states = lax.fori_loop(0, pair * sub, far, tuple(states))
    for extra in range(1, sub):
        later = [n for n, (_, j) in enumerate(items) if j // grp >= extra]
        states = far(pair * sub + extra - 1, states, later)

    for n, (h, j) in enumerate(items):
        acc = states[n][1]
        o_ref[0, h * hd:(h + 1) * hd, j * blk:(j + 1) * blk] = (
            acc[0:hd] / acc[hd:hd + 1]).astype(o_ref.dtype)


def _moba(qt, k_aug, vt, bias):
    bsz, nh, hd, s = qt.shape
    blk = MOBA_BLOCK
    hp = MOBA_HEADS_PER_STEP
    nq = MOBA_QBLOCKS_PER_STEP
    nblk = s // blk
    assert nh % hp == 0 and nblk % nq == 0 and nq % MOBA_FAR_GROUP == 0
    return pl.pallas_call(
        _moba_kernel,
        grid=(bsz, nh // hp, nblk // nq),
        in_specs=[pl.BlockSpec((1, hp, hd, nq * blk), lambda b, g, i: (b, g, 0, i)),
                  pl.BlockSpec(memory_space=pl.ANY),
                  pl.BlockSpec(memory_space=pl.ANY),
                  pl.BlockSpec((hp, blk, 2 * blk), lambda b, g, i: (g, 0, 0),
                               pipeline_mode=pl.Buffered(1))],
        out_specs=pl.BlockSpec((1, hp * hd, nq * blk), lambda b, g, i: (b, g, i)),
        out_shape=jax.ShapeDtypeStruct((bsz, nh * hd, s), BF16),
        scratch_shapes=[pltpu.VMEM((hp, MOBA_MAX_BLOCKS, LANES), F32),
                        pltpu.VMEM((hp * nq, LANES, blk), BF16),
                        pltpu.VMEM((1, hp, nblk, blk, LANES), BF16),
                        pltpu.VMEM((1, hp, nblk, MOBA_VT_ROWS, blk), BF16),
                        pltpu.SemaphoreType.DMA((2,))],
        compiler_params=_cparams(("arbitrary", "arbitrary", "arbitrary")),
        name="moba",
    )(qt, k_aug, vt, bias)


def _gla_kernel(q_ref, k_ref, v_ref, gk_ref, wup_ref, bgk_ref, r_ref, gn_ref, o_ref,
                state_ref, *, nchunk):
    ch = GLA_CHUNK
    tc = nchunk * ch
    dk, dv = GLA_DK, GLA_DV

    @pl.when(pl.program_id(1) == 0)
    def _():
        state_ref[...] = jnp.zeros(state_ref.shape, F32)

    rin = lax.broadcasted_iota(I32, (tc, dk), 0) & (ch - 1)
    causal = (lax.broadcasted_iota(I32, (ch, ch), 1) <= lax.broadcasted_iota(I32, (ch, ch), 0))
    eye = (lax.broadcasted_iota(I32, (dk, dk), 0) == lax.broadcasted_iota(I32, (dk, dk), 1))
    chunks = [slice(n * ch, (n + 1) * ch) for n in range(nchunk)]
    gk = gk_ref[...]
    gk_hi = gk.astype(BF16)
    gk_lo = (gk - gk_hi.astype(F32)).astype(BF16)

    def prep(h, _):
        ks = slice(h * dk, (h + 1) * dk)
        w = wup_ref[:, ks]
        w_hi = w.astype(BF16)
        w_lo = (w - w_hi.astype(F32)).astype(BF16)
        z = (jnp.dot(gk_hi, w_hi, preferred_element_type=F32)
             + jnp.dot(gk_lo, w_hi, preferred_element_type=F32)
             + jnp.dot(gk_hi, w_lo, preferred_element_type=F32) + bgk_ref[:, ks])
        log_a = ((jnp.minimum(z, 0.0) - jnp.log(1.0 + jnp.exp(-jnp.abs(z))))
                 / GLA_GATE_NORMALIZER)
        b = log_a
        sh = 1
        while sh < ch:
            b = b + jnp.where(rin >= sh, pltpu.roll(b, sh, axis=0), 0.0)
            sh *= 2
        q = q_ref[:, ks].astype(F32) * (dk ** -0.5)
        k = k_ref[:, ks].astype(F32)
        q_g = (q * jnp.exp(b)).astype(BF16)
        k_g = (k * jnp.exp(-b)).astype(BF16)
        b3 = b.reshape(nchunk, ch, dk)
        b_last = b3[:, ch - 1:ch, :]
        k_end = (k * jnp.exp(jnp.broadcast_to(b_last, b3.shape) - b3).reshape(tc, dk)
                 ).astype(BF16)
        return q_g, k_g, k_end, jnp.exp(b_last)

    def local(h, pre):
        q_g, k_g, k_end, decay = pre
        o_intra, kv, decay_col = [], [], []
        for n, sl in enumerate(chunks):
            v_c = v_ref[sl, h * dv:(h + 1) * dv]
            att = jnp.where(causal, _nt_dot(q_g[sl], k_g[sl]), 0.0)
            o_intra.append(jnp.dot(att.astype(BF16), v_c, preferred_element_type=F32))
            kv.append(lax.dot_general(k_end[sl], v_c, (((0,), (0,)), ((), ())),
                                      preferred_element_type=F32))
            decay_col.append(jnp.sum(
                jnp.where(eye, jnp.broadcast_to(decay[n], (dk, dk)), 0.0),
                axis=1, keepdims=True))
        return q_g, o_intra, kv, decay_col

    def chain(h, loc):
        q_g, o_intra, kv, decay_col = loc
        state = state_ref[h]
        outs = []
        for n, sl in enumerate(chunks):
            outs.append(o_intra[n] + jnp.dot(q_g[sl], state.astype(BF16),
                                             preferred_element_type=F32))
            state = decay_col[n] * state + kv[n]
        state_ref[h] = state
        return jnp.concatenate(outs, axis=0)

    def finish(h, o):
        vs = slice(h * dv, (h + 1) * dv)
        o_ref[:, vs] = (_rms(o) * gn_ref[...] * r_ref[:, vs].astype(F32)).astype(o_ref.dtype)
        return None

    stages = [prep, local, chain, finish]
    vals = [None] * GLA_HEADS
    for step in range(GLA_HEADS + len(stages) - 1):
        for si, stage in enumerate(stages):
            h = step - si
            if 0 <= h < GLA_HEADS:
                vals[h] = stage(h, vals[h])


def _gla(qkb, vb, gk, wup, bgk, r_act, gn, bsz, seq, nchunk):
    t = qkb.shape[0]
    tc = nchunk * GLA_CHUNK
    nc = seq // tc
    rowblk = lambda w, off: pl.BlockSpec((tc, w), lambda b, c: (b * nc + c, off))
    full = lambda a: pl.BlockSpec(a.shape, lambda b, c: (0, 0))
    return pl.pallas_call(
        functools.partial(_gla_kernel, nchunk=nchunk),
        grid=(bsz, nc),
        in_specs=[rowblk(GLA_KEY_DIM, 0), rowblk(GLA_KEY_DIM, 1), rowblk(GLA_VALUE_DIM, 0),
                  rowblk(LANES, 0), full(wup), full(bgk), rowblk(GLA_VALUE_DIM, 0), full(gn)],
        out_specs=rowblk(GLA_VALUE_DIM, 0),
        out_shape=jax.ShapeDtypeStruct((t, GLA_VALUE_DIM), BF16),
        scratch_shapes=[pltpu.VMEM((GLA_HEADS, GLA_DK, GLA_DV), F32)],
        compiler_params=_cparams(("arbitrary", "arbitrary")),
        name="gla",
    )(qkb, qkb, vb, gk, wup, bgk, r_act, gn)


def _merge_kernel(ya_ref, yb_ref, g_ref, x_ref, wpa_ref, wpb_ref, wout_ref, g1_ref,
                  nw_ref, sc_ref, sh_ref, wr_ref, br_ref,
                  x1_ref, h2_ref, idx_ref, wts_ref, rank_ref, cnt_ref, carry_ref):
    @pl.when(pl.program_id(0) == 0)
    def _():
        carry_ref[...] = jnp.zeros(carry_ref.shape, F32)

    pa = lax.dot_general(ya_ref[0], wpa_ref[...], (((0,), (0,)), ((), ())),
                         preferred_element_type=F32)
    pb = jnp.dot(yb_ref[...], wpb_ref[...], preferred_element_type=F32)
    mixed = (g_ref[:, 0:D_MODEL].astype(F32) * pa
             + g_ref[:, D_MODEL:2 * D_MODEL].astype(F32) * pb)
    y = jnp.dot(mixed.astype(BF16), wout_ref[...], preferred_element_type=F32)
    x1 = x_ref[...] + g1_ref[0] * y
    x1_ref[...] = x1
    h2 = _rms(x1) * nw_ref[...]
    h2 = h2 * (1.0 + sc_ref[0]) + sh_ref[0]
    _store_rows(h2_ref, _pack_halves(h2))
    h_hi = h2.astype(BF16)
    h_lo = (h2 - h_hi.astype(F32)).astype(BF16)
    w = wr_ref[...]
    w_hi = w.astype(BF16)
    w_lo = (w - w_hi.astype(F32)).astype(BF16)
    logits = (_nt_dot(w_hi, h_hi) + _nt_dot(w_hi, h_lo) + _nt_dot(w_lo, h_hi)
              + br_ref[...])
    rowf = lax.broadcasted_iota(I32, logits.shape, 0).astype(F32)
    vals, idxs = [], []
    cur = logits
    for _ in range(TOP_K):
        mx = jnp.max(cur, axis=0, keepdims=True)
        first = jnp.min(jnp.where(cur == mx, rowf, float(N_EXPERTS)), axis=0, keepdims=True)
        vals.append(mx)
        idxs.append(first)
        cur = jnp.where(rowf == first, -jnp.inf, cur)
    es = [jnp.exp(v - vals[0]) for v in vals]
    tot = es[0]
    for e in es[1:]:
        tot = tot + e
    idx_ref[...] = jnp.concatenate(idxs, axis=0).astype(I32)
    wts_ref[...] = jnp.concatenate([e / tot for e in es], axis=0)
    tm = logits.shape[1]
    before = (lax.broadcasted_iota(I32, (tm, tm), 0) < lax.broadcasted_iota(I32, (tm, tm), 1))
    upper = jnp.where(before, 1.0, 0.0).astype(BF16)
    carry = carry_ref[:, 0:1]
    ranks = []
    for k in range(TOP_K):
        onehot = idxs[k] == rowf
        onef = jnp.where(onehot, 1.0, 0.0)
        earlier = jnp.dot(onef.astype(BF16), upper, preferred_element_type=F32) + carry
        ranks.append(jnp.sum(jnp.where(onehot, earlier, 0.0), axis=0, keepdims=True))
        carry = carry + jnp.sum(onef, axis=1, keepdims=True)
    rank_ref[...] = jnp.concatenate(ranks, axis=0).astype(I32)
    total = jnp.broadcast_to(carry, carry_ref.shape)
    carry_ref[...] = total
    cnt_ref[...] = total


def _merge(ya, yb, gates, x2d, wpa, wpb, wout, g1, nw, sc, sh, wr_t, br, seq, tm):
    t, d = x2d.shape
    tpb = seq // tm
    row = lambda w: pl.BlockSpec((tm, w), lambda i: (i, 0))
    full = lambda a: pl.BlockSpec(a.shape, lambda i: (0,) * a.ndim)
    per_b = pl.BlockSpec((1, 1, d), lambda i: (i // tpb, 0, 0))
    colblk = pl.BlockSpec((TOP_K, tm), lambda i: (0, i))
    return pl.pallas_call(
        _merge_kernel,
        grid=(t // tm,),
        in_specs=[pl.BlockSpec((1, MOBA_WIDTH, tm), lambda i: (i // tpb, 0, i % tpb)),
                  row(GLA_VALUE_DIM), row(2 * D_MODEL), row(d),
                  full(wpa), full(wpb), full(wout), per_b,
                  pl.BlockSpec((1, d), lambda i: (0, 0)), per_b, per_b,
                  full(wr_t), full(br)],
        out_specs=[row(d), pl.BlockSpec((tm * ROW_SLABS, LANES), lambda i: (i, 0)),
                   colblk, colblk, colblk,
                   pl.BlockSpec((N_EXPERTS, LANES), lambda i: (0, 0))],
        out_shape=[jax.ShapeDtypeStruct((t, d), F32),
                   jax.ShapeDtypeStruct((t * ROW_SLABS, LANES), U32),
                   jax.ShapeDtypeStruct((TOP_K, t), I32), jax.ShapeDtypeStruct((TOP_K, t), F32),
                   jax.ShapeDtypeStruct((TOP_K, t), I32),
                   jax.ShapeDtypeStruct((N_EXPERTS, LANES), F32)],
        scratch_shapes=[pltpu.VMEM((N_EXPERTS, LANES), F32)],
        compiler_params=_cparams(("arbitrary",)),
        name="merge",
    )(ya, yb, gates, x2d, wpa, wpb, wout, g1, nw.reshape(1, d), sc, sh, wr_t, br)


def _dest_kernel(pstart_ref, idx_ref, rank_ref, dest_ref):
    idx = idx_ref[...]
    off = jnp.zeros(idx.shape, I32)
    for e in range(N_EXPERTS):
        off = jnp.where(idx == e, pstart_ref[e], off)
    dest_ref[...] = rank_ref[...] + off


def _dest(pstart, idx, rank, tm):
    t = idx.shape[1]
    blk = pl.BlockSpec((TOP_K, tm), lambda i: (0, i))
    return pl.pallas_call(
        _dest_kernel,
        grid=(t // tm,),
        in_specs=[pl.BlockSpec(memory_space=pltpu.SMEM), blk, blk],
        out_specs=blk,
        out_shape=jax.ShapeDtypeStruct((TOP_K, t), I32),
        compiler_params=_cparams(("arbitrary",)),
        name="dest",
    )(pstart, idx, rank)


def _scatter_kernel(dest_ref, zrow_ref, h_ref, xout_hbm, zbuf, stage, sems, zsem, *, tm):
    @pl.when(pl.program_id(0) == 0)
    def _():
        zbuf[...] = jnp.zeros(zbuf.shape, zbuf.dtype)

        def zero_rows(row0):
            line0 = pl.multiple_of(row0 * ROW_SLABS, MOE_ROWS * ROW_SLABS)
            return pltpu.make_async_copy(
                zbuf, xout_hbm.at[pl.ds(line0, MOE_ROWS * ROW_SLABS)], zsem)

        for e in range(N_EXPERTS):
            zero_rows(zrow_ref[e]).start()
        for e in range(N_EXPERTS):
            zero_rows(zrow_ref[e]).wait()

        def start_tail(j, carry):
            zero_rows(j * MOE_ROWS).start()
            return carry

        def wait_tail(j, carry):
            zero_rows(j * MOE_ROWS).wait()
            return carry

        n_all = xout_hbm.shape[0] // (MOE_ROWS * ROW_SLABS)
        lax.fori_loop(zrow_ref[N_EXPERTS], n_all, start_tail, 0)
        lax.fori_loop(zrow_ref[N_EXPERTS], n_all, wait_tail, 0)

    i = pl.program_id(0)
    slot = i % 2
    src = stage.at[slot]
    src[...] = h_ref[...]

    def issue(g, carry):
        base = pl.multiple_of(g * ROW_UNROLL, ROW_UNROLL)
        for j in range(ROW_UNROLL):
            for k in range(TOP_K):
                d = dest_ref[k, base + j]
                pltpu.make_async_copy(_row(src, base + j), _row(xout_hbm, d),
                                      sems.at[slot]).start(priority=k % 2)
        return carry

    lax.fori_loop(0, tm // ROW_UNROLL, issue, 0)

    def wait_tile(s):
        for k in range(TOP_K):
            pltpu.make_async_copy(stage.at[s], xout_hbm.at[pl.ds(0, tm * ROW_SLABS)],
                                  sems.at[s]).wait()

    @pl.when(i > 0)
    def _():
        wait_tile(1 - slot)

    @pl.when(i == pl.num_programs(0) - 1)
    def _():
        wait_tile(slot)


def _scatter(dest, last_block_row, h2, n_pad, tm):
    t = h2.shape[0] // ROW_SLABS
    return pl.pallas_call(
        functools.partial(_scatter_kernel, tm=tm),
        grid=(t // tm,),
        in_specs=[pl.BlockSpec((TOP_K, tm), lambda i: (0, i), memory_space=pltpu.SMEM),
                  pl.BlockSpec(memory_space=pltpu.SMEM),
                  pl.BlockSpec((tm * ROW_SLABS, LANES), lambda i: (i, 0))],
        out_specs=pl.BlockSpec(memory_space=pl.ANY),
        out_shape=jax.ShapeDtypeStruct((n_pad * ROW_SLABS, LANES), h2.dtype),
        scratch_shapes=[pltpu.VMEM((MOE_ROWS * ROW_SLABS, LANES), h2.dtype),
                        pltpu.VMEM((2, tm * ROW_SLABS, LANES), h2.dtype),
                        pltpu.SemaphoreType.DMA((2,)), pltpu.SemaphoreType.DMA],
        compiler_params=_cparams(("arbitrary",)),
        name="scatter",
    )(dest, last_block_row, h2)


def _expert_kernel(be_ref, nused_ref, nxt_ref, x_ref, wg_hbm, wu_hbm, wd_hbm, b_ref, o_ref,
                   sg_ref, su_ref, sd_ref, wgb_ref, wub_ref, wdb_ref, wsem):
    i = pl.program_id(0)
    e = be_ref[i]
    active = i < nused_ref[0]
    first = jnp.logical_and(active, jnp.logical_or(i == 0, e != be_ref[jnp.maximum(i - 1, 0)]))

    def weight_copies(expert):
        return (pltpu.make_async_copy(wg_hbm.at[expert], sg_ref, wsem),
                pltpu.make_async_copy(wu_hbm.at[expert], su_ref, wsem),
                pltpu.make_async_copy(wd_hbm.at[expert], sd_ref, wsem))

    @pl.when(i == 0)
    def _():
        for cp in weight_copies(e):
            cp.start()

    @pl.when(first)
    def _():
        for cp in weight_copies(e):
            cp.wait()
        wgb_ref[...] = sg_ref[...].astype(BF16)
        wub_ref[...] = su_ref[...].astype(BF16)
        wdb_ref[...] = sd_ref[...].astype(BF16)

    @pl.when(jnp.logical_and(first, nxt_ref[e] != e))
    def _():
        for cp in weight_copies(nxt_ref[e]):
            cp.start()

    @pl.when(active)
    def _():
        xb = jnp.concatenate([v.astype(BF16) for v in _unpack_halves(_load_rows(x_ref))],
                             axis=1)
        g = jnp.minimum(jnp.dot(xb, wgb_ref[...], preferred_element_type=F32) + b_ref[0, 0:1],
                        SWIGLU_LIMIT)
        u = jnp.clip(jnp.dot(xb, wub_ref[...], preferred_element_type=F32) + b_ref[0, 1:2],
                     -SWIGLU_LIMIT, SWIGLU_LIMIT)
        act = g * _sigmoid(SWIGLU_ALPHA * g) * (u + 1.0)
        _store_rows(o_ref, _pack_halves(
            jnp.dot(act.astype(BF16), wdb_ref[...], preferred_element_type=F32)
            + b_ref[0, 2:3]))

    @pl.when(i >= nused_ref[0])
    def _():
        o_ref[...] = jnp.zeros(o_ref.shape, o_ref.dtype)


def _expert(block_expert, n_used, next_expert, x_pad, wg, bg, wu, bu, wd, bd):
    lines = MOE_ROWS * ROW_SLABS
    _, d, f = wg.shape
    assert d == f, "the three bias vectors are passed as one (E, 3, d) array"
    nblk = x_pad.shape[0] // lines
    hbm = pl.BlockSpec(memory_space=pl.ANY)
    grid_spec = pltpu.PrefetchScalarGridSpec(
        num_scalar_prefetch=3,
        grid=(nblk,),
        in_specs=[pl.BlockSpec((lines, LANES),
                               lambda i, be, nu, nx: (jnp.minimum(i, nu[0] - 1), 0)),
                  hbm, hbm, hbm,
                  pl.BlockSpec((1, 3, f), lambda i, be, nu, nx: (be[i], 0, 0))],
        out_specs=pl.BlockSpec((lines, LANES), lambda i, be, nu, nx: (i, 0)),
        scratch_shapes=[pltpu.VMEM((d, f), F32), pltpu.VMEM((d, f), F32), pltpu.VMEM((f, d), F32),
                        pltpu.VMEM((d, f), BF16), pltpu.VMEM((d, f), BF16),
                        pltpu.VMEM((f, d), BF16), pltpu.SemaphoreType.DMA])
    return pl.pallas_call(
        _expert_kernel,
        grid_spec=grid_spec,
        out_shape=jax.ShapeDtypeStruct(x_pad.shape, U32),
        compiler_params=_cparams(("arbitrary",)),
        name="expert",
    )(block_expert, n_used, next_expert, x_pad, wg, wu, wd, jnp.stack([bg, bu, bd], axis=1))


def _combine_kernel(dest_ref, y_hbm, x1_ref, wts_ref, g2_ref, nf_ref, o_ref, ybuf_a, ybuf_b,
                    sems, *, tm, final, n_tiles):
    i = pl.program_id(0)
    bufs = (ybuf_a, ybuf_b)

    def issue(slot):
        for tt in range(tm):
            for k in range(TOP_K):
                d = dest_ref[k, tt]
                pltpu.make_async_copy(_row(y_hbm, d), _row(bufs[slot].at[k], tt),
                                      sems.at[slot]).start(priority=k % 2)

    def wait(slot):
        for k in range(TOP_K):
            pltpu.make_async_copy(y_hbm.at[pl.ds(0, tm * ROW_SLABS)], bufs[slot].at[k],
                                  sems.at[slot]).wait()

    def finish(slot):
        w = wts_ref[...].T
        lo, hi = None, None
        for k in range(TOP_K):
            y_lo, y_hi = _unpack_halves(_load_rows(bufs[slot].at[k]))
            lo = w[:, k:k + 1] * y_lo + (0.0 if lo is None else lo)
            hi = w[:, k:k + 1] * y_hi + (0.0 if hi is None else hi)
        moe = jnp.concatenate([lo, hi], axis=1)
        x2 = x1_ref[...] + g2_ref[0] * moe
        o_ref[...] = _rms(x2) * nf_ref[...] if final else x2

    @pl.when(i == 0)
    def _():
        issue(0)

    for parity in range(2):
        @pl.when(jnp.logical_and(jnp.logical_and(i > 0, i < n_tiles), i % 2 == parity))
        def _(parity=parity):
            wait(1 - parity)
            issue(parity)
            finish(1 - parity)

    @pl.when(i == n_tiles)
    def _():
        wait((n_tiles - 1) % 2)
        finish((n_tiles - 1) % 2)


def _combine(dest, y_pad, x1, wts, g2, nf, seq, tm, final):
    t, d = x1.shape
    tpb = seq // tm
    n_tiles = t // tm
    lag = lambda i: jnp.maximum(i - 1, 0)
    return pl.pallas_call(
        functools.partial(_combine_kernel, tm=tm, final=final, n_tiles=n_tiles),
        grid=(n_tiles + 1,),
        in_specs=[pl.BlockSpec((TOP_K, tm), lambda i: (0, jnp.minimum(i, n_tiles - 1)),
                               memory_space=pltpu.SMEM),
                  pl.BlockSpec(memory_space=pl.ANY),
                  pl.BlockSpec((tm, d), lambda i: (lag(i), 0)),
                  pl.BlockSpec((TOP_K, tm), lambda i: (0, lag(i))),
                  pl.BlockSpec((1, 1, d), lambda i: (lag(i) // tpb, 0, 0)),
                  pl.BlockSpec((1, d), lambda i: (0, 0))],
        out_specs=pl.BlockSpec((tm, d), lambda i: (lag(i), 0)),
        out_shape=jax.ShapeDtypeStruct((t, d), F32),
        scratch_shapes=[pltpu.VMEM((TOP_K, tm * ROW_SLABS, LANES), y_pad.dtype),
                        pltpu.VMEM((TOP_K, tm * ROW_SLABS, LANES), y_pad.dtype),
                        pltpu.SemaphoreType.DMA((2,))],
        compiler_params=_cparams(("arbitrary",)),
        name="combine",
    )(dest, y_pad, x1, wts, g2, nf.reshape(1, d))


def _pick(n, cands):
    for c in cands:
        if n % c == 0:
            return c
    raise ValueError(f"no tile in {cands} divides {n}")


def kernel(x, c, rel_bias, w_ada, b_ada, norm_mix, w_in, w_gk_up, b_gk, gla_norm,
           w_proj_moba, w_proj_gla, w_out, norm_ffn, w_router, b_router,
           w_gate, b_gate, w_up, b_up, w_down, b_down, norm_final):
    bsz, seq, d = x.shape
    depth = w_ada.shape[0]
    assert d == D_MODEL and seq % MOBA_BLOCK == 0 and seq // MOBA_BLOCK <= MOBA_MAX_BLOCKS
    t = bsz * seq
    tm = _pick(seq, (512, 256))
    nchunk = _pick(seq // GLA_CHUNK, (16, 8, 4))
    x2d = x.reshape(t, d)
    bias = _bias_tiles(rel_bias)
    per_b = lambda v: v.reshape(bsz, 1, d)

    for l in range(depth):
        mod = _ada(c, w_ada[l], b_ada[l])
        sh1, sc1, g1, sh2, sc2, g2 = [per_b(m) for m in jnp.split(mod, 6, axis=-1)]
        qt, k_aug, vt, qkb, vb, gk, r_act, gates = _inproj(
            x2d, norm_mix[l], sc1, sh1, _regroup_w_in(w_in[l]), bsz, seq, tm)
        ya = _moba(qt, k_aug, vt, bias)
        wup = jnp.pad(w_gk_up[l], ((0, LANES - GLA_GATE_RANK), (0, 0)))
        yb = _gla(qkb, vb, gk, wup, b_gk[l].reshape(1, -1), r_act,
                  gla_norm[l].reshape(1, -1), bsz, seq, nchunk)
        x1, h2, idx, wts, rank, cnt = _merge(
            ya, yb, gates, x2d, w_proj_moba[l].astype(BF16), w_proj_gla[l].astype(BF16),
            w_out[l].astype(BF16), g1, norm_ffn[l], sc2, sh2,
            w_router[l].T, b_router[l].reshape(N_EXPERTS, 1), seq, tm)
        counts = cnt[:, 0].astype(I32)
        padded = (counts + MOE_ROWS - 1) // MOE_ROWS * MOE_ROWS
        pcum = jnp.cumsum(padded)
        pstart = (pcum - padded).astype(I32)
        n_blocks = (t * TOP_K + MOE_ROWS - 1) // MOE_ROWS + N_EXPERTS
        block_row0 = jnp.arange(n_blocks, dtype=I32) * MOE_ROWS
        block_expert = jnp.minimum(
            jnp.sum((pcum[None, :] <= block_row0[:, None]).astype(I32), axis=1),
            N_EXPERTS - 1).astype(I32)
        n_used = (pcum[-1:] // MOE_ROWS).astype(I32)
        e_ids = jnp.arange(N_EXPERTS, dtype=I32)
        later = (e_ids[None, :] > e_ids[:, None]) & (padded[None, :] > 0)
        next_expert = jnp.where(later.any(axis=1), jnp.argmax(later, axis=1), e_ids).astype(I32)
        dest = _dest(pstart, idx, rank, _pick(t, (8192, 4096, 2048, 1024, 512, 256)))
        tg = _pick(seq, (256,))
        zero_info = jnp.concatenate([jnp.maximum(pcum - MOE_ROWS, 0).astype(I32), n_used])
        x_pad = _scatter(dest, zero_info, h2, n_blocks * MOE_ROWS, tg)
        y_pad = _expert(block_expert, n_used, next_expert, x_pad, w_gate[l], b_gate[l],
                        w_up[l], b_up[l], w_down[l], b_down[l])
        x2d = _combine(dest, y_pad, x1, wts, g2, norm_final, seq, tg, l == depth - 1)
    return x2d.reshape(bsz, seq, d)
```

```python
import functools
import math

import numpy as np
import jax
import jax.numpy as jnp
from jax import lax
from jax.experimental import pallas as pl
from jax.experimental.pallas import tpu as pltpu

F32 = jnp.float32
BF16 = jnp.bfloat16
I32 = jnp.int32
HIGHEST = lax.Precision.HIGHEST

D_MODEL = 1024
MOBA_HEADS = 8
MOBA_HEAD_DIM = 64
MOBA_WIDTH = MOBA_HEADS * MOBA_HEAD_DIM
MOBA_BLOCK = 256
MOBA_TOPK = 3
MOBA_MAX_BLOCKS = 32
REL_BUCKETS = 32
REL_MAX_DIST = 128
GLA_HEADS = 4
GLA_KEY_DIM = D_MODEL // 2
GLA_VALUE_DIM = D_MODEL
GLA_DK = GLA_KEY_DIM // GLA_HEADS
GLA_DV = GLA_VALUE_DIM // GLA_HEADS
GLA_GATE_RANK = 16
GLA_GATE_NORMALIZER = 16.0
GLA_CHUNK = 64
N_EXPERTS = 32
TOP_K = 4
SWIGLU_ALPHA = 1.702
SWIGLU_LIMIT = 7.0
MOE_ROWS = 512
ROW_UNROLL = 8
EPS = 1e-6
LANES = 128
SUBLANES = 8
BF16_SUBLANES = 16
ADA_COLS = 768
NEG_BIG = -1e30
LOG2E = math.log2(math.e)
VMEM_LIMIT = 56 * 1024 * 1024


def _cparams(sem, vmem=None):
    return pltpu.CompilerParams(dimension_semantics=sem,
                                vmem_limit_bytes=vmem or VMEM_LIMIT)


def _nt_dot(a, b, **kw):
    return lax.dot_general(a, b, (((1,), (1,)), ((), ())),
                           preferred_element_type=F32, **kw)


def _rms(x):
    return x * lax.rsqrt(jnp.mean(x * x, axis=-1, keepdims=True) + EPS)


def _sigmoid(x):
    return 1.0 / (1.0 + jnp.exp(-x))


U32 = jnp.uint32
_HI16 = 0xFFFF0000


def _pack_halves(x):
    n = x.shape[1] // 2
    lo = pltpu.bitcast(x[:, :n].astype(BF16).astype(F32), U32)
    hi = pltpu.bitcast(x[:, n:].astype(BF16).astype(F32), U32)
    return (hi & U32(_HI16)) | (lo >> 16)


def _unpack_halves(w):
    return (pltpu.bitcast(w << 16, F32), pltpu.bitcast(w & U32(_HI16), F32))


ROW_SLABS = D_MODEL // 2 // LANES


def _store_rows(ref, words):
    m = words.shape[0]
    for c in range(ROW_SLABS):
        ref[pl.ds(c, m, stride=ROW_SLABS), :] = words[:, c * LANES:(c + 1) * LANES]


def _load_rows(ref):
    m = ref.shape[0] // ROW_SLABS
    return jnp.concatenate(
        [ref[pl.ds(c, m, stride=ROW_SLABS), :] for c in range(ROW_SLABS)], axis=1)


def _row(ref, i):
    return ref.at[pl.ds(pl.multiple_of(i * ROW_SLABS, ROW_SLABS), ROW_SLABS)]


def _ada_kernel(c_ref, w_ref, b_ref, o_ref):
    c = c_ref[...]
    s = c * _sigmoid(c)
    o_ref[...] = jnp.dot(s, w_ref[...], precision=HIGHEST,
                         preferred_element_type=F32) + b_ref[...]


def _ada(c, w, b):
    bsz, d = c.shape
    n = w.shape[1]
    rows = -(-bsz // SUBLANES) * SUBLANES
    cp = jnp.zeros((rows, d), F32).at[:bsz].set(c)
    tn = ADA_COLS
    out = pl.pallas_call(
        _ada_kernel,
        grid=(n // tn,),
        in_specs=[pl.BlockSpec((rows, d), lambda j: (0, 0)),
                  pl.BlockSpec((d, tn), lambda j: (0, j)),
                  pl.BlockSpec((1, tn), lambda j: (0, j))],
        out_specs=pl.BlockSpec((rows, tn), lambda j: (0, j)),
        out_shape=jax.ShapeDtypeStruct((rows, n), F32),
        compiler_params=_cparams(("arbitrary",)),
        name="ada",
    )(cp, w, b.reshape(1, n))
    return out[:bsz]


_OFF_QA = 0
_OFF_KA = _OFF_QA + MOBA_WIDTH
_OFF_VA = _OFF_KA + MOBA_WIDTH
_OFF_QKB = _OFF_VA + MOBA_WIDTH
_OFF_VB = _OFF_QKB + 2 * GLA_KEY_DIM
_OFF_GK = _OFF_VB + GLA_VALUE_DIM
_OFF_R = _OFF_GK + LANES
_OFF_G = _OFF_R + GLA_VALUE_DIM
_W_CAT = _OFF_G + 2 * D_MODEL


def _regroup_w_in(w):
    o_gk = 3 * MOBA_WIDTH + 2 * GLA_KEY_DIM + GLA_VALUE_DIM
    w = w.astype(BF16)
    gk = jnp.pad(w[:, o_gk:o_gk + GLA_GATE_RANK], ((0, 0), (0, LANES - GLA_GATE_RANK)))
    return jnp.concatenate([w[:, :o_gk], gk, w[:, o_gk + GLA_GATE_RANK:]], axis=1)


def _inproj_kernel(x_ref, nw_ref, sc_ref, sh_ref, w_ref,
                   qt_ref, ka_ref, vt_ref, qkb_ref, vb_ref, gk_ref, r_ref, g_ref, *, tpb):
    tm = x_ref.shape[0]
    hd = MOBA_HEAD_DIM
    nbt = tm // MOBA_BLOCK
    h = _rms(x_ref[...]) * nw_ref[...]
    h = h * (1.0 + sc_ref[0]) + sh_ref[0]
    hb = h.astype(BF16)

    def mm(a, b):
        return jnp.dot(hb, w_ref[:, a:b], preferred_element_type=F32)

    q_t = (mm(_OFF_QA, _OFF_KA) * (hd ** -0.5 * LOG2E)).T
    v_t = mm(_OFF_VA, _OFF_QKB).T
    k_all = mm(_OFF_KA, _OFF_VA)
    blk0 = (pl.program_id(0) % tpb) * nbt
    lane = lax.broadcasted_iota(I32, (MOBA_BLOCK, LANES), 1)
    ones_rows = jnp.where(
        lax.broadcasted_iota(I32, (MOBA_VT_ROWS - hd, MOBA_BLOCK), 0) == 0, 1.0, 0.0)
    heads_per_tile = LANES // hd
    for hh in range(MOBA_HEADS):
        qt_ref[0, hh] = q_t[hh * hd:(hh + 1) * hd].astype(BF16)
        tile = hh // heads_per_tile
        k_h = k_all[:, tile * LANES:(tile + 1) * LANES]
        if hh % heads_per_tile:
            k_h = pltpu.roll(k_h, LANES - (hh % heads_per_tile) * hd, axis=1)
        for j in range(nbt):
            rows = slice(j * MOBA_BLOCK, (j + 1) * MOBA_BLOCK)
            onehot = jnp.where(lane == hd + blk0 + j, 1.0, 0.0)
            ka_ref[0, hh, j] = jnp.where(lane < hd, k_h[rows], onehot).astype(BF16)
            vt_ref[0, hh, j, 0:hd, :] = v_t[hh * hd:(hh + 1) * hd, rows].astype(BF16)
            vt_ref[0, hh, j, hd:MOBA_VT_ROWS, :] = ones_rows.astype(BF16)
    qkb_ref[...] = mm(_OFF_QKB, _OFF_VB).astype(BF16)
    vb_ref[...] = mm(_OFF_VB, _OFF_GK).astype(BF16)
    gk_ref[...] = mm(_OFF_GK, _OFF_R)
    r = mm(_OFF_R, _OFF_G)
    r_ref[...] = (r * _sigmoid(r)).astype(BF16)
    g_ref[...] = _sigmoid(mm(_OFF_G, _W_CAT)).astype(BF16)


def _inproj(x2d, nw, sc, sh, w_cat, bsz, seq, tm):
    t, d = x2d.shape
    tpb = seq // tm
    nbt = tm // MOBA_BLOCK
    nh, hd = MOBA_HEADS, MOBA_HEAD_DIM
    row = lambda w: pl.BlockSpec((tm, w), lambda i: (i, 0))
    per_b = pl.BlockSpec((1, 1, d), lambda i: (i // tpb, 0, 0))
    rows_out = [(2 * GLA_KEY_DIM, BF16), (GLA_VALUE_DIM, BF16), (LANES, F32),
                (GLA_VALUE_DIM, BF16), (2 * D_MODEL, BF16)]
    return pl.pallas_call(
        functools.partial(_inproj_kernel, tpb=tpb),
        grid=(t // tm,),
        in_specs=[row(d), pl.BlockSpec((1, d), lambda i: (0, 0)), per_b, per_b,
                  pl.BlockSpec((d, _W_CAT), lambda i: (0, 0), pipeline_mode=pl.Buffered(1))],
        out_specs=[pl.BlockSpec((1, nh, hd, tm), lambda i: (i // tpb, 0, 0, i % tpb)),
                   pl.BlockSpec((1, nh, nbt, MOBA_BLOCK, LANES),
                                lambda i: (i // tpb, 0, i % tpb, 0, 0)),
                   pl.BlockSpec((1, nh, nbt, MOBA_VT_ROWS, MOBA_BLOCK),
                                lambda i: (i // tpb, 0, i % tpb, 0, 0))]
                  + [row(w) for w, _ in rows_out],
        out_shape=[jax.ShapeDtypeStruct((bsz, nh, hd, seq), BF16),
                   jax.ShapeDtypeStruct((bsz, nh, seq // MOBA_BLOCK, MOBA_BLOCK, LANES), BF16),
                   jax.ShapeDtypeStruct((bsz, nh, seq // MOBA_BLOCK, MOBA_VT_ROWS, MOBA_BLOCK),
                                        BF16)]
                  + [jax.ShapeDtypeStruct((t, w), dt) for w, dt in rows_out],
        compiler_params=_cparams(("arbitrary",)),
        name="inproj",
    )(x2d, nw.reshape(1, d), sc, sh, w_cat)


def _t5_bucket_np(n):
    n = np.maximum(n, 0)
    max_exact = REL_BUCKETS // 2
    nf = np.maximum(n, max_exact).astype(np.float32)
    large = max_exact + (np.log(nf / max_exact) / math.log(REL_MAX_DIST / max_exact)
                         * (REL_BUCKETS - max_exact)).astype(np.int32)
    large = np.minimum(large, REL_BUCKETS - 1)
    return np.where(n < max_exact, n, large).astype(np.int32)


def _bucket_table():
    kj = np.arange(MOBA_BLOCK)[:, None]
    qi = np.arange(2 * MOBA_BLOCK)[None, :] % MOBA_BLOCK
    prev = np.arange(2 * MOBA_BLOCK)[None, :] < MOBA_BLOCK
    bucket = _t5_bucket_np(qi - kj + np.where(prev, MOBA_BLOCK, 0))
    return np.where(prev | (kj <= qi), bucket, -1).astype(np.int32)


def _bias_kernel(rb_ref, bucket_ref, o_ref):
    h = pl.program_id(0)
    bk = bucket_ref[...]
    far = rb_ref[(REL_BUCKETS - 1) * MOBA_HEADS + h]
    acc = jnp.zeros(bk.shape, F32)
    for b in range(REL_BUCKETS):
        acc = jnp.where(bk == b, rb_ref[b * MOBA_HEADS + h] - far, acc)
    o_ref[0] = jnp.where(bk < 0, NEG_BIG, acc * LOG2E)


def _bias_tiles(rel_bias):
    bucket = jnp.asarray(_bucket_table())
    return pl.pallas_call(
        _bias_kernel,
        grid=(MOBA_HEADS,),
        in_specs=[pl.BlockSpec(memory_space=pltpu.SMEM),
                  pl.BlockSpec(bucket.shape, lambda h: (0, 0))],
        out_specs=pl.BlockSpec((1,) + bucket.shape, lambda h: (h, 0, 0)),
        out_shape=jax.ShapeDtypeStruct((MOBA_HEADS,) + bucket.shape, F32),
        compiler_params=_cparams(("arbitrary",)),
        name="bias",
    )(rel_bias.reshape(-1), bucket)


MOBA_HEADS_PER_STEP = 8
MOBA_FAR_GROUP = 2
MOBA_QBLOCKS_PER_STEP = 4
MOBA_FAR_LAGS = (0, 3, 6)
MOBA_NEAR_LAGS = (0, 6, 12, 18)
MOBA_VT_ROWS = MOBA_HEAD_DIM + BF16_SUBLANES


def _moba_kernel(qt_ref, k_hbm, vt_hbm, bias_ref, o_ref, kmean_ref, qa_ref, k_ref, vt_ref,
                 kv_sems):
    blk = MOBA_BLOCK
    nb = MOBA_MAX_BLOCKS
    hd = MOBA_HEAD_DIM
    hp = MOBA_HEADS_PER_STEP
    grp = MOBA_FAR_GROUP
    nq = MOBA_QBLOCKS_PER_STEP
    pair = pl.program_id(2)
    neg = -jnp.inf
    items = [(h, j) for j in range(nq) for h in range(hp)]

    @pl.when(pair == 0)
    def _():
        kmean_ref[...] = jnp.zeros(kmean_ref.shape, F32)

    def kv_copies(first_blk):
        src = (pl.program_id(0), pl.ds(pl.program_id(1) * hp, hp), pl.ds(first_blk, nq))
        dst = (0, slice(None), pl.ds(first_blk, nq))
        return (pltpu.make_async_copy(k_hbm.at[src], k_ref.at[dst], kv_sems.at[0]),
                pltpu.make_async_copy(vt_hbm.at[src], vt_ref.at[dst], kv_sems.at[1]))

    @pl.when(pair == 0)
    def _():
        for cp in kv_copies(0):
            cp.start()

    for cp in kv_copies(pair * nq):
        cp.wait()

    @pl.when(pair + 1 < pl.num_programs(2))
    def _():
        for cp in kv_copies((pair + 1) * nq):
            cp.start()

    for h in range(hp):
        for j in range(nq):
            kmean_ref[h, pl.ds(pair * nq + j, 1), :] = jnp.mean(
                k_ref[0, h, pair * nq + j].astype(F32), axis=0, keepdims=True)

    row = lax.broadcasted_iota(I32, (nb, blk), 0)
    rowf = row.astype(F32)
    pad = jnp.zeros((LANES - hd, blk), BF16)
    pad_hi = jnp.zeros((LANES - hd - nb, blk), BF16)

    def skewed(stages, lags, todo=None):
        todo = list(range(len(items))) if todo is None else todo
        vals = {}
        for step in range(len(todo) + lags[-1]):
            for stage, lag in zip(stages, lags):
                pos = step - lag
                if 0 <= pos < len(todo):
                    vals[todo[pos]] = stage(todo[pos], vals.get(todo[pos]))
        return vals

    def select(n, _):
        h, j = items[n]
        qi = pair * nq + j
        qt = qt_ref[0, h, :, j * blk:(j + 1) * blk]
        km = kmean_ref[h, :, 0:hd]
        km_hi = km.astype(BF16)
        km_lo = (km - km_hi.astype(F32)).astype(BF16)
        gate = (jnp.dot(km_hi, qt, preferred_element_type=F32)
                + jnp.dot(km_lo, qt, preferred_element_type=F32))
        g = jnp.where(row < qi, gate, neg)
        sel = jnp.zeros((nb, blk), F32)
        for _ in range(MOBA_TOPK):
            mx = jnp.max(g, axis=0, keepdims=True)
            first = jnp.min(jnp.where(g == mx, rowf, float(nb)), axis=0, keepdims=True)
            pick = rowf == jnp.where(mx > neg, first, -1.0)
            sel = jnp.where(pick, 1.0, sel)
            g = jnp.where(pick, neg, g)
        mask_prev = jnp.where(sel > 0.0, jnp.where(row == qi - 1, 0.0, NEG_BIG), NEG_BIG)
        mask_far = jnp.where(sel > 0.0, jnp.where(row < qi - 1, 0.0, NEG_BIG), NEG_BIG)
        qa_ref[n] = jnp.concatenate([qt, mask_far.astype(BF16), pad_hi], axis=0)
        return (jnp.concatenate([qt, pad], axis=0),
                jnp.concatenate([qt, mask_prev.astype(BF16), pad_hi], axis=0))

    def own_prev(n):
        h, j = items[n]
        qi = pair * nq + j
        return h, qi, jnp.maximum(qi - 1, 0)

    def near_scores(n, qa):
        h, qi, prev_j = own_prev(n)
        qa_own, qa_prev = qa
        s_own = jnp.dot(k_ref[0, h, qi], qa_own, preferred_element_type=F32)
        s_prev = jnp.dot(k_ref[0, h, prev_j], qa_prev, preferred_element_type=F32)
        return s_own, s_prev

    def near_softmax(n, ss):
        h = items[n][0]
        s_own, s_prev = ss
        s = jnp.concatenate([s_own + bias_ref[h, :, blk:2 * blk],
                             s_prev + bias_ref[h, :, 0:blk]], axis=0)
        m0 = jnp.max(s, axis=0, keepdims=True)
        return m0, jnp.exp2(s - m0)

    def near_pv(n, mp):
        h, qi, prev_j = own_prev(n)
        m0, p = mp
        pb = p.astype(BF16)
        acc = (jnp.dot(vt_ref[0, h, qi], pb[0:blk], preferred_element_type=F32)
               + jnp.dot(vt_ref[0, h, prev_j], pb[blk:2 * blk], preferred_element_type=F32))
        return m0, acc

    near = skewed([select, near_scores, near_softmax, near_pv], MOBA_NEAR_LAGS)
    states = tuple(near[n] for n in range(len(items)))

    def far(gi, states, todo=None):
        j0 = gi * grp

        def qk(n, _):
            kt = k_ref[0, items[n][0], pl.ds(j0, grp)].reshape(grp * blk, LANES)
            return jnp.dot(kt, qa_ref[n], preferred_element_type=F32)

        def softmax(n, s):
            m_old = states[n][0]
            m_new = jnp.maximum(m_old, jnp.max(s, axis=0, keepdims=True))
            return m_new, jnp.exp2(m_old - m_new), jnp.exp2(s - m_new)

        def pv(n, sm):
            m_new, a, p = sm
            pb = p.astype(BF16)
            tot = a * states[n][1]
            for i in range(grp):
                tot = tot + jnp.dot(vt_ref[0, items[n][0], j0 + i], pb[i * blk:(i + 1) * blk],
                                    preferred_element_type=F32)
            return m_new, tot

        new = skewed([qk, softmax, pv], MOBA_FAR_LAGS, todo)
        return tuple(new.get(n, states[n]) for n in range(len(items)))

    sub = nq // grp
    states = lax.fori_loop(0, pair * sub, far, tuple(states))
    for extra in range(1, sub):
        later = [n for n, (_, j) in enumerate(items) if j // grp >= extra]
        states = far(pair * sub + extra - 1, states, later)

    for n, (h, j) in enumerate(items):
        acc = states[n][1]
        o_ref[0, h * hd:(h + 1) * hd, j * blk:(j + 1) * blk] = (
            acc[0:hd] / acc[hd:hd + 1]).astype(o_ref.dtype)


def _moba(qt, k_aug, vt, bias):
    bsz, nh, hd, s = qt.shape
    blk = MOBA_BLOCK
    hp = MOBA_HEADS_PER_STEP
    nq = MOBA_QBLOCKS_PER_STEP
    nblk = s // blk
    assert nh % hp == 0 and nblk % nq == 0 and nq % MOBA_FAR_GROUP == 0
    return pl.pallas_call(
        _moba_kernel,
        grid=(bsz, nh // hp, nblk // nq),
        in_specs=[pl.BlockSpec((1, hp, hd, nq * blk), lambda b, g, i: (b, g, 0, i)),
                  pl.BlockSpec(memory_space=pl.ANY),
                  pl.BlockSpec(memory_space=pl.ANY),
                  pl.BlockSpec((hp, blk, 2 * blk), lambda b, g, i: (g, 0, 0),
                               pipeline_mode=pl.Buffered(1))],
        out_specs=pl.BlockSpec((1, hp * hd, nq * blk), lambda b, g, i: (b, g, i)),
        out_shape=jax.ShapeDtypeStruct((bsz, nh * hd, s), BF16),
        scratch_shapes=[pltpu.VMEM((hp, MOBA_MAX_BLOCKS, LANES), F32),
                        pltpu.VMEM((hp * nq, LANES, blk), BF16),
                        pltpu.VMEM((1, hp, nblk, blk, LANES), BF16),
                        pltpu.VMEM((1, hp, nblk, MOBA_VT_ROWS, blk), BF16),
                        pltpu.SemaphoreType.DMA((2,))],
        compiler_params=_cparams(("arbitrary", "arbitrary", "arbitrary")),
        name="moba",
    )(qt, k_aug, vt, bias)


def _gla_kernel(q_ref, k_ref, v_ref, gk_ref, wup_ref, bgk_ref, r_ref, gn_ref, o_ref,
                state_ref, *, nchunk):
    ch = GLA_CHUNK
    tc = nchunk * ch
    dk, dv = GLA_DK, GLA_DV

    @pl.when(pl.program_id(1) == 0)
    def _():
        state_ref[...] = jnp.zeros(state_ref.shape, F32)

    rin = lax.broadcasted_iota(I32, (tc, dk), 0) & (ch - 1)
    causal = (lax.broadcasted_iota(I32, (ch, ch), 1) <= lax.broadcasted_iota(I32, (ch, ch), 0))
    eye = (lax.broadcasted_iota(I32, (dk, dk), 0) == lax.broadcasted_iota(I32, (dk, dk), 1))
    chunks = [slice(n * ch, (n + 1) * ch) for n in range(nchunk)]
    gk = gk_ref[...]
    gk_hi = gk.astype(BF16)
    gk_lo = (gk - gk_hi.astype(F32)).astype(BF16)

    def prep(h, _):
        ks = slice(h * dk, (h + 1) * dk)
        w = wup_ref[:, ks]
        w_hi = w.astype(BF16)
        w_lo = (w - w_hi.astype(F32)).astype(BF16)
        z = (jnp.dot(gk_hi, w_hi, preferred_element_type=F32)
             + jnp.dot(gk_lo, w_hi, preferred_element_type=F32)
             + jnp.dot(gk_hi, w_lo, preferred_element_type=F32) + bgk_ref[:, ks])
        log_a = ((jnp.minimum(z, 0.0) - jnp.log(1.0 + jnp.exp(-jnp.abs(z))))
                 / GLA_GATE_NORMALIZER)
        b = log_a
        sh = 1
        while sh < ch:
            b = b + jnp.where(rin >= sh, pltpu.roll(b, sh, axis=0), 0.0)
            sh *= 2
        q = q_ref[:, ks].astype(F32) * (dk ** -0.5)
        k = k_ref[:, ks].astype(F32)
        q_g = (q * jnp.exp(b)).astype(BF16)
        k_g = (k * jnp.exp(-b)).astype(BF16)
        b3 = b.reshape(nchunk, ch, dk)
        b_last = b3[:, ch - 1:ch, :]
        k_end = (k * jnp.exp(jnp.broadcast_to(b_last, b3.shape) - b3).reshape(tc, dk)
                 ).astype(BF16)
        return q_g, k_g, k_end, jnp.exp(b_last)

    def local(h, pre):
        q_g, k_g, k_end, decay = pre
        o_intra, kv, decay_col = [], [], []
        for n, sl in enumerate(chunks):
            v_c = v_ref[sl, h * dv:(h + 1) * dv]
            att = jnp.where(causal, _nt_dot(q_g[sl], k_g[sl]), 0.0)
            o_intra.append(jnp.dot(att.astype(BF16), v_c, preferred_element_type=F32))
            kv.append(lax.dot_general(k_end[sl], v_c, (((0,), (0,)), ((), ())),
                                      preferred_element_type=F32))
            decay_col.append(jnp.sum(
                jnp.where(eye, jnp.broadcast_to(decay[n], (dk, dk)), 0.0),
                axis=1, keepdims=True))
        return q_g, o_intra, kv, decay_col

    def chain(h, loc):
        q_g, o_intra, kv, decay_col = loc
        state = state_ref[h]
        outs = []
        for n, sl in enumerate(chunks):
            outs.append(o_intra[n] + jnp.dot(q_g[sl], state.astype(BF16),
                                             preferred_element_type=F32))
            state = decay_col[n] * state + kv[n]
        state_ref[h] = state
        return jnp.concatenate(outs, axis=0)

    def finish(h, o):
        vs = slice(h * dv, (h + 1) * dv)
        o_ref[:, vs] = (_rms(o) * gn_ref[...] * r_ref[:, vs].astype(F32)).astype(o_ref.dtype)
        return None

    stages = [prep, local, chain, finish]
    vals = [None] * GLA_HEADS
    for step in range(GLA_HEADS + len(stages) - 1):
        for si, stage in enumerate(stages):
            h = step - si
            if 0 <= h < GLA_HEADS:
                vals[h] = stage(h, vals[h])


def _gla(qkb, vb, gk, wup, bgk, r_act, gn, bsz, seq, nchunk):
    t = qkb.shape[0]
    tc = nchunk * GLA_CHUNK
    nc = seq // tc
    rowblk = lambda w, off: pl.BlockSpec((tc, w), lambda b, c: (b * nc + c, off))
    full = lambda a: pl.BlockSpec(a.shape, lambda b, c: (0, 0))
    return pl.pallas_call(
        functools.partial(_gla_kernel, nchunk=nchunk),
        grid=(bsz, nc),
        in_specs=[rowblk(GLA_KEY_DIM, 0), rowblk(GLA_KEY_DIM, 1), rowblk(GLA_VALUE_DIM, 0),
                  rowblk(LANES, 0), full(wup), full(bgk), rowblk(GLA_VALUE_DIM, 0), full(gn)],
        out_specs=rowblk(GLA_VALUE_DIM, 0),
        out_shape=jax.ShapeDtypeStruct((t, GLA_VALUE_DIM), BF16),
        scratch_shapes=[pltpu.VMEM((GLA_HEADS, GLA_DK, GLA_DV), F32)],
        compiler_params=_cparams(("arbitrary", "arbitrary")),
        name="gla",
    )(qkb, qkb, vb, gk, wup, bgk, r_act, gn)


def _merge_kernel(ya_ref, yb_ref, g_ref, x_ref, wpa_ref, wpb_ref, wout_ref, g1_ref,
                  nw_ref, sc_ref, sh_ref, wr_ref, br_ref,
                  x1_ref, h2_ref, idx_ref, wts_ref, rank_ref, cnt_ref, carry_ref):
    @pl.when(pl.program_id(0) == 0)
    def _():
        carry_ref[...] = jnp.zeros(carry_ref.shape, F32)

    pa = lax.dot_general(ya_ref[0], wpa_ref[...], (((0,), (0,)), ((), ())),
                         preferred_element_type=F32)
    pb = jnp.dot(yb_ref[...], wpb_ref[...], preferred_element_type=F32)
    mixed = (g_ref[:, 0:D_MODEL].astype(F32) * pa
             + g_ref[:, D_MODEL:2 * D_MODEL].astype(F32) * pb)
    y = jnp.dot(mixed.astype(BF16), wout_ref[...], preferred_element_type=F32)
    x1 = x_ref[...] + g1_ref[0] * y
    x1_ref[...] = x1
    h2 = _rms(x1) * nw_ref[...]
    h2 = h2 * (1.0 + sc_ref[0]) + sh_ref[0]
    _store_rows(h2_ref, _pack_halves(h2))
    h_hi = h2.astype(BF16)
    h_lo = (h2 - h_hi.astype(F32)).astype(BF16)
    w = wr_ref[...]
    w_hi = w.astype(BF16)
    w_lo = (w - w_hi.astype(F32)).astype(BF16)
    logits = (_nt_dot(w_hi, h_hi) + _nt_dot(w_hi, h_lo) + _nt_dot(w_lo, h_hi)
              + br_ref[...])
    rowf = lax.broadcasted_iota(I32, logits.shape, 0).astype(F32)
    vals, idxs = [], []
    cur = logits
    for _ in range(TOP_K):
        mx = jnp.max(cur, axis=0, keepdims=True)
        first = jnp.min(jnp.where(cur == mx, rowf, float(N_EXPERTS)), axis=0, keepdims=True)
        vals.append(mx)
        idxs.append(first)
        cur = jnp.where(rowf == first, -jnp.inf, cur)
    es = [jnp.exp(v - vals[0]) for v in vals]
    tot = es[0]
    for e in es[1:]:
        tot = tot + e
    idx_ref[...] = jnp.concatenate(idxs, axis=0).astype(I32)
    wts_ref[...] = jnp.concatenate([e / tot for e in es], axis=0)
    tm = logits.shape[1]
    before = (lax.broadcasted_iota(I32, (tm, tm), 0) < lax.broadcasted_iota(I32, (tm, tm), 1))
    upper = jnp.where(before, 1.0, 0.0).astype(BF16)
    carry = carry_ref[:, 0:1]
    ranks = []
    for k in range(TOP_K):
        onehot = idxs[k] == rowf
        onef = jnp.where(onehot, 1.0, 0.0)
        earlier = jnp.dot(onef.astype(BF16), upper, preferred_element_type=F32) + carry
        ranks.append(jnp.sum(jnp.where(onehot, earlier, 0.0), axis=0, keepdims=True))
        carry = carry + jnp.sum(onef, axis=1, keepdims=True)
    rank_ref[...] = jnp.concatenate(ranks, axis=0).astype(I32)
    total = jnp.broadcast_to(carry, carry_ref.shape)
    carry_ref[...] = total
    cnt_ref[...] = total


def _merge(ya, yb, gates, x2d, wpa, wpb, wout, g1, nw, sc, sh, wr_t, br, seq, tm):
    t, d = x2d.shape
    tpb = seq // tm
    row = lambda w: pl.BlockSpec((tm, w), lambda i: (i, 0))
    full = lambda a: pl.BlockSpec(a.shape, lambda i: (0,) * a.ndim)
    per_b = pl.BlockSpec((1, 1, d), lambda i: (i // tpb, 0, 0))
    colblk = pl.BlockSpec((TOP_K, tm), lambda i: (0, i))
    return pl.pallas_call(
        _merge_kernel,
        grid=(t // tm,),
        in_specs=[pl.BlockSpec((1, MOBA_WIDTH, tm), lambda i: (i // tpb, 0, i % tpb)),
                  row(GLA_VALUE_DIM), row(2 * D_MODEL), row(d),
                  full(wpa), full(wpb), full(wout), per_b,
                  pl.BlockSpec((1, d), lambda i: (0, 0)), per_b, per_b,
                  full(wr_t), full(br)],
        out_specs=[row(d), pl.BlockSpec((tm * ROW_SLABS, LANES), lambda i: (i, 0)),
                   colblk, colblk, colblk,
                   pl.BlockSpec((N_EXPERTS, LANES), lambda i: (0, 0))],
        out_shape=[jax.ShapeDtypeStruct((t, d), F32),
                   jax.ShapeDtypeStruct((t * ROW_SLABS, LANES), U32),
                   jax.ShapeDtypeStruct((TOP_K, t), I32), jax.ShapeDtypeStruct((TOP_K, t), F32),
                   jax.ShapeDtypeStruct((TOP_K, t), I32),
                   jax.ShapeDtypeStruct((N_EXPERTS, LANES), F32)],
        scratch_shapes=[pltpu.VMEM((N_EXPERTS, LANES), F32)],
        compiler_params=_cparams(("arbitrary",)),
        name="merge",
    )(ya, yb, gates, x2d, wpa, wpb, wout, g1, nw.reshape(1, d), sc, sh, wr_t, br)


def _dest_kernel(pstart_ref, idx_ref, rank_ref, dest_ref):
    idx = idx_ref[...]
    off = jnp.zeros(idx.shape, I32)
    for e in range(N_EXPERTS):
        off = jnp.where(idx == e, pstart_ref[e], off)
    dest_ref[...] = rank_ref[...] + off


def _dest(pstart, idx, rank, tm):
    t = idx.shape[1]
    blk = pl.BlockSpec((TOP_K, tm), lambda i: (0, i))
    return pl.pallas_call(
        _dest_kernel,
        grid=(t // tm,),
        in_specs=[pl.BlockSpec(memory_space=pltpu.SMEM), blk, blk],
        out_specs=blk,
        out_shape=jax.ShapeDtypeStruct((TOP_K, t), I32),
        compiler_params=_cparams(("arbitrary",)),
        name="dest",
    )(pstart, idx, rank)


def _scatter_kernel(dest_ref, zrow_ref, h_ref, xout_hbm, zbuf, stage, sems, zsem, *, tm):
    @pl.when(pl.program_id(0) == 0)
    def _():
        zbuf[...] = jnp.zeros(zbuf.shape, zbuf.dtype)

        def zero_rows(row0):
            line0 = pl.multiple_of(row0 * ROW_SLABS, MOE_ROWS * ROW_SLABS)
            return pltpu.make_async_copy(
                zbuf, xout_hbm.at[pl.ds(line0, MOE_ROWS * ROW_SLABS)], zsem)

        for e in range(N_EXPERTS):
            zero_rows(zrow_ref[e]).start()
        for e in range(N_EXPERTS):
            zero_rows(zrow_ref[e]).wait()

        def start_tail(j, carry):
            zero_rows(j * MOE_ROWS).start()
            return carry

        def wait_tail(j, carry):
            zero_rows(j * MOE_ROWS).wait()
            return carry

        n_all = xout_hbm.shape[0] // (MOE_ROWS * ROW_SLABS)
        lax.fori_loop(zrow_ref[N_EXPERTS], n_all, start_tail, 0)
        lax.fori_loop(zrow_ref[N_EXPERTS], n_all, wait_tail, 0)

    i = pl.program_id(0)
    slot = i % 2
    src = stage.at[slot]
    src[...] = h_ref[...]

    def issue(g, carry):
        base = pl.multiple_of(g * ROW_UNROLL, ROW_UNROLL)
        for j in range(ROW_UNROLL):
            for k in range(TOP_K):
                d = dest_ref[k, base + j]
                pltpu.make_async_copy(_row(src, base + j), _row(xout_hbm, d),
                                      sems.at[slot]).start(priority=k % 2)
        return carry

    lax.fori_loop(0, tm // ROW_UNROLL, issue, 0)

    def wait_tile(s):
        for k in range(TOP_K):
            pltpu.make_async_copy(stage.at[s], xout_hbm.at[pl.ds(0, tm * ROW_SLABS)],
                                  sems.at[s]).wait()

    @pl.when(i > 0)
    def _():
        wait_tile(1 - slot)

    @pl.when(i == pl.num_programs(0) - 1)
    def _():
        wait_tile(slot)


def _scatter(dest, last_block_row, h2, n_pad, tm):
    t = h2.shape[0] // ROW_SLABS
    return pl.pallas_call(
        functools.partial(_scatter_kernel, tm=tm),
        grid=(t // tm,),
        in_specs=[pl.BlockSpec((TOP_K, tm), lambda i: (0, i), memory_space=pltpu.SMEM),
                  pl.BlockSpec(memory_space=pltpu.SMEM),
                  pl.BlockSpec((tm * ROW_SLABS, LANES), lambda i: (i, 0))],
        out_specs=pl.BlockSpec(memory_space=pl.ANY),
        out_shape=jax.ShapeDtypeStruct((n_pad * ROW_SLABS, LANES), h2.dtype),
        scratch_shapes=[pltpu.VMEM((MOE_ROWS * ROW_SLABS, LANES), h2.dtype),
                        pltpu.VMEM((2, tm * ROW_SLABS, LANES), h2.dtype),
                        pltpu.SemaphoreType.DMA((2,)), pltpu.SemaphoreType.DMA],
        compiler_params=_cparams(("arbitrary",)),
        name="scatter",
    )(dest, last_block_row, h2)


def _expert_kernel(be_ref, nused_ref, nxt_ref, x_ref, wg_hbm, wu_hbm, wd_hbm, b_ref, o_ref,
                   sg_ref, su_ref, sd_ref, wgb_ref, wub_ref, wdb_ref, wsem):
    i = pl.program_id(0)
    e = be_ref[i]
    active = i < nused_ref[0]
    first = jnp.logical_and(active, jnp.logical_or(i == 0, e != be_ref[jnp.maximum(i - 1, 0)]))

    def weight_copies(expert):
        return (pltpu.make_async_copy(wg_hbm.at[expert], sg_ref, wsem),
                pltpu.make_async_copy(wu_hbm.at[expert], su_ref, wsem),
                pltpu.make_async_copy(wd_hbm.at[expert], sd_ref, wsem))

    @pl.when(i == 0)
    def _():
        for cp in weight_copies(e):
            cp.start()

    @pl.when(first)
    def _():
        for cp in weight_copies(e):
            cp.wait()
        wgb_ref[...] = sg_ref[...].astype(BF16)
        wub_ref[...] = su_ref[...].astype(BF16)
        wdb_ref[...] = sd_ref[...].astype(BF16)

    @pl.when(jnp.logical_and(first, nxt_ref[e] != e))
    def _():
        for cp in weight_copies(nxt_ref[e]):
            cp.start()

    @pl.when(active)
    def _():
        xb = jnp.concatenate([v.astype(BF16) for v in _unpack_halves(_load_rows(x_ref))],
                             axis=1)
        g = jnp.minimum(jnp.dot(xb, wgb_ref[...], preferred_element_type=F32) + b_ref[0, 0:1],
                        SWIGLU_LIMIT)
        u = jnp.clip(jnp.dot(xb, wub_ref[...], preferred_element_type=F32) + b_ref[0, 1:2],
                     -SWIGLU_LIMIT, SWIGLU_LIMIT)
        act = g * _sigmoid(SWIGLU_ALPHA * g) * (u + 1.0)
        _store_rows(o_ref, _pack_halves(
            jnp.dot(act.astype(BF16), wdb_ref[...], preferred_element_type=F32)
            + b_ref[0, 2:3]))

    @pl.when(i >= nused_ref[0])
    def _():
        o_ref[...] = jnp.zeros(o_ref.shape, o_ref.dtype)


def _expert(block_expert, n_used, next_expert, x_pad, wg, bg, wu, bu, wd, bd):
    lines = MOE_ROWS * ROW_SLABS
    _, d, f = wg.shape
    assert d == f, "the three bias vectors are passed as one (E, 3, d) array"
    nblk = x_pad.shape[0] // lines
    hbm = pl.BlockSpec(memory_space=pl.ANY)
    grid_spec = pltpu.PrefetchScalarGridSpec(
        num_scalar_prefetch=3,
        grid=(nblk,),
        in_specs=[pl.BlockSpec((lines, LANES),
                               lambda i, be, nu, nx: (jnp.minimum(i, nu[0] - 1), 0)),
                  hbm, hbm, hbm,
                  pl.BlockSpec((1, 3, f), lambda i, be, nu, nx: (be[i], 0, 0))],
        out_specs=pl.BlockSpec((lines, LANES), lambda i, be, nu, nx: (i, 0)),
        scratch_shapes=[pltpu.VMEM((d, f), F32), pltpu.VMEM((d, f), F32), pltpu.VMEM((f, d), F32),
                        pltpu.VMEM((d, f), BF16), pltpu.VMEM((d, f), BF16),
                        pltpu.VMEM((f, d), BF16), pltpu.SemaphoreType.DMA])
    return pl.pallas_call(
        _expert_kernel,
        grid_spec=grid_spec,
        out_shape=jax.ShapeDtypeStruct(x_pad.shape, U32),
        compiler_params=_cparams(("arbitrary",)),
        name="expert",
    )(block_expert, n_used, next_expert, x_pad, wg, wu, wd, jnp.stack([bg, bu, bd], axis=1))


def _combine_kernel(dest_ref, y_hbm, x1_ref, wts_ref, g2_ref, nf_ref, o_ref, ybuf_a, ybuf_b,
                    sems, *, tm, final, n_tiles):
    i = pl.program_id(0)
    bufs = (ybuf_a, ybuf_b)

    def issue(slot):
        for tt in range(tm):
            for k in range(TOP_K):
                d = dest_ref[k, tt]
                pltpu.make_async_copy(_row(y_hbm, d), _row(bufs[slot].at[k], tt),
                                      sems.at[slot]).start(priority=k % 2)

    def wait(slot):
        for k in range(TOP_K):
            pltpu.make_async_copy(y_hbm.at[pl.ds(0, tm * ROW_SLABS)], bufs[slot].at[k],
                                  sems.at[slot]).wait()

    def finish(slot):
        w = wts_ref[...].T
        lo, hi = None, None
        for k in range(TOP_K):
            y_lo, y_hi = _unpack_halves(_load_rows(bufs[slot].at[k]))
            lo = w[:, k:k + 1] * y_lo + (0.0 if lo is None else lo)
            hi = w[:, k:k + 1] * y_hi + (0.0 if hi is None else hi)
        moe = jnp.concatenate([lo, hi], axis=1)
        x2 = x1_ref[...] + g2_ref[0] * moe
        o_ref[...] = _rms(x2) * nf_ref[...] if final else x2

    @pl.when(i == 0)
    def _():
        issue(0)

    for parity in range(2):
        @pl.when(jnp.logical_and(jnp.logical_and(i > 0, i < n_tiles), i % 2 == parity))
        def _(parity=parity):
            wait(1 - parity)
            issue(parity)
            finish(1 - parity)

    @pl.when(i == n_tiles)
    def _():
        wait((n_tiles - 1) % 2)
        finish((n_tiles - 1) % 2)


def _combine(dest, y_pad, x1, wts, g2, nf, seq, tm, final):
    t, d = x1.shape
    tpb = seq // tm
    n_tiles = t // tm
    lag = lambda i: jnp.maximum(i - 1, 0)
    return pl.pallas_call(
        functools.partial(_combine_kernel, tm=tm, final=final, n_tiles=n_tiles),
        grid=(n_tiles + 1,),
        in_specs=[pl.BlockSpec((TOP_K, tm), lambda i: (0, jnp.minimum(i, n_tiles - 1)),
                               memory_space=pltpu.SMEM),
                  pl.BlockSpec(memory_space=pl.ANY),
                  pl.BlockSpec((tm, d), lambda i: (lag(i), 0)),
                  pl.BlockSpec((TOP_K, tm), lambda i: (0, lag(i))),
                  pl.BlockSpec((1, 1, d), lambda i: (lag(i) // tpb, 0, 0)),
                  pl.BlockSpec((1, d), lambda i: (0, 0))],
        out_specs=pl.BlockSpec((tm, d), lambda i: (lag(i), 0)),
        out_shape=jax.ShapeDtypeStruct((t, d), F32),
        scratch_shapes=[pltpu.VMEM((TOP_K, tm * ROW_SLABS, LANES), y_pad.dtype),
                        pltpu.VMEM((TOP_K, tm * ROW_SLABS, LANES), y_pad.dtype),
                        pltpu.SemaphoreType.DMA((2,))],
        compiler_params=_cparams(("arbitrary",)),
        name="combine",
    )(dest, y_pad, x1, wts, g2, nf.reshape(1, d))


def _pick(n, cands):
    for c in cands:
        if n % c == 0:
            return c
    raise ValueError(f"no tile in {cands} divides {n}")


def kernel(x, c, rel_bias, w_ada, b_ada, norm_mix, w_in, w_gk_up, b_gk, gla_norm,
           w_proj_moba, w_proj_gla, w_out, norm_ffn, w_router, b_router,
           w_gate, b_gate, w_up, b_up, w_down, b_down, norm_final):
    bsz, seq, d = x.shape
    depth = w_ada.shape[0]
    assert d == D_MODEL and seq % MOBA_BLOCK == 0 and seq // MOBA_BLOCK <= MOBA_MAX_BLOCKS
    t = bsz * seq
    tm = _pick(seq, (512, 256))
    nchunk = _pick(seq // GLA_CHUNK, (16, 8, 4))
    x2d = x.reshape(t, d)
    bias = _bias_tiles(rel_bias)
    per_b = lambda v: v.reshape(bsz, 1, d)

    for l in range(depth):
        mod = _ada(c, w_ada[l], b_ada[l])
        sh1, sc1, g1, sh2, sc2, g2 = [per_b(m) for m in jnp.split(mod, 6, axis=-1)]
        qt, k_aug, vt, qkb, vb, gk, r_act, gates = _inproj(
            x2d, norm_mix[l], sc1, sh1, _regroup_w_in(w_in[l]), bsz, seq, tm)
        ya = _moba(qt, k_aug, vt, bias)
        wup = jnp.pad(w_gk_up[l], ((0, LANES - GLA_GATE_RANK), (0, 0)))
        yb = _gla(qkb, vb, gk, wup, b_gk[l].reshape(1, -1), r_act,
                  gla_norm[l].reshape(1, -1), bsz, seq, nchunk)
        x1, h2, idx, wts, rank, cnt = _merge(
            ya, yb, gates, x2d, w_proj_moba[l].astype(BF16), w_proj_gla[l].astype(BF16),
            w_out[l].astype(BF16), g1, norm_ffn[l], sc2, sh2,
            w_router[l].T, b_router[l].reshape(N_EXPERTS, 1), seq, tm)
        counts = cnt[:, 0].astype(I32)
        padded = (counts + MOE_ROWS - 1) // MOE_ROWS * MOE_ROWS
        pcum = jnp.cumsum(padded)
        pstart = (pcum - padded).astype(I32)
        n_blocks = (t * TOP_K + MOE_ROWS - 1) // MOE_ROWS + N_EXPERTS
        block_row0 = jnp.arange(n_blocks, dtype=I32) * MOE_ROWS
        block_expert = jnp.minimum(
            jnp.sum((pcum[None, :] <= block_row0[:, None]).astype(I32), axis=1),
            N_EXPERTS - 1).astype(I32)
        n_used = (pcum[-1:] // MOE_ROWS).astype(I32)
        e_ids = jnp.arange(N_EXPERTS, dtype=I32)
        later = (e_ids[None, :] > e_ids[:, None]) & (padded[None, :] > 0)
        next_expert = jnp.where(later.any(axis=1), jnp.argmax(later, axis=1), e_ids).astype(I32)
        dest = _dest(pstart, idx, rank, _pick(t, (8192, 4096, 2048, 1024, 512, 256)))
        tg = _pick(seq, (256,))
        zero_info = jnp.concatenate([jnp.maximum(pcum - MOE_ROWS, 0).astype(I32), n_used])
        x_pad = _scatter(dest, zero_info, h2, n_blocks * MOE_ROWS, tg)
        y_pad = _expert(block_expert, n_used, next_expert, x_pad, w_gate[l], b_gate[l],
                        w_up[l], b_up[l], w_down[l], b_down[l])
        x2d = _combine(dest, y_pad, x1, wts, g2, norm_final, seq, tg, l == depth - 1)
    return x2d.reshape(bsz, seq, d)
```

```python
import functools
import math

import numpy as np
import jax
import jax.numpy as jnp
from jax import lax
from jax.experimental import pallas as pl
from jax.experimental.pallas import tpu as pltpu

F32 = jnp.float32
BF16 = jnp.bfloat16
I32 = jnp.int32
HIGHEST = lax.Precision.HIGHEST

D_MODEL = 1024
MOBA_HEADS = 8
MOBA_HEAD_DIM = 64
MOBA_WIDTH = MOBA_HEADS * MOBA_HEAD_DIM
MOBA_BLOCK = 256
MOBA_TOPK = 3
MOBA_MAX_BLOCKS = 32
REL_BUCKETS = 32
REL_MAX_DIST = 128
GLA_HEADS = 4
GLA_KEY_DIM = D_MODEL // 2
GLA_VALUE_DIM = D_MODEL
GLA_DK = GLA_KEY_DIM // GLA_HEADS
GLA_DV = GLA_VALUE_DIM // GLA_HEADS
GLA_GATE_RANK = 16
GLA_GATE_NORMALIZER = 16.0
GLA_CHUNK = 64
N_EXPERTS = 32
TOP_K = 4
SWIGLU_ALPHA = 1.702
SWIGLU_LIMIT = 7.0
MOE_ROWS = 512
ROW_UNROLL = 8
EPS = 1e-6
LANES = 128
SUBLANES = 8
BF16_SUBLANES = 16
ADA_COLS = 768
NEG_BIG = -1e30
LOG2E = math.log2(math.e)
VMEM_LIMIT = 56 * 1024 * 1024


def _cparams(sem, vmem=None):
    return pltpu.CompilerParams(dimension_semantics=sem,
                                vmem_limit_bytes=vmem or VMEM_LIMIT)


def _nt_dot(a, b, **kw):
    return lax.dot_general(a, b, (((1,), (1,)), ((), ())),
                           preferred_element_type=F32, **kw)


def _rms(x):
    return x * lax.rsqrt(jnp.mean(x * x, axis=-1, keepdims=True) + EPS)


def _sigmoid(x):
    return 1.0 / (1.0 + jnp.exp(-x))


U32 = jnp.uint32
_HI16 = 0xFFFF0000


def _pack_halves(x):
    n = x.shape[1] // 2
    lo = pltpu.bitcast(x[:, :n].astype(BF16).astype(F32), U32)
    hi = pltpu.bitcast(x[:, n:].astype(BF16).astype(F32), U32)
    return (hi & U32(_HI16)) | (lo >> 16)


def _unpack_halves(w):
    return (pltpu.bitcast(w << 16, F32), pltpu.bitcast(w & U32(_HI16), F32))


ROW_SLABS = D_MODEL // 2 // LANES


def _store_rows(ref, words):
    m = words.shape[0]
    for c in range(ROW_SLABS):
        ref[pl.ds(c, m, stride=ROW_SLABS), :] = words[:, c * LANES:(c + 1) * LANES]


def _load_rows(ref):
    m = ref.shape[0] // ROW_SLABS
    return jnp.concatenate(
        [ref[pl.ds(c, m, stride=ROW_SLABS), :] for c in range(ROW_SLABS)], axis=1)


def _row(ref, i):
    return ref.at[pl.ds(pl.multiple_of(i * ROW_SLABS, ROW_SLABS), ROW_SLABS)]


def _ada_kernel(c_ref, w_ref, b_ref, o_ref):
    c = c_ref[...]
    s = c * _sigmoid(c)
    o_ref[...] = jnp.dot(s, w_ref[...], precision=HIGHEST,
                         preferred_element_type=F32) + b_ref[...]


def _ada(c, w, b):
    bsz, d = c.shape
    n = w.shape[1]
    rows = -(-bsz // SUBLANES) * SUBLANES
    cp = jnp.zeros((rows, d), F32).at[:bsz].set(c)
    tn = ADA_COLS
    out = pl.pallas_call(
        _ada_kernel,
        grid=(n // tn,),
        in_specs=[pl.BlockSpec((rows, d), lambda j: (0, 0)),
                  pl.BlockSpec((d, tn), lambda j: (0, j)),
                  pl.BlockSpec((1, tn), lambda j: (0, j))],
        out_specs=pl.BlockSpec((rows, tn), lambda j: (0, j)),
        out_shape=jax.ShapeDtypeStruct((rows, n), F32),
        compiler_params=_cparams(("arbitrary",)),
        name="ada",
    )(cp, w, b.reshape(1, n))
    return out[:bsz]


_OFF_QA = 0
_OFF_KA = _OFF_QA + MOBA_WIDTH
_OFF_VA = _OFF_KA + MOBA_WIDTH
_OFF_QKB = _OFF_VA + MOBA_WIDTH
_OFF_VB = _OFF_QKB + 2 * GLA_KEY_DIM
_OFF_GK = _OFF_VB + GLA_VALUE_DIM
_OFF_R = _OFF_GK + LANES
_OFF_G = _OFF_R + GLA_VALUE_DIM
_W_CAT = _OFF_G + 2 * D_MODEL


def _regroup_w_in(w):
    o_gk = 3 * MOBA_WIDTH + 2 * GLA_KEY_DIM + GLA_VALUE_DIM
    w = w.astype(BF16)
    gk = jnp.pad(w[:, o_gk:o_gk + GLA_GATE_RANK], ((0, 0), (0, LANES - GLA_GATE_RANK)))
    return jnp.concatenate([w[:, :o_gk], gk, w[:, o_gk + GLA_GATE_RANK:]], axis=1)


def _inproj_kernel(x_ref, nw_ref, sc_ref, sh_ref, w_ref,
                   qt_ref, ka_ref, vt_ref, qkb_ref, vb_ref, gk_ref, r_ref, g_ref, *, tpb):
    tm = x_ref.shape[0]
    hd = MOBA_HEAD_DIM
    nbt = tm // MOBA_BLOCK
    h = _rms(x_ref[...]) * nw_ref[...]
    h = h * (1.0 + sc_ref[0]) + sh_ref[0]
    hb = h.astype(BF16)

    def mm(a, b):
        return jnp.dot(hb, w_ref[:, a:b], preferred_element_type=F32)

    q_t = (mm(_OFF_QA, _OFF_KA) * (hd ** -0.5 * LOG2E)).T
    v_t = mm(_OFF_VA, _OFF_QKB).T
    k_all = mm(_OFF_KA, _OFF_VA)
    blk0 = (pl.program_id(0) % tpb) * nbt
    lane = lax.broadcasted_iota(I32, (MOBA_BLOCK, LANES), 1)
    ones_rows = jnp.where(
        lax.broadcasted_iota(I32, (MOBA_VT_ROWS - hd, MOBA_BLOCK), 0) == 0, 1.0, 0.0)
    heads_per_tile = LANES // hd
    for hh in range(MOBA_HEADS):
        qt_ref[0, hh] = q_t[hh * hd:(hh + 1) * hd].astype(BF16)
        tile = hh // heads_per_tile
        k_h = k_all[:, tile * LANES:(tile + 1) * LANES]
        if hh % heads_per_tile:
            k_h = pltpu.roll(k_h, LANES - (hh % heads_per_tile) * hd, axis=1)
        for j in range(nbt):
            rows = slice(j * MOBA_BLOCK, (j + 1) * MOBA_BLOCK)
            onehot = jnp.where(lane == hd + blk0 + j, 1.0, 0.0)
            ka_ref[0, hh, j] = jnp.where(lane < hd, k_h[rows], onehot).astype(BF16)
            vt_ref[0, hh, j, 0:hd, :] = v_t[hh * hd:(hh + 1) * hd, rows].astype(BF16)
            vt_ref[0, hh, j, hd:MOBA_VT_ROWS, :] = ones_rows.astype(BF16)
    qkb_ref[...] = mm(_OFF_QKB, _OFF_VB).astype(BF16)
    vb_ref[...] = mm(_OFF_VB, _OFF_GK).astype(BF16)
    gk_ref[...] = mm(_OFF_GK, _OFF_R)
    r = mm(_OFF_R, _OFF_G)
    r_ref[...] = (r * _sigmoid(r)).astype(BF16)
    g_ref[...] = _sigmoid(mm(_OFF_G, _W_CAT)).astype(BF16)


def _inproj(x2d, nw, sc, sh, w_cat, bsz, seq, tm):
    t, d = x2d.shape
    tpb = seq // tm
    nbt = tm // MOBA_BLOCK
    nh, hd = MOBA_HEADS, MOBA_HEAD_DIM
    row = lambda w: pl.BlockSpec((tm, w), lambda i: (i, 0))
    per_b = pl.BlockSpec((1, 1, d), lambda i: (i // tpb, 0, 0))
    rows_out = [(2 * GLA_KEY_DIM, BF16), (GLA_VALUE_DIM, BF16), (LANES, F32),
                (GLA_VALUE_DIM, BF16), (2 * D_MODEL, BF16)]
    return pl.pallas_call(
        functools.partial(_inproj_kernel, tpb=tpb),
        grid=(t // tm,),
        in_specs=[row(d), pl.BlockSpec((1, d), lambda i: (0, 0)), per_b, per_b,
                  pl.BlockSpec((d, _W_CAT), lambda i: (0, 0), pipeline_mode=pl.Buffered(1))],
        out_specs=[pl.BlockSpec((1, nh, hd, tm), lambda i: (i // tpb, 0, 0, i % tpb)),
                   pl.BlockSpec((1, nh, nbt, MOBA_BLOCK, LANES),
                                lambda i: (i // tpb, 0, i % tpb, 0, 0)),
                   pl.BlockSpec((1, nh, nbt, MOBA_VT_ROWS, MOBA_BLOCK),
                                lambda i: (i // tpb, 0, i % tpb, 0, 0))]
                  + [row(w) for w, _ in rows_out],
        out_shape=[jax.ShapeDtypeStruct((bsz, nh, hd, seq), BF16),
                   jax.ShapeDtypeStruct((bsz, nh, seq // MOBA_BLOCK, MOBA_BLOCK, LANES), BF16),
                   jax.ShapeDtypeStruct((bsz, nh, seq // MOBA_BLOCK, MOBA_VT_ROWS, MOBA_BLOCK),
                                        BF16)]
                  + [jax.ShapeDtypeStruct((t, w), dt) for w, dt in rows_out],
        compiler_params=_cparams(("arbitrary",)),
        name="inproj",
    )(x2d, nw.reshape(1, d), sc, sh, w_cat)


def _t5_bucket_np(n):
    n = np.maximum(n, 0)
    max_exact = REL_BUCKETS // 2
    nf = np.maximum(n, max_exact).astype(np.float32)
    large = max_exact + (np.log(nf / max_exact) / math.log(REL_MAX_DIST / max_exact)
                         * (REL_BUCKETS - max_exact)).astype(np.int32)
    large = np.minimum(large, REL_BUCKETS - 1)
    return np.where(n < max_exact, n, large).astype(np.int32)


def _bucket_table():
    kj = np.arange(MOBA_BLOCK)[:, None]
    qi = np.arange(2 * MOBA_BLOCK)[None, :] % MOBA_BLOCK
    prev = np.arange(2 * MOBA_BLOCK)[None, :] < MOBA_BLOCK
    bucket = _t5_bucket_np(qi - kj + np.where(prev, MOBA_BLOCK, 0))
    return np.where(prev | (kj <= qi), bucket, -1).astype(np.int32)


def _bias_kernel(rb_ref, bucket_ref, o_ref):
    h = pl.program_id(0)
    bk = bucket_ref[...]
    far = rb_ref[(REL_BUCKETS - 1) * MOBA_HEADS + h]
    acc = jnp.zeros(bk.shape, F32)
    for b in range(REL_BUCKETS):
        acc = jnp.where(bk == b, rb_ref[b * MOBA_HEADS + h] - far, acc)
    o_ref[0] = jnp.where(bk < 0, NEG_BIG, acc * LOG2E)


def _bias_tiles(rel_bias):
    bucket = jnp.asarray(_bucket_table())
    return pl.pallas_call(
        _bias_kernel,
        grid=(MOBA_HEADS,),
        in_specs=[pl.BlockSpec(memory_space=pltpu.SMEM),
                  pl.BlockSpec(bucket.shape, lambda h: (0, 0))],
        out_specs=pl.BlockSpec((1,) + bucket.shape, lambda h: (h, 0, 0)),
        out_shape=jax.ShapeDtypeStruct((MOBA_HEADS,) + bucket.shape, F32),
        compiler_params=_cparams(("arbitrary",)),
        name="bias",
    )(rel_bias.reshape(-1), bucket)


MOBA_HEADS_PER_STEP = 8
MOBA_FAR_GROUP = 2
MOBA_QBLOCKS_PER_STEP = 4
MOBA_FAR_LAGS = (0, 3, 6)
MOBA_NEAR_LAGS = (0, 4, 8, 12)
MOBA_VT_ROWS = MOBA_HEAD_DIM + BF16_SUBLANES


def _moba_kernel(qt_ref, k_hbm, vt_hbm, bias_ref, o_ref, kmean_ref, qa_ref, k_ref, vt_ref,
                 kv_sems):
    blk = MOBA_BLOCK
    nb = MOBA_MAX_BLOCKS
    hd = MOBA_HEAD_DIM
    hp = MOBA_HEADS_PER_STEP
    grp = MOBA_FAR_GROUP
    nq = MOBA_QBLOCKS_PER_STEP
    pair = pl.program_id(2)
    neg = -jnp.inf
    items = [(h, j) for j in range(nq) for h in range(hp)]

    @pl.when(pair == 0)
    def _():
        kmean_ref[...] = jnp.zeros(kmean_ref.shape, F32)

    def kv_copies(first_blk):
        src = (pl.program_id(0), pl.ds(pl.program_id(1) * hp, hp), pl.ds(first_blk, nq))
        dst = (0, slice(None), pl.ds(first_blk, nq))
        return (pltpu.make_async_copy(k_hbm.at[src], k_ref.at[dst], kv_sems.at[0]),
                pltpu.make_async_copy(vt_hbm.at[src], vt_ref.at[dst], kv_sems.at[1]))

    @pl.when(pair == 0)
    def _():
        for cp in kv_copies(0):
            cp.start()

    for cp in kv_copies(pair * nq):
        cp.wait()

    @pl.when(pair + 1 < pl.num_programs(2))
    def _():
        for cp in kv_copies((pair + 1) * nq):
            cp.start()

    for h in range(hp):
        for j in range(nq):
            kmean_ref[h, pl.ds(pair * nq + j, 1), :] = jnp.mean(
                k_ref[0, h, pair * nq + j].astype(F32), axis=0, keepdims=True)

    row = lax.broadcasted_iota(I32, (nb, blk), 0)
    rowf = row.astype(F32)
    pad = jnp.zeros((LANES - hd, blk), BF16)
    pad_hi = jnp.zeros((LANES - hd - nb, blk), BF16)

    def skewed(stages, lags, todo=None):
        todo = list(range(len(items))) if todo is None else todo
        vals = {}
        for step in range(len(todo) + lags[-1]):
            for stage, lag in zip(stages, lags):
                pos = step - lag
                if 0 <= pos < len(todo):
                    vals[todo[pos]] = stage(todo[pos], vals.get(todo[pos]))
        return vals

    def select(n, _):
        h, j = items[n]
        qi = pair * nq + j
        qt = qt_ref[0, h, :, j * blk:(j + 1) * blk]
        km = kmean_ref[h, :, 0:hd]
        km_hi = km.astype(BF16)
        km_lo = (km - km_hi.astype(F32)).astype(BF16)
        gate = (jnp.dot(km_hi, qt, preferred_element_type=F32)
                + jnp.dot(km_lo, qt, preferred_element_type=F32))
        g = jnp.where(row < qi, gate, neg)
        sel = jnp.zeros((nb, blk), F32)
        for _ in range(MOBA_TOPK):
            mx = jnp.max(g, axis=0, keepdims=True)
            first = jnp.min(jnp.where(g == mx, rowf, float(nb)), axis=0, keepdims=True)
            pick = rowf == jnp.where(mx > neg, first, -1.0)
            sel = jnp.where(pick, 1.0, sel)
            g = jnp.where(pick, neg, g)
        mask_prev = jnp.where(sel > 0.0, jnp.where(row == qi - 1, 0.0, NEG_BIG), NEG_BIG)
        mask_far = jnp.where(sel > 0.0, jnp.where(row < qi - 1, 0.0, NEG_BIG), NEG_BIG)
        qa_ref[n] = jnp.concatenate([qt, mask_far.astype(BF16), pad_hi], axis=0)
        return (jnp.concatenate([qt, pad], axis=0),
                jnp.concatenate([qt, mask_prev.astype(BF16), pad_hi], axis=0))

    def own_prev(n):
        h, j = items[n]
        qi = pair * nq + j
        return h, qi, jnp.maximum(qi - 1, 0)

    def near_scores(n, qa):
        h, qi, prev_j = own_prev(n)
        qa_own, qa_prev = qa
        s_own = jnp.dot(k_ref[0, h, qi], qa_own, preferred_element_type=F32)
        s_prev = jnp.dot(k_ref[0, h, prev_j], qa_prev, preferred_element_type=F32)
        return s_own, s_prev

    def near_softmax(n, ss):
        h = items[n][0]
        s_own, s_prev = ss
        s = jnp.concatenate([s_own + bias_ref[h, :, blk:2 * blk],
                             s_prev + bias_ref[h, :, 0:blk]], axis=0)
        m0 = jnp.max(s, axis=0, keepdims=True)
        return m0, jnp.exp2(s - m0)

    def near_pv(n, mp):
        h, qi, prev_j = own_prev(n)
        m0, p = mp
        pb = p.astype(BF16)
        acc = (jnp.dot(vt_ref[0, h, qi], pb[0:blk], preferred_element_type=F32)
               + jnp.dot(vt_ref[0, h, prev_j], pb[blk:2 * blk], preferred_element_type=F32))
        return m0, acc

    near = skewed([select, near_scores, near_softmax, near_pv], MOBA_NEAR_LAGS)
    states = tuple(near[n] for n in range(len(items)))

    def far(gi, states, todo=None):
        j0 = gi * grp

        def qk(n, _):
            kt = k_ref[0, items[n][0], pl.ds(j0, grp)].reshape(grp * blk, LANES)
            return jnp.dot(kt, qa_ref[n], preferred_element_type=F32)

        def softmax(n, s):
            m_old = states[n][0]
            m_new = jnp.maximum(m_old, jnp.max(s, axis=0, keepdims=True))
            return m_new, jnp.exp2(m_old - m_new), s

        def pv(n, sm):
            m_new, a, s = sm
            tot = a * states[n][1]
            for i in range(grp):
                pb = jnp.exp2(s[i * blk:(i + 1) * blk] - m_new).astype(BF16)
                tot = tot + jnp.dot(vt_ref[0, items[n][0], j0 + i], pb,
                                    preferred_element_type=F32)
            return m_new, tot

        new = skewed([qk, softmax, pv], MOBA_FAR_LAGS, todo)
        return tuple(new.get(n, states[n]) for n in range(len(items)))

    sub = nq // grp
    states = lax.fori_loop(0, pair * sub, far, tuple(states))
    for extra in range(1, sub):
        later = [n for n, (_, j) in enumerate(items) if j // grp >= extra]
        states = far(pair * sub + extra - 1, states, later)

    for n, (h, j) in enumerate(items):
        acc = states[n][1]
        o_ref[0, h * hd:(h + 1) * hd, j * blk:(j + 1) * blk] = (
            acc[0:hd] / acc[hd:hd + 1]).astype(o_ref.dtype)


def _moba(qt, k_aug, vt, bias):
    bsz, nh, hd, s = qt.shape
    blk = MOBA_BLOCK
    hp = MOBA_HEADS_PER_STEP
    nq = MOBA_QBLOCKS_PER_STEP
    nblk = s // blk
    assert nh % hp == 0 and nblk % nq == 0 and nq % MOBA_FAR_GROUP == 0
    return pl.pallas_call(
        _moba_kernel,
        grid=(bsz, nh // hp, nblk // nq),
        in_specs=[pl.BlockSpec((1, hp, hd, nq * blk), lambda b, g, i: (b, g, 0, i)),
                  pl.BlockSpec(memory_space=pl.ANY),
                  pl.BlockSpec(memory_space=pl.ANY),
                  pl.BlockSpec((hp, blk, 2 * blk), lambda b, g, i: (g, 0, 0),
                               pipeline_mode=pl.Buffered(1))],
        out_specs=pl.BlockSpec((1, hp * hd, nq * blk), lambda b, g, i: (b, g, i)),
        out_shape=jax.ShapeDtypeStruct((bsz, nh * hd, s), BF16),
        scratch_shapes=[pltpu.VMEM((hp, MOBA_MAX_BLOCKS, LANES), F32),
                        pltpu.VMEM((hp * nq, LANES, blk), BF16),
                        pltpu.VMEM((1, hp, nblk, blk, LANES), BF16),
                        pltpu.VMEM((1, hp, nblk, MOBA_VT_ROWS, blk), BF16),
                        pltpu.SemaphoreType.DMA((2,))],
        compiler_params=_cparams(("arbitrary", "arbitrary", "arbitrary")),
        name="moba",
    )(qt, k_aug, vt, bias)


def _gla_kernel(q_ref, k_ref, v_ref, gk_ref, wup_ref, bgk_ref, r_ref, gn_ref, o_ref,
                state_ref, *, nchunk):
    ch = GLA_CHUNK
    tc = nchunk * ch
    dk, dv = GLA_DK, GLA_DV

    @pl.when(pl.program_id(1) == 0)
    def _():
        state_ref[...] = jnp.zeros(state_ref.shape, F32)

    rin = lax.broadcasted_iota(I32, (tc, dk), 0) & (ch - 1)
    causal = (lax.broadcasted_iota(I32, (ch, ch), 1) <= lax.broadcasted_iota(I32, (ch, ch), 0))
    eye = (lax.broadcasted_iota(I32, (dk, dk), 0) == lax.broadcasted_iota(I32, (dk, dk), 1))
    chunks = [slice(n * ch, (n + 1) * ch) for n in range(nchunk)]
    gk = gk_ref[...]
    gk_hi = gk.astype(BF16)
    gk_lo = (gk - gk_hi.astype(F32)).astype(BF16)

    def prep(h, _):
        ks = slice(h * dk, (h + 1) * dk)
        w = wup_ref[:, ks]
        w_hi = w.astype(BF16)
        w_lo = (w - w_hi.astype(F32)).astype(BF16)
        z = (jnp.dot(gk_hi, w_hi, preferred_element_type=F32)
             + jnp.dot(gk_lo, w_hi, preferred_element_type=F32)
             + jnp.dot(gk_hi, w_lo, preferred_element_type=F32) + bgk_ref[:, ks])
        log_a = ((jnp.minimum(z, 0.0) - jnp.log(1.0 + jnp.exp(-jnp.abs(z))))
                 / GLA_GATE_NORMALIZER)
        b = log_a
        sh = 1
        while sh < ch:
            b = b + jnp.where(rin >= sh, pltpu.roll(b, sh, axis=0), 0.0)
            sh *= 2
        q = q_ref[:, ks].astype(F32) * (dk ** -0.5)
        k = k_ref[:, ks].astype(F32)
        q_g = (q * jnp.exp(b)).astype(BF16)
        k_g = (k * jnp.exp(-b)).astype(BF16)
        b3 = b.reshape(nchunk, ch, dk)
        b_last = b3[:, ch - 1:ch, :]
        k_end = (k * jnp.exp(jnp.broadcast_to(b_last, b3.shape) - b3).reshape(tc, dk)
                 ).astype(BF16)
        return q_g, k_g, k_end, jnp.exp(b_last)

    def local(h, pre):
        q_g, k_g, k_end, decay = pre
        o_intra, kv, decay_col = [], [], []
        for n, sl in enumerate(chunks):
            v_c = v_ref[sl, h * dv:(h + 1) * dv]
            att = jnp.where(causal, _nt_dot(q_g[sl], k_g[sl]), 0.0)
            o_intra.append(jnp.dot(att.astype(BF16), v_c, preferred_element_type=F32))
            kv.append(lax.dot_general(k_end[sl], v_c, (((0,), (0,)), ((), ())),
                                      preferred_element_type=F32))
            decay_col.append(jnp.sum(
                jnp.where(eye, jnp.broadcast_to(decay[n], (dk, dk)), 0.0),
                axis=1, keepdims=True))
        return q_g, o_intra, kv, decay_col

    def chain(h, loc):
        q_g, o_intra, kv, decay_col = loc
        state = state_ref[h]
        outs = []
        for n, sl in enumerate(chunks):
            outs.append(o_intra[n] + jnp.dot(q_g[sl], state.astype(BF16),
                                             preferred_element_type=F32))
            state = decay_col[n] * state + kv[n]
        state_ref[h] = state
        return jnp.concatenate(outs, axis=0)

    def finish(h, o):
        vs = slice(h * dv, (h + 1) * dv)
        o_ref[:, vs] = (_rms(o) * gn_ref[...] * r_ref[:, vs].astype(F32)).astype(o_ref.dtype)
        return None

    stages = [prep, local, chain, finish]
    vals = [None] * GLA_HEADS
    for step in range(GLA_HEADS + len(stages) - 1):
        for si, stage in enumerate(stages):
            h = step - si
            if 0 <= h < GLA_HEADS:
                vals[h] = stage(h, vals[h])


def _gla(qkb, vb, gk, wup, bgk, r_act, gn, bsz, seq, nchunk):
    t = qkb.shape[0]
    tc = nchunk * GLA_CHUNK
    nc = seq // tc
    rowblk = lambda w, off: pl.BlockSpec((tc, w), lambda b, c: (b * nc + c, off))
    full = lambda a: pl.BlockSpec(a.shape, lambda b, c: (0, 0))
    return pl.pallas_call(
        functools.partial(_gla_kernel, nchunk=nchunk),
        grid=(bsz, nc),
        in_specs=[rowblk(GLA_KEY_DIM, 0), rowblk(GLA_KEY_DIM, 1), rowblk(GLA_VALUE_DIM, 0),
                  rowblk(LANES, 0), full(wup), full(bgk), rowblk(GLA_VALUE_DIM, 0), full(gn)],
        out_specs=rowblk(GLA_VALUE_DIM, 0),
        out_shape=jax.ShapeDtypeStruct((t, GLA_VALUE_DIM), BF16),
        scratch_shapes=[pltpu.VMEM((GLA_HEADS, GLA_DK, GLA_DV), F32)],
        compiler_params=_cparams(("arbitrary", "arbitrary")),
        name="gla",
    )(qkb, qkb, vb, gk, wup, bgk, r_act, gn)


def _merge_kernel(ya_ref, yb_ref, g_ref, x_ref, wpa_ref, wpb_ref, wout_ref, g1_ref,
                  nw_ref, sc_ref, sh_ref, wr_ref, br_ref,
                  x1_ref, h2_ref, idx_ref, wts_ref, rank_ref, cnt_ref, carry_ref):
    @pl.when(pl.program_id(0) == 0)
    def _():
        carry_ref[...] = jnp.zeros(carry_ref.shape, F32)

    pa = lax.dot_general(ya_ref[0], wpa_ref[...], (((0,), (0,)), ((), ())),
                         preferred_element_type=F32)
    pb = jnp.dot(yb_ref[...], wpb_ref[...], preferred_element_type=F32)
    mixed = (g_ref[:, 0:D_MODEL].astype(F32) * pa
             + g_ref[:, D_MODEL:2 * D_MODEL].astype(F32) * pb)
    y = jnp.dot(mixed.astype(BF16), wout_ref[...], preferred_element_type=F32)
    x1 = x_ref[...] + g1_ref[0] * y
    x1_ref[...] = x1
    h2 = _rms(x1) * nw_ref[...]
    h2 = h2 * (1.0 + sc_ref[0]) + sh_ref[0]
    _store_rows(h2_ref, _pack_halves(h2))
    h_hi = h2.astype(BF16)
    h_lo = (h2 - h_hi.astype(F32)).astype(BF16)
    w = wr_ref[...]
    w_hi = w.astype(BF16)
    w_lo = (w - w_hi.astype(F32)).astype(BF16)
    logits = (_nt_dot(w_hi, h_hi) + _nt_dot(w_hi, h_lo) + _nt_dot(w_lo, h_hi)
              + br_ref[...])
    rowf = lax.broadcasted_iota(I32, logits.shape, 0).astype(F32)
    vals, idxs = [], []
    cur = logits
    for _ in range(TOP_K):
        mx = jnp.max(cur, axis=0, keepdims=True)
        first = jnp.min(jnp.where(cur == mx, rowf, float(N_EXPERTS)), axis=0, keepdims=True)
        vals.append(mx)
        idxs.append(first)
        cur = jnp.where(rowf == first, -jnp.inf, cur)
    es = [jnp.exp(v - vals[0]) for v in vals]
    tot = es[0]
    for e in es[1:]:
        tot = tot + e
    idx_ref[...] = jnp.concatenate(idxs, axis=0).astype(I32)
    wts_ref[...] = jnp.concatenate([e / tot for e in es], axis=0)
    tm = logits.shape[1]
    before = (lax.broadcasted_iota(I32, (tm, tm), 0) < lax.broadcasted_iota(I32, (tm, tm), 1))
    upper = jnp.where(before, 1.0, 0.0).astype(BF16)
    carry = carry_ref[:, 0:1]
    ranks = []
    for k in range(TOP_K):
        onehot = idxs[k] == rowf
        onef = jnp.where(onehot, 1.0, 0.0)
        earlier = jnp.dot(onef.astype(BF16), upper, preferred_element_type=F32) + carry
        ranks.append(jnp.sum(jnp.where(onehot, earlier, 0.0), axis=0, keepdims=True))
        carry = carry + jnp.sum(onef, axis=1, keepdims=True)
    rank_ref[...] = jnp.concatenate(ranks, axis=0).astype(I32)
    total = jnp.broadcast_to(carry, carry_ref.shape)
    carry_ref[...] = total
    cnt_ref[...] = total


def _merge(ya, yb, gates, x2d, wpa, wpb, wout, g1, nw, sc, sh, wr_t, br, seq, tm):
    t, d = x2d.shape
    tpb = seq // tm
    row = lambda w: pl.BlockSpec((tm, w), lambda i: (i, 0))
    full = lambda a: pl.BlockSpec(a.shape, lambda i: (0,) * a.ndim)
    per_b = pl.BlockSpec((1, 1, d), lambda i: (i // tpb, 0, 0))
    colblk = pl.BlockSpec((TOP_K, tm), lambda i: (0, i))
    return pl.pallas_call(
        _merge_kernel,
        grid=(t // tm,),
        in_specs=[pl.BlockSpec((1, MOBA_WIDTH, tm), lambda i: (i // tpb, 0, i % tpb)),
                  row(GLA_VALUE_DIM), row(2 * D_MODEL), row(d),
                  full(wpa), full(wpb), full(wout), per_b,
                  pl.BlockSpec((1, d), lambda i: (0, 0)), per_b, per_b,
                  full(wr_t), full(br)],
        out_specs=[row(d), pl.BlockSpec((tm * ROW_SLABS, LANES), lambda i: (i, 0)),
                   colblk, colblk, colblk,
                   pl.BlockSpec((N_EXPERTS, LANES), lambda i: (0, 0))],
        out_shape=[jax.ShapeDtypeStruct((t, d), F32),
                   jax.ShapeDtypeStruct((t * ROW_SLABS, LANES), U32),
                   jax.ShapeDtypeStruct((TOP_K, t), I32), jax.ShapeDtypeStruct((TOP_K, t), F32),
                   jax.ShapeDtypeStruct((TOP_K, t), I32),
                   jax.ShapeDtypeStruct((N_EXPERTS, LANES), F32)],
        scratch_shapes=[pltpu.VMEM((N_EXPERTS, LANES), F32)],
        compiler_params=_cparams(("arbitrary",)),
        name="merge",
    )(ya, yb, gates, x2d, wpa, wpb, wout, g1, nw.reshape(1, d), sc, sh, wr_t, br)


def _dest_kernel(pstart_ref, idx_ref, rank_ref, dest_ref):
    idx = idx_ref[...]
    off = jnp.zeros(idx.shape, I32)
    for e in range(N_EXPERTS):
        off = jnp.where(idx == e, pstart_ref[e], off)
    dest_ref[...] = rank_ref[...] + off


def _dest(pstart, idx, rank, tm):
    t = idx.shape[1]
    blk = pl.BlockSpec((TOP_K, tm), lambda i: (0, i))
    return pl.pallas_call(
        _dest_kernel,
        grid=(t // tm,),
        in_specs=[pl.BlockSpec(memory_space=pltpu.SMEM), blk, blk],
        out_specs=blk,
        out_shape=jax.ShapeDtypeStruct((TOP_K, t), I32),
        compiler_params=_cparams(("arbitrary",)),
        name="dest",
    )(pstart, idx, rank)


def _scatter_kernel(dest_ref, zrow_ref, h_ref, xout_hbm, zbuf, stage, sems, zsem, *, tm):
    @pl.when(pl.program_id(0) == 0)
    def _():
        zbuf[...] = jnp.zeros(zbuf.shape, zbuf.dtype)

        def zero_rows(row0):
            line0 = pl.multiple_of(row0 * ROW_SLABS, MOE_ROWS * ROW_SLABS)
            return pltpu.make_async_copy(
                zbuf, xout_hbm.at[pl.ds(line0, MOE_ROWS * ROW_SLABS)], zsem)

        for e in range(N_EXPERTS):
            zero_rows(zrow_ref[e]).start()
        for e in range(N_EXPERTS):
            zero_rows(zrow_ref[e]).wait()

        def start_tail(j, carry):
            zero_rows(j * MOE_ROWS).start()
            return carry

        def wait_tail(j, carry):
            zero_rows(j * MOE_ROWS).wait()
            return carry

        n_all = xout_hbm.shape[0] // (MOE_ROWS * ROW_SLABS)
        lax.fori_loop(zrow_ref[N_EXPERTS], n_all, start_tail, 0)
        lax.fori_loop(zrow_ref[N_EXPERTS], n_all, wait_tail, 0)

    i = pl.program_id(0)
    slot = i % 2
    src = stage.at[slot]
    src[...] = h_ref[...]

    def issue(g, carry):
        base = pl.multiple_of(g * ROW_UNROLL, ROW_UNROLL)
        for j in range(ROW_UNROLL):
            for k in range(TOP_K):
                d = dest_ref[k, base + j]
                pltpu.make_async_copy(_row(src, base + j), _row(xout_hbm, d),
                                      sems.at[slot]).start(priority=k % 2)
        return carry

    lax.fori_loop(0, tm // ROW_UNROLL, issue, 0)

    def wait_tile(s):
        for k in range(TOP_K):
            pltpu.make_async_copy(stage.at[s], xout_hbm.at[pl.ds(0, tm * ROW_SLABS)],
                                  sems.at[s]).wait()

    @pl.when(i > 0)
    def _():
        wait_tile(1 - slot)

    @pl.when(i == pl.num_programs(0) - 1)
    def _():
        wait_tile(slot)


def _scatter(dest, last_block_row, h2, n_pad, tm):
    t = h2.shape[0] // ROW_SLABS
    return pl.pallas_call(
        functools.partial(_scatter_kernel, tm=tm),
        grid=(t // tm,),
        in_specs=[pl.BlockSpec((TOP_K, tm), lambda i: (0, i), memory_space=pltpu.SMEM),
                  pl.BlockSpec(memory_space=pltpu.SMEM),
                  pl.BlockSpec((tm * ROW_SLABS, LANES), lambda i: (i, 0))],
        out_specs=pl.BlockSpec(memory_space=pl.ANY),
        out_shape=jax.ShapeDtypeStruct((n_pad * ROW_SLABS, LANES), h2.dtype),
        scratch_shapes=[pltpu.VMEM((MOE_ROWS * ROW_SLABS, LANES), h2.dtype),
                        pltpu.VMEM((2, tm * ROW_SLABS, LANES), h2.dtype),
                        pltpu.SemaphoreType.DMA((2,)), pltpu.SemaphoreType.DMA],
        compiler_params=_cparams(("arbitrary",)),
        name="scatter",
    )(dest, last_block_row, h2)


def _expert_kernel(be_ref, nused_ref, nxt_ref, x_ref, wg_hbm, wu_hbm, wd_hbm, b_ref, o_ref,
                   sg_ref, su_ref, sd_ref, wgb_ref, wub_ref, wdb_ref, wsem):
    i = pl.program_id(0)
    e = be_ref[i]
    active = i < nused_ref[0]
    first = jnp.logical_and(active, jnp.logical_or(i == 0, e != be_ref[jnp.maximum(i - 1, 0)]))

    def weight_copies(expert):
        return (pltpu.make_async_copy(wg_hbm.at[expert], sg_ref, wsem),
                pltpu.make_async_copy(wu_hbm.at[expert], su_ref, wsem),
                pltpu.make_async_copy(wd_hbm.at[expert], sd_ref, wsem))

    @pl.when(i == 0)
    def _():
        for cp in weight_copies(e):
            cp.start()

    @pl.when(first)
    def _():
        for cp in weight_copies(e):
            cp.wait()
        wgb_ref[...] = sg_ref[...].astype(BF16)
        wub_ref[...] = su_ref[...].astype(BF16)
        wdb_ref[...] = sd_ref[...].astype(BF16)

    @pl.when(jnp.logical_and(first, nxt_ref[e] != e))
    def _():
        for cp in weight_copies(nxt_ref[e]):
            cp.start()

    @pl.when(active)
    def _():
        xb = jnp.concatenate([v.astype(BF16) for v in _unpack_halves(_load_rows(x_ref))],
                             axis=1)
        g = jnp.minimum(jnp.dot(xb, wgb_ref[...], preferred_element_type=F32) + b_ref[0, 0:1],
                        SWIGLU_LIMIT)
        u = jnp.clip(jnp.dot(xb, wub_ref[...], preferred_element_type=F32) + b_ref[0, 1:2],
                     -SWIGLU_LIMIT, SWIGLU_LIMIT)
        act = g * _sigmoid(SWIGLU_ALPHA * g) * (u + 1.0)
        _store_rows(o_ref, _pack_halves(
            jnp.dot(act.astype(BF16), wdb_ref[...], preferred_element_type=F32)
            + b_ref[0, 2:3]))

    @pl.when(i >= nused_ref[0])
    def _():
        o_ref[...] = jnp.zeros(o_ref.shape, o_ref.dtype)


def _expert(block_expert, n_used, next_expert, x_pad, wg, bg, wu, bu, wd, bd):
    lines = MOE_ROWS * ROW_SLABS
    _, d, f = wg.shape
    assert d == f, "the three bias vectors are passed as one (E, 3, d) array"
    nblk = x_pad.shape[0] // lines
    hbm = pl.BlockSpec(memory_space=pl.ANY)
    grid_spec = pltpu.PrefetchScalarGridSpec(
        num_scalar_prefetch=3,
        grid=(nblk,),
        in_specs=[pl.BlockSpec((lines, LANES),
                               lambda i, be, nu, nx: (jnp.minimum(i, nu[0] - 1), 0)),
                  hbm, hbm, hbm,
                  pl.BlockSpec((1, 3, f), lambda i, be, nu, nx: (be[i], 0, 0))],
        out_specs=pl.BlockSpec((lines, LANES), lambda i, be, nu, nx: (i, 0)),
        scratch_shapes=[pltpu.VMEM((d, f), F32), pltpu.VMEM((d, f), F32), pltpu.VMEM((f, d), F32),
                        pltpu.VMEM((d, f), BF16), pltpu.VMEM((d, f), BF16),
                        pltpu.VMEM((f, d), BF16), pltpu.SemaphoreType.DMA])
    return pl.pallas_call(
        _expert_kernel,
        grid_spec=grid_spec,
        out_shape=jax.ShapeDtypeStruct(x_pad.shape, U32),
        compiler_params=_cparams(("arbitrary",)),
        name="expert",
    )(block_expert, n_used, next_expert, x_pad, wg, wu, wd, jnp.stack([bg, bu, bd], axis=1))


def _combine_kernel(dest_ref, y_hbm, x1_ref, wts_ref, g2_ref, nf_ref, o_ref, ybuf_a, ybuf_b,
                    sems, *, tm, final, n_tiles):
    i = pl.program_id(0)
    bufs = (ybuf_a, ybuf_b)

    def issue(slot):
        for tt in range(tm):
            for k in range(TOP_K):
                d = dest_ref[k, tt]
                pltpu.make_async_copy(_row(y_hbm, d), _row(bufs[slot].at[k], tt),
                                      sems.at[slot]).start(priority=k % 2)

    def wait(slot):
        for k in range(TOP_K):
            pltpu.make_async_copy(y_hbm.at[pl.ds(0, tm * ROW_SLABS)], bufs[slot].at[k],
                                  sems.at[slot]).wait()

    def finish(slot):
        w = wts_ref[...].T
        lo, hi = None, None
        for k in range(TOP_K):
            y_lo, y_hi = _unpack_halves(_load_rows(bufs[slot].at[k]))
            lo = w[:, k:k + 1] * y_lo + (0.0 if lo is None else lo)
            hi = w[:, k:k + 1] * y_hi + (0.0 if hi is None else hi)
        moe = jnp.concatenate([lo, hi], axis=1)
        x2 = x1_ref[...] + g2_ref[0] * moe
        o_ref[...] = _rms(x2) * nf_ref[...] if final else x2

    @pl.when(i == 0)
    def _():
        issue(0)

    for parity in range(2):
        @pl.when(jnp.logical_and(jnp.logical_and(i > 0, i < n_tiles), i % 2 == parity))
        def _(parity=parity):
            wait(1 - parity)
            issue(parity)
            finish(1 - parity)

    @pl.when(i == n_tiles)
    def _():
        wait((n_tiles - 1) % 2)
        finish((n_tiles - 1) % 2)


def _combine(dest, y_pad, x1, wts, g2, nf, seq, tm, final):
    t, d = x1.shape
    tpb = seq // tm
    n_tiles = t // tm
    lag = lambda i: jnp.maximum(i - 1, 0)
    return pl.pallas_call(
        functools.partial(_combine_kernel, tm=tm, final=final, n_tiles=n_tiles),
        grid=(n_tiles + 1,),
        in_specs=[pl.BlockSpec((TOP_K, tm), lambda i: (0, jnp.minimum(i, n_tiles - 1)),
                               memory_space=pltpu.SMEM),
                  pl.BlockSpec(memory_space=pl.ANY),
                  pl.BlockSpec((tm, d), lambda i: (lag(i), 0)),
                  pl.BlockSpec((TOP_K, tm), lambda i: (0, lag(i))),
                  pl.BlockSpec((1, 1, d), lambda i: (lag(i) // tpb, 0, 0)),
                  pl.BlockSpec((1, d), lambda i: (0, 0))],
        out_specs=pl.BlockSpec((tm, d), lambda i: (lag(i), 0)),
        out_shape=jax.ShapeDtypeStruct((t, d), F32),
        scratch_shapes=[pltpu.VMEM((TOP_K, tm * ROW_SLABS, LANES), y_pad.dtype),
                        pltpu.VMEM((TOP_K, tm * ROW_SLABS, LANES), y_pad.dtype),
                        pltpu.SemaphoreType.DMA((2,))],
        compiler_params=_cparams(("arbitrary",)),
        name="combine",
    )(dest, y_pad, x1, wts, g2, nf.reshape(1, d))


def _pick(n, cands):
    for c in cands:
        if n % c == 0:
            return c
    raise ValueError(f"no tile in {cands} divides {n}")


def kernel(x, c, rel_bias, w_ada, b_ada, norm_mix, w_in, w_gk_up, b_gk, gla_norm,
           w_proj_moba, w_proj_gla, w_out, norm_ffn, w_router, b_router,
           w_gate, b_gate, w_up, b_up, w_down, b_down, norm_final):
    bsz, seq, d = x.shape
    depth = w_ada.shape[0]
    assert d == D_MODEL and seq % MOBA_BLOCK == 0 and seq // MOBA_BLOCK <= MOBA_MAX_BLOCKS
    t = bsz * seq
    tm = _pick(seq, (512, 256))
    nchunk = _pick(seq // GLA_CHUNK, (16, 8, 4))
    x2d = x.reshape(t, d)
    bias = _bias_tiles(rel_bias)
    per_b = lambda v: v.reshape(bsz, 1, d)

    for l in range(depth):
        mod = _ada(c, w_ada[l], b_ada[l])
        sh1, sc1, g1, sh2, sc2, g2 = [per_b(m) for m in jnp.split(mod, 6, axis=-1)]
        qt, k_aug, vt, qkb, vb, gk, r_act, gates = _inproj(
            x2d, norm_mix[l], sc1, sh1, _regroup_w_in(w_in[l]), bsz, seq, tm)
        ya = _moba(qt, k_aug, vt, bias)
        wup = jnp.pad(w_gk_up[l], ((0, LANES - GLA_GATE_RANK), (0, 0)))
        yb = _gla(qkb, vb, gk, wup, b_gk[l].reshape(1, -1), r_act,
                  gla_norm[l].reshape(1, -1), bsz, seq, nchunk)
        x1, h2, idx, wts, rank, cnt = _merge(
            ya, yb, gates, x2d, w_proj_moba[l].astype(BF16), w_proj_gla[l].astype(BF16),
            w_out[l].astype(BF16), g1, norm_ffn[l], sc2, sh2,
            w_router[l].T, b_router[l].reshape(N_EXPERTS, 1), seq, tm)
        counts = cnt[:, 0].astype(I32)
        padded = (counts + MOE_ROWS - 1) // MOE_ROWS * MOE_ROWS
        pcum = jnp.cumsum(padded)
        pstart = (pcum - padded).astype(I32)
        n_blocks = (t * TOP_K + MOE_ROWS - 1) // MOE_ROWS + N_EXPERTS
        block_row0 = jnp.arange(n_blocks, dtype=I32) * MOE_ROWS
        block_expert = jnp.minimum(
            jnp.sum((pcum[None, :] <= block_row0[:, None]).astype(I32), axis=1),
            N_EXPERTS - 1).astype(I32)
        n_used = (pcum[-1:] // MOE_ROWS).astype(I32)
        e_ids = jnp.arange(N_EXPERTS, dtype=I32)
        later = (e_ids[None, :] > e_ids[:, None]) & (padded[None, :] > 0)
        next_expert = jnp.where(later.any(axis=1), jnp.argmax(later, axis=1), e_ids).astype(I32)
        dest = _dest(pstart, idx, rank, _pick(t, (8192, 4096, 2048, 1024, 512, 256)))
        tg = _pick(seq, (256,))
        zero_info = jnp.concatenate([jnp.maximum(pcum - MOE_ROWS, 0).astype(I32), n_used])
        x_pad = _scatter(dest, zero_info, h2, n_blocks * MOE_ROWS, tg)
        y_pad = _expert(block_expert, n_used, next_expert, x_pad, w_gate[l], b_gate[l],
                        w_up[l], b_up[l], w_down[l], b_down[l])
        x2d = _combine(dest, y_pad, x1, wts, g2, norm_final, seq, tg, l == depth - 1)
    return x2d.reshape(bsz, seq, d)
```
